```python
import math
import jax
import jax.numpy as jnp
from jax import lax
import numpy as np

D_MODEL = 1024
BATCH = 8
SEQ = 2048
DEPTH = 2

RW_HEADS = 8
RW_HEAD_DIM = 64
RW_WIDTH = RW_HEADS * RW_HEAD_DIM
RW_DECAY_RANK = 64
RW_ICLR_RANK = 64
RW_GATE_RANK = 160
RW_GN_EPS = 64e-5
RW_PROJ = 3 * RW_WIDTH + RW_DECAY_RANK + RW_ICLR_RANK + RW_GATE_RANK
SC_WIDTH = 512
SC_CONV = 3
SC_PROJ = 3 * SC_WIDTH
AB_PROJ = RW_PROJ + SC_PROJ
AB_OUT_IN = RW_WIDTH + SC_WIDTH
DIL_PATTERNS = ((128, 1), (512, 4), (2048, 16))
N_GROUPS = 3
DIL_HEADS = 8
DIL_HEAD_DIM = 64
DIL_WIDTH = DIL_HEADS * DIL_HEAD_DIM
DIL_PROJ = N_GROUPS * 3 * DIL_WIDTH
BLOCK = 128
N_BUCKETS = 32
MAX_DISTANCE = 2048
D_FF = 4 * D_MODEL
N_EVEN = (DEPTH + 1) // 2
N_ODD = DEPTH // 2
DEEPNORM_ALPHA = (2 * DEPTH) ** 0.25
DEEPNORM_BETA = (8 * DEPTH) ** -0.25
LN_EPS = 1e-5

kernel_name = 'hybrid_rwkv7_shortconv_dilated_attn_trunk'


def _layer_norm(x, g, b):
    xf = x.astype(jnp.float32)
    mu = jnp.mean(xf, -1, keepdims=True)
    var = jnp.mean(jnp.square(xf - mu), -1, keepdims=True)
    return ((xf - mu) * lax.rsqrt(var + LN_EPS) * g + b).astype(x.dtype)


def _token_shift(p):
    return jnp.pad(p, ((0, 0), (1, 0), (0, 0)))[:, :-1]


def _rwkv7_step(state, inp):
    r, w, k, v, a, b = inp
    sa = jnp.einsum('bhvk,bhk->bhv', state, a)
    state = state * w[:, :, None, :] + sa[..., None] * b[:, :, None, :] + v[..., None] * k[:, :, None, :]
    return state, jnp.einsum('bhvk,bhk->bhv', state, r)


def _causal_depthwise_conv(z, w):
    return lax.conv_general_dilated(z, w[:, None, :].astype(z.dtype), window_strides=(1,),
                                    padding=((SC_CONV - 1, 0),),
                                    dimension_numbers=('NWC', 'WIO', 'NWC'),
                                    feature_group_count=z.shape[-1])


def rwkv_shortconv_mixer(u, w_in, mu, w0, w_up, a0, a_up, g_up, k_k, k_a, r_k, lnx_g, lnx_b, conv_w, w_out):
    f32 = jnp.float32
    B, T, _ = u.shape
    p = u @ w_in
    pa, pb = p[..., :RW_PROJ], p[..., RW_PROJ:]
    pa = pa + mu * (_token_shift(pa) - pa)
    W = RW_WIDTH
    r, k, v, wd, ad, gd = jnp.split(
        pa, [W, 2 * W, 3 * W, 3 * W + RW_DECAY_RANK, 3 * W + RW_DECAY_RANK + RW_ICLR_RANK], axis=-1)
    logw = -jax.nn.softplus(-(w0 + jnp.tanh(wd) @ w_up).astype(f32)) - 0.5
    decay = jnp.exp(-jnp.exp(logw))
    iclr = jax.nn.sigmoid((a0 + ad @ a_up).astype(f32))
    gate = (jax.nn.sigmoid(gd) @ g_up).astype(f32)

    def heads(t):
        return t.astype(f32).reshape(B, T, RW_HEADS, RW_HEAD_DIM)

    kf = k.astype(f32)
    kk = heads(kf * k_k)
    kk = kk / jnp.maximum(jnp.linalg.norm(kk, axis=-1, keepdims=True), 1e-12)
    k_h = heads(kf * (1.0 + (iclr - 1.0) * k_a))
    r_h, v_h, w_h, a_h = heads(r), heads(v), heads(decay), heads(iclr)
    xs = tuple(jnp.moveaxis(t, 1, 0) for t in (r_h, w_h, k_h, v_h, -kk, kk * a_h))
    s0 = jnp.zeros((B, RW_HEADS, RW_HEAD_DIM, RW_HEAD_DIM), f32)
    _, y = lax.scan(_rwkv7_step, s0, xs)
    y = jnp.moveaxis(y, 0, 1)
    mean = jnp.mean(y, -1, keepdims=True)
    var = jnp.mean(jnp.square(y - mean), -1, keepdims=True)
    y = ((y - mean) * lax.rsqrt(var + RW_GN_EPS)).reshape(B, T, W) * lnx_g + lnx_b
    bonus = jnp.sum(r_h * k_h * r_k, -1, keepdims=True) * v_h
    y_a = ((y + bonus.reshape(B, T, W)) * gate).astype(u.dtype)
    h, b_gate, c_gate = jnp.split(pb, 3, axis=-1)
    y_b = b_gate * _causal_depthwise_conv(c_gate * h, conv_w)
    return jnp.concatenate([y_a, y_b], axis=-1) @ w_out


def _t5_bucket(dist):
    exact = N_BUCKETS // 2
    logd = jnp.log(jnp.maximum(dist, 1).astype(jnp.float32) / exact) / math.log(MAX_DISTANCE / exact)
    large = jnp.minimum(exact + (logd * (N_BUCKETS - exact)).astype(jnp.int32), N_BUCKETS - 1)
    return jnp.where(dist < exact, dist, large)


def _to_blocks(x, dil, nb):
    b, t = x.shape[:2]
    rest = x.shape[2:]
    L = t // dil
    x = jnp.moveaxis(x.reshape((b, L, dil) + rest), 2, 1)
    pad = [(0, 0)] * x.ndim
    pad[2] = (0, nb * BLOCK - L)
    return jnp.pad(x, pad).reshape((b, dil, nb, BLOCK) + rest)


def _from_blocks(x, L):
    b, dil, nb, blk = x.shape[:4]
    rest = x.shape[4:]
    x = x.reshape((b, dil, nb * blk) + rest)[:, :, :L]
    return jnp.moveaxis(x, 1, 2).reshape((b, L * dil) + rest)


def _with_prev_block(xb):
    prev = jnp.pad(xb, ((0, 0), (0, 0), (1, 0)) + ((0, 0),) * (xb.ndim - 3))[:, :, :-1]
    return jnp.concatenate([prev, xb], axis=3)


def _dilated_group_attention(q, k, v, bias, band, dil):
    t = q.shape[1]
    L = t // dil
    nb = -(-L // BLOCK)
    qb = _to_blocks(q, dil, nb)
    kc = _with_prev_block(_to_blocks(k, dil, nb))
    vc = _with_prev_block(_to_blocks(v, dil, nb))
    s = jnp.einsum('brnqhd,brnkhd->brnhqk', qb, kc).astype(jnp.float32) * (DIL_HEAD_DIM ** -0.5) + bias
    kidx = jnp.arange(2 * BLOCK)
    not_pad = (jnp.arange(nb) > 0)[:, None, None] | (kidx >= BLOCK)[None, None, :]
    valid = (band[None] & not_pad)[None, None, :, None]
    s = jnp.where(valid, s, -jnp.inf)
    m = jnp.max(s, -1, keepdims=True)
    e = jnp.exp(s - m)
    den = jnp.sum(e, -1, keepdims=True)
    o = jnp.einsum('brnhqk,brnkhd->brnqhd', (e / den).astype(v.dtype), vc)
    lse = jnp.moveaxis((m + jnp.log(den))[..., 0], 3, 4)
    return _from_blocks(o, L), _from_blocks(lse, L)


def dilated_attention_mixer(u, w_qkv, w_out, rel_bias):
    B, T, _ = u.shape
    p = (u @ w_qkv).reshape(B, T, N_GROUPS, 3, DIL_HEADS, DIL_HEAD_DIM)
    qi = jnp.arange(BLOCK)[:, None]
    ki = jnp.arange(2 * BLOCK)[None, :]
    rel = BLOCK + qi - ki
    outs, lses = [], []
    for g, (window, dil) in enumerate(DIL_PATTERNS):
        span = window // dil
        band = (rel >= 0) & (rel <= span)
        bucket = _t5_bucket(jnp.clip(rel, 0, span) * dil)
        bias = jnp.transpose(rel_bias[bucket][..., g * DIL_HEADS:(g + 1) * DIL_HEADS], (2, 0, 1)).astype(jnp.float32)
        o, lse = _dilated_group_attention(p[:, :, g, 0], p[:, :, g, 1], p[:, :, g, 2], bias, band, dil)
        outs.append(o.astype(jnp.float32))
        lses.append(lse)
    wts = jax.nn.softmax(jnp.stack(lses), axis=0)
    o = jnp.sum(wts[..., None] * jnp.stack(outs), axis=0)
    return o.reshape(B, T, DIL_WIDTH).astype(u.dtype) @ w_out


def sq_relu_mlp(u, w1, w2):
    return jnp.square(jax.nn.relu(u @ w1)) @ w2


def _fwd_setup_inputs(seed: int = 0) -> dict:
    key = jax.random.key(seed)
    ks = iter(jax.random.split(key, 32))
    D = D_MODEL

    def nrm(shape, s):
        return s * jax.random.normal(next(ks), shape, jnp.float32)

    return {
        'x': nrm((BATCH, SEQ, D), 1.0),
        'c': nrm((BATCH, D), 1.0),
        'ada_w': nrm((DEPTH, D, 6 * D), 0.1 * D ** -0.5),
        'ada_b': nrm((DEPTH, 6 * D), 0.01),
        'ln_g': 1.0 + nrm((DEPTH, 2, D), 0.02),
        'ln_b': nrm((DEPTH, 2, D), 0.02),
        'ab_w_in': nrm((N_EVEN, D, AB_PROJ), D ** -0.5),
        'rw_mu': jax.random.uniform(next(ks), (N_EVEN, RW_PROJ), jnp.float32),
        'rw_w0': -2.0 + nrm((N_EVEN, RW_WIDTH), 1.0),
        'rw_w_up': nrm((N_EVEN, RW_DECAY_RANK, RW_WIDTH), RW_DECAY_RANK ** -0.5),
        'rw_a0': nrm((N_EVEN, RW_WIDTH), 0.5),
        'rw_a_up': nrm((N_EVEN, RW_ICLR_RANK, RW_WIDTH), RW_ICLR_RANK ** -0.5),
        'rw_g_up': nrm((N_EVEN, RW_GATE_RANK, RW_WIDTH), RW_GATE_RANK ** -0.5),
        'rw_k_k': 0.85 + nrm((N_EVEN, RW_WIDTH), 0.05),
        'rw_k_a': 1.0 + nrm((N_EVEN, RW_WIDTH), 0.05),
        'rw_r_k': nrm((N_EVEN, RW_HEADS, RW_HEAD_DIM), 0.1),
        'rw_lnx_g': 1.0 + nrm((N_EVEN, RW_WIDTH), 0.02),
        'rw_lnx_b': nrm((N_EVEN, RW_WIDTH), 0.02),
        'sc_conv_w': nrm((N_EVEN, SC_CONV, SC_WIDTH), SC_CONV ** -0.5),
        'ab_w_out': nrm((N_EVEN, AB_OUT_IN, D), DEEPNORM_BETA * AB_OUT_IN ** -0.5),
        'dil_w_qkv': nrm((N_ODD, D, DIL_PROJ), D ** -0.5),
        'dil_w_out': nrm((N_ODD, DIL_WIDTH, D), DEEPNORM_BETA * DIL_WIDTH ** -0.5),
        'rel_bias': nrm((N_BUCKETS, N_GROUPS * DIL_HEADS), 0.5),
        'mlp_w1': nrm((DEPTH, D, D_FF), D ** -0.5),
        'mlp_w2': nrm((DEPTH, D_FF, D), DEEPNORM_BETA * D_FF ** -0.5),
    }


def _fwd_reference(x, c, ada_w, ada_b, ln_g, ln_b, ab_w_in, rw_mu, rw_w0, rw_w_up, rw_a0, rw_a_up, rw_g_up,
              rw_k_k, rw_k_a, rw_r_k, rw_lnx_g, rw_lnx_b, sc_conv_w, ab_w_out, dil_w_qkv, dil_w_out,
              rel_bias, mlp_w1, mlp_w2):
    cond = jax.nn.silu(c)
    for i in range(DEPTH):
        mod = (cond @ ada_w[i] + ada_b[i])[:, None, :]
        sh1, sc1, g1, sh2, sc2, g2 = jnp.split(mod, 6, axis=-1)
        j = i // 2
        u = x * (1 + sc1) + sh1
        if i % 2 == 0:
            y = rwkv_shortconv_mixer(u, ab_w_in[j], rw_mu[j], rw_w0[j], rw_w_up[j], rw_a0[j], rw_a_up[j],
                                     rw_g_up[j], rw_k_k[j], rw_k_a[j], rw_r_k[j], rw_lnx_g[j], rw_lnx_b[j],
                                     sc_conv_w[j], ab_w_out[j])
        else:
            y = dilated_attention_mixer(u, dil_w_qkv[j], dil_w_out[j], rel_bias)
        x = _layer_norm(DEEPNORM_ALPHA * x + (1 + g1) * y, ln_g[i, 0], ln_b[i, 0])
        u = x * (1 + sc2) + sh2
        y = sq_relu_mlp(u, mlp_w1[i], mlp_w2[i])
        x = _layer_norm(DEEPNORM_ALPHA * x + (1 + g2) * y, ln_g[i, 1], ln_b[i, 1])
    return x


import jax as _jax
import jax.numpy as _jnp

TWIN_FORMAT = 'train_step'
FWD_PARAMS = ['x', 'c', 'ada_w', 'ada_b', 'ln_g', 'ln_b', 'ab_w_in', 'rw_mu', 'rw_w0', 'rw_w_up', 'rw_a0', 'rw_a_up', 'rw_g_up', 'rw_k_k', 'rw_k_a', 'rw_r_k', 'rw_lnx_g', 'rw_lnx_b', 'sc_conv_w', 'ab_w_out', 'dil_w_qkv', 'dil_w_out', 'rel_bias', 'mlp_w1', 'mlp_w2']
TWIN_WEIGHTS = ['ada_w', 'ada_b', 'ln_g', 'ln_b', 'ab_w_in', 'rw_mu', 'rw_w0', 'rw_w_up', 'rw_a0', 'rw_a_up', 'rw_g_up', 'rw_k_k', 'rw_k_a', 'rw_r_k', 'rw_lnx_g', 'rw_lnx_b', 'sc_conv_w', 'ab_w_out', 'dil_w_qkv', 'dil_w_out', 'rel_bias', 'mlp_w1', 'mlp_w2']
TWIN_DIFF_INPUT = 'x'
TWIN_INPUTS = ['x', 'c', 'ada_w', 'ada_b', 'ln_g', 'ln_b', 'ab_w_in', 'rw_mu', 'rw_w0', 'rw_w_up', 'rw_a0', 'rw_a_up', 'rw_g_up', 'rw_k_k', 'rw_k_a', 'rw_r_k', 'rw_lnx_g', 'rw_lnx_b', 'sc_conv_w', 'ab_w_out', 'dil_w_qkv', 'dil_w_out', 'rel_bias', 'mlp_w1', 'mlp_w2', 'loss_target', 'm_ada_w', 'm_ada_b', 'm_ln_g', 'm_ln_b', 'm_ab_w_in', 'm_rw_mu', 'm_rw_w0', 'm_rw_w_up', 'm_rw_a0', 'm_rw_a_up', 'm_rw_g_up', 'm_rw_k_k', 'm_rw_k_a', 'm_rw_r_k', 'm_rw_lnx_g', 'm_rw_lnx_b', 'm_sc_conv_w', 'm_ab_w_out', 'm_dil_w_qkv', 'm_dil_w_out', 'm_rel_bias', 'm_mlp_w1', 'm_mlp_w2', 'v_ada_w', 'v_ada_b', 'v_ln_g', 'v_ln_b', 'v_ab_w_in', 'v_rw_mu', 'v_rw_w0', 'v_rw_w_up', 'v_rw_a0', 'v_rw_a_up', 'v_rw_g_up', 'v_rw_k_k', 'v_rw_k_a', 'v_rw_r_k', 'v_rw_lnx_g', 'v_rw_lnx_b', 'v_sc_conv_w', 'v_ab_w_out', 'v_dil_w_qkv', 'v_dil_w_out', 'v_rel_bias', 'v_mlp_w1', 'v_mlp_w2']
TWIN_OUTPUTS = ['loss', 'grad_x', 'grad_ada_w', 'grad_ada_b', 'grad_ln_g', 'grad_ln_b', 'grad_ab_w_in', 'grad_rw_mu', 'grad_rw_w0', 'grad_rw_w_up', 'grad_rw_a0', 'grad_rw_a_up', 'grad_rw_g_up', 'grad_rw_k_k', 'grad_rw_k_a', 'grad_rw_r_k', 'grad_rw_lnx_g', 'grad_rw_lnx_b', 'grad_sc_conv_w', 'grad_ab_w_out', 'grad_dil_w_qkv', 'grad_dil_w_out', 'grad_rel_bias', 'grad_mlp_w1', 'grad_mlp_w2', 'delta_ada_w', 'delta_ada_b', 'delta_ln_g', 'delta_ln_b', 'delta_ab_w_in', 'delta_rw_mu', 'delta_rw_w0', 'delta_rw_w_up', 'delta_rw_a0', 'delta_rw_a_up', 'delta_rw_g_up', 'delta_rw_k_k', 'delta_rw_k_a', 'delta_rw_r_k', 'delta_rw_lnx_g', 'delta_rw_lnx_b', 'delta_sc_conv_w', 'delta_ab_w_out', 'delta_dil_w_qkv', 'delta_dil_w_out', 'delta_rel_bias', 'delta_mlp_w1', 'delta_mlp_w2', 'new_m_ada_w', 'new_m_ada_b', 'new_m_ln_g', 'new_m_ln_b', 'new_m_ab_w_in', 'new_m_rw_mu', 'new_m_rw_w0', 'new_m_rw_w_up', 'new_m_rw_a0', 'new_m_rw_a_up', 'new_m_rw_g_up', 'new_m_rw_k_k', 'new_m_rw_k_a', 'new_m_rw_r_k', 'new_m_rw_lnx_g', 'new_m_rw_lnx_b', 'new_m_sc_conv_w', 'new_m_ab_w_out', 'new_m_dil_w_qkv', 'new_m_dil_w_out', 'new_m_rel_bias', 'new_m_mlp_w1', 'new_m_mlp_w2', 'new_v_ada_w', 'new_v_ada_b', 'new_v_ln_g', 'new_v_ln_b', 'new_v_ab_w_in', 'new_v_rw_mu', 'new_v_rw_w0', 'new_v_rw_w_up', 'new_v_rw_a0', 'new_v_rw_a_up', 'new_v_rw_g_up', 'new_v_rw_k_k', 'new_v_rw_k_a', 'new_v_rw_r_k', 'new_v_rw_lnx_g', 'new_v_rw_lnx_b', 'new_v_sc_conv_w', 'new_v_ab_w_out', 'new_v_dil_w_qkv', 'new_v_dil_w_out', 'new_v_rel_bias', 'new_v_mlp_w1', 'new_v_mlp_w2']
TWIN_LEAF_KINDS = {'loss': 'loss', 'grad_x': 'grad_x', 'grad_ada_w': 'grad_w', 'grad_ada_b': 'grad_w', 'grad_ln_g': 'grad_w', 'grad_ln_b': 'grad_w', 'grad_ab_w_in': 'grad_w', 'grad_rw_mu': 'grad_w', 'grad_rw_w0': 'grad_w', 'grad_rw_w_up': 'grad_w', 'grad_rw_a0': 'grad_w', 'grad_rw_a_up': 'grad_w', 'grad_rw_g_up': 'grad_w', 'grad_rw_k_k': 'grad_w', 'grad_rw_k_a': 'grad_w', 'grad_rw_r_k': 'grad_w', 'grad_rw_lnx_g': 'grad_w', 'grad_rw_lnx_b': 'grad_w', 'grad_sc_conv_w': 'grad_w', 'grad_ab_w_out': 'grad_w', 'grad_dil_w_qkv': 'grad_w', 'grad_dil_w_out': 'grad_w', 'grad_rel_bias': 'grad_w', 'grad_mlp_w1': 'grad_w', 'grad_mlp_w2': 'grad_w', 'delta_ada_w': 'delta_w', 'delta_ada_b': 'delta_w', 'delta_ln_g': 'delta_w', 'delta_ln_b': 'delta_w', 'delta_ab_w_in': 'delta_w', 'delta_rw_mu': 'delta_w', 'delta_rw_w0': 'delta_w', 'delta_rw_w_up': 'delta_w', 'delta_rw_a0': 'delta_w', 'delta_rw_a_up': 'delta_w', 'delta_rw_g_up': 'delta_w', 'delta_rw_k_k': 'delta_w', 'delta_rw_k_a': 'delta_w', 'delta_rw_r_k': 'delta_w', 'delta_rw_lnx_g': 'delta_w', 'delta_rw_lnx_b': 'delta_w', 'delta_sc_conv_w': 'delta_w', 'delta_ab_w_out': 'delta_w', 'delta_dil_w_qkv': 'delta_w', 'delta_dil_w_out': 'delta_w', 'delta_rel_bias': 'delta_w', 'delta_mlp_w1': 'delta_w', 'delta_mlp_w2': 'delta_w', 'new_m_ada_w': 'new_m', 'new_m_ada_b': 'new_m', 'new_m_ln_g': 'new_m', 'new_m_ln_b': 'new_m', 'new_m_ab_w_in': 'new_m', 'new_m_rw_mu': 'new_m', 'new_m_rw_w0': 'new_m', 'new_m_rw_w_up': 'new_m', 'new_m_rw_a0': 'new_m', 'new_m_rw_a_up': 'new_m', 'new_m_rw_g_up': 'new_m', 'new_m_rw_k_k': 'new_m', 'new_m_rw_k_a': 'new_m', 'new_m_rw_r_k': 'new_m', 'new_m_rw_lnx_g': 'new_m', 'new_m_rw_lnx_b': 'new_m', 'new_m_sc_conv_w': 'new_m', 'new_m_ab_w_out': 'new_m', 'new_m_dil_w_qkv': 'new_m', 'new_m_dil_w_out': 'new_m', 'new_m_rel_bias': 'new_m', 'new_m_mlp_w1': 'new_m', 'new_m_mlp_w2': 'new_m', 'new_v_ada_w': 'new_v', 'new_v_ada_b': 'new_v', 'new_v_ln_g': 'new_v', 'new_v_ln_b': 'new_v', 'new_v_ab_w_in': 'new_v', 'new_v_rw_mu': 'new_v', 'new_v_rw_w0': 'new_v', 'new_v_rw_w_up': 'new_v', 'new_v_rw_a0': 'new_v', 'new_v_rw_a_up': 'new_v', 'new_v_rw_g_up': 'new_v', 'new_v_rw_k_k': 'new_v', 'new_v_rw_k_a': 'new_v', 'new_v_rw_r_k': 'new_v', 'new_v_rw_lnx_g': 'new_v', 'new_v_rw_lnx_b': 'new_v', 'new_v_sc_conv_w': 'new_v', 'new_v_ab_w_out': 'new_v', 'new_v_dil_w_qkv': 'new_v', 'new_v_dil_w_out': 'new_v', 'new_v_rel_bias': 'new_v', 'new_v_mlp_w1': 'new_v', 'new_v_mlp_w2': 'new_v'}


def _forward(args):
    return _fwd_reference(*[args[k] for k in FWD_PARAMS])


def _output_shape():
    out = _jax.eval_shape(lambda: _forward(_fwd_setup_inputs(0)))
    return out.shape, out.dtype

N_MICROBATCH = 1
ADAM_LR = 0.001
ADAM_B1 = 0.9
ADAM_B2 = 0.999
ADAM_EPS = 1e-08
ADAM_WD = 0.01
ADAM_STEP = 10
PER_EXAMPLE_BATCH_AXIS = {'x': 0, 'c': 0, 'loss_target': 0}
SHARED_INPUTS = []
_WEIGHT_DTYPES = {'ada_w': _jnp.float32, 'ada_b': _jnp.float32, 'ln_g': _jnp.float32, 'ln_b': _jnp.float32, 'ab_w_in': _jnp.float32, 'rw_mu': _jnp.float32, 'rw_w0': _jnp.float32, 'rw_w_up': _jnp.float32, 'rw_a0': _jnp.float32, 'rw_a_up': _jnp.float32, 'rw_g_up': _jnp.float32, 'rw_k_k': _jnp.float32, 'rw_k_a': _jnp.float32, 'rw_r_k': _jnp.float32, 'rw_lnx_g': _jnp.float32, 'rw_lnx_b': _jnp.float32, 'sc_conv_w': _jnp.float32, 'ab_w_out': _jnp.float32, 'dil_w_qkv': _jnp.float32, 'dil_w_out': _jnp.float32, 'rel_bias': _jnp.float32, 'mlp_w1': _jnp.float32, 'mlp_w2': _jnp.float32}
MOMENT_SCALE = {'ada_w': 3.445281e-02, 'ada_b': 6.751240e-02, 'ln_g': 8.047459e+00, 'ln_b': 1.876782e+00, 'ab_w_in': 3.601119e-02, 'rw_mu': 4.286913e-02, 'rw_w0': 1.272660e-02, 'rw_w_up': 2.386006e-03, 'rw_a0': 1.025153e-02, 'rw_a_up': 8.477887e-03, 'rw_g_up': 2.554177e-02, 'rw_k_k': 1.103782e-02, 'rw_k_a': 2.918505e-02, 'rw_r_k': 5.847848e-02, 'rw_lnx_g': 2.747140e-02, 'rw_lnx_b': 4.901343e-02, 'sc_conv_w': 4.574466e-02, 'ab_w_out': 7.330597e-02, 'dil_w_qkv': 7.743605e-03, 'dil_w_out': 2.420302e-02, 'rel_bias': 9.645160e-03, 'mlp_w1': 2.962334e-02, 'mlp_w2': 1.328453e-01}


def _to_microbatches(a, axis):
    t = _jnp.moveaxis(a, axis, 0)
    t = t.reshape((N_MICROBATCH, t.shape[0] // N_MICROBATCH) + t.shape[1:])
    return _jnp.moveaxis(t, 1, axis + 1)


def setup_inputs(seed: int = 0) -> dict:
    inp = _fwd_setup_inputs(seed)
    key = _jax.random.fold_in(_jax.random.key(seed), 7919)
    shape, _ = _output_shape()
    out = dict(inp)
    out["loss_target"] = _jax.random.normal(_jax.random.fold_in(key, 0), shape, _jnp.float32)
    for i, name in enumerate(TWIN_WEIGHTS):
        w = inp[name].astype(_jnp.float32)
        if MOMENT_SCALE is None:
            s = _jnp.sqrt(_jnp.mean(_jnp.square(w)) + 1e-30)
        else:
            s = MOMENT_SCALE[name]
        km, kv = _jax.random.split(_jax.random.fold_in(key, i + 1))
        out[name] = w
        out["m_" + name] = s * _jax.random.normal(km, w.shape, _jnp.float32)
        out["v_" + name] = (s * s) * _jax.random.uniform(kv, w.shape, _jnp.float32, 0.5, 1.5)
    if N_MICROBATCH > 1:
        for name, axis in PER_EXAMPLE_BATCH_AXIS.items():
            out[name] = _to_microbatches(out[name], axis)
    return {'x': out['x'], 'c': out['c'], 'ada_w': out['ada_w'], 'ada_b': out['ada_b'], 'ln_g': out['ln_g'], 'ln_b': out['ln_b'], 'ab_w_in': out['ab_w_in'], 'rw_mu': out['rw_mu'], 'rw_w0': out['rw_w0'], 'rw_w_up': out['rw_w_up'], 'rw_a0': out['rw_a0'], 'rw_a_up': out['rw_a_up'], 'rw_g_up': out['rw_g_up'], 'rw_k_k': out['rw_k_k'], 'rw_k_a': out['rw_k_a'], 'rw_r_k': out['rw_r_k'], 'rw_lnx_g': out['rw_lnx_g'], 'rw_lnx_b': out['rw_lnx_b'], 'sc_conv_w': out['sc_conv_w'], 'ab_w_out': out['ab_w_out'], 'dil_w_qkv': out['dil_w_qkv'], 'dil_w_out': out['dil_w_out'], 'rel_bias': out['rel_bias'], 'mlp_w1': out['mlp_w1'], 'mlp_w2': out['mlp_w2'], 'loss_target': out['loss_target'], 'm_ada_w': out['m_ada_w'], 'm_ada_b': out['m_ada_b'], 'm_ln_g': out['m_ln_g'], 'm_ln_b': out['m_ln_b'], 'm_ab_w_in': out['m_ab_w_in'], 'm_rw_mu': out['m_rw_mu'], 'm_rw_w0': out['m_rw_w0'], 'm_rw_w_up': out['m_rw_w_up'], 'm_rw_a0': out['m_rw_a0'], 'm_rw_a_up': out['m_rw_a_up'], 'm_rw_g_up': out['m_rw_g_up'], 'm_rw_k_k': out['m_rw_k_k'], 'm_rw_k_a': out['m_rw_k_a'], 'm_rw_r_k': out['m_rw_r_k'], 'm_rw_lnx_g': out['m_rw_lnx_g'], 'm_rw_lnx_b': out['m_rw_lnx_b'], 'm_sc_conv_w': out['m_sc_conv_w'], 'm_ab_w_out': out['m_ab_w_out'], 'm_dil_w_qkv': out['m_dil_w_qkv'], 'm_dil_w_out': out['m_dil_w_out'], 'm_rel_bias': out['m_rel_bias'], 'm_mlp_w1': out['m_mlp_w1'], 'm_mlp_w2': out['m_mlp_w2'], 'v_ada_w': out['v_ada_w'], 'v_ada_b': out['v_ada_b'], 'v_ln_g': out['v_ln_g'], 'v_ln_b': out['v_ln_b'], 'v_ab_w_in': out['v_ab_w_in'], 'v_rw_mu': out['v_rw_mu'], 'v_rw_w0': out['v_rw_w0'], 'v_rw_w_up': out['v_rw_w_up'], 'v_rw_a0': out['v_rw_a0'], 'v_rw_a_up': out['v_rw_a_up'], 'v_rw_g_up': out['v_rw_g_up'], 'v_rw_k_k': out['v_rw_k_k'], 'v_rw_k_a': out['v_rw_k_a'], 'v_rw_r_k': out['v_rw_r_k'], 'v_rw_lnx_g': out['v_rw_lnx_g'], 'v_rw_lnx_b': out['v_rw_lnx_b'], 'v_sc_conv_w': out['v_sc_conv_w'], 'v_ab_w_out': out['v_ab_w_out'], 'v_dil_w_qkv': out['v_dil_w_qkv'], 'v_dil_w_out': out['v_dil_w_out'], 'v_rel_bias': out['v_rel_bias'], 'v_mlp_w1': out['v_mlp_w1'], 'v_mlp_w2': out['v_mlp_w2']}


def _loss(weights, diff, rest, loss_target):
    with _jax.named_scope("forward"):
        args = {**rest, TWIN_DIFF_INPUT: diff, **{k: w.astype(_WEIGHT_DTYPES[k]) for k, w in weights.items()}}
        y = _forward(args)
    with _jax.named_scope("loss_head"):
        err = _jnp.square(y.astype(_jnp.float32) - loss_target)
        return 0.5 * _jnp.sum(_jnp.mean(err, axis=-1)) if err.ndim else 0.5 * err


def _adamw(w, g, m, v):
    m = ADAM_B1 * m + (1.0 - ADAM_B1) * g
    v = ADAM_B2 * v + (1.0 - ADAM_B2) * _jnp.square(g)
    m_hat = m / (1.0 - ADAM_B1 ** ADAM_STEP)
    v_hat = v / (1.0 - ADAM_B2 ** ADAM_STEP)
    delta = -ADAM_LR * (m_hat / (_jnp.sqrt(v_hat) + ADAM_EPS) + ADAM_WD * w)
    return delta, m, v


def reference(x, c, ada_w, ada_b, ln_g, ln_b, ab_w_in, rw_mu, rw_w0, rw_w_up, rw_a0, rw_a_up, rw_g_up, rw_k_k, rw_k_a, rw_r_k, rw_lnx_g, rw_lnx_b, sc_conv_w, ab_w_out, dil_w_qkv, dil_w_out, rel_bias, mlp_w1, mlp_w2, loss_target, m_ada_w, m_ada_b, m_ln_g, m_ln_b, m_ab_w_in, m_rw_mu, m_rw_w0, m_rw_w_up, m_rw_a0, m_rw_a_up, m_rw_g_up, m_rw_k_k, m_rw_k_a, m_rw_r_k, m_rw_lnx_g, m_rw_lnx_b, m_sc_conv_w, m_ab_w_out, m_dil_w_qkv, m_dil_w_out, m_rel_bias, m_mlp_w1, m_mlp_w2, v_ada_w, v_ada_b, v_ln_g, v_ln_b, v_ab_w_in, v_rw_mu, v_rw_w0, v_rw_w_up, v_rw_a0, v_rw_a_up, v_rw_g_up, v_rw_k_k, v_rw_k_a, v_rw_r_k, v_rw_lnx_g, v_rw_lnx_b, v_sc_conv_w, v_ab_w_out, v_dil_w_qkv, v_dil_w_out, v_rel_bias, v_mlp_w1, v_mlp_w2):
    given = dict(x=x, c=c, ada_w=ada_w, ada_b=ada_b, ln_g=ln_g, ln_b=ln_b, ab_w_in=ab_w_in, rw_mu=rw_mu, rw_w0=rw_w0, rw_w_up=rw_w_up, rw_a0=rw_a0, rw_a_up=rw_a_up, rw_g_up=rw_g_up, rw_k_k=rw_k_k, rw_k_a=rw_k_a, rw_r_k=rw_r_k, rw_lnx_g=rw_lnx_g, rw_lnx_b=rw_lnx_b, sc_conv_w=sc_conv_w, ab_w_out=ab_w_out, dil_w_qkv=dil_w_qkv, dil_w_out=dil_w_out, rel_bias=rel_bias, mlp_w1=mlp_w1, mlp_w2=mlp_w2, loss_target=loss_target, m_ada_w=m_ada_w, m_ada_b=m_ada_b, m_ln_g=m_ln_g, m_ln_b=m_ln_b, m_ab_w_in=m_ab_w_in, m_rw_mu=m_rw_mu, m_rw_w0=m_rw_w0, m_rw_w_up=m_rw_w_up, m_rw_a0=m_rw_a0, m_rw_a_up=m_rw_a_up, m_rw_g_up=m_rw_g_up, m_rw_k_k=m_rw_k_k, m_rw_k_a=m_rw_k_a, m_rw_r_k=m_rw_r_k, m_rw_lnx_g=m_rw_lnx_g, m_rw_lnx_b=m_rw_lnx_b, m_sc_conv_w=m_sc_conv_w, m_ab_w_out=m_ab_w_out, m_dil_w_qkv=m_dil_w_qkv, m_dil_w_out=m_dil_w_out, m_rel_bias=m_rel_bias, m_mlp_w1=m_mlp_w1, m_mlp_w2=m_mlp_w2, v_ada_w=v_ada_w, v_ada_b=v_ada_b, v_ln_g=v_ln_g, v_ln_b=v_ln_b, v_ab_w_in=v_ab_w_in, v_rw_mu=v_rw_mu, v_rw_w0=v_rw_w0, v_rw_w_up=v_rw_w_up, v_rw_a0=v_rw_a0, v_rw_a_up=v_rw_a_up, v_rw_g_up=v_rw_g_up, v_rw_k_k=v_rw_k_k, v_rw_k_a=v_rw_k_a, v_rw_r_k=v_rw_r_k, v_rw_lnx_g=v_rw_lnx_g, v_rw_lnx_b=v_rw_lnx_b, v_sc_conv_w=v_sc_conv_w, v_ab_w_out=v_ab_w_out, v_dil_w_qkv=v_dil_w_qkv, v_dil_w_out=v_dil_w_out, v_rel_bias=v_rel_bias, v_mlp_w1=v_mlp_w1, v_mlp_w2=v_mlp_w2)
    weights = {n: given[n] for n in TWIN_WEIGHTS}
    shared = {n: given[n] for n in SHARED_INPUTS}
    per_example = {n: given[n] for n in ['x', 'c']}
    grad_fn = _jax.value_and_grad(_loss, argnums=(0, 1))

    def one_microbatch(ex, loss_target):
        ex = dict(ex)
        diff = ex.pop(TWIN_DIFF_INPUT)
        return grad_fn(weights, diff, {**shared, **ex}, loss_target)

    if N_MICROBATCH == 1:
        loss, (grad_w, grad_x) = one_microbatch(per_example, given["loss_target"])
    else:
        def body(carry, xs):
            loss_sum, grad_sum = carry
            l_k, (gw_k, gx_k) = one_microbatch(xs[0], xs[1])
            with _jax.named_scope("update"):
                return (loss_sum + l_k, _jax.tree.map(_jnp.add, grad_sum, gw_k)), gx_k

        init = (_jnp.zeros((), _jnp.float32), _jax.tree.map(_jnp.zeros_like, weights))
        (loss, grad_w), grad_x = _jax.lax.scan(body, init, (per_example, given["loss_target"]))
    with _jax.named_scope("update"):
        delta_w, new_m, new_v = {}, {}, {}
        for n in TWIN_WEIGHTS:
            delta_w[n], new_m[n], new_v[n] = _adamw(weights[n], grad_w[n], given["m_" + n], given["v_" + n])
    return (loss, grad_x, *[grad_w[n] for n in TWIN_WEIGHTS], *[delta_w[n] for n in TWIN_WEIGHTS],
            *[new_m[n] for n in TWIN_WEIGHTS], *[new_v[n] for n in TWIN_WEIGHTS])
```

```python
import functools
import math

import numpy as np
import jax
import jax.numpy as jnp
from jax import lax
from jax.experimental import pallas as pl
from jax.experimental.pallas import tpu as pltpu

F32 = jnp.float32
BF16 = jnp.bfloat16
MESH = pl.DeviceIdType.MESH

D_MODEL = 1024
DEPTH = 2
RW_WIDTH = 512
HEAD_DIM = 64
N_HEADS = 8
RW_DECAY_RANK = 64
RW_ICLR_RANK = 64
RW_GATE_RANK = 160
RW_GN_EPS = 64e-5
RW_PROJ = 3 * RW_WIDTH + RW_DECAY_RANK + RW_ICLR_RANK + RW_GATE_RANK
SC_WIDTH = 512
AB_PROJ = RW_PROJ + 3 * SC_WIDTH
DIL_PATTERNS = ((128, 1), (512, 4), (2048, 16))
N_GROUPS = 3
DIL_WIDTH = 512
DIL_PROJ = N_GROUPS * 3 * DIL_WIDTH
BLOCK = 128
N_BUCKETS = 32
MAX_DISTANCE = 2048
D_FF = 4 * D_MODEL
ALPHA = (2 * DEPTH) ** 0.25
LN_EPS = 1e-5
ADAM_LR = 0.001
ADAM_B1 = 0.9
ADAM_B2 = 0.999
ADAM_EPS = 1e-08
ADAM_WD = 0.01
ADAM_STEP = 10

N_CHIPS = 4
N_DEV = 8
LANES = 128
SUBLANES = 8
ROW_W = 1024
SCAN_CHUNK = 16
VMEM_LIMIT = 48 * 1024 * 1024
NEG_BIG = -1e30


def _pick(n, cands):
    for c in cands:
        if n % c == 0:
            return c
    return n


def _cparams(sem=None, vmem=None):
    return pltpu.CompilerParams(dimension_semantics=sem, vmem_limit_bytes=vmem)


_DOT_DIMS = {
    "nn": (((1,), (0,)), ((), ())),
    "nt": (((1,), (1,)), ((), ())),
    "tn": (((0,), (0,)), ((), ())),
}


def _mm(a, b, mode, name):
    if mode == "nn":
        (m, k), (_, n) = a.shape, b.shape
    elif mode == "nt":
        (m, k), (n, _) = a.shape, b.shape
    else:
        (k, m), (_, n) = a.shape, b.shape
    tm = _pick(m, (512, 256, 128))
    tn = _pick(n, (512, 384, 256, 128))
    tk = _pick(k, (512, 256, 128))
    nk = k // tk
    if mode == "tn":
        a_spec = pl.BlockSpec((tk, tm), lambda i, j, kk: (kk, i))
    else:
        a_spec = pl.BlockSpec((tm, tk), lambda i, j, kk: (i, kk))
    if mode == "nt":
        b_spec = pl.BlockSpec((tn, tk), lambda i, j, kk: (j, kk))
    else:
        b_spec = pl.BlockSpec((tk, tn), lambda i, j, kk: (kk, j))
    dims = _DOT_DIMS[mode]

    def body(a_ref, b_ref, o_ref, acc_ref):
        kk = pl.program_id(2)

        @pl.when(kk == 0)
        def _():
            acc_ref[...] = jnp.zeros_like(acc_ref)

        acc_ref[...] += lax.dot_general(a_ref[...].astype(BF16), b_ref[...].astype(BF16), dims,
                                        preferred_element_type=F32)

        @pl.when(kk == nk - 1)
        def _():
            o_ref[...] = acc_ref[...]

    return pl.pallas_call(
        body, name=name, grid=(m // tm, n // tn, nk),
        in_specs=[a_spec, b_spec],
        out_specs=pl.BlockSpec((tm, tn), lambda i, j, kk: (i, j)),
        out_shape=jax.ShapeDtypeStruct((m, n), F32),
        scratch_shapes=[pltpu.VMEM((tm, tn), F32)],
        compiler_params=_cparams(("parallel", "parallel", "arbitrary"), VMEM_LIMIT),
    )(a, b)


def _linear(x, w, name):
    @jax.custom_vjp
    def op(x, w):
        return _mm(x, w, "nn", name + "_fwd")

    def fwd(x, w):
        return _mm(x, w, "nn", name + "_fwd"), (x, w)

    def bwd(res, dy):
        x, w = res
        return _mm(dy, w, "nt", name + "_dx"), _mm(x, dy, "tn", name + "_dw")

    op.defvjp(fwd, bwd)
    return op(x, w)


def _const_map(ndim):
    return lambda *g: (0,) * ndim


def _first_step(n_grid):
    return functools.reduce(jnp.logical_and, [pl.program_id(d) == 0 for d in range(n_grid)])


def _fused(fn, ins, outs, grid, name):
    arrays = [i[0] for i in ins]
    n_in, n_out, n_grid = len(ins), len(outs), len(grid)
    in_specs = [pl.BlockSpec(bs, im) for (_, bs, im, _) in ins]
    out_specs = [pl.BlockSpec(bs, im) for (_, bs, im, _) in outs]
    out_shapes = [jax.ShapeDtypeStruct(s, F32) for (s, _, _, _) in outs]
    sem = ("arbitrary",) * n_grid

    def fwd_call(*xs):
        def body(*refs):
            vals = [r[...] for r in refs[:n_in]]
            ys = fn(*vals)
            first = _first_step(n_grid)
            for o_ref, y, (_, _, _, kind) in zip(refs[n_in:], ys, outs):
                if kind == "t":
                    o_ref[...] = y
                else:
                    @pl.when(first)
                    def _(o_ref=o_ref):
                        o_ref[...] = jnp.zeros_like(o_ref)

                    o_ref[...] += y

        return pl.pallas_call(
            body, name=name + "_fwd", grid=grid, in_specs=in_specs, out_specs=out_specs,
            out_shape=out_shapes, compiler_params=_cparams(sem, VMEM_LIMIT))(*xs)

    def bwd_call(xs, dys):
        d_specs = [pl.BlockSpec(bs, im) for (_, bs, im, _) in outs]
        g_specs = [pl.BlockSpec(bs, im) for (_, bs, im, _) in ins]
        g_shapes = [jax.ShapeDtypeStruct(a.shape, F32) for a in arrays]

        def body(*refs):
            vals = [r[...] for r in refs[:n_in]]
            dvals = tuple(r[...] for r in refs[n_in:n_in + n_out])
            _, vjp = jax.vjp(lambda *v: tuple(fn(*v)), *vals)
            gs = vjp(dvals)
            first = _first_step(n_grid)
            for g_ref, g, (_, _, _, kind) in zip(refs[n_in + n_out:], gs, ins):
                if kind == "t":
                    g_ref[...] = g
                else:
                    @pl.when(first)
                    def _(g_ref=g_ref):
                        g_ref[...] = jnp.zeros_like(g_ref)

                    g_ref[...] += g

        return pl.pallas_call(
            body, name=name + "_bwd", grid=grid, in_specs=in_specs + d_specs, out_specs=g_specs,
            out_shape=g_shapes, compiler_params=_cparams(sem, VMEM_LIMIT))(*xs, *dys)

    @jax.custom_vjp
    def op(*xs):
        return tuple(fwd_call(*xs))

    def op_fwd(*xs):
        return tuple(fwd_call(*xs)), xs

    def op_bwd(xs, dys):
        return tuple(bwd_call(xs, dys))

    op.defvjp(op_fwd, op_bwd)
    return op(*arrays)


def _tiled(a, tile, cols=None, col_block=0):
    cols = a.shape[1] if cols is None else cols
    return (a, (tile, cols), lambda i, cb=col_block: (i, cb), "t")


def _shared(a):
    return (a, a.shape, _const_map(a.ndim), "b")


def _tiled_out(rows, cols, tile):
    return ((rows, cols), (tile, cols), lambda i: (i, 0), "t")


@jax.custom_vjp
def _bdot(x, w):
    return jnp.dot(x.astype(BF16), w.astype(BF16), preferred_element_type=F32)


def _bdot_fwd(x, w):
    return _bdot(x, w), (x, w)


def _bdot_bwd(res, dy):
    x, w = res
    dyb = dy.astype(BF16)
    dx = lax.dot_general(dyb, w.astype(BF16), _DOT_DIMS["nt"], preferred_element_type=F32)
    dw = lax.dot_general(x.astype(BF16), dyb, _DOT_DIMS["tn"], preferred_element_type=F32)
    return dx, dw


_bdot.defvjp(_bdot_fwd, _bdot_bwd)


def _head_sum(x):
    n = x.shape[-1]
    hi = lax.broadcasted_iota(jnp.int32, (n, n), 0) // HEAD_DIM
    hj = lax.broadcasted_iota(jnp.int32, (n, n), 1) // HEAD_DIM
    e = (hi == hj).astype(F32)
    return jnp.dot(x, e, precision=lax.Precision.HIGHEST, preferred_element_type=F32)


def _softplus(x):
    return jnp.maximum(x, 0.0) + jnp.log1p(jnp.exp(-jnp.abs(x)))


def _layer_norm_rows(z, g, b):
    mu = jnp.mean(z, axis=-1, keepdims=True)
    zc = z - mu
    var = jnp.mean(zc * zc, axis=-1, keepdims=True)
    return zc * lax.rsqrt(var + LN_EPS) * g + b


N_PAIRS = N_HEADS // 2


def _scan_consts():
    k = lax.broadcasted_iota(jnp.int32, (HEAD_DIM, LANES), 0)
    j = lax.broadcasted_iota(jnp.int32, (HEAD_DIM, LANES), 1)
    diag = (j % HEAD_DIM) == k
    jj = lax.broadcasted_iota(jnp.int32, (LANES, LANES), 0) // HEAD_DIM
    ll = lax.broadcasted_iota(jnp.int32, (LANES, LANES), 1) // HEAD_DIM
    same_head = (jj == ll).astype(BF16)
    lane_lo = j < HEAD_DIM
    return diag, same_head, lane_lo


def _row_to_col(row, diag_bf, same_head):
    rb = jnp.broadcast_to(row, (16, LANES))
    hi = rb.astype(BF16)
    r1 = rb - hi.astype(F32)
    mid = r1.astype(BF16)
    lo = (r1 - mid.astype(F32)).astype(BF16)

    def spread(p):
        return jnp.dot(jnp.concatenate([p] * 4, axis=0) * diag_bf, same_head, preferred_element_type=F32)

    return (spread(hi) + spread(mid)) + spread(lo)


def _unrolled(n, body, carry):
    for i in range(n):
        carry = body(i, carry)
    return carry


def _fill_cols(srcs, col_ref, diag_bf, same_head, n_steps):
    def conv(t, carry):
        for vi, src in enumerate(srcs):
            for hp in range(N_PAIRS):
                row = src[pl.ds(t, 1), hp * LANES:(hp + 1) * LANES]
                col_ref[vi, hp, t] = _row_to_col(row, diag_bf, same_head)
        return carry

    _unrolled(n_steps, conv, 0)


def _scan_fwd_call(r, w, k, v, a, b):
    t_len = r.shape[0]
    ch = SCAN_CHUNK
    n_ch = t_len // ch

    def body(r_ref, w_ref, k_ref, v_ref, a_ref, b_ref, y_ref, ck_ref, st_ref, col_ref):
        c = pl.program_id(0)

        @pl.when(c == 0)
        def _():
            st_ref[...] = jnp.zeros_like(st_ref)

        ck_ref[0] = st_ref[...]
        diag, same_head, _ = _scan_consts()
        diag_bf = diag.astype(BF16)
        _fill_cols((w_ref, a_ref, b_ref, k_ref, r_ref), col_ref, diag_bf, same_head, ch)

        def step(t, states):
            new = []
            for hp in range(N_PAIRS):
                lanes = slice(hp * LANES, (hp + 1) * LANES)
                s = states[hp]
                sa = jnp.sum(s * col_ref[1, hp, t], axis=0, keepdims=True)
                s = s * col_ref[0, hp, t] + col_ref[2, hp, t] * sa + col_ref[3, hp, t] * v_ref[pl.ds(t, 1), lanes]
                y_ref[pl.ds(t, 1), lanes] = jnp.sum(s * col_ref[4, hp, t], axis=0, keepdims=True)
                new.append(s)
            return tuple(new)

        states = _unrolled(ch, step, tuple(st_ref[hp] for hp in range(N_PAIRS)))
        for hp in range(N_PAIRS):
            st_ref[hp] = states[hp]

    row_spec = pl.BlockSpec((ch, RW_WIDTH), lambda c: (c, 0))
    return pl.pallas_call(
        body, name="rwkv_scan_fwd", grid=(n_ch,),
        in_specs=[row_spec] * 6,
        out_specs=[row_spec, pl.BlockSpec((1, N_PAIRS, HEAD_DIM, LANES), lambda c: (c, 0, 0, 0))],
        out_shape=[jax.ShapeDtypeStruct((t_len, RW_WIDTH), F32),
                   jax.ShapeDtypeStruct((n_ch, N_PAIRS, HEAD_DIM, LANES), F32)],
        scratch_shapes=[pltpu.VMEM((N_PAIRS, HEAD_DIM, LANES), F32),
                        pltpu.VMEM((5, N_PAIRS, ch, HEAD_DIM, LANES), F32)],
        compiler_params=_cparams(("arbitrary",), VMEM_LIMIT),
    )(r, w, k, v, a, b)


def _scan_bwd_call(r, w, k, v, a, b, ck, dy):
    t_len = r.shape[0]
    ch = SCAN_CHUNK
    n_ch = t_len // ch

    def body(r_ref, w_ref, k_ref, v_ref, a_ref, b_ref, ck_ref, dy_ref,
             dr_ref, dw_ref, dk_ref, dv_ref, da_ref, db_ref,
             ds_ref, col_ref, sp_ref, sa_ref):
        c = pl.program_id(0)

        @pl.when(c == 0)
        def _():
            ds_ref[...] = jnp.zeros_like(ds_ref)

        diag, same_head, lane_lo = _scan_consts()
        diag_bf = diag.astype(BF16)
        diag_f = diag.astype(F32)
        _fill_cols((w_ref, a_ref, b_ref, k_ref, r_ref), col_ref, diag_bf, same_head, ch)

        def replay(t, states):
            new = []
            for hp in range(N_PAIRS):
                lanes = slice(hp * LANES, (hp + 1) * LANES)
                s = states[hp]
                sp_ref[t, hp] = s
                sa = jnp.sum(s * col_ref[1, hp, t], axis=0, keepdims=True)
                sa_ref[pl.ds(t, 1), lanes] = sa
                new.append(s * col_ref[0, hp, t] + col_ref[2, hp, t] * sa
                           + col_ref[3, hp, t] * v_ref[pl.ds(t, 1), lanes])
            return tuple(new)

        _unrolled(ch, replay, tuple(ck_ref[0, hp] for hp in range(N_PAIRS)))

        def key_row(p):
            lo = jnp.sum(jnp.where(lane_lo, p, 0.0), axis=1, keepdims=True)
            hi = jnp.sum(jnp.where(lane_lo, 0.0, p), axis=1, keepdims=True)
            return jnp.sum(jnp.where(lane_lo, lo, hi) * diag_f, axis=0, keepdims=True)

        def back(i, grads):
            t = ch - 1 - i
            new = []
            for hp in range(N_PAIRS):
                lanes = slice(hp * LANES, (hp + 1) * LANES)
                wc, ac, bc, kc, rc = (col_ref[vi, hp, t] for vi in range(5))
                sp = sp_ref[t, hp]
                sa = sa_ref[pl.ds(t, 1), lanes]
                vrow = v_ref[pl.ds(t, 1), lanes]
                dyrow = dy_ref[pl.ds(t, 1), lanes]
                st = sp * wc + bc * sa + kc * vrow
                g = grads[hp] + rc * dyrow
                dsa = jnp.sum(g * bc, axis=0, keepdims=True)
                dv_ref[pl.ds(t, 1), lanes] = jnp.sum(g * kc, axis=0, keepdims=True)
                dr_ref[pl.ds(t, 1), lanes] = key_row(st * dyrow)
                dk_ref[pl.ds(t, 1), lanes] = key_row(g * vrow)
                db_ref[pl.ds(t, 1), lanes] = key_row(g * sa)
                dw_ref[pl.ds(t, 1), lanes] = key_row(g * sp)
                da_ref[pl.ds(t, 1), lanes] = key_row(sp * dsa)
                new.append(g * wc + ac * dsa)
            return tuple(new)

        grads = _unrolled(ch, back, tuple(ds_ref[hp] for hp in range(N_PAIRS)))
        for hp in range(N_PAIRS):
            ds_ref[hp] = grads[hp]

    row_spec = pl.BlockSpec((ch, RW_WIDTH), lambda c: (n_ch - 1 - c, 0))
    out_sds = jax.ShapeDtypeStruct((t_len, RW_WIDTH), F32)
    return pl.pallas_call(
        body, name="rwkv_scan_bwd", grid=(n_ch,),
        in_specs=[row_spec] * 6 + [pl.BlockSpec((1, N_PAIRS, HEAD_DIM, LANES), lambda c: (n_ch - 1 - c, 0, 0, 0)),
                                   row_spec],
        out_specs=[row_spec] * 6,
        out_shape=[out_sds] * 6,
        scratch_shapes=[pltpu.VMEM((N_PAIRS, HEAD_DIM, LANES), F32),
                        pltpu.VMEM((5, N_PAIRS, ch, HEAD_DIM, LANES), F32),
                        pltpu.VMEM((ch, N_PAIRS, HEAD_DIM, LANES), F32),
                        pltpu.VMEM((ch, RW_WIDTH), F32)],
        compiler_params=_cparams(("arbitrary",), VMEM_LIMIT),
    )(r, w, k, v, a, b, ck, dy)


@jax.custom_vjp
def _rwkv_scan(r, w, k, v, a, b):
    return _scan_fwd_call(r, w, k, v, a, b)[0]


def _rwkv_scan_fwd(r, w, k, v, a, b):
    y, ck = _scan_fwd_call(r, w, k, v, a, b)
    return y, (r, w, k, v, a, b, ck)


def _rwkv_scan_bwd(res, dy):
    return tuple(_scan_bwd_call(*res, dy))


_rwkv_scan.defvjp(_rwkv_scan_fwd, _rwkv_scan_bwd)


ATT_SCALE = HEAD_DIM ** -0.5
COLS_PER_POS = DIL_PROJ // DIL_WIDTH


def _att_masks():
    qi = lax.broadcasted_iota(jnp.int32, (BLOCK, BLOCK), 0)
    ki = lax.broadcasted_iota(jnp.int32, (BLOCK, BLOCK), 1)
    lane = lax.broadcasted_iota(jnp.int32, (1, LANES), 1)
    return ki <= qi, ki >= qi, lane


def _att_fwd_call(p, bias, g, dil):
    t_len = p.shape[0]
    l_len = t_len // dil
    nb = l_len // BLOCK
    pv = p.reshape(l_len, dil * DIL_PROJ)
    base = g * 3

    def body(q_ref, kc_ref, kp_ref, vc_ref, vp_ref, bias_ref, o_ref, lse_ref):
        n = pl.program_id(1)
        cur_ok, prev_band, lane = _att_masks()
        prev_ok = jnp.logical_and(prev_band, n > 0)
        for hp in range(N_PAIRS):
            lanes = slice(hp * LANES, (hp + 1) * LANES)
            q2 = q_ref[:, lanes].astype(BF16)
            kc = kc_ref[:, lanes].astype(BF16)
            kp = kp_ref[:, lanes].astype(BF16)
            vc = vc_ref[:, lanes].astype(BF16)
            vp = vp_ref[:, lanes].astype(BF16)
            o2 = jnp.zeros((BLOCK, LANES), F32)
            for hh in range(2):
                h = 2 * hp + hh
                mine = (lane // HEAD_DIM) == hh
                qm = jnp.where(mine, q2, jnp.zeros_like(q2))
                s_c = lax.dot_general(qm, kc, _DOT_DIMS["nt"], preferred_element_type=F32) * ATT_SCALE
                s_p = lax.dot_general(qm, kp, _DOT_DIMS["nt"], preferred_element_type=F32) * ATT_SCALE
                s_c = jnp.where(cur_ok, s_c + bias_ref[h, :, BLOCK:], NEG_BIG)
                s_p = jnp.where(prev_ok, s_p + bias_ref[h, :, :BLOCK], NEG_BIG)
                m = jnp.maximum(jnp.max(s_c, axis=-1, keepdims=True), jnp.max(s_p, axis=-1, keepdims=True))
                e_c = jnp.exp(s_c - m)
                e_p = jnp.exp(s_p - m)
                den = jnp.sum(e_c, axis=-1, keepdims=True) + jnp.sum(e_p, axis=-1, keepdims=True)
                o_h = (jnp.dot((e_c / den).astype(BF16), vc, preferred_element_type=F32)
                       + jnp.dot((e_p / den).astype(BF16), vp, preferred_element_type=F32))
                o2 = o2 + jnp.where(mine, o_h, 0.0)
                lse_ref[h] = jnp.broadcast_to(m + jnp.log(den), (BLOCK, LANES))
            o_ref[:, lanes] = o2

    def col(j):
        return lambda r, n: (n, r * COLS_PER_POS + base + j)

    def col_prev(j):
        return lambda r, n: (jnp.maximum(n - 1, 0), r * COLS_PER_POS + base + j)

    blk = (BLOCK, DIL_WIDTH)
    o, lse = pl.pallas_call(
        body, name=f"dil_att_fwd_g{g}", grid=(dil, nb),
        in_specs=[pl.BlockSpec(blk, col(0)), pl.BlockSpec(blk, col(1)), pl.BlockSpec(blk, col_prev(1)),
                  pl.BlockSpec(blk, col(2)), pl.BlockSpec(blk, col_prev(2)),
                  pl.BlockSpec(bias.shape, _const_map(3))],
        out_specs=[pl.BlockSpec(blk, lambda r, n: (n, r)),
                   pl.BlockSpec((N_HEADS, BLOCK, LANES), lambda r, n: (0, n, r))],
        out_shape=[jax.ShapeDtypeStruct((l_len, dil * DIL_WIDTH), F32),
                   jax.ShapeDtypeStruct((N_HEADS, l_len, dil * LANES), F32)],
        compiler_params=_cparams(("arbitrary", "arbitrary"), VMEM_LIMIT),
    )(pv, pv, pv, pv, pv, bias)
    return o.reshape(t_len, DIL_WIDTH), lse.reshape(N_HEADS, t_len, LANES)


def _att_bwd_call(p, bias, o, lse, do, dlse, g, dil):
    t_len = p.shape[0]
    l_len = t_len // dil
    nb = l_len // BLOCK
    pv = p.reshape(l_len, dil * DIL_PROJ)
    ov = o.reshape(l_len, dil * DIL_WIDTH)
    dov = do.reshape(l_len, dil * DIL_WIDTH)
    lsev = lse.reshape(N_HEADS, l_len, dil * LANES)
    dlsev = dlse.reshape(N_HEADS, l_len, dil * LANES)
    base = g * 3

    def body(q_ref, qn_ref, k_ref, v_ref, do_ref, don_ref, o_ref, on_ref, lse_ref, lsen_ref, dl_ref, dln_ref,
             bias_ref, dq_ref, dk_ref, dv_ref, dbias_ref, carry_ref):
        r = pl.program_id(0)
        n = pl.program_id(1)
        cur_ok, prev_band, lane = _att_masks()
        has_next = n + 1 < nb

        @pl.when(jnp.logical_and(r == 0, n == 0))
        def _():
            dbias_ref[...] = jnp.zeros_like(dbias_ref)

        @pl.when(n == 0)
        def _():
            carry_ref[...] = jnp.zeros_like(carry_ref)

        for hp in range(N_PAIRS):
            lanes = slice(hp * LANES, (hp + 1) * LANES)
            k2 = k_ref[:, lanes].astype(BF16)
            v2 = v_ref[:, lanes].astype(BF16)
            dk2 = jnp.zeros((BLOCK, LANES), F32)
            dv2 = jnp.zeros((BLOCK, LANES), F32)
            dq_cur = carry_ref[:, lanes]
            dq_next = jnp.zeros((BLOCK, LANES), F32)
            for hh in range(2):
                h = 2 * hp + hh
                mine = (lane // HEAD_DIM) == hh
                tiles = (
                    (q_ref, do_ref, o_ref, lse_ref, dl_ref, cur_ok, slice(BLOCK, 2 * BLOCK), None),
                    (qn_ref, don_ref, on_ref, lsen_ref, dln_ref, prev_band, slice(0, BLOCK), has_next),
                )
                for ti, (qr, dor, orf, lr, dlr, ok, bcols, gate) in enumerate(tiles):
                    q2 = qr[:, lanes].astype(BF16)
                    qm = jnp.where(mine, q2, jnp.zeros_like(q2))
                    do_f = jnp.where(mine, dor[:, lanes], 0.0)
                    dom = do_f.astype(BF16)
                    s = lax.dot_general(qm, k2, _DOT_DIMS["nt"], preferred_element_type=F32) * ATT_SCALE
                    s = s + bias_ref[h, :, bcols]
                    if gate is not None:
                        ok = jnp.logical_and(ok, gate)
                    pr = jnp.where(ok, jnp.exp(jnp.minimum(s - lr[h], 0.0)), 0.0)
                    dp = lax.dot_general(dom, v2, _DOT_DIMS["nt"], preferred_element_type=F32)
                    delta = jnp.sum(do_f * orf[:, lanes], axis=-1, keepdims=True)
                    dl = jnp.sum(dlr[h], axis=-1, keepdims=True)
                    ds = pr * (dp - delta + dl)
                    dsb = ds.astype(BF16)
                    dq_h = jnp.where(mine, jnp.dot(dsb, k2, preferred_element_type=F32), 0.0) * ATT_SCALE
                    if ti == 0:
                        dq_cur = dq_cur + dq_h
                    else:
                        dq_next = dq_next + dq_h
                    dk2 = dk2 + lax.dot_general(dsb, qm, _DOT_DIMS["tn"], preferred_element_type=F32) * ATT_SCALE
                    dv2 = dv2 + lax.dot_general(pr.astype(BF16), dom, _DOT_DIMS["tn"], preferred_element_type=F32)
                    dbias_ref[h, :, bcols] += ds
            dq_ref[:, lanes] = dq_cur
            carry_ref[:, lanes] = dq_next
            dk_ref[:, lanes] = dk2
            dv_ref[:, lanes] = dv2

    def nxt(n):
        return jnp.minimum(n + 1, nb - 1)

    blk = (BLOCK, DIL_WIDTH)
    hblk = (N_HEADS, BLOCK, LANES)
    qcol = lambda j: (lambda r, n: (n, r * COLS_PER_POS + base + j))
    q_next = lambda r, n: (nxt(n), r * COLS_PER_POS + base)
    rown = lambda r, n: (n, r)
    rown_next = lambda r, n: (nxt(n), r)
    hrow = lambda r, n: (0, n, r)
    hrow_next = lambda r, n: (0, nxt(n), r)
    sds = jax.ShapeDtypeStruct((l_len, dil * DIL_WIDTH), F32)
    dq, dk, dv, dbias = pl.pallas_call(
        body, name=f"dil_att_bwd_g{g}", grid=(dil, nb),
        in_specs=[pl.BlockSpec(blk, qcol(0)), pl.BlockSpec(blk, q_next),
                  pl.BlockSpec(blk, qcol(1)), pl.BlockSpec(blk, qcol(2)),
                  pl.BlockSpec(blk, rown), pl.BlockSpec(blk, rown_next),
                  pl.BlockSpec(blk, rown), pl.BlockSpec(blk, rown_next),
                  pl.BlockSpec(hblk, hrow), pl.BlockSpec(hblk, hrow_next),
                  pl.BlockSpec(hblk, hrow), pl.BlockSpec(hblk, hrow_next),
                  pl.BlockSpec(bias.shape, _const_map(3))],
        out_specs=[pl.BlockSpec(blk, rown)] * 3 + [pl.BlockSpec(bias.shape, _const_map(3))],
        out_shape=[sds, sds, sds, jax.ShapeDtypeStruct(bias.shape, F32)],
        scratch_shapes=[pltpu.VMEM((BLOCK, DIL_WIDTH), F32)],
        compiler_params=_cparams(("arbitrary", "arbitrary"), VMEM_LIMIT),
    )(pv, pv, pv, pv, dov, dov, ov, ov, lsev, lsev, dlsev, dlsev, bias)
    shp = (t_len, DIL_WIDTH)
    return dq.reshape(shp), dk.reshape(shp), dv.reshape(shp), dbias


def _dilated_attention(p, bias, g, dil):
    @jax.custom_vjp
    def op(p, bias):
        return _att_fwd_call(p, bias, g, dil)

    def fwd(p, bias):
        o, lse = _att_fwd_call(p, bias, g, dil)
        return (o, lse), (p, bias, o, lse)

    def bwd(res, cts):
        p, bias, o, lse = res
        do, dlse = cts
        dq, dk, dv, dbias = _att_bwd_call(p, bias, o, lse, do, dlse, g, dil)
        zero = jnp.zeros((p.shape[0], DIL_WIDTH), F32)
        parts = [zero] * (3 * N_GROUPS)
        parts[3 * g:3 * g + 3] = [dq, dk, dv]
        return jnp.concatenate(parts, axis=1), dbias

    op.defvjp(fwd, bwd)
    return op(p, bias)


def _me():
    return lax.axis_index("x"), lax.axis_index("y"), lax.axis_index("c")


def _flip(me, f):
    return tuple((1 - m) if b else m for m, b in zip(me, f))


def _chip_of(d):
    return 2 * d[0] + d[1]


def _dev_of(d):
    return 4 * d[0] + 2 * d[1] + d[2]


CHIP_FLIPS = ((1, 0, 0), (0, 1, 0), (1, 1, 0))
ALL_FLIPS = tuple((a, b, c) for a in (0, 1) for b in (0, 1) for c in (0, 1) if a or b or c)
CORE_FLIPS = ((0, 0, 1),)


def _exchange(src, n_slots, flips, send_slot, recv_slot, name):
    _, rows, cols = src.shape
    n = len(flips)

    def body(src_ref, dst_ref, send_sems, recv_sems, local_sem):
        me = _me()
        local = pltpu.make_async_copy(src_ref.at[send_slot(me, me)], dst_ref.at[recv_slot(me)], local_sem)
        local.start()
        copies = []
        for kk, f in enumerate(flips):
            peer = _flip(me, f)
            cp = pltpu.make_async_remote_copy(
                src_ref=src_ref.at[send_slot(me, peer)], dst_ref=dst_ref.at[recv_slot(me)],
                send_sem=send_sems.at[kk], recv_sem=recv_sems.at[kk],
                device_id=peer, device_id_type=MESH)
            cp.start()
            copies.append(cp)
        for cp in copies:
            cp.wait()
        local.wait()

    return pl.pallas_call(
        body, name=name,
        out_shape=jax.ShapeDtypeStruct((n_slots, rows, cols), src.dtype),
        in_specs=[pl.BlockSpec(memory_space=pl.ANY)],
        out_specs=pl.BlockSpec(memory_space=pl.ANY),
        scratch_shapes=[pltpu.SemaphoreType.DMA((n,)), pltpu.SemaphoreType.DMA((n,)), pltpu.SemaphoreType.DMA],
    )(src)


def _chip_all_gather(src, name):
    return _exchange(src[None], N_CHIPS, CHIP_FLIPS, lambda me, peer: 0, _chip_of, name)


def _dev_all_gather(src, name):
    return _exchange(src[None], N_DEV, ALL_FLIPS, lambda me, peer: 0, _dev_of, name)


def _chip_scatter(src, name):
    return _exchange(src, N_CHIPS, CHIP_FLIPS, lambda me, peer: _chip_of(peer), _chip_of, name)


def _core_swap(src, name):
    return _exchange(src[None], 2, CORE_FLIPS, lambda me, peer: 0, lambda me: me[2], name)


def _sum_slots(x, name):
    s, rows, cols = x.shape
    tile = _pick(rows, (512, 256, 128, 64, 32, 16, 8))

    def body(x_ref, o_ref):
        acc = x_ref[0]
        for i in range(1, s):
            acc = acc + x_ref[i]
        o_ref[...] = acc

    return pl.pallas_call(
        body, name=name, grid=(rows // tile,),
        in_specs=[pl.BlockSpec((s, tile, cols), lambda i: (0, i, 0))],
        out_specs=pl.BlockSpec((tile, cols), lambda i: (i, 0)),
        out_shape=jax.ShapeDtypeStruct((rows, cols), F32),
        compiler_params=_cparams(("parallel",), VMEM_LIMIT),
    )(x)


def _adamw(w, g, m, v, name):
    rows, cols = w.shape
    tile = rows
    if rows * cols * 4 > 2 * 1024 * 1024:
        tile = _pick(rows, (256, 128, 64, 32, 16, 8))
    c1 = 1.0 / (1.0 - ADAM_B1 ** ADAM_STEP)
    c2 = 1.0 / (1.0 - ADAM_B2 ** ADAM_STEP)

    def body(w_ref, g_ref, m_ref, v_ref, d_ref, nm_ref, nv_ref):
        gv = g_ref[...]
        nm = ADAM_B1 * m_ref[...] + (1.0 - ADAM_B1) * gv
        nv = ADAM_B2 * v_ref[...] + (1.0 - ADAM_B2) * (gv * gv)
        m_hat = nm * c1
        v_hat = nv * c2
        d_ref[...] = -ADAM_LR * (m_hat / (jnp.sqrt(v_hat) + ADAM_EPS) + ADAM_WD * w_ref[...])
        nm_ref[...] = nm
        nv_ref[...] = nv

    spec = pl.BlockSpec((tile, cols), lambda i: (i, 0))
    sds = jax.ShapeDtypeStruct((rows, cols), F32)
    return pl.pallas_call(
        body, name=name, grid=(rows // tile,),
        in_specs=[spec] * 4, out_specs=[spec] * 3, out_shape=[sds] * 3,
        compiler_params=_cparams(("parallel",), VMEM_LIMIT),
    )(w, g, m, v)


def _ada_fwd(c_all, ada_w, ada_b_cols):
    n_col = ada_w.shape[2]

    def body(c_ref, w_ref, b_ref, o_ref):
        cv = c_ref[...]
        cond = (cv * jax.nn.sigmoid(cv)).astype(BF16)
        o_ref[0] = jnp.dot(cond, w_ref[0].astype(BF16), preferred_element_type=F32) + b_ref[0]

    return pl.pallas_call(
        body, name="ada_fwd", grid=(DEPTH,),
        in_specs=[pl.BlockSpec(c_all.shape, lambda i: (0, 0)),
                  pl.BlockSpec((1, D_MODEL, n_col), lambda i: (i, 0, 0)),
                  pl.BlockSpec((1, 1, n_col), lambda i: (i, 0, 0))],
        out_specs=pl.BlockSpec((1, N_DEV, n_col), lambda i: (i, 0, 0)),
        out_shape=jax.ShapeDtypeStruct((DEPTH, N_DEV, n_col), F32),
        compiler_params=_cparams(("parallel",), VMEM_LIMIT),
    )(c_all, ada_w, ada_b_cols)


def _ada_grad(c_all_t, dmod_cols):
    n_col = dmod_cols.shape[2]

    def body(c_ref, d_ref, o_ref):
        cv = c_ref[...]
        cond = cv * jax.nn.sigmoid(cv)
        o_ref[0] = jnp.dot(cond, d_ref[0], precision=lax.Precision.HIGHEST, preferred_element_type=F32)

    return pl.pallas_call(
        body, name="ada_grad", grid=(DEPTH,),
        in_specs=[pl.BlockSpec(c_all_t.shape, lambda i: (0, 0)),
                  pl.BlockSpec((1, LANES, n_col), lambda i: (i, 0, 0))],
        out_specs=pl.BlockSpec((1, D_MODEL, n_col), lambda i: (i, 0, 0)),
        out_shape=jax.ShapeDtypeStruct((DEPTH, D_MODEL, n_col), F32),
        compiler_params=_cparams(("parallel",), VMEM_LIMIT),
    )(c_all_t, dmod_cols)


ROW_TILE = 256


def _shift_rows(a, n=1):
    return jnp.pad(a, ((n, 0), (0, 0)))[:-n]


def _modulate(x, sc, sh, name):
    def fn(x, sc, sh):
        return (x * (1.0 + sc) + sh,)

    t = x.shape[0]
    return _fused(fn, [_tiled(x, ROW_TILE), _shared(sc), _shared(sh)],
                  [_tiled_out(t, D_MODEL, ROW_TILE)], (t // ROW_TILE,), name)[0]


def _resid_ln_mod(x, y, gate, ln_g, ln_b, sc, sh, name):
    def fn(x, y, gate, ln_g, ln_b, sc, sh):
        x1 = _layer_norm_rows(ALPHA * x + (1.0 + gate) * y, ln_g, ln_b)
        return x1, x1 * (1.0 + sc) + sh

    t = x.shape[0]
    return _fused(fn, [_tiled(x, ROW_TILE), _tiled(y, ROW_TILE)] + [_shared(a) for a in (gate, ln_g, ln_b, sc, sh)],
                  [_tiled_out(t, D_MODEL, ROW_TILE)] * 2, (t // ROW_TILE,), name)


def _resid_ln_loss(x, y, gate, ln_g, ln_b, target, name):
    def fn(x, y, gate, ln_g, ln_b, target):
        x1 = _layer_norm_rows(ALPHA * x + (1.0 + gate) * y, ln_g, ln_b)
        err = jnp.square(x1 - target)
        per_row = jnp.mean(err, axis=-1, keepdims=True)
        return (0.5 * jnp.sum(per_row, axis=0, keepdims=True),)

    t = x.shape[0]
    return _fused(fn, [_tiled(x, ROW_TILE), _tiled(y, ROW_TILE)] + [_shared(a) for a in (gate, ln_g, ln_b)]
                  + [_tiled(target, ROW_TILE)],
                  [((1, 1), (1, 1), _const_map(2), "a")], (t // ROW_TILE,), name)[0]


def _sq_relu(h, name):
    def fn(h):
        return (jnp.square(jnp.maximum(h, 0.0)),)

    t, f = h.shape
    return _fused(fn, [_tiled(h, ROW_TILE)], [_tiled_out(t, f, ROW_TILE)], (t // ROW_TILE,), name)[0]


def _mlp(u, w1, w2, name):
    h = _linear(u, w1, name + "_w1")
    return _linear(_sq_relu(h, name + "_act"), w2, name + "_w2")


AB_PIECES = (("r", 0, 512, 512), ("k", 512, 512, 512), ("v", 1024, 512, 512),
             ("wd", 1536, 64, 128), ("ad", 1600, 64, 128), ("gd", 1664, 160, 256),
             ("h", 1824, 512, 512), ("bg", 2336, 512, 512), ("cg", 2848, 512, 512))
AB_PAD_COLS = sum(p[3] for p in AB_PIECES)


def _regroup_cols(w):
    parts = []
    for _, start, width, padded in AB_PIECES:
        piece = w[..., start:start + width]
        if padded != width:
            piece = jnp.pad(piece, [(0, 0)] * (w.ndim - 1) + [(0, padded - width)])
        parts.append(piece)
    return jnp.concatenate(parts, axis=-1)


def _pad_rows(w, rows):
    return jnp.pad(w, ((0, rows - w.shape[0]), (0, 0)))


def _rwkv_shortconv(u, wts):
    t = u.shape[0]
    p = _linear(u, _regroup_cols(wts["ab_w_in"][0]), "ab_in")
    mu = _regroup_cols(jnp.pad(wts["rw_mu"], ((0, 0), (0, AB_PROJ - RW_PROJ))))
    w_up = _pad_rows(wts["rw_w_up"][0], 128)
    a_up = _pad_rows(wts["rw_a_up"][0], 128)
    g_up = _pad_rows(wts["rw_g_up"][0], 256)

    def pre(rp, rs, kp, ks, vp, vs, wdp, wds, adp, ads, gdp, gds, h, cg,
            mu_r, mu_k, mu_v, mu_w, mu_a, mu_g, w0, w_up, a0, a_up, g_up, k_k, k_a):
        def mix(pv, sv, m):
            return pv + m * (sv - pv)

        r, k, v = mix(rp, rs, mu_r), mix(kp, ks, mu_k), mix(vp, vs, mu_v)
        wd, ad, gd = mix(wdp, wds, mu_w), mix(adp, ads, mu_a), mix(gdp, gds, mu_g)
        logw = -_softplus(-(w0 + _bdot(jnp.tanh(wd), w_up))) - 0.5
        decay = jnp.exp(-jnp.exp(logw))
        iclr = jax.nn.sigmoid(a0 + _bdot(ad, a_up))
        gate = _bdot(jax.nn.sigmoid(gd), g_up)
        kk = k * k_k
        kk = kk / jnp.maximum(jnp.sqrt(_head_sum(kk * kk)), 1e-12)
        k_h = k * (1.0 + (iclr - 1.0) * k_a)
        return r, decay, k_h, v, -kk, kk * iclr, gate, cg * h

    tile = ROW_TILE
    names = [q[0] for q in AB_PIECES]
    cuts = list(np.cumsum([q[3] for q in AB_PIECES])[:-1])
    pp = dict(zip(names, jnp.split(p, cuts, axis=1)))
    mp = dict(zip(names, jnp.split(mu, cuts, axis=1)))

    ins = []
    for name in ("r", "k", "v", "wd", "ad", "gd"):
        ins += [_tiled(pp[name], tile), _tiled(_shift_rows(pp[name]), tile)]
    ins += [_tiled(pp["h"], tile), _tiled(pp["cg"], tile)]
    ins += [_shared(mp[name]) for name in ("r", "k", "v", "wd", "ad", "gd")]
    ins += [_shared(a) for a in (wts["rw_w0"], w_up, wts["rw_a0"], a_up, g_up, wts["rw_k_k"], wts["rw_k_a"])]
    outs = [_tiled_out(t, RW_WIDTH, tile)] * 8
    r, decay, k_h, v, a, b, gate, z = _fused(pre, ins, outs, (t // tile,), "rwkv_pre")

    y = _rwkv_scan(r, decay, k_h, v, a, b)

    conv_w = wts["sc_conv_w"][0]
    r_k = wts["rw_r_k"].reshape(1, RW_WIDTH)

    def post(y, r, k_h, v, gate, bg, z, z1, z2, lnx_g, lnx_b, r_k, c0, c1, c2):
        mean = _head_sum(y) * (1.0 / HEAD_DIM)
        yc = y - mean
        var = _head_sum(yc * yc) * (1.0 / HEAD_DIM)
        yn = yc * lax.rsqrt(var + RW_GN_EPS) * lnx_g + lnx_b
        bonus = _head_sum(r * k_h * r_k) * v
        return (yn + bonus) * gate, bg * (c0 * z2 + c1 * z1 + c2 * z)

    ins = [_tiled(a_, tile) for a_ in (y, r, k_h, v, gate, pp["bg"])]
    ins += [_tiled(a_, tile) for a_ in (z, _shift_rows(z, 1), _shift_rows(z, 2))]
    ins += [_shared(a_) for a_ in (wts["rw_lnx_g"], wts["rw_lnx_b"], r_k, conv_w[0:1], conv_w[1:2], conv_w[2:3])]
    y_a, y_b = _fused(post, ins, [_tiled_out(t, RW_WIDTH, tile)] * 2, (t // tile,), "rwkv_post")
    return _linear(jnp.concatenate([y_a, y_b], axis=1), wts["ab_w_out"][0], "ab_out")


def _t5_bucket_np(dist):
    exact = N_BUCKETS // 2
    logd = np.log(np.maximum(dist, 1).astype(np.float32) / exact) / math.log(MAX_DISTANCE / exact)
    large = np.minimum(exact + (logd * (N_BUCKETS - exact)).astype(np.int32), N_BUCKETS - 1)
    return np.where(dist < exact, dist, large)


def _merge_groups(os_, lses, name):
    t = os_[0].shape[0]
    tile = ROW_TILE

    def fn(o0, o1, o2, l0, l1, l2):
        lane = lax.broadcasted_iota(jnp.int32, (1, LANES), 1)
        lo = lane < HEAD_DIM
        ls = [jnp.where(lo, l[0], l[1]) for l in (l0, l1, l2)]
        m = jnp.maximum(jnp.maximum(ls[0], ls[1]), ls[2])
        es = [jnp.exp(l - m) for l in ls]
        den = es[0] + es[1] + es[2]
        return ((es[0] * o0 + es[1] * o1 + es[2] * o2) / den,)

    ins = [(o, (tile, LANES), lambda i, hp: (i, hp), "t") for o in os_]
    ins += [(l, (2, tile, LANES), lambda i, hp: (hp, i, 0), "t") for l in lses]
    outs = [((t, DIL_WIDTH), (tile, LANES), lambda i, hp: (i, hp), "t")]
    return _fused(fn, ins, outs, (t // tile, N_PAIRS), name)[0]


def _dilated_mixer(u, wts):
    p = _linear(u, wts["dil_w_qkv"][0], "dil_qkv")
    qi = np.arange(BLOCK)[:, None]
    ki = np.arange(2 * BLOCK)[None, :]
    rel = BLOCK + qi - ki
    os_, lses = [], []
    for g, (window, dil) in enumerate(DIL_PATTERNS):
        span = window // dil
        bucket = _t5_bucket_np(np.clip(rel, 0, span) * dil)
        bias = jnp.transpose(wts["rel_bias"][bucket][..., g * N_HEADS:(g + 1) * N_HEADS], (2, 0, 1))
        o, lse = _dilated_attention(p, bias, g, dil)
        os_.append(o)
        lses.append(lse)
    o = _merge_groups(os_, lses, "dil_merge")
    return _linear(o, wts["dil_w_out"][0], "dil_out")


def _forward_local(x, mods, wts, target):
    u = _modulate(x, mods[0, 1], mods[0, 0], "mod_in")
    for i in range(DEPTH):
        sh2, sc2, g1, g2 = mods[i, 3], mods[i, 4], mods[i, 2], mods[i, 5]
        y = _rwkv_shortconv(u, wts) if i % 2 == 0 else _dilated_mixer(u, wts)
        x, u = _resid_ln_mod(x, y, g1, wts["ln_g"][i, 0:1], wts["ln_b"][i, 0:1], sc2, sh2, f"ln_mix{i}")
        y = _mlp(u, wts["mlp_w1"][i], wts["mlp_w2"][i], f"mlp{i}")
        if i + 1 < DEPTH:
            x, u = _resid_ln_mod(x, y, g2, wts["ln_g"][i, 1:2], wts["ln_b"][i, 1:2],
                                 mods[i + 1, 1], mods[i + 1, 0], f"ln_mlp{i}")
        else:
            return _resid_ln_loss(x, y, g2, wts["ln_g"][i, 1:2], wts["ln_b"][i, 1:2], target, "ln_loss")


SHARDED = {"ab_w_in": 2, "ab_w_out": 1, "dil_w_qkv": 2, "dil_w_out": 2, "mlp_w1": 2, "mlp_w2": 1,
           "ln_g": 2, "ln_b": 2, "rw_w_up": 2, "rw_a_up": 2, "rw_g_up": 2, "sc_conv_w": 2}
BIG = ("ab_w_in", "ab_w_out", "dil_w_qkv", "dil_w_out", "mlp_w1", "mlp_w2")
SMALL_SHARDED = ("ln_g", "ln_b", "rw_w_up", "rw_a_up", "rw_g_up", "sc_conv_w")
REPLICATED = ("ada_b", "rw_mu", "rw_w0", "rw_a0", "rw_k_k", "rw_k_a", "rw_r_k", "rw_lnx_g", "rw_lnx_b", "rel_bias")
WEIGHT_ORDER = ("ada_w", "ada_b", "ln_g", "ln_b", "ab_w_in", "rw_mu", "rw_w0", "rw_w_up", "rw_a0", "rw_a_up",
                "rw_g_up", "rw_k_k", "rw_k_a", "rw_r_k", "rw_lnx_g", "rw_lnx_b", "sc_conv_w", "ab_w_out",
                "dil_w_qkv", "dil_w_out", "rel_bias", "mlp_w1", "mlp_w2")


PACK_ROWS = 16


def _rows_of(n_elems):
    return -(-n_elems // (ROW_W * PACK_ROWS)) * PACK_ROWS


def _to_rows(a):
    flat = a.reshape(-1)
    rows = _rows_of(flat.shape[0])
    if rows * ROW_W != flat.shape[0]:
        flat = jnp.pad(flat, (0, rows * ROW_W - flat.shape[0]))
    return flat.reshape(rows, ROW_W)


def _from_rows(rows, shape):
    n = int(np.prod(shape))
    return rows.reshape(-1)[:n].reshape(shape)


def _split_chips(full, axis):
    shp = full.shape
    parts = full.reshape(shp[:axis] + (N_CHIPS, shp[axis] // N_CHIPS) + shp[axis + 1:])
    return jnp.moveaxis(parts, axis, 0)


def _join_chips(parts, axis):
    moved = jnp.moveaxis(parts, 0, axis)
    shp = moved.shape
    return moved.reshape(shp[:axis] + (shp[axis] * shp[axis + 1],) + shp[axis + 2:])


def _pack_rows(arrays):
    blocks = [_to_rows(a) for a in arrays]
    total = sum(b.shape[0] for b in blocks)
    pad = (-total) % 256
    if pad:
        blocks.append(jnp.zeros((pad, ROW_W), blocks[0].dtype))
    return jnp.concatenate(blocks, axis=0)


def _unpack_rows(buf, shapes):
    out, r0 = [], 0
    for shp in shapes:
        n = _rows_of(int(np.prod(shp)))
        out.append(_from_rows(buf[r0:r0 + n], shp))
        r0 += n
    return out


def _as2d(a):
    return a.reshape(-1, a.shape[-1])


def kernel(x, c, ada_w, ada_b, ln_g, ln_b, ab_w_in, rw_mu, rw_w0, rw_w_up, rw_a0, rw_a_up, rw_g_up, rw_k_k, rw_k_a, rw_r_k, rw_lnx_g, rw_lnx_b, sc_conv_w, ab_w_out, dil_w_qkv, dil_w_out, rel_bias, mlp_w1, mlp_w2, loss_target, m_ada_w, m_ada_b, m_ln_g, m_ln_b, m_ab_w_in, m_rw_mu, m_rw_w0, m_rw_w_up, m_rw_a0, m_rw_a_up, m_rw_g_up, m_rw_k_k, m_rw_k_a, m_rw_r_k, m_rw_lnx_g, m_rw_lnx_b, m_sc_conv_w, m_ab_w_out, m_dil_w_qkv, m_dil_w_out, m_rel_bias, m_mlp_w1, m_mlp_w2, v_ada_w, v_ada_b, v_ln_g, v_ln_b, v_ab_w_in, v_rw_mu, v_rw_w0, v_rw_w_up, v_rw_a0, v_rw_a_up, v_rw_g_up, v_rw_k_k, v_rw_k_a, v_rw_r_k, v_rw_lnx_g, v_rw_lnx_b, v_sc_conv_w, v_ab_w_out, v_dil_w_qkv, v_dil_w_out, v_rel_bias, v_mlp_w1, v_mlp_w2):
    args = dict(locals())
    w_in = {n: args[n] for n in WEIGHT_ORDER}
    m_in = {n: args["m_" + n] for n in WEIGHT_ORDER}
    v_in = {n: args["v_" + n] for n in WEIGHT_ORDER}
    me = _me()
    chip = _chip_of(me)
    dev = _dev_of(me)

    c_all = _dev_all_gather(c, "gather_c")[:, 0, :]
    n_col = ada_w.shape[2]
    ada_b_cols = lax.dynamic_slice_in_dim(ada_b, chip * n_col, n_col, axis=1)[:, None, :]
    mod_cols = _ada_fwd(c_all, ada_w, ada_b_cols)

    big_buf = _pack_rows([w_in[n].astype(BF16) for n in BIG])
    big_all = _chip_all_gather(big_buf, "gather_big")
    small_buf = _pack_rows([mod_cols] + [w_in[n] for n in SMALL_SHARDED])
    small_all = _chip_all_gather(small_buf, "gather_small")

    wts = {n: w_in[n] for n in REPLICATED}
    big_parts = zip(*[_unpack_rows(big_all[j], [w_in[n].shape for n in BIG]) for j in range(N_CHIPS)])
    for n, parts in zip(BIG, big_parts):
        wts[n] = _join_chips(jnp.stack(parts), SHARDED[n]).astype(F32)
    small_shapes = [mod_cols.shape] + [w_in[n].shape for n in SMALL_SHARDED]
    small_parts = list(zip(*[_unpack_rows(small_all[j], small_shapes) for j in range(N_CHIPS)]))
    for n, parts in zip(SMALL_SHARDED, small_parts[1:]):
        wts[n] = _join_chips(jnp.stack(parts), SHARDED[n])
    mod_all = _join_chips(jnp.stack(small_parts[0]), 2)
    mods = lax.dynamic_slice_in_dim(mod_all, dev, 1, axis=1).reshape(DEPTH, 6, 1, D_MODEL)

    def local_loss(xv, modv, wv):
        return _forward_local(xv, modv, wv, loss_target[0])[0, 0]

    loss_local, (grad_x, dmods, dw) = jax.value_and_grad(local_loss, argnums=(0, 1, 2))(x[0], mods, wts)
    loss = lax.psum(loss_local, ("x", "y", "c"))

    small_row = jnp.concatenate([dmods.reshape(-1)] + [dw[n].reshape(-1) for n in REPLICATED[1:]])
    n_small = small_row.shape[0]
    n_small_pad = -(-n_small // LANES) * LANES
    small_row = jnp.pad(small_row, (0, n_small_pad - n_small))[None, :]
    rows_all = _dev_all_gather(small_row, "gather_small_grads")
    small_sum = _sum_slots(rows_all, "sum_small_grads")

    dmod_all = rows_all[:, 0, :DEPTH * 6 * D_MODEL].reshape(N_DEV, DEPTH, 6 * D_MODEL)
    dmod_cols = lax.dynamic_slice_in_dim(dmod_all, chip * n_col, n_col, axis=2)
    dmod_cols = jnp.pad(jnp.moveaxis(dmod_cols, 0, 1), ((0, 0), (0, LANES - N_DEV), (0, 0)))
    c_all_t = jnp.pad(c_all.T, ((0, 0), (0, LANES - N_DEV)))
    grads = {"ada_w": _ada_grad(c_all_t, dmod_cols)}
    grads["ada_b"] = small_sum[0, :DEPTH * 6 * D_MODEL].reshape(ada_b.shape)
    r0 = DEPTH * 6 * D_MODEL
    for n in REPLICATED[1:]:
        size = int(np.prod(w_in[n].shape))
        grads[n] = small_sum[0, r0:r0 + size].reshape(w_in[n].shape)
        r0 += size

    sharded_names = BIG + SMALL_SHARDED
    per_chip = [_split_chips(dw[n], SHARDED[n]) for n in sharded_names]
    send = jnp.stack([_pack_rows([pc[j] for pc in per_chip]) for j in range(N_CHIPS)])
    recv = _chip_scatter(send, "scatter_grads")
    core_part = _sum_slots(recv, "sum_chips")
    both = _core_swap(core_part, "swap_cores")
    g_rows = _sum_slots(both, "sum_cores")
    for n, g in zip(sharded_names, _unpack_rows(g_rows, [w_in[n].shape for n in sharded_names])):
        grads[n] = g

    deltas, new_m, new_v = {}, {}, {}
    for n in WEIGHT_ORDER:
        shp = w_in[n].shape
        d, nm, nv = _adamw(_as2d(w_in[n]), _as2d(grads[n]), _as2d(m_in[n]), _as2d(v_in[n]), "adamw_" + n)
        deltas[n], new_m[n], new_v[n] = d.reshape(shp), nm.reshape(shp), nv.reshape(shp)

    return (loss, grad_x[None], *[grads[n] for n in WEIGHT_ORDER], *[deltas[n] for n in WEIGHT_ORDER],
            *[new_m[n] for n in WEIGHT_ORDER], *[new_v[n] for n in WEIGHT_ORDER])
```

```python
import functools
import math

import numpy as np
import jax
import jax.numpy as jnp
from jax import lax
from jax.experimental import pallas as pl
from jax.experimental.pallas import tpu as pltpu

F32 = jnp.float32
BF16 = jnp.bfloat16
MESH = pl.DeviceIdType.MESH

D_MODEL = 1024
DEPTH = 2
RW_WIDTH = 512
HEAD_DIM = 64
N_HEADS = 8
RW_DECAY_RANK = 64
RW_ICLR_RANK = 64
RW_GATE_RANK = 160
RW_GN_EPS = 64e-5
RW_PROJ = 3 * RW_WIDTH + RW_DECAY_RANK + RW_ICLR_RANK + RW_GATE_RANK
SC_WIDTH = 512
AB_PROJ = RW_PROJ + 3 * SC_WIDTH
DIL_PATTERNS = ((128, 1), (512, 4), (2048, 16))
N_GROUPS = 3
DIL_WIDTH = 512
DIL_PROJ = N_GROUPS * 3 * DIL_WIDTH
BLOCK = 128
N_BUCKETS = 32
MAX_DISTANCE = 2048
D_FF = 4 * D_MODEL
ALPHA = (2 * DEPTH) ** 0.25
LN_EPS = 1e-5
ADAM_LR = 0.001
ADAM_B1 = 0.9
ADAM_B2 = 0.999
ADAM_EPS = 1e-08
ADAM_WD = 0.01
ADAM_STEP = 10

N_CHIPS = 4
N_DEV = 8
LANES = 128
SUBLANES = 8
ROW_W = 1024
SCAN_CHUNK = 16
VMEM_LIMIT = 48 * 1024 * 1024
NEG_BIG = -1e30


def _pick(n, cands):
    for c in cands:
        if n % c == 0:
            return c
    return n


def _cparams(sem=None, vmem=None):
    return pltpu.CompilerParams(dimension_semantics=sem, vmem_limit_bytes=vmem)


_DOT_DIMS = {
    "nn": (((1,), (0,)), ((), ())),
    "nt": (((1,), (1,)), ((), ())),
    "tn": (((0,), (0,)), ((), ())),
}


def _mm(a, b, mode, name):
    if mode == "nn":
        (m, k), (_, n) = a.shape, b.shape
    elif mode == "nt":
        (m, k), (n, _) = a.shape, b.shape
    else:
        (k, m), (_, n) = a.shape, b.shape
    tm = _pick(m, (1024, 512, 256, 128))
    tn = _pick(n, (1024, 768, 512, 384, 256, 128))
    tk = _pick(k, (512, 256, 128))
    nk = k // tk
    if mode == "tn":
        a_spec = pl.BlockSpec((tk, tm), lambda i, j, kk: (kk, i))
    else:
        a_spec = pl.BlockSpec((tm, tk), lambda i, j, kk: (i, kk))
    if mode == "nt":
        b_spec = pl.BlockSpec((tn, tk), lambda i, j, kk: (j, kk))
    else:
        b_spec = pl.BlockSpec((tk, tn), lambda i, j, kk: (kk, j))
    dims = _DOT_DIMS[mode]

    def body(a_ref, b_ref, o_ref, acc_ref):
        kk = pl.program_id(2)

        @pl.when(kk == 0)
        def _():
            acc_ref[...] = jnp.zeros_like(acc_ref)

        acc_ref[...] += lax.dot_general(a_ref[...].astype(BF16), b_ref[...].astype(BF16), dims,
                                        preferred_element_type=F32)

        @pl.when(kk == nk - 1)
        def _():
            o_ref[...] = acc_ref[...]

    return pl.pallas_call(
        body, name=name, grid=(m // tm, n // tn, nk),
        in_specs=[a_spec, b_spec],
        out_specs=pl.BlockSpec((tm, tn), lambda i, j, kk: (i, j)),
        out_shape=jax.ShapeDtypeStruct((m, n), F32),
        scratch_shapes=[pltpu.VMEM((tm, tn), F32)],
        compiler_params=_cparams(("parallel", "parallel", "arbitrary"), VMEM_LIMIT),
    )(a, b)


def _linear(x, w, slot, name):
    @jax.custom_vjp
    def op(x, w, slot):
        return _mm(x, w, "nn", name + "_fwd")

    def fwd(x, w, slot):
        return _mm(x, w, "nn", name + "_fwd"), (x, w)

    def bwd(res, dy):
        x, w = res
        return _mm(dy, w, "nt", name + "_dx"), jnp.zeros_like(w), _mm(x, dy, "tn", name + "_dw")

    op.defvjp(fwd, bwd)
    return op(x, w, slot)


def _const_map(ndim):
    return lambda *g: (0,) * ndim


def _first_step(n_grid):
    return functools.reduce(jnp.logical_and, [pl.program_id(d) == 0 for d in range(n_grid)])


def _fused(fn, ins, outs, grid, name):
    arrays = [i[0] for i in ins]
    n_in, n_out, n_grid = len(ins), len(outs), len(grid)
    in_specs = [pl.BlockSpec(bs, im) for (_, bs, im, _) in ins]
    out_specs = [pl.BlockSpec(bs, im) for (_, bs, im, _) in outs]
    out_shapes = [jax.ShapeDtypeStruct(s, F32) for (s, _, _, _) in outs]
    sem = ("arbitrary",) * n_grid

    def fwd_call(*xs):
        def body(*refs):
            vals = [r[...] for r in refs[:n_in]]
            ys = fn(*vals)
            first = _first_step(n_grid)
            for o_ref, y, (_, _, _, kind) in zip(refs[n_in:], ys, outs):
                if kind == "t":
                    o_ref[...] = y
                else:
                    @pl.when(first)
                    def _(o_ref=o_ref):
                        o_ref[...] = jnp.zeros_like(o_ref)

                    o_ref[...] += y

        return pl.pallas_call(
            body, name=name + "_fwd", grid=grid, in_specs=in_specs, out_specs=out_specs,
            out_shape=out_shapes, compiler_params=_cparams(sem, VMEM_LIMIT))(*xs)

    def bwd_call(xs, dys):
        d_specs = [pl.BlockSpec(bs, im) for (_, bs, im, _) in outs]
        g_specs = [pl.BlockSpec(bs, im) for (_, bs, im, _) in ins]
        g_shapes = [jax.ShapeDtypeStruct(a.shape, F32) for a in arrays]

        def body(*refs):
            vals = [r[...] for r in refs[:n_in]]
            dvals = tuple(r[...] for r in refs[n_in:n_in + n_out])
            _, vjp = jax.vjp(lambda *v: tuple(fn(*v)), *vals)
            gs = vjp(dvals)
            first = _first_step(n_grid)
            for g_ref, g, (_, _, _, kind) in zip(refs[n_in + n_out:], gs, ins):
                if kind == "t":
                    g_ref[...] = g
                else:
                    @pl.when(first)
                    def _(g_ref=g_ref):
                        g_ref[...] = jnp.zeros_like(g_ref)

                    g_ref[...] += g

        return pl.pallas_call(
            body, name=name + "_bwd", grid=grid, in_specs=in_specs + d_specs, out_specs=g_specs,
            out_shape=g_shapes, compiler_params=_cparams(sem, VMEM_LIMIT))(*xs, *dys)

    @jax.custom_vjp
    def op(*xs):
        return tuple(fwd_call(*xs))

    def op_fwd(*xs):
        return tuple(fwd_call(*xs)), xs

    def op_bwd(xs, dys):
        return tuple(bwd_call(xs, dys))

    op.defvjp(op_fwd, op_bwd)
    return op(*arrays)


def _tiled(a, tile, cols=None, col_block=0):
    cols = a.shape[1] if cols is None else cols
    return (a, (tile, cols), lambda i, cb=col_block: (i, cb), "t")


def _shared(a):
    return (a, a.shape, _const_map(a.ndim), "b")


def _tiled_out(rows, cols, tile):
    return ((rows, cols), (tile, cols), lambda i: (i, 0), "t")


@jax.custom_vjp
def _bdot(x, w):
    return jnp.dot(x.astype(BF16), w.astype(BF16), preferred_element_type=F32)


def _bdot_fwd(x, w):
    return _bdot(x, w), (x, w)


def _bdot_bwd(res, dy):
    x, w = res
    dyb = dy.astype(BF16)
    dx = lax.dot_general(dyb, w.astype(BF16), _DOT_DIMS["nt"], preferred_element_type=F32)
    dw = lax.dot_general(x.astype(BF16), dyb, _DOT_DIMS["tn"], preferred_element_type=F32)
    return dx, dw


_bdot.defvjp(_bdot_fwd, _bdot_bwd)


def _head_sum(x):
    n = x.shape[-1]
    hi = lax.broadcasted_iota(jnp.int32, (n, n), 0) // HEAD_DIM
    hj = lax.broadcasted_iota(jnp.int32, (n, n), 1) // HEAD_DIM
    e = (hi == hj).astype(F32)
    return jnp.dot(x, e, precision=lax.Precision.HIGHEST, preferred_element_type=F32)


def _softplus(x):
    return jnp.maximum(x, 0.0) + jnp.log1p(jnp.exp(-jnp.abs(x)))


def _layer_norm_rows(z, g, b):
    mu = jnp.mean(z, axis=-1, keepdims=True)
    zc = z - mu
    var = jnp.mean(zc * zc, axis=-1, keepdims=True)
    return zc * lax.rsqrt(var + LN_EPS) * g + b


N_PAIRS = N_HEADS // 2


def _scan_consts():
    k = lax.broadcasted_iota(jnp.int32, (HEAD_DIM, LANES), 0)
    j = lax.broadcasted_iota(jnp.int32, (HEAD_DIM, LANES), 1)
    diag = (j % HEAD_DIM) == k
    jj = lax.broadcasted_iota(jnp.int32, (LANES, LANES), 0) // HEAD_DIM
    ll = lax.broadcasted_iota(jnp.int32, (LANES, LANES), 1) // HEAD_DIM
    same_head = (jj == ll).astype(BF16)
    lane_lo = j < HEAD_DIM
    return diag, same_head, lane_lo


def _unrolled(n, body, carry):
    for i in range(n):
        carry = body(i, carry)
    return carry


def _fill_cols(srcs, col_ref, diag_bf, same_head, n_steps):
    diag3 = jnp.concatenate([diag_bf] * 3, axis=1)
    rhs3 = jnp.concatenate([same_head] * 3, axis=0)
    for t in range(n_steps):
        blocks = []
        for src in srcs:
            rb = jnp.broadcast_to(src[t:t + 1, :], (16, RW_WIDTH))
            hi = rb.astype(BF16)
            r1 = rb - hi.astype(F32)
            mid = r1.astype(BF16)
            lo = (r1 - mid.astype(F32)).astype(BF16)
            for hp in range(N_PAIRS):
                lanes = slice(hp * LANES, (hp + 1) * LANES)
                cat = jnp.concatenate([hi[:, lanes], mid[:, lanes], lo[:, lanes]], axis=1)
                blocks.append(jnp.concatenate([cat] * 4, axis=0) * diag3)
        out = jnp.dot(jnp.concatenate(blocks, axis=0), rhs3, preferred_element_type=F32)
        for vi in range(len(srcs)):
            for hp in range(N_PAIRS):
                r0 = (vi * N_PAIRS + hp) * HEAD_DIM
                col_ref[vi, hp, t] = out[r0:r0 + HEAD_DIM]


def _scan_fwd_call(r, w, k, v, a, b):
    t_len = r.shape[0]
    ch = SCAN_CHUNK
    n_ch = t_len // ch

    def body(r_ref, w_ref, k_ref, v_ref, a_ref, b_ref, y_ref, ck_ref, st_ref, col_ref):
        c = pl.program_id(0)

        @pl.when(c == 0)
        def _():
            st_ref[...] = jnp.zeros_like(st_ref)

        ck_ref[0] = st_ref[...]
        diag, same_head, _ = _scan_consts()
        diag_bf = diag.astype(BF16)
        _fill_cols((w_ref, a_ref, b_ref, k_ref, r_ref), col_ref, diag_bf, same_head, ch)

        def step(t, states):
            new = []
            for hp in range(N_PAIRS):
                lanes = slice(hp * LANES, (hp + 1) * LANES)
                s = states[hp]
                sa = jnp.sum(s * col_ref[1, hp, t], axis=0, keepdims=True)
                s = s * col_ref[0, hp, t] + col_ref[2, hp, t] * sa + col_ref[3, hp, t] * v_ref[pl.ds(t, 1), lanes]
                y_ref[pl.ds(t, 1), lanes] = jnp.sum(s * col_ref[4, hp, t], axis=0, keepdims=True)
                new.append(s)
            return tuple(new)

        states = _unrolled(ch, step, tuple(st_ref[hp] for hp in range(N_PAIRS)))
        for hp in range(N_PAIRS):
            st_ref[hp] = states[hp]

    row_spec = pl.BlockSpec((ch, RW_WIDTH), lambda c: (c, 0))
    return pl.pallas_call(
        body, name="rwkv_scan_fwd", grid=(n_ch,),
        in_specs=[row_spec] * 6,
        out_specs=[row_spec, pl.BlockSpec((1, N_PAIRS, HEAD_DIM, LANES), lambda c: (c, 0, 0, 0))],
        out_shape=[jax.ShapeDtypeStruct((t_len, RW_WIDTH), F32),
                   jax.ShapeDtypeStruct((n_ch, N_PAIRS, HEAD_DIM, LANES), F32)],
        scratch_shapes=[pltpu.VMEM((N_PAIRS, HEAD_DIM, LANES), F32),
                        pltpu.VMEM((5, N_PAIRS, ch, HEAD_DIM, LANES), F32)],
        compiler_params=_cparams(("arbitrary",), VMEM_LIMIT),
    )(r, w, k, v, a, b)


def _scan_bwd_call(r, w, k, v, a, b, ck, dy):
    t_len = r.shape[0]
    ch = SCAN_CHUNK
    n_ch = t_len // ch

    def body(r_ref, w_ref, k_ref, v_ref, a_ref, b_ref, ck_ref, dy_ref,
             dr_ref, dw_ref, dk_ref, dv_ref, da_ref, db_ref,
             ds_ref, col_ref, sp_ref, sa_ref):
        c = pl.program_id(0)

        @pl.when(c == 0)
        def _():
            ds_ref[...] = jnp.zeros_like(ds_ref)

        diag, same_head, lane_lo = _scan_consts()
        diag_bf = diag.astype(BF16)
        diag_f = diag.astype(F32)
        _fill_cols((w_ref, a_ref, b_ref, k_ref, r_ref), col_ref, diag_bf, same_head, ch)

        def replay(t, states):
            new = []
            for hp in range(N_PAIRS):
                lanes = slice(hp * LANES, (hp + 1) * LANES)
                s = states[hp]
                sp_ref[t, hp] = s
                sa = jnp.sum(s * col_ref[1, hp, t], axis=0, keepdims=True)
                sa_ref[pl.ds(t, 1), lanes] = sa
                new.append(s * col_ref[0, hp, t] + col_ref[2, hp, t] * sa
                           + col_ref[3, hp, t] * v_ref[pl.ds(t, 1), lanes])
            return tuple(new)

        _unrolled(ch, replay, tuple(ck_ref[0, hp] for hp in range(N_PAIRS)))

        def key_row(p):
            lo = jnp.sum(jnp.where(lane_lo, p, 0.0), axis=1, keepdims=True)
            hi = jnp.sum(jnp.where(lane_lo, 0.0, p), axis=1, keepdims=True)
            return jnp.sum(jnp.where(lane_lo, lo, hi) * diag_f, axis=0, keepdims=True)

        def back(i, grads):
            t = ch - 1 - i
            new = []
            for hp in range(N_PAIRS):
                lanes = slice(hp * LANES, (hp + 1) * LANES)
                wc, ac, bc, kc, rc = (col_ref[vi, hp, t] for vi in range(5))
                sp = sp_ref[t, hp]
                sa = sa_ref[pl.ds(t, 1), lanes]
                vrow = v_ref[pl.ds(t, 1), lanes]
                dyrow = dy_ref[pl.ds(t, 1), lanes]
                st = sp * wc + bc * sa + kc * vrow
                g = grads[hp] + rc * dyrow
                dsa = jnp.sum(g * bc, axis=0, keepdims=True)
                dv_ref[pl.ds(t, 1), lanes] = jnp.sum(g * kc, axis=0, keepdims=True)
                dr_ref[pl.ds(t, 1), lanes] = key_row(st * dyrow)
                dk_ref[pl.ds(t, 1), lanes] = key_row(g * vrow)
                db_ref[pl.ds(t, 1), lanes] = key_row(g * sa)
                dw_ref[pl.ds(t, 1), lanes] = key_row(g * sp)
                da_ref[pl.ds(t, 1), lanes] = key_row(sp * dsa)
                new.append(g * wc + ac * dsa)
            return tuple(new)

        grads = _unrolled(ch, back, tuple(ds_ref[hp] for hp in range(N_PAIRS)))
        for hp in range(N_PAIRS):
            ds_ref[hp] = grads[hp]

    row_spec = pl.BlockSpec((ch, RW_WIDTH), lambda c: (n_ch - 1 - c, 0))
    out_sds = jax.ShapeDtypeStruct((t_len, RW_WIDTH), F32)
    return pl.pallas_call(
        body, name="rwkv_scan_bwd", grid=(n_ch,),
        in_specs=[row_spec] * 6 + [pl.BlockSpec((1, N_PAIRS, HEAD_DIM, LANES), lambda c: (n_ch - 1 - c, 0, 0, 0)),
                                   row_spec],
        out_specs=[row_spec] * 6,
        out_shape=[out_sds] * 6,
        scratch_shapes=[pltpu.VMEM((N_PAIRS, HEAD_DIM, LANES), F32),
                        pltpu.VMEM((5, N_PAIRS, ch, HEAD_DIM, LANES), F32),
                        pltpu.VMEM((ch, N_PAIRS, HEAD_DIM, LANES), F32),
                        pltpu.VMEM((ch, RW_WIDTH), F32)],
        compiler_params=_cparams(("arbitrary",), VMEM_LIMIT),
    )(r, w, k, v, a, b, ck, dy)


@jax.custom_vjp
def _rwkv_scan(r, w, k, v, a, b):
    return _scan_fwd_call(r, w, k, v, a, b)[0]


def _rwkv_scan_fwd(r, w, k, v, a, b):
    y, ck = _scan_fwd_call(r, w, k, v, a, b)
    return y, (r, w, k, v, a, b, ck)


def _rwkv_scan_bwd(res, dy):
    return tuple(_scan_bwd_call(*res, dy))


_rwkv_scan.defvjp(_rwkv_scan_fwd, _rwkv_scan_bwd)


ATT_SCALE = HEAD_DIM ** -0.5
COLS_PER_POS = DIL_PROJ // DIL_WIDTH


def _att_masks():
    qi = lax.broadcasted_iota(jnp.int32, (BLOCK, BLOCK), 0)
    ki = lax.broadcasted_iota(jnp.int32, (BLOCK, BLOCK), 1)
    lane = lax.broadcasted_iota(jnp.int32, (1, LANES), 1)
    return ki <= qi, ki >= qi, lane


def _att_fwd_call(p, bias, g, dil):
    t_len = p.shape[0]
    l_len = t_len // dil
    nb = l_len // BLOCK
    pv = p.reshape(l_len, dil * DIL_PROJ)
    base = g * 3

    def body(q_ref, kc_ref, kp_ref, vc_ref, vp_ref, bias_ref, o_ref, lse_ref):
        n = pl.program_id(1)
        cur_ok, prev_band, lane = _att_masks()
        prev_ok = jnp.logical_and(prev_band, n > 0)
        for hp in range(N_PAIRS):
            lanes = slice(hp * LANES, (hp + 1) * LANES)
            q2 = q_ref[:, lanes].astype(BF16)
            kc = kc_ref[:, lanes].astype(BF16)
            kp = kp_ref[:, lanes].astype(BF16)
            vc = vc_ref[:, lanes].astype(BF16)
            vp = vp_ref[:, lanes].astype(BF16)
            o2 = jnp.zeros((BLOCK, LANES), F32)
            for hh in range(2):
                h = 2 * hp + hh
                mine = (lane // HEAD_DIM) == hh
                qm = jnp.where(mine, q2, jnp.zeros_like(q2))
                s_c = lax.dot_general(qm, kc, _DOT_DIMS["nt"], preferred_element_type=F32) * ATT_SCALE
                s_p = lax.dot_general(qm, kp, _DOT_DIMS["nt"], preferred_element_type=F32) * ATT_SCALE
                s_c = jnp.where(cur_ok, s_c + bias_ref[h, :, BLOCK:], NEG_BIG)
                s_p = jnp.where(prev_ok, s_p + bias_ref[h, :, :BLOCK], NEG_BIG)
                m = jnp.maximum(jnp.max(s_c, axis=-1, keepdims=True), jnp.max(s_p, axis=-1, keepdims=True))
                e_c = jnp.exp(s_c - m)
                e_p = jnp.exp(s_p - m)
                den = jnp.sum(e_c, axis=-1, keepdims=True) + jnp.sum(e_p, axis=-1, keepdims=True)
                o_h = (jnp.dot((e_c / den).astype(BF16), vc, preferred_element_type=F32)
                       + jnp.dot((e_p / den).astype(BF16), vp, preferred_element_type=F32))
                o2 = o2 + jnp.where(mine, o_h, 0.0)
                lse_ref[h] = jnp.broadcast_to(m + jnp.log(den), (BLOCK, LANES))
            o_ref[:, lanes] = o2

    def col(j):
        return lambda r, n: (n, r * COLS_PER_POS + base + j)

    def col_prev(j):
        return lambda r, n: (jnp.maximum(n - 1, 0), r * COLS_PER_POS + base + j)

    blk = (BLOCK, DIL_WIDTH)
    o, lse = pl.pallas_call(
        body, name=f"dil_att_fwd_g{g}", grid=(dil, nb),
        in_specs=[pl.BlockSpec(blk, col(0)), pl.BlockSpec(blk, col(1)), pl.BlockSpec(blk, col_prev(1)),
                  pl.BlockSpec(blk, col(2)), pl.BlockSpec(blk, col_prev(2)),
                  pl.BlockSpec(bias.shape, _const_map(3))],
        out_specs=[pl.BlockSpec(blk, lambda r, n: (n, r)),
                   pl.BlockSpec((N_HEADS, BLOCK, LANES), lambda r, n: (0, n, r))],
        out_shape=[jax.ShapeDtypeStruct((l_len, dil * DIL_WIDTH), F32),
                   jax.ShapeDtypeStruct((N_HEADS, l_len, dil * LANES), F32)],
        compiler_params=_cparams(("arbitrary", "arbitrary"), VMEM_LIMIT),
    )(pv, pv, pv, pv, pv, bias)
    return o.reshape(t_len, DIL_WIDTH), lse.reshape(N_HEADS, t_len, LANES)


def _att_bwd_call(p, bias, o, lse, do, dlse, g, dil):
    t_len = p.shape[0]
    l_len = t_len // dil
    nb = l_len // BLOCK
    pv = p.reshape(l_len, dil * DIL_PROJ)
    ov = o.reshape(l_len, dil * DIL_WIDTH)
    dov = do.reshape(l_len, dil * DIL_WIDTH)
    lsev = lse.reshape(N_HEADS, l_len, dil * LANES)
    dlsev = dlse.reshape(N_HEADS, l_len, dil * LANES)
    base = g * 3

    def body(q_ref, qn_ref, k_ref, v_ref, do_ref, don_ref, o_ref, on_ref, lse_ref, lsen_ref, dl_ref, dln_ref,
             bias_ref, dq_ref, dk_ref, dv_ref, dbias_ref, carry_ref):
        r = pl.program_id(0)
        n = pl.program_id(1)
        cur_ok, prev_band, lane = _att_masks()
        has_next = n + 1 < nb

        @pl.when(jnp.logical_and(r == 0, n == 0))
        def _():
            dbias_ref[...] = jnp.zeros_like(dbias_ref)

        @pl.when(n == 0)
        def _():
            carry_ref[...] = jnp.zeros_like(carry_ref)

        for hp in range(N_PAIRS):
            lanes = slice(hp * LANES, (hp + 1) * LANES)
            k2 = k_ref[:, lanes].astype(BF16)
            v2 = v_ref[:, lanes].astype(BF16)
            dk2 = jnp.zeros((BLOCK, LANES), F32)
            dv2 = jnp.zeros((BLOCK, LANES), F32)
            dq_cur = carry_ref[:, lanes]
            dq_next = jnp.zeros((BLOCK, LANES), F32)
            for hh in range(2):
                h = 2 * hp + hh
                mine = (lane // HEAD_DIM) == hh
                tiles = (
                    (q_ref, do_ref, o_ref, lse_ref, dl_ref, cur_ok, slice(BLOCK, 2 * BLOCK), None),
                    (qn_ref, don_ref, on_ref, lsen_ref, dln_ref, prev_band, slice(0, BLOCK), has_next),
                )
                for ti, (qr, dor, orf, lr, dlr, ok, bcols, gate) in enumerate(tiles):
                    q2 = qr[:, lanes].astype(BF16)
                    qm = jnp.where(mine, q2, jnp.zeros_like(q2))
                    do_f = jnp.where(mine, dor[:, lanes], 0.0)
                    dom = do_f.astype(BF16)
                    s = lax.dot_general(qm, k2, _DOT_DIMS["nt"], preferred_element_type=F32) * ATT_SCALE
                    s = s + bias_ref[h, :, bcols]
                    if gate is not None:
                        ok = jnp.logical_and(ok, gate)
                    pr = jnp.where(ok, jnp.exp(jnp.minimum(s - lr[h], 0.0)), 0.0)
                    dp = lax.dot_general(dom, v2, _DOT_DIMS["nt"], preferred_element_type=F32)
                    delta = jnp.sum(do_f * orf[:, lanes], axis=-1, keepdims=True)
                    dl = jnp.sum(dlr[h], axis=-1, keepdims=True)
                    ds = pr * (dp - delta + dl)
                    dsb = ds.astype(BF16)
                    dq_h = jnp.where(mine, jnp.dot(dsb, k2, preferred_element_type=F32), 0.0) * ATT_SCALE
                    if ti == 0:
                        dq_cur = dq_cur + dq_h
                    else:
                        dq_next = dq_next + dq_h
                    dk2 = dk2 + lax.dot_general(dsb, qm, _DOT_DIMS["tn"], preferred_element_type=F32) * ATT_SCALE
                    dv2 = dv2 + lax.dot_general(pr.astype(BF16), dom, _DOT_DIMS["tn"], preferred_element_type=F32)
                    dbias_ref[h, :, bcols] += ds
            dq_ref[:, lanes] = dq_cur
            carry_ref[:, lanes] = dq_next
            dk_ref[:, lanes] = dk2
            dv_ref[:, lanes] = dv2

    def nxt(n):
        return jnp.minimum(n + 1, nb - 1)

    blk = (BLOCK, DIL_WIDTH)
    hblk = (N_HEADS, BLOCK, LANES)
    qcol = lambda j: (lambda r, n: (n, r * COLS_PER_POS + base + j))
    q_next = lambda r, n: (nxt(n), r * COLS_PER_POS + base)
    rown = lambda r, n: (n, r)
    rown_next = lambda r, n: (nxt(n), r)
    hrow = lambda r, n: (0, n, r)
    hrow_next = lambda r, n: (0, nxt(n), r)
    sds = jax.ShapeDtypeStruct((l_len, dil * DIL_WIDTH), F32)
    dq, dk, dv, dbias = pl.pallas_call(
        body, name=f"dil_att_bwd_g{g}", grid=(dil, nb),
        in_specs=[pl.BlockSpec(blk, qcol(0)), pl.BlockSpec(blk, q_next),
                  pl.BlockSpec(blk, qcol(1)), pl.BlockSpec(blk, qcol(2)),
                  pl.BlockSpec(blk, rown), pl.BlockSpec(blk, rown_next),
                  pl.BlockSpec(blk, rown), pl.BlockSpec(blk, rown_next),
                  pl.BlockSpec(hblk, hrow), pl.BlockSpec(hblk, hrow_next),
                  pl.BlockSpec(hblk, hrow), pl.BlockSpec(hblk, hrow_next),
                  pl.BlockSpec(bias.shape, _const_map(3))],
        out_specs=[pl.BlockSpec(blk, rown)] * 3 + [pl.BlockSpec(bias.shape, _const_map(3))],
        out_shape=[sds, sds, sds, jax.ShapeDtypeStruct(bias.shape, F32)],
        scratch_shapes=[pltpu.VMEM((BLOCK, DIL_WIDTH), F32)],
        compiler_params=_cparams(("arbitrary", "arbitrary"), VMEM_LIMIT),
    )(pv, pv, pv, pv, dov, dov, ov, ov, lsev, lsev, dlsev, dlsev, bias)
    shp = (t_len, DIL_WIDTH)
    return dq.reshape(shp), dk.reshape(shp), dv.reshape(shp), dbias


def _att_all_groups(p, biases):
    outs = [_att_fwd_call(p, biases[g], g, dil) for g, (_, dil) in enumerate(DIL_PATTERNS)]
    return tuple(o for o, _ in outs), tuple(l for _, l in outs)


@jax.custom_vjp
def _dilated_attention(p, biases):
    return _att_all_groups(p, biases)


def _dilated_attention_fwd(p, biases):
    os_, lses = _att_all_groups(p, biases)
    return (os_, lses), (p, biases, os_, lses)


def _dilated_attention_bwd(res, cts):
    p, biases, os_, lses = res
    dos, dlses = cts
    parts, dbiases = [], []
    for g, (_, dil) in enumerate(DIL_PATTERNS):
        dq, dk, dv, dbias = _att_bwd_call(p, biases[g], os_[g], lses[g], dos[g], dlses[g], g, dil)
        parts += [dq, dk, dv]
        dbiases.append(dbias)
    return jnp.concatenate(parts, axis=1), tuple(dbiases)


_dilated_attention.defvjp(_dilated_attention_fwd, _dilated_attention_bwd)


def _me():
    return lax.axis_index("x"), lax.axis_index("y"), lax.axis_index("c")


def _flip(me, f):
    return tuple((1 - m) if b else m for m, b in zip(me, f))


def _chip_of(d):
    return 2 * d[0] + d[1]


def _dev_of(d):
    return 4 * d[0] + 2 * d[1] + d[2]


EXCHANGE_CHUNKS = 8
CHIP_FLIPS = ((1, 0, 0), (0, 1, 0), (1, 1, 0))
ALL_FLIPS = tuple((a, b, c) for a in (0, 1) for b in (0, 1) for c in (0, 1) if a or b or c)
CORE_FLIPS = ((0, 0, 1),)


def _exchange(src, n_slots, flips, send_slot, recv_slot, name):
    _, rows, cols = src.shape
    chunks = EXCHANGE_CHUNKS if rows % (EXCHANGE_CHUNKS * PACK_ROWS) == 0 else 1
    rc = rows // chunks
    n = len(flips) * chunks

    def body(src_ref, dst_ref, send_sems, recv_sems, local_sem):
        me = _me()
        local = pltpu.make_async_copy(src_ref.at[send_slot(me, me)], dst_ref.at[recv_slot(me)], local_sem)
        local.start()
        copies = []
        for q in range(chunks):
            for kk, f in enumerate(flips):
                peer = _flip(me, f)
                cp = pltpu.make_async_remote_copy(
                    src_ref=src_ref.at[send_slot(me, peer), pl.ds(q * rc, rc)],
                    dst_ref=dst_ref.at[recv_slot(me), pl.ds(q * rc, rc)],
                    send_sem=send_sems.at[kk * chunks + q], recv_sem=recv_sems.at[kk * chunks + q],
                    device_id=peer, device_id_type=MESH)
                cp.start()
                copies.append(cp)
        for cp in copies:
            cp.wait()
        local.wait()

    return pl.pallas_call(
        body, name=name,
        out_shape=jax.ShapeDtypeStruct((n_slots, rows, cols), src.dtype),
        in_specs=[pl.BlockSpec(memory_space=pl.ANY)],
        out_specs=pl.BlockSpec(memory_space=pl.ANY),
        scratch_shapes=[pltpu.SemaphoreType.DMA((n,)), pltpu.SemaphoreType.DMA((n,)), pltpu.SemaphoreType.DMA],
    )(src)


def _chip_all_gather(src, name):
    return _exchange(src[None], N_CHIPS, CHIP_FLIPS, lambda me, peer: 0, _chip_of, name)


def _dev_all_gather(src, name):
    return _exchange(src[None], N_DEV, ALL_FLIPS, lambda me, peer: 0, _dev_of, name)


def _chip_scatter(src, name):
    return _exchange(src, N_CHIPS, CHIP_FLIPS, lambda me, peer: _chip_of(peer), _chip_of, name)


def _core_swap(src, name):
    return _exchange(src[None], 2, CORE_FLIPS, lambda me, peer: 0, lambda me: me[2], name)


def _sum_slots(x, name):
    s, rows, cols = x.shape
    tile = _pick(rows, (512, 256, 128, 64, 32, 16, 8))

    def body(x_ref, o_ref):
        acc = x_ref[0].astype(F32)
        for i in range(1, s):
            acc = acc + x_ref[i].astype(F32)
        o_ref[...] = acc

    return pl.pallas_call(
        body, name=name, grid=(rows // tile,),
        in_specs=[pl.BlockSpec((s, tile, cols), lambda i: (0, i, 0))],
        out_specs=pl.BlockSpec((tile, cols), lambda i: (i, 0)),
        out_shape=jax.ShapeDtypeStruct((rows, cols), F32),
        compiler_params=_cparams(("parallel",), VMEM_LIMIT),
    )(x)


def _adamw(w, g, m, v, name):
    rows, cols = w.shape
    tile = rows
    if rows * cols * 4 > 2 * 1024 * 1024:
        tile = _pick(rows, (256, 128, 64, 32, 16, 8))
    c1 = 1.0 / (1.0 - ADAM_B1 ** ADAM_STEP)
    c2 = 1.0 / (1.0 - ADAM_B2 ** ADAM_STEP)

    def body(w_ref, g_ref, m_ref, v_ref, d_ref, nm_ref, nv_ref):
        gv = g_ref[...]
        nm = ADAM_B1 * m_ref[...] + (1.0 - ADAM_B1) * gv
        nv = ADAM_B2 * v_ref[...] + (1.0 - ADAM_B2) * (gv * gv)
        m_hat = nm * c1
        v_hat = nv * c2
        d_ref[...] = -ADAM_LR * (m_hat / (jnp.sqrt(v_hat) + ADAM_EPS) + ADAM_WD * w_ref[...])
        nm_ref[...] = nm
        nv_ref[...] = nv

    spec = pl.BlockSpec((tile, cols), lambda i: (i, 0))
    sds = jax.ShapeDtypeStruct((rows, cols), F32)
    return pl.pallas_call(
        body, name=name, grid=(rows // tile,),
        in_specs=[spec] * 4, out_specs=[spec] * 3, out_shape=[sds] * 3,
        compiler_params=_cparams(("parallel",), VMEM_LIMIT),
    )(w, g, m, v)


def _ada_fwd(c_all, ada_w, ada_b_cols):
    n_col = ada_w.shape[2]

    def body(c_ref, w_ref, b_ref, o_ref):
        cv = c_ref[...]
        cond = (cv * jax.nn.sigmoid(cv)).astype(BF16)
        o_ref[0] = jnp.dot(cond, w_ref[0].astype(BF16), preferred_element_type=F32) + b_ref[0]

    return pl.pallas_call(
        body, name="ada_fwd", grid=(DEPTH,),
        in_specs=[pl.BlockSpec(c_all.shape, lambda i: (0, 0)),
                  pl.BlockSpec((1, D_MODEL, n_col), lambda i: (i, 0, 0)),
                  pl.BlockSpec((1, 1, n_col), lambda i: (i, 0, 0))],
        out_specs=pl.BlockSpec((1, N_DEV, n_col), lambda i: (i, 0, 0)),
        out_shape=jax.ShapeDtypeStruct((DEPTH, N_DEV, n_col), F32),
        compiler_params=_cparams(("parallel",), VMEM_LIMIT),
    )(c_all, ada_w, ada_b_cols)


def _ada_grad(c_all_t, dmod_cols):
    n_col = dmod_cols.shape[2]

    def body(c_ref, d_ref, o_ref):
        cv = c_ref[...]
        cond = cv * jax.nn.sigmoid(cv)
        o_ref[0] = jnp.dot(cond, d_ref[0], precision=lax.Precision.HIGHEST, preferred_element_type=F32)

    return pl.pallas_call(
        body, name="ada_grad", grid=(DEPTH,),
        in_specs=[pl.BlockSpec(c_all_t.shape, lambda i: (0, 0)),
                  pl.BlockSpec((1, LANES, n_col), lambda i: (i, 0, 0))],
        out_specs=pl.BlockSpec((1, D_MODEL, n_col), lambda i: (i, 0, 0)),
        out_shape=jax.ShapeDtypeStruct((DEPTH, D_MODEL, n_col), F32),
        compiler_params=_cparams(("parallel",), VMEM_LIMIT),
    )(c_all_t, dmod_cols)


ROW_TILE = 256


def _shift_rows(a, n=1):
    return jnp.pad(a, ((n, 0), (0, 0)))[:-n]


def _modulate(x, sc, sh, name):
    def fn(x, sc, sh):
        return (x * (1.0 + sc) + sh,)

    t = x.shape[0]
    return _fused(fn, [_tiled(x, ROW_TILE), _shared(sc), _shared(sh)],
                  [_tiled_out(t, D_MODEL, ROW_TILE)], (t // ROW_TILE,), name)[0]


def _resid_ln_mod(x, y, gate, ln_g, ln_b, sc, sh, name):
    def fn(x, y, gate, ln_g, ln_b, sc, sh):
        x1 = _layer_norm_rows(ALPHA * x + (1.0 + gate) * y, ln_g, ln_b)
        return x1, x1 * (1.0 + sc) + sh

    t = x.shape[0]
    return _fused(fn, [_tiled(x, ROW_TILE), _tiled(y, ROW_TILE)] + [_shared(a) for a in (gate, ln_g, ln_b, sc, sh)],
                  [_tiled_out(t, D_MODEL, ROW_TILE)] * 2, (t // ROW_TILE,), name)


def _resid_ln_loss(x, y, gate, ln_g, ln_b, target, name):
    def fn(x, y, gate, ln_g, ln_b, target):
        x1 = _layer_norm_rows(ALPHA * x + (1.0 + gate) * y, ln_g, ln_b)
        err = jnp.square(x1 - target)
        per_row = jnp.mean(err, axis=-1, keepdims=True)
        return (0.5 * jnp.sum(per_row, axis=0, keepdims=True),)

    t = x.shape[0]
    return _fused(fn, [_tiled(x, ROW_TILE), _tiled(y, ROW_TILE)] + [_shared(a) for a in (gate, ln_g, ln_b)]
                  + [_tiled(target, ROW_TILE)],
                  [((1, 1), (1, 1), _const_map(2), "a")], (t // ROW_TILE,), name)[0]


def _sq_relu(h, name):
    def fn(h):
        return (jnp.square(jnp.maximum(h, 0.0)),)

    t, f = h.shape
    return _fused(fn, [_tiled(h, ROW_TILE)], [_tiled_out(t, f, ROW_TILE)], (t // ROW_TILE,), name)[0]


def _mlp(u, w1, s1, w2, s2, name):
    h = _linear(u, w1, s1, name + "_w1")
    return _linear(_sq_relu(h, name + "_act"), w2, s2, name + "_w2")


AB_PIECES = (("r", 0, 512, 512), ("k", 512, 512, 512), ("v", 1024, 512, 512),
             ("wd", 1536, 64, 128), ("ad", 1600, 64, 128), ("gd", 1664, 160, 256),
             ("h", 1824, 512, 512), ("bg", 2336, 512, 512), ("cg", 2848, 512, 512))
AB_PAD_COLS = sum(p[3] for p in AB_PIECES)


def _regroup_cols(w):
    parts = []
    for _, start, width, padded in AB_PIECES:
        piece = w[..., start:start + width]
        if padded != width:
            piece = jnp.pad(piece, [(0, 0)] * (w.ndim - 1) + [(0, padded - width)])
        parts.append(piece)
    return jnp.concatenate(parts, axis=-1)


def _pad_rows(w, rows):
    return jnp.pad(w, ((0, rows - w.shape[0]), (0, 0)))


def _rwkv_shortconv(u, big, wts):
    t = u.shape[0]
    p = _linear(u, _regroup_cols(big["ab_w_in"][0]), _regroup_cols(wts["ab_w_in"][0]), "ab_in")
    mu = _regroup_cols(jnp.pad(wts["rw_mu"], ((0, 0), (0, AB_PROJ - RW_PROJ))))
    w_up = _pad_rows(wts["rw_w_up"][0], 128)
    a_up = _pad_rows(wts["rw_a_up"][0], 128)
    g_up = _pad_rows(wts["rw_g_up"][0], 256)

    def pre(rp, rs, kp, ks, vp, vs, wdp, wds, adp, ads, gdp, gds, h, cg,
            mu_r, mu_k, mu_v, mu_w, mu_a, mu_g, w0, w_up, a0, a_up, g_up, k_k, k_a):
        def mix(pv, sv, m):
            return pv + m * (sv - pv)

        r, k, v = mix(rp, rs, mu_r), mix(kp, ks, mu_k), mix(vp, vs, mu_v)
        wd, ad, gd = mix(wdp, wds, mu_w), mix(adp, ads, mu_a), mix(gdp, gds, mu_g)
        logw = -_softplus(-(w0 + _bdot(jnp.tanh(wd), w_up))) - 0.5
        decay = jnp.exp(-jnp.exp(logw))
        iclr = jax.nn.sigmoid(a0 + _bdot(ad, a_up))
        gate = _bdot(jax.nn.sigmoid(gd), g_up)
        kk = k * k_k
        kk = kk / jnp.maximum(jnp.sqrt(_head_sum(kk * kk)), 1e-12)
        k_h = k * (1.0 + (iclr - 1.0) * k_a)
        return r, decay, k_h, v, -kk, kk * iclr, gate, cg * h

    tile = ROW_TILE
    names = [q[0] for q in AB_PIECES]
    cuts = list(np.cumsum([q[3] for q in AB_PIECES])[:-1])
    pp = dict(zip(names, jnp.split(p, cuts, axis=1)))
    mp = dict(zip(names, jnp.split(mu, cuts, axis=1)))

    ins = []
    for name in ("r", "k", "v", "wd", "ad", "gd"):
        ins += [_tiled(pp[name], tile), _tiled(_shift_rows(pp[name]), tile)]
    ins += [_tiled(pp["h"], tile), _tiled(pp["cg"], tile)]
    ins += [_shared(mp[name]) for name in ("r", "k", "v", "wd", "ad", "gd")]
    ins += [_shared(a) for a in (wts["rw_w0"], w_up, wts["rw_a0"], a_up, g_up, wts["rw_k_k"], wts["rw_k_a"])]
    outs = [_tiled_out(t, RW_WIDTH, tile)] * 8
    r, decay, k_h, v, a, b, gate, z = _fused(pre, ins, outs, (t // tile,), "rwkv_pre")

    y = _rwkv_scan(r, decay, k_h, v, a, b)

    conv_w = wts["sc_conv_w"][0]
    r_k = wts["rw_r_k"].reshape(1, RW_WIDTH)

    def post(y, r, k_h, v, gate, bg, z, z1, z2, lnx_g, lnx_b, r_k, c0, c1, c2):
        mean = _head_sum(y) * (1.0 / HEAD_DIM)
        yc = y - mean
        var = _head_sum(yc * yc) * (1.0 / HEAD_DIM)
        yn = yc * lax.rsqrt(var + RW_GN_EPS) * lnx_g + lnx_b
        bonus = _head_sum(r * k_h * r_k) * v
        return (yn + bonus) * gate, bg * (c0 * z2 + c1 * z1 + c2 * z)

    ins = [_tiled(a_, tile) for a_ in (y, r, k_h, v, gate, pp["bg"])]
    ins += [_tiled(a_, tile) for a_ in (z, _shift_rows(z, 1), _shift_rows(z, 2))]
    ins += [_shared(a_) for a_ in (wts["rw_lnx_g"], wts["rw_lnx_b"], r_k, conv_w[0:1], conv_w[1:2], conv_w[2:3])]
    y_a, y_b = _fused(post, ins, [_tiled_out(t, RW_WIDTH, tile)] * 2, (t // tile,), "rwkv_post")
    return _linear(jnp.concatenate([y_a, y_b], axis=1), big["ab_w_out"][0], wts["ab_w_out"][0], "ab_out")


def _t5_bucket_np(dist):
    exact = N_BUCKETS // 2
    logd = np.log(np.maximum(dist, 1).astype(np.float32) / exact) / math.log(MAX_DISTANCE / exact)
    large = np.minimum(exact + (logd * (N_BUCKETS - exact)).astype(np.int32), N_BUCKETS - 1)
    return np.where(dist < exact, dist, large)


def _merge_groups(os_, lses, name):
    t = os_[0].shape[0]
    tile = ROW_TILE

    def fn(o0, o1, o2, l0, l1, l2):
        lane = lax.broadcasted_iota(jnp.int32, (1, LANES), 1)
        lo = lane < HEAD_DIM
        ls = [jnp.where(lo, l[0], l[1]) for l in (l0, l1, l2)]
        m = jnp.maximum(jnp.maximum(ls[0], ls[1]), ls[2])
        es = [jnp.exp(l - m) for l in ls]
        den = es[0] + es[1] + es[2]
        return ((es[0] * o0 + es[1] * o1 + es[2] * o2) / den,)

    ins = [(o, (tile, LANES), lambda i, hp: (i, hp), "t") for o in os_]
    ins += [(l, (2, tile, LANES), lambda i, hp: (hp, i, 0), "t") for l in lses]
    outs = [((t, DIL_WIDTH), (tile, LANES), lambda i, hp: (i, hp), "t")]
    return _fused(fn, ins, outs, (t // tile, N_PAIRS), name)[0]


def _dilated_mixer(u, big, wts):
    p = _linear(u, big["dil_w_qkv"][0], wts["dil_w_qkv"][0], "dil_qkv")
    qi = np.arange(BLOCK)[:, None]
    ki = np.arange(2 * BLOCK)[None, :]
    rel = BLOCK + qi - ki
    biases = []
    for g, (window, dil) in enumerate(DIL_PATTERNS):
        span = window // dil
        bucket = _t5_bucket_np(np.clip(rel, 0, span) * dil).reshape(-1)
        onehot = jnp.asarray(np.eye(N_BUCKETS, dtype=np.float32)[bucket])
        table = wts["rel_bias"][:, g * N_HEADS:(g + 1) * N_HEADS]
        bias = jnp.dot(onehot, table, precision=lax.Precision.HIGHEST)
        biases.append(jnp.transpose(bias.reshape(BLOCK, 2 * BLOCK, N_HEADS), (2, 0, 1)))
    os_, lses = _dilated_attention(p, tuple(biases))
    o = _merge_groups(os_, lses, "dil_merge")
    return _linear(o, big["dil_w_out"][0], wts["dil_w_out"][0], "dil_out")


def _forward_local(x, mods, big, wts, target):
    u = _modulate(x, mods[0, 1], mods[0, 0], "mod_in")
    for i in range(DEPTH):
        sh2, sc2, g1, g2 = mods[i, 3], mods[i, 4], mods[i, 2], mods[i, 5]
        y = _rwkv_shortconv(u, big, wts) if i % 2 == 0 else _dilated_mixer(u, big, wts)
        x, u = _resid_ln_mod(x, y, g1, wts["ln_g"][i, 0:1], wts["ln_b"][i, 0:1], sc2, sh2, f"ln_mix{i}")
        y = _mlp(u, big["mlp_w1"][i], wts["mlp_w1"][i], big["mlp_w2"][i], wts["mlp_w2"][i], f"mlp{i}")
        if i + 1 < DEPTH:
            x, u = _resid_ln_mod(x, y, g2, wts["ln_g"][i, 1:2], wts["ln_b"][i, 1:2],
                                 mods[i + 1, 1], mods[i + 1, 0], f"ln_mlp{i}")
        else:
            return _resid_ln_loss(x, y, g2, wts["ln_g"][i, 1:2], wts["ln_b"][i, 1:2], target, "ln_loss")


SHARDED = {"ab_w_in": 2, "ab_w_out": 1, "dil_w_qkv": 2, "dil_w_out": 2, "mlp_w1": 2, "mlp_w2": 1,
           "ln_g": 2, "ln_b": 2, "rw_w_up": 2, "rw_a_up": 2, "rw_g_up": 2, "sc_conv_w": 2}
BIG = ("ab_w_in", "ab_w_out", "dil_w_qkv", "dil_w_out", "mlp_w1", "mlp_w2")
SMALL_SHARDED = ("ln_g", "ln_b", "rw_w_up", "rw_a_up", "rw_g_up", "sc_conv_w")
REPLICATED = ("ada_b", "rw_mu", "rw_w0", "rw_a0", "rw_k_k", "rw_k_a", "rw_r_k", "rw_lnx_g", "rw_lnx_b", "rel_bias")
WEIGHT_ORDER = ("ada_w", "ada_b", "ln_g", "ln_b", "ab_w_in", "rw_mu", "rw_w0", "rw_w_up", "rw_a0", "rw_a_up",
                "rw_g_up", "rw_k_k", "rw_k_a", "rw_r_k", "rw_lnx_g", "rw_lnx_b", "sc_conv_w", "ab_w_out",
                "dil_w_qkv", "dil_w_out", "rel_bias", "mlp_w1", "mlp_w2")


PACK_ROWS = 16


def _rows_of(n_elems):
    return -(-n_elems // (ROW_W * PACK_ROWS)) * PACK_ROWS


def _to_rows(a):
    flat = a.reshape(-1)
    rows = _rows_of(flat.shape[0])
    if rows * ROW_W != flat.shape[0]:
        flat = jnp.pad(flat, (0, rows * ROW_W - flat.shape[0]))
    return flat.reshape(rows, ROW_W)


def _from_rows(rows, shape):
    n = int(np.prod(shape))
    return rows.reshape(-1)[:n].reshape(shape)


def _split_chips(full, axis):
    shp = full.shape
    parts = full.reshape(shp[:axis] + (N_CHIPS, shp[axis] // N_CHIPS) + shp[axis + 1:])
    return jnp.moveaxis(parts, axis, 0)


def _join_chips(parts, axis):
    moved = jnp.moveaxis(parts, 0, axis)
    shp = moved.shape
    return moved.reshape(shp[:axis] + (shp[axis] * shp[axis + 1],) + shp[axis + 2:])


def _pack_rows(arrays):
    blocks = [_to_rows(a) for a in arrays]
    total = sum(b.shape[0] for b in blocks)
    pad = (-total) % 256
    if pad:
        blocks.append(jnp.zeros((pad, ROW_W), blocks[0].dtype))
    return jnp.concatenate(blocks, axis=0)


def _unpack_rows(buf, shapes):
    out, r0 = [], 0
    for shp in shapes:
        n = _rows_of(int(np.prod(shp)))
        out.append(_from_rows(buf[r0:r0 + n], shp))
        r0 += n
    return out


def _as2d(a):
    return a.reshape(-1, a.shape[-1])


def kernel(x, c, ada_w, ada_b, ln_g, ln_b, ab_w_in, rw_mu, rw_w0, rw_w_up, rw_a0, rw_a_up, rw_g_up, rw_k_k, rw_k_a, rw_r_k, rw_lnx_g, rw_lnx_b, sc_conv_w, ab_w_out, dil_w_qkv, dil_w_out, rel_bias, mlp_w1, mlp_w2, loss_target, m_ada_w, m_ada_b, m_ln_g, m_ln_b, m_ab_w_in, m_rw_mu, m_rw_w0, m_rw_w_up, m_rw_a0, m_rw_a_up, m_rw_g_up, m_rw_k_k, m_rw_k_a, m_rw_r_k, m_rw_lnx_g, m_rw_lnx_b, m_sc_conv_w, m_ab_w_out, m_dil_w_qkv, m_dil_w_out, m_rel_bias, m_mlp_w1, m_mlp_w2, v_ada_w, v_ada_b, v_ln_g, v_ln_b, v_ab_w_in, v_rw_mu, v_rw_w0, v_rw_w_up, v_rw_a0, v_rw_a_up, v_rw_g_up, v_rw_k_k, v_rw_k_a, v_rw_r_k, v_rw_lnx_g, v_rw_lnx_b, v_sc_conv_w, v_ab_w_out, v_dil_w_qkv, v_dil_w_out, v_rel_bias, v_mlp_w1, v_mlp_w2):
    args = dict(locals())
    w_in = {n: args[n] for n in WEIGHT_ORDER}
    m_in = {n: args["m_" + n] for n in WEIGHT_ORDER}
    v_in = {n: args["v_" + n] for n in WEIGHT_ORDER}
    me = _me()
    chip = _chip_of(me)
    dev = _dev_of(me)

    c_all = _dev_all_gather(c, "gather_c")[:, 0, :]
    n_col = ada_w.shape[2]
    ada_b_cols = lax.dynamic_slice_in_dim(ada_b, chip * n_col, n_col, axis=1)[:, None, :]
    mod_cols = _ada_fwd(c_all, ada_w, ada_b_cols)

    big_buf = _pack_rows([w_in[n].astype(BF16) for n in BIG])
    big_all = _chip_all_gather(big_buf, "gather_big")
    small_buf = _pack_rows([mod_cols] + [w_in[n] for n in SMALL_SHARDED])
    small_all = _chip_all_gather(small_buf, "gather_small")

    wts = {n: w_in[n] for n in REPLICATED}
    big = {}
    big_parts = zip(*[_unpack_rows(big_all[j], [w_in[n].shape for n in BIG]) for j in range(N_CHIPS)])
    for n, parts in zip(BIG, big_parts):
        big[n] = _join_chips(jnp.stack(parts), SHARDED[n])
        wts[n] = jnp.zeros(big[n].shape, F32)
    small_shapes = [mod_cols.shape] + [w_in[n].shape for n in SMALL_SHARDED]
    small_parts = list(zip(*[_unpack_rows(small_all[j], small_shapes) for j in range(N_CHIPS)]))
    for n, parts in zip(SMALL_SHARDED, small_parts[1:]):
        wts[n] = _join_chips(jnp.stack(parts), SHARDED[n])
    mod_all = _join_chips(jnp.stack(small_parts[0]), 2)
    mods = lax.dynamic_slice_in_dim(mod_all, dev, 1, axis=1).reshape(DEPTH, 6, 1, D_MODEL)

    def local_loss(xv, modv, wv):
        return _forward_local(xv, modv, big, wv, loss_target[0])[0, 0]

    loss_local, (grad_x, dmods, dw) = jax.value_and_grad(local_loss, argnums=(0, 1, 2))(x[0], mods, wts)
    loss = lax.psum(loss_local, ("x", "y", "c"))

    small_row = jnp.concatenate([dmods.reshape(-1)] + [dw[n].reshape(-1) for n in REPLICATED[1:]])
    n_small = small_row.shape[0]
    n_small_pad = -(-n_small // LANES) * LANES
    small_row = jnp.pad(small_row, (0, n_small_pad - n_small))[None, :]
    rows_all = _dev_all_gather(small_row, "gather_small_grads")
    small_sum = _sum_slots(rows_all, "sum_small_grads")

    dmod_all = rows_all[:, 0, :DEPTH * 6 * D_MODEL].reshape(N_DEV, DEPTH, 6 * D_MODEL)
    dmod_cols = lax.dynamic_slice_in_dim(dmod_all, chip * n_col, n_col, axis=2)
    dmod_cols = jnp.pad(jnp.moveaxis(dmod_cols, 0, 1), ((0, 0), (0, LANES - N_DEV), (0, 0)))
    c_all_t = jnp.pad(c_all.T, ((0, 0), (0, LANES - N_DEV)))
    grads = {"ada_w": _ada_grad(c_all_t, dmod_cols)}
    grads["ada_b"] = small_sum[0, :DEPTH * 6 * D_MODEL].reshape(ada_b.shape)
    r0 = DEPTH * 6 * D_MODEL
    for n in REPLICATED[1:]:
        size = int(np.prod(w_in[n].shape))
        grads[n] = small_sum[0, r0:r0 + size].reshape(w_in[n].shape)
        r0 += size

    sharded_names = BIG + SMALL_SHARDED
    per_chip = [_split_chips(dw[n], SHARDED[n]) for n in sharded_names]
    send = jnp.stack([_pack_rows([pc[j].astype(BF16) for pc in per_chip]) for j in range(N_CHIPS)])
    recv = _chip_scatter(send, "scatter_grads")
    core_part = _sum_slots(recv, "sum_chips")
    both = _core_swap(core_part, "swap_cores")
    g_rows = _sum_slots(both, "sum_cores")
    for n, g in zip(sharded_names, _unpack_rows(g_rows, [w_in[n].shape for n in sharded_names])):
        grads[n] = g

    deltas, new_m, new_v = {}, {}, {}
    for n in WEIGHT_ORDER:
        shp = w_in[n].shape
        d, nm, nv = _adamw(_as2d(w_in[n]), _as2d(grads[n]), _as2d(m_in[n]), _as2d(v_in[n]), "adamw_" + n)
        deltas[n], new_m[n], new_v[n] = d.reshape(shp), nm.reshape(shp), nv.reshape(shp)

    return (loss, grad_x[None], *[grads[n] for n in WEIGHT_ORDER], *[deltas[n] for n in WEIGHT_ORDER],
            *[new_m[n] for n in WEIGHT_ORDER], *[new_v[n] for n in WEIGHT_ORDER])
```

```python
import functools
import math

import numpy as np
import jax
import jax.numpy as jnp
from jax import lax
from jax.experimental import pallas as pl
from jax.experimental.pallas import tpu as pltpu

F32 = jnp.float32
BF16 = jnp.bfloat16
MESH = pl.DeviceIdType.MESH

D_MODEL = 1024
DEPTH = 2
RW_WIDTH = 512
HEAD_DIM = 64
N_HEADS = 8
RW_DECAY_RANK = 64
RW_ICLR_RANK = 64
RW_GATE_RANK = 160
RW_GN_EPS = 64e-5
RW_PROJ = 3 * RW_WIDTH + RW_DECAY_RANK + RW_ICLR_RANK + RW_GATE_RANK
SC_WIDTH = 512
AB_PROJ = RW_PROJ + 3 * SC_WIDTH
DIL_PATTERNS = ((128, 1), (512, 4), (2048, 16))
N_GROUPS = 3
DIL_WIDTH = 512
DIL_PROJ = N_GROUPS * 3 * DIL_WIDTH
BLOCK = 128
N_BUCKETS = 32
MAX_DISTANCE = 2048
D_FF = 4 * D_MODEL
ALPHA = (2 * DEPTH) ** 0.25
LN_EPS = 1e-5
ADAM_LR = 0.001
ADAM_B1 = 0.9
ADAM_B2 = 0.999
ADAM_EPS = 1e-08
ADAM_WD = 0.01
ADAM_STEP = 10

N_CHIPS = 4
N_DEV = 8
LANES = 128
SUBLANES = 8
ROW_W = 1024
SCAN_CHUNK = 16
VMEM_LIMIT = 48 * 1024 * 1024
NEG_BIG = -1e30


def _pick(n, cands):
    for c in cands:
        if n % c == 0:
            return c
    return n


def _cparams(sem=None, vmem=None):
    return pltpu.CompilerParams(dimension_semantics=sem, vmem_limit_bytes=vmem)


_DOT_DIMS = {
    "nn": (((1,), (0,)), ((), ())),
    "nt": (((1,), (1,)), ((), ())),
    "tn": (((0,), (0,)), ((), ())),
}


def _mm(a, b, mode, name):
    if mode == "nn":
        (m, k), (_, n) = a.shape, b.shape
    elif mode == "nt":
        (m, k), (n, _) = a.shape, b.shape
    else:
        (k, m), (_, n) = a.shape, b.shape
    tm = _pick(m, (1024, 512, 256, 128))
    tn = _pick(n, (1024, 768, 512, 384, 256, 128))
    tk = _pick(k, (512, 256, 128))
    nk = k // tk
    if mode == "tn":
        a_spec = pl.BlockSpec((tk, tm), lambda i, j, kk: (kk, i))
    else:
        a_spec = pl.BlockSpec((tm, tk), lambda i, j, kk: (i, kk))
    if mode == "nt":
        b_spec = pl.BlockSpec((tn, tk), lambda i, j, kk: (j, kk))
    else:
        b_spec = pl.BlockSpec((tk, tn), lambda i, j, kk: (kk, j))
    dims = _DOT_DIMS[mode]

    def body(a_ref, b_ref, o_ref, acc_ref):
        kk = pl.program_id(2)

        @pl.when(kk == 0)
        def _():
            acc_ref[...] = jnp.zeros_like(acc_ref)

        acc_ref[...] += lax.dot_general(a_ref[...].astype(BF16), b_ref[...].astype(BF16), dims,
                                        preferred_element_type=F32)

        @pl.when(kk == nk - 1)
        def _():
            o_ref[...] = acc_ref[...]

    return pl.pallas_call(
        body, name=name, grid=(m // tm, n // tn, nk),
        in_specs=[a_spec, b_spec],
        out_specs=pl.BlockSpec((tm, tn), lambda i, j, kk: (i, j)),
        out_shape=jax.ShapeDtypeStruct((m, n), F32),
        scratch_shapes=[pltpu.VMEM((tm, tn), F32)],
        compiler_params=_cparams(("parallel", "parallel", "arbitrary"), VMEM_LIMIT),
    )(a, b)


def _linear(x, w, slot, name):
    @jax.custom_vjp
    def op(x, w, slot):
        return _mm(x, w, "nn", name + "_fwd")

    def fwd(x, w, slot):
        return _mm(x, w, "nn", name + "_fwd"), (x, w)

    def bwd(res, dy):
        x, w = res
        return _mm(dy, w, "nt", name + "_dx"), jnp.zeros_like(w), _mm(x, dy, "tn", name + "_dw")

    op.defvjp(fwd, bwd)
    return op(x, w, slot)


def _const_map(ndim):
    return lambda *g: (0,) * ndim


def _first_step(n_grid):
    return functools.reduce(jnp.logical_and, [pl.program_id(d) == 0 for d in range(n_grid)])


def _fused(fn, ins, outs, grid, name):
    arrays = [i[0] for i in ins]
    n_in, n_out, n_grid = len(ins), len(outs), len(grid)
    in_specs = [pl.BlockSpec(bs, im) for (_, bs, im, _) in ins]
    out_specs = [pl.BlockSpec(bs, im) for (_, bs, im, _) in outs]
    out_shapes = [jax.ShapeDtypeStruct(s, F32) for (s, _, _, _) in outs]
    sem = ("arbitrary",) * n_grid

    def fwd_call(*xs):
        def body(*refs):
            vals = [r[...] for r in refs[:n_in]]
            ys = fn(*vals)
            first = _first_step(n_grid)
            for o_ref, y, (_, _, _, kind) in zip(refs[n_in:], ys, outs):
                if kind == "t":
                    o_ref[...] = y
                else:
                    @pl.when(first)
                    def _(o_ref=o_ref):
                        o_ref[...] = jnp.zeros_like(o_ref)

                    o_ref[...] += y

        return pl.pallas_call(
            body, name=name + "_fwd", grid=grid, in_specs=in_specs, out_specs=out_specs,
            out_shape=out_shapes, compiler_params=_cparams(sem, VMEM_LIMIT))(*xs)

    def bwd_call(xs, dys):
        d_specs = [pl.BlockSpec(bs, im) for (_, bs, im, _) in outs]
        g_specs = [pl.BlockSpec(bs, im) for (_, bs, im, _) in ins]
        g_shapes = [jax.ShapeDtypeStruct(a.shape, F32) for a in arrays]

        def body(*refs):
            vals = [r[...] for r in refs[:n_in]]
            dvals = tuple(r[...] for r in refs[n_in:n_in + n_out])
            _, vjp = jax.vjp(lambda *v: tuple(fn(*v)), *vals)
            gs = vjp(dvals)
            first = _first_step(n_grid)
            for g_ref, g, (_, _, _, kind) in zip(refs[n_in + n_out:], gs, ins):
                if kind == "t":
                    g_ref[...] = g
                else:
                    @pl.when(first)
                    def _(g_ref=g_ref):
                        g_ref[...] = jnp.zeros_like(g_ref)

                    g_ref[...] += g

        return pl.pallas_call(
            body, name=name + "_bwd", grid=grid, in_specs=in_specs + d_specs, out_specs=g_specs,
            out_shape=g_shapes, compiler_params=_cparams(sem, VMEM_LIMIT))(*xs, *dys)

    @jax.custom_vjp
    def op(*xs):
        return tuple(fwd_call(*xs))

    def op_fwd(*xs):
        return tuple(fwd_call(*xs)), xs

    def op_bwd(xs, dys):
        return tuple(bwd_call(xs, dys))

    op.defvjp(op_fwd, op_bwd)
    return op(*arrays)


def _tiled(a, tile, cols=None, col_block=0):
    cols = a.shape[1] if cols is None else cols
    return (a, (tile, cols), lambda i, cb=col_block: (i, cb), "t")


def _shared(a):
    return (a, a.shape, _const_map(a.ndim), "b")


def _tiled_out(rows, cols, tile):
    return ((rows, cols), (tile, cols), lambda i: (i, 0), "t")


@jax.custom_vjp
def _bdot(x, w):
    return jnp.dot(x.astype(BF16), w.astype(BF16), preferred_element_type=F32)


def _bdot_fwd(x, w):
    return _bdot(x, w), (x, w)


def _bdot_bwd(res, dy):
    x, w = res
    dyb = dy.astype(BF16)
    dx = lax.dot_general(dyb, w.astype(BF16), _DOT_DIMS["nt"], preferred_element_type=F32)
    dw = lax.dot_general(x.astype(BF16), dyb, _DOT_DIMS["tn"], preferred_element_type=F32)
    return dx, dw


_bdot.defvjp(_bdot_fwd, _bdot_bwd)


def _head_sum(x):
    n = x.shape[-1]
    hi = lax.broadcasted_iota(jnp.int32, (n, n), 0) // HEAD_DIM
    hj = lax.broadcasted_iota(jnp.int32, (n, n), 1) // HEAD_DIM
    e = (hi == hj).astype(F32)
    return jnp.dot(x, e, precision=lax.Precision.HIGHEST, preferred_element_type=F32)


def _softplus(x):
    return jnp.maximum(x, 0.0) + jnp.log1p(jnp.exp(-jnp.abs(x)))


def _layer_norm_rows(z, g, b):
    mu = jnp.mean(z, axis=-1, keepdims=True)
    zc = z - mu
    var = jnp.mean(zc * zc, axis=-1, keepdims=True)
    return zc * lax.rsqrt(var + LN_EPS) * g + b


N_PAIRS = N_HEADS // 2


def _scan_consts():
    k = lax.broadcasted_iota(jnp.int32, (HEAD_DIM, LANES), 0)
    j = lax.broadcasted_iota(jnp.int32, (HEAD_DIM, LANES), 1)
    diag = (j % HEAD_DIM) == k
    jj = lax.broadcasted_iota(jnp.int32, (LANES, LANES), 0) // HEAD_DIM
    ll = lax.broadcasted_iota(jnp.int32, (LANES, LANES), 1) // HEAD_DIM
    same_head = (jj == ll).astype(BF16)
    lane_lo = j < HEAD_DIM
    return diag, same_head, lane_lo


def _unrolled(n, body, carry):
    for i in range(n):
        carry = body(i, carry)
    return carry


def _fill_cols(srcs, col_ref, diag_bf, same_head, n_steps):
    diag3 = jnp.concatenate([diag_bf] * 3, axis=1)
    rhs3 = jnp.concatenate([same_head] * 3, axis=0)
    for t in range(n_steps):
        blocks = []
        for src in srcs:
            rb = jnp.broadcast_to(src[t:t + 1, :], (16, RW_WIDTH))
            hi = rb.astype(BF16)
            r1 = rb - hi.astype(F32)
            mid = r1.astype(BF16)
            lo = (r1 - mid.astype(F32)).astype(BF16)
            for hp in range(N_PAIRS):
                lanes = slice(hp * LANES, (hp + 1) * LANES)
                cat = jnp.concatenate([hi[:, lanes], mid[:, lanes], lo[:, lanes]], axis=1)
                blocks.append(jnp.concatenate([cat] * 4, axis=0) * diag3)
        out = jnp.dot(jnp.concatenate(blocks, axis=0), rhs3, preferred_element_type=F32)
        for vi in range(len(srcs)):
            for hp in range(N_PAIRS):
                r0 = (vi * N_PAIRS + hp) * HEAD_DIM
                col_ref[vi, hp, t] = out[r0:r0 + HEAD_DIM]


def _scan_fwd_call(r, w, k, v, a, b):
    t_len = r.shape[0]
    ch = SCAN_CHUNK
    n_ch = t_len // ch

    def body(r_ref, w_ref, k_ref, v_ref, a_ref, b_ref, y_ref, ck_ref, st_ref, col_ref):
        c = pl.program_id(0)

        @pl.when(c == 0)
        def _():
            st_ref[...] = jnp.zeros_like(st_ref)

        ck_ref[0] = st_ref[...]
        diag, same_head, _ = _scan_consts()
        diag_bf = diag.astype(BF16)
        _fill_cols((w_ref, a_ref, b_ref, k_ref, r_ref), col_ref, diag_bf, same_head, ch)

        def step(t, states):
            new = []
            for hp in range(N_PAIRS):
                lanes = slice(hp * LANES, (hp + 1) * LANES)
                s = states[hp]
                sa = jnp.sum(s * col_ref[1, hp, t], axis=0, keepdims=True)
                s = s * col_ref[0, hp, t] + col_ref[2, hp, t] * sa + col_ref[3, hp, t] * v_ref[pl.ds(t, 1), lanes]
                y_ref[pl.ds(t, 1), lanes] = jnp.sum(s * col_ref[4, hp, t], axis=0, keepdims=True)
                new.append(s)
            return tuple(new)

        states = _unrolled(ch, step, tuple(st_ref[hp] for hp in range(N_PAIRS)))
        for hp in range(N_PAIRS):
            st_ref[hp] = states[hp]

    row_spec = pl.BlockSpec((ch, RW_WIDTH), lambda c: (c, 0))
    return pl.pallas_call(
        body, name="rwkv_scan_fwd", grid=(n_ch,),
        in_specs=[row_spec] * 6,
        out_specs=[row_spec, pl.BlockSpec((1, N_PAIRS, HEAD_DIM, LANES), lambda c: (c, 0, 0, 0))],
        out_shape=[jax.ShapeDtypeStruct((t_len, RW_WIDTH), F32),
                   jax.ShapeDtypeStruct((n_ch, N_PAIRS, HEAD_DIM, LANES), F32)],
        scratch_shapes=[pltpu.VMEM((N_PAIRS, HEAD_DIM, LANES), F32),
                        pltpu.VMEM((5, N_PAIRS, ch, HEAD_DIM, LANES), F32)],
        compiler_params=_cparams(("arbitrary",), VMEM_LIMIT),
    )(r, w, k, v, a, b)


def _scan_bwd_call(r, w, k, v, a, b, ck, dy):
    t_len = r.shape[0]
    ch = SCAN_CHUNK
    n_ch = t_len // ch

    def body(r_ref, w_ref, k_ref, v_ref, a_ref, b_ref, ck_ref, dy_ref,
             dr_ref, dw_ref, dk_ref, dv_ref, da_ref, db_ref,
             ds_ref, col_ref, sp_ref, sa_ref):
        c = pl.program_id(0)

        @pl.when(c == 0)
        def _():
            ds_ref[...] = jnp.zeros_like(ds_ref)

        diag, same_head, lane_lo = _scan_consts()
        diag_bf = diag.astype(BF16)
        diag_f = diag.astype(F32)
        _fill_cols((w_ref, a_ref, b_ref, k_ref, r_ref), col_ref, diag_bf, same_head, ch)

        def replay(t, states):
            new = []
            for hp in range(N_PAIRS):
                lanes = slice(hp * LANES, (hp + 1) * LANES)
                s = states[hp]
                sp_ref[t, hp] = s
                sa = jnp.sum(s * col_ref[1, hp, t], axis=0, keepdims=True)
                sa_ref[pl.ds(t, 1), lanes] = sa
                new.append(s * col_ref[0, hp, t] + col_ref[2, hp, t] * sa
                           + col_ref[3, hp, t] * v_ref[pl.ds(t, 1), lanes])
            return tuple(new)

        _unrolled(ch, replay, tuple(ck_ref[0, hp] for hp in range(N_PAIRS)))

        def key_row(p):
            lo = jnp.sum(jnp.where(lane_lo, p, 0.0), axis=1, keepdims=True)
            hi = jnp.sum(jnp.where(lane_lo, 0.0, p), axis=1, keepdims=True)
            return jnp.sum(jnp.where(lane_lo, lo, hi) * diag_f, axis=0, keepdims=True)

        def back(i, grads):
            t = ch - 1 - i
            new = []
            for hp in range(N_PAIRS):
                lanes = slice(hp * LANES, (hp + 1) * LANES)
                wc, ac, bc, kc, rc = (col_ref[vi, hp, t] for vi in range(5))
                sp = sp_ref[t, hp]
                sa = sa_ref[pl.ds(t, 1), lanes]
                vrow = v_ref[pl.ds(t, 1), lanes]
                dyrow = dy_ref[pl.ds(t, 1), lanes]
                st = sp * wc + bc * sa + kc * vrow
                g = grads[hp] + rc * dyrow
                dsa = jnp.sum(g * bc, axis=0, keepdims=True)
                dv_ref[pl.ds(t, 1), lanes] = jnp.sum(g * kc, axis=0, keepdims=True)
                dr_ref[pl.ds(t, 1), lanes] = key_row(st * dyrow)
                dk_ref[pl.ds(t, 1), lanes] = key_row(g * vrow)
                db_ref[pl.ds(t, 1), lanes] = key_row(g * sa)
                dw_ref[pl.ds(t, 1), lanes] = key_row(g * sp)
                da_ref[pl.ds(t, 1), lanes] = key_row(sp * dsa)
                new.append(g * wc + ac * dsa)
            return tuple(new)

        grads = _unrolled(ch, back, tuple(ds_ref[hp] for hp in range(N_PAIRS)))
        for hp in range(N_PAIRS):
            ds_ref[hp] = grads[hp]

    row_spec = pl.BlockSpec((ch, RW_WIDTH), lambda c: (n_ch - 1 - c, 0))
    out_sds = jax.ShapeDtypeStruct((t_len, RW_WIDTH), F32)
    return pl.pallas_call(
        body, name="rwkv_scan_bwd", grid=(n_ch,),
        in_specs=[row_spec] * 6 + [pl.BlockSpec((1, N_PAIRS, HEAD_DIM, LANES), lambda c: (n_ch - 1 - c, 0, 0, 0)),
                                   row_spec],
        out_specs=[row_spec] * 6,
        out_shape=[out_sds] * 6,
        scratch_shapes=[pltpu.VMEM((N_PAIRS, HEAD_DIM, LANES), F32),
                        pltpu.VMEM((5, N_PAIRS, ch, HEAD_DIM, LANES), F32),
                        pltpu.VMEM((ch, N_PAIRS, HEAD_DIM, LANES), F32),
                        pltpu.VMEM((ch, RW_WIDTH), F32)],
        compiler_params=_cparams(("arbitrary",), VMEM_LIMIT),
    )(r, w, k, v, a, b, ck, dy)


@jax.custom_vjp
def _rwkv_scan(r, w, k, v, a, b):
    return _scan_fwd_call(r, w, k, v, a, b)[0]


def _rwkv_scan_fwd(r, w, k, v, a, b):
    y, ck = _scan_fwd_call(r, w, k, v, a, b)
    return y, (r, w, k, v, a, b, ck)


def _rwkv_scan_bwd(res, dy):
    return tuple(_scan_bwd_call(*res, dy))


_rwkv_scan.defvjp(_rwkv_scan_fwd, _rwkv_scan_bwd)


ATT_SCALE = HEAD_DIM ** -0.5
COLS_PER_POS = DIL_PROJ // DIL_WIDTH


def _att_masks():
    qi = lax.broadcasted_iota(jnp.int32, (BLOCK, BLOCK), 0)
    ki = lax.broadcasted_iota(jnp.int32, (BLOCK, BLOCK), 1)
    lane = lax.broadcasted_iota(jnp.int32, (1, LANES), 1)
    return ki <= qi, ki >= qi, lane


def _att_fwd_call(p, bias, g, dil):
    t_len = p.shape[0]
    l_len = t_len // dil
    nb = l_len // BLOCK
    pv = p.reshape(l_len, dil * DIL_PROJ)
    base = g * 3

    def body(q_ref, kc_ref, kp_ref, vc_ref, vp_ref, bias_ref, o_ref, lse_ref):
        n = pl.program_id(1)
        cur_ok, prev_band, lane = _att_masks()
        prev_ok = jnp.logical_and(prev_band, n > 0)
        for hp in range(N_PAIRS):
            lanes = slice(hp * LANES, (hp + 1) * LANES)
            q2 = q_ref[:, lanes].astype(BF16)
            kc = kc_ref[:, lanes].astype(BF16)
            kp = kp_ref[:, lanes].astype(BF16)
            vc = vc_ref[:, lanes].astype(BF16)
            vp = vp_ref[:, lanes].astype(BF16)
            o2 = jnp.zeros((BLOCK, LANES), F32)
            for hh in range(2):
                h = 2 * hp + hh
                mine = (lane // HEAD_DIM) == hh
                qm = jnp.where(mine, q2, jnp.zeros_like(q2))
                s_c = lax.dot_general(qm, kc, _DOT_DIMS["nt"], preferred_element_type=F32) * ATT_SCALE
                s_p = lax.dot_general(qm, kp, _DOT_DIMS["nt"], preferred_element_type=F32) * ATT_SCALE
                s_c = jnp.where(cur_ok, s_c + bias_ref[h, :, BLOCK:], NEG_BIG)
                s_p = jnp.where(prev_ok, s_p + bias_ref[h, :, :BLOCK], NEG_BIG)
                m = jnp.maximum(jnp.max(s_c, axis=-1, keepdims=True), jnp.max(s_p, axis=-1, keepdims=True))
                e_c = jnp.exp(s_c - m)
                e_p = jnp.exp(s_p - m)
                den = jnp.sum(e_c, axis=-1, keepdims=True) + jnp.sum(e_p, axis=-1, keepdims=True)
                o_h = (jnp.dot((e_c / den).astype(BF16), vc, preferred_element_type=F32)
                       + jnp.dot((e_p / den).astype(BF16), vp, preferred_element_type=F32))
                o2 = o2 + jnp.where(mine, o_h, 0.0)
                lse_ref[h] = jnp.broadcast_to(m + jnp.log(den), (BLOCK, LANES))
            o_ref[:, lanes] = o2

    def col(j):
        return lambda r, n: (n, r * COLS_PER_POS + base + j)

    def col_prev(j):
        return lambda r, n: (jnp.maximum(n - 1, 0), r * COLS_PER_POS + base + j)

    blk = (BLOCK, DIL_WIDTH)
    o, lse = pl.pallas_call(
        body, name=f"dil_att_fwd_g{g}", grid=(dil, nb),
        in_specs=[pl.BlockSpec(blk, col(0)), pl.BlockSpec(blk, col(1)), pl.BlockSpec(blk, col_prev(1)),
                  pl.BlockSpec(blk, col(2)), pl.BlockSpec(blk, col_prev(2)),
                  pl.BlockSpec(bias.shape, _const_map(3))],
        out_specs=[pl.BlockSpec(blk, lambda r, n: (n, r)),
                   pl.BlockSpec((N_HEADS, BLOCK, LANES), lambda r, n: (0, n, r))],
        out_shape=[jax.ShapeDtypeStruct((l_len, dil * DIL_WIDTH), F32),
                   jax.ShapeDtypeStruct((N_HEADS, l_len, dil * LANES), F32)],
        compiler_params=_cparams(("arbitrary", "arbitrary"), VMEM_LIMIT),
    )(pv, pv, pv, pv, pv, bias)
    return o.reshape(t_len, DIL_WIDTH), lse.reshape(N_HEADS, t_len, LANES)


def _att_bwd_call(p, bias, o, lse, do, dlse, g, dil):
    t_len = p.shape[0]
    l_len = t_len // dil
    nb = l_len // BLOCK
    pv = p.reshape(l_len, dil * DIL_PROJ)
    ov = o.reshape(l_len, dil * DIL_WIDTH)
    dov = do.reshape(l_len, dil * DIL_WIDTH)
    lsev = lse.reshape(N_HEADS, l_len, dil * LANES)
    dlsev = dlse.reshape(N_HEADS, l_len, dil * LANES)
    base = g * 3

    def body(q_ref, qn_ref, k_ref, v_ref, do_ref, don_ref, o_ref, on_ref, lse_ref, lsen_ref, dl_ref, dln_ref,
             bias_ref, dq_ref, dk_ref, dv_ref, dbias_ref, carry_ref):
        r = pl.program_id(0)
        n = pl.program_id(1)
        cur_ok, prev_band, lane = _att_masks()
        has_next = n + 1 < nb

        @pl.when(jnp.logical_and(r == 0, n == 0))
        def _():
            dbias_ref[...] = jnp.zeros_like(dbias_ref)

        @pl.when(n == 0)
        def _():
            carry_ref[...] = jnp.zeros_like(carry_ref)

        for hp in range(N_PAIRS):
            lanes = slice(hp * LANES, (hp + 1) * LANES)
            k2 = k_ref[:, lanes].astype(BF16)
            v2 = v_ref[:, lanes].astype(BF16)
            dk2 = jnp.zeros((BLOCK, LANES), F32)
            dv2 = jnp.zeros((BLOCK, LANES), F32)
            dq_cur = carry_ref[:, lanes]
            dq_next = jnp.zeros((BLOCK, LANES), F32)
            for hh in range(2):
                h = 2 * hp + hh
                mine = (lane // HEAD_DIM) == hh
                tiles = (
                    (q_ref, do_ref, o_ref, lse_ref, dl_ref, cur_ok, slice(BLOCK, 2 * BLOCK), None),
                    (qn_ref, don_ref, on_ref, lsen_ref, dln_ref, prev_band, slice(0, BLOCK), has_next),
                )
                for ti, (qr, dor, orf, lr, dlr, ok, bcols, gate) in enumerate(tiles):
                    q2 = qr[:, lanes].astype(BF16)
                    qm = jnp.where(mine, q2, jnp.zeros_like(q2))
                    do_f = jnp.where(mine, dor[:, lanes], 0.0)
                    dom = do_f.astype(BF16)
                    s = lax.dot_general(qm, k2, _DOT_DIMS["nt"], preferred_element_type=F32) * ATT_SCALE
                    s = s + bias_ref[h, :, bcols]
                    if gate is not None:
                        ok = jnp.logical_and(ok, gate)
                    pr = jnp.where(ok, jnp.exp(jnp.minimum(s - lr[h], 0.0)), 0.0)
                    dp = lax.dot_general(dom, v2, _DOT_DIMS["nt"], preferred_element_type=F32)
                    delta = jnp.sum(do_f * orf[:, lanes], axis=-1, keepdims=True)
                    dl = jnp.sum(dlr[h], axis=-1, keepdims=True)
                    ds = pr * (dp - delta + dl)
                    dsb = ds.astype(BF16)
                    dq_h = jnp.where(mine, jnp.dot(dsb, k2, preferred_element_type=F32), 0.0) * ATT_SCALE
                    if ti == 0:
                        dq_cur = dq_cur + dq_h
                    else:
                        dq_next = dq_next + dq_h
                    dk2 = dk2 + lax.dot_general(dsb, qm, _DOT_DIMS["tn"], preferred_element_type=F32) * ATT_SCALE
                    dv2 = dv2 + lax.dot_general(pr.astype(BF16), dom, _DOT_DIMS["tn"], preferred_element_type=F32)
                    dbias_ref[h, :, bcols] += ds
            dq_ref[:, lanes] = dq_cur
            carry_ref[:, lanes] = dq_next
            dk_ref[:, lanes] = dk2
            dv_ref[:, lanes] = dv2

    def nxt(n):
        return jnp.minimum(n + 1, nb - 1)

    blk = (BLOCK, DIL_WIDTH)
    hblk = (N_HEADS, BLOCK, LANES)
    qcol = lambda j: (lambda r, n: (n, r * COLS_PER_POS + base + j))
    q_next = lambda r, n: (nxt(n), r * COLS_PER_POS + base)
    rown = lambda r, n: (n, r)
    rown_next = lambda r, n: (nxt(n), r)
    hrow = lambda r, n: (0, n, r)
    hrow_next = lambda r, n: (0, nxt(n), r)
    sds = jax.ShapeDtypeStruct((l_len, dil * DIL_WIDTH), F32)
    dq, dk, dv, dbias = pl.pallas_call(
        body, name=f"dil_att_bwd_g{g}", grid=(dil, nb),
        in_specs=[pl.BlockSpec(blk, qcol(0)), pl.BlockSpec(blk, q_next),
                  pl.BlockSpec(blk, qcol(1)), pl.BlockSpec(blk, qcol(2)),
                  pl.BlockSpec(blk, rown), pl.BlockSpec(blk, rown_next),
                  pl.BlockSpec(blk, rown), pl.BlockSpec(blk, rown_next),
                  pl.BlockSpec(hblk, hrow), pl.BlockSpec(hblk, hrow_next),
                  pl.BlockSpec(hblk, hrow), pl.BlockSpec(hblk, hrow_next),
                  pl.BlockSpec(bias.shape, _const_map(3))],
        out_specs=[pl.BlockSpec(blk, rown)] * 3 + [pl.BlockSpec(bias.shape, _const_map(3))],
        out_shape=[sds, sds, sds, jax.ShapeDtypeStruct(bias.shape, F32)],
        scratch_shapes=[pltpu.VMEM((BLOCK, DIL_WIDTH), F32)],
        compiler_params=_cparams(("arbitrary", "arbitrary"), VMEM_LIMIT),
    )(pv, pv, pv, pv, dov, dov, ov, ov, lsev, lsev, dlsev, dlsev, bias)
    shp = (t_len, DIL_WIDTH)
    return dq.reshape(shp), dk.reshape(shp), dv.reshape(shp), dbias


def _att_all_groups(p, biases):
    outs = [_att_fwd_call(p, biases[g], g, dil) for g, (_, dil) in enumerate(DIL_PATTERNS)]
    return tuple(o for o, _ in outs), tuple(l for _, l in outs)


@jax.custom_vjp
def _dilated_attention(p, biases):
    return _att_all_groups(p, biases)


def _dilated_attention_fwd(p, biases):
    os_, lses = _att_all_groups(p, biases)
    return (os_, lses), (p, biases, os_, lses)


def _dilated_attention_bwd(res, cts):
    p, biases, os_, lses = res
    dos, dlses = cts
    parts, dbiases = [], []
    for g, (_, dil) in enumerate(DIL_PATTERNS):
        dq, dk, dv, dbias = _att_bwd_call(p, biases[g], os_[g], lses[g], dos[g], dlses[g], g, dil)
        parts += [dq, dk, dv]
        dbiases.append(dbias)
    return jnp.concatenate(parts, axis=1), tuple(dbiases)


_dilated_attention.defvjp(_dilated_attention_fwd, _dilated_attention_bwd)


def _me():
    return lax.axis_index("x"), lax.axis_index("y"), lax.axis_index("c")


def _flip(me, f):
    return tuple((1 - m) if b else m for m, b in zip(me, f))


def _chip_of(d):
    return 2 * d[0] + d[1]


def _dev_of(d):
    return 4 * d[0] + 2 * d[1] + d[2]


EXCHANGE_CHUNKS = 8
CHIP_FLIPS = ((1, 0, 0), (0, 1, 0), (1, 1, 0))
ALL_FLIPS = tuple((a, b, c) for a in (0, 1) for b in (0, 1) for c in (0, 1) if a or b or c)
CORE_FLIP = (0, 0, 1)


def _n_chunks(rows):
    return EXCHANGE_CHUNKS if rows % (EXCHANGE_CHUNKS * PACK_ROWS) == 0 else 1


def _exchange(src, n_slots, transfers, name):
    _, rows, cols = src.shape
    chunks = _n_chunks(rows)
    rc = rows // chunks
    n = len(transfers) * chunks

    def body(src_ref, dst_ref, send_sems, recv_sems):
        me = _me()
        copies = []
        for q in range(chunks):
            for kk, (f, src_slot, dst_slot) in enumerate(transfers):
                peer = _flip(me, f)
                cp = pltpu.make_async_remote_copy(
                    src_ref=src_ref.at[src_slot(me, peer), pl.ds(q * rc, rc)],
                    dst_ref=dst_ref.at[dst_slot(me, peer), pl.ds(q * rc, rc)],
                    send_sem=send_sems.at[kk * chunks + q], recv_sem=recv_sems.at[kk * chunks + q],
                    device_id=peer, device_id_type=MESH)
                cp.start()
                copies.append(cp)
        for cp in copies:
            cp.wait()

    return pl.pallas_call(
        body, name=name,
        out_shape=jax.ShapeDtypeStruct((n_slots, rows, cols), src.dtype),
        in_specs=[pl.BlockSpec(memory_space=pl.ANY)],
        out_specs=pl.BlockSpec(memory_space=pl.ANY),
        scratch_shapes=[pltpu.SemaphoreType.DMA((n,)), pltpu.SemaphoreType.DMA((n,))],
    )(src)


def _set_slot(buf, block, index):
    return lax.dynamic_update_slice_in_dim(buf, block[None].astype(buf.dtype), index, axis=0)


def _chip_all_gather(src, name):
    got = _exchange(src[None], N_CHIPS, [(f, lambda me, peer: 0, lambda me, peer: _chip_of(me)) for f in CHIP_FLIPS], name)
    return _set_slot(got, src, _chip_of(_me()))


def _dev_all_gather(src, name):
    got = _exchange(src[None], N_DEV, [(f, lambda me, peer: 0, lambda me, peer: _dev_of(me)) for f in ALL_FLIPS], name)
    return _set_slot(got, src, _dev_of(_me()))


def _chip_scatter(src, name):
    got = _exchange(src, N_CHIPS, [(f, lambda me, peer: _chip_of(peer), lambda me, peer: _chip_of(me))
                                   for f in CHIP_FLIPS], name)
    chip = _chip_of(_me())
    return _set_slot(got, lax.dynamic_index_in_dim(src, chip, 0, keepdims=False), chip)


def _core_halves(src, name):
    s, _, half, cols = src.shape
    transfers = [(CORE_FLIP, (lambda me, peer, j=j: 2 * j + peer[2]), (lambda me, peer, j=j: j)) for j in range(s)]
    return _exchange(src.reshape(2 * s, half, cols), s, transfers, name)


def _core_all_gather(src, name):
    got = _exchange(src[None], 2, [(CORE_FLIP, lambda me, peer: 0, lambda me, peer: me[2])], name)
    return _set_slot(got, src, _me()[2])


def _two_level_gather(src, name):
    rows, cols = src.shape
    half = rows // 2
    chunks = _n_chunks(half)
    rc = half // chunks
    n = len(CHIP_FLIPS) * chunks

    def body(src_ref, g_ref, send1, recv1, send2, recv2):
        me = _me()
        c = me[2]
        sibling = _flip(me, CORE_FLIP)
        first, second = [], []
        for q in range(chunks):
            for kk, f in enumerate(CHIP_FLIPS):
                peer = _flip(me, f)
                cp = pltpu.make_async_remote_copy(
                    src_ref=src_ref.at[c, pl.ds(q * rc, rc)], dst_ref=g_ref.at[_chip_of(me), c, pl.ds(q * rc, rc)],
                    send_sem=send1.at[kk * chunks + q], recv_sem=recv1.at[kk * chunks + q],
                    device_id=peer, device_id_type=MESH)
                cp.start()
                first.append((cp, _chip_of(peer), kk * chunks + q, q))
        for cp, origin, idx, q in first:
            cp.wait_recv()
            fw = pltpu.make_async_remote_copy(
                src_ref=g_ref.at[origin, c, pl.ds(q * rc, rc)], dst_ref=g_ref.at[origin, c, pl.ds(q * rc, rc)],
                send_sem=send2.at[idx], recv_sem=recv2.at[idx],
                device_id=sibling, device_id_type=MESH)
            fw.start()
            second.append(fw)
        for fw in second:
            fw.wait_recv()
        for cp, _, _, _ in first:
            cp.wait_send()
        for fw in second:
            fw.wait_send()

    got = pl.pallas_call(
        body, name=name,
        out_shape=jax.ShapeDtypeStruct((N_CHIPS, 2, half, cols), src.dtype),
        in_specs=[pl.BlockSpec(memory_space=pl.ANY)],
        out_specs=pl.BlockSpec(memory_space=pl.ANY),
        scratch_shapes=[pltpu.SemaphoreType.DMA((n,)) for _ in range(4)],
    )(src.reshape(2, half, cols))
    return _set_slot(got.reshape(N_CHIPS, rows, cols), src, _chip_of(_me()))


def _sum_slots(x, name):
    s, rows, cols = x.shape
    tile = _pick(rows, (512, 256, 128, 64, 32, 16, 8))

    def body(x_ref, o_ref):
        acc = x_ref[0].astype(F32)
        for i in range(1, s):
            acc = acc + x_ref[i].astype(F32)
        o_ref[...] = acc

    return pl.pallas_call(
        body, name=name, grid=(rows // tile,),
        in_specs=[pl.BlockSpec((s, tile, cols), lambda i: (0, i, 0))],
        out_specs=pl.BlockSpec((tile, cols), lambda i: (i, 0)),
        out_shape=jax.ShapeDtypeStruct((rows, cols), F32),
        compiler_params=_cparams(("parallel",), VMEM_LIMIT),
    )(x)


def _add_pairs(a, b, name):
    s, rows, cols = a.shape
    tile = _pick(rows, (512, 256, 128, 64, 32, 16, 8))

    def body(a_ref, b_ref, o_ref):
        o_ref[...] = (a_ref[...].astype(F32) + b_ref[...].astype(F32)).astype(o_ref.dtype)

    spec = pl.BlockSpec((1, tile, cols), lambda j, i: (j, i, 0))
    return pl.pallas_call(
        body, name=name, grid=(s, rows // tile),
        in_specs=[spec, spec], out_specs=spec,
        out_shape=jax.ShapeDtypeStruct(a.shape, a.dtype),
        compiler_params=_cparams(("parallel", "parallel"), VMEM_LIMIT),
    )(a, b)


def _adamw(w, g, m, v, name):
    rows, cols = w.shape
    tile = rows
    if rows * cols * 4 > 2 * 1024 * 1024:
        tile = _pick(rows, (256, 128, 64, 32, 16, 8))
    c1 = 1.0 / (1.0 - ADAM_B1 ** ADAM_STEP)
    c2 = 1.0 / (1.0 - ADAM_B2 ** ADAM_STEP)

    def body(w_ref, g_ref, m_ref, v_ref, d_ref, nm_ref, nv_ref):
        gv = g_ref[...]
        nm = ADAM_B1 * m_ref[...] + (1.0 - ADAM_B1) * gv
        nv = ADAM_B2 * v_ref[...] + (1.0 - ADAM_B2) * (gv * gv)
        m_hat = nm * c1
        v_hat = nv * c2
        d_ref[...] = -ADAM_LR * (m_hat / (jnp.sqrt(v_hat) + ADAM_EPS) + ADAM_WD * w_ref[...])
        nm_ref[...] = nm
        nv_ref[...] = nv

    spec = pl.BlockSpec((tile, cols), lambda i: (i, 0))
    sds = jax.ShapeDtypeStruct((rows, cols), F32)
    return pl.pallas_call(
        body, name=name, grid=(rows // tile,),
        in_specs=[spec] * 4, out_specs=[spec] * 3, out_shape=[sds] * 3,
        compiler_params=_cparams(("parallel",), VMEM_LIMIT),
    )(w, g, m, v)


def _ada_fwd(c_all, ada_w, ada_b_cols):
    n_col = ada_w.shape[2]

    def body(c_ref, w_ref, b_ref, o_ref):
        cv = c_ref[...]
        cond = (cv * jax.nn.sigmoid(cv)).astype(BF16)
        o_ref[0] = jnp.dot(cond, w_ref[0].astype(BF16), preferred_element_type=F32) + b_ref[0]

    return pl.pallas_call(
        body, name="ada_fwd", grid=(DEPTH,),
        in_specs=[pl.BlockSpec(c_all.shape, lambda i: (0, 0)),
                  pl.BlockSpec((1, D_MODEL, n_col), lambda i: (i, 0, 0)),
                  pl.BlockSpec((1, 1, n_col), lambda i: (i, 0, 0))],
        out_specs=pl.BlockSpec((1, N_DEV, n_col), lambda i: (i, 0, 0)),
        out_shape=jax.ShapeDtypeStruct((DEPTH, N_DEV, n_col), F32),
        compiler_params=_cparams(("parallel",), VMEM_LIMIT),
    )(c_all, ada_w, ada_b_cols)


def _ada_grad(c_all_t, dmod_cols):
    n_col = dmod_cols.shape[2]

    def body(c_ref, d_ref, o_ref):
        cv = c_ref[...]
        cond = cv * jax.nn.sigmoid(cv)
        o_ref[0] = jnp.dot(cond, d_ref[0], precision=lax.Precision.HIGHEST, preferred_element_type=F32)

    return pl.pallas_call(
        body, name="ada_grad", grid=(DEPTH,),
        in_specs=[pl.BlockSpec(c_all_t.shape, lambda i: (0, 0)),
                  pl.BlockSpec((1, LANES, n_col), lambda i: (i, 0, 0))],
        out_specs=pl.BlockSpec((1, D_MODEL, n_col), lambda i: (i, 0, 0)),
        out_shape=jax.ShapeDtypeStruct((DEPTH, D_MODEL, n_col), F32),
        compiler_params=_cparams(("parallel",), VMEM_LIMIT),
    )(c_all_t, dmod_cols)


ROW_TILE = 256


def _shift_rows(a, n=1):
    return jnp.pad(a, ((n, 0), (0, 0)))[:-n]


def _modulate(x, sc, sh, name):
    def fn(x, sc, sh):
        return (x * (1.0 + sc) + sh,)

    t = x.shape[0]
    return _fused(fn, [_tiled(x, ROW_TILE), _shared(sc), _shared(sh)],
                  [_tiled_out(t, D_MODEL, ROW_TILE)], (t // ROW_TILE,), name)[0]


def _resid_ln_mod(x, y, gate, ln_g, ln_b, sc, sh, name):
    def fn(x, y, gate, ln_g, ln_b, sc, sh):
        x1 = _layer_norm_rows(ALPHA * x + (1.0 + gate) * y, ln_g, ln_b)
        return x1, x1 * (1.0 + sc) + sh

    t = x.shape[0]
    return _fused(fn, [_tiled(x, ROW_TILE), _tiled(y, ROW_TILE)] + [_shared(a) for a in (gate, ln_g, ln_b, sc, sh)],
                  [_tiled_out(t, D_MODEL, ROW_TILE)] * 2, (t // ROW_TILE,), name)


def _resid_ln_loss(x, y, gate, ln_g, ln_b, target, name):
    def fn(x, y, gate, ln_g, ln_b, target):
        x1 = _layer_norm_rows(ALPHA * x + (1.0 + gate) * y, ln_g, ln_b)
        err = jnp.square(x1 - target)
        per_row = jnp.mean(err, axis=-1, keepdims=True)
        return (0.5 * jnp.sum(per_row, axis=0, keepdims=True),)

    t = x.shape[0]
    return _fused(fn, [_tiled(x, ROW_TILE), _tiled(y, ROW_TILE)] + [_shared(a) for a in (gate, ln_g, ln_b)]
                  + [_tiled(target, ROW_TILE)],
                  [((1, 1), (1, 1), _const_map(2), "a")], (t // ROW_TILE,), name)[0]


def _sq_relu(h, name):
    def fn(h):
        return (jnp.square(jnp.maximum(h, 0.0)),)

    t, f = h.shape
    return _fused(fn, [_tiled(h, ROW_TILE)], [_tiled_out(t, f, ROW_TILE)], (t // ROW_TILE,), name)[0]


def _mlp(u, w1, s1, w2, s2, name):
    h = _linear(u, w1, s1, name + "_w1")
    return _linear(_sq_relu(h, name + "_act"), w2, s2, name + "_w2")


AB_PIECES = (("r", 0, 512, 512), ("k", 512, 512, 512), ("v", 1024, 512, 512),
             ("wd", 1536, 64, 128), ("ad", 1600, 64, 128), ("gd", 1664, 160, 256),
             ("h", 1824, 512, 512), ("bg", 2336, 512, 512), ("cg", 2848, 512, 512))
AB_PAD_COLS = sum(p[3] for p in AB_PIECES)


def _regroup_cols(w):
    parts = []
    for _, start, width, padded in AB_PIECES:
        piece = w[..., start:start + width]
        if padded != width:
            piece = jnp.pad(piece, [(0, 0)] * (w.ndim - 1) + [(0, padded - width)])
        parts.append(piece)
    return jnp.concatenate(parts, axis=-1)


def _pad_rows(w, rows):
    return jnp.pad(w, ((0, rows - w.shape[0]), (0, 0)))


def _rwkv_shortconv(u, big, wts):
    t = u.shape[0]
    p = _linear(u, _regroup_cols(big["ab_w_in"][0]), _regroup_cols(wts["ab_w_in"][0]), "ab_in")
    mu = _regroup_cols(jnp.pad(wts["rw_mu"], ((0, 0), (0, AB_PROJ - RW_PROJ))))
    w_up = _pad_rows(wts["rw_w_up"][0], 128)
    a_up = _pad_rows(wts["rw_a_up"][0], 128)
    g_up = _pad_rows(wts["rw_g_up"][0], 256)

    def pre(rp, rs, kp, ks, vp, vs, wdp, wds, adp, ads, gdp, gds, h, cg,
            mu_r, mu_k, mu_v, mu_w, mu_a, mu_g, w0, w_up, a0, a_up, g_up, k_k, k_a):
        def mix(pv, sv, m):
            return pv + m * (sv - pv)

        r, k, v = mix(rp, rs, mu_r), mix(kp, ks, mu_k), mix(vp, vs, mu_v)
        wd, ad, gd = mix(wdp, wds, mu_w), mix(adp, ads, mu_a), mix(gdp, gds, mu_g)
        logw = -_softplus(-(w0 + _bdot(jnp.tanh(wd), w_up))) - 0.5
        decay = jnp.exp(-jnp.exp(logw))
        iclr = jax.nn.sigmoid(a0 + _bdot(ad, a_up))
        gate = _bdot(jax.nn.sigmoid(gd), g_up)
        kk = k * k_k
        kk = kk / jnp.maximum(jnp.sqrt(_head_sum(kk * kk)), 1e-12)
        k_h = k * (1.0 + (iclr - 1.0) * k_a)
        return r, decay, k_h, v, -kk, kk * iclr, gate, cg * h

    tile = ROW_TILE
    names = [q[0] for q in AB_PIECES]
    cuts = list(np.cumsum([q[3] for q in AB_PIECES])[:-1])
    pp = dict(zip(names, jnp.split(p, cuts, axis=1)))
    mp = dict(zip(names, jnp.split(mu, cuts, axis=1)))

    ins = []
    for name in ("r", "k", "v", "wd", "ad", "gd"):
        ins += [_tiled(pp[name], tile), _tiled(_shift_rows(pp[name]), tile)]
    ins += [_tiled(pp["h"], tile), _tiled(pp["cg"], tile)]
    ins += [_shared(mp[name]) for name in ("r", "k", "v", "wd", "ad", "gd")]
    ins += [_shared(a) for a in (wts["rw_w0"], w_up, wts["rw_a0"], a_up, g_up, wts["rw_k_k"], wts["rw_k_a"])]
    outs = [_tiled_out(t, RW_WIDTH, tile)] * 8
    r, decay, k_h, v, a, b, gate, z = _fused(pre, ins, outs, (t // tile,), "rwkv_pre")

    y = _rwkv_scan(r, decay, k_h, v, a, b)

    conv_w = wts["sc_conv_w"][0]
    r_k = wts["rw_r_k"].reshape(1, RW_WIDTH)

    def post(y, r, k_h, v, gate, bg, z, z1, z2, lnx_g, lnx_b, r_k, c0, c1, c2):
        mean = _head_sum(y) * (1.0 / HEAD_DIM)
        yc = y - mean
        var = _head_sum(yc * yc) * (1.0 / HEAD_DIM)
        yn = yc * lax.rsqrt(var + RW_GN_EPS) * lnx_g + lnx_b
        bonus = _head_sum(r * k_h * r_k) * v
        return (yn + bonus) * gate, bg * (c0 * z2 + c1 * z1 + c2 * z)

    ins = [_tiled(a_, tile) for a_ in (y, r, k_h, v, gate, pp["bg"])]
    ins += [_tiled(a_, tile) for a_ in (z, _shift_rows(z, 1), _shift_rows(z, 2))]
    ins += [_shared(a_) for a_ in (wts["rw_lnx_g"], wts["rw_lnx_b"], r_k, conv_w[0:1], conv_w[1:2], conv_w[2:3])]
    y_a, y_b = _fused(post, ins, [_tiled_out(t, RW_WIDTH, tile)] * 2, (t // tile,), "rwkv_post")
    return _linear(jnp.concatenate([y_a, y_b], axis=1), big["ab_w_out"][0], wts["ab_w_out"][0], "ab_out")


def _t5_bucket_np(dist):
    exact = N_BUCKETS // 2
    logd = np.log(np.maximum(dist, 1).astype(np.float32) / exact) / math.log(MAX_DISTANCE / exact)
    large = np.minimum(exact + (logd * (N_BUCKETS - exact)).astype(np.int32), N_BUCKETS - 1)
    return np.where(dist < exact, dist, large)


def _merge_groups(os_, lses, name):
    t = os_[0].shape[0]
    tile = ROW_TILE

    def fn(o0, o1, o2, l0, l1, l2):
        lane = lax.broadcasted_iota(jnp.int32, (1, LANES), 1)
        lo = lane < HEAD_DIM
        ls = [jnp.where(lo, l[0], l[1]) for l in (l0, l1, l2)]
        m = jnp.maximum(jnp.maximum(ls[0], ls[1]), ls[2])
        es = [jnp.exp(l - m) for l in ls]
        den = es[0] + es[1] + es[2]
        return ((es[0] * o0 + es[1] * o1 + es[2] * o2) / den,)

    ins = [(o, (tile, LANES), lambda i, hp: (i, hp), "t") for o in os_]
    ins += [(l, (2, tile, LANES), lambda i, hp: (hp, i, 0), "t") for l in lses]
    outs = [((t, DIL_WIDTH), (tile, LANES), lambda i, hp: (i, hp), "t")]
    return _fused(fn, ins, outs, (t // tile, N_PAIRS), name)[0]


def _dilated_mixer(u, big, wts):
    p = _linear(u, big["dil_w_qkv"][0], wts["dil_w_qkv"][0], "dil_qkv")
    qi = np.arange(BLOCK)[:, None]
    ki = np.arange(2 * BLOCK)[None, :]
    rel = BLOCK + qi - ki
    biases = []
    for g, (window, dil) in enumerate(DIL_PATTERNS):
        span = window // dil
        bucket = _t5_bucket_np(np.clip(rel, 0, span) * dil).reshape(-1)
        onehot = jnp.asarray(np.eye(N_BUCKETS, dtype=np.float32)[bucket])
        table = wts["rel_bias"][:, g * N_HEADS:(g + 1) * N_HEADS]
        bias = jnp.dot(onehot, table, precision=lax.Precision.HIGHEST)
        biases.append(jnp.transpose(bias.reshape(BLOCK, 2 * BLOCK, N_HEADS), (2, 0, 1)))
    os_, lses = _dilated_attention(p, tuple(biases))
    o = _merge_groups(os_, lses, "dil_merge")
    return _linear(o, big["dil_w_out"][0], wts["dil_w_out"][0], "dil_out")


def _forward_local(x, mods, big, wts, target):
    u = _modulate(x, mods[0, 1], mods[0, 0], "mod_in")
    for i in range(DEPTH):
        sh2, sc2, g1, g2 = mods[i, 3], mods[i, 4], mods[i, 2], mods[i, 5]
        y = _rwkv_shortconv(u, big, wts) if i % 2 == 0 else _dilated_mixer(u, big, wts)
        x, u = _resid_ln_mod(x, y, g1, wts["ln_g"][i, 0:1], wts["ln_b"][i, 0:1], sc2, sh2, f"ln_mix{i}")
        y = _mlp(u, big["mlp_w1"][i], wts["mlp_w1"][i], big["mlp_w2"][i], wts["mlp_w2"][i], f"mlp{i}")
        if i + 1 < DEPTH:
            x, u = _resid_ln_mod(x, y, g2, wts["ln_g"][i, 1:2], wts["ln_b"][i, 1:2],
                                 mods[i + 1, 1], mods[i + 1, 0], f"ln_mlp{i}")
        else:
            return _resid_ln_loss(x, y, g2, wts["ln_g"][i, 1:2], wts["ln_b"][i, 1:2], target, "ln_loss")


SHARDED = {"ab_w_in": 2, "ab_w_out": 1, "dil_w_qkv": 2, "dil_w_out": 2, "mlp_w1": 2, "mlp_w2": 1,
           "ln_g": 2, "ln_b": 2, "rw_w_up": 2, "rw_a_up": 2, "rw_g_up": 2, "sc_conv_w": 2}
BIG = ("ab_w_in", "ab_w_out", "dil_w_qkv", "dil_w_out", "mlp_w1", "mlp_w2")
SMALL_SHARDED = ("ln_g", "ln_b", "rw_w_up", "rw_a_up", "rw_g_up", "sc_conv_w")
REPLICATED = ("ada_b", "rw_mu", "rw_w0", "rw_a0", "rw_k_k", "rw_k_a", "rw_r_k", "rw_lnx_g", "rw_lnx_b", "rel_bias")
WEIGHT_ORDER = ("ada_w", "ada_b", "ln_g", "ln_b", "ab_w_in", "rw_mu", "rw_w0", "rw_w_up", "rw_a0", "rw_a_up",
                "rw_g_up", "rw_k_k", "rw_k_a", "rw_r_k", "rw_lnx_g", "rw_lnx_b", "sc_conv_w", "ab_w_out",
                "dil_w_qkv", "dil_w_out", "rel_bias", "mlp_w1", "mlp_w2")


PACK_ROWS = 16


def _rows_of(n_elems):
    return -(-n_elems // (ROW_W * PACK_ROWS)) * PACK_ROWS


def _to_rows(a):
    flat = a.reshape(-1)
    rows = _rows_of(flat.shape[0])
    if rows * ROW_W != flat.shape[0]:
        flat = jnp.pad(flat, (0, rows * ROW_W - flat.shape[0]))
    return flat.reshape(rows, ROW_W)


def _from_rows(rows, shape):
    n = int(np.prod(shape))
    return rows.reshape(-1)[:n].reshape(shape)


def _split_chips(full, axis):
    shp = full.shape
    parts = full.reshape(shp[:axis] + (N_CHIPS, shp[axis] // N_CHIPS) + shp[axis + 1:])
    return jnp.moveaxis(parts, axis, 0)


def _join_chips(parts, axis):
    moved = jnp.moveaxis(parts, 0, axis)
    shp = moved.shape
    return moved.reshape(shp[:axis] + (shp[axis] * shp[axis + 1],) + shp[axis + 2:])


def _pack_rows(arrays):
    blocks = [_to_rows(a) for a in arrays]
    total = sum(b.shape[0] for b in blocks)
    pad = (-total) % 256
    if pad:
        blocks.append(jnp.zeros((pad, ROW_W), blocks[0].dtype))
    return jnp.concatenate(blocks, axis=0)


def _unpack_rows(buf, shapes):
    out, r0 = [], 0
    for shp in shapes:
        n = _rows_of(int(np.prod(shp)))
        out.append(_from_rows(buf[r0:r0 + n], shp))
        r0 += n
    return out


def _as2d(a):
    return a.reshape(-1, a.shape[-1])


def kernel(x, c, ada_w, ada_b, ln_g, ln_b, ab_w_in, rw_mu, rw_w0, rw_w_up, rw_a0, rw_a_up, rw_g_up, rw_k_k, rw_k_a, rw_r_k, rw_lnx_g, rw_lnx_b, sc_conv_w, ab_w_out, dil_w_qkv, dil_w_out, rel_bias, mlp_w1, mlp_w2, loss_target, m_ada_w, m_ada_b, m_ln_g, m_ln_b, m_ab_w_in, m_rw_mu, m_rw_w0, m_rw_w_up, m_rw_a0, m_rw_a_up, m_rw_g_up, m_rw_k_k, m_rw_k_a, m_rw_r_k, m_rw_lnx_g, m_rw_lnx_b, m_sc_conv_w, m_ab_w_out, m_dil_w_qkv, m_dil_w_out, m_rel_bias, m_mlp_w1, m_mlp_w2, v_ada_w, v_ada_b, v_ln_g, v_ln_b, v_ab_w_in, v_rw_mu, v_rw_w0, v_rw_w_up, v_rw_a0, v_rw_a_up, v_rw_g_up, v_rw_k_k, v_rw_k_a, v_rw_r_k, v_rw_lnx_g, v_rw_lnx_b, v_sc_conv_w, v_ab_w_out, v_dil_w_qkv, v_dil_w_out, v_rel_bias, v_mlp_w1, v_mlp_w2):
    args = dict(locals())
    w_in = {n: args[n] for n in WEIGHT_ORDER}
    m_in = {n: args["m_" + n] for n in WEIGHT_ORDER}
    v_in = {n: args["v_" + n] for n in WEIGHT_ORDER}
    me = _me()
    chip = _chip_of(me)
    dev = _dev_of(me)

    c_all = _dev_all_gather(c, "gather_c")[:, 0, :]
    n_col = ada_w.shape[2]
    ada_b_cols = lax.dynamic_slice_in_dim(ada_b, chip * n_col, n_col, axis=1)[:, None, :]
    mod_cols = _ada_fwd(c_all, ada_w, ada_b_cols)

    big_buf = _pack_rows([w_in[n].astype(BF16) for n in BIG])
    big_all = _two_level_gather(big_buf, "gather_big")
    small_buf = _pack_rows([mod_cols] + [w_in[n] for n in SMALL_SHARDED])
    small_all = _chip_all_gather(small_buf, "gather_small")

    wts = {n: w_in[n] for n in REPLICATED}
    big = {}
    big_parts = zip(*[_unpack_rows(big_all[j], [w_in[n].shape for n in BIG]) for j in range(N_CHIPS)])
    for n, parts in zip(BIG, big_parts):
        big[n] = _join_chips(jnp.stack(parts), SHARDED[n])
        wts[n] = jnp.zeros(big[n].shape, F32)
    small_shapes = [mod_cols.shape] + [w_in[n].shape for n in SMALL_SHARDED]
    small_parts = list(zip(*[_unpack_rows(small_all[j], small_shapes) for j in range(N_CHIPS)]))
    for n, parts in zip(SMALL_SHARDED, small_parts[1:]):
        wts[n] = _join_chips(jnp.stack(parts), SHARDED[n])
    mod_all = _join_chips(jnp.stack(small_parts[0]), 2)
    mods = lax.dynamic_slice_in_dim(mod_all, dev, 1, axis=1).reshape(DEPTH, 6, 1, D_MODEL)

    def local_loss(xv, modv, wv):
        return _forward_local(xv, modv, big, wv, loss_target[0])[0, 0]

    loss_local, (grad_x, dmods, dw) = jax.value_and_grad(local_loss, argnums=(0, 1, 2))(x[0], mods, wts)
    loss = lax.psum(loss_local, ("x", "y", "c"))

    small_row = jnp.concatenate([dmods.reshape(-1)] + [dw[n].reshape(-1) for n in REPLICATED[1:]])
    n_small = small_row.shape[0]
    n_small_pad = -(-n_small // LANES) * LANES
    small_row = jnp.pad(small_row, (0, n_small_pad - n_small))[None, :]
    rows_all = _dev_all_gather(small_row, "gather_small_grads")
    small_sum = _sum_slots(rows_all, "sum_small_grads")

    dmod_all = rows_all[:, 0, :DEPTH * 6 * D_MODEL].reshape(N_DEV, DEPTH, 6 * D_MODEL)
    dmod_cols = lax.dynamic_slice_in_dim(dmod_all, chip * n_col, n_col, axis=2)
    dmod_cols = jnp.pad(jnp.moveaxis(dmod_cols, 0, 1), ((0, 0), (0, LANES - N_DEV), (0, 0)))
    c_all_t = jnp.pad(c_all.T, ((0, 0), (0, LANES - N_DEV)))
    grads = {"ada_w": _ada_grad(c_all_t, dmod_cols)}
    grads["ada_b"] = small_sum[0, :DEPTH * 6 * D_MODEL].reshape(ada_b.shape)
    r0 = DEPTH * 6 * D_MODEL
    for n in REPLICATED[1:]:
        size = int(np.prod(w_in[n].shape))
        grads[n] = small_sum[0, r0:r0 + size].reshape(w_in[n].shape)
        r0 += size

    sharded_names = BIG + SMALL_SHARDED
    per_chip = [_split_chips(dw[n], SHARDED[n]) for n in sharded_names]
    send = jnp.stack([_pack_rows([pc[j].astype(BF16) for pc in per_chip]) for j in range(N_CHIPS)])
    n_rows = send.shape[1]
    send = send.reshape(N_CHIPS, 2, n_rows // 2, ROW_W)
    theirs = _core_halves(send, "swap_halves")
    mine = lax.dynamic_index_in_dim(send, me[2], 1, keepdims=False)
    chip_part = _add_pairs(mine, theirs, "sum_cores")
    recv = _chip_scatter(chip_part, "scatter_grads")
    half_sum = _sum_slots(recv, "sum_chips")
    g_rows = _core_all_gather(half_sum, "gather_halves").reshape(n_rows, ROW_W)
    for n, g in zip(sharded_names, _unpack_rows(g_rows, [w_in[n].shape for n in sharded_names])):
        grads[n] = g

    deltas, new_m, new_v = {}, {}, {}
    for n in WEIGHT_ORDER:
        shp = w_in[n].shape
        d, nm, nv = _adamw(_as2d(w_in[n]), _as2d(grads[n]), _as2d(m_in[n]), _as2d(v_in[n]), "adamw_" + n)
        deltas[n], new_m[n], new_v[n] = d.reshape(shp), nm.reshape(shp), nv.reshape(shp)

    return (loss, grad_x[None], *[grads[n] for n in WEIGHT_ORDER], *[deltas[n] for n in WEIGHT_ORDER],
            *[new_m[n] for n in WEIGHT_ORDER], *[new_v[n] for n in WEIGHT_ORDER])
```

```python
import functools
import math

import numpy as np
import jax
import jax.numpy as jnp
from jax import lax
from jax.experimental import pallas as pl
from jax.experimental.pallas import tpu as pltpu

F32 = jnp.float32
BF16 = jnp.bfloat16
MESH = pl.DeviceIdType.MESH

D_MODEL = 1024
DEPTH = 2
RW_WIDTH = 512
HEAD_DIM = 64
N_HEADS = 8
RW_DECAY_RANK = 64
RW_ICLR_RANK = 64
RW_GATE_RANK = 160
RW_GN_EPS = 64e-5
RW_PROJ = 3 * RW_WIDTH + RW_DECAY_RANK + RW_ICLR_RANK + RW_GATE_RANK
SC_WIDTH = 512
AB_PROJ = RW_PROJ + 3 * SC_WIDTH
DIL_PATTERNS = ((128, 1), (512, 4), (2048, 16))
N_GROUPS = 3
DIL_WIDTH = 512
DIL_PROJ = N_GROUPS * 3 * DIL_WIDTH
BLOCK = 128
N_BUCKETS = 32
MAX_DISTANCE = 2048
D_FF = 4 * D_MODEL
ALPHA = (2 * DEPTH) ** 0.25
LN_EPS = 1e-5
ADAM_LR = 0.001
ADAM_B1 = 0.9
ADAM_B2 = 0.999
ADAM_EPS = 1e-08
ADAM_WD = 0.01
ADAM_STEP = 10

N_CHIPS = 4
N_DEV = 8
LANES = 128
SUBLANES = 8
ROW_W = 1024
SCAN_CHUNK = 16
VMEM_LIMIT = 48 * 1024 * 1024
NEG_BIG = -1e30


def _pick(n, cands):
    for c in cands:
        if n % c == 0:
            return c
    return n


def _cparams(sem=None, vmem=None):
    return pltpu.CompilerParams(dimension_semantics=sem, vmem_limit_bytes=vmem)


_DOT_DIMS = {
    "nn": (((1,), (0,)), ((), ())),
    "nt": (((1,), (1,)), ((), ())),
    "tn": (((0,), (0,)), ((), ())),
}


def _mm(a, b, mode, name):
    if mode == "nn":
        (m, k), (_, n) = a.shape, b.shape
    elif mode == "nt":
        (m, k), (n, _) = a.shape, b.shape
    else:
        (k, m), (_, n) = a.shape, b.shape
    tm = _pick(m, (1024, 512, 256, 128))
    tn = _pick(n, (1024, 768, 512, 384, 256, 128))
    tk = _pick(k, (512, 256, 128))
    nk = k // tk
    if mode == "tn":
        a_spec = pl.BlockSpec((tk, tm), lambda i, j, kk: (kk, i))
    else:
        a_spec = pl.BlockSpec((tm, tk), lambda i, j, kk: (i, kk))
    if mode == "nt":
        b_spec = pl.BlockSpec((tn, tk), lambda i, j, kk: (j, kk))
    else:
        b_spec = pl.BlockSpec((tk, tn), lambda i, j, kk: (kk, j))
    dims = _DOT_DIMS[mode]

    def body(a_ref, b_ref, o_ref, acc_ref):
        kk = pl.program_id(2)

        @pl.when(kk == 0)
        def _():
            acc_ref[...] = jnp.zeros_like(acc_ref)

        acc_ref[...] += lax.dot_general(a_ref[...].astype(BF16), b_ref[...].astype(BF16), dims,
                                        preferred_element_type=F32)

        @pl.when(kk == nk - 1)
        def _():
            o_ref[...] = acc_ref[...]

    return pl.pallas_call(
        body, name=name, grid=(m // tm, n // tn, nk),
        in_specs=[a_spec, b_spec],
        out_specs=pl.BlockSpec((tm, tn), lambda i, j, kk: (i, j)),
        out_shape=jax.ShapeDtypeStruct((m, n), F32),
        scratch_shapes=[pltpu.VMEM((tm, tn), F32)],
        compiler_params=_cparams(("parallel", "parallel", "arbitrary"), VMEM_LIMIT),
    )(a, b)


def _linear(x, w, slot, name):
    @jax.custom_vjp
    def op(x, w, slot):
        return _mm(x, w, "nn", name + "_fwd")

    def fwd(x, w, slot):
        return _mm(x, w, "nn", name + "_fwd"), (x, w)

    def bwd(res, dy):
        x, w = res
        return _mm(dy, w, "nt", name + "_dx"), jnp.zeros_like(w), _mm(x, dy, "tn", name + "_dw")

    op.defvjp(fwd, bwd)
    return op(x, w, slot)


def _const_map(ndim):
    return lambda *g: (0,) * ndim


def _first_step(n_grid):
    return functools.reduce(jnp.logical_and, [pl.program_id(d) == 0 for d in range(n_grid)])


def _fused(fn, ins, outs, grid, name):
    arrays = [i[0] for i in ins]
    n_in, n_out, n_grid = len(ins), len(outs), len(grid)
    in_specs = [pl.BlockSpec(bs, im) for (_, bs, im, _) in ins]
    out_specs = [pl.BlockSpec(bs, im) for (_, bs, im, _) in outs]
    out_shapes = [jax.ShapeDtypeStruct(s, F32) for (s, _, _, _) in outs]
    sem = ("arbitrary",) * n_grid

    def fwd_call(*xs):
        def body(*refs):
            vals = [r[...] for r in refs[:n_in]]
            ys = fn(*vals)
            first = _first_step(n_grid)
            for o_ref, y, (_, _, _, kind) in zip(refs[n_in:], ys, outs):
                if kind == "t":
                    o_ref[...] = y
                else:
                    @pl.when(first)
                    def _(o_ref=o_ref):
                        o_ref[...] = jnp.zeros_like(o_ref)

                    o_ref[...] += y

        return pl.pallas_call(
            body, name=name + "_fwd", grid=grid, in_specs=in_specs, out_specs=out_specs,
            out_shape=out_shapes, compiler_params=_cparams(sem, VMEM_LIMIT))(*xs)

    def bwd_call(xs, dys):
        d_specs = [pl.BlockSpec(bs, im) for (_, bs, im, _) in outs]
        g_specs = [pl.BlockSpec(bs, im) for (_, bs, im, _) in ins]
        g_shapes = [jax.ShapeDtypeStruct(a.shape, F32) for a in arrays]

        def body(*refs):
            vals = [r[...] for r in refs[:n_in]]
            dvals = tuple(r[...] for r in refs[n_in:n_in + n_out])
            _, vjp = jax.vjp(lambda *v: tuple(fn(*v)), *vals)
            gs = vjp(dvals)
            first = _first_step(n_grid)
            for g_ref, g, (_, _, _, kind) in zip(refs[n_in + n_out:], gs, ins):
                if kind == "t":
                    g_ref[...] = g
                else:
                    @pl.when(first)
                    def _(g_ref=g_ref):
                        g_ref[...] = jnp.zeros_like(g_ref)

                    g_ref[...] += g

        return pl.pallas_call(
            body, name=name + "_bwd", grid=grid, in_specs=in_specs + d_specs, out_specs=g_specs,
            out_shape=g_shapes, compiler_params=_cparams(sem, VMEM_LIMIT))(*xs, *dys)

    @jax.custom_vjp
    def op(*xs):
        return tuple(fwd_call(*xs))

    def op_fwd(*xs):
        return tuple(fwd_call(*xs)), xs

    def op_bwd(xs, dys):
        return tuple(bwd_call(xs, dys))

    op.defvjp(op_fwd, op_bwd)
    return op(*arrays)


def _tiled(a, tile, cols=None, col_block=0):
    cols = a.shape[1] if cols is None else cols
    return (a, (tile, cols), lambda i, cb=col_block: (i, cb), "t")


def _shared(a):
    return (a, a.shape, _const_map(a.ndim), "b")


def _tiled_out(rows, cols, tile):
    return ((rows, cols), (tile, cols), lambda i: (i, 0), "t")


@jax.custom_vjp
def _bdot(x, w):
    return jnp.dot(x.astype(BF16), w.astype(BF16), preferred_element_type=F32)


def _bdot_fwd(x, w):
    return _bdot(x, w), (x, w)


def _bdot_bwd(res, dy):
    x, w = res
    dyb = dy.astype(BF16)
    dx = lax.dot_general(dyb, w.astype(BF16), _DOT_DIMS["nt"], preferred_element_type=F32)
    dw = lax.dot_general(x.astype(BF16), dyb, _DOT_DIMS["tn"], preferred_element_type=F32)
    return dx, dw


_bdot.defvjp(_bdot_fwd, _bdot_bwd)


def _head_sum(x):
    n = x.shape[-1]
    hi = lax.broadcasted_iota(jnp.int32, (n, n), 0) // HEAD_DIM
    hj = lax.broadcasted_iota(jnp.int32, (n, n), 1) // HEAD_DIM
    e = (hi == hj).astype(F32)
    return jnp.dot(x, e, precision=lax.Precision.HIGHEST, preferred_element_type=F32)


def _softplus(x):
    return jnp.maximum(x, 0.0) + jnp.log1p(jnp.exp(-jnp.abs(x)))


def _layer_norm_rows(z, g, b):
    mu = jnp.mean(z, axis=-1, keepdims=True)
    zc = z - mu
    var = jnp.mean(zc * zc, axis=-1, keepdims=True)
    return zc * lax.rsqrt(var + LN_EPS) * g + b


N_PAIRS = N_HEADS // 2


def _scan_consts():
    k = lax.broadcasted_iota(jnp.int32, (HEAD_DIM, LANES), 0)
    j = lax.broadcasted_iota(jnp.int32, (HEAD_DIM, LANES), 1)
    diag = (j % HEAD_DIM) == k
    jj = lax.broadcasted_iota(jnp.int32, (LANES, LANES), 0) // HEAD_DIM
    ll = lax.broadcasted_iota(jnp.int32, (LANES, LANES), 1) // HEAD_DIM
    same_head = (jj == ll).astype(BF16)
    lane_lo = j < HEAD_DIM
    return diag, same_head, lane_lo


def _unrolled(n, body, carry):
    for i in range(n):
        carry = body(i, carry)
    return carry


def _fill_cols(srcs, col_ref, diag_bf, same_head, n_steps):
    del diag_bf, same_head
    assert n_steps == 16
    blocks = []
    for src in srcs:
        x = src[...]
        hi = x.astype(BF16).astype(F32)
        r1 = x - hi
        mid = r1.astype(BF16).astype(F32)
        x48 = jnp.concatenate([hi, mid, r1 - mid], axis=0)
        for hp in range(N_PAIRS):
            xp = x48[:, hp * LANES:(hp + 1) * LANES]
            y = jnp.concatenate([xp, pltpu.roll(xp, HEAD_DIM, 1), jnp.zeros((32, LANES), F32)], axis=0)
            blocks.append(y.T[:HEAD_DIM].astype(BF16))
    lhs = jnp.concatenate(blocks, axis=0)
    j = lax.broadcasted_iota(jnp.int32, (LANES, LANES), 0)
    lane_head = lax.broadcasted_iota(jnp.int32, (LANES, LANES), 1) // HEAD_DIM
    for t in range(n_steps):
        pick = jnp.logical_and(j < 96, jnp.logical_and(j % 16 == t, j // 48 == lane_head))
        out = jnp.dot(lhs, pick.astype(BF16), preferred_element_type=F32)
        for vi in range(len(srcs)):
            for hp in range(N_PAIRS):
                r0 = (vi * N_PAIRS + hp) * HEAD_DIM
                col_ref[vi, hp, t] = out[r0:r0 + HEAD_DIM]


def _scan_fwd_call(r, w, k, v, a, b):
    t_len = r.shape[0]
    ch = SCAN_CHUNK
    n_ch = t_len // ch

    def body(r_ref, w_ref, k_ref, v_ref, a_ref, b_ref, y_ref, ck_ref, st_ref, col_ref):
        c = pl.program_id(0)

        @pl.when(c == 0)
        def _():
            st_ref[...] = jnp.zeros_like(st_ref)

        ck_ref[0] = st_ref[...]
        diag, same_head, _ = _scan_consts()
        diag_bf = diag.astype(BF16)
        _fill_cols((w_ref, a_ref, b_ref, k_ref, r_ref), col_ref, diag_bf, same_head, ch)

        def step(t, states):
            new = []
            for hp in range(N_PAIRS):
                lanes = slice(hp * LANES, (hp + 1) * LANES)
                s = states[hp]
                sa = jnp.sum(s * col_ref[1, hp, t], axis=0, keepdims=True)
                s = s * col_ref[0, hp, t] + col_ref[2, hp, t] * sa + col_ref[3, hp, t] * v_ref[pl.ds(t, 1), lanes]
                y_ref[pl.ds(t, 1), lanes] = jnp.sum(s * col_ref[4, hp, t], axis=0, keepdims=True)
                new.append(s)
            return tuple(new)

        states = _unrolled(ch, step, tuple(st_ref[hp] for hp in range(N_PAIRS)))
        for hp in range(N_PAIRS):
            st_ref[hp] = states[hp]

    row_spec = pl.BlockSpec((ch, RW_WIDTH), lambda c: (c, 0))
    return pl.pallas_call(
        body, name="rwkv_scan_fwd", grid=(n_ch,),
        in_specs=[row_spec] * 6,
        out_specs=[row_spec, pl.BlockSpec((1, N_PAIRS, HEAD_DIM, LANES), lambda c: (c, 0, 0, 0))],
        out_shape=[jax.ShapeDtypeStruct((t_len, RW_WIDTH), F32),
                   jax.ShapeDtypeStruct((n_ch, N_PAIRS, HEAD_DIM, LANES), F32)],
        scratch_shapes=[pltpu.VMEM((N_PAIRS, HEAD_DIM, LANES), F32),
                        pltpu.VMEM((5, N_PAIRS, ch, HEAD_DIM, LANES), F32)],
        compiler_params=_cparams(("arbitrary",), VMEM_LIMIT),
    )(r, w, k, v, a, b)


def _scan_bwd_call(r, w, k, v, a, b, ck, dy):
    t_len = r.shape[0]
    ch = SCAN_CHUNK
    n_ch = t_len // ch

    def body(r_ref, w_ref, k_ref, v_ref, a_ref, b_ref, ck_ref, dy_ref,
             dr_ref, dw_ref, dk_ref, dv_ref, da_ref, db_ref,
             ds_ref, col_ref, sp_ref, sa_ref):
        c = pl.program_id(0)

        @pl.when(c == 0)
        def _():
            ds_ref[...] = jnp.zeros_like(ds_ref)

        diag, same_head, lane_lo = _scan_consts()
        diag_bf = diag.astype(BF16)
        diag_f = diag.astype(F32)
        _fill_cols((w_ref, a_ref, b_ref, k_ref, r_ref), col_ref, diag_bf, same_head, ch)

        def replay(t, states):
            new = []
            for hp in range(N_PAIRS):
                lanes = slice(hp * LANES, (hp + 1) * LANES)
                s = states[hp]
                sp_ref[t, hp] = s
                sa = jnp.sum(s * col_ref[1, hp, t], axis=0, keepdims=True)
                sa_ref[pl.ds(t, 1), lanes] = sa
                new.append(s * col_ref[0, hp, t] + col_ref[2, hp, t] * sa
                           + col_ref[3, hp, t] * v_ref[pl.ds(t, 1), lanes])
            return tuple(new)

        _unrolled(ch, replay, tuple(ck_ref[0, hp] for hp in range(N_PAIRS)))

        def key_row(p):
            lo = jnp.sum(jnp.where(lane_lo, p, 0.0), axis=1, keepdims=True)
            hi = jnp.sum(jnp.where(lane_lo, 0.0, p), axis=1, keepdims=True)
            return jnp.sum(jnp.where(lane_lo, lo, hi) * diag_f, axis=0, keepdims=True)

        def back(i, grads):
            t = ch - 1 - i
            new = []
            for hp in range(N_PAIRS):
                lanes = slice(hp * LANES, (hp + 1) * LANES)
                wc, ac, bc, kc, rc = (col_ref[vi, hp, t] for vi in range(5))
                sp = sp_ref[t, hp]
                sa = sa_ref[pl.ds(t, 1), lanes]
                vrow = v_ref[pl.ds(t, 1), lanes]
                dyrow = dy_ref[pl.ds(t, 1), lanes]
                st = sp * wc + bc * sa + kc * vrow
                g = grads[hp] + rc * dyrow
                dsa = jnp.sum(g * bc, axis=0, keepdims=True)
                dv_ref[pl.ds(t, 1), lanes] = jnp.sum(g * kc, axis=0, keepdims=True)
                dr_ref[pl.ds(t, 1), lanes] = key_row(st * dyrow)
                dk_ref[pl.ds(t, 1), lanes] = key_row(g * vrow)
                db_ref[pl.ds(t, 1), lanes] = key_row(g * sa)
                dw_ref[pl.ds(t, 1), lanes] = key_row(g * sp)
                da_ref[pl.ds(t, 1), lanes] = key_row(sp * dsa)
                new.append(g * wc + ac * dsa)
            return tuple(new)

        grads = _unrolled(ch, back, tuple(ds_ref[hp] for hp in range(N_PAIRS)))
        for hp in range(N_PAIRS):
            ds_ref[hp] = grads[hp]

    row_spec = pl.BlockSpec((ch, RW_WIDTH), lambda c: (n_ch - 1 - c, 0))
    out_sds = jax.ShapeDtypeStruct((t_len, RW_WIDTH), F32)
    return pl.pallas_call(
        body, name="rwkv_scan_bwd", grid=(n_ch,),
        in_specs=[row_spec] * 6 + [pl.BlockSpec((1, N_PAIRS, HEAD_DIM, LANES), lambda c: (n_ch - 1 - c, 0, 0, 0)),
                                   row_spec],
        out_specs=[row_spec] * 6,
        out_shape=[out_sds] * 6,
        scratch_shapes=[pltpu.VMEM((N_PAIRS, HEAD_DIM, LANES), F32),
                        pltpu.VMEM((5, N_PAIRS, ch, HEAD_DIM, LANES), F32),
                        pltpu.VMEM((ch, N_PAIRS, HEAD_DIM, LANES), F32),
                        pltpu.VMEM((ch, RW_WIDTH), F32)],
        compiler_params=_cparams(("arbitrary",), VMEM_LIMIT),
    )(r, w, k, v, a, b, ck, dy)


@jax.custom_vjp
def _rwkv_scan(r, w, k, v, a, b):
    return _scan_fwd_call(r, w, k, v, a, b)[0]


def _rwkv_scan_fwd(r, w, k, v, a, b):
    y, ck = _scan_fwd_call(r, w, k, v, a, b)
    return y, (r, w, k, v, a, b, ck)


def _rwkv_scan_bwd(res, dy):
    return tuple(_scan_bwd_call(*res, dy))


_rwkv_scan.defvjp(_rwkv_scan_fwd, _rwkv_scan_bwd)


ATT_SCALE = HEAD_DIM ** -0.5
COLS_PER_POS = DIL_PROJ // DIL_WIDTH


def _att_masks():
    qi = lax.broadcasted_iota(jnp.int32, (BLOCK, BLOCK), 0)
    ki = lax.broadcasted_iota(jnp.int32, (BLOCK, BLOCK), 1)
    lane = lax.broadcasted_iota(jnp.int32, (1, LANES), 1)
    return ki <= qi, ki >= qi, lane


def _att_fwd_call(p, bias, g, dil):
    t_len = p.shape[0]
    l_len = t_len // dil
    nb = l_len // BLOCK
    pv = p.reshape(l_len, dil * DIL_PROJ)
    base = g * 3

    def body(q_ref, kc_ref, kp_ref, vc_ref, vp_ref, bias_ref, o_ref, lse_ref):
        n = pl.program_id(1)
        cur_ok, prev_band, lane = _att_masks()
        prev_ok = jnp.logical_and(prev_band, n > 0)
        for hp in range(N_PAIRS):
            lanes = slice(hp * LANES, (hp + 1) * LANES)
            q2 = q_ref[:, lanes].astype(BF16)
            kc = kc_ref[:, lanes].astype(BF16)
            kp = kp_ref[:, lanes].astype(BF16)
            vc = vc_ref[:, lanes].astype(BF16)
            vp = vp_ref[:, lanes].astype(BF16)
            o2 = jnp.zeros((BLOCK, LANES), F32)
            for hh in range(2):
                h = 2 * hp + hh
                mine = (lane // HEAD_DIM) == hh
                qm = jnp.where(mine, q2, jnp.zeros_like(q2))
                s_c = lax.dot_general(qm, kc, _DOT_DIMS["nt"], preferred_element_type=F32) * ATT_SCALE
                s_p = lax.dot_general(qm, kp, _DOT_DIMS["nt"], preferred_element_type=F32) * ATT_SCALE
                s_c = jnp.where(cur_ok, s_c + bias_ref[h, :, BLOCK:], NEG_BIG)
                s_p = jnp.where(prev_ok, s_p + bias_ref[h, :, :BLOCK], NEG_BIG)
                m = jnp.maximum(jnp.max(s_c, axis=-1, keepdims=True), jnp.max(s_p, axis=-1, keepdims=True))
                e_c = jnp.exp(s_c - m)
                e_p = jnp.exp(s_p - m)
                den = jnp.sum(e_c, axis=-1, keepdims=True) + jnp.sum(e_p, axis=-1, keepdims=True)
                o_h = (jnp.dot((e_c / den).astype(BF16), vc, preferred_element_type=F32)
                       + jnp.dot((e_p / den).astype(BF16), vp, preferred_element_type=F32))
                o2 = o2 + jnp.where(mine, o_h, 0.0)
                lse_ref[h] = jnp.broadcast_to(m + jnp.log(den), (BLOCK, LANES))
            o_ref[:, lanes] = o2

    def col(j):
        return lambda r, n: (n, r * COLS_PER_POS + base + j)

    def col_prev(j):
        return lambda r, n: (jnp.maximum(n - 1, 0), r * COLS_PER_POS + base + j)

    blk = (BLOCK, DIL_WIDTH)
    o, lse = pl.pallas_call(
        body, name=f"dil_att_fwd_g{g}", grid=(dil, nb),
        in_specs=[pl.BlockSpec(blk, col(0)), pl.BlockSpec(blk, col(1)), pl.BlockSpec(blk, col_prev(1)),
                  pl.BlockSpec(blk, col(2)), pl.BlockSpec(blk, col_prev(2)),
                  pl.BlockSpec(bias.shape, _const_map(3))],
        out_specs=[pl.BlockSpec(blk, lambda r, n: (n, r)),
                   pl.BlockSpec((N_HEADS, BLOCK, LANES), lambda r, n: (0, n, r))],
        out_shape=[jax.ShapeDtypeStruct((l_len, dil * DIL_WIDTH), F32),
                   jax.ShapeDtypeStruct((N_HEADS, l_len, dil * LANES), F32)],
        compiler_params=_cparams(("arbitrary", "arbitrary"), VMEM_LIMIT),
    )(pv, pv, pv, pv, pv, bias)
    return o.reshape(t_len, DIL_WIDTH), lse.reshape(N_HEADS, t_len, LANES)


def _att_bwd_call(p, bias, o, lse, do, dlse, g, dil):
    t_len = p.shape[0]
    l_len = t_len // dil
    nb = l_len // BLOCK
    pv = p.reshape(l_len, dil * DIL_PROJ)
    ov = o.reshape(l_len, dil * DIL_WIDTH)
    dov = do.reshape(l_len, dil * DIL_WIDTH)
    lsev = lse.reshape(N_HEADS, l_len, dil * LANES)
    dlsev = dlse.reshape(N_HEADS, l_len, dil * LANES)
    base = g * 3

    def body(q_ref, qn_ref, k_ref, v_ref, do_ref, don_ref, o_ref, on_ref, lse_ref, lsen_ref, dl_ref, dln_ref,
             bias_ref, dq_ref, dk_ref, dv_ref, dbias_ref, carry_ref):
        r = pl.program_id(0)
        n = pl.program_id(1)
        cur_ok, prev_band, lane = _att_masks()
        has_next = n + 1 < nb

        @pl.when(jnp.logical_and(r == 0, n == 0))
        def _():
            dbias_ref[...] = jnp.zeros_like(dbias_ref)

        @pl.when(n == 0)
        def _():
            carry_ref[...] = jnp.zeros_like(carry_ref)

        for hp in range(N_PAIRS):
            lanes = slice(hp * LANES, (hp + 1) * LANES)
            k2 = k_ref[:, lanes].astype(BF16)
            v2 = v_ref[:, lanes].astype(BF16)
            dk2 = jnp.zeros((BLOCK, LANES), F32)
            dv2 = jnp.zeros((BLOCK, LANES), F32)
            dq_cur = carry_ref[:, lanes]
            dq_next = jnp.zeros((BLOCK, LANES), F32)
            for hh in range(2):
                h = 2 * hp + hh
                mine = (lane // HEAD_DIM) == hh
                tiles = (
                    (q_ref, do_ref, o_ref, lse_ref, dl_ref, cur_ok, slice(BLOCK, 2 * BLOCK), None),
                    (qn_ref, don_ref, on_ref, lsen_ref, dln_ref, prev_band, slice(0, BLOCK), has_next),
                )
                for ti, (qr, dor, orf, lr, dlr, ok, bcols, gate) in enumerate(tiles):
                    q2 = qr[:, lanes].astype(BF16)
                    qm = jnp.where(mine, q2, jnp.zeros_like(q2))
                    do_f = jnp.where(mine, dor[:, lanes], 0.0)
                    dom = do_f.astype(BF16)
                    s = lax.dot_general(qm, k2, _DOT_DIMS["nt"], preferred_element_type=F32) * ATT_SCALE
                    s = s + bias_ref[h, :, bcols]
                    if gate is not None:
                        ok = jnp.logical_and(ok, gate)
                    pr = jnp.where(ok, jnp.exp(jnp.minimum(s - lr[h], 0.0)), 0.0)
                    dp = lax.dot_general(dom, v2, _DOT_DIMS["nt"], preferred_element_type=F32)
                    delta = jnp.sum(do_f * orf[:, lanes], axis=-1, keepdims=True)
                    dl = jnp.sum(dlr[h], axis=-1, keepdims=True)
                    ds = pr * (dp - delta + dl)
                    dsb = ds.astype(BF16)
                    dq_h = jnp.where(mine, jnp.dot(dsb, k2, preferred_element_type=F32), 0.0) * ATT_SCALE
                    if ti == 0:
                        dq_cur = dq_cur + dq_h
                    else:
                        dq_next = dq_next + dq_h
                    dk2 = dk2 + lax.dot_general(dsb, qm, _DOT_DIMS["tn"], preferred_element_type=F32) * ATT_SCALE
                    dv2 = dv2 + lax.dot_general(pr.astype(BF16), dom, _DOT_DIMS["tn"], preferred_element_type=F32)
                    dbias_ref[h, :, bcols] += ds
            dq_ref[:, lanes] = dq_cur
            carry_ref[:, lanes] = dq_next
            dk_ref[:, lanes] = dk2
            dv_ref[:, lanes] = dv2

    def nxt(n):
        return jnp.minimum(n + 1, nb - 1)

    blk = (BLOCK, DIL_WIDTH)
    hblk = (N_HEADS, BLOCK, LANES)
    qcol = lambda j: (lambda r, n: (n, r * COLS_PER_POS + base + j))
    q_next = lambda r, n: (nxt(n), r * COLS_PER_POS + base)
    rown = lambda r, n: (n, r)
    rown_next = lambda r, n: (nxt(n), r)
    hrow = lambda r, n: (0, n, r)
    hrow_next = lambda r, n: (0, nxt(n), r)
    sds = jax.ShapeDtypeStruct((l_len, dil * DIL_WIDTH), F32)
    dq, dk, dv, dbias = pl.pallas_call(
        body, name=f"dil_att_bwd_g{g}", grid=(dil, nb),
        in_specs=[pl.BlockSpec(blk, qcol(0)), pl.BlockSpec(blk, q_next),
                  pl.BlockSpec(blk, qcol(1)), pl.BlockSpec(blk, qcol(2)),
                  pl.BlockSpec(blk, rown), pl.BlockSpec(blk, rown_next),
                  pl.BlockSpec(blk, rown), pl.BlockSpec(blk, rown_next),
                  pl.BlockSpec(hblk, hrow), pl.BlockSpec(hblk, hrow_next),
                  pl.BlockSpec(hblk, hrow), pl.BlockSpec(hblk, hrow_next),
                  pl.BlockSpec(bias.shape, _const_map(3))],
        out_specs=[pl.BlockSpec(blk, rown)] * 3 + [pl.BlockSpec(bias.shape, _const_map(3))],
        out_shape=[sds, sds, sds, jax.ShapeDtypeStruct(bias.shape, F32)],
        scratch_shapes=[pltpu.VMEM((BLOCK, DIL_WIDTH), F32)],
        compiler_params=_cparams(("arbitrary", "arbitrary"), VMEM_LIMIT),
    )(pv, pv, pv, pv, dov, dov, ov, ov, lsev, lsev, dlsev, dlsev, bias)
    shp = (t_len, DIL_WIDTH)
    return dq.reshape(shp), dk.reshape(shp), dv.reshape(shp), dbias


def _att_all_groups(p, biases):
    outs = [_att_fwd_call(p, biases[g], g, dil) for g, (_, dil) in enumerate(DIL_PATTERNS)]
    return tuple(o for o, _ in outs), tuple(l for _, l in outs)


@jax.custom_vjp
def _dilated_attention(p, biases):
    return _att_all_groups(p, biases)


def _dilated_attention_fwd(p, biases):
    os_, lses = _att_all_groups(p, biases)
    return (os_, lses), (p, biases, os_, lses)


def _dilated_attention_bwd(res, cts):
    p, biases, os_, lses = res
    dos, dlses = cts
    parts, dbiases = [], []
    for g, (_, dil) in enumerate(DIL_PATTERNS):
        dq, dk, dv, dbias = _att_bwd_call(p, biases[g], os_[g], lses[g], dos[g], dlses[g], g, dil)
        parts += [dq, dk, dv]
        dbiases.append(dbias)
    return jnp.concatenate(parts, axis=1), tuple(dbiases)


_dilated_attention.defvjp(_dilated_attention_fwd, _dilated_attention_bwd)


def _me():
    return lax.axis_index("x"), lax.axis_index("y"), lax.axis_index("c")


def _flip(me, f):
    return tuple((1 - m) if b else m for m, b in zip(me, f))


def _chip_of(d):
    return 2 * d[0] + d[1]


def _dev_of(d):
    return 4 * d[0] + 2 * d[1] + d[2]


EXCHANGE_CHUNKS = 8
CHIP_FLIPS = ((1, 0, 0), (0, 1, 0), (1, 1, 0))
ALL_FLIPS = tuple((a, b, c) for a in (0, 1) for b in (0, 1) for c in (0, 1) if a or b or c)
CORE_FLIP = (0, 0, 1)


def _n_chunks(rows):
    return EXCHANGE_CHUNKS if rows % (EXCHANGE_CHUNKS * PACK_ROWS) == 0 else 1


def _exchange(src, n_slots, transfers, name):
    _, rows, cols = src.shape
    chunks = _n_chunks(rows)
    rc = rows // chunks
    n = len(transfers) * chunks

    def body(src_ref, dst_ref, send_sems, recv_sems):
        me = _me()
        copies = []
        for q in range(chunks):
            for kk, (f, src_slot, dst_slot) in enumerate(transfers):
                peer = _flip(me, f)
                cp = pltpu.make_async_remote_copy(
                    src_ref=src_ref.at[src_slot(me, peer), pl.ds(q * rc, rc)],
                    dst_ref=dst_ref.at[dst_slot(me, peer), pl.ds(q * rc, rc)],
                    send_sem=send_sems.at[kk * chunks + q], recv_sem=recv_sems.at[kk * chunks + q],
                    device_id=peer, device_id_type=MESH)
                cp.start()
                copies.append(cp)
        for cp in copies:
            cp.wait()

    return pl.pallas_call(
        body, name=name,
        out_shape=jax.ShapeDtypeStruct((n_slots, rows, cols), src.dtype),
        in_specs=[pl.BlockSpec(memory_space=pl.ANY)],
        out_specs=pl.BlockSpec(memory_space=pl.ANY),
        scratch_shapes=[pltpu.SemaphoreType.DMA((n,)), pltpu.SemaphoreType.DMA((n,))],
    )(src)


def _set_slot(buf, block, index):
    return lax.dynamic_update_slice_in_dim(buf, block[None].astype(buf.dtype), index, axis=0)


def _chip_all_gather(src, name):
    got = _exchange(src[None], N_CHIPS, [(f, lambda me, peer: 0, lambda me, peer: _chip_of(me)) for f in CHIP_FLIPS], name)
    return _set_slot(got, src, _chip_of(_me()))


def _dev_all_gather(src, name):
    got = _exchange(src[None], N_DEV, [(f, lambda me, peer: 0, lambda me, peer: _dev_of(me)) for f in ALL_FLIPS], name)
    return _set_slot(got, src, _dev_of(_me()))


def _chip_scatter(src, name):
    got = _exchange(src, N_CHIPS, [(f, lambda me, peer: _chip_of(peer), lambda me, peer: _chip_of(me))
                                   for f in CHIP_FLIPS], name)
    chip = _chip_of(_me())
    return _set_slot(got, lax.dynamic_index_in_dim(src, chip, 0, keepdims=False), chip)


def _core_halves(src, name):
    s, _, half, cols = src.shape
    transfers = [(CORE_FLIP, (lambda me, peer, j=j: 2 * j + peer[2]), (lambda me, peer, j=j: j)) for j in range(s)]
    return _exchange(src.reshape(2 * s, half, cols), s, transfers, name)


def _core_all_gather(src, name):
    got = _exchange(src[None], 2, [(CORE_FLIP, lambda me, peer: 0, lambda me, peer: me[2])], name)
    return _set_slot(got, src, _me()[2])


def _two_level_gather(src, name):
    rows, cols = src.shape
    half = rows // 2
    chunks = _n_chunks(half)
    rc = half // chunks
    n = len(CHIP_FLIPS) * chunks

    def body(src_ref, g_ref, send1, recv1, send2, recv2):
        me = _me()
        c = me[2]
        sibling = _flip(me, CORE_FLIP)
        first, second = [], []
        for q in range(chunks):
            for kk, f in enumerate(CHIP_FLIPS):
                peer = _flip(me, f)
                cp = pltpu.make_async_remote_copy(
                    src_ref=src_ref.at[c, pl.ds(q * rc, rc)], dst_ref=g_ref.at[_chip_of(me), c, pl.ds(q * rc, rc)],
                    send_sem=send1.at[kk * chunks + q], recv_sem=recv1.at[kk * chunks + q],
                    device_id=peer, device_id_type=MESH)
                cp.start()
                first.append((cp, _chip_of(peer), kk * chunks + q, q))
        for cp, origin, idx, q in first:
            cp.wait_recv()
            fw = pltpu.make_async_remote_copy(
                src_ref=g_ref.at[origin, c, pl.ds(q * rc, rc)], dst_ref=g_ref.at[origin, c, pl.ds(q * rc, rc)],
                send_sem=send2.at[idx], recv_sem=recv2.at[idx],
                device_id=sibling, device_id_type=MESH)
            fw.start()
            second.append(fw)
        for fw in second:
            fw.wait_recv()
        for cp, _, _, _ in first:
            cp.wait_send()
        for fw in second:
            fw.wait_send()

    got = pl.pallas_call(
        body, name=name,
        out_shape=jax.ShapeDtypeStruct((N_CHIPS, 2, half, cols), src.dtype),
        in_specs=[pl.BlockSpec(memory_space=pl.ANY)],
        out_specs=pl.BlockSpec(memory_space=pl.ANY),
        scratch_shapes=[pltpu.SemaphoreType.DMA((n,)) for _ in range(4)],
    )(src.reshape(2, half, cols))
    return _set_slot(got.reshape(N_CHIPS, rows, cols), src, _chip_of(_me()))


def _sum_slots(x, name):
    s, rows, cols = x.shape
    tile = _pick(rows, (512, 256, 128, 64, 32, 16, 8))

    def body(x_ref, o_ref):
        acc = x_ref[0].astype(F32)
        for i in range(1, s):
            acc = acc + x_ref[i].astype(F32)
        o_ref[...] = acc

    return pl.pallas_call(
        body, name=name, grid=(rows // tile,),
        in_specs=[pl.BlockSpec((s, tile, cols), lambda i: (0, i, 0))],
        out_specs=pl.BlockSpec((tile, cols), lambda i: (i, 0)),
        out_shape=jax.ShapeDtypeStruct((rows, cols), F32),
        compiler_params=_cparams(("parallel",), VMEM_LIMIT),
    )(x)


def _add_pairs(a, b, name):
    s, rows, cols = a.shape
    tile = _pick(rows, (512, 256, 128, 64, 32, 16, 8))

    def body(a_ref, b_ref, o_ref):
        o_ref[...] = (a_ref[...].astype(F32) + b_ref[...].astype(F32)).astype(o_ref.dtype)

    spec = pl.BlockSpec((1, tile, cols), lambda j, i: (j, i, 0))
    return pl.pallas_call(
        body, name=name, grid=(s, rows // tile),
        in_specs=[spec, spec], out_specs=spec,
        out_shape=jax.ShapeDtypeStruct(a.shape, a.dtype),
        compiler_params=_cparams(("parallel", "parallel"), VMEM_LIMIT),
    )(a, b)


def _adamw(w, g, m, v, name):
    rows, cols = w.shape
    tile = rows
    if rows * cols * 4 > 2 * 1024 * 1024:
        tile = _pick(rows, (256, 128, 64, 32, 16, 8))
    c1 = 1.0 / (1.0 - ADAM_B1 ** ADAM_STEP)
    c2 = 1.0 / (1.0 - ADAM_B2 ** ADAM_STEP)

    def body(w_ref, g_ref, m_ref, v_ref, d_ref, nm_ref, nv_ref):
        gv = g_ref[...]
        nm = ADAM_B1 * m_ref[...] + (1.0 - ADAM_B1) * gv
        nv = ADAM_B2 * v_ref[...] + (1.0 - ADAM_B2) * (gv * gv)
        m_hat = nm * c1
        v_hat = nv * c2
        d_ref[...] = -ADAM_LR * (m_hat / (jnp.sqrt(v_hat) + ADAM_EPS) + ADAM_WD * w_ref[...])
        nm_ref[...] = nm
        nv_ref[...] = nv

    spec = pl.BlockSpec((tile, cols), lambda i: (i, 0))
    sds = jax.ShapeDtypeStruct((rows, cols), F32)
    return pl.pallas_call(
        body, name=name, grid=(rows // tile,),
        in_specs=[spec] * 4, out_specs=[spec] * 3, out_shape=[sds] * 3,
        compiler_params=_cparams(("parallel",), VMEM_LIMIT),
    )(w, g, m, v)


def _ada_fwd(c_all, ada_w, ada_b_cols):
    n_col = ada_w.shape[2]

    def body(c_ref, w_ref, b_ref, o_ref):
        cv = c_ref[...]
        cond = (cv * jax.nn.sigmoid(cv)).astype(BF16)
        o_ref[0] = jnp.dot(cond, w_ref[0].astype(BF16), preferred_element_type=F32) + b_ref[0]

    return pl.pallas_call(
        body, name="ada_fwd", grid=(DEPTH,),
        in_specs=[pl.BlockSpec(c_all.shape, lambda i: (0, 0)),
                  pl.BlockSpec((1, D_MODEL, n_col), lambda i: (i, 0, 0)),
                  pl.BlockSpec((1, 1, n_col), lambda i: (i, 0, 0))],
        out_specs=pl.BlockSpec((1, N_DEV, n_col), lambda i: (i, 0, 0)),
        out_shape=jax.ShapeDtypeStruct((DEPTH, N_DEV, n_col), F32),
        compiler_params=_cparams(("parallel",), VMEM_LIMIT),
    )(c_all, ada_w, ada_b_cols)


def _ada_grad(c_all_t, dmod_cols):
    n_col = dmod_cols.shape[2]

    def body(c_ref, d_ref, o_ref):
        cv = c_ref[...]
        cond = cv * jax.nn.sigmoid(cv)
        o_ref[0] = jnp.dot(cond, d_ref[0], precision=lax.Precision.HIGHEST, preferred_element_type=F32)

    return pl.pallas_call(
        body, name="ada_grad", grid=(DEPTH,),
        in_specs=[pl.BlockSpec(c_all_t.shape, lambda i: (0, 0)),
                  pl.BlockSpec((1, LANES, n_col), lambda i: (i, 0, 0))],
        out_specs=pl.BlockSpec((1, D_MODEL, n_col), lambda i: (i, 0, 0)),
        out_shape=jax.ShapeDtypeStruct((DEPTH, D_MODEL, n_col), F32),
        compiler_params=_cparams(("parallel",), VMEM_LIMIT),
    )(c_all_t, dmod_cols)


ROW_TILE = 256


def _shift_rows(a, n=1):
    return jnp.pad(a, ((n, 0), (0, 0)))[:-n]


def _modulate(x, sc, sh, name):
    def fn(x, sc, sh):
        return (x * (1.0 + sc) + sh,)

    t = x.shape[0]
    return _fused(fn, [_tiled(x, ROW_TILE), _shared(sc), _shared(sh)],
                  [_tiled_out(t, D_MODEL, ROW_TILE)], (t // ROW_TILE,), name)[0]


def _resid_ln_mod(x, y, gate, ln_g, ln_b, sc, sh, name):
    def fn(x, y, gate, ln_g, ln_b, sc, sh):
        x1 = _layer_norm_rows(ALPHA * x + (1.0 + gate) * y, ln_g, ln_b)
        return x1, x1 * (1.0 + sc) + sh

    t = x.shape[0]
    return _fused(fn, [_tiled(x, ROW_TILE), _tiled(y, ROW_TILE)] + [_shared(a) for a in (gate, ln_g, ln_b, sc, sh)],
                  [_tiled_out(t, D_MODEL, ROW_TILE)] * 2, (t // ROW_TILE,), name)


def _resid_ln_loss(x, y, gate, ln_g, ln_b, target, name):
    def fn(x, y, gate, ln_g, ln_b, target):
        x1 = _layer_norm_rows(ALPHA * x + (1.0 + gate) * y, ln_g, ln_b)
        err = jnp.square(x1 - target)
        per_row = jnp.mean(err, axis=-1, keepdims=True)
        return (0.5 * jnp.sum(per_row, axis=0, keepdims=True),)

    t = x.shape[0]
    return _fused(fn, [_tiled(x, ROW_TILE), _tiled(y, ROW_TILE)] + [_shared(a) for a in (gate, ln_g, ln_b)]
                  + [_tiled(target, ROW_TILE)],
                  [((1, 1), (1, 1), _const_map(2), "a")], (t // ROW_TILE,), name)[0]


def _sq_relu(h, name):
    def fn(h):
        return (jnp.square(jnp.maximum(h, 0.0)),)

    t, f = h.shape
    return _fused(fn, [_tiled(h, ROW_TILE)], [_tiled_out(t, f, ROW_TILE)], (t // ROW_TILE,), name)[0]


def _mlp(u, w1, s1, w2, s2, name):
    h = _linear(u, w1, s1, name + "_w1")
    return _linear(_sq_relu(h, name + "_act"), w2, s2, name + "_w2")


AB_PIECES = (("r", 0, 512, 512), ("k", 512, 512, 512), ("v", 1024, 512, 512),
             ("wd", 1536, 64, 128), ("ad", 1600, 64, 128), ("gd", 1664, 160, 256),
             ("h", 1824, 512, 512), ("bg", 2336, 512, 512), ("cg", 2848, 512, 512))
AB_PAD_COLS = sum(p[3] for p in AB_PIECES)


def _regroup_cols(w):
    parts = []
    for _, start, width, padded in AB_PIECES:
        piece = w[..., start:start + width]
        if padded != width:
            piece = jnp.pad(piece, [(0, 0)] * (w.ndim - 1) + [(0, padded - width)])
        parts.append(piece)
    return jnp.concatenate(parts, axis=-1)


def _pad_rows(w, rows):
    return jnp.pad(w, ((0, rows - w.shape[0]), (0, 0)))


def _rwkv_shortconv(u, big, wts):
    t = u.shape[0]
    p = _linear(u, _regroup_cols(big["ab_w_in"][0]), _regroup_cols(wts["ab_w_in"][0]), "ab_in")
    mu = _regroup_cols(jnp.pad(wts["rw_mu"], ((0, 0), (0, AB_PROJ - RW_PROJ))))
    w_up = _pad_rows(wts["rw_w_up"][0], 128)
    a_up = _pad_rows(wts["rw_a_up"][0], 128)
    g_up = _pad_rows(wts["rw_g_up"][0], 256)

    def pre(rp, rs, kp, ks, vp, vs, wdp, wds, adp, ads, gdp, gds, h, cg,
            mu_r, mu_k, mu_v, mu_w, mu_a, mu_g, w0, w_up, a0, a_up, g_up, k_k, k_a):
        def mix(pv, sv, m):
            return pv + m * (sv - pv)

        r, k, v = mix(rp, rs, mu_r), mix(kp, ks, mu_k), mix(vp, vs, mu_v)
        wd, ad, gd = mix(wdp, wds, mu_w), mix(adp, ads, mu_a), mix(gdp, gds, mu_g)
        logw = -_softplus(-(w0 + _bdot(jnp.tanh(wd), w_up))) - 0.5
        decay = jnp.exp(-jnp.exp(logw))
        iclr = jax.nn.sigmoid(a0 + _bdot(ad, a_up))
        gate = _bdot(jax.nn.sigmoid(gd), g_up)
        kk = k * k_k
        kk = kk / jnp.maximum(jnp.sqrt(_head_sum(kk * kk)), 1e-12)
        k_h = k * (1.0 + (iclr - 1.0) * k_a)
        return r, decay, k_h, v, -kk, kk * iclr, gate, cg * h

    tile = ROW_TILE
    names = [q[0] for q in AB_PIECES]
    cuts = list(np.cumsum([q[3] for q in AB_PIECES])[:-1])
    pp = dict(zip(names, jnp.split(p, cuts, axis=1)))
    mp = dict(zip(names, jnp.split(mu, cuts, axis=1)))

    ins = []
    for name in ("r", "k", "v", "wd", "ad", "gd"):
        ins += [_tiled(pp[name], tile), _tiled(_shift_rows(pp[name]), tile)]
    ins += [_tiled(pp["h"], tile), _tiled(pp["cg"], tile)]
    ins += [_shared(mp[name]) for name in ("r", "k", "v", "wd", "ad", "gd")]
    ins += [_shared(a) for a in (wts["rw_w0"], w_up, wts["rw_a0"], a_up, g_up, wts["rw_k_k"], wts["rw_k_a"])]
    outs = [_tiled_out(t, RW_WIDTH, tile)] * 8
    r, decay, k_h, v, a, b, gate, z = _fused(pre, ins, outs, (t // tile,), "rwkv_pre")

    y = _rwkv_scan(r, decay, k_h, v, a, b)

    conv_w = wts["sc_conv_w"][0]
    r_k = wts["rw_r_k"].reshape(1, RW_WIDTH)

    def post(y, r, k_h, v, gate, bg, z, z1, z2, lnx_g, lnx_b, r_k, c0, c1, c2):
        mean = _head_sum(y) * (1.0 / HEAD_DIM)
        yc = y - mean
        var = _head_sum(yc * yc) * (1.0 / HEAD_DIM)
        yn = yc * lax.rsqrt(var + RW_GN_EPS) * lnx_g + lnx_b
        bonus = _head_sum(r * k_h * r_k) * v
        return (yn + bonus) * gate, bg * (c0 * z2 + c1 * z1 + c2 * z)

    ins = [_tiled(a_, tile) for a_ in (y, r, k_h, v, gate, pp["bg"])]
    ins += [_tiled(a_, tile) for a_ in (z, _shift_rows(z, 1), _shift_rows(z, 2))]
    ins += [_shared(a_) for a_ in (wts["rw_lnx_g"], wts["rw_lnx_b"], r_k, conv_w[0:1], conv_w[1:2], conv_w[2:3])]
    y_a, y_b = _fused(post, ins, [_tiled_out(t, RW_WIDTH, tile)] * 2, (t // tile,), "rwkv_post")
    return _linear(jnp.concatenate([y_a, y_b], axis=1), big["ab_w_out"][0], wts["ab_w_out"][0], "ab_out")


def _t5_bucket_np(dist):
    exact = N_BUCKETS // 2
    logd = np.log(np.maximum(dist, 1).astype(np.float32) / exact) / math.log(MAX_DISTANCE / exact)
    large = np.minimum(exact + (logd * (N_BUCKETS - exact)).astype(np.int32), N_BUCKETS - 1)
    return np.where(dist < exact, dist, large)


def _merge_groups(os_, lses, name):
    t = os_[0].shape[0]
    tile = ROW_TILE

    def fn(o0, o1, o2, l0, l1, l2):
        lane = lax.broadcasted_iota(jnp.int32, (1, LANES), 1)
        lo = lane < HEAD_DIM
        ls = [jnp.where(lo, l[0], l[1]) for l in (l0, l1, l2)]
        m = jnp.maximum(jnp.maximum(ls[0], ls[1]), ls[2])
        es = [jnp.exp(l - m) for l in ls]
        den = es[0] + es[1] + es[2]
        return ((es[0] * o0 + es[1] * o1 + es[2] * o2) / den,)

    ins = [(o, (tile, LANES), lambda i, hp: (i, hp), "t") for o in os_]
    ins += [(l, (2, tile, LANES), lambda i, hp: (hp, i, 0), "t") for l in lses]
    outs = [((t, DIL_WIDTH), (tile, LANES), lambda i, hp: (i, hp), "t")]
    return _fused(fn, ins, outs, (t // tile, N_PAIRS), name)[0]


def _dilated_mixer(u, big, wts):
    p = _linear(u, big["dil_w_qkv"][0], wts["dil_w_qkv"][0], "dil_qkv")
    qi = np.arange(BLOCK)[:, None]
    ki = np.arange(2 * BLOCK)[None, :]
    rel = BLOCK + qi - ki
    biases = []
    for g, (window, dil) in enumerate(DIL_PATTERNS):
        span = window // dil
        bucket = _t5_bucket_np(np.clip(rel, 0, span) * dil).reshape(-1)
        onehot = jnp.asarray(np.eye(N_BUCKETS, dtype=np.float32)[bucket])
        table = wts["rel_bias"][:, g * N_HEADS:(g + 1) * N_HEADS]
        bias = jnp.dot(onehot, table, precision=lax.Precision.HIGHEST)
        biases.append(jnp.transpose(bias.reshape(BLOCK, 2 * BLOCK, N_HEADS), (2, 0, 1)))
    os_, lses = _dilated_attention(p, tuple(biases))
    o = _merge_groups(os_, lses, "dil_merge")
    return _linear(o, big["dil_w_out"][0], wts["dil_w_out"][0], "dil_out")


def _forward_local(x, mods, big, wts, target):
    u = _modulate(x, mods[0, 1], mods[0, 0], "mod_in")
    for i in range(DEPTH):
        sh2, sc2, g1, g2 = mods[i, 3], mods[i, 4], mods[i, 2], mods[i, 5]
        y = _rwkv_shortconv(u, big, wts) if i % 2 == 0 else _dilated_mixer(u, big, wts)
        x, u = _resid_ln_mod(x, y, g1, wts["ln_g"][i, 0:1], wts["ln_b"][i, 0:1], sc2, sh2, f"ln_mix{i}")
        y = _mlp(u, big["mlp_w1"][i], wts["mlp_w1"][i], big["mlp_w2"][i], wts["mlp_w2"][i], f"mlp{i}")
        if i + 1 < DEPTH:
            x, u = _resid_ln_mod(x, y, g2, wts["ln_g"][i, 1:2], wts["ln_b"][i, 1:2],
                                 mods[i + 1, 1], mods[i + 1, 0], f"ln_mlp{i}")
        else:
            return _resid_ln_loss(x, y, g2, wts["ln_g"][i, 1:2], wts["ln_b"][i, 1:2], target, "ln_loss")


SHARDED = {"ab_w_in": 2, "ab_w_out": 1, "dil_w_qkv": 2, "dil_w_out": 2, "mlp_w1": 2, "mlp_w2": 1,
           "ln_g": 2, "ln_b": 2, "rw_w_up": 2, "rw_a_up": 2, "rw_g_up": 2, "sc_conv_w": 2}
BIG = ("ab_w_in", "ab_w_out", "dil_w_qkv", "dil_w_out", "mlp_w1", "mlp_w2")
SMALL_SHARDED = ("ln_g", "ln_b", "rw_w_up", "rw_a_up", "rw_g_up", "sc_conv_w")
REPLICATED = ("ada_b", "rw_mu", "rw_w0", "rw_a0", "rw_k_k", "rw_k_a", "rw_r_k", "rw_lnx_g", "rw_lnx_b", "rel_bias")
WEIGHT_ORDER = ("ada_w", "ada_b", "ln_g", "ln_b", "ab_w_in", "rw_mu", "rw_w0", "rw_w_up", "rw_a0", "rw_a_up",
                "rw_g_up", "rw_k_k", "rw_k_a", "rw_r_k", "rw_lnx_g", "rw_lnx_b", "sc_conv_w", "ab_w_out",
                "dil_w_qkv", "dil_w_out", "rel_bias", "mlp_w1", "mlp_w2")


PACK_ROWS = 16


def _rows_of(n_elems):
    return -(-n_elems // (ROW_W * PACK_ROWS)) * PACK_ROWS


def _to_rows(a):
    flat = a.reshape(-1)
    rows = _rows_of(flat.shape[0])
    if rows * ROW_W != flat.shape[0]:
        flat = jnp.pad(flat, (0, rows * ROW_W - flat.shape[0]))
    return flat.reshape(rows, ROW_W)


def _from_rows(rows, shape):
    n = int(np.prod(shape))
    return rows.reshape(-1)[:n].reshape(shape)


def _split_chips(full, axis):
    shp = full.shape
    parts = full.reshape(shp[:axis] + (N_CHIPS, shp[axis] // N_CHIPS) + shp[axis + 1:])
    return jnp.moveaxis(parts, axis, 0)


def _join_chips(parts, axis):
    moved = jnp.moveaxis(parts, 0, axis)
    shp = moved.shape
    return moved.reshape(shp[:axis] + (shp[axis] * shp[axis + 1],) + shp[axis + 2:])


def _pack_rows(arrays):
    blocks = [_to_rows(a) for a in arrays]
    total = sum(b.shape[0] for b in blocks)
    pad = (-total) % 256
    if pad:
        blocks.append(jnp.zeros((pad, ROW_W), blocks[0].dtype))
    return jnp.concatenate(blocks, axis=0)


def _unpack_rows(buf, shapes):
    out, r0 = [], 0
    for shp in shapes:
        n = _rows_of(int(np.prod(shp)))
        out.append(_from_rows(buf[r0:r0 + n], shp))
        r0 += n
    return out


def _as2d(a):
    return a.reshape(-1, a.shape[-1])


def kernel(x, c, ada_w, ada_b, ln_g, ln_b, ab_w_in, rw_mu, rw_w0, rw_w_up, rw_a0, rw_a_up, rw_g_up, rw_k_k, rw_k_a, rw_r_k, rw_lnx_g, rw_lnx_b, sc_conv_w, ab_w_out, dil_w_qkv, dil_w_out, rel_bias, mlp_w1, mlp_w2, loss_target, m_ada_w, m_ada_b, m_ln_g, m_ln_b, m_ab_w_in, m_rw_mu, m_rw_w0, m_rw_w_up, m_rw_a0, m_rw_a_up, m_rw_g_up, m_rw_k_k, m_rw_k_a, m_rw_r_k, m_rw_lnx_g, m_rw_lnx_b, m_sc_conv_w, m_ab_w_out, m_dil_w_qkv, m_dil_w_out, m_rel_bias, m_mlp_w1, m_mlp_w2, v_ada_w, v_ada_b, v_ln_g, v_ln_b, v_ab_w_in, v_rw_mu, v_rw_w0, v_rw_w_up, v_rw_a0, v_rw_a_up, v_rw_g_up, v_rw_k_k, v_rw_k_a, v_rw_r_k, v_rw_lnx_g, v_rw_lnx_b, v_sc_conv_w, v_ab_w_out, v_dil_w_qkv, v_dil_w_out, v_rel_bias, v_mlp_w1, v_mlp_w2):
    args = dict(locals())
    w_in = {n: args[n] for n in WEIGHT_ORDER}
    m_in = {n: args["m_" + n] for n in WEIGHT_ORDER}
    v_in = {n: args["v_" + n] for n in WEIGHT_ORDER}
    me = _me()
    chip = _chip_of(me)
    dev = _dev_of(me)

    c_all = _dev_all_gather(c, "gather_c")[:, 0, :]
    n_col = ada_w.shape[2]
    ada_b_cols = lax.dynamic_slice_in_dim(ada_b, chip * n_col, n_col, axis=1)[:, None, :]
    mod_cols = _ada_fwd(c_all, ada_w, ada_b_cols)

    big_buf = _pack_rows([w_in[n].astype(BF16) for n in BIG])
    big_all = _two_level_gather(big_buf, "gather_big")
    small_buf = _pack_rows([mod_cols] + [w_in[n] for n in SMALL_SHARDED])
    small_all = _chip_all_gather(small_buf, "gather_small")

    wts = {n: w_in[n] for n in REPLICATED}
    big = {}
    big_parts = zip(*[_unpack_rows(big_all[j], [w_in[n].shape for n in BIG]) for j in range(N_CHIPS)])
    for n, parts in zip(BIG, big_parts):
        big[n] = _join_chips(jnp.stack(parts), SHARDED[n])
        wts[n] = jnp.zeros(big[n].shape, F32)
    small_shapes = [mod_cols.shape] + [w_in[n].shape for n in SMALL_SHARDED]
    small_parts = list(zip(*[_unpack_rows(small_all[j], small_shapes) for j in range(N_CHIPS)]))
    for n, parts in zip(SMALL_SHARDED, small_parts[1:]):
        wts[n] = _join_chips(jnp.stack(parts), SHARDED[n])
    mod_all = _join_chips(jnp.stack(small_parts[0]), 2)
    mods = lax.dynamic_slice_in_dim(mod_all, dev, 1, axis=1).reshape(DEPTH, 6, 1, D_MODEL)

    def local_loss(xv, modv, wv):
        return _forward_local(xv, modv, big, wv, loss_target[0])[0, 0]

    loss_local, (grad_x, dmods, dw) = jax.value_and_grad(local_loss, argnums=(0, 1, 2))(x[0], mods, wts)
    loss = lax.psum(loss_local, ("x", "y", "c"))

    small_row = jnp.concatenate([dmods.reshape(-1)] + [dw[n].reshape(-1) for n in REPLICATED[1:]])
    n_small = small_row.shape[0]
    n_small_pad = -(-n_small // LANES) * LANES
    small_row = jnp.pad(small_row, (0, n_small_pad - n_small))[None, :]
    rows_all = _dev_all_gather(small_row, "gather_small_grads")
    small_sum = _sum_slots(rows_all, "sum_small_grads")

    dmod_all = rows_all[:, 0, :DEPTH * 6 * D_MODEL].reshape(N_DEV, DEPTH, 6 * D_MODEL)
    dmod_cols = lax.dynamic_slice_in_dim(dmod_all, chip * n_col, n_col, axis=2)
    dmod_cols = jnp.pad(jnp.moveaxis(dmod_cols, 0, 1), ((0, 0), (0, LANES - N_DEV), (0, 0)))
    c_all_t = jnp.pad(c_all.T, ((0, 0), (0, LANES - N_DEV)))
    grads = {"ada_w": _ada_grad(c_all_t, dmod_cols)}
    grads["ada_b"] = small_sum[0, :DEPTH * 6 * D_MODEL].reshape(ada_b.shape)
    r0 = DEPTH * 6 * D_MODEL
    for n in REPLICATED[1:]:
        size = int(np.prod(w_in[n].shape))
        grads[n] = small_sum[0, r0:r0 + size].reshape(w_in[n].shape)
        r0 += size

    sharded_names = BIG + SMALL_SHARDED
    per_chip = [_split_chips(dw[n], SHARDED[n]) for n in sharded_names]
    send = jnp.stack([_pack_rows([pc[j].astype(BF16) for pc in per_chip]) for j in range(N_CHIPS)])
    n_rows = send.shape[1]
    send = send.reshape(N_CHIPS, 2, n_rows // 2, ROW_W)
    theirs = _core_halves(send, "swap_halves")
    mine = lax.dynamic_index_in_dim(send, me[2], 1, keepdims=False)
    chip_part = _add_pairs(mine, theirs, "sum_cores")
    recv = _chip_scatter(chip_part, "scatter_grads")
    half_sum = _sum_slots(recv, "sum_chips")
    g_rows = _core_all_gather(half_sum, "gather_halves").reshape(n_rows, ROW_W)
    for n, g in zip(sharded_names, _unpack_rows(g_rows, [w_in[n].shape for n in sharded_names])):
        grads[n] = g

    deltas, new_m, new_v = {}, {}, {}
    for n in WEIGHT_ORDER:
        shp = w_in[n].shape
        d, nm, nv = _adamw(_as2d(w_in[n]), _as2d(grads[n]), _as2d(m_in[n]), _as2d(v_in[n]), "adamw_" + n)
        deltas[n], new_m[n], new_v[n] = d.reshape(shp), nm.reshape(shp), nv.reshape(shp)

    return (loss, grad_x[None], *[grads[n] for n in WEIGHT_ORDER], *[deltas[n] for n in WEIGHT_ORDER],
            *[new_m[n] for n in WEIGHT_ORDER], *[new_v[n] for n in WEIGHT_ORDER])
```

```python
import functools
import math

import numpy as np
import jax
import jax.numpy as jnp
from jax import lax
from jax.experimental import pallas as pl
from jax.experimental.pallas import tpu as pltpu

F32 = jnp.float32
BF16 = jnp.bfloat16
MESH = pl.DeviceIdType.MESH

D_MODEL = 1024
DEPTH = 2
RW_WIDTH = 512
HEAD_DIM = 64
N_HEADS = 8
RW_DECAY_RANK = 64
RW_ICLR_RANK = 64
RW_GATE_RANK = 160
RW_GN_EPS = 64e-5
RW_PROJ = 3 * RW_WIDTH + RW_DECAY_RANK + RW_ICLR_RANK + RW_GATE_RANK
SC_WIDTH = 512
AB_PROJ = RW_PROJ + 3 * SC_WIDTH
DIL_PATTERNS = ((128, 1), (512, 4), (2048, 16))
N_GROUPS = 3
DIL_WIDTH = 512
DIL_PROJ = N_GROUPS * 3 * DIL_WIDTH
BLOCK = 128
N_BUCKETS = 32
MAX_DISTANCE = 2048
D_FF = 4 * D_MODEL
ALPHA = (2 * DEPTH) ** 0.25
LN_EPS = 1e-5
ADAM_LR = 0.001
ADAM_B1 = 0.9
ADAM_B2 = 0.999
ADAM_EPS = 1e-08
ADAM_WD = 0.01
ADAM_STEP = 10

N_CHIPS = 4
N_DEV = 8
LANES = 128
SUBLANES = 8
ROW_W = 1024
SCAN_CHUNK = 16
VMEM_LIMIT = 48 * 1024 * 1024
NEG_BIG = -1e30


def _pick(n, cands):
    for c in cands:
        if n % c == 0:
            return c
    return n


def _cparams(sem=None, vmem=None):
    return pltpu.CompilerParams(dimension_semantics=sem, vmem_limit_bytes=vmem)


_DOT_DIMS = {
    "nn": (((1,), (0,)), ((), ())),
    "nt": (((1,), (1,)), ((), ())),
    "tn": (((0,), (0,)), ((), ())),
}


def _mm(a, b, mode, name):
    if mode == "nn":
        (m, k), (_, n) = a.shape, b.shape
    elif mode == "nt":
        (m, k), (n, _) = a.shape, b.shape
    else:
        (k, m), (_, n) = a.shape, b.shape
    tm = _pick(m, (1024, 512, 256, 128))
    tn = _pick(n, (1024, 768, 512, 384, 256, 128))
    tk = _pick(k, (1024, 512, 256, 128) if mode == "tn" else (2048, 1024, 512, 256, 128))
    nk = k // tk
    if mode == "tn":
        a_spec = pl.BlockSpec((tk, tm), lambda i, j, kk: (kk, i))
    else:
        a_spec = pl.BlockSpec((tm, tk), lambda i, j, kk: (i, kk))
    if mode == "nt":
        b_spec = pl.BlockSpec((tn, tk), lambda i, j, kk: (j, kk))
    else:
        b_spec = pl.BlockSpec((tk, tn), lambda i, j, kk: (kk, j))
    dims = _DOT_DIMS[mode]

    def body(a_ref, b_ref, o_ref):
        part = lax.dot_general(a_ref[...].astype(BF16), b_ref[...].astype(BF16), dims, preferred_element_type=F32)
        if nk == 1:
            o_ref[...] = part
        else:
            kk = pl.program_id(2)

            @pl.when(kk == 0)
            def _():
                o_ref[...] = part

            @pl.when(kk > 0)
            def _():
                o_ref[...] += part

    return pl.pallas_call(
        body, name=name, grid=(m // tm, n // tn, nk),
        in_specs=[a_spec, b_spec],
        out_specs=pl.BlockSpec((tm, tn), lambda i, j, kk: (i, j)),
        out_shape=jax.ShapeDtypeStruct((m, n), F32),
        compiler_params=_cparams(("parallel", "parallel", "arbitrary"), VMEM_LIMIT),
    )(a, b)


def _linear(x, w, slot, name):
    @jax.custom_vjp
    def op(x, w, slot):
        return _mm(x, w, "nn", name + "_fwd")

    def fwd(x, w, slot):
        return _mm(x, w, "nn", name + "_fwd"), (x, w)

    def bwd(res, dy):
        x, w = res
        return _mm(dy, w, "nt", name + "_dx"), jnp.zeros_like(w), _mm(x, dy, "tn", name + "_dw")

    op.defvjp(fwd, bwd)
    return op(x, w, slot)


def _const_map(ndim):
    return lambda *g: (0,) * ndim


def _first_step(n_grid):
    return functools.reduce(jnp.logical_and, [pl.program_id(d) == 0 for d in range(n_grid)])


def _fused(fn, ins, outs, grid, name):
    arrays = [i[0] for i in ins]
    n_in, n_out, n_grid = len(ins), len(outs), len(grid)
    in_specs = [pl.BlockSpec(bs, im) for (_, bs, im, _) in ins]
    out_specs = [pl.BlockSpec(bs, im) for (_, bs, im, _) in outs]
    out_shapes = [jax.ShapeDtypeStruct(s, F32) for (s, _, _, _) in outs]
    sem = ("arbitrary",) * n_grid

    def fwd_call(*xs):
        def body(*refs):
            vals = [r[...] for r in refs[:n_in]]
            ys = fn(*vals)
            first = _first_step(n_grid)
            for o_ref, y, (_, _, _, kind) in zip(refs[n_in:], ys, outs):
                if kind == "t":
                    o_ref[...] = y
                else:
                    @pl.when(first)
                    def _(o_ref=o_ref):
                        o_ref[...] = jnp.zeros_like(o_ref)

                    o_ref[...] += y

        return pl.pallas_call(
            body, name=name + "_fwd", grid=grid, in_specs=in_specs, out_specs=out_specs,
            out_shape=out_shapes, compiler_params=_cparams(sem, VMEM_LIMIT))(*xs)

    def bwd_call(xs, dys):
        d_specs = [pl.BlockSpec(bs, im) for (_, bs, im, _) in outs]
        g_specs = [pl.BlockSpec(bs, im) for (_, bs, im, _) in ins]
        g_shapes = [jax.ShapeDtypeStruct(a.shape, F32) for a in arrays]

        def body(*refs):
            vals = [r[...] for r in refs[:n_in]]
            dvals = tuple(r[...] for r in refs[n_in:n_in + n_out])
            _, vjp = jax.vjp(lambda *v: tuple(fn(*v)), *vals)
            gs = vjp(dvals)
            first = _first_step(n_grid)
            for g_ref, g, (_, _, _, kind) in zip(refs[n_in + n_out:], gs, ins):
                if kind == "t":
                    g_ref[...] = g
                else:
                    @pl.when(first)
                    def _(g_ref=g_ref):
                        g_ref[...] = jnp.zeros_like(g_ref)

                    g_ref[...] += g

        return pl.pallas_call(
            body, name=name + "_bwd", grid=grid, in_specs=in_specs + d_specs, out_specs=g_specs,
            out_shape=g_shapes, compiler_params=_cparams(sem, VMEM_LIMIT))(*xs, *dys)

    @jax.custom_vjp
    def op(*xs):
        return tuple(fwd_call(*xs))

    def op_fwd(*xs):
        return tuple(fwd_call(*xs)), xs

    def op_bwd(xs, dys):
        return tuple(bwd_call(xs, dys))

    op.defvjp(op_fwd, op_bwd)
    return op(*arrays)


def _tiled(a, tile, cols=None, col_block=0):
    cols = a.shape[1] if cols is None else cols
    return (a, (tile, cols), lambda i, cb=col_block: (i, cb), "t")


def _shared(a):
    return (a, a.shape, _const_map(a.ndim), "b")


def _tiled_out(rows, cols, tile):
    return ((rows, cols), (tile, cols), lambda i: (i, 0), "t")


@jax.custom_vjp
def _bdot(x, w):
    return jnp.dot(x.astype(BF16), w.astype(BF16), preferred_element_type=F32)


def _bdot_fwd(x, w):
    return _bdot(x, w), (x, w)


def _bdot_bwd(res, dy):
    x, w = res
    dyb = dy.astype(BF16)
    dx = lax.dot_general(dyb, w.astype(BF16), _DOT_DIMS["nt"], preferred_element_type=F32)
    dw = lax.dot_general(x.astype(BF16), dyb, _DOT_DIMS["tn"], preferred_element_type=F32)
    return dx, dw


_bdot.defvjp(_bdot_fwd, _bdot_bwd)


def _head_sum(x):
    n = x.shape[-1]
    hi = lax.broadcasted_iota(jnp.int32, (n, n), 0) // HEAD_DIM
    hj = lax.broadcasted_iota(jnp.int32, (n, n), 1) // HEAD_DIM
    e = (hi == hj).astype(F32)
    return jnp.dot(x, e, precision=lax.Precision.HIGHEST, preferred_element_type=F32)


def _softplus(x):
    return jnp.maximum(x, 0.0) + jnp.log1p(jnp.exp(-jnp.abs(x)))


def _layer_norm_rows(z, g, b):
    mu = jnp.mean(z, axis=-1, keepdims=True)
    zc = z - mu
    var = jnp.mean(zc * zc, axis=-1, keepdims=True)
    return zc * lax.rsqrt(var + LN_EPS) * g + b


N_PAIRS = N_HEADS // 2


def _scan_consts():
    k = lax.broadcasted_iota(jnp.int32, (HEAD_DIM, LANES), 0)
    j = lax.broadcasted_iota(jnp.int32, (HEAD_DIM, LANES), 1)
    diag = (j % HEAD_DIM) == k
    jj = lax.broadcasted_iota(jnp.int32, (LANES, LANES), 0) // HEAD_DIM
    ll = lax.broadcasted_iota(jnp.int32, (LANES, LANES), 1) // HEAD_DIM
    same_head = (jj == ll).astype(BF16)
    lane_lo = j < HEAD_DIM
    return diag, same_head, lane_lo


def _unrolled(n, body, carry):
    for i in range(n):
        carry = body(i, carry)
    return carry


def _fill_cols(srcs, col_ref, diag_bf, same_head, n_steps):
    del diag_bf, same_head
    assert n_steps == 16
    blocks = []
    for src in srcs:
        x = src[...]
        hi = x.astype(BF16).astype(F32)
        r1 = x - hi
        mid = r1.astype(BF16).astype(F32)
        x48 = jnp.concatenate([hi, mid, r1 - mid], axis=0)
        for hp in range(N_PAIRS):
            xp = x48[:, hp * LANES:(hp + 1) * LANES]
            y = jnp.concatenate([xp, pltpu.roll(xp, HEAD_DIM, 1), jnp.zeros((32, LANES), F32)], axis=0)
            blocks.append(y.T[:HEAD_DIM].astype(BF16))
    lhs = jnp.concatenate(blocks, axis=0)
    j = lax.broadcasted_iota(jnp.int32, (LANES, LANES), 0)
    lane_head = lax.broadcasted_iota(jnp.int32, (LANES, LANES), 1) // HEAD_DIM
    for t in range(n_steps):
        pick = jnp.logical_and(j < 96, jnp.logical_and(j % 16 == t, j // 48 == lane_head))
        out = jnp.dot(lhs, pick.astype(BF16), preferred_element_type=F32)
        for vi in range(len(srcs)):
            for hp in range(N_PAIRS):
                r0 = (vi * N_PAIRS + hp) * HEAD_DIM
                col_ref[vi, hp, t] = out[r0:r0 + HEAD_DIM]


def _scan_fwd_call(r, w, k, v, a, b):
    t_len = r.shape[0]
    ch = SCAN_CHUNK
    n_ch = t_len // ch

    def body(r_ref, w_ref, k_ref, v_ref, a_ref, b_ref, y_ref, ck_ref, st_ref, col_ref):
        c = pl.program_id(0)

        @pl.when(c == 0)
        def _():
            st_ref[...] = jnp.zeros_like(st_ref)

        ck_ref[0] = st_ref[...]
        diag, same_head, _ = _scan_consts()
        diag_bf = diag.astype(BF16)
        _fill_cols((w_ref, a_ref, b_ref, k_ref, r_ref), col_ref, diag_bf, same_head, ch)

        def step(t, states):
            new = []
            for hp in range(N_PAIRS):
                lanes = slice(hp * LANES, (hp + 1) * LANES)
                s = states[hp]
                sa = jnp.sum(s * col_ref[1, hp, t], axis=0, keepdims=True)
                s = s * col_ref[0, hp, t] + col_ref[2, hp, t] * sa + col_ref[3, hp, t] * v_ref[pl.ds(t, 1), lanes]
                y_ref[pl.ds(t, 1), lanes] = jnp.sum(s * col_ref[4, hp, t], axis=0, keepdims=True)
                new.append(s)
            return tuple(new)

        states = _unrolled(ch, step, tuple(st_ref[hp] for hp in range(N_PAIRS)))
        for hp in range(N_PAIRS):
            st_ref[hp] = states[hp]

    row_spec = pl.BlockSpec((ch, RW_WIDTH), lambda c: (c, 0))
    return pl.pallas_call(
        body, name="rwkv_scan_fwd", grid=(n_ch,),
        in_specs=[row_spec] * 6,
        out_specs=[row_spec, pl.BlockSpec((1, N_PAIRS, HEAD_DIM, LANES), lambda c: (c, 0, 0, 0))],
        out_shape=[jax.ShapeDtypeStruct((t_len, RW_WIDTH), F32),
                   jax.ShapeDtypeStruct((n_ch, N_PAIRS, HEAD_DIM, LANES), F32)],
        scratch_shapes=[pltpu.VMEM((N_PAIRS, HEAD_DIM, LANES), F32),
                        pltpu.VMEM((5, N_PAIRS, ch, HEAD_DIM, LANES), F32)],
        compiler_params=_cparams(("arbitrary",), VMEM_LIMIT),
    )(r, w, k, v, a, b)


def _scan_bwd_call(r, w, k, v, a, b, ck, dy):
    t_len = r.shape[0]
    ch = SCAN_CHUNK
    n_ch = t_len // ch

    def body(r_ref, w_ref, k_ref, v_ref, a_ref, b_ref, ck_ref, dy_ref,
             dr_ref, dw_ref, dk_ref, dv_ref, da_ref, db_ref,
             ds_ref, col_ref, sp_ref, sa_ref):
        c = pl.program_id(0)

        @pl.when(c == 0)
        def _():
            ds_ref[...] = jnp.zeros_like(ds_ref)

        diag, same_head, lane_lo = _scan_consts()
        diag_bf = diag.astype(BF16)
        diag_f = diag.astype(F32)
        _fill_cols((w_ref, a_ref, b_ref, k_ref, r_ref), col_ref, diag_bf, same_head, ch)

        def replay(t, states):
            new = []
            for hp in range(N_PAIRS):
                lanes = slice(hp * LANES, (hp + 1) * LANES)
                s = states[hp]
                sp_ref[t, hp] = s
                sa = jnp.sum(s * col_ref[1, hp, t], axis=0, keepdims=True)
                sa_ref[pl.ds(t, 1), lanes] = sa
                new.append(s * col_ref[0, hp, t] + col_ref[2, hp, t] * sa
                           + col_ref[3, hp, t] * v_ref[pl.ds(t, 1), lanes])
            return tuple(new)

        _unrolled(ch, replay, tuple(ck_ref[0, hp] for hp in range(N_PAIRS)))

        def key_rows(ps):
            stacked = jnp.concatenate([p.astype(BF16) for p in ps], axis=0)
            q = jnp.dot(stacked, same_head, preferred_element_type=F32)
            return [jnp.sum(q[i * HEAD_DIM:(i + 1) * HEAD_DIM] * diag_f, axis=0, keepdims=True)
                    for i in range(len(ps))]

        def back(i, grads):
            t = ch - 1 - i
            new = []
            for hp in range(N_PAIRS):
                lanes = slice(hp * LANES, (hp + 1) * LANES)
                wc, ac, bc, kc, rc = (col_ref[vi, hp, t] for vi in range(5))
                sp = sp_ref[t, hp]
                sa = sa_ref[pl.ds(t, 1), lanes]
                vrow = v_ref[pl.ds(t, 1), lanes]
                dyrow = dy_ref[pl.ds(t, 1), lanes]
                st = sp * wc + bc * sa + kc * vrow
                g = grads[hp] + rc * dyrow
                dsa = jnp.sum(g * bc, axis=0, keepdims=True)
                dv_ref[pl.ds(t, 1), lanes] = jnp.sum(g * kc, axis=0, keepdims=True)
                rows = key_rows([st * dyrow, g * vrow, g * sa, g * sp, sp * dsa])
                for out_ref, row in zip((dr_ref, dk_ref, db_ref, dw_ref, da_ref), rows):
                    out_ref[pl.ds(t, 1), lanes] = row
                new.append(g * wc + ac * dsa)
            return tuple(new)

        grads = _unrolled(ch, back, tuple(ds_ref[hp] for hp in range(N_PAIRS)))
        for hp in range(N_PAIRS):
            ds_ref[hp] = grads[hp]

    row_spec = pl.BlockSpec((ch, RW_WIDTH), lambda c: (n_ch - 1 - c, 0))
    out_sds = jax.ShapeDtypeStruct((t_len, RW_WIDTH), F32)
    return pl.pallas_call(
        body, name="rwkv_scan_bwd", grid=(n_ch,),
        in_specs=[row_spec] * 6 + [pl.BlockSpec((1, N_PAIRS, HEAD_DIM, LANES), lambda c: (n_ch - 1 - c, 0, 0, 0)),
                                   row_spec],
        out_specs=[row_spec] * 6,
        out_shape=[out_sds] * 6,
        scratch_shapes=[pltpu.VMEM((N_PAIRS, HEAD_DIM, LANES), F32),
                        pltpu.VMEM((5, N_PAIRS, ch, HEAD_DIM, LANES), F32),
                        pltpu.VMEM((ch, N_PAIRS, HEAD_DIM, LANES), F32),
                        pltpu.VMEM((ch, RW_WIDTH), F32)],
        compiler_params=_cparams(("arbitrary",), VMEM_LIMIT),
    )(r, w, k, v, a, b, ck, dy)


@jax.custom_vjp
def _rwkv_scan(r, w, k, v, a, b):
    return _scan_fwd_call(r, w, k, v, a, b)[0]


def _rwkv_scan_fwd(r, w, k, v, a, b):
    y, ck = _scan_fwd_call(r, w, k, v, a, b)
    return y, (r, w, k, v, a, b, ck)


def _rwkv_scan_bwd(res, dy):
    return tuple(_scan_bwd_call(*res, dy))


_rwkv_scan.defvjp(_rwkv_scan_fwd, _rwkv_scan_bwd)


ATT_SCALE = HEAD_DIM ** -0.5
COLS_PER_POS = DIL_PROJ // DIL_WIDTH


def _att_masks():
    qi = lax.broadcasted_iota(jnp.int32, (BLOCK, BLOCK), 0)
    ki = lax.broadcasted_iota(jnp.int32, (BLOCK, BLOCK), 1)
    lane = lax.broadcasted_iota(jnp.int32, (1, LANES), 1)
    return ki <= qi, ki >= qi, lane


def _att_fwd_call(p, bias, g, dil):
    t_len = p.shape[0]
    l_len = t_len // dil
    nb = l_len // BLOCK
    pv = p.reshape(l_len, dil * DIL_PROJ)
    base = g * 3

    def body(q_ref, kc_ref, kp_ref, vc_ref, vp_ref, bias_ref, o_ref, lse_ref):
        n = pl.program_id(1)
        cur_ok, prev_band, lane = _att_masks()
        prev_ok = jnp.logical_and(prev_band, n > 0)
        for hp in range(N_PAIRS):
            lanes = slice(hp * LANES, (hp + 1) * LANES)
            q2 = q_ref[:, lanes].astype(BF16)
            kc = kc_ref[:, lanes].astype(BF16)
            kp = kp_ref[:, lanes].astype(BF16)
            vc = vc_ref[:, lanes].astype(BF16)
            vp = vp_ref[:, lanes].astype(BF16)
            o2 = jnp.zeros((BLOCK, LANES), F32)
            for hh in range(2):
                h = 2 * hp + hh
                mine = (lane // HEAD_DIM) == hh
                qm = jnp.where(mine, q2, jnp.zeros_like(q2))
                s_c = lax.dot_general(qm, kc, _DOT_DIMS["nt"], preferred_element_type=F32) * ATT_SCALE
                s_p = lax.dot_general(qm, kp, _DOT_DIMS["nt"], preferred_element_type=F32) * ATT_SCALE
                s_c = jnp.where(cur_ok, s_c + bias_ref[h, :, BLOCK:], NEG_BIG)
                s_p = jnp.where(prev_ok, s_p + bias_ref[h, :, :BLOCK], NEG_BIG)
                m = jnp.maximum(jnp.max(s_c, axis=-1, keepdims=True), jnp.max(s_p, axis=-1, keepdims=True))
                e_c = jnp.exp(s_c - m)
                e_p = jnp.exp(s_p - m)
                den = jnp.sum(e_c, axis=-1, keepdims=True) + jnp.sum(e_p, axis=-1, keepdims=True)
                o_h = (jnp.dot((e_c / den).astype(BF16), vc, preferred_element_type=F32)
                       + jnp.dot((e_p / den).astype(BF16), vp, preferred_element_type=F32))
                o2 = o2 + jnp.where(mine, o_h, 0.0)
                lse_ref[h] = jnp.broadcast_to(m + jnp.log(den), (BLOCK, LANES))
            o_ref[:, lanes] = o2

    def col(j):
        return lambda r, n: (n, r * COLS_PER_POS + base + j)

    def col_prev(j):
        return lambda r, n: (jnp.maximum(n - 1, 0), r * COLS_PER_POS + base + j)

    blk = (BLOCK, DIL_WIDTH)
    o, lse = pl.pallas_call(
        body, name=f"dil_att_fwd_g{g}", grid=(dil, nb),
        in_specs=[pl.BlockSpec(blk, col(0)), pl.BlockSpec(blk, col(1)), pl.BlockSpec(blk, col_prev(1)),
                  pl.BlockSpec(blk, col(2)), pl.BlockSpec(blk, col_prev(2)),
                  pl.BlockSpec(bias.shape, _const_map(3))],
        out_specs=[pl.BlockSpec(blk, lambda r, n: (n, r)),
                   pl.BlockSpec((N_HEADS, BLOCK, LANES), lambda r, n: (0, n, r))],
        out_shape=[jax.ShapeDtypeStruct((l_len, dil * DIL_WIDTH), F32),
                   jax.ShapeDtypeStruct((N_HEADS, l_len, dil * LANES), F32)],
        compiler_params=_cparams(("arbitrary", "arbitrary"), VMEM_LIMIT),
    )(pv, pv, pv, pv, pv, bias)
    return o.reshape(t_len, DIL_WIDTH), lse.reshape(N_HEADS, t_len, LANES)


def _att_bwd_call(p, bias, o, lse, do, dlse, g, dil):
    t_len = p.shape[0]
    l_len = t_len // dil
    nb = l_len // BLOCK
    pv = p.reshape(l_len, dil * DIL_PROJ)
    ov = o.reshape(l_len, dil * DIL_WIDTH)
    dov = do.reshape(l_len, dil * DIL_WIDTH)
    lsev = lse.reshape(N_HEADS, l_len, dil * LANES)
    dlsev = dlse.reshape(N_HEADS, l_len, dil * LANES)
    base = g * 3

    def body(q_ref, qn_ref, k_ref, v_ref, do_ref, don_ref, o_ref, on_ref, lse_ref, lsen_ref, dl_ref, dln_ref,
             bias_ref, dq_ref, dk_ref, dv_ref, dbias_ref, carry_ref):
        r = pl.program_id(0)
        n = pl.program_id(1)
        cur_ok, prev_band, lane = _att_masks()
        has_next = n + 1 < nb

        @pl.when(jnp.logical_and(r == 0, n == 0))
        def _():
            dbias_ref[...] = jnp.zeros_like(dbias_ref)

        @pl.when(n == 0)
        def _():
            carry_ref[...] = jnp.zeros_like(carry_ref)

        for hp in range(N_PAIRS):
            lanes = slice(hp * LANES, (hp + 1) * LANES)
            k2 = k_ref[:, lanes].astype(BF16)
            v2 = v_ref[:, lanes].astype(BF16)
            dk2 = jnp.zeros((BLOCK, LANES), F32)
            dv2 = jnp.zeros((BLOCK, LANES), F32)
            dq_cur = carry_ref[:, lanes]
            dq_next = jnp.zeros((BLOCK, LANES), F32)
            for hh in range(2):
                h = 2 * hp + hh
                mine = (lane // HEAD_DIM) == hh
                tiles = (
                    (q_ref, do_ref, o_ref, lse_ref, dl_ref, cur_ok, slice(BLOCK, 2 * BLOCK), None),
                    (qn_ref, don_ref, on_ref, lsen_ref, dln_ref, prev_band, slice(0, BLOCK), has_next),
                )
                for ti, (qr, dor, orf, lr, dlr, ok, bcols, gate) in enumerate(tiles):
                    q2 = qr[:, lanes].astype(BF16)
                    qm = jnp.where(mine, q2, jnp.zeros_like(q2))
                    do_f = jnp.where(mine, dor[:, lanes], 0.0)
                    dom = do_f.astype(BF16)
                    s = lax.dot_general(qm, k2, _DOT_DIMS["nt"], preferred_element_type=F32) * ATT_SCALE
                    s = s + bias_ref[h, :, bcols]
                    if gate is not None:
                        ok = jnp.logical_and(ok, gate)
                    pr = jnp.where(ok, jnp.exp(jnp.minimum(s - lr[h], 0.0)), 0.0)
                    dp = lax.dot_general(dom, v2, _DOT_DIMS["nt"], preferred_element_type=F32)
                    delta = jnp.sum(do_f * orf[:, lanes], axis=-1, keepdims=True)
                    dl = jnp.sum(dlr[h], axis=-1, keepdims=True)
                    ds = pr * (dp - delta + dl)
                    dsb = ds.astype(BF16)
                    dq_h = jnp.where(mine, jnp.dot(dsb, k2, preferred_element_type=F32), 0.0) * ATT_SCALE
                    if ti == 0:
                        dq_cur = dq_cur + dq_h
                    else:
                        dq_next = dq_next + dq_h
                    dk2 = dk2 + lax.dot_general(dsb, qm, _DOT_DIMS["tn"], preferred_element_type=F32) * ATT_SCALE
                    dv2 = dv2 + lax.dot_general(pr.astype(BF16), dom, _DOT_DIMS["tn"], preferred_element_type=F32)
                    dbias_ref[h, :, bcols] += ds
            dq_ref[:, lanes] = dq_cur
            carry_ref[:, lanes] = dq_next
            dk_ref[:, lanes] = dk2
            dv_ref[:, lanes] = dv2

    def nxt(n):
        return jnp.minimum(n + 1, nb - 1)

    blk = (BLOCK, DIL_WIDTH)
    hblk = (N_HEADS, BLOCK, LANES)
    qcol = lambda j: (lambda r, n: (n, r * COLS_PER_POS + base + j))
    q_next = lambda r, n: (nxt(n), r * COLS_PER_POS + base)
    rown = lambda r, n: (n, r)
    rown_next = lambda r, n: (nxt(n), r)
    hrow = lambda r, n: (0, n, r)
    hrow_next = lambda r, n: (0, nxt(n), r)
    sds = jax.ShapeDtypeStruct((l_len, dil * DIL_WIDTH), F32)
    dq, dk, dv, dbias = pl.pallas_call(
        body, name=f"dil_att_bwd_g{g}", grid=(dil, nb),
        in_specs=[pl.BlockSpec(blk, qcol(0)), pl.BlockSpec(blk, q_next),
                  pl.BlockSpec(blk, qcol(1)), pl.BlockSpec(blk, qcol(2)),
                  pl.BlockSpec(blk, rown), pl.BlockSpec(blk, rown_next),
                  pl.BlockSpec(blk, rown), pl.BlockSpec(blk, rown_next),
                  pl.BlockSpec(hblk, hrow), pl.BlockSpec(hblk, hrow_next),
                  pl.BlockSpec(hblk, hrow), pl.BlockSpec(hblk, hrow_next),
                  pl.BlockSpec(bias.shape, _const_map(3))],
        out_specs=[pl.BlockSpec(blk, rown)] * 3 + [pl.BlockSpec(bias.shape, _const_map(3))],
        out_shape=[sds, sds, sds, jax.ShapeDtypeStruct(bias.shape, F32)],
        scratch_shapes=[pltpu.VMEM((BLOCK, DIL_WIDTH), F32)],
        compiler_params=_cparams(("arbitrary", "arbitrary"), VMEM_LIMIT),
    )(pv, pv, pv, pv, dov, dov, ov, ov, lsev, lsev, dlsev, dlsev, bias)
    shp = (t_len, DIL_WIDTH)
    return dq.reshape(shp), dk.reshape(shp), dv.reshape(shp), dbias


def _att_all_groups(p, biases):
    outs = [_att_fwd_call(p, biases[g], g, dil) for g, (_, dil) in enumerate(DIL_PATTERNS)]
    return tuple(o for o, _ in outs), tuple(l for _, l in outs)


@jax.custom_vjp
def _dilated_attention(p, biases):
    return _att_all_groups(p, biases)


def _dilated_attention_fwd(p, biases):
    os_, lses = _att_all_groups(p, biases)
    return (os_, lses), (p, biases, os_, lses)


def _dilated_attention_bwd(res, cts):
    p, biases, os_, lses = res
    dos, dlses = cts
    parts, dbiases = [], []
    for g, (_, dil) in enumerate(DIL_PATTERNS):
        dq, dk, dv, dbias = _att_bwd_call(p, biases[g], os_[g], lses[g], dos[g], dlses[g], g, dil)
        parts += [dq, dk, dv]
        dbiases.append(dbias)
    return jnp.concatenate(parts, axis=1), tuple(dbiases)


_dilated_attention.defvjp(_dilated_attention_fwd, _dilated_attention_bwd)


def _me():
    return lax.axis_index("x"), lax.axis_index("y"), lax.axis_index("c")


def _flip(me, f):
    return tuple((1 - m) if b else m for m, b in zip(me, f))


def _chip_of(d):
    return 2 * d[0] + d[1]


def _dev_of(d):
    return 4 * d[0] + 2 * d[1] + d[2]


EXCHANGE_CHUNKS = 8
CHIP_FLIPS = ((1, 0, 0), (0, 1, 0), (1, 1, 0))
ALL_FLIPS = tuple((a, b, c) for a in (0, 1) for b in (0, 1) for c in (0, 1) if a or b or c)
CORE_FLIP = (0, 0, 1)


def _n_chunks(rows):
    return EXCHANGE_CHUNKS if rows % (EXCHANGE_CHUNKS * PACK_ROWS) == 0 else 1


def _exchange(src, n_slots, transfers, name):
    _, rows, cols = src.shape
    chunks = _n_chunks(rows)
    rc = rows // chunks
    n = len(transfers) * chunks

    def body(src_ref, dst_ref, send_sems, recv_sems):
        me = _me()
        copies = []
        for q in range(chunks):
            for kk, (f, src_slot, dst_slot) in enumerate(transfers):
                peer = _flip(me, f)
                cp = pltpu.make_async_remote_copy(
                    src_ref=src_ref.at[src_slot(me, peer), pl.ds(q * rc, rc)],
                    dst_ref=dst_ref.at[dst_slot(me, peer), pl.ds(q * rc, rc)],
                    send_sem=send_sems.at[kk * chunks + q], recv_sem=recv_sems.at[kk * chunks + q],
                    device_id=peer, device_id_type=MESH)
                cp.start()
                copies.append(cp)
        for cp in copies:
            cp.wait()

    return pl.pallas_call(
        body, name=name,
        out_shape=jax.ShapeDtypeStruct((n_slots, rows, cols), src.dtype),
        in_specs=[pl.BlockSpec(memory_space=pl.ANY)],
        out_specs=pl.BlockSpec(memory_space=pl.ANY),
        scratch_shapes=[pltpu.SemaphoreType.DMA((n,)), pltpu.SemaphoreType.DMA((n,))],
    )(src)


def _set_slot(buf, block, index):
    return lax.dynamic_update_slice_in_dim(buf, block[None].astype(buf.dtype), index, axis=0)


def _chip_all_gather(src, name):
    got = _exchange(src[None], N_CHIPS, [(f, lambda me, peer: 0, lambda me, peer: _chip_of(me)) for f in CHIP_FLIPS], name)
    return _set_slot(got, src, _chip_of(_me()))


def _dev_all_gather(src, name):
    got = _exchange(src[None], N_DEV, [(f, lambda me, peer: 0, lambda me, peer: _dev_of(me)) for f in ALL_FLIPS], name)
    return _set_slot(got, src, _dev_of(_me()))


def _chip_scatter(src, name):
    got = _exchange(src, N_CHIPS, [(f, lambda me, peer: _chip_of(peer), lambda me, peer: _chip_of(me))
                                   for f in CHIP_FLIPS], name)
    chip = _chip_of(_me())
    return _set_slot(got, lax.dynamic_index_in_dim(src, chip, 0, keepdims=False), chip)


def _core_halves(src, name):
    s, _, half, cols = src.shape
    transfers = [(CORE_FLIP, (lambda me, peer, j=j: 2 * j + peer[2]), (lambda me, peer, j=j: j)) for j in range(s)]
    return _exchange(src.reshape(2 * s, half, cols), s, transfers, name)


def _core_all_gather(src, name):
    got = _exchange(src[None], 2, [(CORE_FLIP, lambda me, peer: 0, lambda me, peer: me[2])], name)
    return _set_slot(got, src, _me()[2])


def _two_level_gather(src, name):
    rows, cols = src.shape
    half = rows // 2
    chunks = _n_chunks(half)
    rc = half // chunks
    n = len(CHIP_FLIPS) * chunks

    def body(src_ref, g_ref, send1, recv1, send2, recv2):
        me = _me()
        c = me[2]
        sibling = _flip(me, CORE_FLIP)
        first, second = [], []
        for q in range(chunks):
            for kk, f in enumerate(CHIP_FLIPS):
                peer = _flip(me, f)
                cp = pltpu.make_async_remote_copy(
                    src_ref=src_ref.at[c, pl.ds(q * rc, rc)], dst_ref=g_ref.at[_chip_of(me), c, pl.ds(q * rc, rc)],
                    send_sem=send1.at[kk * chunks + q], recv_sem=recv1.at[kk * chunks + q],
                    device_id=peer, device_id_type=MESH)
                cp.start()
                first.append((cp, _chip_of(peer), kk * chunks + q, q))
        for cp, origin, idx, q in first:
            cp.wait_recv()
            fw = pltpu.make_async_remote_copy(
                src_ref=g_ref.at[origin, c, pl.ds(q * rc, rc)], dst_ref=g_ref.at[origin, c, pl.ds(q * rc, rc)],
                send_sem=send2.at[idx], recv_sem=recv2.at[idx],
                device_id=sibling, device_id_type=MESH)
            fw.start()
            second.append(fw)
        for fw in second:
            fw.wait_recv()
        for cp, _, _, _ in first:
            cp.wait_send()
        for fw in second:
            fw.wait_send()

    got = pl.pallas_call(
        body, name=name,
        out_shape=jax.ShapeDtypeStruct((N_CHIPS, 2, half, cols), src.dtype),
        in_specs=[pl.BlockSpec(memory_space=pl.ANY)],
        out_specs=pl.BlockSpec(memory_space=pl.ANY),
        scratch_shapes=[pltpu.SemaphoreType.DMA((n,)) for _ in range(4)],
    )(src.reshape(2, half, cols))
    return _set_slot(got.reshape(N_CHIPS, rows, cols), src, _chip_of(_me()))


def _sum_slots(x, name):
    s, rows, cols = x.shape
    tile = _pick(rows, (512, 256, 128, 64, 32, 16, 8))

    def body(x_ref, o_ref):
        acc = x_ref[0].astype(F32)
        for i in range(1, s):
            acc = acc + x_ref[i].astype(F32)
        o_ref[...] = acc

    return pl.pallas_call(
        body, name=name, grid=(rows // tile,),
        in_specs=[pl.BlockSpec((s, tile, cols), lambda i: (0, i, 0))],
        out_specs=pl.BlockSpec((tile, cols), lambda i: (i, 0)),
        out_shape=jax.ShapeDtypeStruct((rows, cols), F32),
        compiler_params=_cparams(("parallel",), VMEM_LIMIT),
    )(x)


def _add_pairs(a, b, name):
    s, rows, cols = a.shape
    tile = _pick(rows, (512, 256, 128, 64, 32, 16, 8))

    def body(a_ref, b_ref, o_ref):
        o_ref[...] = (a_ref[...].astype(F32) + b_ref[...].astype(F32)).astype(o_ref.dtype)

    spec = pl.BlockSpec((1, tile, cols), lambda j, i: (j, i, 0))
    return pl.pallas_call(
        body, name=name, grid=(s, rows // tile),
        in_specs=[spec, spec], out_specs=spec,
        out_shape=jax.ShapeDtypeStruct(a.shape, a.dtype),
        compiler_params=_cparams(("parallel", "parallel"), VMEM_LIMIT),
    )(a, b)


def _adamw(w, g, m, v, name):
    rows, cols = w.shape
    tile = rows
    if rows * cols * 4 > 2 * 1024 * 1024:
        tile = _pick(rows, (256, 128, 64, 32, 16, 8))
    c1 = 1.0 / (1.0 - ADAM_B1 ** ADAM_STEP)
    c2 = 1.0 / (1.0 - ADAM_B2 ** ADAM_STEP)

    def body(w_ref, g_ref, m_ref, v_ref, d_ref, nm_ref, nv_ref):
        gv = g_ref[...]
        nm = ADAM_B1 * m_ref[...] + (1.0 - ADAM_B1) * gv
        nv = ADAM_B2 * v_ref[...] + (1.0 - ADAM_B2) * (gv * gv)
        m_hat = nm * c1
        v_hat = nv * c2
        d_ref[...] = -ADAM_LR * (m_hat / (jnp.sqrt(v_hat) + ADAM_EPS) + ADAM_WD * w_ref[...])
        nm_ref[...] = nm
        nv_ref[...] = nv

    spec = pl.BlockSpec((tile, cols), lambda i: (i, 0))
    sds = jax.ShapeDtypeStruct((rows, cols), F32)
    return pl.pallas_call(
        body, name=name, grid=(rows // tile,),
        in_specs=[spec] * 4, out_specs=[spec] * 3, out_shape=[sds] * 3,
        compiler_params=_cparams(("parallel",), VMEM_LIMIT),
    )(w, g, m, v)


def _ada_fwd(c_all, ada_w, ada_b_cols):
    n_col = ada_w.shape[2]

    def body(c_ref, w_ref, b_ref, o_ref):
        cv = c_ref[...]
        cond = (cv * jax.nn.sigmoid(cv)).astype(BF16)
        o_ref[0] = jnp.dot(cond, w_ref[0].astype(BF16), preferred_element_type=F32) + b_ref[0]

    return pl.pallas_call(
        body, name="ada_fwd", grid=(DEPTH,),
        in_specs=[pl.BlockSpec(c_all.shape, lambda i: (0, 0)),
                  pl.BlockSpec((1, D_MODEL, n_col), lambda i: (i, 0, 0)),
                  pl.BlockSpec((1, 1, n_col), lambda i: (i, 0, 0))],
        out_specs=pl.BlockSpec((1, N_DEV, n_col), lambda i: (i, 0, 0)),
        out_shape=jax.ShapeDtypeStruct((DEPTH, N_DEV, n_col), F32),
        compiler_params=_cparams(("parallel",), VMEM_LIMIT),
    )(c_all, ada_w, ada_b_cols)


def _ada_grad(c_all_t, dmod_cols):
    n_col = dmod_cols.shape[2]

    def body(c_ref, d_ref, o_ref):
        cv = c_ref[...]
        cond = cv * jax.nn.sigmoid(cv)
        o_ref[0] = jnp.dot(cond, d_ref[0], precision=lax.Precision.HIGHEST, preferred_element_type=F32)

    return pl.pallas_call(
        body, name="ada_grad", grid=(DEPTH,),
        in_specs=[pl.BlockSpec(c_all_t.shape, lambda i: (0, 0)),
                  pl.BlockSpec((1, LANES, n_col), lambda i: (i, 0, 0))],
        out_specs=pl.BlockSpec((1, D_MODEL, n_col), lambda i: (i, 0, 0)),
        out_shape=jax.ShapeDtypeStruct((DEPTH, D_MODEL, n_col), F32),
        compiler_params=_cparams(("parallel",), VMEM_LIMIT),
    )(c_all_t, dmod_cols)


ROW_TILE = 256


def _shift_rows(a, n=1):
    return jnp.pad(a, ((n, 0), (0, 0)))[:-n]


def _modulate(x, sc, sh, name):
    def fn(x, sc, sh):
        return (x * (1.0 + sc) + sh,)

    t = x.shape[0]
    return _fused(fn, [_tiled(x, ROW_TILE), _shared(sc), _shared(sh)],
                  [_tiled_out(t, D_MODEL, ROW_TILE)], (t // ROW_TILE,), name)[0]


def _resid_ln_mod(x, y, gate, ln_g, ln_b, sc, sh, name):
    def fn(x, y, gate, ln_g, ln_b, sc, sh):
        x1 = _layer_norm_rows(ALPHA * x + (1.0 + gate) * y, ln_g, ln_b)
        return x1, x1 * (1.0 + sc) + sh

    t = x.shape[0]
    return _fused(fn, [_tiled(x, ROW_TILE), _tiled(y, ROW_TILE)] + [_shared(a) for a in (gate, ln_g, ln_b, sc, sh)],
                  [_tiled_out(t, D_MODEL, ROW_TILE)] * 2, (t // ROW_TILE,), name)


def _resid_ln_loss(x, y, gate, ln_g, ln_b, target, name):
    def fn(x, y, gate, ln_g, ln_b, target):
        x1 = _layer_norm_rows(ALPHA * x + (1.0 + gate) * y, ln_g, ln_b)
        err = jnp.square(x1 - target)
        per_row = jnp.mean(err, axis=-1, keepdims=True)
        return (0.5 * jnp.sum(per_row, axis=0, keepdims=True),)

    t = x.shape[0]
    return _fused(fn, [_tiled(x, ROW_TILE), _tiled(y, ROW_TILE)] + [_shared(a) for a in (gate, ln_g, ln_b)]
                  + [_tiled(target, ROW_TILE)],
                  [((1, 1), (1, 1), _const_map(2), "a")], (t // ROW_TILE,), name)[0]


def _sq_relu(h, name):
    def fn(h):
        return (jnp.square(jnp.maximum(h, 0.0)),)

    t, f = h.shape
    return _fused(fn, [_tiled(h, ROW_TILE)], [_tiled_out(t, f, ROW_TILE)], (t // ROW_TILE,), name)[0]


def _mlp(u, w1, s1, w2, s2, name):
    h = _linear(u, w1, s1, name + "_w1")
    return _linear(_sq_relu(h, name + "_act"), w2, s2, name + "_w2")


AB_PIECES = (("r", 0, 512, 512), ("k", 512, 512, 512), ("v", 1024, 512, 512),
             ("wd", 1536, 64, 128), ("ad", 1600, 64, 128), ("gd", 1664, 160, 256),
             ("h", 1824, 512, 512), ("bg", 2336, 512, 512), ("cg", 2848, 512, 512))
AB_PAD_COLS = sum(p[3] for p in AB_PIECES)


def _regroup_cols(w):
    parts = []
    for _, start, width, padded in AB_PIECES:
        piece = w[..., start:start + width]
        if padded != width:
            piece = jnp.pad(piece, [(0, 0)] * (w.ndim - 1) + [(0, padded - width)])
        parts.append(piece)
    return jnp.concatenate(parts, axis=-1)


def _pad_rows(w, rows):
    return jnp.pad(w, ((0, rows - w.shape[0]), (0, 0)))


def _rwkv_shortconv(u, big, wts):
    t = u.shape[0]
    p = _linear(u, _regroup_cols(big["ab_w_in"][0]), _regroup_cols(wts["ab_w_in"][0]), "ab_in")
    mu = _regroup_cols(jnp.pad(wts["rw_mu"], ((0, 0), (0, AB_PROJ - RW_PROJ))))
    w_up = _pad_rows(wts["rw_w_up"][0], 128)
    a_up = _pad_rows(wts["rw_a_up"][0], 128)
    g_up = _pad_rows(wts["rw_g_up"][0], 256)

    def pre(rp, rs, kp, ks, vp, vs, wdp, wds, adp, ads, gdp, gds, h, cg,
            mu_r, mu_k, mu_v, mu_w, mu_a, mu_g, w0, w_up, a0, a_up, g_up, k_k, k_a):
        def mix(pv, sv, m):
            return pv + m * (sv - pv)

        r, k, v = mix(rp, rs, mu_r), mix(kp, ks, mu_k), mix(vp, vs, mu_v)
        wd, ad, gd = mix(wdp, wds, mu_w), mix(adp, ads, mu_a), mix(gdp, gds, mu_g)
        logw = -_softplus(-(w0 + _bdot(jnp.tanh(wd), w_up))) - 0.5
        decay = jnp.exp(-jnp.exp(logw))
        iclr = jax.nn.sigmoid(a0 + _bdot(ad, a_up))
        gate = _bdot(jax.nn.sigmoid(gd), g_up)
        kk = k * k_k
        kk = kk / jnp.maximum(jnp.sqrt(_head_sum(kk * kk)), 1e-12)
        k_h = k * (1.0 + (iclr - 1.0) * k_a)
        return r, decay, k_h, v, -kk, kk * iclr, gate, cg * h

    tile = ROW_TILE
    names = [q[0] for q in AB_PIECES]
    cuts = list(np.cumsum([q[3] for q in AB_PIECES])[:-1])
    pp = dict(zip(names, jnp.split(p, cuts, axis=1)))
    mp = dict(zip(names, jnp.split(mu, cuts, axis=1)))

    ins = []
    for name in ("r", "k", "v", "wd", "ad", "gd"):
        ins += [_tiled(pp[name], tile), _tiled(_shift_rows(pp[name]), tile)]
    ins += [_tiled(pp["h"], tile), _tiled(pp["cg"], tile)]
    ins += [_shared(mp[name]) for name in ("r", "k", "v", "wd", "ad", "gd")]
    ins += [_shared(a) for a in (wts["rw_w0"], w_up, wts["rw_a0"], a_up, g_up, wts["rw_k_k"], wts["rw_k_a"])]
    outs = [_tiled_out(t, RW_WIDTH, tile)] * 8
    r, decay, k_h, v, a, b, gate, z = _fused(pre, ins, outs, (t // tile,), "rwkv_pre")

    y = _rwkv_scan(r, decay, k_h, v, a, b)

    conv_w = wts["sc_conv_w"][0]
    r_k = wts["rw_r_k"].reshape(1, RW_WIDTH)

    def post(y, r, k_h, v, gate, bg, z, z1, z2, lnx_g, lnx_b, r_k, c0, c1, c2):
        mean = _head_sum(y) * (1.0 / HEAD_DIM)
        yc = y - mean
        var = _head_sum(yc * yc) * (1.0 / HEAD_DIM)
        yn = yc * lax.rsqrt(var + RW_GN_EPS) * lnx_g + lnx_b
        bonus = _head_sum(r * k_h * r_k) * v
        return (yn + bonus) * gate, bg * (c0 * z2 + c1 * z1 + c2 * z)

    ins = [_tiled(a_, tile) for a_ in (y, r, k_h, v, gate, pp["bg"])]
    ins += [_tiled(a_, tile) for a_ in (z, _shift_rows(z, 1), _shift_rows(z, 2))]
    ins += [_shared(a_) for a_ in (wts["rw_lnx_g"], wts["rw_lnx_b"], r_k, conv_w[0:1], conv_w[1:2], conv_w[2:3])]
    y_a, y_b = _fused(post, ins, [_tiled_out(t, RW_WIDTH, tile)] * 2, (t // tile,), "rwkv_post")
    return _linear(jnp.concatenate([y_a, y_b], axis=1), big["ab_w_out"][0], wts["ab_w_out"][0], "ab_out")


def _t5_bucket_np(dist):
    exact = N_BUCKETS // 2
    logd = np.log(np.maximum(dist, 1).astype(np.float32) / exact) / math.log(MAX_DISTANCE / exact)
    large = np.minimum(exact + (logd * (N_BUCKETS - exact)).astype(np.int32), N_BUCKETS - 1)
    return np.where(dist < exact, dist, large)


def _merge_groups(os_, lses, name):
    t = os_[0].shape[0]
    tile = ROW_TILE

    def fn(o0, o1, o2, l0, l1, l2):
        lane = lax.broadcasted_iota(jnp.int32, (1, LANES), 1)
        lo = lane < HEAD_DIM
        ls = [jnp.where(lo, l[0], l[1]) for l in (l0, l1, l2)]
        m = jnp.maximum(jnp.maximum(ls[0], ls[1]), ls[2])
        es = [jnp.exp(l - m) for l in ls]
        den = es[0] + es[1] + es[2]
        return ((es[0] * o0 + es[1] * o1 + es[2] * o2) / den,)

    ins = [(o, (tile, LANES), lambda i, hp: (i, hp), "t") for o in os_]
    ins += [(l, (2, tile, LANES), lambda i, hp: (hp, i, 0), "t") for l in lses]
    outs = [((t, DIL_WIDTH), (tile, LANES), lambda i, hp: (i, hp), "t")]
    return _fused(fn, ins, outs, (t // tile, N_PAIRS), name)[0]


def _dilated_mixer(u, big, wts):
    p = _linear(u, big["dil_w_qkv"][0], wts["dil_w_qkv"][0], "dil_qkv")
    qi = np.arange(BLOCK)[:, None]
    ki = np.arange(2 * BLOCK)[None, :]
    rel = BLOCK + qi - ki
    biases = []
    for g, (window, dil) in enumerate(DIL_PATTERNS):
        span = window // dil
        bucket = _t5_bucket_np(np.clip(rel, 0, span) * dil).reshape(-1)
        onehot = jnp.asarray(np.eye(N_BUCKETS, dtype=np.float32)[bucket])
        table = wts["rel_bias"][:, g * N_HEADS:(g + 1) * N_HEADS]
        bias = jnp.dot(onehot, table, precision=lax.Precision.HIGHEST)
        biases.append(jnp.transpose(bias.reshape(BLOCK, 2 * BLOCK, N_HEADS), (2, 0, 1)))
    os_, lses = _dilated_attention(p, tuple(biases))
    o = _merge_groups(os_, lses, "dil_merge")
    return _linear(o, big["dil_w_out"][0], wts["dil_w_out"][0], "dil_out")


def _forward_local(x, mods, big, wts, target):
    u = _modulate(x, mods[0, 1], mods[0, 0], "mod_in")
    for i in range(DEPTH):
        sh2, sc2, g1, g2 = mods[i, 3], mods[i, 4], mods[i, 2], mods[i, 5]
        y = _rwkv_shortconv(u, big, wts) if i % 2 == 0 else _dilated_mixer(u, big, wts)
        x, u = _resid_ln_mod(x, y, g1, wts["ln_g"][i, 0:1], wts["ln_b"][i, 0:1], sc2, sh2, f"ln_mix{i}")
        y = _mlp(u, big["mlp_w1"][i], wts["mlp_w1"][i], big["mlp_w2"][i], wts["mlp_w2"][i], f"mlp{i}")
        if i + 1 < DEPTH:
            x, u = _resid_ln_mod(x, y, g2, wts["ln_g"][i, 1:2], wts["ln_b"][i, 1:2],
                                 mods[i + 1, 1], mods[i + 1, 0], f"ln_mlp{i}")
        else:
            return _resid_ln_loss(x, y, g2, wts["ln_g"][i, 1:2], wts["ln_b"][i, 1:2], target, "ln_loss")


SHARDED = {"ab_w_in": 2, "ab_w_out": 1, "dil_w_qkv": 2, "dil_w_out": 2, "mlp_w1": 2, "mlp_w2": 1,
           "ln_g": 2, "ln_b": 2, "rw_w_up": 2, "rw_a_up": 2, "rw_g_up": 2, "sc_conv_w": 2}
BIG = ("ab_w_in", "ab_w_out", "dil_w_qkv", "dil_w_out", "mlp_w1", "mlp_w2")
SMALL_SHARDED = ("ln_g", "ln_b", "rw_w_up", "rw_a_up", "rw_g_up", "sc_conv_w")
REPLICATED = ("ada_b", "rw_mu", "rw_w0", "rw_a0", "rw_k_k", "rw_k_a", "rw_r_k", "rw_lnx_g", "rw_lnx_b", "rel_bias")
WEIGHT_ORDER = ("ada_w", "ada_b", "ln_g", "ln_b", "ab_w_in", "rw_mu", "rw_w0", "rw_w_up", "rw_a0", "rw_a_up",
                "rw_g_up", "rw_k_k", "rw_k_a", "rw_r_k", "rw_lnx_g", "rw_lnx_b", "sc_conv_w", "ab_w_out",
                "dil_w_qkv", "dil_w_out", "rel_bias", "mlp_w1", "mlp_w2")


PACK_ROWS = 16


def _rows_of(n_elems):
    return -(-n_elems // (ROW_W * PACK_ROWS)) * PACK_ROWS


def _to_rows(a):
    flat = a.reshape(-1)
    rows = _rows_of(flat.shape[0])
    if rows * ROW_W != flat.shape[0]:
        flat = jnp.pad(flat, (0, rows * ROW_W - flat.shape[0]))
    return flat.reshape(rows, ROW_W)


def _from_rows(rows, shape):
    n = int(np.prod(shape))
    return rows.reshape(-1)[:n].reshape(shape)


def _split_chips(full, axis):
    shp = full.shape
    parts = full.reshape(shp[:axis] + (N_CHIPS, shp[axis] // N_CHIPS) + shp[axis + 1:])
    return jnp.moveaxis(parts, axis, 0)


def _join_chips(parts, axis):
    moved = jnp.moveaxis(parts, 0, axis)
    shp = moved.shape
    return moved.reshape(shp[:axis] + (shp[axis] * shp[axis + 1],) + shp[axis + 2:])


def _pack_rows(arrays):
    blocks = [_to_rows(a) for a in arrays]
    total = sum(b.shape[0] for b in blocks)
    pad = (-total) % 256
    if pad:
        blocks.append(jnp.zeros((pad, ROW_W), blocks[0].dtype))
    return jnp.concatenate(blocks, axis=0)


def _unpack_rows(buf, shapes):
    out, r0 = [], 0
    for shp in shapes:
        n = _rows_of(int(np.prod(shp)))
        out.append(_from_rows(buf[r0:r0 + n], shp))
        r0 += n
    return out


def _as2d(a):
    return a.reshape(-1, a.shape[-1])


def kernel(x, c, ada_w, ada_b, ln_g, ln_b, ab_w_in, rw_mu, rw_w0, rw_w_up, rw_a0, rw_a_up, rw_g_up, rw_k_k, rw_k_a, rw_r_k, rw_lnx_g, rw_lnx_b, sc_conv_w, ab_w_out, dil_w_qkv, dil_w_out, rel_bias, mlp_w1, mlp_w2, loss_target, m_ada_w, m_ada_b, m_ln_g, m_ln_b, m_ab_w_in, m_rw_mu, m_rw_w0, m_rw_w_up, m_rw_a0, m_rw_a_up, m_rw_g_up, m_rw_k_k, m_rw_k_a, m_rw_r_k, m_rw_lnx_g, m_rw_lnx_b, m_sc_conv_w, m_ab_w_out, m_dil_w_qkv, m_dil_w_out, m_rel_bias, m_mlp_w1, m_mlp_w2, v_ada_w, v_ada_b, v_ln_g, v_ln_b, v_ab_w_in, v_rw_mu, v_rw_w0, v_rw_w_up, v_rw_a0, v_rw_a_up, v_rw_g_up, v_rw_k_k, v_rw_k_a, v_rw_r_k, v_rw_lnx_g, v_rw_lnx_b, v_sc_conv_w, v_ab_w_out, v_dil_w_qkv, v_dil_w_out, v_rel_bias, v_mlp_w1, v_mlp_w2):
    args = dict(locals())
    w_in = {n: args[n] for n in WEIGHT_ORDER}
    m_in = {n: args["m_" + n] for n in WEIGHT_ORDER}
    v_in = {n: args["v_" + n] for n in WEIGHT_ORDER}
    me = _me()
    chip = _chip_of(me)
    dev = _dev_of(me)

    c_all = _dev_all_gather(c, "gather_c")[:, 0, :]
    n_col = ada_w.shape[2]
    ada_b_cols = lax.dynamic_slice_in_dim(ada_b, chip * n_col, n_col, axis=1)[:, None, :]
    mod_cols = _ada_fwd(c_all, ada_w, ada_b_cols)

    big_buf = _pack_rows([w_in[n].astype(BF16) for n in BIG])
    big_all = _two_level_gather(big_buf, "gather_big")
    small_buf = _pack_rows([mod_cols] + [w_in[n] for n in SMALL_SHARDED])
    small_all = _chip_all_gather(small_buf, "gather_small")

    wts = {n: w_in[n] for n in REPLICATED}
    big = {}
    big_parts = zip(*[_unpack_rows(big_all[j], [w_in[n].shape for n in BIG]) for j in range(N_CHIPS)])
    for n, parts in zip(BIG, big_parts):
        big[n] = _join_chips(jnp.stack(parts), SHARDED[n])
        wts[n] = jnp.zeros(big[n].shape, F32)
    small_shapes = [mod_cols.shape] + [w_in[n].shape for n in SMALL_SHARDED]
    small_parts = list(zip(*[_unpack_rows(small_all[j], small_shapes) for j in range(N_CHIPS)]))
    for n, parts in zip(SMALL_SHARDED, small_parts[1:]):
        wts[n] = _join_chips(jnp.stack(parts), SHARDED[n])
    mod_all = _join_chips(jnp.stack(small_parts[0]), 2)
    mods = lax.dynamic_slice_in_dim(mod_all, dev, 1, axis=1).reshape(DEPTH, 6, 1, D_MODEL)

    def local_loss(xv, modv, wv):
        return _forward_local(xv, modv, big, wv, loss_target[0])[0, 0]

    loss_local, (grad_x, dmods, dw) = jax.value_and_grad(local_loss, argnums=(0, 1, 2))(x[0], mods, wts)
    loss = lax.psum(loss_local, ("x", "y", "c"))

    small_row = jnp.concatenate([dmods.reshape(-1)] + [dw[n].reshape(-1) for n in REPLICATED[1:]])
    n_small = small_row.shape[0]
    n_small_pad = -(-n_small // LANES) * LANES
    small_row = jnp.pad(small_row, (0, n_small_pad - n_small))[None, :]
    rows_all = _dev_all_gather(small_row, "gather_small_grads")
    small_sum = _sum_slots(rows_all, "sum_small_grads")

    dmod_all = rows_all[:, 0, :DEPTH * 6 * D_MODEL].reshape(N_DEV, DEPTH, 6 * D_MODEL)
    dmod_cols = lax.dynamic_slice_in_dim(dmod_all, chip * n_col, n_col, axis=2)
    dmod_cols = jnp.pad(jnp.moveaxis(dmod_cols, 0, 1), ((0, 0), (0, LANES - N_DEV), (0, 0)))
    c_all_t = jnp.pad(c_all.T, ((0, 0), (0, LANES - N_DEV)))
    grads = {"ada_w": _ada_grad(c_all_t, dmod_cols)}
    grads["ada_b"] = small_sum[0, :DEPTH * 6 * D_MODEL].reshape(ada_b.shape)
    r0 = DEPTH * 6 * D_MODEL
    for n in REPLICATED[1:]:
        size = int(np.prod(w_in[n].shape))
        grads[n] = small_sum[0, r0:r0 + size].reshape(w_in[n].shape)
        r0 += size

    sharded_names = BIG + SMALL_SHARDED
    per_chip = [_split_chips(dw[n], SHARDED[n]) for n in sharded_names]
    send = jnp.stack([_pack_rows([pc[j].astype(BF16) for pc in per_chip]) for j in range(N_CHIPS)])
    n_rows = send.shape[1]
    send = send.reshape(N_CHIPS, 2, n_rows // 2, ROW_W)
    theirs = _core_halves(send, "swap_halves")
    mine = lax.dynamic_index_in_dim(send, me[2], 1, keepdims=False)
    chip_part = _add_pairs(mine, theirs, "sum_cores")
    recv = _chip_scatter(chip_part, "scatter_grads")
    half_sum = _sum_slots(recv, "sum_chips")
    g_rows = _core_all_gather(half_sum, "gather_halves").reshape(n_rows, ROW_W)
    for n, g in zip(sharded_names, _unpack_rows(g_rows, [w_in[n].shape for n in sharded_names])):
        grads[n] = g

    deltas, new_m, new_v = {}, {}, {}
    for n in WEIGHT_ORDER:
        shp = w_in[n].shape
        d, nm, nv = _adamw(_as2d(w_in[n]), _as2d(grads[n]), _as2d(m_in[n]), _as2d(v_in[n]), "adamw_" + n)
        deltas[n], new_m[n], new_v[n] = d.reshape(shp), nm.reshape(shp), nv.reshape(shp)

    return (loss, grad_x[None], *[grads[n] for n in WEIGHT_ORDER], *[deltas[n] for n in WEIGHT_ORDER],
            *[new_m[n] for n in WEIGHT_ORDER], *[new_v[n] for n in WEIGHT_ORDER])
```

```python
import functools
import math

import numpy as np
import jax
import jax.numpy as jnp
from jax import lax
from jax.experimental import pallas as pl
from jax.experimental.pallas import tpu as pltpu

F32 = jnp.float32
BF16 = jnp.bfloat16
MESH = pl.DeviceIdType.MESH

D_MODEL = 1024
DEPTH = 2
RW_WIDTH = 512
HEAD_DIM = 64
N_HEADS = 8
RW_DECAY_RANK = 64
RW_ICLR_RANK = 64
RW_GATE_RANK = 160
RW_GN_EPS = 64e-5
RW_PROJ = 3 * RW_WIDTH + RW_DECAY_RANK + RW_ICLR_RANK + RW_GATE_RANK
SC_WIDTH = 512
AB_PROJ = RW_PROJ + 3 * SC_WIDTH
DIL_PATTERNS = ((128, 1), (512, 4), (2048, 16))
N_GROUPS = 3
DIL_WIDTH = 512
DIL_PROJ = N_GROUPS * 3 * DIL_WIDTH
BLOCK = 128
N_BUCKETS = 32
MAX_DISTANCE = 2048
D_FF = 4 * D_MODEL
ALPHA = (2 * DEPTH) ** 0.25
LN_EPS = 1e-5
ADAM_LR = 0.001
ADAM_B1 = 0.9
ADAM_B2 = 0.999
ADAM_EPS = 1e-08
ADAM_WD = 0.01
ADAM_STEP = 10

N_CHIPS = 4
N_DEV = 8
LANES = 128
SUBLANES = 8
ROW_W = 1024
SCAN_CHUNK = 16
VMEM_LIMIT = 48 * 1024 * 1024
NEG_BIG = -1e30


def _pick(n, cands):
    for c in cands:
        if n % c == 0:
            return c
    return n


def _cparams(sem=None, vmem=None):
    return pltpu.CompilerParams(dimension_semantics=sem, vmem_limit_bytes=vmem)


_DOT_DIMS = {
    "nn": (((1,), (0,)), ((), ())),
    "nt": (((1,), (1,)), ((), ())),
    "tn": (((0,), (0,)), ((), ())),
}


def _mm(a, b, mode, name):
    if mode == "nn":
        (m, k), (_, n) = a.shape, b.shape
    elif mode == "nt":
        (m, k), (n, _) = a.shape, b.shape
    else:
        (k, m), (_, n) = a.shape, b.shape
    tm = _pick(m, (1024, 512, 256, 128))
    tn = _pick(n, (1024, 768, 512, 384, 256, 128))
    tk = _pick(k, (1024, 512, 256, 128) if mode == "tn" else (2048, 1024, 512, 256, 128))
    nk = k // tk
    if mode == "tn":
        a_spec = pl.BlockSpec((tk, tm), lambda i, j, kk: (kk, i))
    else:
        a_spec = pl.BlockSpec((tm, tk), lambda i, j, kk: (i, kk))
    if mode == "nt":
        b_spec = pl.BlockSpec((tn, tk), lambda i, j, kk: (j, kk))
    else:
        b_spec = pl.BlockSpec((tk, tn), lambda i, j, kk: (kk, j))
    dims = _DOT_DIMS[mode]

    def body(a_ref, b_ref, o_ref):
        part = lax.dot_general(a_ref[...].astype(BF16), b_ref[...].astype(BF16), dims, preferred_element_type=F32)
        if nk == 1:
            o_ref[...] = part
        else:
            kk = pl.program_id(2)

            @pl.when(kk == 0)
            def _():
                o_ref[...] = part

            @pl.when(kk > 0)
            def _():
                o_ref[...] += part

    return pl.pallas_call(
        body, name=name, grid=(m // tm, n // tn, nk),
        in_specs=[a_spec, b_spec],
        out_specs=pl.BlockSpec((tm, tn), lambda i, j, kk: (i, j)),
        out_shape=jax.ShapeDtypeStruct((m, n), F32),
        compiler_params=_cparams(("parallel", "parallel", "arbitrary"), VMEM_LIMIT),
    )(a, b)


def _linear(x, w, slot, name):
    @jax.custom_vjp
    def op(x, w, slot):
        return _mm(x, w, "nn", name + "_fwd")

    def fwd(x, w, slot):
        return _mm(x, w, "nn", name + "_fwd"), (x, w)

    def bwd(res, dy):
        x, w = res
        return _mm(dy, w, "nt", name + "_dx"), jnp.zeros_like(w), _mm(x, dy, "tn", name + "_dw")

    op.defvjp(fwd, bwd)
    return op(x, w, slot)


def _const_map(ndim):
    return lambda *g: (0,) * ndim


def _first_step(n_grid):
    return functools.reduce(jnp.logical_and, [pl.program_id(d) == 0 for d in range(n_grid)])


def _fused(fn, ins, outs, grid, name):
    arrays = [i[0] for i in ins]
    n_in, n_out, n_grid = len(ins), len(outs), len(grid)
    in_specs = [pl.BlockSpec(bs, im) for (_, bs, im, _) in ins]
    out_specs = [pl.BlockSpec(bs, im) for (_, bs, im, _) in outs]
    out_shapes = [jax.ShapeDtypeStruct(s, F32) for (s, _, _, _) in outs]
    sem = ("arbitrary",) * n_grid

    def fwd_call(*xs):
        def body(*refs):
            vals = [r[...] for r in refs[:n_in]]
            ys = fn(*vals)
            first = _first_step(n_grid)
            for o_ref, y, (_, _, _, kind) in zip(refs[n_in:], ys, outs):
                if kind == "t":
                    o_ref[...] = y
                else:
                    @pl.when(first)
                    def _(o_ref=o_ref):
                        o_ref[...] = jnp.zeros_like(o_ref)

                    o_ref[...] += y

        return pl.pallas_call(
            body, name=name + "_fwd", grid=grid, in_specs=in_specs, out_specs=out_specs,
            out_shape=out_shapes, compiler_params=_cparams(sem, VMEM_LIMIT))(*xs)

    def bwd_call(xs, dys):
        d_specs = [pl.BlockSpec(bs, im) for (_, bs, im, _) in outs]
        g_specs = [pl.BlockSpec(bs, im) for (_, bs, im, _) in ins]
        g_shapes = [jax.ShapeDtypeStruct(a.shape, F32) for a in arrays]

        def body(*refs):
            vals = [r[...] for r in refs[:n_in]]
            dvals = tuple(r[...] for r in refs[n_in:n_in + n_out])
            _, vjp = jax.vjp(lambda *v: tuple(fn(*v)), *vals)
            gs = vjp(dvals)
            first = _first_step(n_grid)
            for g_ref, g, (_, _, _, kind) in zip(refs[n_in + n_out:], gs, ins):
                if kind == "t":
                    g_ref[...] = g
                else:
                    @pl.when(first)
                    def _(g_ref=g_ref):
                        g_ref[...] = jnp.zeros_like(g_ref)

                    g_ref[...] += g

        return pl.pallas_call(
            body, name=name + "_bwd", grid=grid, in_specs=in_specs + d_specs, out_specs=g_specs,
            out_shape=g_shapes, compiler_params=_cparams(sem, VMEM_LIMIT))(*xs, *dys)

    @jax.custom_vjp
    def op(*xs):
        return tuple(fwd_call(*xs))

    def op_fwd(*xs):
        return tuple(fwd_call(*xs)), xs

    def op_bwd(xs, dys):
        return tuple(bwd_call(xs, dys))

    op.defvjp(op_fwd, op_bwd)
    return op(*arrays)


def _tiled(a, tile, cols=None, col_block=0):
    cols = a.shape[1] if cols is None else cols
    return (a, (tile, cols), lambda i, cb=col_block: (i, cb), "t")


def _shared(a):
    return (a, a.shape, _const_map(a.ndim), "b")


def _tiled_out(rows, cols, tile):
    return ((rows, cols), (tile, cols), lambda i: (i, 0), "t")


@jax.custom_vjp
def _bdot(x, w):
    return jnp.dot(x.astype(BF16), w.astype(BF16), preferred_element_type=F32)


def _bdot_fwd(x, w):
    return _bdot(x, w), (x, w)


def _bdot_bwd(res, dy):
    x, w = res
    dyb = dy.astype(BF16)
    dx = lax.dot_general(dyb, w.astype(BF16), _DOT_DIMS["nt"], preferred_element_type=F32)
    dw = lax.dot_general(x.astype(BF16), dyb, _DOT_DIMS["tn"], preferred_element_type=F32)
    return dx, dw


_bdot.defvjp(_bdot_fwd, _bdot_bwd)


def _head_sum(x):
    n = x.shape[-1]
    hi = lax.broadcasted_iota(jnp.int32, (n, n), 0) // HEAD_DIM
    hj = lax.broadcasted_iota(jnp.int32, (n, n), 1) // HEAD_DIM
    e = (hi == hj).astype(F32)
    return jnp.dot(x, e, precision=lax.Precision.HIGHEST, preferred_element_type=F32)


def _softplus(x):
    return jnp.maximum(x, 0.0) + jnp.log1p(jnp.exp(-jnp.abs(x)))


def _layer_norm_rows(z, g, b):
    mu = jnp.mean(z, axis=-1, keepdims=True)
    zc = z - mu
    var = jnp.mean(zc * zc, axis=-1, keepdims=True)
    return zc * lax.rsqrt(var + LN_EPS) * g + b


N_PAIRS = N_HEADS // 2


def _scan_consts():
    k = lax.broadcasted_iota(jnp.int32, (HEAD_DIM, LANES), 0)
    j = lax.broadcasted_iota(jnp.int32, (HEAD_DIM, LANES), 1)
    diag = (j % HEAD_DIM) == k
    jj = lax.broadcasted_iota(jnp.int32, (LANES, LANES), 0) // HEAD_DIM
    ll = lax.broadcasted_iota(jnp.int32, (LANES, LANES), 1) // HEAD_DIM
    same_head = (jj == ll).astype(BF16)
    lane_lo = j < HEAD_DIM
    return diag, same_head, lane_lo


def _unrolled(n, body, carry):
    for i in range(n):
        carry = body(i, carry)
    return carry


def _fill_cols(srcs, col_ref, diag_bf, same_head, n_steps):
    del diag_bf, same_head
    assert n_steps == 16
    blocks = []
    for src in srcs:
        x = src[...]
        hi = x.astype(BF16).astype(F32)
        r1 = x - hi
        mid = r1.astype(BF16).astype(F32)
        x48 = jnp.concatenate([hi, mid, r1 - mid], axis=0)
        for hp in range(N_PAIRS):
            xp = x48[:, hp * LANES:(hp + 1) * LANES]
            y = jnp.concatenate([xp, pltpu.roll(xp, HEAD_DIM, 1), jnp.zeros((32, LANES), F32)], axis=0)
            blocks.append(y.T[:HEAD_DIM].astype(BF16))
    lhs = jnp.concatenate(blocks, axis=0)
    j = lax.broadcasted_iota(jnp.int32, (LANES, LANES), 0)
    lane_head = lax.broadcasted_iota(jnp.int32, (LANES, LANES), 1) // HEAD_DIM
    for t in range(n_steps):
        pick = jnp.logical_and(j < 96, jnp.logical_and(j % 16 == t, j // 48 == lane_head))
        out = jnp.dot(lhs, pick.astype(BF16), preferred_element_type=F32)
        for vi in range(len(srcs)):
            for hp in range(N_PAIRS):
                r0 = (vi * N_PAIRS + hp) * HEAD_DIM
                col_ref[vi, hp, t] = out[r0:r0 + HEAD_DIM]


def _scan_fwd_call(r, w, k, v, a, b, wsrc):
    t_len = r.shape[0]
    ch = SCAN_CHUNK
    n_ch = t_len // ch
    w_rows, w_cols = wsrc.shape
    half = w_rows // 2
    chunks = _n_chunks(half)
    rc = half // chunks
    transfers = [(q, kk) for q in range(chunks) for kk in range(len(CHIP_FLIPS))]
    hand_on_step = [max(1, min(n_ch - 1, (i + 1) * (n_ch - 8) // len(transfers) + 3)) for i in range(len(transfers))]

    def body(r_ref, w_ref, k_ref, v_ref, a_ref, b_ref, src_ref, y_ref, ck_ref, g_ref,
             st_ref, col_ref, send1, recv1, send2, recv2):
        c = pl.program_id(0)
        me = _me()
        core = me[2]
        sibling = _flip(me, CORE_FLIP)

        def over_ici(q, kk):
            return pltpu.make_async_remote_copy(
                src_ref=src_ref.at[core, pl.ds(q * rc, rc)], dst_ref=g_ref.at[_chip_of(me), core, pl.ds(q * rc, rc)],
                send_sem=send1.at[kk * chunks + q], recv_sem=recv1.at[kk * chunks + q],
                device_id=_flip(me, CHIP_FLIPS[kk]), device_id_type=MESH)

        def hand_on(q, kk):
            landed = g_ref.at[_chip_of(_flip(me, CHIP_FLIPS[kk])), core, pl.ds(q * rc, rc)]
            return pltpu.make_async_remote_copy(
                src_ref=landed, dst_ref=landed, send_sem=send2.at[kk * chunks + q], recv_sem=recv2.at[kk * chunks + q],
                device_id=sibling, device_id_type=MESH)

        @pl.when(c == 0)
        def _():
            st_ref[...] = jnp.zeros_like(st_ref)
            for q, kk in transfers:
                over_ici(q, kk).start()

        for step in sorted(set(hand_on_step)):
            @pl.when(c == step)
            def _(step=step):
                for (q, kk), s in zip(transfers, hand_on_step):
                    if s == step:
                        over_ici(q, kk).wait_recv()
                        hand_on(q, kk).start()

        ck_ref[0] = st_ref[...]
        diag, same_head, _ = _scan_consts()
        diag_bf = diag.astype(BF16)
        _fill_cols((w_ref, a_ref, b_ref, k_ref, r_ref), col_ref, diag_bf, same_head, ch)

        def step(t, states):
            new = []
            for hp in range(N_PAIRS):
                lanes = slice(hp * LANES, (hp + 1) * LANES)
                s = states[hp]
                sa = jnp.sum(s * col_ref[1, hp, t], axis=0, keepdims=True)
                s = s * col_ref[0, hp, t] + col_ref[2, hp, t] * sa + col_ref[3, hp, t] * v_ref[pl.ds(t, 1), lanes]
                y_ref[pl.ds(t, 1), lanes] = jnp.sum(s * col_ref[4, hp, t], axis=0, keepdims=True)
                new.append(s)
            return tuple(new)

        states = _unrolled(ch, step, tuple(st_ref[hp] for hp in range(N_PAIRS)))
        for hp in range(N_PAIRS):
            st_ref[hp] = states[hp]

        @pl.when(c == n_ch - 1)
        def _():
            for q, kk in transfers:
                hand_on(q, kk).wait_recv()
            for q, kk in transfers:
                over_ici(q, kk).wait_send()
                hand_on(q, kk).wait_send()

    row_spec = pl.BlockSpec((ch, RW_WIDTH), lambda c: (c, 0))
    any_spec = pl.BlockSpec(memory_space=pl.ANY)
    n_sem = len(transfers)
    y, ck, got = pl.pallas_call(
        body, name="rwkv_scan_fwd", grid=(n_ch,),
        in_specs=[row_spec] * 6 + [any_spec],
        out_specs=[row_spec, pl.BlockSpec((1, N_PAIRS, HEAD_DIM, LANES), lambda c: (c, 0, 0, 0)), any_spec],
        out_shape=[jax.ShapeDtypeStruct((t_len, RW_WIDTH), F32),
                   jax.ShapeDtypeStruct((n_ch, N_PAIRS, HEAD_DIM, LANES), F32),
                   jax.ShapeDtypeStruct((N_CHIPS, 2, half, w_cols), wsrc.dtype)],
        scratch_shapes=[pltpu.VMEM((N_PAIRS, HEAD_DIM, LANES), F32),
                        pltpu.VMEM((5, N_PAIRS, ch, HEAD_DIM, LANES), F32)]
        + [pltpu.SemaphoreType.DMA((n_sem,)) for _ in range(4)],
        compiler_params=_cparams(("arbitrary",), VMEM_LIMIT),
    )(r, w, k, v, a, b, wsrc.reshape(2, half, w_cols))
    return y, ck, got.reshape(N_CHIPS, w_rows, w_cols)


def _scan_bwd_call(r, w, k, v, a, b, ck, dy, gsend):
    t_len = r.shape[0]
    ch = SCAN_CHUNK
    n_ch = t_len // ch
    _, g_rows, g_cols = gsend.shape
    chunks = _n_chunks(g_rows)
    rc = g_rows // chunks
    transfers = [(q, kk) for q in range(chunks) for kk in range(len(CHIP_FLIPS))]

    def body(r_ref, w_ref, k_ref, v_ref, a_ref, b_ref, ck_ref, dy_ref, gs_ref,
             dr_ref, dw_ref, dk_ref, dv_ref, da_ref, db_ref, gr_ref,
             ds_ref, col_ref, sp_ref, sa_ref, send_sems, recv_sems):
        c = pl.program_id(0)
        me = _me()

        def to_chip(q, kk):
            peer = _flip(me, CHIP_FLIPS[kk])
            return pltpu.make_async_remote_copy(
                src_ref=gs_ref.at[_chip_of(peer), pl.ds(q * rc, rc)], dst_ref=gr_ref.at[_chip_of(me), pl.ds(q * rc, rc)],
                send_sem=send_sems.at[kk * chunks + q], recv_sem=recv_sems.at[kk * chunks + q],
                device_id=peer, device_id_type=MESH)

        @pl.when(c == 0)
        def _():
            ds_ref[...] = jnp.zeros_like(ds_ref)
            for q, kk in transfers:
                to_chip(q, kk).start()

        diag, same_head, lane_lo = _scan_consts()
        diag_bf = diag.astype(BF16)
        diag_f = diag.astype(F32)
        _fill_cols((w_ref, a_ref, b_ref, k_ref, r_ref), col_ref, diag_bf, same_head, ch)

        def replay(t, states):
            new = []
            for hp in range(N_PAIRS):
                lanes = slice(hp * LANES, (hp + 1) * LANES)
                s = states[hp]
                sp_ref[t, hp] = s
                sa = jnp.sum(s * col_ref[1, hp, t], axis=0, keepdims=True)
                sa_ref[pl.ds(t, 1), lanes] = sa
                new.append(s * col_ref[0, hp, t] + col_ref[2, hp, t] * sa
                           + col_ref[3, hp, t] * v_ref[pl.ds(t, 1), lanes])
            return tuple(new)

        _unrolled(ch, replay, tuple(ck_ref[0, hp] for hp in range(N_PAIRS)))

        def key_rows(ps):
            stacked = jnp.concatenate([p.astype(BF16) for p in ps], axis=0)
            q = jnp.dot(stacked, same_head, preferred_element_type=F32)
            return [jnp.sum(q[i * HEAD_DIM:(i + 1) * HEAD_DIM] * diag_f, axis=0, keepdims=True)
                    for i in range(len(ps))]

        def back(i, grads):
            t = ch - 1 - i
            new = []
            for hp in range(N_PAIRS):
                lanes = slice(hp * LANES, (hp + 1) * LANES)
                wc, ac, bc, kc, rc = (col_ref[vi, hp, t] for vi in range(5))
                sp = sp_ref[t, hp]
                sa = sa_ref[pl.ds(t, 1), lanes]
                vrow = v_ref[pl.ds(t, 1), lanes]
                dyrow = dy_ref[pl.ds(t, 1), lanes]
                st = sp * wc + bc * sa + kc * vrow
                g = grads[hp] + rc * dyrow
                dsa = jnp.sum(g * bc, axis=0, keepdims=True)
                dv_ref[pl.ds(t, 1), lanes] = jnp.sum(g * kc, axis=0, keepdims=True)
                rows = key_rows([st * dyrow, g * vrow, g * sa, g * sp, sp * dsa])
                for out_ref, row in zip((dr_ref, dk_ref, db_ref, dw_ref, da_ref), rows):
                    out_ref[pl.ds(t, 1), lanes] = row
                new.append(g * wc + ac * dsa)
            return tuple(new)

        grads = _unrolled(ch, back, tuple(ds_ref[hp] for hp in range(N_PAIRS)))
        for hp in range(N_PAIRS):
            ds_ref[hp] = grads[hp]

        @pl.when(c == n_ch - 1)
        def _():
            for q, kk in transfers:
                to_chip(q, kk).wait()

    row_spec = pl.BlockSpec((ch, RW_WIDTH), lambda c: (n_ch - 1 - c, 0))
    any_spec = pl.BlockSpec(memory_space=pl.ANY)
    out_sds = jax.ShapeDtypeStruct((t_len, RW_WIDTH), F32)
    n_sem = len(transfers)
    return pl.pallas_call(
        body, name="rwkv_scan_bwd", grid=(n_ch,),
        in_specs=[row_spec] * 6 + [pl.BlockSpec((1, N_PAIRS, HEAD_DIM, LANES), lambda c: (n_ch - 1 - c, 0, 0, 0)),
                                   row_spec, any_spec],
        out_specs=[row_spec] * 6 + [any_spec],
        out_shape=[out_sds] * 6 + [jax.ShapeDtypeStruct(gsend.shape, gsend.dtype)],
        scratch_shapes=[pltpu.VMEM((N_PAIRS, HEAD_DIM, LANES), F32),
                        pltpu.VMEM((5, N_PAIRS, ch, HEAD_DIM, LANES), F32),
                        pltpu.VMEM((ch, N_PAIRS, HEAD_DIM, LANES), F32),
                        pltpu.VMEM((ch, RW_WIDTH), F32),
                        pltpu.SemaphoreType.DMA((n_sem,)), pltpu.SemaphoreType.DMA((n_sem,))],
        compiler_params=_cparams(("arbitrary",), VMEM_LIMIT),
    )(r, w, k, v, a, b, ck, dy, gsend)


@jax.custom_vjp
def _rwkv_scan(r, w, k, v, a, b, wsrc, gslot):
    y, _, got = _scan_fwd_call(r, w, k, v, a, b, wsrc)
    return y, _set_slot(got, wsrc, _chip_of(_me())), gslot


def _rwkv_scan_fwd(r, w, k, v, a, b, wsrc, gslot):
    y, ck, got = _scan_fwd_call(r, w, k, v, a, b, wsrc)
    return (y, _set_slot(got, wsrc, _chip_of(_me())), gslot), (r, w, k, v, a, b, ck, wsrc)


def _rwkv_scan_bwd(res, cts):
    *saved, wsrc = res
    dy, _, gsend = cts
    *dins, got = _scan_bwd_call(*saved, dy, gsend)
    chip = _chip_of(_me())
    got = _set_slot(got, lax.dynamic_index_in_dim(gsend, chip, 0, keepdims=False), chip)
    return (*dins, jnp.zeros_like(wsrc), got)


_rwkv_scan.defvjp(_rwkv_scan_fwd, _rwkv_scan_bwd)


ATT_SCALE = HEAD_DIM ** -0.5
COLS_PER_POS = DIL_PROJ // DIL_WIDTH


def _att_masks():
    qi = lax.broadcasted_iota(jnp.int32, (BLOCK, BLOCK), 0)
    ki = lax.broadcasted_iota(jnp.int32, (BLOCK, BLOCK), 1)
    lane = lax.broadcasted_iota(jnp.int32, (1, LANES), 1)
    return ki <= qi, ki >= qi, lane


def _att_fwd_call(p, bias, g, dil):
    t_len = p.shape[0]
    l_len = t_len // dil
    nb = l_len // BLOCK
    pv = p.reshape(l_len, dil * DIL_PROJ)
    base = g * 3

    def body(q_ref, kc_ref, kp_ref, vc_ref, vp_ref, bias_ref, o_ref, lse_ref):
        n = pl.program_id(1)
        cur_ok, prev_band, lane = _att_masks()
        prev_ok = jnp.logical_and(prev_band, n > 0)
        for hp in range(N_PAIRS):
            lanes = slice(hp * LANES, (hp + 1) * LANES)
            q2 = q_ref[:, lanes].astype(BF16)
            kc = kc_ref[:, lanes].astype(BF16)
            kp = kp_ref[:, lanes].astype(BF16)
            vc = vc_ref[:, lanes].astype(BF16)
            vp = vp_ref[:, lanes].astype(BF16)
            o2 = jnp.zeros((BLOCK, LANES), F32)
            for hh in range(2):
                h = 2 * hp + hh
                mine = (lane // HEAD_DIM) == hh
                qm = jnp.where(mine, q2, jnp.zeros_like(q2))
                s_c = lax.dot_general(qm, kc, _DOT_DIMS["nt"], preferred_element_type=F32) * ATT_SCALE
                s_p = lax.dot_general(qm, kp, _DOT_DIMS["nt"], preferred_element_type=F32) * ATT_SCALE
                s_c = jnp.where(cur_ok, s_c + bias_ref[h, :, BLOCK:], NEG_BIG)
                s_p = jnp.where(prev_ok, s_p + bias_ref[h, :, :BLOCK], NEG_BIG)
                m = jnp.maximum(jnp.max(s_c, axis=-1, keepdims=True), jnp.max(s_p, axis=-1, keepdims=True))
                e_c = jnp.exp(s_c - m)
                e_p = jnp.exp(s_p - m)
                den = jnp.sum(e_c, axis=-1, keepdims=True) + jnp.sum(e_p, axis=-1, keepdims=True)
                o_h = (jnp.dot((e_c / den).astype(BF16), vc, preferred_element_type=F32)
                       + jnp.dot((e_p / den).astype(BF16), vp, preferred_element_type=F32))
                o2 = o2 + jnp.where(mine, o_h, 0.0)
                lse_ref[h] = jnp.broadcast_to(m + jnp.log(den), (BLOCK, LANES))
            o_ref[:, lanes] = o2

    def col(j):
        return lambda r, n: (n, r * COLS_PER_POS + base + j)

    def col_prev(j):
        return lambda r, n: (jnp.maximum(n - 1, 0), r * COLS_PER_POS + base + j)

    blk = (BLOCK, DIL_WIDTH)
    o, lse = pl.pallas_call(
        body, name=f"dil_att_fwd_g{g}", grid=(dil, nb),
        in_specs=[pl.BlockSpec(blk, col(0)), pl.BlockSpec(blk, col(1)), pl.BlockSpec(blk, col_prev(1)),
                  pl.BlockSpec(blk, col(2)), pl.BlockSpec(blk, col_prev(2)),
                  pl.BlockSpec(bias.shape, _const_map(3))],
        out_specs=[pl.BlockSpec(blk, lambda r, n: (n, r)),
                   pl.BlockSpec((N_HEADS, BLOCK, LANES), lambda r, n: (0, n, r))],
        out_shape=[jax.ShapeDtypeStruct((l_len, dil * DIL_WIDTH), F32),
                   jax.ShapeDtypeStruct((N_HEADS, l_len, dil * LANES), F32)],
        compiler_params=_cparams(("arbitrary", "arbitrary"), VMEM_LIMIT),
    )(pv, pv, pv, pv, pv, bias)
    return o.reshape(t_len, DIL_WIDTH), lse.reshape(N_HEADS, t_len, LANES)


def _att_bwd_call(p, bias, o, lse, do, dlse, g, dil):
    t_len = p.shape[0]
    l_len = t_len // dil
    nb = l_len // BLOCK
    pv = p.reshape(l_len, dil * DIL_PROJ)
    ov = o.reshape(l_len, dil * DIL_WIDTH)
    dov = do.reshape(l_len, dil * DIL_WIDTH)
    lsev = lse.reshape(N_HEADS, l_len, dil * LANES)
    dlsev = dlse.reshape(N_HEADS, l_len, dil * LANES)
    base = g * 3

    def body(q_ref, qn_ref, k_ref, v_ref, do_ref, don_ref, o_ref, on_ref, lse_ref, lsen_ref, dl_ref, dln_ref,
             bias_ref, dq_ref, dk_ref, dv_ref, dbias_ref, carry_ref):
        r = pl.program_id(0)
        n = pl.program_id(1)
        cur_ok, prev_band, lane = _att_masks()
        has_next = n + 1 < nb

        @pl.when(jnp.logical_and(r == 0, n == 0))
        def _():
            dbias_ref[...] = jnp.zeros_like(dbias_ref)

        @pl.when(n == 0)
        def _():
            carry_ref[...] = jnp.zeros_like(carry_ref)

        for hp in range(N_PAIRS):
            lanes = slice(hp * LANES, (hp + 1) * LANES)
            k2 = k_ref[:, lanes].astype(BF16)
            v2 = v_ref[:, lanes].astype(BF16)
            dk2 = jnp.zeros((BLOCK, LANES), F32)
            dv2 = jnp.zeros((BLOCK, LANES), F32)
            dq_cur = carry_ref[:, lanes]
            dq_next = jnp.zeros((BLOCK, LANES), F32)
            for hh in range(2):
                h = 2 * hp + hh
                mine = (lane // HEAD_DIM) == hh
                tiles = (
                    (q_ref, do_ref, o_ref, lse_ref, dl_ref, cur_ok, slice(BLOCK, 2 * BLOCK), None),
                    (qn_ref, don_ref, on_ref, lsen_ref, dln_ref, prev_band, slice(0, BLOCK), has_next),
                )
                for ti, (qr, dor, orf, lr, dlr, ok, bcols, gate) in enumerate(tiles):
                    q2 = qr[:, lanes].astype(BF16)
                    qm = jnp.where(mine, q2, jnp.zeros_like(q2))
                    do_f = jnp.where(mine, dor[:, lanes], 0.0)
                    dom = do_f.astype(BF16)
                    s = lax.dot_general(qm, k2, _DOT_DIMS["nt"], preferred_element_type=F32) * ATT_SCALE
                    s = s + bias_ref[h, :, bcols]
                    if gate is not None:
                        ok = jnp.logical_and(ok, gate)
                    pr = jnp.where(ok, jnp.exp(jnp.minimum(s - lr[h], 0.0)), 0.0)
                    dp = lax.dot_general(dom, v2, _DOT_DIMS["nt"], preferred_element_type=F32)
                    delta = jnp.sum(do_f * orf[:, lanes], axis=-1, keepdims=True)
                    dl = jnp.sum(dlr[h], axis=-1, keepdims=True)
                    ds = pr * (dp - delta + dl)
                    dsb = ds.astype(BF16)
                    dq_h = jnp.where(mine, jnp.dot(dsb, k2, preferred_element_type=F32), 0.0) * ATT_SCALE
                    if ti == 0:
                        dq_cur = dq_cur + dq_h
                    else:
                        dq_next = dq_next + dq_h
                    dk2 = dk2 + lax.dot_general(dsb, qm, _DOT_DIMS["tn"], preferred_element_type=F32) * ATT_SCALE
                    dv2 = dv2 + lax.dot_general(pr.astype(BF16), dom, _DOT_DIMS["tn"], preferred_element_type=F32)
                    dbias_ref[h, :, bcols] += ds
            dq_ref[:, lanes] = dq_cur
            carry_ref[:, lanes] = dq_next
            dk_ref[:, lanes] = dk2
            dv_ref[:, lanes] = dv2

    def nxt(n):
        return jnp.minimum(n + 1, nb - 1)

    blk = (BLOCK, DIL_WIDTH)
    hblk = (N_HEADS, BLOCK, LANES)
    qcol = lambda j: (lambda r, n: (n, r * COLS_PER_POS + base + j))
    q_next = lambda r, n: (nxt(n), r * COLS_PER_POS + base)
    rown = lambda r, n: (n, r)
    rown_next = lambda r, n: (nxt(n), r)
    hrow = lambda r, n: (0, n, r)
    hrow_next = lambda r, n: (0, nxt(n), r)
    sds = jax.ShapeDtypeStruct((l_len, dil * DIL_WIDTH), F32)
    dq, dk, dv, dbias = pl.pallas_call(
        body, name=f"dil_att_bwd_g{g}", grid=(dil, nb),
        in_specs=[pl.BlockSpec(blk, qcol(0)), pl.BlockSpec(blk, q_next),
                  pl.BlockSpec(blk, qcol(1)), pl.BlockSpec(blk, qcol(2)),
                  pl.BlockSpec(blk, rown), pl.BlockSpec(blk, rown_next),
                  pl.BlockSpec(blk, rown), pl.BlockSpec(blk, rown_next),
                  pl.BlockSpec(hblk, hrow), pl.BlockSpec(hblk, hrow_next),
                  pl.BlockSpec(hblk, hrow), pl.BlockSpec(hblk, hrow_next),
                  pl.BlockSpec(bias.shape, _const_map(3))],
        out_specs=[pl.BlockSpec(blk, rown)] * 3 + [pl.BlockSpec(bias.shape, _const_map(3))],
        out_shape=[sds, sds, sds, jax.ShapeDtypeStruct(bias.shape, F32)],
        scratch_shapes=[pltpu.VMEM((BLOCK, DIL_WIDTH), F32)],
        compiler_params=_cparams(("arbitrary", "arbitrary"), VMEM_LIMIT),
    )(pv, pv, pv, pv, dov, dov, ov, ov, lsev, lsev, dlsev, dlsev, bias)
    shp = (t_len, DIL_WIDTH)
    return dq.reshape(shp), dk.reshape(shp), dv.reshape(shp), dbias


def _att_all_groups(p, biases):
    outs = [_att_fwd_call(p, biases[g], g, dil) for g, (_, dil) in enumerate(DIL_PATTERNS)]
    return tuple(o for o, _ in outs), tuple(l for _, l in outs)


@jax.custom_vjp
def _dilated_attention(p, biases):
    return _att_all_groups(p, biases)


def _dilated_attention_fwd(p, biases):
    os_, lses = _att_all_groups(p, biases)
    return (os_, lses), (p, biases, os_, lses)


def _dilated_attention_bwd(res, cts):
    p, biases, os_, lses = res
    dos, dlses = cts
    parts, dbiases = [], []
    for g, (_, dil) in enumerate(DIL_PATTERNS):
        dq, dk, dv, dbias = _att_bwd_call(p, biases[g], os_[g], lses[g], dos[g], dlses[g], g, dil)
        parts += [dq, dk, dv]
        dbiases.append(dbias)
    return jnp.concatenate(parts, axis=1), tuple(dbiases)


_dilated_attention.defvjp(_dilated_attention_fwd, _dilated_attention_bwd)


def _me():
    return lax.axis_index("x"), lax.axis_index("y"), lax.axis_index("c")


def _flip(me, f):
    return tuple((1 - m) if b else m for m, b in zip(me, f))


def _chip_of(d):
    return 2 * d[0] + d[1]


def _dev_of(d):
    return 4 * d[0] + 2 * d[1] + d[2]


EXCHANGE_CHUNKS = 8
CHIP_FLIPS = ((1, 0, 0), (0, 1, 0), (1, 1, 0))
ALL_FLIPS = tuple((a, b, c) for a in (0, 1) for b in (0, 1) for c in (0, 1) if a or b or c)
CORE_FLIP = (0, 0, 1)


def _n_chunks(rows):
    return EXCHANGE_CHUNKS if rows % (EXCHANGE_CHUNKS * PACK_ROWS) == 0 else 1


def _exchange(src, n_slots, transfers, name):
    _, rows, cols = src.shape
    chunks = _n_chunks(rows)
    rc = rows // chunks
    n = len(transfers) * chunks

    def body(src_ref, dst_ref, send_sems, recv_sems):
        me = _me()
        copies = []
        for q in range(chunks):
            for kk, (f, src_slot, dst_slot) in enumerate(transfers):
                peer = _flip(me, f)
                cp = pltpu.make_async_remote_copy(
                    src_ref=src_ref.at[src_slot(me, peer), pl.ds(q * rc, rc)],
                    dst_ref=dst_ref.at[dst_slot(me, peer), pl.ds(q * rc, rc)],
                    send_sem=send_sems.at[kk * chunks + q], recv_sem=recv_sems.at[kk * chunks + q],
                    device_id=peer, device_id_type=MESH)
                cp.start()
                copies.append(cp)
        for cp in copies:
            cp.wait()

    return pl.pallas_call(
        body, name=name,
        out_shape=jax.ShapeDtypeStruct((n_slots, rows, cols), src.dtype),
        in_specs=[pl.BlockSpec(memory_space=pl.ANY)],
        out_specs=pl.BlockSpec(memory_space=pl.ANY),
        scratch_shapes=[pltpu.SemaphoreType.DMA((n,)), pltpu.SemaphoreType.DMA((n,))],
    )(src)


def _set_slot(buf, block, index):
    return lax.dynamic_update_slice_in_dim(buf, block[None].astype(buf.dtype), index, axis=0)


def _chip_all_gather(src, name):
    got = _exchange(src[None], N_CHIPS, [(f, lambda me, peer: 0, lambda me, peer: _chip_of(me)) for f in CHIP_FLIPS], name)
    return _set_slot(got, src, _chip_of(_me()))


def _dev_all_gather(src, name):
    got = _exchange(src[None], N_DEV, [(f, lambda me, peer: 0, lambda me, peer: _dev_of(me)) for f in ALL_FLIPS], name)
    return _set_slot(got, src, _dev_of(_me()))


def _chip_scatter(src, name):
    got = _exchange(src, N_CHIPS, [(f, lambda me, peer: _chip_of(peer), lambda me, peer: _chip_of(me))
                                   for f in CHIP_FLIPS], name)
    chip = _chip_of(_me())
    return _set_slot(got, lax.dynamic_index_in_dim(src, chip, 0, keepdims=False), chip)


def _core_halves(src, name):
    s, _, half, cols = src.shape
    transfers = [(CORE_FLIP, (lambda me, peer, j=j: 2 * j + peer[2]), (lambda me, peer, j=j: j)) for j in range(s)]
    return _exchange(src.reshape(2 * s, half, cols), s, transfers, name)


def _core_all_gather(src, name):
    got = _exchange(src[None], 2, [(CORE_FLIP, lambda me, peer: 0, lambda me, peer: me[2])], name)
    return _set_slot(got, src, _me()[2])


def _two_level_gather(src, name):
    rows, cols = src.shape
    half = rows // 2
    chunks = _n_chunks(half)
    rc = half // chunks
    n = len(CHIP_FLIPS) * chunks

    def body(src_ref, g_ref, send1, recv1, send2, recv2):
        me = _me()
        c = me[2]
        sibling = _flip(me, CORE_FLIP)
        first, second = [], []
        for q in range(chunks):
            for kk, f in enumerate(CHIP_FLIPS):
                peer = _flip(me, f)
                cp = pltpu.make_async_remote_copy(
                    src_ref=src_ref.at[c, pl.ds(q * rc, rc)], dst_ref=g_ref.at[_chip_of(me), c, pl.ds(q * rc, rc)],
                    send_sem=send1.at[kk * chunks + q], recv_sem=recv1.at[kk * chunks + q],
                    device_id=peer, device_id_type=MESH)
                cp.start()
                first.append((cp, _chip_of(peer), kk * chunks + q, q))
        for cp, origin, idx, q in first:
            cp.wait_recv()
            fw = pltpu.make_async_remote_copy(
                src_ref=g_ref.at[origin, c, pl.ds(q * rc, rc)], dst_ref=g_ref.at[origin, c, pl.ds(q * rc, rc)],
                send_sem=send2.at[idx], recv_sem=recv2.at[idx],
                device_id=sibling, device_id_type=MESH)
            fw.start()
            second.append(fw)
        for fw in second:
            fw.wait_recv()
        for cp, _, _, _ in first:
            cp.wait_send()
        for fw in second:
            fw.wait_send()

    got = pl.pallas_call(
        body, name=name,
        out_shape=jax.ShapeDtypeStruct((N_CHIPS, 2, half, cols), src.dtype),
        in_specs=[pl.BlockSpec(memory_space=pl.ANY)],
        out_specs=pl.BlockSpec(memory_space=pl.ANY),
        scratch_shapes=[pltpu.SemaphoreType.DMA((n,)) for _ in range(4)],
    )(src.reshape(2, half, cols))
    return _set_slot(got.reshape(N_CHIPS, rows, cols), src, _chip_of(_me()))


def _sum_slots(x, name):
    s, rows, cols = x.shape
    tile = _pick(rows, (512, 256, 128, 64, 32, 16, 8))

    def body(x_ref, o_ref):
        acc = x_ref[0].astype(F32)
        for i in range(1, s):
            acc = acc + x_ref[i].astype(F32)
        o_ref[...] = acc

    return pl.pallas_call(
        body, name=name, grid=(rows // tile,),
        in_specs=[pl.BlockSpec((s, tile, cols), lambda i: (0, i, 0))],
        out_specs=pl.BlockSpec((tile, cols), lambda i: (i, 0)),
        out_shape=jax.ShapeDtypeStruct((rows, cols), F32),
        compiler_params=_cparams(("parallel",), VMEM_LIMIT),
    )(x)


def _add_pairs(a, b, name):
    s, rows, cols = a.shape
    tile = _pick(rows, (512, 256, 128, 64, 32, 16, 8))

    def body(a_ref, b_ref, o_ref):
        o_ref[...] = (a_ref[...].astype(F32) + b_ref[...].astype(F32)).astype(o_ref.dtype)

    spec = pl.BlockSpec((1, tile, cols), lambda j, i: (j, i, 0))
    return pl.pallas_call(
        body, name=name, grid=(s, rows // tile),
        in_specs=[spec, spec], out_specs=spec,
        out_shape=jax.ShapeDtypeStruct(a.shape, a.dtype),
        compiler_params=_cparams(("parallel", "parallel"), VMEM_LIMIT),
    )(a, b)


def _adamw(w, g, m, v, name):
    rows, cols = w.shape
    tile = rows
    if rows * cols * 4 > 2 * 1024 * 1024:
        tile = _pick(rows, (256, 128, 64, 32, 16, 8))
    c1 = 1.0 / (1.0 - ADAM_B1 ** ADAM_STEP)
    c2 = 1.0 / (1.0 - ADAM_B2 ** ADAM_STEP)

    def body(w_ref, g_ref, m_ref, v_ref, d_ref, nm_ref, nv_ref):
        gv = g_ref[...]
        nm = ADAM_B1 * m_ref[...] + (1.0 - ADAM_B1) * gv
        nv = ADAM_B2 * v_ref[...] + (1.0 - ADAM_B2) * (gv * gv)
        m_hat = nm * c1
        v_hat = nv * c2
        d_ref[...] = -ADAM_LR * (m_hat / (jnp.sqrt(v_hat) + ADAM_EPS) + ADAM_WD * w_ref[...])
        nm_ref[...] = nm
        nv_ref[...] = nv

    spec = pl.BlockSpec((tile, cols), lambda i: (i, 0))
    sds = jax.ShapeDtypeStruct((rows, cols), F32)
    return pl.pallas_call(
        body, name=name, grid=(rows // tile,),
        in_specs=[spec] * 4, out_specs=[spec] * 3, out_shape=[sds] * 3,
        compiler_params=_cparams(("parallel",), VMEM_LIMIT),
    )(w, g, m, v)


def _ada_fwd(c_all, ada_w, ada_b_cols):
    n_col = ada_w.shape[2]

    def body(c_ref, w_ref, b_ref, o_ref):
        cv = c_ref[...]
        cond = (cv * jax.nn.sigmoid(cv)).astype(BF16)
        o_ref[0] = jnp.dot(cond, w_ref[0].astype(BF16), preferred_element_type=F32) + b_ref[0]

    return pl.pallas_call(
        body, name="ada_fwd", grid=(DEPTH,),
        in_specs=[pl.BlockSpec(c_all.shape, lambda i: (0, 0)),
                  pl.BlockSpec((1, D_MODEL, n_col), lambda i: (i, 0, 0)),
                  pl.BlockSpec((1, 1, n_col), lambda i: (i, 0, 0))],
        out_specs=pl.BlockSpec((1, N_DEV, n_col), lambda i: (i, 0, 0)),
        out_shape=jax.ShapeDtypeStruct((DEPTH, N_DEV, n_col), F32),
        compiler_params=_cparams(("parallel",), VMEM_LIMIT),
    )(c_all, ada_w, ada_b_cols)


def _ada_grad(c_all_t, dmod_cols):
    n_col = dmod_cols.shape[2]

    def body(c_ref, d_ref, o_ref):
        cv = c_ref[...]
        cond = cv * jax.nn.sigmoid(cv)
        o_ref[0] = jnp.dot(cond, d_ref[0], precision=lax.Precision.HIGHEST, preferred_element_type=F32)

    return pl.pallas_call(
        body, name="ada_grad", grid=(DEPTH,),
        in_specs=[pl.BlockSpec(c_all_t.shape, lambda i: (0, 0)),
                  pl.BlockSpec((1, LANES, n_col), lambda i: (i, 0, 0))],
        out_specs=pl.BlockSpec((1, D_MODEL, n_col), lambda i: (i, 0, 0)),
        out_shape=jax.ShapeDtypeStruct((DEPTH, D_MODEL, n_col), F32),
        compiler_params=_cparams(("parallel",), VMEM_LIMIT),
    )(c_all_t, dmod_cols)


ROW_TILE = 256


def _shift_rows(a, n=1):
    return jnp.pad(a, ((n, 0), (0, 0)))[:-n]


def _modulate(x, sc, sh, name):
    def fn(x, sc, sh):
        return (x * (1.0 + sc) + sh,)

    t = x.shape[0]
    return _fused(fn, [_tiled(x, ROW_TILE), _shared(sc), _shared(sh)],
                  [_tiled_out(t, D_MODEL, ROW_TILE)], (t // ROW_TILE,), name)[0]


def _resid_ln_mod(x, y, gate, ln_g, ln_b, sc, sh, name):
    def fn(x, y, gate, ln_g, ln_b, sc, sh):
        x1 = _layer_norm_rows(ALPHA * x + (1.0 + gate) * y, ln_g, ln_b)
        return x1, x1 * (1.0 + sc) + sh

    t = x.shape[0]
    return _fused(fn, [_tiled(x, ROW_TILE), _tiled(y, ROW_TILE)] + [_shared(a) for a in (gate, ln_g, ln_b, sc, sh)],
                  [_tiled_out(t, D_MODEL, ROW_TILE)] * 2, (t // ROW_TILE,), name)


def _resid_ln_loss(x, y, gate, ln_g, ln_b, target, name):
    def fn(x, y, gate, ln_g, ln_b, target):
        x1 = _layer_norm_rows(ALPHA * x + (1.0 + gate) * y, ln_g, ln_b)
        err = jnp.square(x1 - target)
        per_row = jnp.mean(err, axis=-1, keepdims=True)
        return (0.5 * jnp.sum(per_row, axis=0, keepdims=True),)

    t = x.shape[0]
    return _fused(fn, [_tiled(x, ROW_TILE), _tiled(y, ROW_TILE)] + [_shared(a) for a in (gate, ln_g, ln_b)]
                  + [_tiled(target, ROW_TILE)],
                  [((1, 1), (1, 1), _const_map(2), "a")], (t // ROW_TILE,), name)[0]


def _sq_relu(h, name):
    def fn(h):
        return (jnp.square(jnp.maximum(h, 0.0)),)

    t, f = h.shape
    return _fused(fn, [_tiled(h, ROW_TILE)], [_tiled_out(t, f, ROW_TILE)], (t // ROW_TILE,), name)[0]


def _mlp(u, w1, s1, w2, s2, name):
    h = _linear(u, w1, s1, name + "_w1")
    return _linear(_sq_relu(h, name + "_act"), w2, s2, name + "_w2")


AB_PIECES = (("r", 0, 512, 512), ("k", 512, 512, 512), ("v", 1024, 512, 512),
             ("wd", 1536, 64, 128), ("ad", 1600, 64, 128), ("gd", 1664, 160, 256),
             ("h", 1824, 512, 512), ("bg", 2336, 512, 512), ("cg", 2848, 512, 512))
AB_PAD_COLS = sum(p[3] for p in AB_PIECES)


def _regroup_cols(w):
    parts = []
    for _, start, width, padded in AB_PIECES:
        piece = w[..., start:start + width]
        if padded != width:
            piece = jnp.pad(piece, [(0, 0)] * (w.ndim - 1) + [(0, padded - width)])
        parts.append(piece)
    return jnp.concatenate(parts, axis=-1)


def _pad_rows(w, rows):
    return jnp.pad(w, ((0, rows - w.shape[0]), (0, 0)))


def _rwkv_shortconv(u, big, wts, wsrc, gslot):
    t = u.shape[0]
    p = _linear(u, _regroup_cols(big["ab_w_in"][0]), _regroup_cols(wts["ab_w_in"][0]), "ab_in")
    mu = _regroup_cols(jnp.pad(wts["rw_mu"], ((0, 0), (0, AB_PROJ - RW_PROJ))))
    w_up = _pad_rows(wts["rw_w_up"][0], 128)
    a_up = _pad_rows(wts["rw_a_up"][0], 128)
    g_up = _pad_rows(wts["rw_g_up"][0], 256)

    def pre(rp, rs, kp, ks, vp, vs, wdp, wds, adp, ads, gdp, gds, h, cg,
            mu_r, mu_k, mu_v, mu_w, mu_a, mu_g, w0, w_up, a0, a_up, g_up, k_k, k_a):
        def mix(pv, sv, m):
            return pv + m * (sv - pv)

        r, k, v = mix(rp, rs, mu_r), mix(kp, ks, mu_k), mix(vp, vs, mu_v)
        wd, ad, gd = mix(wdp, wds, mu_w), mix(adp, ads, mu_a), mix(gdp, gds, mu_g)
        logw = -_softplus(-(w0 + _bdot(jnp.tanh(wd), w_up))) - 0.5
        decay = jnp.exp(-jnp.exp(logw))
        iclr = jax.nn.sigmoid(a0 + _bdot(ad, a_up))
        gate = _bdot(jax.nn.sigmoid(gd), g_up)
        kk = k * k_k
        kk = kk / jnp.maximum(jnp.sqrt(_head_sum(kk * kk)), 1e-12)
        k_h = k * (1.0 + (iclr - 1.0) * k_a)
        return r, decay, k_h, v, -kk, kk * iclr, gate, cg * h

    tile = ROW_TILE
    names = [q[0] for q in AB_PIECES]
    cuts = list(np.cumsum([q[3] for q in AB_PIECES])[:-1])
    pp = dict(zip(names, jnp.split(p, cuts, axis=1)))
    mp = dict(zip(names, jnp.split(mu, cuts, axis=1)))

    ins = []
    for name in ("r", "k", "v", "wd", "ad", "gd"):
        ins += [_tiled(pp[name], tile), _tiled(_shift_rows(pp[name]), tile)]
    ins += [_tiled(pp["h"], tile), _tiled(pp["cg"], tile)]
    ins += [_shared(mp[name]) for name in ("r", "k", "v", "wd", "ad", "gd")]
    ins += [_shared(a) for a in (wts["rw_w0"], w_up, wts["rw_a0"], a_up, g_up, wts["rw_k_k"], wts["rw_k_a"])]
    outs = [_tiled_out(t, RW_WIDTH, tile)] * 8
    r, decay, k_h, v, a, b, gate, z = _fused(pre, ins, outs, (t // tile,), "rwkv_pre")

    y, gathered, token = _rwkv_scan(r, decay, k_h, v, a, b, wsrc, gslot)

    conv_w = wts["sc_conv_w"][0]
    r_k = wts["rw_r_k"].reshape(1, RW_WIDTH)

    def post(y, r, k_h, v, gate, bg, z, z1, z2, lnx_g, lnx_b, r_k, c0, c1, c2):
        mean = _head_sum(y) * (1.0 / HEAD_DIM)
        yc = y - mean
        var = _head_sum(yc * yc) * (1.0 / HEAD_DIM)
        yn = yc * lax.rsqrt(var + RW_GN_EPS) * lnx_g + lnx_b
        bonus = _head_sum(r * k_h * r_k) * v
        return (yn + bonus) * gate, bg * (c0 * z2 + c1 * z1 + c2 * z)

    ins = [_tiled(a_, tile) for a_ in (y, r, k_h, v, gate, pp["bg"])]
    ins += [_tiled(a_, tile) for a_ in (z, _shift_rows(z, 1), _shift_rows(z, 2))]
    ins += [_shared(a_) for a_ in (wts["rw_lnx_g"], wts["rw_lnx_b"], r_k, conv_w[0:1], conv_w[1:2], conv_w[2:3])]
    y_a, y_b = _fused(post, ins, [_tiled_out(t, RW_WIDTH, tile)] * 2, (t // tile,), "rwkv_post")
    out = _linear(jnp.concatenate([y_a, y_b], axis=1), big["ab_w_out"][0], wts["ab_w_out"][0], "ab_out")
    return out, gathered, token


def _t5_bucket_np(dist):
    exact = N_BUCKETS // 2
    logd = np.log(np.maximum(dist, 1).astype(np.float32) / exact) / math.log(MAX_DISTANCE / exact)
    large = np.minimum(exact + (logd * (N_BUCKETS - exact)).astype(np.int32), N_BUCKETS - 1)
    return np.where(dist < exact, dist, large)


def _merge_groups(os_, lses, name):
    t = os_[0].shape[0]
    tile = ROW_TILE

    def fn(o0, o1, o2, l0, l1, l2):
        lane = lax.broadcasted_iota(jnp.int32, (1, LANES), 1)
        lo = lane < HEAD_DIM
        ls = [jnp.where(lo, l[0], l[1]) for l in (l0, l1, l2)]
        m = jnp.maximum(jnp.maximum(ls[0], ls[1]), ls[2])
        es = [jnp.exp(l - m) for l in ls]
        den = es[0] + es[1] + es[2]
        return ((es[0] * o0 + es[1] * o1 + es[2] * o2) / den,)

    ins = [(o, (tile, LANES), lambda i, hp: (i, hp), "t") for o in os_]
    ins += [(l, (2, tile, LANES), lambda i, hp: (hp, i, 0), "t") for l in lses]
    outs = [((t, DIL_WIDTH), (tile, LANES), lambda i, hp: (i, hp), "t")]
    return _fused(fn, ins, outs, (t // tile, N_PAIRS), name)[0]


def _dilated_mixer(u, big, wts):
    p = _linear(u, big["dil_w_qkv"][0], wts["dil_w_qkv"][0], "dil_qkv")
    qi = np.arange(BLOCK)[:, None]
    ki = np.arange(2 * BLOCK)[None, :]
    rel = BLOCK + qi - ki
    biases = []
    for g, (window, dil) in enumerate(DIL_PATTERNS):
        span = window // dil
        bucket = _t5_bucket_np(np.clip(rel, 0, span) * dil).reshape(-1)
        onehot = jnp.asarray(np.eye(N_BUCKETS, dtype=np.float32)[bucket])
        table = wts["rel_bias"][:, g * N_HEADS:(g + 1) * N_HEADS]
        bias = jnp.dot(onehot, table, precision=lax.Precision.HIGHEST)
        biases.append(jnp.transpose(bias.reshape(BLOCK, 2 * BLOCK, N_HEADS), (2, 0, 1)))
    os_, lses = _dilated_attention(p, tuple(biases))
    o = _merge_groups(os_, lses, "dil_merge")
    return _linear(o, big["dil_w_out"][0], wts["dil_w_out"][0], "dil_out")


def _forward_local(x, mods, big, wts, wsrc, gslot, late_shapes, target):
    u = _modulate(x, mods[0, 1], mods[0, 0], "mod_in")
    for i in range(DEPTH):
        sh2, sc2, g1, g2 = mods[i, 3], mods[i, 4], mods[i, 2], mods[i, 5]
        if i == 0:
            y, gathered, token = _rwkv_shortconv(u, big, wts, wsrc, gslot)
            shard_shapes = [s[:SHARDED[n]] + (s[SHARDED[n]] // N_CHIPS,) + s[SHARDED[n] + 1:]
                            for n, s in late_shapes.items()]
            parts = _unpack_chips(gathered, shard_shapes)
            big = {**big, **{n: _join_chips(p, SHARDED[n]) for n, p in zip(late_shapes, parts)}}
            wts = {**wts, **_gradient_slots(token, late_shapes, tuple(late_shapes))}
        else:
            y = _dilated_mixer(u, big, wts)
        x, u = _resid_ln_mod(x, y, g1, wts["ln_g"][i, 0:1], wts["ln_b"][i, 0:1], sc2, sh2, f"ln_mix{i}")
        y = _mlp(u, big["mlp_w1"][i], wts["mlp_w1"][i], big["mlp_w2"][i], wts["mlp_w2"][i], f"mlp{i}")
        if i + 1 < DEPTH:
            x, u = _resid_ln_mod(x, y, g2, wts["ln_g"][i, 1:2], wts["ln_b"][i, 1:2],
                                 mods[i + 1, 1], mods[i + 1, 0], f"ln_mlp{i}")
        else:
            return _resid_ln_loss(x, y, g2, wts["ln_g"][i, 1:2], wts["ln_b"][i, 1:2], target, "ln_loss")


SHARDED = {"ab_w_in": 2, "ab_w_out": 1, "dil_w_qkv": 2, "dil_w_out": 2, "mlp_w1": 2, "mlp_w2": 1,
           "ln_g": 2, "ln_b": 2, "rw_w_up": 2, "rw_a_up": 2, "rw_g_up": 2, "sc_conv_w": 2}
FIRST_MIXER = ("ab_w_in", "ab_w_out")
LATER_LAYERS = ("dil_w_qkv", "dil_w_out", "mlp_w1", "mlp_w2")
SMALL_SHARDED = ("ln_g", "ln_b", "rw_w_up", "rw_a_up", "rw_g_up", "sc_conv_w")
REPLICATED = ("ada_b", "rw_mu", "rw_w0", "rw_a0", "rw_k_k", "rw_k_a", "rw_r_k", "rw_lnx_g", "rw_lnx_b", "rel_bias")
WEIGHT_ORDER = ("ada_w", "ada_b", "ln_g", "ln_b", "ab_w_in", "rw_mu", "rw_w0", "rw_w_up", "rw_a0", "rw_a_up",
                "rw_g_up", "rw_k_k", "rw_k_a", "rw_r_k", "rw_lnx_g", "rw_lnx_b", "sc_conv_w", "ab_w_out",
                "dil_w_qkv", "dil_w_out", "rel_bias", "mlp_w1", "mlp_w2")


PACK_ROWS = 16


def _rows_of(n_elems):
    return -(-n_elems // (ROW_W * PACK_ROWS)) * PACK_ROWS


def _to_rows(a):
    flat = a.reshape(-1)
    rows = _rows_of(flat.shape[0])
    if rows * ROW_W != flat.shape[0]:
        flat = jnp.pad(flat, (0, rows * ROW_W - flat.shape[0]))
    return flat.reshape(rows, ROW_W)


def _from_rows(rows, shape):
    n = int(np.prod(shape))
    return rows.reshape(-1)[:n].reshape(shape)


def _split_chips(full, axis):
    shp = full.shape
    parts = full.reshape(shp[:axis] + (N_CHIPS, shp[axis] // N_CHIPS) + shp[axis + 1:])
    return jnp.moveaxis(parts, axis, 0)


def _join_chips(parts, axis):
    moved = jnp.moveaxis(parts, 0, axis)
    shp = moved.shape
    return moved.reshape(shp[:axis] + (shp[axis] * shp[axis + 1],) + shp[axis + 2:])


def _pack_rows(arrays):
    blocks = [_to_rows(a) for a in arrays]
    total = sum(b.shape[0] for b in blocks)
    pad = (-total) % 256
    if pad:
        blocks.append(jnp.zeros((pad, ROW_W), blocks[0].dtype))
    return jnp.concatenate(blocks, axis=0)


def _unpack_rows(buf, shapes):
    out, r0 = [], 0
    for shp in shapes:
        n = _rows_of(int(np.prod(shp)))
        out.append(_from_rows(buf[r0:r0 + n], shp))
        r0 += n
    return out


def _pack_chips(parts):
    blocks = []
    for p in parts:
        flat = p.reshape(N_CHIPS, -1)
        rows = _rows_of(flat.shape[1])
        if rows * ROW_W != flat.shape[1]:
            flat = jnp.pad(flat, ((0, 0), (0, rows * ROW_W - flat.shape[1])))
        blocks.append(flat.reshape(N_CHIPS, rows, ROW_W))
    total = sum(b.shape[1] for b in blocks)
    pad = (-total) % 256
    if pad:
        blocks.append(jnp.zeros((N_CHIPS, pad, ROW_W), blocks[0].dtype))
    return jnp.concatenate(blocks, axis=1)


def _unpack_chips(buf, shapes):
    out, r0 = [], 0
    for shp in shapes:
        size = int(np.prod(shp))
        n = _rows_of(size)
        out.append(buf[:, r0:r0 + n].reshape(N_CHIPS, -1)[:, :size].reshape((N_CHIPS,) + tuple(shp)))
        r0 += n
    return out


def _gradient_slots(token, full_shapes, names):
    @jax.custom_vjp
    def route(token):
        return {n: jnp.zeros(full_shapes[n], F32) for n in names}

    def fwd(token):
        return {n: jnp.zeros(full_shapes[n], F32) for n in names}, None

    def bwd(_, d):
        return (_pack_chips([_split_chips(d[n], SHARDED[n]).astype(BF16) for n in names]),)

    route.defvjp(fwd, bwd)
    return route(token)


def _as2d(a):
    return a.reshape(-1, a.shape[-1])


def kernel(x, c, ada_w, ada_b, ln_g, ln_b, ab_w_in, rw_mu, rw_w0, rw_w_up, rw_a0, rw_a_up, rw_g_up, rw_k_k, rw_k_a, rw_r_k, rw_lnx_g, rw_lnx_b, sc_conv_w, ab_w_out, dil_w_qkv, dil_w_out, rel_bias, mlp_w1, mlp_w2, loss_target, m_ada_w, m_ada_b, m_ln_g, m_ln_b, m_ab_w_in, m_rw_mu, m_rw_w0, m_rw_w_up, m_rw_a0, m_rw_a_up, m_rw_g_up, m_rw_k_k, m_rw_k_a, m_rw_r_k, m_rw_lnx_g, m_rw_lnx_b, m_sc_conv_w, m_ab_w_out, m_dil_w_qkv, m_dil_w_out, m_rel_bias, m_mlp_w1, m_mlp_w2, v_ada_w, v_ada_b, v_ln_g, v_ln_b, v_ab_w_in, v_rw_mu, v_rw_w0, v_rw_w_up, v_rw_a0, v_rw_a_up, v_rw_g_up, v_rw_k_k, v_rw_k_a, v_rw_r_k, v_rw_lnx_g, v_rw_lnx_b, v_sc_conv_w, v_ab_w_out, v_dil_w_qkv, v_dil_w_out, v_rel_bias, v_mlp_w1, v_mlp_w2):
    args = dict(locals())
    w_in = {n: args[n] for n in WEIGHT_ORDER}
    m_in = {n: args["m_" + n] for n in WEIGHT_ORDER}
    v_in = {n: args["v_" + n] for n in WEIGHT_ORDER}
    me = _me()
    chip = _chip_of(me)
    dev = _dev_of(me)

    c_all = _dev_all_gather(c, "gather_c")[:, 0, :]
    n_col = ada_w.shape[2]
    ada_b_cols = lax.dynamic_slice_in_dim(ada_b, chip * n_col, n_col, axis=1)[:, None, :]
    mod_cols = _ada_fwd(c_all, ada_w, ada_b_cols)

    first_buf = _pack_rows([w_in[n].astype(BF16) for n in FIRST_MIXER])
    first_all = _two_level_gather(first_buf, "gather_first")
    late_buf = _pack_rows([w_in[n].astype(BF16) for n in LATER_LAYERS])
    small_buf = _pack_rows([mod_cols] + [w_in[n] for n in SMALL_SHARDED])
    small_all = _chip_all_gather(small_buf, "gather_small")

    def full_shape(n):
        shp = w_in[n].shape
        return shp[:SHARDED[n]] + (shp[SHARDED[n]] * N_CHIPS,) + shp[SHARDED[n] + 1:]

    wts = {n: w_in[n] for n in REPLICATED}
    big = {}
    for n, part in zip(FIRST_MIXER, _unpack_chips(first_all, [w_in[n].shape for n in FIRST_MIXER])):
        big[n] = _join_chips(part, SHARDED[n])
        wts[n] = jnp.zeros(full_shape(n), F32)
    small_parts = _unpack_chips(small_all, [mod_cols.shape] + [w_in[n].shape for n in SMALL_SHARDED])
    for n, part in zip(SMALL_SHARDED, small_parts[1:]):
        wts[n] = _join_chips(part, SHARDED[n])
    mod_all = _join_chips(small_parts[0], 2)
    mods = lax.dynamic_slice_in_dim(mod_all, dev, 1, axis=1).reshape(DEPTH, 6, 1, D_MODEL)
    late_shapes = {n: full_shape(n) for n in LATER_LAYERS}
    late_slot = jnp.zeros((N_CHIPS,) + late_buf.shape, BF16)

    def local_loss(xv, modv, wv, slot):
        return _forward_local(xv, modv, big, wv, late_buf, slot, late_shapes, loss_target[0])[0, 0]

    loss_local, (grad_x, dmods, dw, late_recv) = jax.value_and_grad(local_loss, argnums=(0, 1, 2, 3))(
        x[0], mods, wts, late_slot)
    loss = lax.psum(loss_local, ("x", "y", "c"))

    small_row = jnp.concatenate([dmods.reshape(-1)] + [dw[n].reshape(-1) for n in REPLICATED[1:]])
    n_small = small_row.shape[0]
    n_small_pad = -(-n_small // LANES) * LANES
    small_row = jnp.pad(small_row, (0, n_small_pad - n_small))[None, :]
    rows_all = _dev_all_gather(small_row, "gather_small_grads")
    small_sum = _sum_slots(rows_all, "sum_small_grads")

    dmod_all = rows_all[:, 0, :DEPTH * 6 * D_MODEL].reshape(N_DEV, DEPTH, 6 * D_MODEL)
    dmod_cols = lax.dynamic_slice_in_dim(dmod_all, chip * n_col, n_col, axis=2)
    dmod_cols = jnp.pad(jnp.moveaxis(dmod_cols, 0, 1), ((0, 0), (0, LANES - N_DEV), (0, 0)))
    c_all_t = jnp.pad(c_all.T, ((0, 0), (0, LANES - N_DEV)))
    grads = {"ada_w": _ada_grad(c_all_t, dmod_cols)}
    grads["ada_b"] = small_sum[0, :DEPTH * 6 * D_MODEL].reshape(ada_b.shape)
    r0 = DEPTH * 6 * D_MODEL
    for n in REPLICATED[1:]:
        size = int(np.prod(w_in[n].shape))
        grads[n] = small_sum[0, r0:r0 + size].reshape(w_in[n].shape)
        r0 += size

    sharded_names = FIRST_MIXER + SMALL_SHARDED
    send = _pack_chips([_split_chips(dw[n], SHARDED[n]).astype(BF16) for n in sharded_names])
    n_rows = send.shape[1]
    send = send.reshape(N_CHIPS, 2, n_rows // 2, ROW_W)
    theirs = _core_halves(send, "swap_halves")
    mine = lax.dynamic_index_in_dim(send, me[2], 1, keepdims=False)
    chip_part = _add_pairs(mine, theirs, "sum_cores")
    recv = _chip_scatter(chip_part, "scatter_grads")
    half_sum = _sum_slots(recv, "sum_chips")
    g_rows = _core_all_gather(half_sum, "gather_halves").reshape(n_rows, ROW_W)
    for n, g in zip(sharded_names, _unpack_rows(g_rows, [w_in[n].shape for n in sharded_names])):
        grads[n] = g

    late_part = _sum_slots(late_recv, "sum_chips_late")
    late_rows = _sum_slots(_core_all_gather(late_part, "swap_late"), "sum_cores_late")
    for n, g in zip(LATER_LAYERS, _unpack_rows(late_rows, [w_in[n].shape for n in LATER_LAYERS])):
        grads[n] = g

    deltas, new_m, new_v = {}, {}, {}
    for n in WEIGHT_ORDER:
        shp = w_in[n].shape
        d, nm, nv = _adamw(_as2d(w_in[n]), _as2d(grads[n]), _as2d(m_in[n]), _as2d(v_in[n]), "adamw_" + n)
        deltas[n], new_m[n], new_v[n] = d.reshape(shp), nm.reshape(shp), nv.reshape(shp)

    return (loss, grad_x[None], *[grads[n] for n in WEIGHT_ORDER], *[deltas[n] for n in WEIGHT_ORDER],
            *[new_m[n] for n in WEIGHT_ORDER], *[new_v[n] for n in WEIGHT_ORDER])
```

```python
import functools
import math

import numpy as np
import jax
import jax.numpy as jnp
from jax import lax
from jax.experimental import pallas as pl
from jax.experimental.pallas import tpu as pltpu

F32 = jnp.float32
BF16 = jnp.bfloat16
MESH = pl.DeviceIdType.MESH

D_MODEL = 1024
DEPTH = 2
RW_WIDTH = 512
HEAD_DIM = 64
N_HEADS = 8
RW_DECAY_RANK = 64
RW_ICLR_RANK = 64
RW_GATE_RANK = 160
RW_GN_EPS = 64e-5
RW_PROJ = 3 * RW_WIDTH + RW_DECAY_RANK + RW_ICLR_RANK + RW_GATE_RANK
SC_WIDTH = 512
AB_PROJ = RW_PROJ + 3 * SC_WIDTH
DIL_PATTERNS = ((128, 1), (512, 4), (2048, 16))
N_GROUPS = 3
DIL_WIDTH = 512
DIL_PROJ = N_GROUPS * 3 * DIL_WIDTH
BLOCK = 128
N_BUCKETS = 32
MAX_DISTANCE = 2048
D_FF = 4 * D_MODEL
ALPHA = (2 * DEPTH) ** 0.25
LN_EPS = 1e-5
ADAM_LR = 0.001
ADAM_B1 = 0.9
ADAM_B2 = 0.999
ADAM_EPS = 1e-08
ADAM_WD = 0.01
ADAM_STEP = 10

N_CHIPS = 4
N_DEV = 8
LANES = 128
SUBLANES = 8
ROW_W = 1024
SCAN_CHUNK = 16
VMEM_LIMIT = 48 * 1024 * 1024
NEG_BIG = -1e30


def _pick(n, cands):
    for c in cands:
        if n % c == 0:
            return c
    return n


def _cparams(sem=None, vmem=None):
    return pltpu.CompilerParams(dimension_semantics=sem, vmem_limit_bytes=vmem)


_DOT_DIMS = {
    "nn": (((1,), (0,)), ((), ())),
    "nt": (((1,), (1,)), ((), ())),
    "tn": (((0,), (0,)), ((), ())),
}


def _mm(a, b, mode, name, square_relu=False, pre_act=None):
    if mode == "nn":
        (m, k), (_, n) = a.shape, b.shape
    elif mode == "nt":
        (m, k), (n, _) = a.shape, b.shape
    else:
        (k, m), (_, n) = a.shape, b.shape
    tm = _pick(m, (1024, 512, 256, 128))
    tn = _pick(n, (1024, 768, 512, 384, 256, 128))
    wide_k = mode != "tn" and pre_act is None
    tk = _pick(k, (2048, 1024, 512, 256, 128) if wide_k else (1024, 512, 256, 128))
    nk = k // tk
    if mode == "tn":
        a_spec = pl.BlockSpec((tk, tm), lambda i, j, kk: (kk, i))
    else:
        a_spec = pl.BlockSpec((tm, tk), lambda i, j, kk: (i, kk))
    if mode == "nt":
        b_spec = pl.BlockSpec((tn, tk), lambda i, j, kk: (j, kk))
    else:
        b_spec = pl.BlockSpec((tk, tn), lambda i, j, kk: (kk, j))
    dims = _DOT_DIMS[mode]

    out_spec = pl.BlockSpec((tm, tn), lambda i, j, kk: (i, j))

    def body(*refs):
        a_ref, b_ref = refs[:2]
        h_ref = refs[2] if pre_act is not None else None
        o_ref = refs[3] if pre_act is not None else refs[2]
        act_ref = refs[-1] if square_relu else None
        part = lax.dot_general(a_ref[...].astype(BF16), b_ref[...].astype(BF16), dims, preferred_element_type=F32)

        def finish(total):
            if h_ref is not None:
                total = total * (2.0 * jnp.maximum(h_ref[...], 0.0))
            o_ref[...] = total
            if act_ref is not None:
                act_ref[...] = jnp.square(jnp.maximum(total, 0.0)).astype(BF16)

        if nk == 1:
            finish(part)
        else:
            kk = pl.program_id(2)

            @pl.when(kk == 0)
            def _():
                o_ref[...] = part

            @pl.when(jnp.logical_and(kk > 0, kk < nk - 1))
            def _():
                o_ref[...] += part

            @pl.when(kk == nk - 1)
            def _():
                finish(o_ref[...] + part)

    operands = [a, b] + ([pre_act] if pre_act is not None else [])
    in_specs = [a_spec, b_spec] + ([out_spec] if pre_act is not None else [])
    out_shape = [jax.ShapeDtypeStruct((m, n), F32)] + ([jax.ShapeDtypeStruct((m, n), BF16)] if square_relu else [])
    outs = pl.pallas_call(
        body, name=name, grid=(m // tm, n // tn, nk),
        in_specs=in_specs, out_specs=[out_spec] * len(out_shape), out_shape=out_shape,
        compiler_params=_cparams(("parallel", "parallel", "arbitrary"), VMEM_LIMIT),
    )(*operands)
    return tuple(outs) if square_relu else outs[0]


def _linear(x, w, slot, name):
    @jax.custom_vjp
    def op(x, w, slot):
        return _mm(x, w, "nn", name + "_fwd")

    def fwd(x, w, slot):
        return _mm(x, w, "nn", name + "_fwd"), (x, w)

    def bwd(res, dy):
        x, w = res
        return _mm(dy, w, "nt", name + "_dx"), jnp.zeros_like(w), _mm(x, dy, "tn", name + "_dw")

    op.defvjp(fwd, bwd)
    return op(x, w, slot)


def _const_map(ndim):
    return lambda *g: (0,) * ndim


def _first_step(n_grid):
    return functools.reduce(jnp.logical_and, [pl.program_id(d) == 0 for d in range(n_grid)])


def _fused(fn, ins, outs, grid, name):
    arrays = [i[0] for i in ins]
    n_in, n_out, n_grid = len(ins), len(outs), len(grid)
    in_specs = [pl.BlockSpec(bs, im) for (_, bs, im, _) in ins]
    out_specs = [pl.BlockSpec(bs, im) for (_, bs, im, _) in outs]
    out_shapes = [jax.ShapeDtypeStruct(s, F32) for (s, _, _, _) in outs]
    sem = ("arbitrary",) * n_grid

    def fwd_call(*xs):
        def body(*refs):
            vals = [r[...] for r in refs[:n_in]]
            ys = fn(*vals)
            first = _first_step(n_grid)
            for o_ref, y, (_, _, _, kind) in zip(refs[n_in:], ys, outs):
                if kind == "t":
                    o_ref[...] = y
                else:
                    @pl.when(first)
                    def _(o_ref=o_ref):
                        o_ref[...] = jnp.zeros_like(o_ref)

                    o_ref[...] += y

        return pl.pallas_call(
            body, name=name + "_fwd", grid=grid, in_specs=in_specs, out_specs=out_specs,
            out_shape=out_shapes, compiler_params=_cparams(sem, VMEM_LIMIT))(*xs)

    def bwd_call(xs, dys):
        d_specs = [pl.BlockSpec(bs, im) for (_, bs, im, _) in outs]
        g_specs = [pl.BlockSpec(bs, im) for (_, bs, im, _) in ins]
        g_shapes = [jax.ShapeDtypeStruct(a.shape, F32) for a in arrays]

        def body(*refs):
            vals = [r[...] for r in refs[:n_in]]
            dvals = tuple(r[...] for r in refs[n_in:n_in + n_out])
            _, vjp = jax.vjp(lambda *v: tuple(fn(*v)), *vals)
            gs = vjp(dvals)
            first = _first_step(n_grid)
            for g_ref, g, (_, _, _, kind) in zip(refs[n_in + n_out:], gs, ins):
                if kind == "t":
                    g_ref[...] = g
                else:
                    @pl.when(first)
                    def _(g_ref=g_ref):
                        g_ref[...] = jnp.zeros_like(g_ref)

                    g_ref[...] += g

        return pl.pallas_call(
            body, name=name + "_bwd", grid=grid, in_specs=in_specs + d_specs, out_specs=g_specs,
            out_shape=g_shapes, compiler_params=_cparams(sem, VMEM_LIMIT))(*xs, *dys)

    @jax.custom_vjp
    def op(*xs):
        return tuple(fwd_call(*xs))

    def op_fwd(*xs):
        return tuple(fwd_call(*xs)), xs

    def op_bwd(xs, dys):
        return tuple(bwd_call(xs, dys))

    op.defvjp(op_fwd, op_bwd)
    return op(*arrays)


def _tiled(a, tile, cols=None, col_block=0):
    cols = a.shape[1] if cols is None else cols
    return (a, (tile, cols), lambda i, cb=col_block: (i, cb), "t")


def _shared(a):
    return (a, a.shape, _const_map(a.ndim), "b")


def _tiled_out(rows, cols, tile):
    return ((rows, cols), (tile, cols), lambda i: (i, 0), "t")


@jax.custom_vjp
def _bdot(x, w):
    return jnp.dot(x.astype(BF16), w.astype(BF16), preferred_element_type=F32)


def _bdot_fwd(x, w):
    return _bdot(x, w), (x, w)


def _bdot_bwd(res, dy):
    x, w = res
    dyb = dy.astype(BF16)
    dx = lax.dot_general(dyb, w.astype(BF16), _DOT_DIMS["nt"], preferred_element_type=F32)
    dw = lax.dot_general(x.astype(BF16), dyb, _DOT_DIMS["tn"], preferred_element_type=F32)
    return dx, dw


_bdot.defvjp(_bdot_fwd, _bdot_bwd)


def _head_sum(x):
    n = x.shape[-1]
    hi = lax.broadcasted_iota(jnp.int32, (n, n), 0) // HEAD_DIM
    hj = lax.broadcasted_iota(jnp.int32, (n, n), 1) // HEAD_DIM
    e = (hi == hj).astype(F32)
    return jnp.dot(x, e, precision=lax.Precision.HIGHEST, preferred_element_type=F32)


def _softplus(x):
    return jnp.maximum(x, 0.0) + jnp.log1p(jnp.exp(-jnp.abs(x)))


def _layer_norm_rows(z, g, b):
    mu = jnp.mean(z, axis=-1, keepdims=True)
    zc = z - mu
    var = jnp.mean(zc * zc, axis=-1, keepdims=True)
    return zc * lax.rsqrt(var + LN_EPS) * g + b


N_PAIRS = N_HEADS // 2


def _scan_consts():
    k = lax.broadcasted_iota(jnp.int32, (HEAD_DIM, LANES), 0)
    j = lax.broadcasted_iota(jnp.int32, (HEAD_DIM, LANES), 1)
    diag = (j % HEAD_DIM) == k
    jj = lax.broadcasted_iota(jnp.int32, (LANES, LANES), 0) // HEAD_DIM
    ll = lax.broadcasted_iota(jnp.int32, (LANES, LANES), 1) // HEAD_DIM
    same_head = (jj == ll).astype(BF16)
    lane_lo = j < HEAD_DIM
    return diag, same_head, lane_lo


def _unrolled(n, body, carry):
    for i in range(n):
        carry = body(i, carry)
    return carry


def _fill_cols(srcs, col_ref, diag_bf, same_head, n_steps):
    del diag_bf, same_head
    assert n_steps == 16
    blocks = []
    for src in srcs:
        x = src[...]
        hi = x.astype(BF16).astype(F32)
        r1 = x - hi
        mid = r1.astype(BF16).astype(F32)
        x48 = jnp.concatenate([hi, mid, r1 - mid], axis=0)
        for hp in range(N_PAIRS):
            xp = x48[:, hp * LANES:(hp + 1) * LANES]
            y = jnp.concatenate([xp, pltpu.roll(xp, HEAD_DIM, 1), jnp.zeros((32, LANES), F32)], axis=0)
            blocks.append(y.T[:HEAD_DIM].astype(BF16))
    lhs = jnp.concatenate(blocks, axis=0)
    j = lax.broadcasted_iota(jnp.int32, (LANES, LANES), 0)
    lane_head = lax.broadcasted_iota(jnp.int32, (LANES, LANES), 1) // HEAD_DIM
    for t in range(n_steps):
        pick = jnp.logical_and(j < 96, jnp.logical_and(j % 16 == t, j // 48 == lane_head))
        out = jnp.dot(lhs, pick.astype(BF16), preferred_element_type=F32)
        for vi in range(len(srcs)):
            for hp in range(N_PAIRS):
                r0 = (vi * N_PAIRS + hp) * HEAD_DIM
                col_ref[vi, hp, t] = out[r0:r0 + HEAD_DIM]


def _scan_fwd_call(r, w, k, v, a, b, wsrc):
    t_len = r.shape[0]
    ch = SCAN_CHUNK
    n_ch = t_len // ch
    w_rows, w_cols = wsrc.shape
    half = w_rows // 2
    chunks = _n_chunks(half)
    rc = half // chunks
    transfers = [(q, kk) for q in range(chunks) for kk in range(len(CHIP_FLIPS))]
    hand_on_step = [max(1, min(n_ch - 1, (i + 1) * (n_ch - 8) // len(transfers) + 3)) for i in range(len(transfers))]

    def body(r_ref, w_ref, k_ref, v_ref, a_ref, b_ref, src_ref, y_ref, ck_ref, g_ref,
             st_ref, col_ref, send1, recv1, send2, recv2):
        c = pl.program_id(0)
        me = _me()
        core = me[2]
        sibling = _flip(me, CORE_FLIP)

        def over_ici(q, kk):
            return pltpu.make_async_remote_copy(
                src_ref=src_ref.at[core, pl.ds(q * rc, rc)], dst_ref=g_ref.at[_chip_of(me), core, pl.ds(q * rc, rc)],
                send_sem=send1.at[kk * chunks + q], recv_sem=recv1.at[kk * chunks + q],
                device_id=_flip(me, CHIP_FLIPS[kk]), device_id_type=MESH)

        def hand_on(q, kk):
            landed = g_ref.at[_chip_of(_flip(me, CHIP_FLIPS[kk])), core, pl.ds(q * rc, rc)]
            return pltpu.make_async_remote_copy(
                src_ref=landed, dst_ref=landed, send_sem=send2.at[kk * chunks + q], recv_sem=recv2.at[kk * chunks + q],
                device_id=sibling, device_id_type=MESH)

        @pl.when(c == 0)
        def _():
            st_ref[...] = jnp.zeros_like(st_ref)
            for q, kk in transfers:
                over_ici(q, kk).start()

        for step in sorted(set(hand_on_step)):
            @pl.when(c == step)
            def _(step=step):
                for (q, kk), s in zip(transfers, hand_on_step):
                    if s == step:
                        over_ici(q, kk).wait_recv()
                        hand_on(q, kk).start()

        ck_ref[0] = st_ref[...]
        diag, same_head, _ = _scan_consts()
        diag_bf = diag.astype(BF16)
        _fill_cols((w_ref, a_ref, b_ref, k_ref, r_ref), col_ref, diag_bf, same_head, ch)

        def step(t, states):
            new = []
            for hp in range(N_PAIRS):
                lanes = slice(hp * LANES, (hp + 1) * LANES)
                s = states[hp]
                sa = jnp.sum(s * col_ref[1, hp, t], axis=0, keepdims=True)
                s = s * col_ref[0, hp, t] + col_ref[2, hp, t] * sa + col_ref[3, hp, t] * v_ref[pl.ds(t, 1), lanes]
                y_ref[pl.ds(t, 1), lanes] = jnp.sum(s * col_ref[4, hp, t], axis=0, keepdims=True)
                new.append(s)
            return tuple(new)

        states = _unrolled(ch, step, tuple(st_ref[hp] for hp in range(N_PAIRS)))
        for hp in range(N_PAIRS):
            st_ref[hp] = states[hp]

        @pl.when(c == n_ch - 1)
        def _():
            for q, kk in transfers:
                hand_on(q, kk).wait_recv()
            for q, kk in transfers:
                over_ici(q, kk).wait_send()
                hand_on(q, kk).wait_send()

    row_spec = pl.BlockSpec((ch, RW_WIDTH), lambda c: (c, 0))
    any_spec = pl.BlockSpec(memory_space=pl.ANY)
    n_sem = len(transfers)
    y, ck, got = pl.pallas_call(
        body, name="rwkv_scan_fwd", grid=(n_ch,),
        in_specs=[row_spec] * 6 + [any_spec],
        out_specs=[row_spec, pl.BlockSpec((1, N_PAIRS, HEAD_DIM, LANES), lambda c: (c, 0, 0, 0)), any_spec],
        out_shape=[jax.ShapeDtypeStruct((t_len, RW_WIDTH), F32),
                   jax.ShapeDtypeStruct((n_ch, N_PAIRS, HEAD_DIM, LANES), F32),
                   jax.ShapeDtypeStruct((N_CHIPS, 2, half, w_cols), wsrc.dtype)],
        scratch_shapes=[pltpu.VMEM((N_PAIRS, HEAD_DIM, LANES), F32),
                        pltpu.VMEM((5, N_PAIRS, ch, HEAD_DIM, LANES), F32)]
        + [pltpu.SemaphoreType.DMA((n_sem,)) for _ in range(4)],
        compiler_params=_cparams(("arbitrary",), VMEM_LIMIT),
    )(r, w, k, v, a, b, wsrc.reshape(2, half, w_cols))
    return y, ck, got.reshape(N_CHIPS, w_rows, w_cols)


def _scan_bwd_call(r, w, k, v, a, b, ck, dy, gsend):
    t_len = r.shape[0]
    ch = SCAN_CHUNK
    n_ch = t_len // ch
    _, g_rows, g_cols = gsend.shape
    chunks = _n_chunks(g_rows)
    rc = g_rows // chunks
    transfers = [(q, kk) for q in range(chunks) for kk in range(len(CHIP_FLIPS))]

    def body(r_ref, w_ref, k_ref, v_ref, a_ref, b_ref, ck_ref, dy_ref, gs_ref,
             dr_ref, dw_ref, dk_ref, dv_ref, da_ref, db_ref, gr_ref,
             ds_ref, col_ref, sp_ref, sa_ref, send_sems, recv_sems):
        c = pl.program_id(0)
        me = _me()

        def to_chip(q, kk):
            peer = _flip(me, CHIP_FLIPS[kk])
            return pltpu.make_async_remote_copy(
                src_ref=gs_ref.at[_chip_of(peer), pl.ds(q * rc, rc)], dst_ref=gr_ref.at[_chip_of(me), pl.ds(q * rc, rc)],
                send_sem=send_sems.at[kk * chunks + q], recv_sem=recv_sems.at[kk * chunks + q],
                device_id=peer, device_id_type=MESH)

        @pl.when(c == 0)
        def _():
            ds_ref[...] = jnp.zeros_like(ds_ref)
            for q, kk in transfers:
                to_chip(q, kk).start()

        diag, same_head, lane_lo = _scan_consts()
        diag_bf = diag.astype(BF16)
        diag_f = diag.astype(F32)
        _fill_cols((w_ref, a_ref, b_ref, k_ref, r_ref), col_ref, diag_bf, same_head, ch)

        def replay(t, states):
            new = []
            for hp in range(N_PAIRS):
                lanes = slice(hp * LANES, (hp + 1) * LANES)
                s = states[hp]
                sp_ref[t, hp] = s
                sa = jnp.sum(s * col_ref[1, hp, t], axis=0, keepdims=True)
                sa_ref[pl.ds(t, 1), lanes] = sa
                new.append(s * col_ref[0, hp, t] + col_ref[2, hp, t] * sa
                           + col_ref[3, hp, t] * v_ref[pl.ds(t, 1), lanes])
            return tuple(new)

        _unrolled(ch, replay, tuple(ck_ref[0, hp] for hp in range(N_PAIRS)))

        def key_rows(ps):
            stacked = jnp.concatenate([p.astype(BF16) for p in ps], axis=0)
            q = jnp.dot(stacked, same_head, preferred_element_type=F32)
            return [jnp.sum(q[i * HEAD_DIM:(i + 1) * HEAD_DIM] * diag_f, axis=0, keepdims=True)
                    for i in range(len(ps))]

        def back(i, grads):
            t = ch - 1 - i
            new = []
            for hp in range(N_PAIRS):
                lanes = slice(hp * LANES, (hp + 1) * LANES)
                wc, ac, bc, kc, rc = (col_ref[vi, hp, t] for vi in range(5))
                sp = sp_ref[t, hp]
                sa = sa_ref[pl.ds(t, 1), lanes]
                vrow = v_ref[pl.ds(t, 1), lanes]
                dyrow = dy_ref[pl.ds(t, 1), lanes]
                st = sp * wc + bc * sa + kc * vrow
                g = grads[hp] + rc * dyrow
                dsa = jnp.sum(g * bc, axis=0, keepdims=True)
                dv_ref[pl.ds(t, 1), lanes] = jnp.sum(g * kc, axis=0, keepdims=True)
                rows = key_rows([st * dyrow, g * vrow, g * sa, g * sp, sp * dsa])
                for out_ref, row in zip((dr_ref, dk_ref, db_ref, dw_ref, da_ref), rows):
                    out_ref[pl.ds(t, 1), lanes] = row
                new.append(g * wc + ac * dsa)
            return tuple(new)

        grads = _unrolled(ch, back, tuple(ds_ref[hp] for hp in range(N_PAIRS)))
        for hp in range(N_PAIRS):
            ds_ref[hp] = grads[hp]

        @pl.when(c == n_ch - 1)
        def _():
            for q, kk in transfers:
                to_chip(q, kk).wait()

    row_spec = pl.BlockSpec((ch, RW_WIDTH), lambda c: (n_ch - 1 - c, 0))
    any_spec = pl.BlockSpec(memory_space=pl.ANY)
    out_sds = jax.ShapeDtypeStruct((t_len, RW_WIDTH), F32)
    n_sem = len(transfers)
    return pl.pallas_call(
        body, name="rwkv_scan_bwd", grid=(n_ch,),
        in_specs=[row_spec] * 6 + [pl.BlockSpec((1, N_PAIRS, HEAD_DIM, LANES), lambda c: (n_ch - 1 - c, 0, 0, 0)),
                                   row_spec, any_spec],
        out_specs=[row_spec] * 6 + [any_spec],
        out_shape=[out_sds] * 6 + [jax.ShapeDtypeStruct(gsend.shape, gsend.dtype)],
        scratch_shapes=[pltpu.VMEM((N_PAIRS, HEAD_DIM, LANES), F32),
                        pltpu.VMEM((5, N_PAIRS, ch, HEAD_DIM, LANES), F32),
                        pltpu.VMEM((ch, N_PAIRS, HEAD_DIM, LANES), F32),
                        pltpu.VMEM((ch, RW_WIDTH), F32),
                        pltpu.SemaphoreType.DMA((n_sem,)), pltpu.SemaphoreType.DMA((n_sem,))],
        compiler_params=_cparams(("arbitrary",), VMEM_LIMIT),
    )(r, w, k, v, a, b, ck, dy, gsend)


def _scan_token(wsrc):
    return jnp.zeros((N_CHIPS,) + wsrc.shape, BF16)


@jax.custom_vjp
def _rwkv_scan(r, w, k, v, a, b, wsrc, gslot):
    y, _, got = _scan_fwd_call(r, w, k, v, a, b, wsrc)
    return y, got, _scan_token(wsrc)


def _rwkv_scan_fwd(r, w, k, v, a, b, wsrc, gslot):
    y, ck, got = _scan_fwd_call(r, w, k, v, a, b, wsrc)
    return (y, got, _scan_token(wsrc)), (r, w, k, v, a, b, ck, wsrc)


def _rwkv_scan_bwd(res, cts):
    *saved, wsrc = res
    dy, _, gsend = cts
    *dins, got = _scan_bwd_call(*saved, dy, gsend)
    summed = _sum_slots_own(got, gsend, _chip_of(_me()), "sum_chips_late")
    return (*dins, jnp.zeros_like(wsrc), summed)


_rwkv_scan.defvjp(_rwkv_scan_fwd, _rwkv_scan_bwd)


ATT_SCALE = HEAD_DIM ** -0.5
COLS_PER_POS = DIL_PROJ // DIL_WIDTH


def _att_masks():
    qi = lax.broadcasted_iota(jnp.int32, (BLOCK, BLOCK), 0)
    ki = lax.broadcasted_iota(jnp.int32, (BLOCK, BLOCK), 1)
    lane = lax.broadcasted_iota(jnp.int32, (1, LANES), 1)
    return ki <= qi, ki >= qi, lane


def _att_fwd_call(p, bias, g, dil):
    t_len = p.shape[0]
    l_len = t_len // dil
    nb = l_len // BLOCK
    pv = p.reshape(l_len, dil * DIL_PROJ)
    base = g * 3

    def body(q_ref, kc_ref, kp_ref, vc_ref, vp_ref, bias_ref, o_ref, lse_ref):
        n = pl.program_id(1)
        cur_ok, prev_band, lane = _att_masks()
        prev_ok = jnp.logical_and(prev_band, n > 0)
        for hp in range(N_PAIRS):
            lanes = slice(hp * LANES, (hp + 1) * LANES)
            q2 = q_ref[:, lanes].astype(BF16)
            kc = kc_ref[:, lanes].astype(BF16)
            kp = kp_ref[:, lanes].astype(BF16)
            vc = vc_ref[:, lanes].astype(BF16)
            vp = vp_ref[:, lanes].astype(BF16)
            o2 = jnp.zeros((BLOCK, LANES), F32)
            for hh in range(2):
                h = 2 * hp + hh
                mine = (lane // HEAD_DIM) == hh
                qm = jnp.where(mine, q2, jnp.zeros_like(q2))
                s_c = lax.dot_general(qm, kc, _DOT_DIMS["nt"], preferred_element_type=F32) * ATT_SCALE
                s_p = lax.dot_general(qm, kp, _DOT_DIMS["nt"], preferred_element_type=F32) * ATT_SCALE
                s_c = jnp.where(cur_ok, s_c + bias_ref[h, :, BLOCK:], NEG_BIG)
                s_p = jnp.where(prev_ok, s_p + bias_ref[h, :, :BLOCK], NEG_BIG)
                m = jnp.maximum(jnp.max(s_c, axis=-1, keepdims=True), jnp.max(s_p, axis=-1, keepdims=True))
                e_c = jnp.exp(s_c - m)
                e_p = jnp.exp(s_p - m)
                den = jnp.sum(e_c, axis=-1, keepdims=True) + jnp.sum(e_p, axis=-1, keepdims=True)
                o_h = (jnp.dot((e_c / den).astype(BF16), vc, preferred_element_type=F32)
                       + jnp.dot((e_p / den).astype(BF16), vp, preferred_element_type=F32))
                o2 = o2 + jnp.where(mine, o_h, 0.0)
                lse_ref[h] = jnp.broadcast_to(m + jnp.log(den), (BLOCK, LANES))
            o_ref[:, lanes] = o2

    def col(j):
        return lambda r, n: (n, r * COLS_PER_POS + base + j)

    def col_prev(j):
        return lambda r, n: (jnp.maximum(n - 1, 0), r * COLS_PER_POS + base + j)

    blk = (BLOCK, DIL_WIDTH)
    o, lse = pl.pallas_call(
        body, name=f"dil_att_fwd_g{g}", grid=(dil, nb),
        in_specs=[pl.BlockSpec(blk, col(0)), pl.BlockSpec(blk, col(1)), pl.BlockSpec(blk, col_prev(1)),
                  pl.BlockSpec(blk, col(2)), pl.BlockSpec(blk, col_prev(2)),
                  pl.BlockSpec(bias.shape, _const_map(3))],
        out_specs=[pl.BlockSpec(blk, lambda r, n: (n, r)),
                   pl.BlockSpec((N_HEADS, BLOCK, LANES), lambda r, n: (0, n, r))],
        out_shape=[jax.ShapeDtypeStruct((l_len, dil * DIL_WIDTH), F32),
                   jax.ShapeDtypeStruct((N_HEADS, l_len, dil * LANES), F32)],
        compiler_params=_cparams(("arbitrary", "arbitrary"), VMEM_LIMIT),
    )(pv, pv, pv, pv, pv, bias)
    return o.reshape(t_len, DIL_WIDTH), lse.reshape(N_HEADS, t_len, LANES)


def _att_bwd_call(p, bias, o, lse, do, dlse, g, dil):
    t_len = p.shape[0]
    l_len = t_len // dil
    nb = l_len // BLOCK
    pv = p.reshape(l_len, dil * DIL_PROJ)
    ov = o.reshape(l_len, dil * DIL_WIDTH)
    dov = do.reshape(l_len, dil * DIL_WIDTH)
    lsev = lse.reshape(N_HEADS, l_len, dil * LANES)
    dlsev = dlse.reshape(N_HEADS, l_len, dil * LANES)
    base = g * 3

    def body(q_ref, qn_ref, k_ref, v_ref, do_ref, don_ref, o_ref, on_ref, lse_ref, lsen_ref, dl_ref, dln_ref,
             bias_ref, dq_ref, dk_ref, dv_ref, dbias_ref, carry_ref):
        r = pl.program_id(0)
        n = pl.program_id(1)
        cur_ok, prev_band, lane = _att_masks()
        has_next = n + 1 < nb

        @pl.when(jnp.logical_and(r == 0, n == 0))
        def _():
            dbias_ref[...] = jnp.zeros_like(dbias_ref)

        @pl.when(n == 0)
        def _():
            carry_ref[...] = jnp.zeros_like(carry_ref)

        for hp in range(N_PAIRS):
            lanes = slice(hp * LANES, (hp + 1) * LANES)
            k2 = k_ref[:, lanes].astype(BF16)
            v2 = v_ref[:, lanes].astype(BF16)
            dk2 = jnp.zeros((BLOCK, LANES), F32)
            dv2 = jnp.zeros((BLOCK, LANES), F32)
            dq_cur = carry_ref[:, lanes]
            dq_next = jnp.zeros((BLOCK, LANES), F32)
            for hh in range(2):
                h = 2 * hp + hh
                mine = (lane // HEAD_DIM) == hh
                tiles = (
                    (q_ref, do_ref, o_ref, lse_ref, dl_ref, cur_ok, slice(BLOCK, 2 * BLOCK), None),
                    (qn_ref, don_ref, on_ref, lsen_ref, dln_ref, prev_band, slice(0, BLOCK), has_next),
                )
                for ti, (qr, dor, orf, lr, dlr, ok, bcols, gate) in enumerate(tiles):
                    q2 = qr[:, lanes].astype(BF16)
                    qm = jnp.where(mine, q2, jnp.zeros_like(q2))
                    do_f = jnp.where(mine, dor[:, lanes], 0.0)
                    dom = do_f.astype(BF16)
                    s = lax.dot_general(qm, k2, _DOT_DIMS["nt"], preferred_element_type=F32) * ATT_SCALE
                    s = s + bias_ref[h, :, bcols]
                    if gate is not None:
                        ok = jnp.logical_and(ok, gate)
                    pr = jnp.where(ok, jnp.exp(jnp.minimum(s - lr[h], 0.0)), 0.0)
                    dp = lax.dot_general(dom, v2, _DOT_DIMS["nt"], preferred_element_type=F32)
                    delta = jnp.sum(do_f * orf[:, lanes], axis=-1, keepdims=True)
                    dl = jnp.sum(dlr[h], axis=-1, keepdims=True)
                    ds = pr * (dp - delta + dl)
                    dsb = ds.astype(BF16)
                    dq_h = jnp.where(mine, jnp.dot(dsb, k2, preferred_element_type=F32), 0.0) * ATT_SCALE
                    if ti == 0:
                        dq_cur = dq_cur + dq_h
                    else:
                        dq_next = dq_next + dq_h
                    dk2 = dk2 + lax.dot_general(dsb, qm, _DOT_DIMS["tn"], preferred_element_type=F32) * ATT_SCALE
                    dv2 = dv2 + lax.dot_general(pr.astype(BF16), dom, _DOT_DIMS["tn"], preferred_element_type=F32)
                    dbias_ref[h, :, bcols] += ds
            dq_ref[:, lanes] = dq_cur
            carry_ref[:, lanes] = dq_next
            dk_ref[:, lanes] = dk2
            dv_ref[:, lanes] = dv2

    def nxt(n):
        return jnp.minimum(n + 1, nb - 1)

    blk = (BLOCK, DIL_WIDTH)
    hblk = (N_HEADS, BLOCK, LANES)
    qcol = lambda j: (lambda r, n: (n, r * COLS_PER_POS + base + j))
    q_next = lambda r, n: (nxt(n), r * COLS_PER_POS + base)
    rown = lambda r, n: (n, r)
    rown_next = lambda r, n: (nxt(n), r)
    hrow = lambda r, n: (0, n, r)
    hrow_next = lambda r, n: (0, nxt(n), r)
    sds = jax.ShapeDtypeStruct((l_len, dil * DIL_WIDTH), F32)
    dq, dk, dv, dbias = pl.pallas_call(
        body, name=f"dil_att_bwd_g{g}", grid=(dil, nb),
        in_specs=[pl.BlockSpec(blk, qcol(0)), pl.BlockSpec(blk, q_next),
                  pl.BlockSpec(blk, qcol(1)), pl.BlockSpec(blk, qcol(2)),
                  pl.BlockSpec(blk, rown), pl.BlockSpec(blk, rown_next),
                  pl.BlockSpec(blk, rown), pl.BlockSpec(blk, rown_next),
                  pl.BlockSpec(hblk, hrow), pl.BlockSpec(hblk, hrow_next),
                  pl.BlockSpec(hblk, hrow), pl.BlockSpec(hblk, hrow_next),
                  pl.BlockSpec(bias.shape, _const_map(3))],
        out_specs=[pl.BlockSpec(blk, rown)] * 3 + [pl.BlockSpec(bias.shape, _const_map(3))],
        out_shape=[sds, sds, sds, jax.ShapeDtypeStruct(bias.shape, F32)],
        scratch_shapes=[pltpu.VMEM((BLOCK, DIL_WIDTH), F32)],
        compiler_params=_cparams(("arbitrary", "arbitrary"), VMEM_LIMIT),
    )(pv, pv, pv, pv, dov, dov, ov, ov, lsev, lsev, dlsev, dlsev, bias)
    shp = (t_len, DIL_WIDTH)
    return dq.reshape(shp), dk.reshape(shp), dv.reshape(shp), dbias


def _att_all_groups(p, biases):
    outs = [_att_fwd_call(p, biases[g], g, dil) for g, (_, dil) in enumerate(DIL_PATTERNS)]
    return tuple(o for o, _ in outs), tuple(l for _, l in outs)


@jax.custom_vjp
def _dilated_attention(p, biases):
    return _att_all_groups(p, biases)


def _dilated_attention_fwd(p, biases):
    os_, lses = _att_all_groups(p, biases)
    return (os_, lses), (p, biases, os_, lses)


def _dilated_attention_bwd(res, cts):
    p, biases, os_, lses = res
    dos, dlses = cts
    parts, dbiases = [], []
    for g, (_, dil) in enumerate(DIL_PATTERNS):
        dq, dk, dv, dbias = _att_bwd_call(p, biases[g], os_[g], lses[g], dos[g], dlses[g], g, dil)
        parts += [dq, dk, dv]
        dbiases.append(dbias)
    return jnp.concatenate(parts, axis=1), tuple(dbiases)


_dilated_attention.defvjp(_dilated_attention_fwd, _dilated_attention_bwd)


def _me():
    return lax.axis_index("x"), lax.axis_index("y"), lax.axis_index("c")


def _flip(me, f):
    return tuple((1 - m) if b else m for m, b in zip(me, f))


def _chip_of(d):
    return 2 * d[0] + d[1]


def _dev_of(d):
    return 4 * d[0] + 2 * d[1] + d[2]


EXCHANGE_CHUNKS = 8
CHIP_FLIPS = ((1, 0, 0), (0, 1, 0), (1, 1, 0))
ALL_FLIPS = tuple((a, b, c) for a in (0, 1) for b in (0, 1) for c in (0, 1) if a or b or c)
CORE_FLIP = (0, 0, 1)


def _n_chunks(rows):
    return EXCHANGE_CHUNKS if rows % (EXCHANGE_CHUNKS * PACK_ROWS) == 0 else 1


def _exchange(src, n_slots, transfers, name):
    _, rows, cols = src.shape
    chunks = _n_chunks(rows)
    rc = rows // chunks
    n = len(transfers) * chunks

    def body(src_ref, dst_ref, send_sems, recv_sems):
        me = _me()
        copies = []
        for q in range(chunks):
            for kk, (f, src_slot, dst_slot) in enumerate(transfers):
                peer = _flip(me, f)
                cp = pltpu.make_async_remote_copy(
                    src_ref=src_ref.at[src_slot(me, peer), pl.ds(q * rc, rc)],
                    dst_ref=dst_ref.at[dst_slot(me, peer), pl.ds(q * rc, rc)],
                    send_sem=send_sems.at[kk * chunks + q], recv_sem=recv_sems.at[kk * chunks + q],
                    device_id=peer, device_id_type=MESH)
                cp.start()
                copies.append(cp)
        for cp in copies:
            cp.wait()

    return pl.pallas_call(
        body, name=name,
        out_shape=jax.ShapeDtypeStruct((n_slots, rows, cols), src.dtype),
        in_specs=[pl.BlockSpec(memory_space=pl.ANY)],
        out_specs=pl.BlockSpec(memory_space=pl.ANY),
        scratch_shapes=[pltpu.SemaphoreType.DMA((n,)), pltpu.SemaphoreType.DMA((n,))],
    )(src)


def _set_slot(buf, block, index):
    return lax.dynamic_update_slice_in_dim(buf, block[None].astype(buf.dtype), index, axis=0)


def _chip_all_gather(src, name):
    got = _exchange(src[None], N_CHIPS, [(f, lambda me, peer: 0, lambda me, peer: _chip_of(me)) for f in CHIP_FLIPS], name)
    return _set_slot(got, src, _chip_of(_me()))


def _dev_all_gather(src, name):
    got = _exchange(src[None], N_DEV, [(f, lambda me, peer: 0, lambda me, peer: _dev_of(me)) for f in ALL_FLIPS], name)
    return _set_slot(got, src, _dev_of(_me()))


def _chip_scatter(src, name):
    got = _exchange(src, N_CHIPS, [(f, lambda me, peer: _chip_of(peer), lambda me, peer: _chip_of(me))
                                   for f in CHIP_FLIPS], name)
    chip = _chip_of(_me())
    return _set_slot(got, lax.dynamic_index_in_dim(src, chip, 0, keepdims=False), chip)


def _core_halves(src, name):
    s, _, half, cols = src.shape
    transfers = [(CORE_FLIP, (lambda me, peer, j=j: 2 * j + peer[2]), (lambda me, peer, j=j: j)) for j in range(s)]
    return _exchange(src.reshape(2 * s, half, cols), s, transfers, name)


def _core_all_gather(src, name):
    got = _exchange(src[None], 2, [(CORE_FLIP, lambda me, peer: 0, lambda me, peer: me[2])], name)
    return _set_slot(got, src, _me()[2])


def _two_level_gather(src, name):
    rows, cols = src.shape
    half = rows // 2
    chunks = _n_chunks(half)
    rc = half // chunks
    n = len(CHIP_FLIPS) * chunks

    def body(src_ref, g_ref, send1, recv1, send2, recv2):
        me = _me()
        c = me[2]
        sibling = _flip(me, CORE_FLIP)
        first, second = [], []
        for q in range(chunks):
            for kk, f in enumerate(CHIP_FLIPS):
                peer = _flip(me, f)
                cp = pltpu.make_async_remote_copy(
                    src_ref=src_ref.at[c, pl.ds(q * rc, rc)], dst_ref=g_ref.at[_chip_of(me), c, pl.ds(q * rc, rc)],
                    send_sem=send1.at[kk * chunks + q], recv_sem=recv1.at[kk * chunks + q],
                    device_id=peer, device_id_type=MESH)
                cp.start()
                first.append((cp, _chip_of(peer), kk * chunks + q, q))
        for cp, origin, idx, q in first:
            cp.wait_recv()
            fw = pltpu.make_async_remote_copy(
                src_ref=g_ref.at[origin, c, pl.ds(q * rc, rc)], dst_ref=g_ref.at[origin, c, pl.ds(q * rc, rc)],
                send_sem=send2.at[idx], recv_sem=recv2.at[idx],
                device_id=sibling, device_id_type=MESH)
            fw.start()
            second.append(fw)
        for fw in second:
            fw.wait_recv()
        for cp, _, _, _ in first:
            cp.wait_send()
        for fw in second:
            fw.wait_send()

    got = pl.pallas_call(
        body, name=name,
        out_shape=jax.ShapeDtypeStruct((N_CHIPS, 2, half, cols), src.dtype),
        in_specs=[pl.BlockSpec(memory_space=pl.ANY)],
        out_specs=pl.BlockSpec(memory_space=pl.ANY),
        scratch_shapes=[pltpu.SemaphoreType.DMA((n,)) for _ in range(4)],
    )(src.reshape(2, half, cols))
    return _set_slot(got.reshape(N_CHIPS, rows, cols), src, _chip_of(_me()))


def _sum_slots(x, name):
    s, rows, cols = x.shape
    tile = _pick(rows, (512, 256, 128, 64, 32, 16, 8))

    def body(x_ref, o_ref):
        acc = x_ref[0].astype(F32)
        for i in range(1, s):
            acc = acc + x_ref[i].astype(F32)
        o_ref[...] = acc

    return pl.pallas_call(
        body, name=name, grid=(rows // tile,),
        in_specs=[pl.BlockSpec((s, tile, cols), lambda i: (0, i, 0))],
        out_specs=pl.BlockSpec((tile, cols), lambda i: (i, 0)),
        out_shape=jax.ShapeDtypeStruct((rows, cols), F32),
        compiler_params=_cparams(("parallel",), VMEM_LIMIT),
    )(x)


def _sum_slots_own(recv, send, chip, name):
    s, rows, cols = recv.shape
    tile = _pick(rows, (512, 256, 128, 64, 32, 16, 8))

    def body(chip_ref, recv_ref, own_ref, o_ref):
        acc = None
        for j in range(s):
            term = jnp.where(chip_ref[0] == j, own_ref[0], recv_ref[j]).astype(F32)
            acc = term if acc is None else acc + term
        o_ref[...] = acc

    grid_spec = pltpu.PrefetchScalarGridSpec(
        num_scalar_prefetch=1, grid=(rows // tile,),
        in_specs=[pl.BlockSpec((s, tile, cols), lambda i, c: (0, i, 0)),
                  pl.BlockSpec((1, tile, cols), lambda i, c: (c[0], i, 0))],
        out_specs=pl.BlockSpec((tile, cols), lambda i, c: (i, 0)))
    return pl.pallas_call(
        body, name=name, grid_spec=grid_spec,
        out_shape=jax.ShapeDtypeStruct((rows, cols), F32),
        compiler_params=_cparams(("arbitrary",), VMEM_LIMIT),
    )(jnp.reshape(chip, (1,)).astype(jnp.int32), recv, send)


def _add_pairs(a, b, name):
    s, rows, cols = a.shape
    tile = _pick(rows, (512, 256, 128, 64, 32, 16, 8))

    def body(a_ref, b_ref, o_ref):
        o_ref[...] = (a_ref[...].astype(F32) + b_ref[...].astype(F32)).astype(o_ref.dtype)

    spec = pl.BlockSpec((1, tile, cols), lambda j, i: (j, i, 0))
    return pl.pallas_call(
        body, name=name, grid=(s, rows // tile),
        in_specs=[spec, spec], out_specs=spec,
        out_shape=jax.ShapeDtypeStruct(a.shape, a.dtype),
        compiler_params=_cparams(("parallel", "parallel"), VMEM_LIMIT),
    )(a, b)


def _adamw(w, g, m, v, name):
    rows, cols = w.shape
    tile = rows
    if rows * cols * 4 > 2 * 1024 * 1024:
        tile = _pick(rows, (256, 128, 64, 32, 16, 8))
    c1 = 1.0 / (1.0 - ADAM_B1 ** ADAM_STEP)
    c2 = 1.0 / (1.0 - ADAM_B2 ** ADAM_STEP)

    def body(w_ref, g_ref, m_ref, v_ref, d_ref, nm_ref, nv_ref):
        gv = g_ref[...]
        nm = ADAM_B1 * m_ref[...] + (1.0 - ADAM_B1) * gv
        nv = ADAM_B2 * v_ref[...] + (1.0 - ADAM_B2) * (gv * gv)
        m_hat = nm * c1
        v_hat = nv * c2
        d_ref[...] = -ADAM_LR * (m_hat / (jnp.sqrt(v_hat) + ADAM_EPS) + ADAM_WD * w_ref[...])
        nm_ref[...] = nm
        nv_ref[...] = nv

    spec = pl.BlockSpec((tile, cols), lambda i: (i, 0))
    sds = jax.ShapeDtypeStruct((rows, cols), F32)
    return pl.pallas_call(
        body, name=name, grid=(rows // tile,),
        in_specs=[spec] * 4, out_specs=[spec] * 3, out_shape=[sds] * 3,
        compiler_params=_cparams(("parallel",), VMEM_LIMIT),
    )(w, g, m, v)


def _ada_fwd(c_all, ada_w, ada_b_cols):
    n_col = ada_w.shape[2]

    def body(c_ref, w_ref, b_ref, o_ref):
        cv = c_ref[...]
        cond = (cv * jax.nn.sigmoid(cv)).astype(BF16)
        o_ref[0] = jnp.dot(cond, w_ref[0].astype(BF16), preferred_element_type=F32) + b_ref[0]

    return pl.pallas_call(
        body, name="ada_fwd", grid=(DEPTH,),
        in_specs=[pl.BlockSpec(c_all.shape, lambda i: (0, 0)),
                  pl.BlockSpec((1, D_MODEL, n_col), lambda i: (i, 0, 0)),
                  pl.BlockSpec((1, 1, n_col), lambda i: (i, 0, 0))],
        out_specs=pl.BlockSpec((1, N_DEV, n_col), lambda i: (i, 0, 0)),
        out_shape=jax.ShapeDtypeStruct((DEPTH, N_DEV, n_col), F32),
        compiler_params=_cparams(("parallel",), VMEM_LIMIT),
    )(c_all, ada_w, ada_b_cols)


def _ada_grad(c_all_t, dmod_cols):
    n_col = dmod_cols.shape[2]

    def body(c_ref, d_ref, o_ref):
        cv = c_ref[...]
        cond = cv * jax.nn.sigmoid(cv)
        o_ref[0] = jnp.dot(cond, d_ref[0], precision=lax.Precision.HIGHEST, preferred_element_type=F32)

    return pl.pallas_call(
        body, name="ada_grad", grid=(DEPTH,),
        in_specs=[pl.BlockSpec(c_all_t.shape, lambda i: (0, 0)),
                  pl.BlockSpec((1, LANES, n_col), lambda i: (i, 0, 0))],
        out_specs=pl.BlockSpec((1, D_MODEL, n_col), lambda i: (i, 0, 0)),
        out_shape=jax.ShapeDtypeStruct((DEPTH, D_MODEL, n_col), F32),
        compiler_params=_cparams(("parallel",), VMEM_LIMIT),
    )(c_all_t, dmod_cols)


ROW_TILE = 256


def _shift_rows(a, n=1):
    return jnp.pad(a, ((n, 0), (0, 0)))[:-n]


def _modulate(x, sc, sh, name):
    def fn(x, sc, sh):
        return (x * (1.0 + sc) + sh,)

    t = x.shape[0]
    return _fused(fn, [_tiled(x, ROW_TILE), _shared(sc), _shared(sh)],
                  [_tiled_out(t, D_MODEL, ROW_TILE)], (t // ROW_TILE,), name)[0]


def _resid_ln_mod(x, y, gate, ln_g, ln_b, sc, sh, name):
    def fn(x, y, gate, ln_g, ln_b, sc, sh):
        x1 = _layer_norm_rows(ALPHA * x + (1.0 + gate) * y, ln_g, ln_b)
        return x1, x1 * (1.0 + sc) + sh

    t = x.shape[0]
    return _fused(fn, [_tiled(x, ROW_TILE), _tiled(y, ROW_TILE)] + [_shared(a) for a in (gate, ln_g, ln_b, sc, sh)],
                  [_tiled_out(t, D_MODEL, ROW_TILE)] * 2, (t // ROW_TILE,), name)


def _resid_ln_loss(x, y, gate, ln_g, ln_b, target, name):
    def fn(x, y, gate, ln_g, ln_b, target):
        x1 = _layer_norm_rows(ALPHA * x + (1.0 + gate) * y, ln_g, ln_b)
        err = jnp.square(x1 - target)
        per_row = jnp.mean(err, axis=-1, keepdims=True)
        return (0.5 * jnp.sum(per_row, axis=0, keepdims=True),)

    t = x.shape[0]
    return _fused(fn, [_tiled(x, ROW_TILE), _tiled(y, ROW_TILE)] + [_shared(a) for a in (gate, ln_g, ln_b)]
                  + [_tiled(target, ROW_TILE)],
                  [((1, 1), (1, 1), _const_map(2), "a")], (t // ROW_TILE,), name)[0]


def _mlp(u, w1, s1, w2, s2, name):
    def run(u, w1, w2):
        h, act = _mm(u, w1, "nn", name + "_w1_fwd", square_relu=True)
        return _mm(act, w2, "nn", name + "_w2_fwd"), (u, w1, w2, h, act)

    @jax.custom_vjp
    def op(u, w1, s1, w2, s2):
        return run(u, w1, w2)[0]

    def fwd(u, w1, s1, w2, s2):
        return run(u, w1, w2)

    def bwd(res, dy):
        u, w1, w2, h, act = res
        dh = _mm(dy, w2, "nt", name + "_w2_dx", pre_act=h)
        dw2 = _mm(act, dy, "tn", name + "_w2_dw")
        du = _mm(dh, w1, "nt", name + "_w1_dx")
        dw1 = _mm(u, dh, "tn", name + "_w1_dw")
        return du, jnp.zeros_like(w1), dw1, jnp.zeros_like(w2), dw2

    op.defvjp(fwd, bwd)
    return op(u, w1, s1, w2, s2)


AB_PIECES = (("r", 0, 512, 512), ("k", 512, 512, 512), ("v", 1024, 512, 512),
             ("wd", 1536, 64, 128), ("ad", 1600, 64, 128), ("gd", 1664, 160, 256),
             ("h", 1824, 512, 512), ("bg", 2336, 512, 512), ("cg", 2848, 512, 512))
AB_PAD_COLS = sum(p[3] for p in AB_PIECES)


def _regroup_cols(w):
    parts = []
    for _, start, width, padded in AB_PIECES:
        piece = w[..., start:start + width]
        if padded != width:
            piece = jnp.pad(piece, [(0, 0)] * (w.ndim - 1) + [(0, padded - width)])
        parts.append(piece)
    return jnp.concatenate(parts, axis=-1)


def _pad_rows(w, rows):
    return jnp.pad(w, ((0, rows - w.shape[0]), (0, 0)))


def _rwkv_shortconv(u, big, wts, wsrc, gslot):
    t = u.shape[0]
    p = _linear(u, _regroup_cols(big["ab_w_in"][0]), _regroup_cols(wts["ab_w_in"][0]), "ab_in")
    mu = _regroup_cols(jnp.pad(wts["rw_mu"], ((0, 0), (0, AB_PROJ - RW_PROJ))))
    w_up = _pad_rows(wts["rw_w_up"][0], 128)
    a_up = _pad_rows(wts["rw_a_up"][0], 128)
    g_up = _pad_rows(wts["rw_g_up"][0], 256)

    def pre(rp, rs, kp, ks, vp, vs, wdp, wds, adp, ads, gdp, gds, h, cg,
            mu_r, mu_k, mu_v, mu_w, mu_a, mu_g, w0, w_up, a0, a_up, g_up, k_k, k_a):
        def mix(pv, sv, m):
            return pv + m * (sv - pv)

        r, k, v = mix(rp, rs, mu_r), mix(kp, ks, mu_k), mix(vp, vs, mu_v)
        wd, ad, gd = mix(wdp, wds, mu_w), mix(adp, ads, mu_a), mix(gdp, gds, mu_g)
        logw = -_softplus(-(w0 + _bdot(jnp.tanh(wd), w_up))) - 0.5
        decay = jnp.exp(-jnp.exp(logw))
        iclr = jax.nn.sigmoid(a0 + _bdot(ad, a_up))
        gate = _bdot(jax.nn.sigmoid(gd), g_up)
        kk = k * k_k
        kk = kk / jnp.maximum(jnp.sqrt(_head_sum(kk * kk)), 1e-12)
        k_h = k * (1.0 + (iclr - 1.0) * k_a)
        return r, decay, k_h, v, -kk, kk * iclr, gate, cg * h

    tile = ROW_TILE
    names = [q[0] for q in AB_PIECES]
    cuts = list(np.cumsum([q[3] for q in AB_PIECES])[:-1])
    pp = dict(zip(names, jnp.split(p, cuts, axis=1)))
    mp = dict(zip(names, jnp.split(mu, cuts, axis=1)))

    ins = []
    for name in ("r", "k", "v", "wd", "ad", "gd"):
        ins += [_tiled(pp[name], tile), _tiled(_shift_rows(pp[name]), tile)]
    ins += [_tiled(pp["h"], tile), _tiled(pp["cg"], tile)]
    ins += [_shared(mp[name]) for name in ("r", "k", "v", "wd", "ad", "gd")]
    ins += [_shared(a) for a in (wts["rw_w0"], w_up, wts["rw_a0"], a_up, g_up, wts["rw_k_k"], wts["rw_k_a"])]
    outs = [_tiled_out(t, RW_WIDTH, tile)] * 8
    r, decay, k_h, v, a, b, gate, z = _fused(pre, ins, outs, (t // tile,), "rwkv_pre")

    y, gathered, token = _rwkv_scan(r, decay, k_h, v, a, b, wsrc, gslot)

    conv_w = wts["sc_conv_w"][0]
    r_k = wts["rw_r_k"].reshape(1, RW_WIDTH)

    def post(y, r, k_h, v, gate, bg, z, z1, z2, lnx_g, lnx_b, r_k, c0, c1, c2):
        mean = _head_sum(y) * (1.0 / HEAD_DIM)
        yc = y - mean
        var = _head_sum(yc * yc) * (1.0 / HEAD_DIM)
        yn = yc * lax.rsqrt(var + RW_GN_EPS) * lnx_g + lnx_b
        bonus = _head_sum(r * k_h * r_k) * v
        return (yn + bonus) * gate, bg * (c0 * z2 + c1 * z1 + c2 * z)

    ins = [_tiled(a_, tile) for a_ in (y, r, k_h, v, gate, pp["bg"])]
    ins += [_tiled(a_, tile) for a_ in (z, _shift_rows(z, 1), _shift_rows(z, 2))]
    ins += [_shared(a_) for a_ in (wts["rw_lnx_g"], wts["rw_lnx_b"], r_k, conv_w[0:1], conv_w[1:2], conv_w[2:3])]
    y_a, y_b = _fused(post, ins, [_tiled_out(t, RW_WIDTH, tile)] * 2, (t // tile,), "rwkv_post")
    out = _linear(jnp.concatenate([y_a, y_b], axis=1), big["ab_w_out"][0], wts["ab_w_out"][0], "ab_out")
    return out, gathered, token


def _t5_bucket_np(dist):
    exact = N_BUCKETS // 2
    logd = np.log(np.maximum(dist, 1).astype(np.float32) / exact) / math.log(MAX_DISTANCE / exact)
    large = np.minimum(exact + (logd * (N_BUCKETS - exact)).astype(np.int32), N_BUCKETS - 1)
    return np.where(dist < exact, dist, large)


def _merge_groups(os_, lses, name):
    t = os_[0].shape[0]
    tile = ROW_TILE

    def fn(o0, o1, o2, l0, l1, l2):
        lane = lax.broadcasted_iota(jnp.int32, (1, LANES), 1)
        lo = lane < HEAD_DIM
        ls = [jnp.where(lo, l[0], l[1]) for l in (l0, l1, l2)]
        m = jnp.maximum(jnp.maximum(ls[0], ls[1]), ls[2])
        es = [jnp.exp(l - m) for l in ls]
        den = es[0] + es[1] + es[2]
        return ((es[0] * o0 + es[1] * o1 + es[2] * o2) / den,)

    ins = [(o, (tile, LANES), lambda i, hp: (i, hp), "t") for o in os_]
    ins += [(l, (2, tile, LANES), lambda i, hp: (hp, i, 0), "t") for l in lses]
    outs = [((t, DIL_WIDTH), (tile, LANES), lambda i, hp: (i, hp), "t")]
    return _fused(fn, ins, outs, (t // tile, N_PAIRS), name)[0]


def _dilated_mixer(u, big, wts):
    p = _linear(u, big["dil_w_qkv"][0], wts["dil_w_qkv"][0], "dil_qkv")
    qi = np.arange(BLOCK)[:, None]
    ki = np.arange(2 * BLOCK)[None, :]
    rel = BLOCK + qi - ki
    biases = []
    for g, (window, dil) in enumerate(DIL_PATTERNS):
        span = window // dil
        bucket = _t5_bucket_np(np.clip(rel, 0, span) * dil).reshape(-1)
        onehot = jnp.asarray(np.eye(N_BUCKETS, dtype=np.float32)[bucket])
        table = wts["rel_bias"][:, g * N_HEADS:(g + 1) * N_HEADS]
        bias = jnp.dot(onehot, table, precision=lax.Precision.HIGHEST)
        biases.append(jnp.transpose(bias.reshape(BLOCK, 2 * BLOCK, N_HEADS), (2, 0, 1)))
    os_, lses = _dilated_attention(p, tuple(biases))
    o = _merge_groups(os_, lses, "dil_merge")
    return _linear(o, big["dil_w_out"][0], wts["dil_w_out"][0], "dil_out")


def _forward_local(x, mods, big, wts, wsrc, gslot, late_shapes, target):
    u = _modulate(x, mods[0, 1], mods[0, 0], "mod_in")
    for i in range(DEPTH):
        sh2, sc2, g1, g2 = mods[i, 3], mods[i, 4], mods[i, 2], mods[i, 5]
        if i == 0:
            y, gathered, token = _rwkv_shortconv(u, big, wts, wsrc, gslot)
            shard_shapes = [s[:SHARDED[n]] + (s[SHARDED[n]] // N_CHIPS,) + s[SHARDED[n] + 1:]
                            for n, s in late_shapes.items()]
            parts = _unpack_chips(gathered, shard_shapes, own=wsrc)
            big = {**big, **{n: _join_chips(p, SHARDED[n]) for n, p in zip(late_shapes, parts)}}
            wts = {**wts, **_gradient_slots(token, late_shapes, tuple(late_shapes))}
        else:
            y = _dilated_mixer(u, big, wts)
        x, u = _resid_ln_mod(x, y, g1, wts["ln_g"][i, 0:1], wts["ln_b"][i, 0:1], sc2, sh2, f"ln_mix{i}")
        y = _mlp(u, big["mlp_w1"][i], wts["mlp_w1"][i], big["mlp_w2"][i], wts["mlp_w2"][i], f"mlp{i}")
        if i + 1 < DEPTH:
            x, u = _resid_ln_mod(x, y, g2, wts["ln_g"][i, 1:2], wts["ln_b"][i, 1:2],
                                 mods[i + 1, 1], mods[i + 1, 0], f"ln_mlp{i}")
        else:
            return _resid_ln_loss(x, y, g2, wts["ln_g"][i, 1:2], wts["ln_b"][i, 1:2], target, "ln_loss")


SHARDED = {"ab_w_in": 2, "ab_w_out": 1, "dil_w_qkv": 2, "dil_w_out": 2, "mlp_w1": 2, "mlp_w2": 1,
           "ln_g": 2, "ln_b": 2, "rw_w_up": 2, "rw_a_up": 2, "rw_g_up": 2, "sc_conv_w": 2}
FIRST_MIXER = ("ab_w_in", "ab_w_out")
LATER_LAYERS = ("dil_w_qkv", "dil_w_out", "mlp_w1", "mlp_w2")
SMALL_SHARDED = ("ln_g", "ln_b", "rw_w_up", "rw_a_up", "rw_g_up", "sc_conv_w")
REPLICATED = ("ada_b", "rw_mu", "rw_w0", "rw_a0", "rw_k_k", "rw_k_a", "rw_r_k", "rw_lnx_g", "rw_lnx_b", "rel_bias")
WEIGHT_ORDER = ("ada_w", "ada_b", "ln_g", "ln_b", "ab_w_in", "rw_mu", "rw_w0", "rw_w_up", "rw_a0", "rw_a_up",
                "rw_g_up", "rw_k_k", "rw_k_a", "rw_r_k", "rw_lnx_g", "rw_lnx_b", "sc_conv_w", "ab_w_out",
                "dil_w_qkv", "dil_w_out", "rel_bias", "mlp_w1", "mlp_w2")


PACK_ROWS = 16


def _rows_of(n_elems):
    return -(-n_elems // (ROW_W * PACK_ROWS)) * PACK_ROWS


def _to_rows(a):
    flat = a.reshape(-1)
    rows = _rows_of(flat.shape[0])
    if rows * ROW_W != flat.shape[0]:
        flat = jnp.pad(flat, (0, rows * ROW_W - flat.shape[0]))
    return flat.reshape(rows, ROW_W)


def _from_rows(rows, shape):
    n = int(np.prod(shape))
    return rows.reshape(-1)[:n].reshape(shape)


def _split_chips(full, axis):
    shp = full.shape
    parts = full.reshape(shp[:axis] + (N_CHIPS, shp[axis] // N_CHIPS) + shp[axis + 1:])
    return jnp.moveaxis(parts, axis, 0)


def _join_chips(parts, axis):
    moved = jnp.moveaxis(parts, 0, axis)
    shp = moved.shape
    return moved.reshape(shp[:axis] + (shp[axis] * shp[axis + 1],) + shp[axis + 2:])


def _pack_rows(arrays, row_multiple=256):
    blocks = [_to_rows(a) for a in arrays]
    total = sum(b.shape[0] for b in blocks)
    pad = (-total) % row_multiple
    if pad:
        blocks.append(jnp.zeros((pad, ROW_W), blocks[0].dtype))
    return jnp.concatenate(blocks, axis=0)


def _unpack_rows(buf, shapes):
    out, r0 = [], 0
    for shp in shapes:
        n = _rows_of(int(np.prod(shp)))
        out.append(_from_rows(buf[r0:r0 + n], shp))
        r0 += n
    return out


def _pack_chips(parts):
    blocks = []
    for p in parts:
        flat = p.reshape(N_CHIPS, -1)
        rows = _rows_of(flat.shape[1])
        if rows * ROW_W != flat.shape[1]:
            flat = jnp.pad(flat, ((0, 0), (0, rows * ROW_W - flat.shape[1])))
        blocks.append(flat.reshape(N_CHIPS, rows, ROW_W))
    total = sum(b.shape[1] for b in blocks)
    pad = (-total) % 256
    if pad:
        blocks.append(jnp.zeros((N_CHIPS, pad, ROW_W), blocks[0].dtype))
    return jnp.concatenate(blocks, axis=1)


def _unpack_chips(buf, shapes, own=None):
    mine = lax.broadcasted_iota(jnp.int32, (N_CHIPS, 1, 1), 0) == _chip_of(_me()) if own is not None else None
    out, r0 = [], 0
    for shp in shapes:
        size = int(np.prod(shp))
        n = _rows_of(size)
        rows = buf[:, r0:r0 + n]
        if own is not None:
            rows = jnp.where(mine, own[None, r0:r0 + n], rows)
        out.append(rows.reshape(N_CHIPS, -1)[:, :size].reshape((N_CHIPS,) + tuple(shp)))
        r0 += n
    return out


def _gradient_slots(token, full_shapes, names):
    @jax.custom_vjp
    def route(token):
        return {n: jnp.zeros(full_shapes[n], F32) for n in names}

    def fwd(token):
        return {n: jnp.zeros(full_shapes[n], F32) for n in names}, None

    def bwd(_, d):
        return (_pack_chips([_split_chips(d[n], SHARDED[n]).astype(BF16) for n in names]),)

    route.defvjp(fwd, bwd)
    return route(token)


def _as2d(a):
    return a.reshape(-1, a.shape[-1])


def kernel(x, c, ada_w, ada_b, ln_g, ln_b, ab_w_in, rw_mu, rw_w0, rw_w_up, rw_a0, rw_a_up, rw_g_up, rw_k_k, rw_k_a, rw_r_k, rw_lnx_g, rw_lnx_b, sc_conv_w, ab_w_out, dil_w_qkv, dil_w_out, rel_bias, mlp_w1, mlp_w2, loss_target, m_ada_w, m_ada_b, m_ln_g, m_ln_b, m_ab_w_in, m_rw_mu, m_rw_w0, m_rw_w_up, m_rw_a0, m_rw_a_up, m_rw_g_up, m_rw_k_k, m_rw_k_a, m_rw_r_k, m_rw_lnx_g, m_rw_lnx_b, m_sc_conv_w, m_ab_w_out, m_dil_w_qkv, m_dil_w_out, m_rel_bias, m_mlp_w1, m_mlp_w2, v_ada_w, v_ada_b, v_ln_g, v_ln_b, v_ab_w_in, v_rw_mu, v_rw_w0, v_rw_w_up, v_rw_a0, v_rw_a_up, v_rw_g_up, v_rw_k_k, v_rw_k_a, v_rw_r_k, v_rw_lnx_g, v_rw_lnx_b, v_sc_conv_w, v_ab_w_out, v_dil_w_qkv, v_dil_w_out, v_rel_bias, v_mlp_w1, v_mlp_w2):
    args = dict(locals())
    w_in = {n: args[n] for n in WEIGHT_ORDER}
    m_in = {n: args["m_" + n] for n in WEIGHT_ORDER}
    v_in = {n: args["v_" + n] for n in WEIGHT_ORDER}
    me = _me()
    chip = _chip_of(me)
    dev = _dev_of(me)

    c_all = _dev_all_gather(c, "gather_c")[:, 0, :]
    n_col = ada_w.shape[2]
    ada_b_cols = lax.dynamic_slice_in_dim(ada_b, chip * n_col, n_col, axis=1)[:, None, :]
    mod_cols = _ada_fwd(c_all, ada_w, ada_b_cols)

    first_buf = _pack_rows([w_in[n].astype(BF16) for n in FIRST_MIXER])
    first_all = _two_level_gather(first_buf, "gather_first")
    late_buf = _pack_rows([w_in[n].astype(BF16) for n in LATER_LAYERS])
    small_buf = _pack_rows([mod_cols] + [w_in[n] for n in SMALL_SHARDED], row_multiple=PACK_ROWS)
    small_all = _chip_all_gather(small_buf, "gather_small")

    def full_shape(n):
        shp = w_in[n].shape
        return shp[:SHARDED[n]] + (shp[SHARDED[n]] * N_CHIPS,) + shp[SHARDED[n] + 1:]

    wts = {n: w_in[n] for n in REPLICATED}
    big = {}
    for n, part in zip(FIRST_MIXER, _unpack_chips(first_all, [w_in[n].shape for n in FIRST_MIXER])):
        big[n] = _join_chips(part, SHARDED[n])
        wts[n] = jnp.zeros(full_shape(n), F32)
    small_parts = _unpack_chips(small_all, [mod_cols.shape] + [w_in[n].shape for n in SMALL_SHARDED])
    for n, part in zip(SMALL_SHARDED, small_parts[1:]):
        wts[n] = _join_chips(part, SHARDED[n])
    mod_all = _join_chips(small_parts[0], 2)
    mods = lax.dynamic_slice_in_dim(mod_all, dev, 1, axis=1).reshape(DEPTH, 6, 1, D_MODEL)
    late_shapes = {n: full_shape(n) for n in LATER_LAYERS}
    late_slot = jnp.zeros(late_buf.shape, F32)

    def local_loss(xv, modv, wv, slot):
        return _forward_local(xv, modv, big, wv, late_buf, slot, late_shapes, loss_target[0])[0, 0]

    loss_local, (grad_x, dmods, dw, late_part) = jax.value_and_grad(local_loss, argnums=(0, 1, 2, 3))(
        x[0], mods, wts, late_slot)
    loss = lax.psum(loss_local, ("x", "y", "c"))

    small_row = jnp.concatenate([dmods.reshape(-1)] + [dw[n].reshape(-1) for n in REPLICATED[1:]])
    n_small = small_row.shape[0]
    n_small_pad = -(-n_small // LANES) * LANES
    small_row = jnp.pad(small_row, (0, n_small_pad - n_small))[None, :]
    rows_all = _dev_all_gather(small_row, "gather_small_grads")
    small_sum = _sum_slots(rows_all, "sum_small_grads")

    dmod_all = rows_all[:, 0, :DEPTH * 6 * D_MODEL].reshape(N_DEV, DEPTH, 6 * D_MODEL)
    dmod_cols = lax.dynamic_slice_in_dim(dmod_all, chip * n_col, n_col, axis=2)
    dmod_cols = jnp.pad(jnp.moveaxis(dmod_cols, 0, 1), ((0, 0), (0, LANES - N_DEV), (0, 0)))
    c_all_t = jnp.pad(c_all.T, ((0, 0), (0, LANES - N_DEV)))
    grads = {"ada_w": _ada_grad(c_all_t, dmod_cols)}
    grads["ada_b"] = small_sum[0, :DEPTH * 6 * D_MODEL].reshape(ada_b.shape)
    r0 = DEPTH * 6 * D_MODEL
    for n in REPLICATED[1:]:
        size = int(np.prod(w_in[n].shape))
        grads[n] = small_sum[0, r0:r0 + size].reshape(w_in[n].shape)
        r0 += size

    sharded_names = FIRST_MIXER + SMALL_SHARDED
    send = _pack_chips([_split_chips(dw[n], SHARDED[n]).astype(BF16) for n in sharded_names])
    n_rows = send.shape[1]
    send = send.reshape(N_CHIPS, 2, n_rows // 2, ROW_W)
    theirs = _core_halves(send, "swap_halves")
    mine = lax.dynamic_index_in_dim(send, me[2], 1, keepdims=False)
    chip_part = _add_pairs(mine, theirs, "sum_cores")
    recv = _chip_scatter(chip_part, "scatter_grads")
    half_sum = _sum_slots(recv, "sum_chips")
    g_rows = _core_all_gather(half_sum, "gather_halves").reshape(n_rows, ROW_W)
    for n, g in zip(sharded_names, _unpack_rows(g_rows, [w_in[n].shape for n in sharded_names])):
        grads[n] = g

    late_rows = _sum_slots(_core_all_gather(late_part, "swap_late"), "sum_cores_late")
    for n, g in zip(LATER_LAYERS, _unpack_rows(late_rows, [w_in[n].shape for n in LATER_LAYERS])):
        grads[n] = g

    deltas, new_m, new_v = {}, {}, {}
    for n in WEIGHT_ORDER:
        shp = w_in[n].shape
        d, nm, nv = _adamw(_as2d(w_in[n]), _as2d(grads[n]), _as2d(m_in[n]), _as2d(v_in[n]), "adamw_" + n)
        deltas[n], new_m[n], new_v[n] = d.reshape(shp), nm.reshape(shp), nv.reshape(shp)

    return (loss, grad_x[None], *[grads[n] for n in WEIGHT_ORDER], *[deltas[n] for n in WEIGHT_ORDER],
            *[new_m[n] for n in WEIGHT_ORDER], *[new_v[n] for n in WEIGHT_ORDER])
```

```python
import functools
import math

import numpy as np
import jax
import jax.numpy as jnp
from jax import lax
from jax.experimental import pallas as pl
from jax.experimental.pallas import tpu as pltpu

F32 = jnp.float32
BF16 = jnp.bfloat16
MESH = pl.DeviceIdType.MESH

D_MODEL = 1024
DEPTH = 2
RW_WIDTH = 512
HEAD_DIM = 64
N_HEADS = 8
RW_DECAY_RANK = 64
RW_ICLR_RANK = 64
RW_GATE_RANK = 160
RW_GN_EPS = 64e-5
RW_PROJ = 3 * RW_WIDTH + RW_DECAY_RANK + RW_ICLR_RANK + RW_GATE_RANK
SC_WIDTH = 512
AB_PROJ = RW_PROJ + 3 * SC_WIDTH
DIL_PATTERNS = ((128, 1), (512, 4), (2048, 16))
N_GROUPS = 3
DIL_WIDTH = 512
DIL_PROJ = N_GROUPS * 3 * DIL_WIDTH
BLOCK = 128
N_BUCKETS = 32
MAX_DISTANCE = 2048
D_FF = 4 * D_MODEL
ALPHA = (2 * DEPTH) ** 0.25
LN_EPS = 1e-5
ADAM_LR = 0.001
ADAM_B1 = 0.9
ADAM_B2 = 0.999
ADAM_EPS = 1e-08
ADAM_WD = 0.01
ADAM_STEP = 10

N_CHIPS = 4
N_DEV = 8
LANES = 128
SUBLANES = 8
ROW_W = 1024
SCAN_CHUNK = 16
VMEM_LIMIT = 48 * 1024 * 1024
NEG_BIG = -1e30


def _pick(n, cands):
    for c in cands:
        if n % c == 0:
            return c
    return n


def _cparams(sem=None, vmem=None):
    return pltpu.CompilerParams(dimension_semantics=sem, vmem_limit_bytes=vmem)


_DOT_DIMS = {
    "nn": (((1,), (0,)), ((), ())),
    "nt": (((1,), (1,)), ((), ())),
    "tn": (((0,), (0,)), ((), ())),
}


def _mm(a, b, mode, name, square_relu=False, pre_act=None):
    if mode == "nn":
        (m, k), (_, n) = a.shape, b.shape
    elif mode == "nt":
        (m, k), (n, _) = a.shape, b.shape
    else:
        (k, m), (_, n) = a.shape, b.shape
    tm = _pick(m, (1024, 512, 256, 128))
    tn = _pick(n, (1024, 768, 512, 384, 256, 128))
    wide_k = mode != "tn" and pre_act is None
    tk = _pick(k, (2048, 1024, 512, 256, 128) if wide_k else (1024, 512, 256, 128))
    nk = k // tk
    if mode == "tn":
        a_spec = pl.BlockSpec((tk, tm), lambda i, j, kk: (kk, i))
    else:
        a_spec = pl.BlockSpec((tm, tk), lambda i, j, kk: (i, kk))
    if mode == "nt":
        b_spec = pl.BlockSpec((tn, tk), lambda i, j, kk: (j, kk))
    else:
        b_spec = pl.BlockSpec((tk, tn), lambda i, j, kk: (kk, j))
    dims = _DOT_DIMS[mode]

    out_spec = pl.BlockSpec((tm, tn), lambda i, j, kk: (i, j))

    def body(*refs):
        a_ref, b_ref = refs[:2]
        h_ref = refs[2] if pre_act is not None else None
        o_ref = refs[3] if pre_act is not None else refs[2]
        act_ref = refs[-1] if square_relu else None
        part = lax.dot_general(a_ref[...].astype(BF16), b_ref[...].astype(BF16), dims, preferred_element_type=F32)

        def finish(total):
            if h_ref is not None:
                total = total * (2.0 * jnp.maximum(h_ref[...], 0.0))
            o_ref[...] = total
            if act_ref is not None:
                act_ref[...] = jnp.square(jnp.maximum(total, 0.0)).astype(BF16)

        if nk == 1:
            finish(part)
        else:
            kk = pl.program_id(2)

            @pl.when(kk == 0)
            def _():
                o_ref[...] = part

            @pl.when(jnp.logical_and(kk > 0, kk < nk - 1))
            def _():
                o_ref[...] += part

            @pl.when(kk == nk - 1)
            def _():
                finish(o_ref[...] + part)

    operands = [a, b] + ([pre_act] if pre_act is not None else [])
    in_specs = [a_spec, b_spec] + ([out_spec] if pre_act is not None else [])
    out_shape = [jax.ShapeDtypeStruct((m, n), F32)] + ([jax.ShapeDtypeStruct((m, n), BF16)] if square_relu else [])
    outs = pl.pallas_call(
        body, name=name, grid=(m // tm, n // tn, nk),
        in_specs=in_specs, out_specs=[out_spec] * len(out_shape), out_shape=out_shape,
        compiler_params=_cparams(("parallel", "parallel", "arbitrary"), VMEM_LIMIT),
    )(*operands)
    return tuple(outs) if square_relu else outs[0]


def _linear(x, w, slot, name):
    @jax.custom_vjp
    def op(x, w, slot):
        return _mm(x, w, "nn", name + "_fwd")

    def fwd(x, w, slot):
        return _mm(x, w, "nn", name + "_fwd"), (x, w)

    def bwd(res, dy):
        x, w = res
        return _mm(dy, w, "nt", name + "_dx"), jnp.zeros_like(w), _mm(x, dy, "tn", name + "_dw")

    op.defvjp(fwd, bwd)
    return op(x, w, slot)


def _const_map(ndim):
    return lambda *g: (0,) * ndim


def _first_step(n_grid):
    return functools.reduce(jnp.logical_and, [pl.program_id(d) == 0 for d in range(n_grid)])


def _fused(fn, ins, outs, grid, name):
    arrays = [i[0] for i in ins]
    n_in, n_out, n_grid = len(ins), len(outs), len(grid)
    in_specs = [pl.BlockSpec(bs, im) for (_, bs, im, _) in ins]
    out_specs = [pl.BlockSpec(bs, im) for (_, bs, im, _) in outs]
    out_shapes = [jax.ShapeDtypeStruct(s, F32) for (s, _, _, _) in outs]
    sem = ("arbitrary",) * n_grid

    def fwd_call(*xs):
        def body(*refs):
            vals = [r[...] for r in refs[:n_in]]
            ys = fn(*vals)
            first = _first_step(n_grid)
            for o_ref, y, (_, _, _, kind) in zip(refs[n_in:], ys, outs):
                if kind == "t":
                    o_ref[...] = y
                else:
                    @pl.when(first)
                    def _(o_ref=o_ref):
                        o_ref[...] = jnp.zeros_like(o_ref)

                    o_ref[...] += y

        return pl.pallas_call(
            body, name=name + "_fwd", grid=grid, in_specs=in_specs, out_specs=out_specs,
            out_shape=out_shapes, compiler_params=_cparams(sem, VMEM_LIMIT))(*xs)

    def bwd_call(xs, dys):
        d_specs = [pl.BlockSpec(bs, im) for (_, bs, im, _) in outs]
        g_specs = [pl.BlockSpec(bs, im) for (_, bs, im, _) in ins]
        g_shapes = [jax.ShapeDtypeStruct(a.shape, F32) for a in arrays]

        def body(*refs):
            vals = [r[...] for r in refs[:n_in]]
            dvals = tuple(r[...] for r in refs[n_in:n_in + n_out])
            _, vjp = jax.vjp(lambda *v: tuple(fn(*v)), *vals)
            gs = vjp(dvals)
            first = _first_step(n_grid)
            for g_ref, g, (_, _, _, kind) in zip(refs[n_in + n_out:], gs, ins):
                if kind == "t":
                    g_ref[...] = g
                else:
                    @pl.when(first)
                    def _(g_ref=g_ref):
                        g_ref[...] = jnp.zeros_like(g_ref)

                    g_ref[...] += g

        return pl.pallas_call(
            body, name=name + "_bwd", grid=grid, in_specs=in_specs + d_specs, out_specs=g_specs,
            out_shape=g_shapes, compiler_params=_cparams(sem, VMEM_LIMIT))(*xs, *dys)

    @jax.custom_vjp
    def op(*xs):
        return tuple(fwd_call(*xs))

    def op_fwd(*xs):
        return tuple(fwd_call(*xs)), xs

    def op_bwd(xs, dys):
        return tuple(bwd_call(xs, dys))

    op.defvjp(op_fwd, op_bwd)
    return op(*arrays)


def _tiled(a, tile, cols=None, col_block=0):
    cols = a.shape[1] if cols is None else cols
    return (a, (tile, cols), lambda i, cb=col_block: (i, cb), "t")


def _shared(a):
    return (a, a.shape, _const_map(a.ndim), "b")


def _tiled_out(rows, cols, tile):
    return ((rows, cols), (tile, cols), lambda i: (i, 0), "t")


@jax.custom_vjp
def _bdot(x, w):
    return jnp.dot(x.astype(BF16), w.astype(BF16), preferred_element_type=F32)


def _bdot_fwd(x, w):
    return _bdot(x, w), (x, w)


def _bdot_bwd(res, dy):
    x, w = res
    dyb = dy.astype(BF16)
    dx = lax.dot_general(dyb, w.astype(BF16), _DOT_DIMS["nt"], preferred_element_type=F32)
    dw = lax.dot_general(x.astype(BF16), dyb, _DOT_DIMS["tn"], preferred_element_type=F32)
    return dx, dw


_bdot.defvjp(_bdot_fwd, _bdot_bwd)


def _head_sum(x):
    n = x.shape[-1]
    hi = lax.broadcasted_iota(jnp.int32, (n, n), 0) // HEAD_DIM
    hj = lax.broadcasted_iota(jnp.int32, (n, n), 1) // HEAD_DIM
    e = (hi == hj).astype(F32)
    return jnp.dot(x, e, precision=lax.Precision.HIGHEST, preferred_element_type=F32)


def _softplus(x):
    return jnp.maximum(x, 0.0) + jnp.log1p(jnp.exp(-jnp.abs(x)))


def _layer_norm_rows(z, g, b):
    mu = jnp.mean(z, axis=-1, keepdims=True)
    zc = z - mu
    var = jnp.mean(zc * zc, axis=-1, keepdims=True)
    return zc * lax.rsqrt(var + LN_EPS) * g + b


N_PAIRS = N_HEADS // 2


def _scan_consts():
    k = lax.broadcasted_iota(jnp.int32, (HEAD_DIM, LANES), 0)
    j = lax.broadcasted_iota(jnp.int32, (HEAD_DIM, LANES), 1)
    diag = (j % HEAD_DIM) == k
    jj = lax.broadcasted_iota(jnp.int32, (LANES, LANES), 0) // HEAD_DIM
    ll = lax.broadcasted_iota(jnp.int32, (LANES, LANES), 1) // HEAD_DIM
    same_head = (jj == ll).astype(BF16)
    lane_lo = j < HEAD_DIM
    return diag, same_head, lane_lo


def _unrolled(n, body, carry):
    for i in range(n):
        carry = body(i, carry)
    return carry


def _fill_cols(srcs, col_ref, diag_bf, same_head, n_steps):
    del diag_bf, same_head
    assert n_steps == 16
    blocks = []
    for src in srcs:
        x = src[...]
        hi = x.astype(BF16).astype(F32)
        r1 = x - hi
        mid = r1.astype(BF16).astype(F32)
        x48 = jnp.concatenate([hi, mid, r1 - mid], axis=0)
        for hp in range(N_PAIRS):
            xp = x48[:, hp * LANES:(hp + 1) * LANES]
            y = jnp.concatenate([xp, pltpu.roll(xp, HEAD_DIM, 1), jnp.zeros((32, LANES), F32)], axis=0)
            blocks.append(y.T[:HEAD_DIM].astype(BF16))
    lhs = jnp.concatenate(blocks, axis=0)
    j = lax.broadcasted_iota(jnp.int32, (LANES, LANES), 0)
    lane_head = lax.broadcasted_iota(jnp.int32, (LANES, LANES), 1) // HEAD_DIM
    for t in range(n_steps):
        pick = jnp.logical_and(j < 96, jnp.logical_and(j % 16 == t, j // 48 == lane_head))
        out = jnp.dot(lhs, pick.astype(BF16), preferred_element_type=F32)
        for vi in range(len(srcs)):
            for hp in range(N_PAIRS):
                r0 = (vi * N_PAIRS + hp) * HEAD_DIM
                col_ref[vi, hp, t] = out[r0:r0 + HEAD_DIM]


def _scan_fwd_call(r, w, k, v, a, b, wsrc):
    t_len = r.shape[0]
    ch = SCAN_CHUNK
    n_ch = t_len // ch
    w_rows, w_cols = wsrc.shape
    half = w_rows // 2
    chunks = _n_chunks(half)
    rc = half // chunks
    transfers = [(q, kk) for q in range(chunks) for kk in range(len(CHIP_FLIPS))]
    hand_on_step = [max(1, min(n_ch - 1, (i + 1) * (n_ch - 8) // len(transfers) + 3)) for i in range(len(transfers))]

    def body(r_ref, w_ref, k_ref, v_ref, a_ref, b_ref, src_ref, y_ref, ck_ref, g_ref,
             st_ref, col_ref, send1, recv1, send2, recv2):
        c = pl.program_id(0)
        me = _me()
        core = me[2]
        sibling = _flip(me, CORE_FLIP)

        def over_ici(q, kk):
            return pltpu.make_async_remote_copy(
                src_ref=src_ref.at[core, pl.ds(q * rc, rc)], dst_ref=g_ref.at[_chip_of(me), core, pl.ds(q * rc, rc)],
                send_sem=send1.at[kk * chunks + q], recv_sem=recv1.at[kk * chunks + q],
                device_id=_flip(me, CHIP_FLIPS[kk]), device_id_type=MESH)

        def hand_on(q, kk):
            landed = g_ref.at[_chip_of(_flip(me, CHIP_FLIPS[kk])), core, pl.ds(q * rc, rc)]
            return pltpu.make_async_remote_copy(
                src_ref=landed, dst_ref=landed, send_sem=send2.at[kk * chunks + q], recv_sem=recv2.at[kk * chunks + q],
                device_id=sibling, device_id_type=MESH)

        @pl.when(c == 0)
        def _():
            st_ref[...] = jnp.zeros_like(st_ref)
            for q, kk in transfers:
                over_ici(q, kk).start()

        for step in sorted(set(hand_on_step)):
            @pl.when(c == step)
            def _(step=step):
                for (q, kk), s in zip(transfers, hand_on_step):
                    if s == step:
                        over_ici(q, kk).wait_recv()
                        hand_on(q, kk).start()

        ck_ref[0] = st_ref[...]
        diag, same_head, _ = _scan_consts()
        diag_bf = diag.astype(BF16)
        _fill_cols((w_ref, a_ref, b_ref, k_ref, r_ref), col_ref, diag_bf, same_head, ch)

        def step(t, states):
            new = []
            for hp in range(N_PAIRS):
                lanes = slice(hp * LANES, (hp + 1) * LANES)
                s = states[hp]
                sa = jnp.sum(s * col_ref[1, hp, t], axis=0, keepdims=True)
                s = s * col_ref[0, hp, t] + col_ref[2, hp, t] * sa + col_ref[3, hp, t] * v_ref[pl.ds(t, 1), lanes]
                y_ref[pl.ds(t, 1), lanes] = jnp.sum(s * col_ref[4, hp, t], axis=0, keepdims=True)
                new.append(s)
            return tuple(new)

        states = _unrolled(ch, step, tuple(st_ref[hp] for hp in range(N_PAIRS)))
        for hp in range(N_PAIRS):
            st_ref[hp] = states[hp]

        @pl.when(c == n_ch - 1)
        def _():
            for q, kk in transfers:
                hand_on(q, kk).wait_recv()
            for q, kk in transfers:
                over_ici(q, kk).wait_send()
                hand_on(q, kk).wait_send()

    row_spec = pl.BlockSpec((ch, RW_WIDTH), lambda c: (c, 0))
    any_spec = pl.BlockSpec(memory_space=pl.ANY)
    n_sem = len(transfers)
    y, ck, got = pl.pallas_call(
        body, name="rwkv_scan_fwd", grid=(n_ch,),
        in_specs=[row_spec] * 6 + [any_spec],
        out_specs=[row_spec, pl.BlockSpec((1, N_PAIRS, HEAD_DIM, LANES), lambda c: (c, 0, 0, 0)), any_spec],
        out_shape=[jax.ShapeDtypeStruct((t_len, RW_WIDTH), F32),
                   jax.ShapeDtypeStruct((n_ch, N_PAIRS, HEAD_DIM, LANES), F32),
                   jax.ShapeDtypeStruct((N_CHIPS, 2, half, w_cols), wsrc.dtype)],
        scratch_shapes=[pltpu.VMEM((N_PAIRS, HEAD_DIM, LANES), F32),
                        pltpu.VMEM((5, N_PAIRS, ch, HEAD_DIM, LANES), F32)]
        + [pltpu.SemaphoreType.DMA((n_sem,)) for _ in range(4)],
        compiler_params=_cparams(("arbitrary",), VMEM_LIMIT),
    )(r, w, k, v, a, b, wsrc.reshape(2, half, w_cols))
    return y, ck, got.reshape(N_CHIPS, w_rows, w_cols)


def _scan_bwd_call(r, w, k, v, a, b, ck, dy, gsend):
    t_len = r.shape[0]
    ch = SCAN_CHUNK
    n_ch = t_len // ch
    _, g_rows, g_cols = gsend.shape
    chunks = _n_chunks(g_rows)
    rc = g_rows // chunks
    transfers = [(q, kk) for q in range(chunks) for kk in range(len(CHIP_FLIPS))]

    def body(r_ref, w_ref, k_ref, v_ref, a_ref, b_ref, ck_ref, dy_ref, gs_ref,
             dr_ref, dw_ref, dk_ref, dv_ref, da_ref, db_ref, gr_ref,
             ds_ref, col_ref, sp_ref, sa_ref, send_sems, recv_sems):
        c = pl.program_id(0)
        me = _me()

        def to_chip(q, kk):
            peer = _flip(me, CHIP_FLIPS[kk])
            return pltpu.make_async_remote_copy(
                src_ref=gs_ref.at[_chip_of(peer), pl.ds(q * rc, rc)], dst_ref=gr_ref.at[_chip_of(me), pl.ds(q * rc, rc)],
                send_sem=send_sems.at[kk * chunks + q], recv_sem=recv_sems.at[kk * chunks + q],
                device_id=peer, device_id_type=MESH)

        @pl.when(c == 0)
        def _():
            ds_ref[...] = jnp.zeros_like(ds_ref)
            for q, kk in transfers:
                to_chip(q, kk).start()

        diag, same_head, lane_lo = _scan_consts()
        diag_bf = diag.astype(BF16)
        diag_f = diag.astype(F32)
        _fill_cols((w_ref, a_ref, b_ref, k_ref, r_ref), col_ref, diag_bf, same_head, ch)

        def replay(t, states):
            new = []
            for hp in range(N_PAIRS):
                lanes = slice(hp * LANES, (hp + 1) * LANES)
                s = states[hp]
                sp_ref[t, hp] = s
                sa = jnp.sum(s * col_ref[1, hp, t], axis=0, keepdims=True)
                sa_ref[pl.ds(t, 1), lanes] = sa
                new.append(s * col_ref[0, hp, t] + col_ref[2, hp, t] * sa
                           + col_ref[3, hp, t] * v_ref[pl.ds(t, 1), lanes])
            return tuple(new)

        _unrolled(ch, replay, tuple(ck_ref[0, hp] for hp in range(N_PAIRS)))

        def key_rows(ps):
            stacked = jnp.concatenate([p.astype(BF16) for p in ps], axis=0)
            q = jnp.dot(stacked, same_head, preferred_element_type=F32)
            return [jnp.sum(q[i * HEAD_DIM:(i + 1) * HEAD_DIM] * diag_f, axis=0, keepdims=True)
                    for i in range(len(ps))]

        def back(i, grads):
            t = ch - 1 - i
            new = []
            for hp in range(N_PAIRS):
                lanes = slice(hp * LANES, (hp + 1) * LANES)
                wc, ac, bc, kc, rc = (col_ref[vi, hp, t] for vi in range(5))
                sp = sp_ref[t, hp]
                sa = sa_ref[pl.ds(t, 1), lanes]
                vrow = v_ref[pl.ds(t, 1), lanes]
                dyrow = dy_ref[pl.ds(t, 1), lanes]
                st = sp * wc + bc * sa + kc * vrow
                g = grads[hp] + rc * dyrow
                dsa = jnp.sum(g * bc, axis=0, keepdims=True)
                dv_ref[pl.ds(t, 1), lanes] = jnp.sum(g * kc, axis=0, keepdims=True)
                rows = key_rows([st * dyrow, g * vrow, g * sa, g * sp, sp * dsa])
                for out_ref, row in zip((dr_ref, dk_ref, db_ref, dw_ref, da_ref), rows):
                    out_ref[pl.ds(t, 1), lanes] = row
                new.append(g * wc + ac * dsa)
            return tuple(new)

        grads = _unrolled(ch, back, tuple(ds_ref[hp] for hp in range(N_PAIRS)))
        for hp in range(N_PAIRS):
            ds_ref[hp] = grads[hp]

        @pl.when(c == n_ch - 1)
        def _():
            for q, kk in transfers:
                to_chip(q, kk).wait()

    row_spec = pl.BlockSpec((ch, RW_WIDTH), lambda c: (n_ch - 1 - c, 0))
    any_spec = pl.BlockSpec(memory_space=pl.ANY)
    out_sds = jax.ShapeDtypeStruct((t_len, RW_WIDTH), F32)
    n_sem = len(transfers)
    return pl.pallas_call(
        body, name="rwkv_scan_bwd", grid=(n_ch,),
        in_specs=[row_spec] * 6 + [pl.BlockSpec((1, N_PAIRS, HEAD_DIM, LANES), lambda c: (n_ch - 1 - c, 0, 0, 0)),
                                   row_spec, any_spec],
        out_specs=[row_spec] * 6 + [any_spec],
        out_shape=[out_sds] * 6 + [jax.ShapeDtypeStruct(gsend.shape, gsend.dtype)],
        scratch_shapes=[pltpu.VMEM((N_PAIRS, HEAD_DIM, LANES), F32),
                        pltpu.VMEM((5, N_PAIRS, ch, HEAD_DIM, LANES), F32),
                        pltpu.VMEM((ch, N_PAIRS, HEAD_DIM, LANES), F32),
                        pltpu.VMEM((ch, RW_WIDTH), F32),
                        pltpu.SemaphoreType.DMA((n_sem,)), pltpu.SemaphoreType.DMA((n_sem,))],
        compiler_params=_cparams(("arbitrary",), VMEM_LIMIT),
    )(r, w, k, v, a, b, ck, dy, gsend)


def _scan_token(wsrc):
    return jnp.zeros((N_CHIPS,) + wsrc.shape, BF16)


@jax.custom_vjp
def _rwkv_scan(r, w, k, v, a, b, wsrc, gslot):
    y, _, got = _scan_fwd_call(r, w, k, v, a, b, wsrc)
    return y, got, _scan_token(wsrc)


def _rwkv_scan_fwd(r, w, k, v, a, b, wsrc, gslot):
    y, ck, got = _scan_fwd_call(r, w, k, v, a, b, wsrc)
    return (y, got, _scan_token(wsrc)), (r, w, k, v, a, b, ck, wsrc)


def _rwkv_scan_bwd(res, cts):
    *saved, wsrc = res
    dy, _, gsend = cts
    *dins, got = _scan_bwd_call(*saved, dy, gsend)
    summed = _sum_slots_own(got, gsend, _chip_of(_me()), "sum_chips_late")
    return (*dins, jnp.zeros_like(wsrc), summed)


_rwkv_scan.defvjp(_rwkv_scan_fwd, _rwkv_scan_bwd)


ATT_SCALE = HEAD_DIM ** -0.5
COLS_PER_POS = DIL_PROJ // DIL_WIDTH


def _att_masks():
    qi = lax.broadcasted_iota(jnp.int32, (BLOCK, BLOCK), 0)
    ki = lax.broadcasted_iota(jnp.int32, (BLOCK, BLOCK), 1)
    lane = lax.broadcasted_iota(jnp.int32, (1, LANES), 1)
    return ki <= qi, ki >= qi, lane


def _att_fwd_call(p, bias, g, dil):
    t_len = p.shape[0]
    l_len = t_len // dil
    nb = l_len // BLOCK

    def body(q_ref, kc_ref, kp_ref, vc_ref, vp_ref, bias_ref, o_ref, lse_ref):
        n = pl.program_id(1)
        cur_ok, prev_band, lane = _att_masks()
        prev_ok = jnp.logical_and(prev_band, n > 0)
        for hp in range(N_PAIRS):
            lanes = slice(hp * LANES, (hp + 1) * LANES)
            q2 = q_ref[:, lanes].astype(BF16)
            kc = kc_ref[:, lanes].astype(BF16)
            kp = kp_ref[:, lanes].astype(BF16)
            vc = vc_ref[:, lanes].astype(BF16)
            vp = vp_ref[:, lanes].astype(BF16)
            o2 = jnp.zeros((BLOCK, LANES), F32)
            for hh in range(2):
                h = 2 * hp + hh
                mine = (lane // HEAD_DIM) == hh
                qm = jnp.where(mine, q2, jnp.zeros_like(q2))
                s_c = lax.dot_general(qm, kc, _DOT_DIMS["nt"], preferred_element_type=F32) * ATT_SCALE
                s_p = lax.dot_general(qm, kp, _DOT_DIMS["nt"], preferred_element_type=F32) * ATT_SCALE
                s_c = jnp.where(cur_ok, s_c + bias_ref[h, :, BLOCK:], NEG_BIG)
                s_p = jnp.where(prev_ok, s_p + bias_ref[h, :, :BLOCK], NEG_BIG)
                m = jnp.maximum(jnp.max(s_c, axis=-1, keepdims=True), jnp.max(s_p, axis=-1, keepdims=True))
                e_c = jnp.exp(s_c - m)
                e_p = jnp.exp(s_p - m)
                den = jnp.sum(e_c, axis=-1, keepdims=True) + jnp.sum(e_p, axis=-1, keepdims=True)
                o_h = (jnp.dot((e_c / den).astype(BF16), vc, preferred_element_type=F32)
                       + jnp.dot((e_p / den).astype(BF16), vp, preferred_element_type=F32))
                o2 = o2 + jnp.where(mine, o_h, 0.0)
                lse_ref[h] = jnp.broadcast_to(m + jnp.log(den), (BLOCK, LANES))
            o_ref[:, lanes] = o2

    def col(j):
        return lambda r, n: (r * nb + n, j)

    def col_prev(j):
        return lambda r, n: (r * nb + jnp.maximum(n - 1, 0), j)

    blk = (BLOCK, DIL_WIDTH)
    return pl.pallas_call(
        body, name=f"dil_att_fwd_g{g}", grid=(dil, nb),
        in_specs=[pl.BlockSpec(blk, col(0)), pl.BlockSpec(blk, col(1)), pl.BlockSpec(blk, col_prev(1)),
                  pl.BlockSpec(blk, col(2)), pl.BlockSpec(blk, col_prev(2)),
                  pl.BlockSpec(bias.shape, _const_map(3))],
        out_specs=[pl.BlockSpec(blk, lambda r, n: (r * nb + n, 0)),
                   pl.BlockSpec((N_HEADS, BLOCK, LANES), lambda r, n: (0, r * nb + n, 0))],
        out_shape=[jax.ShapeDtypeStruct((t_len, DIL_WIDTH), F32),
                   jax.ShapeDtypeStruct((N_HEADS, t_len, LANES), F32)],
        compiler_params=_cparams(("arbitrary", "arbitrary"), VMEM_LIMIT),
    )(p, p, p, p, p, bias)


def _att_bwd_call(p, bias, o, lse, do, dlse, g, dil):
    t_len = p.shape[0]
    l_len = t_len // dil
    nb = l_len // BLOCK

    def body(q_ref, qn_ref, k_ref, v_ref, do_ref, don_ref, o_ref, on_ref, lse_ref, lsen_ref, dl_ref, dln_ref,
             bias_ref, dq_ref, dk_ref, dv_ref, dbias_ref, carry_ref):
        r = pl.program_id(0)
        n = pl.program_id(1)
        cur_ok, prev_band, lane = _att_masks()
        has_next = n + 1 < nb

        @pl.when(jnp.logical_and(r == 0, n == 0))
        def _():
            dbias_ref[...] = jnp.zeros_like(dbias_ref)

        @pl.when(n == 0)
        def _():
            carry_ref[...] = jnp.zeros_like(carry_ref)

        for hp in range(N_PAIRS):
            lanes = slice(hp * LANES, (hp + 1) * LANES)
            k2 = k_ref[:, lanes].astype(BF16)
            v2 = v_ref[:, lanes].astype(BF16)
            dk2 = jnp.zeros((BLOCK, LANES), F32)
            dv2 = jnp.zeros((BLOCK, LANES), F32)
            dq_cur = carry_ref[:, lanes]
            dq_next = jnp.zeros((BLOCK, LANES), F32)
            for hh in range(2):
                h = 2 * hp + hh
                mine = (lane // HEAD_DIM) == hh
                tiles = (
                    (q_ref, do_ref, o_ref, lse_ref, dl_ref, cur_ok, slice(BLOCK, 2 * BLOCK), None),
                    (qn_ref, don_ref, on_ref, lsen_ref, dln_ref, prev_band, slice(0, BLOCK), has_next),
                )
                for ti, (qr, dor, orf, lr, dlr, ok, bcols, gate) in enumerate(tiles):
                    q2 = qr[:, lanes].astype(BF16)
                    qm = jnp.where(mine, q2, jnp.zeros_like(q2))
                    do_f = jnp.where(mine, dor[:, lanes], 0.0)
                    dom = do_f.astype(BF16)
                    s = lax.dot_general(qm, k2, _DOT_DIMS["nt"], preferred_element_type=F32) * ATT_SCALE
                    s = s + bias_ref[h, :, bcols]
                    if gate is not None:
                        ok = jnp.logical_and(ok, gate)
                    pr = jnp.where(ok, jnp.exp(jnp.minimum(s - lr[h], 0.0)), 0.0)
                    dp = lax.dot_general(dom, v2, _DOT_DIMS["nt"], preferred_element_type=F32)
                    delta = jnp.sum(do_f * orf[:, lanes], axis=-1, keepdims=True)
                    dl = jnp.sum(dlr[h], axis=-1, keepdims=True)
                    ds = pr * (dp - delta + dl)
                    dsb = ds.astype(BF16)
                    dq_h = jnp.where(mine, jnp.dot(dsb, k2, preferred_element_type=F32), 0.0) * ATT_SCALE
                    if ti == 0:
                        dq_cur = dq_cur + dq_h
                    else:
                        dq_next = dq_next + dq_h
                    dk2 = dk2 + lax.dot_general(dsb, qm, _DOT_DIMS["tn"], preferred_element_type=F32) * ATT_SCALE
                    dv2 = dv2 + lax.dot_general(pr.astype(BF16), dom, _DOT_DIMS["tn"], preferred_element_type=F32)
                    dbias_ref[h, :, bcols] += ds
            dq_ref[:, lanes] = dq_cur
            carry_ref[:, lanes] = dq_next
            dk_ref[:, lanes] = dk2
            dv_ref[:, lanes] = dv2

    def nxt(n):
        return jnp.minimum(n + 1, nb - 1)

    blk = (BLOCK, DIL_WIDTH)
    hblk = (N_HEADS, BLOCK, LANES)
    qcol = lambda j: (lambda r, n: (r * nb + n, j))
    q_next = lambda r, n: (r * nb + nxt(n), 0)
    rown = lambda r, n: (r * nb + n, 0)
    rown_next = lambda r, n: (r * nb + nxt(n), 0)
    hrow = lambda r, n: (0, r * nb + n, 0)
    hrow_next = lambda r, n: (0, r * nb + nxt(n), 0)
    sds = jax.ShapeDtypeStruct((t_len, DIL_WIDTH), F32)
    return pl.pallas_call(
        body, name=f"dil_att_bwd_g{g}", grid=(dil, nb),
        in_specs=[pl.BlockSpec(blk, qcol(0)), pl.BlockSpec(blk, q_next),
                  pl.BlockSpec(blk, qcol(1)), pl.BlockSpec(blk, qcol(2)),
                  pl.BlockSpec(blk, rown), pl.BlockSpec(blk, rown_next),
                  pl.BlockSpec(blk, rown), pl.BlockSpec(blk, rown_next),
                  pl.BlockSpec(hblk, hrow), pl.BlockSpec(hblk, hrow_next),
                  pl.BlockSpec(hblk, hrow), pl.BlockSpec(hblk, hrow_next),
                  pl.BlockSpec(bias.shape, _const_map(3))],
        out_specs=[pl.BlockSpec(blk, rown)] * 3 + [pl.BlockSpec(bias.shape, _const_map(3))],
        out_shape=[sds, sds, sds, jax.ShapeDtypeStruct(bias.shape, F32)],
        scratch_shapes=[pltpu.VMEM((BLOCK, DIL_WIDTH), F32)],
        compiler_params=_cparams(("arbitrary", "arbitrary"), VMEM_LIMIT),
    )(p, p, p, p, do, do, o, o, lse, lse, dlse, dlse, bias)


def _att_all_groups(ps, biases):
    outs = [_att_fwd_call(ps[g], biases[g], g, dil) for g, (_, dil) in enumerate(DIL_PATTERNS)]
    return tuple(o for o, _ in outs), tuple(l for _, l in outs)


@jax.custom_vjp
def _dilated_attention(ps, biases):
    return _att_all_groups(ps, biases)


def _dilated_attention_fwd(ps, biases):
    os_, lses = _att_all_groups(ps, biases)
    return (os_, lses), (ps, biases, os_, lses)


def _dilated_attention_bwd(res, cts):
    ps, biases, os_, lses = res
    dos, dlses = cts
    dps, dbiases = [], []
    for g, (_, dil) in enumerate(DIL_PATTERNS):
        dq, dk, dv, dbias = _att_bwd_call(ps[g], biases[g], os_[g], lses[g], dos[g], dlses[g], g, dil)
        dps.append(jnp.concatenate([dq, dk, dv], axis=1))
        dbiases.append(dbias)
    return tuple(dps), tuple(dbiases)


_dilated_attention.defvjp(_dilated_attention_fwd, _dilated_attention_bwd)


def _me():
    return lax.axis_index("x"), lax.axis_index("y"), lax.axis_index("c")


def _flip(me, f):
    return tuple((1 - m) if b else m for m, b in zip(me, f))


def _chip_of(d):
    return 2 * d[0] + d[1]


def _dev_of(d):
    return 4 * d[0] + 2 * d[1] + d[2]


EXCHANGE_CHUNKS = 8
CHIP_FLIPS = ((1, 0, 0), (0, 1, 0), (1, 1, 0))
ALL_FLIPS = tuple((a, b, c) for a in (0, 1) for b in (0, 1) for c in (0, 1) if a or b or c)
CORE_FLIP = (0, 0, 1)


def _n_chunks(rows):
    return EXCHANGE_CHUNKS if rows % (EXCHANGE_CHUNKS * PACK_ROWS) == 0 else 1


def _exchange(src, n_slots, transfers, name):
    _, rows, cols = src.shape
    chunks = _n_chunks(rows)
    rc = rows // chunks
    n = len(transfers) * chunks

    def body(src_ref, dst_ref, send_sems, recv_sems):
        me = _me()
        copies = []
        for q in range(chunks):
            for kk, (f, src_slot, dst_slot) in enumerate(transfers):
                peer = _flip(me, f)
                cp = pltpu.make_async_remote_copy(
                    src_ref=src_ref.at[src_slot(me, peer), pl.ds(q * rc, rc)],
                    dst_ref=dst_ref.at[dst_slot(me, peer), pl.ds(q * rc, rc)],
                    send_sem=send_sems.at[kk * chunks + q], recv_sem=recv_sems.at[kk * chunks + q],
                    device_id=peer, device_id_type=MESH)
                cp.start()
                copies.append(cp)
        for cp in copies:
            cp.wait()

    return pl.pallas_call(
        body, name=name,
        out_shape=jax.ShapeDtypeStruct((n_slots, rows, cols), src.dtype),
        in_specs=[pl.BlockSpec(memory_space=pl.ANY)],
        out_specs=pl.BlockSpec(memory_space=pl.ANY),
        scratch_shapes=[pltpu.SemaphoreType.DMA((n,)), pltpu.SemaphoreType.DMA((n,))],
    )(src)


def _set_slot(buf, block, index):
    return lax.dynamic_update_slice_in_dim(buf, block[None].astype(buf.dtype), index, axis=0)


def _chip_all_gather(src, name):
    got = _exchange(src[None], N_CHIPS, [(f, lambda me, peer: 0, lambda me, peer: _chip_of(me)) for f in CHIP_FLIPS], name)
    return _set_slot(got, src, _chip_of(_me()))


def _dev_all_gather(src, name):
    got = _exchange(src[None], N_DEV, [(f, lambda me, peer: 0, lambda me, peer: _dev_of(me)) for f in ALL_FLIPS], name)
    return _set_slot(got, src, _dev_of(_me()))


def _chip_scatter(src, name):
    got = _exchange(src, N_CHIPS, [(f, lambda me, peer: _chip_of(peer), lambda me, peer: _chip_of(me))
                                   for f in CHIP_FLIPS], name)
    chip = _chip_of(_me())
    return _set_slot(got, lax.dynamic_index_in_dim(src, chip, 0, keepdims=False), chip)


def _core_halves(src, name):
    s, _, half, cols = src.shape
    transfers = [(CORE_FLIP, (lambda me, peer, j=j: 2 * j + peer[2]), (lambda me, peer, j=j: j)) for j in range(s)]
    return _exchange(src.reshape(2 * s, half, cols), s, transfers, name)


def _core_all_gather(src, name):
    got = _exchange(src[None], 2, [(CORE_FLIP, lambda me, peer: 0, lambda me, peer: me[2])], name)
    return _set_slot(got, src, _me()[2])


def _two_level_gather(src, name):
    rows, cols = src.shape
    half = rows // 2
    chunks = _n_chunks(half)
    rc = half // chunks
    n = len(CHIP_FLIPS) * chunks

    def body(src_ref, g_ref, send1, recv1, send2, recv2):
        me = _me()
        c = me[2]
        sibling = _flip(me, CORE_FLIP)
        first, second = [], []
        for q in range(chunks):
            for kk, f in enumerate(CHIP_FLIPS):
                peer = _flip(me, f)
                cp = pltpu.make_async_remote_copy(
                    src_ref=src_ref.at[c, pl.ds(q * rc, rc)], dst_ref=g_ref.at[_chip_of(me), c, pl.ds(q * rc, rc)],
                    send_sem=send1.at[kk * chunks + q], recv_sem=recv1.at[kk * chunks + q],
                    device_id=peer, device_id_type=MESH)
                cp.start()
                first.append((cp, _chip_of(peer), kk * chunks + q, q))
        for cp, origin, idx, q in first:
            cp.wait_recv()
            fw = pltpu.make_async_remote_copy(
                src_ref=g_ref.at[origin, c, pl.ds(q * rc, rc)], dst_ref=g_ref.at[origin, c, pl.ds(q * rc, rc)],
                send_sem=send2.at[idx], recv_sem=recv2.at[idx],
                device_id=sibling, device_id_type=MESH)
            fw.start()
            second.append(fw)
        for fw in second:
            fw.wait_recv()
        for cp, _, _, _ in first:
            cp.wait_send()
        for fw in second:
            fw.wait_send()

    got = pl.pallas_call(
        body, name=name,
        out_shape=jax.ShapeDtypeStruct((N_CHIPS, 2, half, cols), src.dtype),
        in_specs=[pl.BlockSpec(memory_space=pl.ANY)],
        out_specs=pl.BlockSpec(memory_space=pl.ANY),
        scratch_shapes=[pltpu.SemaphoreType.DMA((n,)) for _ in range(4)],
    )(src.reshape(2, half, cols))
    return _set_slot(got.reshape(N_CHIPS, rows, cols), src, _chip_of(_me()))


def _sum_slots(x, name):
    s, rows, cols = x.shape
    tile = _pick(rows, (512, 256, 128, 64, 32, 16, 8))

    def body(x_ref, o_ref):
        acc = x_ref[0].astype(F32)
        for i in range(1, s):
            acc = acc + x_ref[i].astype(F32)
        o_ref[...] = acc

    return pl.pallas_call(
        body, name=name, grid=(rows // tile,),
        in_specs=[pl.BlockSpec((s, tile, cols), lambda i: (0, i, 0))],
        out_specs=pl.BlockSpec((tile, cols), lambda i: (i, 0)),
        out_shape=jax.ShapeDtypeStruct((rows, cols), F32),
        compiler_params=_cparams(("parallel",), VMEM_LIMIT),
    )(x)


def _sum_slots_own(recv, send, chip, name):
    s, rows, cols = recv.shape
    tile = _pick(rows, (512, 256, 128, 64, 32, 16, 8))

    def body(chip_ref, recv_ref, own_ref, o_ref):
        acc = None
        for j in range(s):
            term = jnp.where(chip_ref[0] == j, own_ref[0], recv_ref[j]).astype(F32)
            acc = term if acc is None else acc + term
        o_ref[...] = acc

    grid_spec = pltpu.PrefetchScalarGridSpec(
        num_scalar_prefetch=1, grid=(rows // tile,),
        in_specs=[pl.BlockSpec((s, tile, cols), lambda i, c: (0, i, 0)),
                  pl.BlockSpec((1, tile, cols), lambda i, c: (c[0], i, 0))],
        out_specs=pl.BlockSpec((tile, cols), lambda i, c: (i, 0)))
    return pl.pallas_call(
        body, name=name, grid_spec=grid_spec,
        out_shape=jax.ShapeDtypeStruct((rows, cols), F32),
        compiler_params=_cparams(("arbitrary",), VMEM_LIMIT),
    )(jnp.reshape(chip, (1,)).astype(jnp.int32), recv, send)


def _add_pairs(a, b, name):
    s, rows, cols = a.shape
    tile = _pick(rows, (512, 256, 128, 64, 32, 16, 8))

    def body(a_ref, b_ref, o_ref):
        o_ref[...] = (a_ref[...].astype(F32) + b_ref[...].astype(F32)).astype(o_ref.dtype)

    spec = pl.BlockSpec((1, tile, cols), lambda j, i: (j, i, 0))
    return pl.pallas_call(
        body, name=name, grid=(s, rows // tile),
        in_specs=[spec, spec], out_specs=spec,
        out_shape=jax.ShapeDtypeStruct(a.shape, a.dtype),
        compiler_params=_cparams(("parallel", "parallel"), VMEM_LIMIT),
    )(a, b)


def _adamw(w, g, m, v, name):
    rows, cols = w.shape
    tile = rows
    if rows * cols * 4 > 2 * 1024 * 1024:
        tile = _pick(rows, (256, 128, 64, 32, 16, 8))
    c1 = 1.0 / (1.0 - ADAM_B1 ** ADAM_STEP)
    c2 = 1.0 / (1.0 - ADAM_B2 ** ADAM_STEP)

    def body(w_ref, g_ref, m_ref, v_ref, d_ref, nm_ref, nv_ref):
        gv = g_ref[...]
        nm = ADAM_B1 * m_ref[...] + (1.0 - ADAM_B1) * gv
        nv = ADAM_B2 * v_ref[...] + (1.0 - ADAM_B2) * (gv * gv)
        m_hat = nm * c1
        v_hat = nv * c2
        d_ref[...] = -ADAM_LR * (m_hat / (jnp.sqrt(v_hat) + ADAM_EPS) + ADAM_WD * w_ref[...])
        nm_ref[...] = nm
        nv_ref[...] = nv

    spec = pl.BlockSpec((tile, cols), lambda i: (i, 0))
    sds = jax.ShapeDtypeStruct((rows, cols), F32)
    return pl.pallas_call(
        body, name=name, grid=(rows // tile,),
        in_specs=[spec] * 4, out_specs=[spec] * 3, out_shape=[sds] * 3,
        compiler_params=_cparams(("parallel",), VMEM_LIMIT),
    )(w, g, m, v)


def _ada_fwd(c_all, ada_w, ada_b_cols):
    n_col = ada_w.shape[2]

    def body(c_ref, w_ref, b_ref, o_ref):
        cv = c_ref[...]
        cond = (cv * jax.nn.sigmoid(cv)).astype(BF16)
        o_ref[0] = jnp.dot(cond, w_ref[0].astype(BF16), preferred_element_type=F32) + b_ref[0]

    return pl.pallas_call(
        body, name="ada_fwd", grid=(DEPTH,),
        in_specs=[pl.BlockSpec(c_all.shape, lambda i: (0, 0)),
                  pl.BlockSpec((1, D_MODEL, n_col), lambda i: (i, 0, 0)),
                  pl.BlockSpec((1, 1, n_col), lambda i: (i, 0, 0))],
        out_specs=pl.BlockSpec((1, N_DEV, n_col), lambda i: (i, 0, 0)),
        out_shape=jax.ShapeDtypeStruct((DEPTH, N_DEV, n_col), F32),
        compiler_params=_cparams(("parallel",), VMEM_LIMIT),
    )(c_all, ada_w, ada_b_cols)


def _ada_grad(c_all_t, dmod_cols):
    n_col = dmod_cols.shape[2]

    def body(c_ref, d_ref, o_ref):
        cv = c_ref[...]
        cond = cv * jax.nn.sigmoid(cv)
        o_ref[0] = jnp.dot(cond, d_ref[0], precision=lax.Precision.HIGHEST, preferred_element_type=F32)

    return pl.pallas_call(
        body, name="ada_grad", grid=(DEPTH,),
        in_specs=[pl.BlockSpec(c_all_t.shape, lambda i: (0, 0)),
                  pl.BlockSpec((1, LANES, n_col), lambda i: (i, 0, 0))],
        out_specs=pl.BlockSpec((1, D_MODEL, n_col), lambda i: (i, 0, 0)),
        out_shape=jax.ShapeDtypeStruct((DEPTH, D_MODEL, n_col), F32),
        compiler_params=_cparams(("parallel",), VMEM_LIMIT),
    )(c_all_t, dmod_cols)


ROW_TILE = 256


def _shift_rows(a, n=1):
    return jnp.pad(a, ((n, 0), (0, 0)))[:-n]


def _modulate(x, sc, sh, name):
    def fn(x, sc, sh):
        return (x * (1.0 + sc) + sh,)

    t = x.shape[0]
    return _fused(fn, [_tiled(x, ROW_TILE), _shared(sc), _shared(sh)],
                  [_tiled_out(t, D_MODEL, ROW_TILE)], (t // ROW_TILE,), name)[0]


def _resid_ln_mod(x, y, gate, ln_g, ln_b, sc, sh, name):
    def fn(x, y, gate, ln_g, ln_b, sc, sh):
        x1 = _layer_norm_rows(ALPHA * x + (1.0 + gate) * y, ln_g, ln_b)
        return x1, x1 * (1.0 + sc) + sh

    t = x.shape[0]
    return _fused(fn, [_tiled(x, ROW_TILE), _tiled(y, ROW_TILE)] + [_shared(a) for a in (gate, ln_g, ln_b, sc, sh)],
                  [_tiled_out(t, D_MODEL, ROW_TILE)] * 2, (t // ROW_TILE,), name)


def _resid_ln_loss(x, y, gate, ln_g, ln_b, target, name):
    def fn(x, y, gate, ln_g, ln_b, target):
        x1 = _layer_norm_rows(ALPHA * x + (1.0 + gate) * y, ln_g, ln_b)
        err = jnp.square(x1 - target)
        per_row = jnp.mean(err, axis=-1, keepdims=True)
        return (0.5 * jnp.sum(per_row, axis=0, keepdims=True),)

    t = x.shape[0]
    return _fused(fn, [_tiled(x, ROW_TILE), _tiled(y, ROW_TILE)] + [_shared(a) for a in (gate, ln_g, ln_b)]
                  + [_tiled(target, ROW_TILE)],
                  [((1, 1), (1, 1), _const_map(2), "a")], (t // ROW_TILE,), name)[0]


def _mlp(u, w1, s1, w2, s2, name):
    def run(u, w1, w2):
        h, act = _mm(u, w1, "nn", name + "_w1_fwd", square_relu=True)
        return _mm(act, w2, "nn", name + "_w2_fwd"), (u, w1, w2, h, act)

    @jax.custom_vjp
    def op(u, w1, s1, w2, s2):
        return run(u, w1, w2)[0]

    def fwd(u, w1, s1, w2, s2):
        return run(u, w1, w2)

    def bwd(res, dy):
        u, w1, w2, h, act = res
        dh = _mm(dy, w2, "nt", name + "_w2_dx", pre_act=h)
        dw2 = _mm(act, dy, "tn", name + "_w2_dw")
        du = _mm(dh, w1, "nt", name + "_w1_dx")
        dw1 = _mm(u, dh, "tn", name + "_w1_dw")
        return du, jnp.zeros_like(w1), dw1, jnp.zeros_like(w2), dw2

    op.defvjp(fwd, bwd)
    return op(u, w1, s1, w2, s2)


AB_PIECES = (("r", 0, 512, 512), ("k", 512, 512, 512), ("v", 1024, 512, 512),
             ("wd", 1536, 64, 128), ("ad", 1600, 64, 128), ("gd", 1664, 160, 256),
             ("h", 1824, 512, 512), ("bg", 2336, 512, 512), ("cg", 2848, 512, 512))
AB_PAD_COLS = sum(p[3] for p in AB_PIECES)


def _regroup_cols(w):
    parts = []
    for _, start, width, padded in AB_PIECES:
        piece = w[..., start:start + width]
        if padded != width:
            piece = jnp.pad(piece, [(0, 0)] * (w.ndim - 1) + [(0, padded - width)])
        parts.append(piece)
    return jnp.concatenate(parts, axis=-1)


def _pad_rows(w, rows):
    return jnp.pad(w, ((0, rows - w.shape[0]), (0, 0)))


def _rwkv_shortconv(u, big, wts, wsrc, gslot):
    t = u.shape[0]
    p = _linear(u, _regroup_cols(big["ab_w_in"][0]), _regroup_cols(wts["ab_w_in"][0]), "ab_in")
    mu = _regroup_cols(jnp.pad(wts["rw_mu"], ((0, 0), (0, AB_PROJ - RW_PROJ))))
    w_up = _pad_rows(wts["rw_w_up"][0], 128)
    a_up = _pad_rows(wts["rw_a_up"][0], 128)
    g_up = _pad_rows(wts["rw_g_up"][0], 256)

    def pre(rp, rs, kp, ks, vp, vs, wdp, wds, adp, ads, gdp, gds, h, cg,
            mu_r, mu_k, mu_v, mu_w, mu_a, mu_g, w0, w_up, a0, a_up, g_up, k_k, k_a):
        def mix(pv, sv, m):
            return pv + m * (sv - pv)

        r, k, v = mix(rp, rs, mu_r), mix(kp, ks, mu_k), mix(vp, vs, mu_v)
        wd, ad, gd = mix(wdp, wds, mu_w), mix(adp, ads, mu_a), mix(gdp, gds, mu_g)
        logw = -_softplus(-(w0 + _bdot(jnp.tanh(wd), w_up))) - 0.5
        decay = jnp.exp(-jnp.exp(logw))
        iclr = jax.nn.sigmoid(a0 + _bdot(ad, a_up))
        gate = _bdot(jax.nn.sigmoid(gd), g_up)
        kk = k * k_k
        kk = kk / jnp.maximum(jnp.sqrt(_head_sum(kk * kk)), 1e-12)
        k_h = k * (1.0 + (iclr - 1.0) * k_a)
        return r, decay, k_h, v, -kk, kk * iclr, gate, cg * h

    tile = ROW_TILE
    names = [q[0] for q in AB_PIECES]
    cuts = list(np.cumsum([q[3] for q in AB_PIECES])[:-1])
    pp = dict(zip(names, jnp.split(p, cuts, axis=1)))
    mp = dict(zip(names, jnp.split(mu, cuts, axis=1)))

    ins = []
    for name in ("r", "k", "v", "wd", "ad", "gd"):
        ins += [_tiled(pp[name], tile), _tiled(_shift_rows(pp[name]), tile)]
    ins += [_tiled(pp["h"], tile), _tiled(pp["cg"], tile)]
    ins += [_shared(mp[name]) for name in ("r", "k", "v", "wd", "ad", "gd")]
    ins += [_shared(a) for a in (wts["rw_w0"], w_up, wts["rw_a0"], a_up, g_up, wts["rw_k_k"], wts["rw_k_a"])]
    outs = [_tiled_out(t, RW_WIDTH, tile)] * 8
    r, decay, k_h, v, a, b, gate, z = _fused(pre, ins, outs, (t // tile,), "rwkv_pre")

    y, gathered, token = _rwkv_scan(r, decay, k_h, v, a, b, wsrc, gslot)

    conv_w = wts["sc_conv_w"][0]
    r_k = wts["rw_r_k"].reshape(1, RW_WIDTH)

    def post(y, r, k_h, v, gate, bg, z, z1, z2, lnx_g, lnx_b, r_k, c0, c1, c2):
        mean = _head_sum(y) * (1.0 / HEAD_DIM)
        yc = y - mean
        var = _head_sum(yc * yc) * (1.0 / HEAD_DIM)
        yn = yc * lax.rsqrt(var + RW_GN_EPS) * lnx_g + lnx_b
        bonus = _head_sum(r * k_h * r_k) * v
        return (yn + bonus) * gate, bg * (c0 * z2 + c1 * z1 + c2 * z)

    ins = [_tiled(a_, tile) for a_ in (y, r, k_h, v, gate, pp["bg"])]
    ins += [_tiled(a_, tile) for a_ in (z, _shift_rows(z, 1), _shift_rows(z, 2))]
    ins += [_shared(a_) for a_ in (wts["rw_lnx_g"], wts["rw_lnx_b"], r_k, conv_w[0:1], conv_w[1:2], conv_w[2:3])]
    y_a, y_b = _fused(post, ins, [_tiled_out(t, RW_WIDTH, tile)] * 2, (t // tile,), "rwkv_post")
    out = _linear(jnp.concatenate([y_a, y_b], axis=1), big["ab_w_out"][0], wts["ab_w_out"][0], "ab_out")
    return out, gathered, token


def _t5_bucket_np(dist):
    exact = N_BUCKETS // 2
    logd = np.log(np.maximum(dist, 1).astype(np.float32) / exact) / math.log(MAX_DISTANCE / exact)
    large = np.minimum(exact + (logd * (N_BUCKETS - exact)).astype(np.int32), N_BUCKETS - 1)
    return np.where(dist < exact, dist, large)


def _merge_groups(os_, lses, name):
    t = os_[0].shape[0]
    tile = ROW_TILE

    def fn(o0, o1, o2, l0, l1, l2):
        lane = lax.broadcasted_iota(jnp.int32, (1, LANES), 1)
        lo = lane < HEAD_DIM
        ls = [jnp.where(lo, l[0], l[1]) for l in (l0, l1, l2)]
        m = jnp.maximum(jnp.maximum(ls[0], ls[1]), ls[2])
        es = [jnp.exp(l - m) for l in ls]
        den = es[0] + es[1] + es[2]
        return ((es[0] * o0 + es[1] * o1 + es[2] * o2) / den,)

    ins = [(o, (tile, LANES), lambda i, hp: (i, hp), "t") for o in os_]
    ins += [(l, (2, tile, LANES), lambda i, hp: (hp, i, 0), "t") for l in lses]
    outs = [((t, DIL_WIDTH), (tile, LANES), lambda i, hp: (i, hp), "t")]
    return _fused(fn, ins, outs, (t // tile, N_PAIRS), name)[0]


def _residue_major(a, dil, axis=0):
    if dil == 1:
        return a
    shp = a.shape
    split = a.reshape(shp[:axis] + (shp[axis] // dil, dil) + shp[axis + 1:])
    return jnp.swapaxes(split, axis, axis + 1).reshape(shp)


def _position_major(a, dil, axis=0):
    if dil == 1:
        return a
    shp = a.shape
    split = a.reshape(shp[:axis] + (dil, shp[axis] // dil) + shp[axis + 1:])
    return jnp.swapaxes(split, axis, axis + 1).reshape(shp)


def _dilated_mixer(u, big, wts):
    group_cols = 3 * DIL_WIDTH
    ps = []
    for g, (_, dil) in enumerate(DIL_PATTERNS):
        cols = slice(g * group_cols, (g + 1) * group_cols)
        ps.append(_linear(_residue_major(u, dil), big["dil_w_qkv"][0][:, cols], wts["dil_w_qkv"][0][:, cols],
                          f"dil_qkv{g}"))
    qi = np.arange(BLOCK)[:, None]
    ki = np.arange(2 * BLOCK)[None, :]
    rel = BLOCK + qi - ki
    biases = []
    for g, (window, dil) in enumerate(DIL_PATTERNS):
        span = window // dil
        bucket = _t5_bucket_np(np.clip(rel, 0, span) * dil).reshape(-1)
        onehot = jnp.asarray(np.eye(N_BUCKETS, dtype=np.float32)[bucket])
        table = wts["rel_bias"][:, g * N_HEADS:(g + 1) * N_HEADS]
        bias = jnp.dot(onehot, table, precision=lax.Precision.HIGHEST)
        biases.append(jnp.transpose(bias.reshape(BLOCK, 2 * BLOCK, N_HEADS), (2, 0, 1)))
    os_, lses = _dilated_attention(tuple(ps), tuple(biases))
    os_ = [_position_major(o, dil) for o, (_, dil) in zip(os_, DIL_PATTERNS)]
    lses = [_position_major(l, dil, axis=1) for l, (_, dil) in zip(lses, DIL_PATTERNS)]
    o = _merge_groups(os_, lses, "dil_merge")
    return _linear(o, big["dil_w_out"][0], wts["dil_w_out"][0], "dil_out")


def _forward_local(x, mods, big, wts, wsrc, gslot, late_shapes, target):
    u = _modulate(x, mods[0, 1], mods[0, 0], "mod_in")
    for i in range(DEPTH):
        sh2, sc2, g1, g2 = mods[i, 3], mods[i, 4], mods[i, 2], mods[i, 5]
        if i == 0:
            y, gathered, token = _rwkv_shortconv(u, big, wts, wsrc, gslot)
            shard_shapes = [s[:SHARDED[n]] + (s[SHARDED[n]] // N_CHIPS,) + s[SHARDED[n] + 1:]
                            for n, s in late_shapes.items()]
            parts = _unpack_chips(gathered, shard_shapes, own=wsrc)
            big = {**big, **{n: _join_chips(p, SHARDED[n]) for n, p in zip(late_shapes, parts)}}
            wts = {**wts, **_gradient_slots(token, late_shapes, tuple(late_shapes))}
        else:
            y = _dilated_mixer(u, big, wts)
        x, u = _resid_ln_mod(x, y, g1, wts["ln_g"][i, 0:1], wts["ln_b"][i, 0:1], sc2, sh2, f"ln_mix{i}")
        y = _mlp(u, big["mlp_w1"][i], wts["mlp_w1"][i], big["mlp_w2"][i], wts["mlp_w2"][i], f"mlp{i}")
        if i + 1 < DEPTH:
            x, u = _resid_ln_mod(x, y, g2, wts["ln_g"][i, 1:2], wts["ln_b"][i, 1:2],
                                 mods[i + 1, 1], mods[i + 1, 0], f"ln_mlp{i}")
        else:
            return _resid_ln_loss(x, y, g2, wts["ln_g"][i, 1:2], wts["ln_b"][i, 1:2], target, "ln_loss")


SHARDED = {"ab_w_in": 2, "ab_w_out": 1, "dil_w_qkv": 2, "dil_w_out": 2, "mlp_w1": 2, "mlp_w2": 1,
           "ln_g": 2, "ln_b": 2, "rw_w_up": 2, "rw_a_up": 2, "rw_g_up": 2, "sc_conv_w": 2}
FIRST_MIXER = ("ab_w_in", "ab_w_out")
LATER_LAYERS = ("dil_w_qkv", "dil_w_out", "mlp_w1", "mlp_w2")
SMALL_SHARDED = ("ln_g", "ln_b", "rw_w_up", "rw_a_up", "rw_g_up", "sc_conv_w")
REPLICATED = ("ada_b", "rw_mu", "rw_w0", "rw_a0", "rw_k_k", "rw_k_a", "rw_r_k", "rw_lnx_g", "rw_lnx_b", "rel_bias")
WEIGHT_ORDER = ("ada_w", "ada_b", "ln_g", "ln_b", "ab_w_in", "rw_mu", "rw_w0", "rw_w_up", "rw_a0", "rw_a_up",
                "rw_g_up", "rw_k_k", "rw_k_a", "rw_r_k", "rw_lnx_g", "rw_lnx_b", "sc_conv_w", "ab_w_out",
                "dil_w_qkv", "dil_w_out", "rel_bias", "mlp_w1", "mlp_w2")


PACK_ROWS = 16


def _rows_of(n_elems):
    return -(-n_elems // (ROW_W * PACK_ROWS)) * PACK_ROWS


def _to_rows(a):
    flat = a.reshape(-1)
    rows = _rows_of(flat.shape[0])
    if rows * ROW_W != flat.shape[0]:
        flat = jnp.pad(flat, (0, rows * ROW_W - flat.shape[0]))
    return flat.reshape(rows, ROW_W)


def _from_rows(rows, shape):
    n = int(np.prod(shape))
    return rows.reshape(-1)[:n].reshape(shape)


def _split_chips(full, axis):
    shp = full.shape
    parts = full.reshape(shp[:axis] + (N_CHIPS, shp[axis] // N_CHIPS) + shp[axis + 1:])
    return jnp.moveaxis(parts, axis, 0)


def _join_chips(parts, axis):
    moved = jnp.moveaxis(parts, 0, axis)
    shp = moved.shape
    return moved.reshape(shp[:axis] + (shp[axis] * shp[axis + 1],) + shp[axis + 2:])


def _pack_rows(arrays, row_multiple=256):
    blocks = [_to_rows(a) for a in arrays]
    total = sum(b.shape[0] for b in blocks)
    pad = (-total) % row_multiple
    if pad:
        blocks.append(jnp.zeros((pad, ROW_W), blocks[0].dtype))
    return jnp.concatenate(blocks, axis=0)


def _unpack_rows(buf, shapes):
    out, r0 = [], 0
    for shp in shapes:
        n = _rows_of(int(np.prod(shp)))
        out.append(_from_rows(buf[r0:r0 + n], shp))
        r0 += n
    return out


def _pack_chips(parts):
    blocks = []
    for p in parts:
        flat = p.reshape(N_CHIPS, -1)
        rows = _rows_of(flat.shape[1])
        if rows * ROW_W != flat.shape[1]:
            flat = jnp.pad(flat, ((0, 0), (0, rows * ROW_W - flat.shape[1])))
        blocks.append(flat.reshape(N_CHIPS, rows, ROW_W))
    total = sum(b.shape[1] for b in blocks)
    pad = (-total) % 256
    if pad:
        blocks.append(jnp.zeros((N_CHIPS, pad, ROW_W), blocks[0].dtype))
    return jnp.concatenate(blocks, axis=1)


def _unpack_chips(buf, shapes, own=None):
    mine = lax.broadcasted_iota(jnp.int32, (N_CHIPS, 1, 1), 0) == _chip_of(_me()) if own is not None else None
    out, r0 = [], 0
    for shp in shapes:
        size = int(np.prod(shp))
        n = _rows_of(size)
        rows = buf[:, r0:r0 + n]
        if own is not None:
            rows = jnp.where(mine, own[None, r0:r0 + n], rows)
        out.append(rows.reshape(N_CHIPS, -1)[:, :size].reshape((N_CHIPS,) + tuple(shp)))
        r0 += n
    return out


def _gradient_slots(token, full_shapes, names):
    @jax.custom_vjp
    def route(token):
        return {n: jnp.zeros(full_shapes[n], F32) for n in names}

    def fwd(token):
        return {n: jnp.zeros(full_shapes[n], F32) for n in names}, None

    def bwd(_, d):
        return (_pack_chips([_split_chips(d[n], SHARDED[n]).astype(BF16) for n in names]),)

    route.defvjp(fwd, bwd)
    return route(token)


def _as2d(a):
    return a.reshape(-1, a.shape[-1])


def kernel(x, c, ada_w, ada_b, ln_g, ln_b, ab_w_in, rw_mu, rw_w0, rw_w_up, rw_a0, rw_a_up, rw_g_up, rw_k_k, rw_k_a, rw_r_k, rw_lnx_g, rw_lnx_b, sc_conv_w, ab_w_out, dil_w_qkv, dil_w_out, rel_bias, mlp_w1, mlp_w2, loss_target, m_ada_w, m_ada_b, m_ln_g, m_ln_b, m_ab_w_in, m_rw_mu, m_rw_w0, m_rw_w_up, m_rw_a0, m_rw_a_up, m_rw_g_up, m_rw_k_k, m_rw_k_a, m_rw_r_k, m_rw_lnx_g, m_rw_lnx_b, m_sc_conv_w, m_ab_w_out, m_dil_w_qkv, m_dil_w_out, m_rel_bias, m_mlp_w1, m_mlp_w2, v_ada_w, v_ada_b, v_ln_g, v_ln_b, v_ab_w_in, v_rw_mu, v_rw_w0, v_rw_w_up, v_rw_a0, v_rw_a_up, v_rw_g_up, v_rw_k_k, v_rw_k_a, v_rw_r_k, v_rw_lnx_g, v_rw_lnx_b, v_sc_conv_w, v_ab_w_out, v_dil_w_qkv, v_dil_w_out, v_rel_bias, v_mlp_w1, v_mlp_w2):
    args = dict(locals())
    w_in = {n: args[n] for n in WEIGHT_ORDER}
    m_in = {n: args["m_" + n] for n in WEIGHT_ORDER}
    v_in = {n: args["v_" + n] for n in WEIGHT_ORDER}
    me = _me()
    chip = _chip_of(me)
    dev = _dev_of(me)

    c_all = _dev_all_gather(c, "gather_c")[:, 0, :]
    n_col = ada_w.shape[2]
    ada_b_cols = lax.dynamic_slice_in_dim(ada_b, chip * n_col, n_col, axis=1)[:, None, :]
    mod_cols = _ada_fwd(c_all, ada_w, ada_b_cols)

    first_buf = _pack_rows([w_in[n].astype(BF16) for n in FIRST_MIXER])
    first_all = _two_level_gather(first_buf, "gather_first")
    late_buf = _pack_rows([w_in[n].astype(BF16) for n in LATER_LAYERS])
    small_buf = _pack_rows([mod_cols] + [w_in[n] for n in SMALL_SHARDED], row_multiple=PACK_ROWS)
    small_all = _chip_all_gather(small_buf, "gather_small")

    def full_shape(n):
        shp = w_in[n].shape
        return shp[:SHARDED[n]] + (shp[SHARDED[n]] * N_CHIPS,) + shp[SHARDED[n] + 1:]

    wts = {n: w_in[n] for n in REPLICATED}
    big = {}
    for n, part in zip(FIRST_MIXER, _unpack_chips(first_all, [w_in[n].shape for n in FIRST_MIXER])):
        big[n] = _join_chips(part, SHARDED[n])
        wts[n] = jnp.zeros(full_shape(n), F32)
    small_parts = _unpack_chips(small_all, [mod_cols.shape] + [w_in[n].shape for n in SMALL_SHARDED])
    for n, part in zip(SMALL_SHARDED, small_parts[1:]):
        wts[n] = _join_chips(part, SHARDED[n])
    mod_all = _join_chips(small_parts[0], 2)
    mods = lax.dynamic_slice_in_dim(mod_all, dev, 1, axis=1).reshape(DEPTH, 6, 1, D_MODEL)
    late_shapes = {n: full_shape(n) for n in LATER_LAYERS}
    late_slot = jnp.zeros(late_buf.shape, F32)

    def local_loss(xv, modv, wv, slot):
        return _forward_local(xv, modv, big, wv, late_buf, slot, late_shapes, loss_target[0])[0, 0]

    loss_local, (grad_x, dmods, dw, late_part) = jax.value_and_grad(local_loss, argnums=(0, 1, 2, 3))(
        x[0], mods, wts, late_slot)
    loss = lax.psum(loss_local, ("x", "y", "c"))

    small_row = jnp.concatenate([dmods.reshape(-1)] + [dw[n].reshape(-1) for n in REPLICATED[1:]])
    n_small = small_row.shape[0]
    n_small_pad = -(-n_small // LANES) * LANES
    small_row = jnp.pad(small_row, (0, n_small_pad - n_small))[None, :]
    rows_all = _dev_all_gather(small_row, "gather_small_grads")
    small_sum = _sum_slots(rows_all, "sum_small_grads")

    dmod_all = rows_all[:, 0, :DEPTH * 6 * D_MODEL].reshape(N_DEV, DEPTH, 6 * D_MODEL)
    dmod_cols = lax.dynamic_slice_in_dim(dmod_all, chip * n_col, n_col, axis=2)
    dmod_cols = jnp.pad(jnp.moveaxis(dmod_cols, 0, 1), ((0, 0), (0, LANES - N_DEV), (0, 0)))
    c_all_t = jnp.pad(c_all.T, ((0, 0), (0, LANES - N_DEV)))
    grads = {"ada_w": _ada_grad(c_all_t, dmod_cols)}
    grads["ada_b"] = small_sum[0, :DEPTH * 6 * D_MODEL].reshape(ada_b.shape)
    r0 = DEPTH * 6 * D_MODEL
    for n in REPLICATED[1:]:
        size = int(np.prod(w_in[n].shape))
        grads[n] = small_sum[0, r0:r0 + size].reshape(w_in[n].shape)
        r0 += size

    sharded_names = FIRST_MIXER + SMALL_SHARDED
    send = _pack_chips([_split_chips(dw[n], SHARDED[n]).astype(BF16) for n in sharded_names])
    n_rows = send.shape[1]
    send = send.reshape(N_CHIPS, 2, n_rows // 2, ROW_W)
    theirs = _core_halves(send, "swap_halves")
    mine = lax.dynamic_index_in_dim(send, me[2], 1, keepdims=False)
    chip_part = _add_pairs(mine, theirs, "sum_cores")
    recv = _chip_scatter(chip_part, "scatter_grads")
    half_sum = _sum_slots(recv, "sum_chips")
    g_rows = _core_all_gather(half_sum, "gather_halves").reshape(n_rows, ROW_W)
    for n, g in zip(sharded_names, _unpack_rows(g_rows, [w_in[n].shape for n in sharded_names])):
        grads[n] = g

    late_rows = _sum_slots(_core_all_gather(late_part, "swap_late"), "sum_cores_late")
    for n, g in zip(LATER_LAYERS, _unpack_rows(late_rows, [w_in[n].shape for n in LATER_LAYERS])):
        grads[n] = g

    deltas, new_m, new_v = {}, {}, {}
    for n in WEIGHT_ORDER:
        shp = w_in[n].shape
        d, nm, nv = _adamw(_as2d(w_in[n]), _as2d(grads[n]), _as2d(m_in[n]), _as2d(v_in[n]), "adamw_" + n)
        deltas[n], new_m[n], new_v[n] = d.reshape(shp), nm.reshape(shp), nv.reshape(shp)

    return (loss, grad_x[None], *[grads[n] for n in WEIGHT_ORDER], *[deltas[n] for n in WEIGHT_ORDER],
            *[new_m[n] for n in WEIGHT_ORDER], *[new_v[n] for n in WEIGHT_ORDER])
```

```python
import functools
import math

import numpy as np
import jax
import jax.numpy as jnp
from jax import lax
from jax.experimental import pallas as pl
from jax.experimental.pallas import tpu as pltpu

F32 = jnp.float32
BF16 = jnp.bfloat16
MESH = pl.DeviceIdType.MESH

D_MODEL = 1024
DEPTH = 2
RW_WIDTH = 512
HEAD_DIM = 64
N_HEADS = 8
RW_DECAY_RANK = 64
RW_ICLR_RANK = 64
RW_GATE_RANK = 160
RW_GN_EPS = 64e-5
RW_PROJ = 3 * RW_WIDTH + RW_DECAY_RANK + RW_ICLR_RANK + RW_GATE_RANK
SC_WIDTH = 512
AB_PROJ = RW_PROJ + 3 * SC_WIDTH
DIL_PATTERNS = ((128, 1), (512, 4), (2048, 16))
N_GROUPS = 3
DIL_WIDTH = 512
DIL_PROJ = N_GROUPS * 3 * DIL_WIDTH
BLOCK = 128
N_BUCKETS = 32
MAX_DISTANCE = 2048
D_FF = 4 * D_MODEL
ALPHA = (2 * DEPTH) ** 0.25
LN_EPS = 1e-5
ADAM_LR = 0.001
ADAM_B1 = 0.9
ADAM_B2 = 0.999
ADAM_EPS = 1e-08
ADAM_WD = 0.01
ADAM_STEP = 10

N_CHIPS = 4
N_DEV = 8
LANES = 128
ROW_W = 1024
SCAN_CHUNK = 16
VMEM_LIMIT = 48 * 1024 * 1024
NEG_BIG = -1e30


def _pick(n, cands):
    for c in cands:
        if n % c == 0:
            return c
    return n


def _cparams(sem=None, vmem=None):
    return pltpu.CompilerParams(dimension_semantics=sem, vmem_limit_bytes=vmem)


_DOT_DIMS = {
    "nn": (((1,), (0,)), ((), ())),
    "nt": (((1,), (1,)), ((), ())),
    "tn": (((0,), (0,)), ((), ())),
}


def _mm(a, b, mode, name, square_relu=False, pre_act=None):
    if mode == "nn":
        (m, k), (_, n) = a.shape, b.shape
    elif mode == "nt":
        (m, k), (n, _) = a.shape, b.shape
    else:
        (k, m), (_, n) = a.shape, b.shape
    tm = _pick(m, (1024, 512, 256, 128))
    tn = _pick(n, (1024, 768, 512, 384, 256, 128))
    wide_k = mode != "tn" and pre_act is None
    tk = _pick(k, (2048, 1024, 512, 256, 128) if wide_k else (1024, 512, 256, 128))
    nk = k // tk
    if mode == "tn":
        a_spec = pl.BlockSpec((tk, tm), lambda i, j, kk: (kk, i))
    else:
        a_spec = pl.BlockSpec((tm, tk), lambda i, j, kk: (i, kk))
    if mode == "nt":
        b_spec = pl.BlockSpec((tn, tk), lambda i, j, kk: (j, kk))
    else:
        b_spec = pl.BlockSpec((tk, tn), lambda i, j, kk: (kk, j))
    dims = _DOT_DIMS[mode]

    out_spec = pl.BlockSpec((tm, tn), lambda i, j, kk: (i, j))

    def body(*refs):
        a_ref, b_ref = refs[:2]
        h_ref = refs[2] if pre_act is not None else None
        o_ref = refs[3] if pre_act is not None else refs[2]
        act_ref = refs[-1] if square_relu else None
        part = lax.dot_general(a_ref[...].astype(BF16), b_ref[...].astype(BF16), dims, preferred_element_type=F32)

        def finish(total):
            if h_ref is not None:
                total = total * (2.0 * jnp.maximum(h_ref[...], 0.0))
            o_ref[...] = total
            if act_ref is not None:
                act_ref[...] = jnp.square(jnp.maximum(total, 0.0)).astype(BF16)

        if nk == 1:
            finish(part)
        else:
            kk = pl.program_id(2)

            @pl.when(kk == 0)
            def _():
                o_ref[...] = part

            @pl.when(jnp.logical_and(kk > 0, kk < nk - 1))
            def _():
                o_ref[...] += part

            @pl.when(kk == nk - 1)
            def _():
                finish(o_ref[...] + part)

    operands = [a, b] + ([pre_act] if pre_act is not None else [])
    in_specs = [a_spec, b_spec] + ([out_spec] if pre_act is not None else [])
    out_shape = [jax.ShapeDtypeStruct((m, n), F32)] + ([jax.ShapeDtypeStruct((m, n), BF16)] if square_relu else [])
    outs = pl.pallas_call(
        body, name=name, grid=(m // tm, n // tn, nk),
        in_specs=in_specs, out_specs=[out_spec] * len(out_shape), out_shape=out_shape,
        compiler_params=_cparams(("parallel", "parallel", "arbitrary"), VMEM_LIMIT),
    )(*operands)
    return tuple(outs) if square_relu else outs[0]


def _linear(x, w, slot, name):
    @jax.custom_vjp
    def op(x, w, slot):
        return _mm(x, w, "nn", name + "_fwd")

    def fwd(x, w, slot):
        return _mm(x, w, "nn", name + "_fwd"), (x, w)

    def bwd(res, dy):
        x, w = res
        return _mm(dy, w, "nt", name + "_dx"), jnp.zeros_like(w), _mm(x, dy, "tn", name + "_dw")

    op.defvjp(fwd, bwd)
    return op(x, w, slot)


def _const_map(ndim):
    return lambda *g: (0,) * ndim


def _first_step(n_grid):
    return functools.reduce(jnp.logical_and, [pl.program_id(d) == 0 for d in range(n_grid)])


def _fused(fn, ins, outs, grid, name):
    arrays = [i[0] for i in ins]
    n_in, n_out, n_grid = len(ins), len(outs), len(grid)
    in_specs = [pl.BlockSpec(bs, im) for (_, bs, im, _) in ins]
    out_specs = [pl.BlockSpec(bs, im) for (_, bs, im, _) in outs]
    out_shapes = [jax.ShapeDtypeStruct(s, F32) for (s, _, _, _) in outs]
    sem = ("arbitrary",) * n_grid

    def fwd_call(*xs):
        def body(*refs):
            vals = [r[...] for r in refs[:n_in]]
            ys = fn(*vals)
            first = _first_step(n_grid)
            for o_ref, y, (_, _, _, kind) in zip(refs[n_in:], ys, outs):
                if kind == "t":
                    o_ref[...] = y
                else:
                    @pl.when(first)
                    def _(o_ref=o_ref):
                        o_ref[...] = jnp.zeros_like(o_ref)

                    o_ref[...] += y

        return pl.pallas_call(
            body, name=name + "_fwd", grid=grid, in_specs=in_specs, out_specs=out_specs,
            out_shape=out_shapes, compiler_params=_cparams(sem, VMEM_LIMIT))(*xs)

    def bwd_call(xs, dys):
        d_specs = [pl.BlockSpec(bs, im) for (_, bs, im, _) in outs]
        g_specs = [pl.BlockSpec(bs, im) for (_, bs, im, _) in ins]
        g_shapes = [jax.ShapeDtypeStruct(a.shape, F32) for a in arrays]

        def body(*refs):
            vals = [r[...] for r in refs[:n_in]]
            dvals = tuple(r[...] for r in refs[n_in:n_in + n_out])
            _, vjp = jax.vjp(lambda *v: tuple(fn(*v)), *vals)
            gs = vjp(dvals)
            first = _first_step(n_grid)
            for g_ref, g, (_, _, _, kind) in zip(refs[n_in + n_out:], gs, ins):
                if kind == "t":
                    g_ref[...] = g
                else:
                    @pl.when(first)
                    def _(g_ref=g_ref):
                        g_ref[...] = jnp.zeros_like(g_ref)

                    g_ref[...] += g

        return pl.pallas_call(
            body, name=name + "_bwd", grid=grid, in_specs=in_specs + d_specs, out_specs=g_specs,
            out_shape=g_shapes, compiler_params=_cparams(sem, VMEM_LIMIT))(*xs, *dys)

    @jax.custom_vjp
    def op(*xs):
        return tuple(fwd_call(*xs))

    def op_fwd(*xs):
        return tuple(fwd_call(*xs)), xs

    def op_bwd(xs, dys):
        return tuple(bwd_call(xs, dys))

    op.defvjp(op_fwd, op_bwd)
    return op(*arrays)


def _tiled(a, tile, cols=None, col_block=0):
    cols = a.shape[1] if cols is None else cols
    return (a, (tile, cols), lambda i, cb=col_block: (i, cb), "t")


def _shared(a):
    return (a, a.shape, _const_map(a.ndim), "b")


def _tiled_out(rows, cols, tile):
    return ((rows, cols), (tile, cols), lambda i: (i, 0), "t")


@jax.custom_vjp
def _bdot(x, w):
    return jnp.dot(x.astype(BF16), w.astype(BF16), preferred_element_type=F32)


def _bdot_fwd(x, w):
    return _bdot(x, w), (x, w)


def _bdot_bwd(res, dy):
    x, w = res
    dyb = dy.astype(BF16)
    dx = lax.dot_general(dyb, w.astype(BF16), _DOT_DIMS["nt"], preferred_element_type=F32)
    dw = lax.dot_general(x.astype(BF16), dyb, _DOT_DIMS["tn"], preferred_element_type=F32)
    return dx, dw


_bdot.defvjp(_bdot_fwd, _bdot_bwd)


def _head_sum(x):
    n = x.shape[-1]
    hi = lax.broadcasted_iota(jnp.int32, (n, n), 0) // HEAD_DIM
    hj = lax.broadcasted_iota(jnp.int32, (n, n), 1) // HEAD_DIM
    e = (hi == hj).astype(F32)
    return jnp.dot(x, e, precision=lax.Precision.HIGHEST, preferred_element_type=F32)


def _softplus(x):
    return jnp.maximum(x, 0.0) + jnp.log1p(jnp.exp(-jnp.abs(x)))


def _layer_norm_rows(z, g, b):
    mu = jnp.mean(z, axis=-1, keepdims=True)
    zc = z - mu
    var = jnp.mean(zc * zc, axis=-1, keepdims=True)
    return zc * lax.rsqrt(var + LN_EPS) * g + b


N_PAIRS = N_HEADS // 2


def _scan_consts():
    k = lax.broadcasted_iota(jnp.int32, (HEAD_DIM, LANES), 0)
    j = lax.broadcasted_iota(jnp.int32, (HEAD_DIM, LANES), 1)
    diag = ((j % HEAD_DIM) == k).astype(F32)
    jj = lax.broadcasted_iota(jnp.int32, (LANES, LANES), 0) // HEAD_DIM
    ll = lax.broadcasted_iota(jnp.int32, (LANES, LANES), 1) // HEAD_DIM
    same_head = (jj == ll).astype(BF16)
    return diag, same_head


def _unrolled(n, body, carry):
    for i in range(n):
        carry = body(i, carry)
    return carry


def _fill_cols(srcs, col_ref, n_steps):
    assert n_steps == 16
    blocks = []
    for src in srcs:
        x = src[...]
        hi = x.astype(BF16).astype(F32)
        r1 = x - hi
        mid = r1.astype(BF16).astype(F32)
        x48 = jnp.concatenate([hi, mid, r1 - mid], axis=0)
        for hp in range(N_PAIRS):
            xp = x48[:, hp * LANES:(hp + 1) * LANES]
            y = jnp.concatenate([xp, pltpu.roll(xp, HEAD_DIM, 1), jnp.zeros((32, LANES), F32)], axis=0)
            blocks.append(y.T[:HEAD_DIM].astype(BF16))
    lhs = jnp.concatenate(blocks, axis=0)
    j = lax.broadcasted_iota(jnp.int32, (LANES, LANES), 0)
    lane_head = lax.broadcasted_iota(jnp.int32, (LANES, LANES), 1) // HEAD_DIM
    def pick(t):
        return jnp.logical_and(j < 96, jnp.logical_and(j % 16 == t, j // 48 == lane_head)).astype(BF16)

    for t in range(0, n_steps, 2):
        out = jnp.dot(lhs, jnp.concatenate([pick(t), pick(t + 1)], axis=1), preferred_element_type=F32)
        for dt in range(2):
            for vi in range(len(srcs)):
                for hp in range(N_PAIRS):
                    r0 = (vi * N_PAIRS + hp) * HEAD_DIM
                    col_ref[vi, hp, t + dt] = out[r0:r0 + HEAD_DIM, dt * LANES:(dt + 1) * LANES]


def _scan_fwd_call(r, w, k, v, a, b, wsrc):
    t_len = r.shape[0]
    ch = SCAN_CHUNK
    n_ch = t_len // ch
    w_rows, w_cols = wsrc.shape
    half = w_rows // 2
    chunks = _n_chunks(half)
    rc = half // chunks
    transfers = [(q, kk) for q in range(chunks) for kk in range(len(CHIP_FLIPS))]
    hand_on_step = [max(1, min(n_ch - 1, (i + 1) * (n_ch - 8) // len(transfers) + 3)) for i in range(len(transfers))]

    def body(r_ref, w_ref, k_ref, v_ref, a_ref, b_ref, src_ref, y_ref, ck_ref, g_ref,
             st_ref, col_ref, send1, recv1, send2, recv2):
        c = pl.program_id(0)
        me = _me()
        core = me[2]
        sibling = _flip(me, CORE_FLIP)

        def over_ici(q, kk):
            return pltpu.make_async_remote_copy(
                src_ref=src_ref.at[core, pl.ds(q * rc, rc)], dst_ref=g_ref.at[_chip_of(me), core, pl.ds(q * rc, rc)],
                send_sem=send1.at[kk * chunks + q], recv_sem=recv1.at[kk * chunks + q],
                device_id=_flip(me, CHIP_FLIPS[kk]), device_id_type=MESH)

        def hand_on(q, kk):
            landed = g_ref.at[_chip_of(_flip(me, CHIP_FLIPS[kk])), core, pl.ds(q * rc, rc)]
            return pltpu.make_async_remote_copy(
                src_ref=landed, dst_ref=landed, send_sem=send2.at[kk * chunks + q], recv_sem=recv2.at[kk * chunks + q],
                device_id=sibling, device_id_type=MESH)

        @pl.when(c == 0)
        def _():
            st_ref[...] = jnp.zeros_like(st_ref)
            for q, kk in transfers:
                over_ici(q, kk).start()

        for step in sorted(set(hand_on_step)):
            @pl.when(c == step)
            def _(step=step):
                for (q, kk), s in zip(transfers, hand_on_step):
                    if s == step:
                        over_ici(q, kk).wait_recv()
                        hand_on(q, kk).start()

        ck_ref[0] = st_ref[...]
        _fill_cols((w_ref, a_ref, b_ref, k_ref, r_ref), col_ref, ch)

        def step(t, states):
            new = []
            for hp in range(N_PAIRS):
                lanes = slice(hp * LANES, (hp + 1) * LANES)
                s = states[hp]
                sa = jnp.sum(s * col_ref[1, hp, t], axis=0, keepdims=True)
                s = s * col_ref[0, hp, t] + col_ref[2, hp, t] * sa + col_ref[3, hp, t] * v_ref[pl.ds(t, 1), lanes]
                y_ref[pl.ds(t, 1), lanes] = jnp.sum(s * col_ref[4, hp, t], axis=0, keepdims=True)
                new.append(s)
            return tuple(new)

        states = _unrolled(ch, step, tuple(st_ref[hp] for hp in range(N_PAIRS)))
        for hp in range(N_PAIRS):
            st_ref[hp] = states[hp]

        @pl.when(c == n_ch - 1)
        def _():
            for q, kk in transfers:
                hand_on(q, kk).wait_recv()
            for q, kk in transfers:
                over_ici(q, kk).wait_send()
                hand_on(q, kk).wait_send()

    row_spec = pl.BlockSpec((ch, RW_WIDTH), lambda c: (c, 0))
    any_spec = pl.BlockSpec(memory_space=pl.ANY)
    n_sem = len(transfers)
    y, ck, got = pl.pallas_call(
        body, name="rwkv_scan_fwd", grid=(n_ch,),
        in_specs=[row_spec] * 6 + [any_spec],
        out_specs=[row_spec, pl.BlockSpec((1, N_PAIRS, HEAD_DIM, LANES), lambda c: (c, 0, 0, 0)), any_spec],
        out_shape=[jax.ShapeDtypeStruct((t_len, RW_WIDTH), F32),
                   jax.ShapeDtypeStruct((n_ch, N_PAIRS, HEAD_DIM, LANES), F32),
                   jax.ShapeDtypeStruct((N_CHIPS, 2, half, w_cols), wsrc.dtype)],
        scratch_shapes=[pltpu.VMEM((N_PAIRS, HEAD_DIM, LANES), F32),
                        pltpu.VMEM((5, N_PAIRS, ch, HEAD_DIM, LANES), F32)]
        + [pltpu.SemaphoreType.DMA((n_sem,)) for _ in range(4)],
        compiler_params=_cparams(("arbitrary",), VMEM_LIMIT),
    )(r, w, k, v, a, b, wsrc.reshape(2, half, w_cols))
    return y, ck, got.reshape(N_CHIPS, w_rows, w_cols)


def _scan_bwd_call(r, w, k, v, a, b, ck, dy, gsend):
    t_len = r.shape[0]
    ch = SCAN_CHUNK
    n_ch = t_len // ch
    _, g_rows, g_cols = gsend.shape
    chunks = _n_chunks(g_rows)
    rc = g_rows // chunks
    transfers = [(q, kk) for q in range(chunks) for kk in range(len(CHIP_FLIPS))]

    def body(r_ref, w_ref, k_ref, v_ref, a_ref, b_ref, ck_ref, dy_ref, gs_ref,
             dr_ref, dw_ref, dk_ref, dv_ref, da_ref, db_ref, gr_ref,
             ds_ref, col_ref, sp_ref, sa_ref, send_sems, recv_sems):
        c = pl.program_id(0)
        me = _me()

        def to_chip(q, kk):
            peer = _flip(me, CHIP_FLIPS[kk])
            return pltpu.make_async_remote_copy(
                src_ref=gs_ref.at[_chip_of(peer), pl.ds(q * rc, rc)], dst_ref=gr_ref.at[_chip_of(me), pl.ds(q * rc, rc)],
                send_sem=send_sems.at[kk * chunks + q], recv_sem=recv_sems.at[kk * chunks + q],
                device_id=peer, device_id_type=MESH)

        @pl.when(c == 0)
        def _():
            ds_ref[...] = jnp.zeros_like(ds_ref)
            for q, kk in transfers:
                to_chip(q, kk).start()

        diag_f, same_head = _scan_consts()
        _fill_cols((w_ref, a_ref, b_ref, k_ref, r_ref), col_ref, ch)

        def replay(t, states):
            new = []
            for hp in range(N_PAIRS):
                lanes = slice(hp * LANES, (hp + 1) * LANES)
                s = states[hp]
                sp_ref[t, hp] = s
                sa = jnp.sum(s * col_ref[1, hp, t], axis=0, keepdims=True)
                sa_ref[pl.ds(t, 1), lanes] = sa
                new.append(s * col_ref[0, hp, t] + col_ref[2, hp, t] * sa
                           + col_ref[3, hp, t] * v_ref[pl.ds(t, 1), lanes])
            return tuple(new)

        _unrolled(ch, replay, tuple(ck_ref[0, hp] for hp in range(N_PAIRS)))

        def key_rows(ps):
            stacked = jnp.concatenate([p.astype(BF16) for p in ps], axis=0)
            q = jnp.dot(stacked, same_head, preferred_element_type=F32)
            return [jnp.sum(q[i * HEAD_DIM:(i + 1) * HEAD_DIM] * diag_f, axis=0, keepdims=True)
                    for i in range(len(ps))]

        def back(i, grads):
            t = ch - 1 - i
            new = []
            for hp in range(N_PAIRS):
                lanes = slice(hp * LANES, (hp + 1) * LANES)
                wc, ac, bc, kc, rc = (col_ref[vi, hp, t] for vi in range(5))
                sp = sp_ref[t, hp]
                sa = sa_ref[pl.ds(t, 1), lanes]
                vrow = v_ref[pl.ds(t, 1), lanes]
                dyrow = dy_ref[pl.ds(t, 1), lanes]
                st = sp * wc + bc * sa + kc * vrow
                g = grads[hp] + rc * dyrow
                dsa = jnp.sum(g * bc, axis=0, keepdims=True)
                dv_ref[pl.ds(t, 1), lanes] = jnp.sum(g * kc, axis=0, keepdims=True)
                rows = key_rows([st * dyrow, g * vrow, g * sa, g * sp, sp * dsa])
                for out_ref, row in zip((dr_ref, dk_ref, db_ref, dw_ref, da_ref), rows):
                    out_ref[pl.ds(t, 1), lanes] = row
                new.append(g * wc + ac * dsa)
            return tuple(new)

        grads = _unrolled(ch, back, tuple(ds_ref[hp] for hp in range(N_PAIRS)))
        for hp in range(N_PAIRS):
            ds_ref[hp] = grads[hp]

        @pl.when(c == n_ch - 1)
        def _():
            for q, kk in transfers:
                to_chip(q, kk).wait()

    row_spec = pl.BlockSpec((ch, RW_WIDTH), lambda c: (n_ch - 1 - c, 0))
    any_spec = pl.BlockSpec(memory_space=pl.ANY)
    out_sds = jax.ShapeDtypeStruct((t_len, RW_WIDTH), F32)
    n_sem = len(transfers)
    return pl.pallas_call(
        body, name="rwkv_scan_bwd", grid=(n_ch,),
        in_specs=[row_spec] * 6 + [pl.BlockSpec((1, N_PAIRS, HEAD_DIM, LANES), lambda c: (n_ch - 1 - c, 0, 0, 0)),
                                   row_spec, any_spec],
        out_specs=[row_spec] * 6 + [any_spec],
        out_shape=[out_sds] * 6 + [jax.ShapeDtypeStruct(gsend.shape, gsend.dtype)],
        scratch_shapes=[pltpu.VMEM((N_PAIRS, HEAD_DIM, LANES), F32),
                        pltpu.VMEM((5, N_PAIRS, ch, HEAD_DIM, LANES), F32),
                        pltpu.VMEM((ch, N_PAIRS, HEAD_DIM, LANES), F32),
                        pltpu.VMEM((ch, RW_WIDTH), F32),
                        pltpu.SemaphoreType.DMA((n_sem,)), pltpu.SemaphoreType.DMA((n_sem,))],
        compiler_params=_cparams(("arbitrary",), VMEM_LIMIT),
    )(r, w, k, v, a, b, ck, dy, gsend)


def _scan_token(wsrc):
    return jnp.zeros((N_CHIPS,) + wsrc.shape, BF16)


@jax.custom_vjp
def _rwkv_scan(r, w, k, v, a, b, wsrc, gslot):
    y, _, got = _scan_fwd_call(r, w, k, v, a, b, wsrc)
    return y, got, _scan_token(wsrc)


def _rwkv_scan_fwd(r, w, k, v, a, b, wsrc, gslot):
    y, ck, got = _scan_fwd_call(r, w, k, v, a, b, wsrc)
    return (y, got, _scan_token(wsrc)), (r, w, k, v, a, b, ck, wsrc)


def _rwkv_scan_bwd(res, cts):
    *saved, wsrc = res
    dy, _, gsend = cts
    *dins, got = _scan_bwd_call(*saved, dy, gsend)
    summed = _sum_slots_own(got, gsend, _chip_of(_me()), "sum_chips_late")
    return (*dins, jnp.zeros_like(wsrc), summed)


_rwkv_scan.defvjp(_rwkv_scan_fwd, _rwkv_scan_bwd)


ATT_SCALE = HEAD_DIM ** -0.5


def _att_masks():
    qi = lax.broadcasted_iota(jnp.int32, (BLOCK, BLOCK), 0)
    ki = lax.broadcasted_iota(jnp.int32, (BLOCK, BLOCK), 1)
    lane = lax.broadcasted_iota(jnp.int32, (1, LANES), 1)
    return ki <= qi, ki >= qi, lane


def _att_fwd_call(p, bias, g, dil):
    t_len = p.shape[0]
    l_len = t_len // dil
    nb = l_len // BLOCK

    def body(q_ref, kc_ref, kp_ref, vc_ref, vp_ref, bias_ref, o_ref, lse_ref):
        n = pl.program_id(1)
        cur_ok, prev_band, lane = _att_masks()
        prev_ok = jnp.logical_and(prev_band, n > 0)
        for hp in range(N_PAIRS):
            lanes = slice(hp * LANES, (hp + 1) * LANES)
            q2 = q_ref[:, lanes].astype(BF16)
            kc = kc_ref[:, lanes].astype(BF16)
            kp = kp_ref[:, lanes].astype(BF16)
            vc = vc_ref[:, lanes].astype(BF16)
            vp = vp_ref[:, lanes].astype(BF16)
            o2 = jnp.zeros((BLOCK, LANES), F32)
            for hh in range(2):
                h = 2 * hp + hh
                mine = (lane // HEAD_DIM) == hh
                qm = jnp.where(mine, q2, jnp.zeros_like(q2))
                s_c = lax.dot_general(qm, kc, _DOT_DIMS["nt"], preferred_element_type=F32) * ATT_SCALE
                s_c = jnp.where(cur_ok, s_c + bias_ref[h, :, BLOCK:], NEG_BIG)
                m = jnp.max(s_c, axis=-1, keepdims=True)
                if nb > 1:
                    s_p = lax.dot_general(qm, kp, _DOT_DIMS["nt"], preferred_element_type=F32) * ATT_SCALE
                    s_p = jnp.where(prev_ok, s_p + bias_ref[h, :, :BLOCK], NEG_BIG)
                    m = jnp.maximum(m, jnp.max(s_p, axis=-1, keepdims=True))
                e_c = jnp.exp(s_c - m)
                den = jnp.sum(e_c, axis=-1, keepdims=True)
                if nb > 1:
                    e_p = jnp.exp(s_p - m)
                    den = den + jnp.sum(e_p, axis=-1, keepdims=True)
                o_h = jnp.dot((e_c / den).astype(BF16), vc, preferred_element_type=F32)
                if nb > 1:
                    o_h = o_h + jnp.dot((e_p / den).astype(BF16), vp, preferred_element_type=F32)
                o2 = o2 + jnp.where(mine, o_h, 0.0)
                lse_ref[h] = jnp.broadcast_to(m + jnp.log(den), (BLOCK, LANES))
            o_ref[:, lanes] = o2

    def col(j):
        return lambda r, n: (r * nb + n, j)

    def col_prev(j):
        return lambda r, n: (r * nb + jnp.maximum(n - 1, 0), j)

    blk = (BLOCK, DIL_WIDTH)
    return pl.pallas_call(
        body, name=f"dil_att_fwd_g{g}", grid=(dil, nb),
        in_specs=[pl.BlockSpec(blk, col(0)), pl.BlockSpec(blk, col(1)), pl.BlockSpec(blk, col_prev(1)),
                  pl.BlockSpec(blk, col(2)), pl.BlockSpec(blk, col_prev(2)),
                  pl.BlockSpec(bias.shape, _const_map(3))],
        out_specs=[pl.BlockSpec(blk, lambda r, n: (r * nb + n, 0)),
                   pl.BlockSpec((N_HEADS, BLOCK, LANES), lambda r, n: (0, r * nb + n, 0))],
        out_shape=[jax.ShapeDtypeStruct((t_len, DIL_WIDTH), F32),
                   jax.ShapeDtypeStruct((N_HEADS, t_len, LANES), F32)],
        compiler_params=_cparams(("arbitrary", "arbitrary"), VMEM_LIMIT),
    )(p, p, p, p, p, bias)


def _att_bwd_call(p, bias, o, lse, do, dlse, g, dil):
    t_len = p.shape[0]
    l_len = t_len // dil
    nb = l_len // BLOCK

    def body(q_ref, qn_ref, k_ref, v_ref, do_ref, don_ref, o_ref, on_ref, lse_ref, lsen_ref, dl_ref, dln_ref,
             bias_ref, dq_ref, dk_ref, dv_ref, dbias_ref, carry_ref):
        r = pl.program_id(0)
        n = pl.program_id(1)
        cur_ok, prev_band, lane = _att_masks()
        has_next = n + 1 < nb

        @pl.when(jnp.logical_and(r == 0, n == 0))
        def _():
            dbias_ref[...] = jnp.zeros_like(dbias_ref)

        @pl.when(n == 0)
        def _():
            carry_ref[...] = jnp.zeros_like(carry_ref)

        for hp in range(N_PAIRS):
            lanes = slice(hp * LANES, (hp + 1) * LANES)
            k2 = k_ref[:, lanes].astype(BF16)
            v2 = v_ref[:, lanes].astype(BF16)
            dk2 = jnp.zeros((BLOCK, LANES), F32)
            dv2 = jnp.zeros((BLOCK, LANES), F32)
            dq_cur = carry_ref[:, lanes]
            dq_next = jnp.zeros((BLOCK, LANES), F32)
            for hh in range(2):
                h = 2 * hp + hh
                mine = (lane // HEAD_DIM) == hh
                tiles = (
                    (q_ref, do_ref, o_ref, lse_ref, dl_ref, cur_ok, slice(BLOCK, 2 * BLOCK), None),
                    (qn_ref, don_ref, on_ref, lsen_ref, dln_ref, prev_band, slice(0, BLOCK), has_next),
                )
                if nb == 1:
                    tiles = tiles[:1]
                for ti, (qr, dor, orf, lr, dlr, ok, bcols, gate) in enumerate(tiles):
                    q2 = qr[:, lanes].astype(BF16)
                    qm = jnp.where(mine, q2, jnp.zeros_like(q2))
                    do_f = jnp.where(mine, dor[:, lanes], 0.0)
                    dom = do_f.astype(BF16)
                    s = lax.dot_general(qm, k2, _DOT_DIMS["nt"], preferred_element_type=F32) * ATT_SCALE
                    s = s + bias_ref[h, :, bcols]
                    if gate is not None:
                        ok = jnp.logical_and(ok, gate)
                    pr = jnp.where(ok, jnp.exp(jnp.minimum(s - lr[h], 0.0)), 0.0)
                    dp = lax.dot_general(dom, v2, _DOT_DIMS["nt"], preferred_element_type=F32)
                    delta = jnp.sum(do_f * orf[:, lanes], axis=-1, keepdims=True)
                    dl = jnp.sum(dlr[h], axis=-1, keepdims=True)
                    ds = pr * (dp - delta + dl)
                    dsb = ds.astype(BF16)
                    dq_h = jnp.where(mine, jnp.dot(dsb, k2, preferred_element_type=F32), 0.0) * ATT_SCALE
                    if ti == 0:
                        dq_cur = dq_cur + dq_h
                    else:
                        dq_next = dq_next + dq_h
                    dk2 = dk2 + lax.dot_general(dsb, qm, _DOT_DIMS["tn"], preferred_element_type=F32) * ATT_SCALE
                    dv2 = dv2 + lax.dot_general(pr.astype(BF16), dom, _DOT_DIMS["tn"], preferred_element_type=F32)
                    dbias_ref[h, :, bcols] += ds
            dq_ref[:, lanes] = dq_cur
            carry_ref[:, lanes] = dq_next
            dk_ref[:, lanes] = dk2
            dv_ref[:, lanes] = dv2

    def nxt(n):
        return jnp.minimum(n + 1, nb - 1)

    blk = (BLOCK, DIL_WIDTH)
    hblk = (N_HEADS, BLOCK, LANES)
    qcol = lambda j: (lambda r, n: (r * nb + n, j))
    q_next = lambda r, n: (r * nb + nxt(n), 0)
    rown = lambda r, n: (r * nb + n, 0)
    rown_next = lambda r, n: (r * nb + nxt(n), 0)
    hrow = lambda r, n: (0, r * nb + n, 0)
    hrow_next = lambda r, n: (0, r * nb + nxt(n), 0)
    sds = jax.ShapeDtypeStruct((t_len, DIL_WIDTH), F32)
    return pl.pallas_call(
        body, name=f"dil_att_bwd_g{g}", grid=(dil, nb),
        in_specs=[pl.BlockSpec(blk, qcol(0)), pl.BlockSpec(blk, q_next),
                  pl.BlockSpec(blk, qcol(1)), pl.BlockSpec(blk, qcol(2)),
                  pl.BlockSpec(blk, rown), pl.BlockSpec(blk, rown_next),
                  pl.BlockSpec(blk, rown), pl.BlockSpec(blk, rown_next),
                  pl.BlockSpec(hblk, hrow), pl.BlockSpec(hblk, hrow_next),
                  pl.BlockSpec(hblk, hrow), pl.BlockSpec(hblk, hrow_next),
                  pl.BlockSpec(bias.shape, _const_map(3))],
        out_specs=[pl.BlockSpec(blk, rown)] * 3 + [pl.BlockSpec(bias.shape, _const_map(3))],
        out_shape=[sds, sds, sds, jax.ShapeDtypeStruct(bias.shape, F32)],
        scratch_shapes=[pltpu.VMEM((BLOCK, DIL_WIDTH), F32)],
        compiler_params=_cparams(("arbitrary", "arbitrary"), VMEM_LIMIT),
    )(p, p, p, p, do, do, o, o, lse, lse, dlse, dlse, bias)


def _att_all_groups(ps, biases):
    outs = [_att_fwd_call(ps[g], biases[g], g, dil) for g, (_, dil) in enumerate(DIL_PATTERNS)]
    return tuple(o for o, _ in outs), tuple(l for _, l in outs)


@jax.custom_vjp
def _dilated_attention(ps, biases):
    return _att_all_groups(ps, biases)


def _dilated_attention_fwd(ps, biases):
    os_, lses = _att_all_groups(ps, biases)
    return (os_, lses), (ps, biases, os_, lses)


def _dilated_attention_bwd(res, cts):
    ps, biases, os_, lses = res
    dos, dlses = cts
    dps, dbiases = [], []
    for g, (_, dil) in enumerate(DIL_PATTERNS):
        dq, dk, dv, dbias = _att_bwd_call(ps[g], biases[g], os_[g], lses[g], dos[g], dlses[g], g, dil)
        dps.append(jnp.concatenate([dq, dk, dv], axis=1))
        dbiases.append(dbias)
    return tuple(dps), tuple(dbiases)


_dilated_attention.defvjp(_dilated_attention_fwd, _dilated_attention_bwd)


def _me():
    return lax.axis_index("x"), lax.axis_index("y"), lax.axis_index("c")


def _flip(me, f):
    return tuple((1 - m) if b else m for m, b in zip(me, f))


def _chip_of(d):
    return 2 * d[0] + d[1]


def _dev_of(d):
    return 4 * d[0] + 2 * d[1] + d[2]


EXCHANGE_CHUNKS = 8
CHIP_FLIPS = ((1, 0, 0), (0, 1, 0), (1, 1, 0))
ALL_FLIPS = tuple((a, b, c) for a in (0, 1) for b in (0, 1) for c in (0, 1) if a or b or c)
CORE_FLIP = (0, 0, 1)


def _n_chunks(rows):
    return EXCHANGE_CHUNKS if rows % (EXCHANGE_CHUNKS * PACK_ROWS) == 0 else 1


def _exchange(src, n_slots, transfers, name):
    _, rows, cols = src.shape
    chunks = _n_chunks(rows)
    rc = rows // chunks
    n = len(transfers) * chunks

    def body(src_ref, dst_ref, send_sems, recv_sems):
        me = _me()
        copies = []
        for q in range(chunks):
            for kk, (f, src_slot, dst_slot) in enumerate(transfers):
                peer = _flip(me, f)
                cp = pltpu.make_async_remote_copy(
                    src_ref=src_ref.at[src_slot(me, peer), pl.ds(q * rc, rc)],
                    dst_ref=dst_ref.at[dst_slot(me, peer), pl.ds(q * rc, rc)],
                    send_sem=send_sems.at[kk * chunks + q], recv_sem=recv_sems.at[kk * chunks + q],
                    device_id=peer, device_id_type=MESH)
                cp.start()
                copies.append(cp)
        for cp in copies:
            cp.wait()

    return pl.pallas_call(
        body, name=name,
        out_shape=jax.ShapeDtypeStruct((n_slots, rows, cols), src.dtype),
        in_specs=[pl.BlockSpec(memory_space=pl.ANY)],
        out_specs=pl.BlockSpec(memory_space=pl.ANY),
        scratch_shapes=[pltpu.SemaphoreType.DMA((n,)), pltpu.SemaphoreType.DMA((n,))],
    )(src)


def _set_slot(buf, block, index):
    return lax.dynamic_update_slice_in_dim(buf, block[None].astype(buf.dtype), index, axis=0)


def _chip_all_gather(src, name):
    got = _exchange(src[None], N_CHIPS, [(f, lambda me, peer: 0, lambda me, peer: _chip_of(me)) for f in CHIP_FLIPS], name)
    return _set_slot(got, src, _chip_of(_me()))


def _dev_all_gather(src, name):
    got = _exchange(src[None], N_DEV, [(f, lambda me, peer: 0, lambda me, peer: _dev_of(me)) for f in ALL_FLIPS], name)
    return _set_slot(got, src, _dev_of(_me()))


def _chip_scatter(src, name):
    got = _exchange(src, N_CHIPS, [(f, lambda me, peer: _chip_of(peer), lambda me, peer: _chip_of(me))
                                   for f in CHIP_FLIPS], name)
    chip = _chip_of(_me())
    return _set_slot(got, lax.dynamic_index_in_dim(src, chip, 0, keepdims=False), chip)


def _core_halves(src, name):
    s, _, half, cols = src.shape
    transfers = [(CORE_FLIP, (lambda me, peer, j=j: 2 * j + peer[2]), (lambda me, peer, j=j: j)) for j in range(s)]
    return _exchange(src.reshape(2 * s, half, cols), s, transfers, name)


def _core_all_gather(src, name):
    got = _exchange(src[None], 2, [(CORE_FLIP, lambda me, peer: 0, lambda me, peer: me[2])], name)
    return _set_slot(got, src, _me()[2])


def _two_level_gather(src, name):
    rows, cols = src.shape
    half = rows // 2
    chunks = _n_chunks(half)
    rc = half // chunks
    n = len(CHIP_FLIPS) * chunks

    def body(src_ref, g_ref, send1, recv1, send2, recv2):
        me = _me()
        c = me[2]
        sibling = _flip(me, CORE_FLIP)
        first, second = [], []
        for q in range(chunks):
            for kk, f in enumerate(CHIP_FLIPS):
                peer = _flip(me, f)
                cp = pltpu.make_async_remote_copy(
                    src_ref=src_ref.at[c, pl.ds(q * rc, rc)], dst_ref=g_ref.at[_chip_of(me), c, pl.ds(q * rc, rc)],
                    send_sem=send1.at[kk * chunks + q], recv_sem=recv1.at[kk * chunks + q],
                    device_id=peer, device_id_type=MESH)
                cp.start()
                first.append((cp, _chip_of(peer), kk * chunks + q, q))
        for cp, origin, idx, q in first:
            cp.wait_recv()
            fw = pltpu.make_async_remote_copy(
                src_ref=g_ref.at[origin, c, pl.ds(q * rc, rc)], dst_ref=g_ref.at[origin, c, pl.ds(q * rc, rc)],
                send_sem=send2.at[idx], recv_sem=recv2.at[idx],
                device_id=sibling, device_id_type=MESH)
            fw.start()
            second.append(fw)
        for fw in second:
            fw.wait_recv()
        for cp, _, _, _ in first:
            cp.wait_send()
        for fw in second:
            fw.wait_send()

    got = pl.pallas_call(
        body, name=name,
        out_shape=jax.ShapeDtypeStruct((N_CHIPS, 2, half, cols), src.dtype),
        in_specs=[pl.BlockSpec(memory_space=pl.ANY)],
        out_specs=pl.BlockSpec(memory_space=pl.ANY),
        scratch_shapes=[pltpu.SemaphoreType.DMA((n,)) for _ in range(4)],
    )(src.reshape(2, half, cols))
    return _set_slot(got.reshape(N_CHIPS, rows, cols), src, _chip_of(_me()))


def _sum_slots(x, name):
    s, rows, cols = x.shape
    tile = _pick(rows, (512, 256, 128, 64, 32, 16, 8))

    def body(x_ref, o_ref):
        acc = x_ref[0].astype(F32)
        for i in range(1, s):
            acc = acc + x_ref[i].astype(F32)
        o_ref[...] = acc

    return pl.pallas_call(
        body, name=name, grid=(rows // tile,),
        in_specs=[pl.BlockSpec((s, tile, cols), lambda i: (0, i, 0))],
        out_specs=pl.BlockSpec((tile, cols), lambda i: (i, 0)),
        out_shape=jax.ShapeDtypeStruct((rows, cols), F32),
        compiler_params=_cparams(("parallel",), VMEM_LIMIT),
    )(x)


def _sum_slots_own(recv, send, chip, name):
    s, rows, cols = recv.shape
    tile = _pick(rows, (512, 256, 128, 64, 32, 16, 8))

    def body(chip_ref, recv_ref, own_ref, o_ref):
        acc = None
        for j in range(s):
            term = jnp.where(chip_ref[0] == j, own_ref[0], recv_ref[j]).astype(F32)
            acc = term if acc is None else acc + term
        o_ref[...] = acc

    grid_spec = pltpu.PrefetchScalarGridSpec(
        num_scalar_prefetch=1, grid=(rows // tile,),
        in_specs=[pl.BlockSpec((s, tile, cols), lambda i, c: (0, i, 0)),
                  pl.BlockSpec((1, tile, cols), lambda i, c: (c[0], i, 0))],
        out_specs=pl.BlockSpec((tile, cols), lambda i, c: (i, 0)))
    return pl.pallas_call(
        body, name=name, grid_spec=grid_spec,
        out_shape=jax.ShapeDtypeStruct((rows, cols), F32),
        compiler_params=_cparams(("arbitrary",), VMEM_LIMIT),
    )(jnp.reshape(chip, (1,)).astype(jnp.int32), recv, send)


def _add_pairs(a, b, name):
    s, rows, cols = a.shape
    tile = _pick(rows, (512, 256, 128, 64, 32, 16, 8))

    def body(a_ref, b_ref, o_ref):
        o_ref[...] = (a_ref[...].astype(F32) + b_ref[...].astype(F32)).astype(o_ref.dtype)

    spec = pl.BlockSpec((1, tile, cols), lambda j, i: (j, i, 0))
    return pl.pallas_call(
        body, name=name, grid=(s, rows // tile),
        in_specs=[spec, spec], out_specs=spec,
        out_shape=jax.ShapeDtypeStruct(a.shape, a.dtype),
        compiler_params=_cparams(("parallel", "parallel"), VMEM_LIMIT),
    )(a, b)


def _adamw(w, g, m, v, name):
    rows, cols = w.shape
    tile = rows
    if rows * cols * 4 > 2 * 1024 * 1024:
        tile = _pick(rows, (256, 128, 64, 32, 16, 8))
    c1 = 1.0 / (1.0 - ADAM_B1 ** ADAM_STEP)
    c2 = 1.0 / (1.0 - ADAM_B2 ** ADAM_STEP)

    def body(w_ref, g_ref, m_ref, v_ref, d_ref, nm_ref, nv_ref):
        gv = g_ref[...]
        nm = ADAM_B1 * m_ref[...] + (1.0 - ADAM_B1) * gv
        nv = ADAM_B2 * v_ref[...] + (1.0 - ADAM_B2) * (gv * gv)
        m_hat = nm * c1
        v_hat = nv * c2
        d_ref[...] = -ADAM_LR * (m_hat / (jnp.sqrt(v_hat) + ADAM_EPS) + ADAM_WD * w_ref[...])
        nm_ref[...] = nm
        nv_ref[...] = nv

    spec = pl.BlockSpec((tile, cols), lambda i: (i, 0))
    sds = jax.ShapeDtypeStruct((rows, cols), F32)
    return pl.pallas_call(
        body, name=name, grid=(rows // tile,),
        in_specs=[spec] * 4, out_specs=[spec] * 3, out_shape=[sds] * 3,
        compiler_params=_cparams(("parallel",), VMEM_LIMIT),
    )(w, g, m, v)


def _ada_fwd(c_all, ada_w, ada_b_cols):
    n_col = ada_w.shape[2]

    def body(c_ref, w_ref, b_ref, o_ref):
        cv = c_ref[...]
        cond = (cv * jax.nn.sigmoid(cv)).astype(BF16)
        o_ref[0] = jnp.dot(cond, w_ref[0].astype(BF16), preferred_element_type=F32) + b_ref[0]

    return pl.pallas_call(
        body, name="ada_fwd", grid=(DEPTH,),
        in_specs=[pl.BlockSpec(c_all.shape, lambda i: (0, 0)),
                  pl.BlockSpec((1, D_MODEL, n_col), lambda i: (i, 0, 0)),
                  pl.BlockSpec((1, 1, n_col), lambda i: (i, 0, 0))],
        out_specs=pl.BlockSpec((1, N_DEV, n_col), lambda i: (i, 0, 0)),
        out_shape=jax.ShapeDtypeStruct((DEPTH, N_DEV, n_col), F32),
        compiler_params=_cparams(("parallel",), VMEM_LIMIT),
    )(c_all, ada_w, ada_b_cols)


def _ada_grad(c_all_t, dmod_cols):
    n_col = dmod_cols.shape[2]

    def body(c_ref, d_ref, o_ref):
        cv = c_ref[...]
        cond = cv * jax.nn.sigmoid(cv)
        o_ref[0] = jnp.dot(cond, d_ref[0], precision=lax.Precision.HIGHEST, preferred_element_type=F32)

    return pl.pallas_call(
        body, name="ada_grad", grid=(DEPTH,),
        in_specs=[pl.BlockSpec(c_all_t.shape, lambda i: (0, 0)),
                  pl.BlockSpec((1, LANES, n_col), lambda i: (i, 0, 0))],
        out_specs=pl.BlockSpec((1, D_MODEL, n_col), lambda i: (i, 0, 0)),
        out_shape=jax.ShapeDtypeStruct((DEPTH, D_MODEL, n_col), F32),
        compiler_params=_cparams(("parallel",), VMEM_LIMIT),
    )(c_all_t, dmod_cols)


ROW_TILE = 256


def _shift_rows(a, n=1):
    return jnp.pad(a, ((n, 0), (0, 0)))[:-n]


def _modulate(x, sc, sh, name):
    def fn(x, sc, sh):
        return (x * (1.0 + sc) + sh,)

    t = x.shape[0]
    return _fused(fn, [_tiled(x, ROW_TILE), _shared(sc), _shared(sh)],
                  [_tiled_out(t, D_MODEL, ROW_TILE)], (t // ROW_TILE,), name)[0]


def _resid_ln_mod(x, y, gate, ln_g, ln_b, sc, sh, name):
    def fn(x, y, gate, ln_g, ln_b, sc, sh):
        x1 = _layer_norm_rows(ALPHA * x + (1.0 + gate) * y, ln_g, ln_b)
        return x1, x1 * (1.0 + sc) + sh

    t = x.shape[0]
    return _fused(fn, [_tiled(x, ROW_TILE), _tiled(y, ROW_TILE)] + [_shared(a) for a in (gate, ln_g, ln_b, sc, sh)],
                  [_tiled_out(t, D_MODEL, ROW_TILE)] * 2, (t // ROW_TILE,), name)


def _resid_ln_loss(x, y, gate, ln_g, ln_b, target, name):
    def fn(x, y, gate, ln_g, ln_b, target):
        x1 = _layer_norm_rows(ALPHA * x + (1.0 + gate) * y, ln_g, ln_b)
        err = jnp.square(x1 - target)
        per_row = jnp.mean(err, axis=-1, keepdims=True)
        return (0.5 * jnp.sum(per_row, axis=0, keepdims=True),)

    t = x.shape[0]
    return _fused(fn, [_tiled(x, ROW_TILE), _tiled(y, ROW_TILE)] + [_shared(a) for a in (gate, ln_g, ln_b)]
                  + [_tiled(target, ROW_TILE)],
                  [((1, 1), (1, 1), _const_map(2), "a")], (t // ROW_TILE,), name)[0]


def _mlp(u, w1, s1, w2, s2, name):
    def run(u, w1, w2):
        h, act = _mm(u, w1, "nn", name + "_w1_fwd", square_relu=True)
        return _mm(act, w2, "nn", name + "_w2_fwd"), (u, w1, w2, h, act)

    @jax.custom_vjp
    def op(u, w1, s1, w2, s2):
        return run(u, w1, w2)[0]

    def fwd(u, w1, s1, w2, s2):
        return run(u, w1, w2)

    def bwd(res, dy):
        u, w1, w2, h, act = res
        dh = _mm(dy, w2, "nt", name + "_w2_dx", pre_act=h)
        dw2 = _mm(act, dy, "tn", name + "_w2_dw")
        du = _mm(dh, w1, "nt", name + "_w1_dx")
        dw1 = _mm(u, dh, "tn", name + "_w1_dw")
        return du, jnp.zeros_like(w1), dw1, jnp.zeros_like(w2), dw2

    op.defvjp(fwd, bwd)
    return op(u, w1, s1, w2, s2)


AB_PIECES = (("r", 0, 512, 512), ("k", 512, 512, 512), ("v", 1024, 512, 512),
             ("wd", 1536, 64, 128), ("ad", 1600, 64, 128), ("gd", 1664, 160, 256),
             ("h", 1824, 512, 512), ("bg", 2336, 512, 512), ("cg", 2848, 512, 512))
AB_PAD_COLS = sum(p[3] for p in AB_PIECES)


def _regroup_cols(w):
    parts = []
    for _, start, width, padded in AB_PIECES:
        piece = w[..., start:start + width]
        if padded != width:
            piece = jnp.pad(piece, [(0, 0)] * (w.ndim - 1) + [(0, padded - width)])
        parts.append(piece)
    return jnp.concatenate(parts, axis=-1)


def _pad_rows(w, rows):
    return jnp.pad(w, ((0, rows - w.shape[0]), (0, 0)))


def _rwkv_shortconv(u, big, wts, wsrc, gslot):
    t = u.shape[0]
    p = _linear(u, _regroup_cols(big["ab_w_in"][0]), _regroup_cols(wts["ab_w_in"][0]), "ab_in")
    mu = _regroup_cols(jnp.pad(wts["rw_mu"], ((0, 0), (0, AB_PROJ - RW_PROJ))))
    w_up = _pad_rows(wts["rw_w_up"][0], 128)
    a_up = _pad_rows(wts["rw_a_up"][0], 128)
    g_up = _pad_rows(wts["rw_g_up"][0], 256)

    def pre(rp, rs, kp, ks, vp, vs, wdp, wds, adp, ads, gdp, gds, h, cg,
            mu_r, mu_k, mu_v, mu_w, mu_a, mu_g, w0, w_up, a0, a_up, g_up, k_k, k_a):
        def mix(pv, sv, m):
            return pv + m * (sv - pv)

        r, k, v = mix(rp, rs, mu_r), mix(kp, ks, mu_k), mix(vp, vs, mu_v)
        wd, ad, gd = mix(wdp, wds, mu_w), mix(adp, ads, mu_a), mix(gdp, gds, mu_g)
        logw = -_softplus(-(w0 + _bdot(jnp.tanh(wd), w_up))) - 0.5
        decay = jnp.exp(-jnp.exp(logw))
        iclr = jax.nn.sigmoid(a0 + _bdot(ad, a_up))
        gate = _bdot(jax.nn.sigmoid(gd), g_up)
        kk = k * k_k
        kk = kk / jnp.maximum(jnp.sqrt(_head_sum(kk * kk)), 1e-12)
        k_h = k * (1.0 + (iclr - 1.0) * k_a)
        return r, decay, k_h, v, -kk, kk * iclr, gate, cg * h

    tile = ROW_TILE
    names = [q[0] for q in AB_PIECES]
    cuts = list(np.cumsum([q[3] for q in AB_PIECES])[:-1])
    pp = dict(zip(names, jnp.split(p, cuts, axis=1)))
    mp = dict(zip(names, jnp.split(mu, cuts, axis=1)))

    ins = []
    for name in ("r", "k", "v", "wd", "ad", "gd"):
        ins += [_tiled(pp[name], tile), _tiled(_shift_rows(pp[name]), tile)]
    ins += [_tiled(pp["h"], tile), _tiled(pp["cg"], tile)]
    ins += [_shared(mp[name]) for name in ("r", "k", "v", "wd", "ad", "gd")]
    ins += [_shared(a) for a in (wts["rw_w0"], w_up, wts["rw_a0"], a_up, g_up, wts["rw_k_k"], wts["rw_k_a"])]
    outs = [_tiled_out(t, RW_WIDTH, tile)] * 8
    r, decay, k_h, v, a, b, gate, z = _fused(pre, ins, outs, (t // tile,), "rwkv_pre")

    y, gathered, token = _rwkv_scan(r, decay, k_h, v, a, b, wsrc, gslot)

    conv_w = wts["sc_conv_w"][0]
    r_k = wts["rw_r_k"].reshape(1, RW_WIDTH)

    def post(y, r, k_h, v, gate, bg, z, z1, z2, lnx_g, lnx_b, r_k, c0, c1, c2):
        mean = _head_sum(y) * (1.0 / HEAD_DIM)
        yc = y - mean
        var = _head_sum(yc * yc) * (1.0 / HEAD_DIM)
        yn = yc * lax.rsqrt(var + RW_GN_EPS) * lnx_g + lnx_b
        bonus = _head_sum(r * k_h * r_k) * v
        return (yn + bonus) * gate, bg * (c0 * z2 + c1 * z1 + c2 * z)

    ins = [_tiled(a_, tile) for a_ in (y, r, k_h, v, gate, pp["bg"])]
    ins += [_tiled(a_, tile) for a_ in (z, _shift_rows(z, 1), _shift_rows(z, 2))]
    ins += [_shared(a_) for a_ in (wts["rw_lnx_g"], wts["rw_lnx_b"], r_k, conv_w[0:1], conv_w[1:2], conv_w[2:3])]
    y_a, y_b = _fused(post, ins, [_tiled_out(t, RW_WIDTH, tile)] * 2, (t // tile,), "rwkv_post")
    out = _linear(jnp.concatenate([y_a, y_b], axis=1), big["ab_w_out"][0], wts["ab_w_out"][0], "ab_out")
    return out, gathered, token


def _t5_bucket_np(dist):
    exact = N_BUCKETS // 2
    logd = np.log(np.maximum(dist, 1).astype(np.float32) / exact) / math.log(MAX_DISTANCE / exact)
    large = np.minimum(exact + (logd * (N_BUCKETS - exact)).astype(np.int32), N_BUCKETS - 1)
    return np.where(dist < exact, dist, large)


def _merge_groups(os_, lses, name):
    t = os_[0].shape[0]
    tile = ROW_TILE

    def fn(o0, o1, o2, l0, l1, l2):
        lane = lax.broadcasted_iota(jnp.int32, (1, LANES), 1)
        lo = lane < HEAD_DIM
        ls = [jnp.where(lo, l[0], l[1]) for l in (l0, l1, l2)]
        m = jnp.maximum(jnp.maximum(ls[0], ls[1]), ls[2])
        es = [jnp.exp(l - m) for l in ls]
        den = es[0] + es[1] + es[2]
        return ((es[0] * o0 + es[1] * o1 + es[2] * o2) / den,)

    ins = [(o, (tile, LANES), lambda i, hp: (i, hp), "t") for o in os_]
    ins += [(l, (2, tile, LANES), lambda i, hp: (hp, i, 0), "t") for l in lses]
    outs = [((t, DIL_WIDTH), (tile, LANES), lambda i, hp: (i, hp), "t")]
    return _fused(fn, ins, outs, (t // tile, N_PAIRS), name)[0]


def _residue_major(a, dil, axis=0):
    if dil == 1:
        return a
    shp = a.shape
    split = a.reshape(shp[:axis] + (shp[axis] // dil, dil) + shp[axis + 1:])
    return jnp.swapaxes(split, axis, axis + 1).reshape(shp)


def _position_major(a, dil, axis=0):
    if dil == 1:
        return a
    shp = a.shape
    split = a.reshape(shp[:axis] + (dil, shp[axis] // dil) + shp[axis + 1:])
    return jnp.swapaxes(split, axis, axis + 1).reshape(shp)


def _dilated_mixer(u, big, wts):
    group_cols = 3 * DIL_WIDTH
    ps = []
    for g, (_, dil) in enumerate(DIL_PATTERNS):
        cols = slice(g * group_cols, (g + 1) * group_cols)
        ps.append(_linear(_residue_major(u, dil), big["dil_w_qkv"][0][:, cols], wts["dil_w_qkv"][0][:, cols],
                          f"dil_qkv{g}"))
    qi = np.arange(BLOCK)[:, None]
    ki = np.arange(2 * BLOCK)[None, :]
    rel = BLOCK + qi - ki
    biases = []
    for g, (window, dil) in enumerate(DIL_PATTERNS):
        span = window // dil
        bucket = _t5_bucket_np(np.clip(rel, 0, span) * dil).reshape(-1)
        onehot = jnp.asarray(np.eye(N_BUCKETS, dtype=np.float32)[bucket])
        table = wts["rel_bias"][:, g * N_HEADS:(g + 1) * N_HEADS]
        bias = jnp.dot(onehot, table, precision=lax.Precision.HIGHEST)
        biases.append(jnp.transpose(bias.reshape(BLOCK, 2 * BLOCK, N_HEADS), (2, 0, 1)))
    os_, lses = _dilated_attention(tuple(ps), tuple(biases))
    os_ = [_position_major(o, dil) for o, (_, dil) in zip(os_, DIL_PATTERNS)]
    lses = [_position_major(l, dil, axis=1) for l, (_, dil) in zip(lses, DIL_PATTERNS)]
    o = _merge_groups(os_, lses, "dil_merge")
    return _linear(o, big["dil_w_out"][0], wts["dil_w_out"][0], "dil_out")


def _forward_local(x, mods, big, wts, wsrc, gslot, late_shapes, target):
    u = _modulate(x, mods[0, 1], mods[0, 0], "mod_in")
    for i in range(DEPTH):
        sh2, sc2, g1, g2 = mods[i, 3], mods[i, 4], mods[i, 2], mods[i, 5]
        if i == 0:
            y, gathered, token = _rwkv_shortconv(u, big, wts, wsrc, gslot)
            shard_shapes = [s[:SHARDED[n]] + (s[SHARDED[n]] // N_CHIPS,) + s[SHARDED[n] + 1:]
                            for n, s in late_shapes.items()]
            parts = _unpack_chips(gathered, shard_shapes, own=wsrc)
            big = {**big, **{n: _join_chips(p, SHARDED[n]) for n, p in zip(late_shapes, parts)}}
            wts = {**wts, **_gradient_slots(token, late_shapes, tuple(late_shapes))}
        else:
            y = _dilated_mixer(u, big, wts)
        x, u = _resid_ln_mod(x, y, g1, wts["ln_g"][i, 0:1], wts["ln_b"][i, 0:1], sc2, sh2, f"ln_mix{i}")
        y = _mlp(u, big["mlp_w1"][i], wts["mlp_w1"][i], big["mlp_w2"][i], wts["mlp_w2"][i], f"mlp{i}")
        if i + 1 < DEPTH:
            x, u = _resid_ln_mod(x, y, g2, wts["ln_g"][i, 1:2], wts["ln_b"][i, 1:2],
                                 mods[i + 1, 1], mods[i + 1, 0], f"ln_mlp{i}")
        else:
            return _resid_ln_loss(x, y, g2, wts["ln_g"][i, 1:2], wts["ln_b"][i, 1:2], target, "ln_loss")


SHARDED = {"ab_w_in": 2, "ab_w_out": 1, "dil_w_qkv": 2, "dil_w_out": 2, "mlp_w1": 2, "mlp_w2": 1,
           "ln_g": 2, "ln_b": 2, "rw_w_up": 2, "rw_a_up": 2, "rw_g_up": 2, "sc_conv_w": 2}
FIRST_MIXER = ("ab_w_in", "ab_w_out")
LATER_LAYERS = ("dil_w_qkv", "dil_w_out", "mlp_w1", "mlp_w2")
SMALL_SHARDED = ("ln_g", "ln_b", "rw_w_up", "rw_a_up", "rw_g_up", "sc_conv_w")
REPLICATED = ("ada_b", "rw_mu", "rw_w0", "rw_a0", "rw_k_k", "rw_k_a", "rw_r_k", "rw_lnx_g", "rw_lnx_b", "rel_bias")
WEIGHT_ORDER = ("ada_w", "ada_b", "ln_g", "ln_b", "ab_w_in", "rw_mu", "rw_w0", "rw_w_up", "rw_a0", "rw_a_up",
                "rw_g_up", "rw_k_k", "rw_k_a", "rw_r_k", "rw_lnx_g", "rw_lnx_b", "sc_conv_w", "ab_w_out",
                "dil_w_qkv", "dil_w_out", "rel_bias", "mlp_w1", "mlp_w2")


PACK_ROWS = 16


def _rows_of(n_elems):
    return -(-n_elems // (ROW_W * PACK_ROWS)) * PACK_ROWS


def _to_rows(a):
    flat = a.reshape(-1)
    rows = _rows_of(flat.shape[0])
    if rows * ROW_W != flat.shape[0]:
        flat = jnp.pad(flat, (0, rows * ROW_W - flat.shape[0]))
    return flat.reshape(rows, ROW_W)


def _from_rows(rows, shape):
    n = int(np.prod(shape))
    return rows.reshape(-1)[:n].reshape(shape)


def _split_chips(full, axis):
    shp = full.shape
    parts = full.reshape(shp[:axis] + (N_CHIPS, shp[axis] // N_CHIPS) + shp[axis + 1:])
    return jnp.moveaxis(parts, axis, 0)


def _join_chips(parts, axis):
    moved = jnp.moveaxis(parts, 0, axis)
    shp = moved.shape
    return moved.reshape(shp[:axis] + (shp[axis] * shp[axis + 1],) + shp[axis + 2:])


def _pack_rows(arrays, row_multiple=256):
    blocks = [_to_rows(a) for a in arrays]
    total = sum(b.shape[0] for b in blocks)
    pad = (-total) % row_multiple
    if pad:
        blocks.append(jnp.zeros((pad, ROW_W), blocks[0].dtype))
    return jnp.concatenate(blocks, axis=0)


def _unpack_rows(buf, shapes):
    out, r0 = [], 0
    for shp in shapes:
        n = _rows_of(int(np.prod(shp)))
        out.append(_from_rows(buf[r0:r0 + n], shp))
        r0 += n
    return out


def _pack_chips(parts):
    blocks = []
    for p in parts:
        flat = p.reshape(N_CHIPS, -1)
        rows = _rows_of(flat.shape[1])
        if rows * ROW_W != flat.shape[1]:
            flat = jnp.pad(flat, ((0, 0), (0, rows * ROW_W - flat.shape[1])))
        blocks.append(flat.reshape(N_CHIPS, rows, ROW_W))
    total = sum(b.shape[1] for b in blocks)
    pad = (-total) % 256
    if pad:
        blocks.append(jnp.zeros((N_CHIPS, pad, ROW_W), blocks[0].dtype))
    return jnp.concatenate(blocks, axis=1)


def _unpack_chips(buf, shapes, own=None):
    mine = lax.broadcasted_iota(jnp.int32, (N_CHIPS, 1, 1), 0) == _chip_of(_me()) if own is not None else None
    out, r0 = [], 0
    for shp in shapes:
        size = int(np.prod(shp))
        n = _rows_of(size)
        rows = buf[:, r0:r0 + n]
        if own is not None:
            rows = jnp.where(mine, own[None, r0:r0 + n], rows)
        out.append(rows.reshape(N_CHIPS, -1)[:, :size].reshape((N_CHIPS,) + tuple(shp)))
        r0 += n
    return out


def _gradient_slots(token, full_shapes, names):
    @jax.custom_vjp
    def route(token):
        return {n: jnp.zeros(full_shapes[n], F32) for n in names}

    def fwd(token):
        return {n: jnp.zeros(full_shapes[n], F32) for n in names}, None

    def bwd(_, d):
        return (_pack_chips([_split_chips(d[n], SHARDED[n]).astype(BF16) for n in names]),)

    route.defvjp(fwd, bwd)
    return route(token)


def _as2d(a):
    return a.reshape(-1, a.shape[-1])


def kernel(x, c, ada_w, ada_b, ln_g, ln_b, ab_w_in, rw_mu, rw_w0, rw_w_up, rw_a0, rw_a_up, rw_g_up, rw_k_k, rw_k_a, rw_r_k, rw_lnx_g, rw_lnx_b, sc_conv_w, ab_w_out, dil_w_qkv, dil_w_out, rel_bias, mlp_w1, mlp_w2, loss_target, m_ada_w, m_ada_b, m_ln_g, m_ln_b, m_ab_w_in, m_rw_mu, m_rw_w0, m_rw_w_up, m_rw_a0, m_rw_a_up, m_rw_g_up, m_rw_k_k, m_rw_k_a, m_rw_r_k, m_rw_lnx_g, m_rw_lnx_b, m_sc_conv_w, m_ab_w_out, m_dil_w_qkv, m_dil_w_out, m_rel_bias, m_mlp_w1, m_mlp_w2, v_ada_w, v_ada_b, v_ln_g, v_ln_b, v_ab_w_in, v_rw_mu, v_rw_w0, v_rw_w_up, v_rw_a0, v_rw_a_up, v_rw_g_up, v_rw_k_k, v_rw_k_a, v_rw_r_k, v_rw_lnx_g, v_rw_lnx_b, v_sc_conv_w, v_ab_w_out, v_dil_w_qkv, v_dil_w_out, v_rel_bias, v_mlp_w1, v_mlp_w2):
    args = dict(locals())
    w_in = {n: args[n] for n in WEIGHT_ORDER}
    m_in = {n: args["m_" + n] for n in WEIGHT_ORDER}
    v_in = {n: args["v_" + n] for n in WEIGHT_ORDER}
    me = _me()
    chip = _chip_of(me)
    dev = _dev_of(me)

    c_all = _dev_all_gather(c, "gather_c")[:, 0, :]
    n_col = ada_w.shape[2]
    ada_b_cols = lax.dynamic_slice_in_dim(ada_b, chip * n_col, n_col, axis=1)[:, None, :]
    mod_cols = _ada_fwd(c_all, ada_w, ada_b_cols)

    first_buf = _pack_rows([w_in[n].astype(BF16) for n in FIRST_MIXER])
    first_all = _two_level_gather(first_buf, "gather_first")
    late_buf = _pack_rows([w_in[n].astype(BF16) for n in LATER_LAYERS])
    small_buf = _pack_rows([mod_cols] + [w_in[n] for n in SMALL_SHARDED], row_multiple=PACK_ROWS)
    small_all = _chip_all_gather(small_buf, "gather_small")

    def full_shape(n):
        shp = w_in[n].shape
        return shp[:SHARDED[n]] + (shp[SHARDED[n]] * N_CHIPS,) + shp[SHARDED[n] + 1:]

    wts = {n: w_in[n] for n in REPLICATED}
    big = {}
    for n, part in zip(FIRST_MIXER, _unpack_chips(first_all, [w_in[n].shape for n in FIRST_MIXER])):
        big[n] = _join_chips(part, SHARDED[n])
        wts[n] = jnp.zeros(full_shape(n), F32)
    small_parts = _unpack_chips(small_all, [mod_cols.shape] + [w_in[n].shape for n in SMALL_SHARDED])
    for n, part in zip(SMALL_SHARDED, small_parts[1:]):
        wts[n] = _join_chips(part, SHARDED[n])
    mod_all = _join_chips(small_parts[0], 2)
    mods = lax.dynamic_slice_in_dim(mod_all, dev, 1, axis=1).reshape(DEPTH, 6, 1, D_MODEL)
    late_shapes = {n: full_shape(n) for n in LATER_LAYERS}
    late_slot = jnp.zeros(late_buf.shape, F32)

    def local_loss(xv, modv, wv, slot):
        return _forward_local(xv, modv, big, wv, late_buf, slot, late_shapes, loss_target[0])[0, 0]

    loss_local, (grad_x, dmods, dw, late_part) = jax.value_and_grad(local_loss, argnums=(0, 1, 2, 3))(
        x[0], mods, wts, late_slot)
    loss = lax.psum(loss_local, ("x", "y", "c"))

    small_row = jnp.concatenate([dmods.reshape(-1)] + [dw[n].reshape(-1) for n in REPLICATED[1:]])
    n_small = small_row.shape[0]
    n_small_pad = -(-n_small // LANES) * LANES
    small_row = jnp.pad(small_row, (0, n_small_pad - n_small))[None, :]
    rows_all = _dev_all_gather(small_row, "gather_small_grads")
    small_sum = _sum_slots(rows_all, "sum_small_grads")

    dmod_all = rows_all[:, 0, :DEPTH * 6 * D_MODEL].reshape(N_DEV, DEPTH, 6 * D_MODEL)
    dmod_cols = lax.dynamic_slice_in_dim(dmod_all, chip * n_col, n_col, axis=2)
    dmod_cols = jnp.pad(jnp.moveaxis(dmod_cols, 0, 1), ((0, 0), (0, LANES - N_DEV), (0, 0)))
    c_all_t = jnp.pad(c_all.T, ((0, 0), (0, LANES - N_DEV)))
    grads = {"ada_w": _ada_grad(c_all_t, dmod_cols)}
    grads["ada_b"] = small_sum[0, :DEPTH * 6 * D_MODEL].reshape(ada_b.shape)
    r0 = DEPTH * 6 * D_MODEL
    for n in REPLICATED[1:]:
        size = int(np.prod(w_in[n].shape))
        grads[n] = small_sum[0, r0:r0 + size].reshape(w_in[n].shape)
        r0 += size

    sharded_names = FIRST_MIXER + SMALL_SHARDED
    send = _pack_chips([_split_chips(dw[n], SHARDED[n]).astype(BF16) for n in sharded_names])
    n_rows = send.shape[1]
    send = send.reshape(N_CHIPS, 2, n_rows // 2, ROW_W)
    theirs = _core_halves(send, "swap_halves")
    mine = lax.dynamic_index_in_dim(send, me[2], 1, keepdims=False)
    chip_part = _add_pairs(mine, theirs, "sum_cores")
    recv = _chip_scatter(chip_part, "scatter_grads")
    half_sum = _sum_slots(recv, "sum_chips")
    g_rows = _core_all_gather(half_sum, "gather_halves").reshape(n_rows, ROW_W)
    for n, g in zip(sharded_names, _unpack_rows(g_rows, [w_in[n].shape for n in sharded_names])):
        grads[n] = g

    late_rows = _sum_slots(_core_all_gather(late_part, "swap_late"), "sum_cores_late")
    for n, g in zip(LATER_LAYERS, _unpack_rows(late_rows, [w_in[n].shape for n in LATER_LAYERS])):
        grads[n] = g

    deltas, new_m, new_v = {}, {}, {}
    for n in WEIGHT_ORDER:
        shp = w_in[n].shape
        d, nm, nv = _adamw(_as2d(w_in[n]), _as2d(grads[n]), _as2d(m_in[n]), _as2d(v_in[n]), "adamw_" + n)
        deltas[n], new_m[n], new_v[n] = d.reshape(shp), nm.reshape(shp), nv.reshape(shp)

    return (loss, grad_x[None], *[grads[n] for n in WEIGHT_ORDER], *[deltas[n] for n in WEIGHT_ORDER],
            *[new_m[n] for n in WEIGHT_ORDER], *[new_v[n] for n in WEIGHT_ORDER])
```

```python
import functools
import math

import numpy as np
import jax
import jax.numpy as jnp
from jax import lax
from jax.experimental import pallas as pl
from jax.experimental.pallas import tpu as pltpu

F32 = jnp.float32
BF16 = jnp.bfloat16
MESH = pl.DeviceIdType.MESH

D_MODEL = 1024
DEPTH = 2
RW_WIDTH = 512
HEAD_DIM = 64
N_HEADS = 8
RW_DECAY_RANK = 64
RW_ICLR_RANK = 64
RW_GATE_RANK = 160
RW_GN_EPS = 64e-5
RW_PROJ = 3 * RW_WIDTH + RW_DECAY_RANK + RW_ICLR_RANK + RW_GATE_RANK
SC_WIDTH = 512
AB_PROJ = RW_PROJ + 3 * SC_WIDTH
DIL_PATTERNS = ((128, 1), (512, 4), (2048, 16))
N_GROUPS = 3
DIL_WIDTH = 512
DIL_PROJ = N_GROUPS * 3 * DIL_WIDTH
BLOCK = 128
N_BUCKETS = 32
MAX_DISTANCE = 2048
D_FF = 4 * D_MODEL
ALPHA = (2 * DEPTH) ** 0.25
LN_EPS = 1e-5
ADAM_LR = 0.001
ADAM_B1 = 0.9
ADAM_B2 = 0.999
ADAM_EPS = 1e-08
ADAM_WD = 0.01
ADAM_STEP = 10

N_CHIPS = 4
N_DEV = 8
LANES = 128
ROW_W = 1024
SCAN_CHUNK = 16
VMEM_LIMIT = 48 * 1024 * 1024
NEG_BIG = -1e30


def _pick(n, cands):
    for c in cands:
        if n % c == 0:
            return c
    return n


def _cparams(sem=None, vmem=None):
    return pltpu.CompilerParams(dimension_semantics=sem, vmem_limit_bytes=vmem)


_DOT_DIMS = {
    "nn": (((1,), (0,)), ((), ())),
    "nt": (((1,), (1,)), ((), ())),
    "tn": (((0,), (0,)), ((), ())),
}


def _mm(a, b, mode, name, square_relu=False, pre_act=None):
    if mode == "nn":
        (m, k), (_, n) = a.shape, b.shape
    elif mode == "nt":
        (m, k), (n, _) = a.shape, b.shape
    else:
        (k, m), (_, n) = a.shape, b.shape
    tm = _pick(m, (1024, 512, 256, 128))
    tn = _pick(n, (1024, 768, 512, 384, 256, 128))
    wide_k = mode != "tn" and pre_act is None
    tk = _pick(k, (2048, 1024, 512, 256, 128) if wide_k else (1024, 512, 256, 128))
    nk = k // tk
    if mode == "tn":
        a_spec = pl.BlockSpec((tk, tm), lambda i, j, kk: (kk, i))
    else:
        a_spec = pl.BlockSpec((tm, tk), lambda i, j, kk: (i, kk))
    if mode == "nt":
        b_spec = pl.BlockSpec((tn, tk), lambda i, j, kk: (j, kk))
    else:
        b_spec = pl.BlockSpec((tk, tn), lambda i, j, kk: (kk, j))
    dims = _DOT_DIMS[mode]

    out_spec = pl.BlockSpec((tm, tn), lambda i, j, kk: (i, j))

    def body(*refs):
        a_ref, b_ref = refs[:2]
        h_ref = refs[2] if pre_act is not None else None
        o_ref = refs[3] if pre_act is not None else refs[2]
        act_ref = refs[-1] if square_relu else None
        part = lax.dot_general(a_ref[...].astype(BF16), b_ref[...].astype(BF16), dims, preferred_element_type=F32)

        def finish(total):
            if h_ref is not None:
                total = total * (2.0 * jnp.maximum(h_ref[...], 0.0))
            o_ref[...] = total
            if act_ref is not None:
                act_ref[...] = jnp.square(jnp.maximum(total, 0.0)).astype(BF16)

        if nk == 1:
            finish(part)
        else:
            kk = pl.program_id(2)

            @pl.when(kk == 0)
            def _():
                o_ref[...] = part

            @pl.when(jnp.logical_and(kk > 0, kk < nk - 1))
            def _():
                o_ref[...] += part

            @pl.when(kk == nk - 1)
            def _():
                finish(o_ref[...] + part)

    operands = [a, b] + ([pre_act] if pre_act is not None else [])
    in_specs = [a_spec, b_spec] + ([out_spec] if pre_act is not None else [])
    out_shape = [jax.ShapeDtypeStruct((m, n), F32)] + ([jax.ShapeDtypeStruct((m, n), BF16)] if square_relu else [])
    outs = pl.pallas_call(
        body, name=name, grid=(m // tm, n // tn, nk),
        in_specs=in_specs, out_specs=[out_spec] * len(out_shape), out_shape=out_shape,
        compiler_params=_cparams(("parallel", "parallel", "arbitrary"), VMEM_LIMIT),
    )(*operands)
    return tuple(outs) if square_relu else outs[0]


def _linear(x, w, slot, name):
    @jax.custom_vjp
    def op(x, w, slot):
        return _mm(x, w, "nn", name + "_fwd")

    def fwd(x, w, slot):
        return _mm(x, w, "nn", name + "_fwd"), (x, w)

    def bwd(res, dy):
        x, w = res
        return _mm(dy, w, "nt", name + "_dx"), jnp.zeros_like(w), _mm(x, dy, "tn", name + "_dw")

    op.defvjp(fwd, bwd)
    return op(x, w, slot)


def _const_map(ndim):
    return lambda *g: (0,) * ndim


def _first_step(n_grid):
    return functools.reduce(jnp.logical_and, [pl.program_id(d) == 0 for d in range(n_grid)])


def _fused(fn, ins, outs, grid, name):
    arrays = [i[0] for i in ins]
    n_in, n_out, n_grid = len(ins), len(outs), len(grid)
    in_specs = [pl.BlockSpec(bs, im) for (_, bs, im, _) in ins]
    out_specs = [pl.BlockSpec(bs, im) for (_, bs, im, _) in outs]
    out_shapes = [jax.ShapeDtypeStruct(s, F32) for (s, _, _, _) in outs]
    sem = ("arbitrary",) * n_grid

    def fwd_call(*xs):
        def body(*refs):
            vals = [r[...] for r in refs[:n_in]]
            ys = fn(*vals)
            first = _first_step(n_grid)
            for o_ref, y, (_, _, _, kind) in zip(refs[n_in:], ys, outs):
                if kind == "t":
                    o_ref[...] = y
                else:
                    @pl.when(first)
                    def _(o_ref=o_ref):
                        o_ref[...] = jnp.zeros_like(o_ref)

                    o_ref[...] += y

        return pl.pallas_call(
            body, name=name + "_fwd", grid=grid, in_specs=in_specs, out_specs=out_specs,
            out_shape=out_shapes, compiler_params=_cparams(sem, VMEM_LIMIT))(*xs)

    def bwd_call(xs, dys):
        d_specs = [pl.BlockSpec(bs, im) for (_, bs, im, _) in outs]
        g_specs = [pl.BlockSpec(bs, im) for (_, bs, im, _) in ins]
        g_shapes = [jax.ShapeDtypeStruct(a.shape, F32) for a in arrays]

        def body(*refs):
            vals = [r[...] for r in refs[:n_in]]
            dvals = tuple(r[...] for r in refs[n_in:n_in + n_out])
            _, vjp = jax.vjp(lambda *v: tuple(fn(*v)), *vals)
            gs = vjp(dvals)
            first = _first_step(n_grid)
            for g_ref, g, (_, _, _, kind) in zip(refs[n_in + n_out:], gs, ins):
                if kind == "t":
                    g_ref[...] = g
                else:
                    @pl.when(first)
                    def _(g_ref=g_ref):
                        g_ref[...] = jnp.zeros_like(g_ref)

                    g_ref[...] += g

        return pl.pallas_call(
            body, name=name + "_bwd", grid=grid, in_specs=in_specs + d_specs, out_specs=g_specs,
            out_shape=g_shapes, compiler_params=_cparams(sem, VMEM_LIMIT))(*xs, *dys)

    @jax.custom_vjp
    def op(*xs):
        return tuple(fwd_call(*xs))

    def op_fwd(*xs):
        return tuple(fwd_call(*xs)), xs

    def op_bwd(xs, dys):
        return tuple(bwd_call(xs, dys))

    op.defvjp(op_fwd, op_bwd)
    return op(*arrays)


def _tiled(a, tile, cols=None, col_block=0):
    cols = a.shape[1] if cols is None else cols
    return (a, (tile, cols), lambda i, cb=col_block: (i, cb), "t")


def _shared(a):
    return (a, a.shape, _const_map(a.ndim), "b")


def _tiled_out(rows, cols, tile):
    return ((rows, cols), (tile, cols), lambda i: (i, 0), "t")


@jax.custom_vjp
def _bdot(x, w):
    return jnp.dot(x.astype(BF16), w.astype(BF16), preferred_element_type=F32)


def _bdot_fwd(x, w):
    return _bdot(x, w), (x, w)


def _bdot_bwd(res, dy):
    x, w = res
    dyb = dy.astype(BF16)
    dx = lax.dot_general(dyb, w.astype(BF16), _DOT_DIMS["nt"], preferred_element_type=F32)
    dw = lax.dot_general(x.astype(BF16), dyb, _DOT_DIMS["tn"], preferred_element_type=F32)
    return dx, dw


_bdot.defvjp(_bdot_fwd, _bdot_bwd)


def _head_sum(x):
    n = x.shape[-1]
    hi = lax.broadcasted_iota(jnp.int32, (n, n), 0) // HEAD_DIM
    hj = lax.broadcasted_iota(jnp.int32, (n, n), 1) // HEAD_DIM
    e = (hi == hj).astype(F32)
    return jnp.dot(x, e, precision=lax.Precision.HIGHEST, preferred_element_type=F32)


def _softplus(x):
    return jnp.maximum(x, 0.0) + jnp.log1p(jnp.exp(-jnp.abs(x)))


def _layer_norm_rows(z, g, b):
    mu = jnp.mean(z, axis=-1, keepdims=True)
    zc = z - mu
    var = jnp.mean(zc * zc, axis=-1, keepdims=True)
    return zc * lax.rsqrt(var + LN_EPS) * g + b


N_PAIRS = N_HEADS // 2


def _scan_consts():
    k = lax.broadcasted_iota(jnp.int32, (HEAD_DIM, LANES), 0)
    j = lax.broadcasted_iota(jnp.int32, (HEAD_DIM, LANES), 1)
    diag = ((j % HEAD_DIM) == k).astype(F32)
    jj = lax.broadcasted_iota(jnp.int32, (LANES, LANES), 0) // HEAD_DIM
    ll = lax.broadcasted_iota(jnp.int32, (LANES, LANES), 1) // HEAD_DIM
    same_head = (jj == ll).astype(BF16)
    return diag, same_head


def _unrolled(n, body, carry):
    for i in range(n):
        carry = body(i, carry)
    return carry


def _fill_cols(srcs, col_ref, n_steps):
    assert n_steps == 16
    blocks = []
    for src in srcs:
        x = src[...]
        hi = x.astype(BF16).astype(F32)
        r1 = x - hi
        mid = r1.astype(BF16).astype(F32)
        x48 = jnp.concatenate([hi, mid, r1 - mid], axis=0)
        for hp in range(N_PAIRS):
            xp = x48[:, hp * LANES:(hp + 1) * LANES]
            y = jnp.concatenate([xp, pltpu.roll(xp, HEAD_DIM, 1), jnp.zeros((32, LANES), F32)], axis=0)
            blocks.append(y.T[:HEAD_DIM].astype(BF16))
    lhs = jnp.concatenate(blocks, axis=0)
    j = lax.broadcasted_iota(jnp.int32, (LANES, LANES), 0)
    lane_head = lax.broadcasted_iota(jnp.int32, (LANES, LANES), 1) // HEAD_DIM
    for t in range(n_steps):
        pick = jnp.logical_and(j < 96, jnp.logical_and(j % 16 == t, j // 48 == lane_head))
        out = jnp.dot(lhs, pick.astype(BF16), preferred_element_type=F32)
        for vi in range(len(srcs)):
            for hp in range(N_PAIRS):
                r0 = (vi * N_PAIRS + hp) * HEAD_DIM
                col_ref[vi, hp, t] = out[r0:r0 + HEAD_DIM]


def _scan_fwd_call(r, w, k, v, a, b, wsrc):
    t_len = r.shape[0]
    ch = SCAN_CHUNK
    n_ch = t_len // ch
    w_rows, w_cols = wsrc.shape
    half = w_rows // 2
    chunks = _n_chunks(half)
    rc = half // chunks
    transfers = [(q, kk) for q in range(chunks) for kk in range(len(CHIP_FLIPS))]
    hand_on_step = [max(1, min(n_ch - 1, (i + 1) * (n_ch - 8) // len(transfers) + 3)) for i in range(len(transfers))]

    def body(r_ref, w_ref, k_ref, v_ref, a_ref, b_ref, src_ref, y_ref, ck_ref, g_ref,
             st_ref, col_ref, send1, recv1, send2, recv2):
        c = pl.program_id(0)
        me = _me()
        core = me[2]
        sibling = _flip(me, CORE_FLIP)

        def over_ici(q, kk):
            return pltpu.make_async_remote_copy(
                src_ref=src_ref.at[core, pl.ds(q * rc, rc)], dst_ref=g_ref.at[_chip_of(me), core, pl.ds(q * rc, rc)],
                send_sem=send1.at[kk * chunks + q], recv_sem=recv1.at[kk * chunks + q],
                device_id=_flip(me, CHIP_FLIPS[kk]), device_id_type=MESH)

        def hand_on(q, kk):
            landed = g_ref.at[_chip_of(_flip(me, CHIP_FLIPS[kk])), core, pl.ds(q * rc, rc)]
            return pltpu.make_async_remote_copy(
                src_ref=landed, dst_ref=landed, send_sem=send2.at[kk * chunks + q], recv_sem=recv2.at[kk * chunks + q],
                device_id=sibling, device_id_type=MESH)

        @pl.when(c == 0)
        def _():
            st_ref[...] = jnp.zeros_like(st_ref)
            for q, kk in transfers:
                over_ici(q, kk).start()

        for step in sorted(set(hand_on_step)):
            @pl.when(c == step)
            def _(step=step):
                for (q, kk), s in zip(transfers, hand_on_step):
                    if s == step:
                        over_ici(q, kk).wait_recv()
                        hand_on(q, kk).start()

        ck_ref[0] = st_ref[...]
        _fill_cols((w_ref, a_ref, b_ref, k_ref, r_ref), col_ref, ch)

        def step(t, states):
            new = []
            for hp in range(N_PAIRS):
                lanes = slice(hp * LANES, (hp + 1) * LANES)
                s = states[hp]
                sa = jnp.sum(s * col_ref[1, hp, t], axis=0, keepdims=True)
                s = s * col_ref[0, hp, t] + col_ref[2, hp, t] * sa + col_ref[3, hp, t] * v_ref[pl.ds(t, 1), lanes]
                y_ref[pl.ds(t, 1), lanes] = jnp.sum(s * col_ref[4, hp, t], axis=0, keepdims=True)
                new.append(s)
            return tuple(new)

        states = _unrolled(ch, step, tuple(st_ref[hp] for hp in range(N_PAIRS)))
        for hp in range(N_PAIRS):
            st_ref[hp] = states[hp]

        @pl.when(c == n_ch - 1)
        def _():
            for q, kk in transfers:
                hand_on(q, kk).wait_recv()
            for q, kk in transfers:
                over_ici(q, kk).wait_send()
                hand_on(q, kk).wait_send()

    row_spec = pl.BlockSpec((ch, RW_WIDTH), lambda c: (c, 0))
    any_spec = pl.BlockSpec(memory_space=pl.ANY)
    n_sem = len(transfers)
    y, ck, got = pl.pallas_call(
        body, name="rwkv_scan_fwd", grid=(n_ch,),
        in_specs=[row_spec] * 6 + [any_spec],
        out_specs=[row_spec, pl.BlockSpec((1, N_PAIRS, HEAD_DIM, LANES), lambda c: (c, 0, 0, 0)), any_spec],
        out_shape=[jax.ShapeDtypeStruct((t_len, RW_WIDTH), F32),
                   jax.ShapeDtypeStruct((n_ch, N_PAIRS, HEAD_DIM, LANES), F32),
                   jax.ShapeDtypeStruct((N_CHIPS, 2, half, w_cols), wsrc.dtype)],
        scratch_shapes=[pltpu.VMEM((N_PAIRS, HEAD_DIM, LANES), F32),
                        pltpu.VMEM((5, N_PAIRS, ch, HEAD_DIM, LANES), F32)]
        + [pltpu.SemaphoreType.DMA((n_sem,)) for _ in range(4)],
        compiler_params=_cparams(("arbitrary",), VMEM_LIMIT),
    )(r, w, k, v, a, b, wsrc.reshape(2, half, w_cols))
    return y, ck, got.reshape(N_CHIPS, w_rows, w_cols)


def _scan_bwd_call(r, w, k, v, a, b, ck, dy, gsend):
    t_len = r.shape[0]
    ch = SCAN_CHUNK
    n_ch = t_len // ch
    _, g_rows, g_cols = gsend.shape
    chunks = _n_chunks(g_rows)
    rc = g_rows // chunks
    transfers = [(q, kk) for q in range(chunks) for kk in range(len(CHIP_FLIPS))]

    def body(r_ref, w_ref, k_ref, v_ref, a_ref, b_ref, ck_ref, dy_ref, gs_ref,
             dr_ref, dw_ref, dk_ref, dv_ref, da_ref, db_ref, gr_ref,
             ds_ref, col_ref, sp_ref, sa_ref, send_sems, recv_sems):
        c = pl.program_id(0)
        me = _me()

        def to_chip(q, kk):
            peer = _flip(me, CHIP_FLIPS[kk])
            return pltpu.make_async_remote_copy(
                src_ref=gs_ref.at[_chip_of(peer), pl.ds(q * rc, rc)], dst_ref=gr_ref.at[_chip_of(me), pl.ds(q * rc, rc)],
                send_sem=send_sems.at[kk * chunks + q], recv_sem=recv_sems.at[kk * chunks + q],
                device_id=peer, device_id_type=MESH)

        @pl.when(c == 0)
        def _():
            ds_ref[...] = jnp.zeros_like(ds_ref)
            for q, kk in transfers:
                to_chip(q, kk).start()

        diag_f, same_head = _scan_consts()
        _fill_cols((w_ref, a_ref, b_ref, k_ref, r_ref), col_ref, ch)

        def replay(t, states):
            new = []
            for hp in range(N_PAIRS):
                lanes = slice(hp * LANES, (hp + 1) * LANES)
                s = states[hp]
                sp_ref[t, hp] = s
                sa = jnp.sum(s * col_ref[1, hp, t], axis=0, keepdims=True)
                sa_ref[pl.ds(t, 1), lanes] = sa
                new.append(s * col_ref[0, hp, t] + col_ref[2, hp, t] * sa
                           + col_ref[3, hp, t] * v_ref[pl.ds(t, 1), lanes])
            return tuple(new)

        _unrolled(ch, replay, tuple(ck_ref[0, hp] for hp in range(N_PAIRS)))

        def key_rows(ps):
            stacked = jnp.concatenate([p.astype(BF16) for p in ps], axis=0)
            q = jnp.dot(stacked, same_head, preferred_element_type=F32)
            return [jnp.sum(q[i * HEAD_DIM:(i + 1) * HEAD_DIM] * diag_f, axis=0, keepdims=True)
                    for i in range(len(ps))]

        def back(i, grads):
            t = ch - 1 - i
            new = []
            for hp in range(N_PAIRS):
                lanes = slice(hp * LANES, (hp + 1) * LANES)
                wc, ac, bc, kc, rc = (col_ref[vi, hp, t] for vi in range(5))
                sp = sp_ref[t, hp]
                sa = sa_ref[pl.ds(t, 1), lanes]
                vrow = v_ref[pl.ds(t, 1), lanes]
                dyrow = dy_ref[pl.ds(t, 1), lanes]
                st = sp * wc + bc * sa + kc * vrow
                g = grads[hp] + rc * dyrow
                dsa = jnp.sum(g * bc, axis=0, keepdims=True)
                dv_ref[pl.ds(t, 1), lanes] = jnp.sum(g * kc, axis=0, keepdims=True)
                rows = key_rows([st * dyrow, g * vrow, g * sa, g * sp, sp * dsa])
                for out_ref, row in zip((dr_ref, dk_ref, db_ref, dw_ref, da_ref), rows):
                    out_ref[pl.ds(t, 1), lanes] = row
                new.append(g * wc + ac * dsa)
            return tuple(new)

        grads = _unrolled(ch, back, tuple(ds_ref[hp] for hp in range(N_PAIRS)))
        for hp in range(N_PAIRS):
            ds_ref[hp] = grads[hp]

        @pl.when(c == n_ch - 1)
        def _():
            for q, kk in transfers:
                to_chip(q, kk).wait()

    row_spec = pl.BlockSpec((ch, RW_WIDTH), lambda c: (n_ch - 1 - c, 0))
    any_spec = pl.BlockSpec(memory_space=pl.ANY)
    out_sds = jax.ShapeDtypeStruct((t_len, RW_WIDTH), F32)
    n_sem = len(transfers)
    return pl.pallas_call(
        body, name="rwkv_scan_bwd", grid=(n_ch,),
        in_specs=[row_spec] * 6 + [pl.BlockSpec((1, N_PAIRS, HEAD_DIM, LANES), lambda c: (n_ch - 1 - c, 0, 0, 0)),
                                   row_spec, any_spec],
        out_specs=[row_spec] * 6 + [any_spec],
        out_shape=[out_sds] * 6 + [jax.ShapeDtypeStruct(gsend.shape, gsend.dtype)],
        scratch_shapes=[pltpu.VMEM((N_PAIRS, HEAD_DIM, LANES), F32),
                        pltpu.VMEM((5, N_PAIRS, ch, HEAD_DIM, LANES), F32),
                        pltpu.VMEM((ch, N_PAIRS, HEAD_DIM, LANES), F32),
                        pltpu.VMEM((ch, RW_WIDTH), F32),
                        pltpu.SemaphoreType.DMA((n_sem,)), pltpu.SemaphoreType.DMA((n_sem,))],
        compiler_params=_cparams(("arbitrary",), VMEM_LIMIT),
    )(r, w, k, v, a, b, ck, dy, gsend)


def _scan_token(wsrc):
    return jnp.zeros((N_CHIPS,) + wsrc.shape, BF16)


@jax.custom_vjp
def _rwkv_scan(r, w, k, v, a, b, wsrc, gslot):
    y, _, got = _scan_fwd_call(r, w, k, v, a, b, wsrc)
    return y, got, _scan_token(wsrc)


def _rwkv_scan_fwd(r, w, k, v, a, b, wsrc, gslot):
    y, ck, got = _scan_fwd_call(r, w, k, v, a, b, wsrc)
    return (y, got, _scan_token(wsrc)), (r, w, k, v, a, b, ck, wsrc)


def _rwkv_scan_bwd(res, cts):
    *saved, wsrc = res
    dy, _, gsend = cts
    *dins, got = _scan_bwd_call(*saved, dy, gsend)
    summed = _sum_slots_own(got, gsend, _chip_of(_me()), "sum_chips_late")
    return (*dins, jnp.zeros_like(wsrc), summed)


_rwkv_scan.defvjp(_rwkv_scan_fwd, _rwkv_scan_bwd)


ATT_SCALE = HEAD_DIM ** -0.5


def _att_masks():
    qi = lax.broadcasted_iota(jnp.int32, (BLOCK, BLOCK), 0)
    ki = lax.broadcasted_iota(jnp.int32, (BLOCK, BLOCK), 1)
    lane = lax.broadcasted_iota(jnp.int32, (1, LANES), 1)
    return ki <= qi, ki >= qi, lane


def _att_fwd_call(p, bias, g, dil):
    t_len = p.shape[0]
    l_len = t_len // dil
    nb = l_len // BLOCK

    def body(q_ref, kc_ref, kp_ref, vc_ref, vp_ref, bias_ref, o_ref, lse_ref):
        n = pl.program_id(1)
        cur_ok, prev_band, lane = _att_masks()
        prev_ok = jnp.logical_and(prev_band, n > 0)
        for hp in range(N_PAIRS):
            lanes = slice(hp * LANES, (hp + 1) * LANES)
            q2 = q_ref[:, lanes].astype(BF16)
            kc = kc_ref[:, lanes].astype(BF16)
            kp = kp_ref[:, lanes].astype(BF16)
            vc = vc_ref[:, lanes].astype(BF16)
            vp = vp_ref[:, lanes].astype(BF16)
            o2 = jnp.zeros((BLOCK, LANES), F32)
            for hh in range(2):
                h = 2 * hp + hh
                mine = (lane // HEAD_DIM) == hh
                qm = jnp.where(mine, q2, jnp.zeros_like(q2))
                s_c = lax.dot_general(qm, kc, _DOT_DIMS["nt"], preferred_element_type=F32) * ATT_SCALE
                s_p = lax.dot_general(qm, kp, _DOT_DIMS["nt"], preferred_element_type=F32) * ATT_SCALE
                s_c = jnp.where(cur_ok, s_c + bias_ref[h, :, BLOCK:], NEG_BIG)
                s_p = jnp.where(prev_ok, s_p + bias_ref[h, :, :BLOCK], NEG_BIG)
                m = jnp.maximum(jnp.max(s_c, axis=-1, keepdims=True), jnp.max(s_p, axis=-1, keepdims=True))
                e_c = jnp.exp(s_c - m)
                e_p = jnp.exp(s_p - m)
                den = jnp.sum(e_c, axis=-1, keepdims=True) + jnp.sum(e_p, axis=-1, keepdims=True)
                o_h = (jnp.dot((e_c / den).astype(BF16), vc, preferred_element_type=F32)
                       + jnp.dot((e_p / den).astype(BF16), vp, preferred_element_type=F32))
                o2 = o2 + jnp.where(mine, o_h, 0.0)
                lse_ref[h] = jnp.broadcast_to(m + jnp.log(den), (BLOCK, LANES))
            o_ref[:, lanes] = o2

    def col(j):
        return lambda r, n: (r * nb + n, j)

    def col_prev(j):
        return lambda r, n: (r * nb + jnp.maximum(n - 1, 0), j)

    blk = (BLOCK, DIL_WIDTH)
    return pl.pallas_call(
        body, name=f"dil_att_fwd_g{g}", grid=(dil, nb),
        in_specs=[pl.BlockSpec(blk, col(0)), pl.BlockSpec(blk, col(1)), pl.BlockSpec(blk, col_prev(1)),
                  pl.BlockSpec(blk, col(2)), pl.BlockSpec(blk, col_prev(2)),
                  pl.BlockSpec(bias.shape, _const_map(3))],
        out_specs=[pl.BlockSpec(blk, lambda r, n: (r * nb + n, 0)),
                   pl.BlockSpec((N_HEADS, BLOCK, LANES), lambda r, n: (0, r * nb + n, 0))],
        out_shape=[jax.ShapeDtypeStruct((t_len, DIL_WIDTH), F32),
                   jax.ShapeDtypeStruct((N_HEADS, t_len, LANES), F32)],
        compiler_params=_cparams(("arbitrary", "arbitrary"), VMEM_LIMIT),
    )(p, p, p, p, p, bias)


def _att_bwd_call(p, bias, o, lse, do, dlse, g, dil):
    t_len = p.shape[0]
    l_len = t_len // dil
    nb = l_len // BLOCK

    def body(q_ref, qn_ref, k_ref, v_ref, do_ref, don_ref, o_ref, on_ref, lse_ref, lsen_ref, dl_ref, dln_ref,
             bias_ref, dq_ref, dk_ref, dv_ref, dbias_ref, carry_ref):
        r = pl.program_id(0)
        n = pl.program_id(1)
        cur_ok, prev_band, lane = _att_masks()
        has_next = n + 1 < nb

        @pl.when(jnp.logical_and(r == 0, n == 0))
        def _():
            dbias_ref[...] = jnp.zeros_like(dbias_ref)

        @pl.when(n == 0)
        def _():
            carry_ref[...] = jnp.zeros_like(carry_ref)

        for hp in range(N_PAIRS):
            lanes = slice(hp * LANES, (hp + 1) * LANES)
            k2 = k_ref[:, lanes].astype(BF16)
            v2 = v_ref[:, lanes].astype(BF16)
            dk2 = jnp.zeros((BLOCK, LANES), F32)
            dv2 = jnp.zeros((BLOCK, LANES), F32)
            dq_cur = carry_ref[:, lanes]
            dq_next = jnp.zeros((BLOCK, LANES), F32)
            for hh in range(2):
                h = 2 * hp + hh
                mine = (lane // HEAD_DIM) == hh
                tiles = (
                    (q_ref, do_ref, o_ref, lse_ref, dl_ref, cur_ok, slice(BLOCK, 2 * BLOCK), None),
                    (qn_ref, don_ref, on_ref, lsen_ref, dln_ref, prev_band, slice(0, BLOCK), has_next),
                )
                if nb == 1:
                    tiles = tiles[:1]
                for ti, (qr, dor, orf, lr, dlr, ok, bcols, gate) in enumerate(tiles):
                    q2 = qr[:, lanes].astype(BF16)
                    qm = jnp.where(mine, q2, jnp.zeros_like(q2))
                    do_f = jnp.where(mine, dor[:, lanes], 0.0)
                    dom = do_f.astype(BF16)
                    s = lax.dot_general(qm, k2, _DOT_DIMS["nt"], preferred_element_type=F32) * ATT_SCALE
                    s = s + bias_ref[h, :, bcols]
                    if gate is not None:
                        ok = jnp.logical_and(ok, gate)
                    pr = jnp.where(ok, jnp.exp(jnp.minimum(s - lr[h], 0.0)), 0.0)
                    dp = lax.dot_general(dom, v2, _DOT_DIMS["nt"], preferred_element_type=F32)
                    delta = jnp.sum(do_f * orf[:, lanes], axis=-1, keepdims=True)
                    dl = jnp.sum(dlr[h], axis=-1, keepdims=True)
                    ds = pr * (dp - delta + dl)
                    dsb = ds.astype(BF16)
                    dq_h = jnp.where(mine, jnp.dot(dsb, k2, preferred_element_type=F32), 0.0) * ATT_SCALE
                    if ti == 0:
                        dq_cur = dq_cur + dq_h
                    else:
                        dq_next = dq_next + dq_h
                    dk2 = dk2 + lax.dot_general(dsb, qm, _DOT_DIMS["tn"], preferred_element_type=F32) * ATT_SCALE
                    dv2 = dv2 + lax.dot_general(pr.astype(BF16), dom, _DOT_DIMS["tn"], preferred_element_type=F32)
                    dbias_ref[h, :, bcols] += ds
            dq_ref[:, lanes] = dq_cur
            carry_ref[:, lanes] = dq_next
            dk_ref[:, lanes] = dk2
            dv_ref[:, lanes] = dv2

    def nxt(n):
        return jnp.minimum(n + 1, nb - 1)

    blk = (BLOCK, DIL_WIDTH)
    hblk = (N_HEADS, BLOCK, LANES)
    qcol = lambda j: (lambda r, n: (r * nb + n, j))
    q_next = lambda r, n: (r * nb + nxt(n), 0)
    rown = lambda r, n: (r * nb + n, 0)
    rown_next = lambda r, n: (r * nb + nxt(n), 0)
    hrow = lambda r, n: (0, r * nb + n, 0)
    hrow_next = lambda r, n: (0, r * nb + nxt(n), 0)
    sds = jax.ShapeDtypeStruct((t_len, DIL_WIDTH), F32)
    return pl.pallas_call(
        body, name=f"dil_att_bwd_g{g}", grid=(dil, nb),
        in_specs=[pl.BlockSpec(blk, qcol(0)), pl.BlockSpec(blk, q_next),
                  pl.BlockSpec(blk, qcol(1)), pl.BlockSpec(blk, qcol(2)),
                  pl.BlockSpec(blk, rown), pl.BlockSpec(blk, rown_next),
                  pl.BlockSpec(blk, rown), pl.BlockSpec(blk, rown_next),
                  pl.BlockSpec(hblk, hrow), pl.BlockSpec(hblk, hrow_next),
                  pl.BlockSpec(hblk, hrow), pl.BlockSpec(hblk, hrow_next),
                  pl.BlockSpec(bias.shape, _const_map(3))],
        out_specs=[pl.BlockSpec(blk, rown)] * 3 + [pl.BlockSpec(bias.shape, _const_map(3))],
        out_shape=[sds, sds, sds, jax.ShapeDtypeStruct(bias.shape, F32)],
        scratch_shapes=[pltpu.VMEM((BLOCK, DIL_WIDTH), F32)],
        compiler_params=_cparams(("arbitrary", "arbitrary"), VMEM_LIMIT),
    )(p, p, p, p, do, do, o, o, lse, lse, dlse, dlse, bias)


def _att_all_groups(ps, biases):
    outs = [_att_fwd_call(ps[g], biases[g], g, dil) for g, (_, dil) in enumerate(DIL_PATTERNS)]
    return tuple(o for o, _ in outs), tuple(l for _, l in outs)


@jax.custom_vjp
def _dilated_attention(ps, biases):
    return _att_all_groups(ps, biases)


def _dilated_attention_fwd(ps, biases):
    os_, lses = _att_all_groups(ps, biases)
    return (os_, lses), (ps, biases, os_, lses)


def _dilated_attention_bwd(res, cts):
    ps, biases, os_, lses = res
    dos, dlses = cts
    dps, dbiases = [], []
    for g, (_, dil) in enumerate(DIL_PATTERNS):
        dq, dk, dv, dbias = _att_bwd_call(ps[g], biases[g], os_[g], lses[g], dos[g], dlses[g], g, dil)
        dps.append(jnp.concatenate([dq, dk, dv], axis=1))
        dbiases.append(dbias)
    return tuple(dps), tuple(dbiases)


_dilated_attention.defvjp(_dilated_attention_fwd, _dilated_attention_bwd)


def _me():
    return lax.axis_index("x"), lax.axis_index("y"), lax.axis_index("c")


def _flip(me, f):
    return tuple((1 - m) if b else m for m, b in zip(me, f))


def _chip_of(d):
    return 2 * d[0] + d[1]


def _dev_of(d):
    return 4 * d[0] + 2 * d[1] + d[2]


EXCHANGE_CHUNKS = 8
CHIP_FLIPS = ((1, 0, 0), (0, 1, 0), (1, 1, 0))
ALL_FLIPS = tuple((a, b, c) for a in (0, 1) for b in (0, 1) for c in (0, 1) if a or b or c)
CORE_FLIP = (0, 0, 1)


def _n_chunks(rows):
    return EXCHANGE_CHUNKS if rows % (EXCHANGE_CHUNKS * PACK_ROWS) == 0 else 1


def _exchange(src, n_slots, transfers, name):
    _, rows, cols = src.shape
    chunks = _n_chunks(rows)
    rc = rows // chunks
    n = len(transfers) * chunks

    def body(src_ref, dst_ref, send_sems, recv_sems):
        me = _me()
        copies = []
        for q in range(chunks):
            for kk, (f, src_slot, dst_slot) in enumerate(transfers):
                peer = _flip(me, f)
                cp = pltpu.make_async_remote_copy(
                    src_ref=src_ref.at[src_slot(me, peer), pl.ds(q * rc, rc)],
                    dst_ref=dst_ref.at[dst_slot(me, peer), pl.ds(q * rc, rc)],
                    send_sem=send_sems.at[kk * chunks + q], recv_sem=recv_sems.at[kk * chunks + q],
                    device_id=peer, device_id_type=MESH)
                cp.start()
                copies.append(cp)
        for cp in copies:
            cp.wait()

    return pl.pallas_call(
        body, name=name,
        out_shape=jax.ShapeDtypeStruct((n_slots, rows, cols), src.dtype),
        in_specs=[pl.BlockSpec(memory_space=pl.ANY)],
        out_specs=pl.BlockSpec(memory_space=pl.ANY),
        scratch_shapes=[pltpu.SemaphoreType.DMA((n,)), pltpu.SemaphoreType.DMA((n,))],
    )(src)


def _set_slot(buf, block, index):
    return lax.dynamic_update_slice_in_dim(buf, block[None].astype(buf.dtype), index, axis=0)


def _chip_all_gather(src, name):
    got = _exchange(src[None], N_CHIPS, [(f, lambda me, peer: 0, lambda me, peer: _chip_of(me)) for f in CHIP_FLIPS], name)
    return _set_slot(got, src, _chip_of(_me()))


def _dev_all_gather(src, name):
    got = _exchange(src[None], N_DEV, [(f, lambda me, peer: 0, lambda me, peer: _dev_of(me)) for f in ALL_FLIPS], name)
    return _set_slot(got, src, _dev_of(_me()))


def _chip_scatter(src, name):
    got = _exchange(src, N_CHIPS, [(f, lambda me, peer: _chip_of(peer), lambda me, peer: _chip_of(me))
                                   for f in CHIP_FLIPS], name)
    chip = _chip_of(_me())
    return _set_slot(got, lax.dynamic_index_in_dim(src, chip, 0, keepdims=False), chip)


def _core_halves(src, name):
    s, _, half, cols = src.shape
    transfers = [(CORE_FLIP, (lambda me, peer, j=j: 2 * j + peer[2]), (lambda me, peer, j=j: j)) for j in range(s)]
    return _exchange(src.reshape(2 * s, half, cols), s, transfers, name)


def _core_all_gather(src, name):
    got = _exchange(src[None], 2, [(CORE_FLIP, lambda me, peer: 0, lambda me, peer: me[2])], name)
    return _set_slot(got, src, _me()[2])


def _two_level_gather(src, name):
    rows, cols = src.shape
    half = rows // 2
    chunks = _n_chunks(half)
    rc = half // chunks
    n = len(CHIP_FLIPS) * chunks

    def body(src_ref, g_ref, send1, recv1, send2, recv2):
        me = _me()
        c = me[2]
        sibling = _flip(me, CORE_FLIP)
        first, second = [], []
        for q in range(chunks):
            for kk, f in enumerate(CHIP_FLIPS):
                peer = _flip(me, f)
                cp = pltpu.make_async_remote_copy(
                    src_ref=src_ref.at[c, pl.ds(q * rc, rc)], dst_ref=g_ref.at[_chip_of(me), c, pl.ds(q * rc, rc)],
                    send_sem=send1.at[kk * chunks + q], recv_sem=recv1.at[kk * chunks + q],
                    device_id=peer, device_id_type=MESH)
                cp.start()
                first.append((cp, _chip_of(peer), kk * chunks + q, q))
        for cp, origin, idx, q in first:
            cp.wait_recv()
            fw = pltpu.make_async_remote_copy(
                src_ref=g_ref.at[origin, c, pl.ds(q * rc, rc)], dst_ref=g_ref.at[origin, c, pl.ds(q * rc, rc)],
                send_sem=send2.at[idx], recv_sem=recv2.at[idx],
                device_id=sibling, device_id_type=MESH)
            fw.start()
            second.append(fw)
        for fw in second:
            fw.wait_recv()
        for cp, _, _, _ in first:
            cp.wait_send()
        for fw in second:
            fw.wait_send()

    got = pl.pallas_call(
        body, name=name,
        out_shape=jax.ShapeDtypeStruct((N_CHIPS, 2, half, cols), src.dtype),
        in_specs=[pl.BlockSpec(memory_space=pl.ANY)],
        out_specs=pl.BlockSpec(memory_space=pl.ANY),
        scratch_shapes=[pltpu.SemaphoreType.DMA((n,)) for _ in range(4)],
    )(src.reshape(2, half, cols))
    return _set_slot(got.reshape(N_CHIPS, rows, cols), src, _chip_of(_me()))


def _sum_slots(x, name):
    s, rows, cols = x.shape
    tile = _pick(rows, (512, 256, 128, 64, 32, 16, 8))

    def body(x_ref, o_ref):
        acc = x_ref[0].astype(F32)
        for i in range(1, s):
            acc = acc + x_ref[i].astype(F32)
        o_ref[...] = acc

    return pl.pallas_call(
        body, name=name, grid=(rows // tile,),
        in_specs=[pl.BlockSpec((s, tile, cols), lambda i: (0, i, 0))],
        out_specs=pl.BlockSpec((tile, cols), lambda i: (i, 0)),
        out_shape=jax.ShapeDtypeStruct((rows, cols), F32),
        compiler_params=_cparams(("parallel",), VMEM_LIMIT),
    )(x)


def _sum_slots_own(recv, send, chip, name):
    s, rows, cols = recv.shape
    tile = _pick(rows, (512, 256, 128, 64, 32, 16, 8))

    def body(chip_ref, recv_ref, own_ref, o_ref):
        acc = None
        for j in range(s):
            term = jnp.where(chip_ref[0] == j, own_ref[0], recv_ref[j]).astype(F32)
            acc = term if acc is None else acc + term
        o_ref[...] = acc

    grid_spec = pltpu.PrefetchScalarGridSpec(
        num_scalar_prefetch=1, grid=(rows // tile,),
        in_specs=[pl.BlockSpec((s, tile, cols), lambda i, c: (0, i, 0)),
                  pl.BlockSpec((1, tile, cols), lambda i, c: (c[0], i, 0))],
        out_specs=pl.BlockSpec((tile, cols), lambda i, c: (i, 0)))
    return pl.pallas_call(
        body, name=name, grid_spec=grid_spec,
        out_shape=jax.ShapeDtypeStruct((rows, cols), F32),
        compiler_params=_cparams(("arbitrary",), VMEM_LIMIT),
    )(jnp.reshape(chip, (1,)).astype(jnp.int32), recv, send)


def _add_pairs(a, b, name):
    s, rows, cols = a.shape
    tile = _pick(rows, (512, 256, 128, 64, 32, 16, 8))

    def body(a_ref, b_ref, o_ref):
        o_ref[...] = (a_ref[...].astype(F32) + b_ref[...].astype(F32)).astype(o_ref.dtype)

    spec = pl.BlockSpec((1, tile, cols), lambda j, i: (j, i, 0))
    return pl.pallas_call(
        body, name=name, grid=(s, rows // tile),
        in_specs=[spec, spec], out_specs=spec,
        out_shape=jax.ShapeDtypeStruct(a.shape, a.dtype),
        compiler_params=_cparams(("parallel", "parallel"), VMEM_LIMIT),
    )(a, b)


def _adamw(w, g, m, v, name):
    rows, cols = w.shape
    tile = rows
    if rows * cols * 4 > 2 * 1024 * 1024:
        tile = _pick(rows, (256, 128, 64, 32, 16, 8))
    c1 = 1.0 / (1.0 - ADAM_B1 ** ADAM_STEP)
    c2 = 1.0 / (1.0 - ADAM_B2 ** ADAM_STEP)

    def body(w_ref, g_ref, m_ref, v_ref, d_ref, nm_ref, nv_ref):
        gv = g_ref[...]
        nm = ADAM_B1 * m_ref[...] + (1.0 - ADAM_B1) * gv
        nv = ADAM_B2 * v_ref[...] + (1.0 - ADAM_B2) * (gv * gv)
        m_hat = nm * c1
        v_hat = nv * c2
        d_ref[...] = -ADAM_LR * (m_hat / (jnp.sqrt(v_hat) + ADAM_EPS) + ADAM_WD * w_ref[...])
        nm_ref[...] = nm
        nv_ref[...] = nv

    spec = pl.BlockSpec((tile, cols), lambda i: (i, 0))
    sds = jax.ShapeDtypeStruct((rows, cols), F32)
    return pl.pallas_call(
        body, name=name, grid=(rows // tile,),
        in_specs=[spec] * 4, out_specs=[spec] * 3, out_shape=[sds] * 3,
        compiler_params=_cparams(("parallel",), VMEM_LIMIT),
    )(w, g, m, v)


def _ada_fwd(c_all, ada_w, ada_b_cols):
    n_col = ada_w.shape[2]

    def body(c_ref, w_ref, b_ref, o_ref):
        cv = c_ref[...]
        cond = (cv * jax.nn.sigmoid(cv)).astype(BF16)
        o_ref[0] = jnp.dot(cond, w_ref[0].astype(BF16), preferred_element_type=F32) + b_ref[0]

    return pl.pallas_call(
        body, name="ada_fwd", grid=(DEPTH,),
        in_specs=[pl.BlockSpec(c_all.shape, lambda i: (0, 0)),
                  pl.BlockSpec((1, D_MODEL, n_col), lambda i: (i, 0, 0)),
                  pl.BlockSpec((1, 1, n_col), lambda i: (i, 0, 0))],
        out_specs=pl.BlockSpec((1, N_DEV, n_col), lambda i: (i, 0, 0)),
        out_shape=jax.ShapeDtypeStruct((DEPTH, N_DEV, n_col), F32),
        compiler_params=_cparams(("parallel",), VMEM_LIMIT),
    )(c_all, ada_w, ada_b_cols)


def _ada_grad(c_all_t, dmod_cols):
    n_col = dmod_cols.shape[2]

    def body(c_ref, d_ref, o_ref):
        cv = c_ref[...]
        cond = cv * jax.nn.sigmoid(cv)
        o_ref[0] = jnp.dot(cond, d_ref[0], precision=lax.Precision.HIGHEST, preferred_element_type=F32)

    return pl.pallas_call(
        body, name="ada_grad", grid=(DEPTH,),
        in_specs=[pl.BlockSpec(c_all_t.shape, lambda i: (0, 0)),
                  pl.BlockSpec((1, LANES, n_col), lambda i: (i, 0, 0))],
        out_specs=pl.BlockSpec((1, D_MODEL, n_col), lambda i: (i, 0, 0)),
        out_shape=jax.ShapeDtypeStruct((DEPTH, D_MODEL, n_col), F32),
        compiler_params=_cparams(("parallel",), VMEM_LIMIT),
    )(c_all_t, dmod_cols)


ROW_TILE = 256


def _shift_rows(a, n=1):
    return jnp.pad(a, ((n, 0), (0, 0)))[:-n]


def _modulate(x, sc, sh, name):
    def fn(x, sc, sh):
        return (x * (1.0 + sc) + sh,)

    t = x.shape[0]
    return _fused(fn, [_tiled(x, ROW_TILE), _shared(sc), _shared(sh)],
                  [_tiled_out(t, D_MODEL, ROW_TILE)], (t // ROW_TILE,), name)[0]


def _resid_ln_mod(x, y, gate, ln_g, ln_b, sc, sh, name):
    def fn(x, y, gate, ln_g, ln_b, sc, sh):
        x1 = _layer_norm_rows(ALPHA * x + (1.0 + gate) * y, ln_g, ln_b)
        return x1, x1 * (1.0 + sc) + sh

    t = x.shape[0]
    return _fused(fn, [_tiled(x, ROW_TILE), _tiled(y, ROW_TILE)] + [_shared(a) for a in (gate, ln_g, ln_b, sc, sh)],
                  [_tiled_out(t, D_MODEL, ROW_TILE)] * 2, (t // ROW_TILE,), name)


def _resid_ln_loss(x, y, gate, ln_g, ln_b, target, name):
    def fn(x, y, gate, ln_g, ln_b, target):
        x1 = _layer_norm_rows(ALPHA * x + (1.0 + gate) * y, ln_g, ln_b)
        err = jnp.square(x1 - target)
        per_row = jnp.mean(err, axis=-1, keepdims=True)
        return (0.5 * jnp.sum(per_row, axis=0, keepdims=True),)

    t = x.shape[0]
    return _fused(fn, [_tiled(x, ROW_TILE), _tiled(y, ROW_TILE)] + [_shared(a) for a in (gate, ln_g, ln_b)]
                  + [_tiled(target, ROW_TILE)],
                  [((1, 1), (1, 1), _const_map(2), "a")], (t // ROW_TILE,), name)[0]


def _mlp(u, w1, s1, w2, s2, name):
    def run(u, w1, w2):
        h, act = _mm(u, w1, "nn", name + "_w1_fwd", square_relu=True)
        return _mm(act, w2, "nn", name + "_w2_fwd"), (u, w1, w2, h, act)

    @jax.custom_vjp
    def op(u, w1, s1, w2, s2):
        return run(u, w1, w2)[0]

    def fwd(u, w1, s1, w2, s2):
        return run(u, w1, w2)

    def bwd(res, dy):
        u, w1, w2, h, act = res
        dh = _mm(dy, w2, "nt", name + "_w2_dx", pre_act=h)
        dw2 = _mm(act, dy, "tn", name + "_w2_dw")
        du = _mm(dh, w1, "nt", name + "_w1_dx")
        dw1 = _mm(u, dh, "tn", name + "_w1_dw")
        return du, jnp.zeros_like(w1), dw1, jnp.zeros_like(w2), dw2

    op.defvjp(fwd, bwd)
    return op(u, w1, s1, w2, s2)


AB_PIECES = (("r", 0, 512, 512), ("k", 512, 512, 512), ("v", 1024, 512, 512),
             ("wd", 1536, 64, 128), ("ad", 1600, 64, 128), ("gd", 1664, 160, 256),
             ("h", 1824, 512, 512), ("bg", 2336, 512, 512), ("cg", 2848, 512, 512))
AB_PAD_COLS = sum(p[3] for p in AB_PIECES)


def _regroup_cols(w):
    parts = []
    for _, start, width, padded in AB_PIECES:
        piece = w[..., start:start + width]
        if padded != width:
            piece = jnp.pad(piece, [(0, 0)] * (w.ndim - 1) + [(0, padded - width)])
        parts.append(piece)
    return jnp.concatenate(parts, axis=-1)


def _pad_rows(w, rows):
    return jnp.pad(w, ((0, rows - w.shape[0]), (0, 0)))


def _rwkv_shortconv(u, big, wts, wsrc, gslot):
    t = u.shape[0]
    p = _linear(u, _regroup_cols(big["ab_w_in"][0]), _regroup_cols(wts["ab_w_in"][0]), "ab_in")
    mu = _regroup_cols(jnp.pad(wts["rw_mu"], ((0, 0), (0, AB_PROJ - RW_PROJ))))
    w_up = _pad_rows(wts["rw_w_up"][0], 128)
    a_up = _pad_rows(wts["rw_a_up"][0], 128)
    g_up = _pad_rows(wts["rw_g_up"][0], 256)

    def pre(rp, rs, kp, ks, vp, vs, wdp, wds, adp, ads, gdp, gds, h, cg,
            mu_r, mu_k, mu_v, mu_w, mu_a, mu_g, w0, w_up, a0, a_up, g_up, k_k, k_a):
        def mix(pv, sv, m):
            return pv + m * (sv - pv)

        r, k, v = mix(rp, rs, mu_r), mix(kp, ks, mu_k), mix(vp, vs, mu_v)
        wd, ad, gd = mix(wdp, wds, mu_w), mix(adp, ads, mu_a), mix(gdp, gds, mu_g)
        logw = -_softplus(-(w0 + _bdot(jnp.tanh(wd), w_up))) - 0.5
        decay = jnp.exp(-jnp.exp(logw))
        iclr = jax.nn.sigmoid(a0 + _bdot(ad, a_up))
        gate = _bdot(jax.nn.sigmoid(gd), g_up)
        kk = k * k_k
        kk = kk / jnp.maximum(jnp.sqrt(_head_sum(kk * kk)), 1e-12)
        k_h = k * (1.0 + (iclr - 1.0) * k_a)
        return r, decay, k_h, v, -kk, kk * iclr, gate, cg * h

    tile = ROW_TILE
    names = [q[0] for q in AB_PIECES]
    cuts = list(np.cumsum([q[3] for q in AB_PIECES])[:-1])
    pp = dict(zip(names, jnp.split(p, cuts, axis=1)))
    mp = dict(zip(names, jnp.split(mu, cuts, axis=1)))

    ins = []
    for name in ("r", "k", "v", "wd", "ad", "gd"):
        ins += [_tiled(pp[name], tile), _tiled(_shift_rows(pp[name]), tile)]
    ins += [_tiled(pp["h"], tile), _tiled(pp["cg"], tile)]
    ins += [_shared(mp[name]) for name in ("r", "k", "v", "wd", "ad", "gd")]
    ins += [_shared(a) for a in (wts["rw_w0"], w_up, wts["rw_a0"], a_up, g_up, wts["rw_k_k"], wts["rw_k_a"])]
    outs = [_tiled_out(t, RW_WIDTH, tile)] * 8
    r, decay, k_h, v, a, b, gate, z = _fused(pre, ins, outs, (t // tile,), "rwkv_pre")

    y, gathered, token = _rwkv_scan(r, decay, k_h, v, a, b, wsrc, gslot)

    conv_w = wts["sc_conv_w"][0]
    r_k = wts["rw_r_k"].reshape(1, RW_WIDTH)

    def post(y, r, k_h, v, gate, bg, z, z1, z2, lnx_g, lnx_b, r_k, c0, c1, c2):
        mean = _head_sum(y) * (1.0 / HEAD_DIM)
        yc = y - mean
        var = _head_sum(yc * yc) * (1.0 / HEAD_DIM)
        yn = yc * lax.rsqrt(var + RW_GN_EPS) * lnx_g + lnx_b
        bonus = _head_sum(r * k_h * r_k) * v
        return (yn + bonus) * gate, bg * (c0 * z2 + c1 * z1 + c2 * z)

    ins = [_tiled(a_, tile) for a_ in (y, r, k_h, v, gate, pp["bg"])]
    ins += [_tiled(a_, tile) for a_ in (z, _shift_rows(z, 1), _shift_rows(z, 2))]
    ins += [_shared(a_) for a_ in (wts["rw_lnx_g"], wts["rw_lnx_b"], r_k, conv_w[0:1], conv_w[1:2], conv_w[2:3])]
    y_a, y_b = _fused(post, ins, [_tiled_out(t, RW_WIDTH, tile)] * 2, (t // tile,), "rwkv_post")
    out = _linear(jnp.concatenate([y_a, y_b], axis=1), big["ab_w_out"][0], wts["ab_w_out"][0], "ab_out")
    return out, gathered, token


def _t5_bucket_np(dist):
    exact = N_BUCKETS // 2
    logd = np.log(np.maximum(dist, 1).astype(np.float32) / exact) / math.log(MAX_DISTANCE / exact)
    large = np.minimum(exact + (logd * (N_BUCKETS - exact)).astype(np.int32), N_BUCKETS - 1)
    return np.where(dist < exact, dist, large)


def _merge_groups(os_, lses, name):
    t = os_[0].shape[0]
    tile = ROW_TILE

    def fn(o0, o1, o2, l0, l1, l2):
        lane = lax.broadcasted_iota(jnp.int32, (1, LANES), 1)
        lo = lane < HEAD_DIM
        ls = [jnp.where(lo, l[0], l[1]) for l in (l0, l1, l2)]
        m = jnp.maximum(jnp.maximum(ls[0], ls[1]), ls[2])
        es = [jnp.exp(l - m) for l in ls]
        den = es[0] + es[1] + es[2]
        return ((es[0] * o0 + es[1] * o1 + es[2] * o2) / den,)

    ins = [(o, (tile, LANES), lambda i, hp: (i, hp), "t") for o in os_]
    ins += [(l, (2, tile, LANES), lambda i, hp: (hp, i, 0), "t") for l in lses]
    outs = [((t, DIL_WIDTH), (tile, LANES), lambda i, hp: (i, hp), "t")]
    return _fused(fn, ins, outs, (t // tile, N_PAIRS), name)[0]


def _residue_major(a, dil, axis=0):
    if dil == 1:
        return a
    shp = a.shape
    split = a.reshape(shp[:axis] + (shp[axis] // dil, dil) + shp[axis + 1:])
    return jnp.swapaxes(split, axis, axis + 1).reshape(shp)


def _position_major(a, dil, axis=0):
    if dil == 1:
        return a
    shp = a.shape
    split = a.reshape(shp[:axis] + (dil, shp[axis] // dil) + shp[axis + 1:])
    return jnp.swapaxes(split, axis, axis + 1).reshape(shp)


def _dilated_mixer(u, big, wts):
    group_cols = 3 * DIL_WIDTH
    ps = []
    for g, (_, dil) in enumerate(DIL_PATTERNS):
        cols = slice(g * group_cols, (g + 1) * group_cols)
        ps.append(_linear(_residue_major(u, dil), big["dil_w_qkv"][0][:, cols], wts["dil_w_qkv"][0][:, cols],
                          f"dil_qkv{g}"))
    qi = np.arange(BLOCK)[:, None]
    ki = np.arange(2 * BLOCK)[None, :]
    rel = BLOCK + qi - ki
    biases = []
    for g, (window, dil) in enumerate(DIL_PATTERNS):
        span = window // dil
        bucket = _t5_bucket_np(np.clip(rel, 0, span) * dil).reshape(-1)
        onehot = jnp.asarray(np.eye(N_BUCKETS, dtype=np.float32)[bucket])
        table = wts["rel_bias"][:, g * N_HEADS:(g + 1) * N_HEADS]
        bias = jnp.dot(onehot, table, precision=lax.Precision.HIGHEST)
        biases.append(jnp.transpose(bias.reshape(BLOCK, 2 * BLOCK, N_HEADS), (2, 0, 1)))
    os_, lses = _dilated_attention(tuple(ps), tuple(biases))
    os_ = [_position_major(o, dil) for o, (_, dil) in zip(os_, DIL_PATTERNS)]
    lses = [_position_major(l, dil, axis=1) for l, (_, dil) in zip(lses, DIL_PATTERNS)]
    o = _merge_groups(os_, lses, "dil_merge")
    return _linear(o, big["dil_w_out"][0], wts["dil_w_out"][0], "dil_out")


def _forward_local(x, mods, big, wts, wsrc, gslot, late_shapes, target):
    u = _modulate(x, mods[0, 1], mods[0, 0], "mod_in")
    for i in range(DEPTH):
        sh2, sc2, g1, g2 = mods[i, 3], mods[i, 4], mods[i, 2], mods[i, 5]
        if i == 0:
            y, gathered, token = _rwkv_shortconv(u, big, wts, wsrc, gslot)
            shard_shapes = [s[:SHARDED[n]] + (s[SHARDED[n]] // N_CHIPS,) + s[SHARDED[n] + 1:]
                            for n, s in late_shapes.items()]
            parts = _unpack_chips(gathered, shard_shapes, own=wsrc)
            big = {**big, **{n: _join_chips(p, SHARDED[n]) for n, p in zip(late_shapes, parts)}}
            wts = {**wts, **_gradient_slots(token, late_shapes, tuple(late_shapes))}
        else:
            y = _dilated_mixer(u, big, wts)
        x, u = _resid_ln_mod(x, y, g1, wts["ln_g"][i, 0:1], wts["ln_b"][i, 0:1], sc2, sh2, f"ln_mix{i}")
        y = _mlp(u, big["mlp_w1"][i], wts["mlp_w1"][i], big["mlp_w2"][i], wts["mlp_w2"][i], f"mlp{i}")
        if i + 1 < DEPTH:
            x, u = _resid_ln_mod(x, y, g2, wts["ln_g"][i, 1:2], wts["ln_b"][i, 1:2],
                                 mods[i + 1, 1], mods[i + 1, 0], f"ln_mlp{i}")
        else:
            return _resid_ln_loss(x, y, g2, wts["ln_g"][i, 1:2], wts["ln_b"][i, 1:2], target, "ln_loss")


SHARDED = {"ab_w_in": 2, "ab_w_out": 1, "dil_w_qkv": 2, "dil_w_out": 2, "mlp_w1": 2, "mlp_w2": 1,
           "ln_g": 2, "ln_b": 2, "rw_w_up": 2, "rw_a_up": 2, "rw_g_up": 2, "sc_conv_w": 2}
FIRST_MIXER = ("ab_w_in", "ab_w_out")
LATER_LAYERS = ("dil_w_qkv", "dil_w_out", "mlp_w1", "mlp_w2")
SMALL_SHARDED = ("ln_g", "ln_b", "rw_w_up", "rw_a_up", "rw_g_up", "sc_conv_w")
REPLICATED = ("ada_b", "rw_mu", "rw_w0", "rw_a0", "rw_k_k", "rw_k_a", "rw_r_k", "rw_lnx_g", "rw_lnx_b", "rel_bias")
WEIGHT_ORDER = ("ada_w", "ada_b", "ln_g", "ln_b", "ab_w_in", "rw_mu", "rw_w0", "rw_w_up", "rw_a0", "rw_a_up",
                "rw_g_up", "rw_k_k", "rw_k_a", "rw_r_k", "rw_lnx_g", "rw_lnx_b", "sc_conv_w", "ab_w_out",
                "dil_w_qkv", "dil_w_out", "rel_bias", "mlp_w1", "mlp_w2")


PACK_ROWS = 16


def _rows_of(n_elems):
    return -(-n_elems // (ROW_W * PACK_ROWS)) * PACK_ROWS


def _to_rows(a):
    flat = a.reshape(-1)
    rows = _rows_of(flat.shape[0])
    if rows * ROW_W != flat.shape[0]:
        flat = jnp.pad(flat, (0, rows * ROW_W - flat.shape[0]))
    return flat.reshape(rows, ROW_W)


def _from_rows(rows, shape):
    n = int(np.prod(shape))
    return rows.reshape(-1)[:n].reshape(shape)


def _split_chips(full, axis):
    shp = full.shape
    parts = full.reshape(shp[:axis] + (N_CHIPS, shp[axis] // N_CHIPS) + shp[axis + 1:])
    return jnp.moveaxis(parts, axis, 0)


def _join_chips(parts, axis):
    moved = jnp.moveaxis(parts, 0, axis)
    shp = moved.shape
    return moved.reshape(shp[:axis] + (shp[axis] * shp[axis + 1],) + shp[axis + 2:])


def _pack_rows(arrays, row_multiple=256):
    blocks = [_to_rows(a) for a in arrays]
    total = sum(b.shape[0] for b in blocks)
    pad = (-total) % row_multiple
    if pad:
        blocks.append(jnp.zeros((pad, ROW_W), blocks[0].dtype))
    return jnp.concatenate(blocks, axis=0)


def _unpack_rows(buf, shapes):
    out, r0 = [], 0
    for shp in shapes:
        n = _rows_of(int(np.prod(shp)))
        out.append(_from_rows(buf[r0:r0 + n], shp))
        r0 += n
    return out


def _pack_chips(parts):
    blocks = []
    for p in parts:
        flat = p.reshape(N_CHIPS, -1)
        rows = _rows_of(flat.shape[1])
        if rows * ROW_W != flat.shape[1]:
            flat = jnp.pad(flat, ((0, 0), (0, rows * ROW_W - flat.shape[1])))
        blocks.append(flat.reshape(N_CHIPS, rows, ROW_W))
    total = sum(b.shape[1] for b in blocks)
    pad = (-total) % 256
    if pad:
        blocks.append(jnp.zeros((N_CHIPS, pad, ROW_W), blocks[0].dtype))
    return jnp.concatenate(blocks, axis=1)


def _unpack_chips(buf, shapes, own=None):
    mine = lax.broadcasted_iota(jnp.int32, (N_CHIPS, 1, 1), 0) == _chip_of(_me()) if own is not None else None
    out, r0 = [], 0
    for shp in shapes:
        size = int(np.prod(shp))
        n = _rows_of(size)
        rows = buf[:, r0:r0 + n]
        if own is not None:
            rows = jnp.where(mine, own[None, r0:r0 + n], rows)
        out.append(rows.reshape(N_CHIPS, -1)[:, :size].reshape((N_CHIPS,) + tuple(shp)))
        r0 += n
    return out


def _gradient_slots(token, full_shapes, names):
    @jax.custom_vjp
    def route(token):
        return {n: jnp.zeros(full_shapes[n], F32) for n in names}

    def fwd(token):
        return {n: jnp.zeros(full_shapes[n], F32) for n in names}, None

    def bwd(_, d):
        return (_pack_chips([_split_chips(d[n], SHARDED[n]).astype(BF16) for n in names]),)

    route.defvjp(fwd, bwd)
    return route(token)


def _as2d(a):
    return a.reshape(-1, a.shape[-1])


def kernel(x, c, ada_w, ada_b, ln_g, ln_b, ab_w_in, rw_mu, rw_w0, rw_w_up, rw_a0, rw_a_up, rw_g_up, rw_k_k, rw_k_a, rw_r_k, rw_lnx_g, rw_lnx_b, sc_conv_w, ab_w_out, dil_w_qkv, dil_w_out, rel_bias, mlp_w1, mlp_w2, loss_target, m_ada_w, m_ada_b, m_ln_g, m_ln_b, m_ab_w_in, m_rw_mu, m_rw_w0, m_rw_w_up, m_rw_a0, m_rw_a_up, m_rw_g_up, m_rw_k_k, m_rw_k_a, m_rw_r_k, m_rw_lnx_g, m_rw_lnx_b, m_sc_conv_w, m_ab_w_out, m_dil_w_qkv, m_dil_w_out, m_rel_bias, m_mlp_w1, m_mlp_w2, v_ada_w, v_ada_b, v_ln_g, v_ln_b, v_ab_w_in, v_rw_mu, v_rw_w0, v_rw_w_up, v_rw_a0, v_rw_a_up, v_rw_g_up, v_rw_k_k, v_rw_k_a, v_rw_r_k, v_rw_lnx_g, v_rw_lnx_b, v_sc_conv_w, v_ab_w_out, v_dil_w_qkv, v_dil_w_out, v_rel_bias, v_mlp_w1, v_mlp_w2):
    args = dict(locals())
    w_in = {n: args[n] for n in WEIGHT_ORDER}
    m_in = {n: args["m_" + n] for n in WEIGHT_ORDER}
    v_in = {n: args["v_" + n] for n in WEIGHT_ORDER}
    me = _me()
    chip = _chip_of(me)
    dev = _dev_of(me)

    c_all = _dev_all_gather(c, "gather_c")[:, 0, :]
    n_col = ada_w.shape[2]
    ada_b_cols = lax.dynamic_slice_in_dim(ada_b, chip * n_col, n_col, axis=1)[:, None, :]
    mod_cols = _ada_fwd(c_all, ada_w, ada_b_cols)

    first_buf = _pack_rows([w_in[n].astype(BF16) for n in FIRST_MIXER])
    first_all = _two_level_gather(first_buf, "gather_first")
    late_buf = _pack_rows([w_in[n].astype(BF16) for n in LATER_LAYERS])
    small_buf = _pack_rows([mod_cols] + [w_in[n] for n in SMALL_SHARDED], row_multiple=PACK_ROWS)
    small_all = _chip_all_gather(small_buf, "gather_small")

    def full_shape(n):
        shp = w_in[n].shape
        return shp[:SHARDED[n]] + (shp[SHARDED[n]] * N_CHIPS,) + shp[SHARDED[n] + 1:]

    wts = {n: w_in[n] for n in REPLICATED}
    big = {}
    for n, part in zip(FIRST_MIXER, _unpack_chips(first_all, [w_in[n].shape for n in FIRST_MIXER])):
        big[n] = _join_chips(part, SHARDED[n])
        wts[n] = jnp.zeros(full_shape(n), F32)
    small_parts = _unpack_chips(small_all, [mod_cols.shape] + [w_in[n].shape for n in SMALL_SHARDED])
    for n, part in zip(SMALL_SHARDED, small_parts[1:]):
        wts[n] = _join_chips(part, SHARDED[n])
    mod_all = _join_chips(small_parts[0], 2)
    mods = lax.dynamic_slice_in_dim(mod_all, dev, 1, axis=1).reshape(DEPTH, 6, 1, D_MODEL)
    late_shapes = {n: full_shape(n) for n in LATER_LAYERS}
    late_slot = jnp.zeros(late_buf.shape, F32)

    def local_loss(xv, modv, wv, slot):
        return _forward_local(xv, modv, big, wv, late_buf, slot, late_shapes, loss_target[0])[0, 0]

    loss_local, (grad_x, dmods, dw, late_part) = jax.value_and_grad(local_loss, argnums=(0, 1, 2, 3))(
        x[0], mods, wts, late_slot)

    small_row = jnp.concatenate([dmods.reshape(-1)] + [dw[n].reshape(-1) for n in REPLICATED[1:]]
                                + [loss_local.reshape(1)])
    n_small = small_row.shape[0]
    n_small_pad = -(-n_small // LANES) * LANES
    small_row = jnp.pad(small_row, (0, n_small_pad - n_small))[None, :]
    rows_all = _dev_all_gather(small_row, "gather_small_grads")
    small_sum = _sum_slots(rows_all, "sum_small_grads")
    loss = small_sum[0, n_small - 1]

    dmod_all = rows_all[:, 0, :DEPTH * 6 * D_MODEL].reshape(N_DEV, DEPTH, 6 * D_MODEL)
    dmod_cols = lax.dynamic_slice_in_dim(dmod_all, chip * n_col, n_col, axis=2)
    dmod_cols = jnp.pad(jnp.moveaxis(dmod_cols, 0, 1), ((0, 0), (0, LANES - N_DEV), (0, 0)))
    c_all_t = jnp.pad(c_all.T, ((0, 0), (0, LANES - N_DEV)))
    grads = {"ada_w": _ada_grad(c_all_t, dmod_cols)}
    grads["ada_b"] = small_sum[0, :DEPTH * 6 * D_MODEL].reshape(ada_b.shape)
    r0 = DEPTH * 6 * D_MODEL
    for n in REPLICATED[1:]:
        size = int(np.prod(w_in[n].shape))
        grads[n] = small_sum[0, r0:r0 + size].reshape(w_in[n].shape)
        r0 += size

    sharded_names = FIRST_MIXER + SMALL_SHARDED
    send = _pack_chips([_split_chips(dw[n], SHARDED[n]).astype(BF16) for n in sharded_names])
    n_rows = send.shape[1]
    send = send.reshape(N_CHIPS, 2, n_rows // 2, ROW_W)
    theirs = _core_halves(send, "swap_halves")
    mine = lax.dynamic_index_in_dim(send, me[2], 1, keepdims=False)
    chip_part = _add_pairs(mine, theirs, "sum_cores")
    recv = _chip_scatter(chip_part, "scatter_grads")
    half_sum = _sum_slots(recv, "sum_chips")
    g_rows = _core_all_gather(half_sum, "gather_halves").reshape(n_rows, ROW_W)
    for n, g in zip(sharded_names, _unpack_rows(g_rows, [w_in[n].shape for n in sharded_names])):
        grads[n] = g

    late_rows = _sum_slots(_core_all_gather(late_part, "swap_late"), "sum_cores_late")
    for n, g in zip(LATER_LAYERS, _unpack_rows(late_rows, [w_in[n].shape for n in LATER_LAYERS])):
        grads[n] = g

    deltas, new_m, new_v = {}, {}, {}
    for n in WEIGHT_ORDER:
        shp = w_in[n].shape
        d, nm, nv = _adamw(_as2d(w_in[n]), _as2d(grads[n]), _as2d(m_in[n]), _as2d(v_in[n]), "adamw_" + n)
        deltas[n], new_m[n], new_v[n] = d.reshape(shp), nm.reshape(shp), nv.reshape(shp)

    return (loss, grad_x[None], *[grads[n] for n in WEIGHT_ORDER], *[deltas[n] for n in WEIGHT_ORDER],
            *[new_m[n] for n in WEIGHT_ORDER], *[new_v[n] for n in WEIGHT_ORDER])
```

```python
import functools
import math

import numpy as np
import jax
import jax.numpy as jnp
from jax import lax
from jax.experimental import pallas as pl
from jax.experimental.pallas import tpu as pltpu

F32 = jnp.float32
BF16 = jnp.bfloat16
MESH = pl.DeviceIdType.MESH

D_MODEL = 1024
DEPTH = 2
RW_WIDTH = 512
HEAD_DIM = 64
N_HEADS = 8
RW_DECAY_RANK = 64
RW_ICLR_RANK = 64
RW_GATE_RANK = 160
RW_GN_EPS = 64e-5
RW_PROJ = 3 * RW_WIDTH + RW_DECAY_RANK + RW_ICLR_RANK + RW_GATE_RANK
SC_WIDTH = 512
AB_PROJ = RW_PROJ + 3 * SC_WIDTH
DIL_PATTERNS = ((128, 1), (512, 4), (2048, 16))
N_GROUPS = 3
DIL_WIDTH = 512
DIL_PROJ = N_GROUPS * 3 * DIL_WIDTH
BLOCK = 128
N_BUCKETS = 32
MAX_DISTANCE = 2048
D_FF = 4 * D_MODEL
ALPHA = (2 * DEPTH) ** 0.25
LN_EPS = 1e-5
ADAM_LR = 0.001
ADAM_B1 = 0.9
ADAM_B2 = 0.999
ADAM_EPS = 1e-08
ADAM_WD = 0.01
ADAM_STEP = 10

N_CHIPS = 4
N_DEV = 8
LANES = 128
ROW_W = 1024
SCAN_CHUNK = 16
VMEM_LIMIT = 48 * 1024 * 1024
NEG_BIG = -1e30


def _pick(n, cands):
    for c in cands:
        if n % c == 0:
            return c
    return n


def _cparams(sem=None, vmem=None):
    return pltpu.CompilerParams(dimension_semantics=sem, vmem_limit_bytes=vmem)


_DOT_DIMS = {
    "nn": (((1,), (0,)), ((), ())),
    "nt": (((1,), (1,)), ((), ())),
    "tn": (((0,), (0,)), ((), ())),
}


def _mm(a, b, mode, name, square_relu=False, pre_act=None, narrow_out=False):
    if mode == "nn":
        (m, k), (_, n) = a.shape, b.shape
    elif mode == "nt":
        (m, k), (n, _) = a.shape, b.shape
    else:
        (k, m), (_, n) = a.shape, b.shape
    tm = _pick(m, (1024, 512, 256, 128))
    tn = _pick(n, (1024, 768, 512, 384, 256, 128))
    wide_k = mode != "tn" and pre_act is None
    tk = _pick(k, (2048, 1024, 512, 256, 128) if wide_k else (1024, 512, 256, 128))
    nk = k // tk
    if mode == "tn":
        a_spec = pl.BlockSpec((tk, tm), lambda i, j, kk: (kk, i))
    else:
        a_spec = pl.BlockSpec((tm, tk), lambda i, j, kk: (i, kk))
    if mode == "nt":
        b_spec = pl.BlockSpec((tn, tk), lambda i, j, kk: (j, kk))
    else:
        b_spec = pl.BlockSpec((tk, tn), lambda i, j, kk: (kk, j))
    dims = _DOT_DIMS[mode]

    out_spec = pl.BlockSpec((tm, tn), lambda i, j, kk: (i, j))

    own_acc = narrow_out and nk > 1

    def body(*refs):
        a_ref, b_ref = refs[:2]
        h_ref = refs[2] if pre_act is not None else None
        o_ref = refs[3] if pre_act is not None else refs[2]
        act_ref = refs[o_pos + 1] if square_relu else None
        acc_ref = refs[-1] if own_acc else o_ref
        part = lax.dot_general(a_ref[...].astype(BF16), b_ref[...].astype(BF16), dims, preferred_element_type=F32)

        def finish(total):
            if h_ref is not None:
                total = total * (2.0 * jnp.maximum(h_ref[...], 0.0))
            o_ref[...] = total.astype(o_ref.dtype)
            if act_ref is not None:
                act_ref[...] = jnp.square(jnp.maximum(total, 0.0)).astype(BF16)

        if nk == 1:
            finish(part)
        else:
            kk = pl.program_id(2)

            @pl.when(kk == 0)
            def _():
                acc_ref[...] = part

            @pl.when(jnp.logical_and(kk > 0, kk < nk - 1))
            def _():
                acc_ref[...] += part

            @pl.when(kk == nk - 1)
            def _():
                finish(acc_ref[...] + part)

    operands = [a, b] + ([pre_act] if pre_act is not None else [])
    in_specs = [a_spec, b_spec] + ([out_spec] if pre_act is not None else [])
    o_pos = len(operands)
    out_shape = [jax.ShapeDtypeStruct((m, n), BF16 if narrow_out else F32)]
    out_shape += [jax.ShapeDtypeStruct((m, n), BF16)] if square_relu else []
    outs = pl.pallas_call(
        body, name=name, grid=(m // tm, n // tn, nk),
        in_specs=in_specs, out_specs=[out_spec] * len(out_shape), out_shape=out_shape,
        scratch_shapes=[pltpu.VMEM((tm, tn), F32)] if own_acc else [],
        compiler_params=_cparams(("parallel", "parallel", "arbitrary"), VMEM_LIMIT),
    )(*operands)
    return tuple(outs) if square_relu else outs[0]


def _linear(x, w, slot, name):
    @jax.custom_vjp
    def op(x, w, slot):
        return _mm(x, w, "nn", name + "_fwd")

    def fwd(x, w, slot):
        return _mm(x, w, "nn", name + "_fwd"), (x, w)

    def bwd(res, dy):
        x, w = res
        return _mm(dy, w, "nt", name + "_dx"), jnp.zeros_like(w), _mm(x, dy, "tn", name + "_dw", narrow_out=True)

    op.defvjp(fwd, bwd)
    return op(x, w, slot)


def _const_map(ndim):
    return lambda *g: (0,) * ndim


def _first_step(n_grid):
    return functools.reduce(jnp.logical_and, [pl.program_id(d) == 0 for d in range(n_grid)])


def _fused(fn, ins, outs, grid, name):
    arrays = [i[0] for i in ins]
    n_in, n_out, n_grid = len(ins), len(outs), len(grid)
    in_specs = [pl.BlockSpec(bs, im) for (_, bs, im, _) in ins]
    out_specs = [pl.BlockSpec(bs, im) for (_, bs, im, _) in outs]
    out_shapes = [jax.ShapeDtypeStruct(s, F32) for (s, _, _, _) in outs]
    sem = ("arbitrary",) * n_grid

    def fwd_call(*xs):
        def body(*refs):
            vals = [r[...] for r in refs[:n_in]]
            ys = fn(*vals)
            first = _first_step(n_grid)
            for o_ref, y, (_, _, _, kind) in zip(refs[n_in:], ys, outs):
                if kind == "t":
                    o_ref[...] = y
                else:
                    @pl.when(first)
                    def _(o_ref=o_ref):
                        o_ref[...] = jnp.zeros_like(o_ref)

                    o_ref[...] += y

        return pl.pallas_call(
            body, name=name + "_fwd", grid=grid, in_specs=in_specs, out_specs=out_specs,
            out_shape=out_shapes, compiler_params=_cparams(sem, VMEM_LIMIT))(*xs)

    def bwd_call(xs, dys):
        d_specs = [pl.BlockSpec(bs, im) for (_, bs, im, _) in outs]
        g_specs = [pl.BlockSpec(bs, im) for (_, bs, im, _) in ins]
        g_shapes = [jax.ShapeDtypeStruct(a.shape, F32) for a in arrays]

        def body(*refs):
            vals = [r[...] for r in refs[:n_in]]
            dvals = tuple(r[...] for r in refs[n_in:n_in + n_out])
            _, vjp = jax.vjp(lambda *v: tuple(fn(*v)), *vals)
            gs = vjp(dvals)
            first = _first_step(n_grid)
            for g_ref, g, (_, _, _, kind) in zip(refs[n_in + n_out:], gs, ins):
                if kind == "t":
                    g_ref[...] = g
                else:
                    @pl.when(first)
                    def _(g_ref=g_ref):
                        g_ref[...] = jnp.zeros_like(g_ref)

                    g_ref[...] += g

        return pl.pallas_call(
            body, name=name + "_bwd", grid=grid, in_specs=in_specs + d_specs, out_specs=g_specs,
            out_shape=g_shapes, compiler_params=_cparams(sem, VMEM_LIMIT))(*xs, *dys)

    @jax.custom_vjp
    def op(*xs):
        return tuple(fwd_call(*xs))

    def op_fwd(*xs):
        return tuple(fwd_call(*xs)), xs

    def op_bwd(xs, dys):
        return tuple(bwd_call(xs, dys))

    op.defvjp(op_fwd, op_bwd)
    return op(*arrays)


def _tiled(a, tile, cols=None, col_block=0):
    cols = a.shape[1] if cols is None else cols
    return (a, (tile, cols), lambda i, cb=col_block: (i, cb), "t")


def _shared(a):
    return (a, a.shape, _const_map(a.ndim), "b")


def _tiled_out(rows, cols, tile):
    return ((rows, cols), (tile, cols), lambda i: (i, 0), "t")


@jax.custom_vjp
def _bdot(x, w):
    return jnp.dot(x.astype(BF16), w.astype(BF16), preferred_element_type=F32)


def _bdot_fwd(x, w):
    return _bdot(x, w), (x, w)


def _bdot_bwd(res, dy):
    x, w = res
    dyb = dy.astype(BF16)
    dx = lax.dot_general(dyb, w.astype(BF16), _DOT_DIMS["nt"], preferred_element_type=F32)
    dw = lax.dot_general(x.astype(BF16), dyb, _DOT_DIMS["tn"], preferred_element_type=F32)
    return dx, dw


_bdot.defvjp(_bdot_fwd, _bdot_bwd)


def _head_sum(x):
    n = x.shape[-1]
    hi = lax.broadcasted_iota(jnp.int32, (n, n), 0) // HEAD_DIM
    hj = lax.broadcasted_iota(jnp.int32, (n, n), 1) // HEAD_DIM
    e = (hi == hj).astype(F32)
    return jnp.dot(x, e, precision=lax.Precision.HIGHEST, preferred_element_type=F32)


def _softplus(x):
    return jnp.maximum(x, 0.0) + jnp.log1p(jnp.exp(-jnp.abs(x)))


def _layer_norm_rows(z, g, b):
    mu = jnp.mean(z, axis=-1, keepdims=True)
    zc = z - mu
    var = jnp.mean(zc * zc, axis=-1, keepdims=True)
    return zc * lax.rsqrt(var + LN_EPS) * g + b


N_PAIRS = N_HEADS // 2


def _scan_consts():
    k = lax.broadcasted_iota(jnp.int32, (HEAD_DIM, LANES), 0)
    j = lax.broadcasted_iota(jnp.int32, (HEAD_DIM, LANES), 1)
    diag = ((j % HEAD_DIM) == k).astype(F32)
    jj = lax.broadcasted_iota(jnp.int32, (LANES, LANES), 0) // HEAD_DIM
    ll = lax.broadcasted_iota(jnp.int32, (LANES, LANES), 1) // HEAD_DIM
    same_head = (jj == ll).astype(BF16)
    return diag, same_head


def _unrolled(n, body, carry):
    for i in range(n):
        carry = body(i, carry)
    return carry


def _fill_cols(srcs, col_ref, n_steps):
    assert n_steps == 16
    blocks = []
    for src in srcs:
        x = src[...]
        hi = x.astype(BF16).astype(F32)
        r1 = x - hi
        mid = r1.astype(BF16).astype(F32)
        x48 = jnp.concatenate([hi, mid, r1 - mid], axis=0)
        for hp in range(N_PAIRS):
            xp = x48[:, hp * LANES:(hp + 1) * LANES]
            y = jnp.concatenate([xp, pltpu.roll(xp, HEAD_DIM, 1), jnp.zeros((32, LANES), F32)], axis=0)
            blocks.append(y.T[:HEAD_DIM].astype(BF16))
    lhs = jnp.concatenate(blocks, axis=0)
    j = lax.broadcasted_iota(jnp.int32, (LANES, LANES), 0)
    lane_head = lax.broadcasted_iota(jnp.int32, (LANES, LANES), 1) // HEAD_DIM
    for t in range(n_steps):
        pick = jnp.logical_and(j < 96, jnp.logical_and(j % 16 == t, j // 48 == lane_head))
        out = jnp.dot(lhs, pick.astype(BF16), preferred_element_type=F32)
        for vi in range(len(srcs)):
            for hp in range(N_PAIRS):
                r0 = (vi * N_PAIRS + hp) * HEAD_DIM
                col_ref[vi, hp, t] = out[r0:r0 + HEAD_DIM]


def _scan_fwd_call(r, w, k, v, a, b, wsrc):
    t_len = r.shape[0]
    ch = SCAN_CHUNK
    n_ch = t_len // ch
    w_rows, w_cols = wsrc.shape
    half = w_rows // 2
    chunks = _n_chunks(half)
    rc = half // chunks
    transfers = [(q, kk) for q in range(chunks) for kk in range(len(CHIP_FLIPS))]
    hand_on_step = [max(1, min(n_ch - 1, (i + 1) * (n_ch - 8) // len(transfers) + 3)) for i in range(len(transfers))]

    def body(r_ref, w_ref, k_ref, v_ref, a_ref, b_ref, src_ref, y_ref, ck_ref, g_ref,
             st_ref, col_ref, send1, recv1, send2, recv2):
        c = pl.program_id(0)
        me = _me()
        core = me[2]
        sibling = _flip(me, CORE_FLIP)

        def over_ici(q, kk):
            return pltpu.make_async_remote_copy(
                src_ref=src_ref.at[core, pl.ds(q * rc, rc)], dst_ref=g_ref.at[_chip_of(me), core, pl.ds(q * rc, rc)],
                send_sem=send1.at[kk * chunks + q], recv_sem=recv1.at[kk * chunks + q],
                device_id=_flip(me, CHIP_FLIPS[kk]), device_id_type=MESH)

        def hand_on(q, kk):
            landed = g_ref.at[_chip_of(_flip(me, CHIP_FLIPS[kk])), core, pl.ds(q * rc, rc)]
            return pltpu.make_async_remote_copy(
                src_ref=landed, dst_ref=landed, send_sem=send2.at[kk * chunks + q], recv_sem=recv2.at[kk * chunks + q],
                device_id=sibling, device_id_type=MESH)

        @pl.when(c == 0)
        def _():
            st_ref[...] = jnp.zeros_like(st_ref)
            for q, kk in transfers:
                over_ici(q, kk).start()

        for step in sorted(set(hand_on_step)):
            @pl.when(c == step)
            def _(step=step):
                for (q, kk), s in zip(transfers, hand_on_step):
                    if s == step:
                        over_ici(q, kk).wait_recv()
                        hand_on(q, kk).start()

        ck_ref[0] = st_ref[...]
        _fill_cols((w_ref, a_ref, b_ref, k_ref, r_ref), col_ref, ch)

        def step(t, states):
            new = []
            for hp in range(N_PAIRS):
                lanes = slice(hp * LANES, (hp + 1) * LANES)
                s = states[hp]
                sa = jnp.sum(s * col_ref[1, hp, t], axis=0, keepdims=True)
                s = s * col_ref[0, hp, t] + col_ref[2, hp, t] * sa + col_ref[3, hp, t] * v_ref[pl.ds(t, 1), lanes]
                y_ref[pl.ds(t, 1), lanes] = jnp.sum(s * col_ref[4, hp, t], axis=0, keepdims=True)
                new.append(s)
            return tuple(new)

        states = _unrolled(ch, step, tuple(st_ref[hp] for hp in range(N_PAIRS)))
        for hp in range(N_PAIRS):
            st_ref[hp] = states[hp]

        @pl.when(c == n_ch - 1)
        def _():
            for q, kk in transfers:
                hand_on(q, kk).wait_recv()
            for q, kk in transfers:
                over_ici(q, kk).wait_send()
                hand_on(q, kk).wait_send()

    row_spec = pl.BlockSpec((ch, RW_WIDTH), lambda c: (c, 0))
    any_spec = pl.BlockSpec(memory_space=pl.ANY)
    n_sem = len(transfers)
    y, ck, got = pl.pallas_call(
        body, name="rwkv_scan_fwd", grid=(n_ch,),
        in_specs=[row_spec] * 6 + [any_spec],
        out_specs=[row_spec, pl.BlockSpec((1, N_PAIRS, HEAD_DIM, LANES), lambda c: (c, 0, 0, 0)), any_spec],
        out_shape=[jax.ShapeDtypeStruct((t_len, RW_WIDTH), F32),
                   jax.ShapeDtypeStruct((n_ch, N_PAIRS, HEAD_DIM, LANES), F32),
                   jax.ShapeDtypeStruct((N_CHIPS, 2, half, w_cols), wsrc.dtype)],
        scratch_shapes=[pltpu.VMEM((N_PAIRS, HEAD_DIM, LANES), F32),
                        pltpu.VMEM((5, N_PAIRS, ch, HEAD_DIM, LANES), F32)]
        + [pltpu.SemaphoreType.DMA((n_sem,)) for _ in range(4)],
        compiler_params=_cparams(("arbitrary",), VMEM_LIMIT),
    )(r, w, k, v, a, b, wsrc.reshape(2, half, w_cols))
    return y, ck, got.reshape(N_CHIPS, w_rows, w_cols)


def _scan_bwd_call(r, w, k, v, a, b, ck, dy, gsend):
    t_len = r.shape[0]
    ch = SCAN_CHUNK
    n_ch = t_len // ch
    _, g_rows, g_cols = gsend.shape
    chunks = _n_chunks(g_rows)
    rc = g_rows // chunks
    transfers = [(q, kk) for q in range(chunks) for kk in range(len(CHIP_FLIPS))]

    def body(r_ref, w_ref, k_ref, v_ref, a_ref, b_ref, ck_ref, dy_ref, gs_ref,
             dr_ref, dw_ref, dk_ref, dv_ref, da_ref, db_ref, gr_ref,
             ds_ref, col_ref, sp_ref, sa_ref, send_sems, recv_sems):
        c = pl.program_id(0)
        me = _me()

        def to_chip(q, kk):
            peer = _flip(me, CHIP_FLIPS[kk])
            return pltpu.make_async_remote_copy(
                src_ref=gs_ref.at[_chip_of(peer), pl.ds(q * rc, rc)], dst_ref=gr_ref.at[_chip_of(me), pl.ds(q * rc, rc)],
                send_sem=send_sems.at[kk * chunks + q], recv_sem=recv_sems.at[kk * chunks + q],
                device_id=peer, device_id_type=MESH)

        @pl.when(c == 0)
        def _():
            ds_ref[...] = jnp.zeros_like(ds_ref)
            for q, kk in transfers:
                to_chip(q, kk).start()

        diag_f, same_head = _scan_consts()
        _fill_cols((w_ref, a_ref, b_ref, k_ref, r_ref), col_ref, ch)

        def replay(t, states):
            new = []
            for hp in range(N_PAIRS):
                lanes = slice(hp * LANES, (hp + 1) * LANES)
                s = states[hp]
                sp_ref[t, hp] = s
                sa = jnp.sum(s * col_ref[1, hp, t], axis=0, keepdims=True)
                sa_ref[pl.ds(t, 1), lanes] = sa
                new.append(s * col_ref[0, hp, t] + col_ref[2, hp, t] * sa
                           + col_ref[3, hp, t] * v_ref[pl.ds(t, 1), lanes])
            return tuple(new)

        _unrolled(ch, replay, tuple(ck_ref[0, hp] for hp in range(N_PAIRS)))

        def key_rows(ps):
            stacked = jnp.concatenate([p.astype(BF16) for p in ps], axis=0)
            q = jnp.dot(stacked, same_head, preferred_element_type=F32)
            return [jnp.sum(q[i * HEAD_DIM:(i + 1) * HEAD_DIM] * diag_f, axis=0, keepdims=True)
                    for i in range(len(ps))]

        def back(i, grads):
            t = ch - 1 - i
            new = []
            for hp in range(N_PAIRS):
                lanes = slice(hp * LANES, (hp + 1) * LANES)
                wc, ac, bc, kc, rc = (col_ref[vi, hp, t] for vi in range(5))
                sp = sp_ref[t, hp]
                sa = sa_ref[pl.ds(t, 1), lanes]
                vrow = v_ref[pl.ds(t, 1), lanes]
                dyrow = dy_ref[pl.ds(t, 1), lanes]
                st = sp * wc + bc * sa + kc * vrow
                g = grads[hp] + rc * dyrow
                dsa = jnp.sum(g * bc, axis=0, keepdims=True)
                dv_ref[pl.ds(t, 1), lanes] = jnp.sum(g * kc, axis=0, keepdims=True)
                rows = key_rows([st * dyrow, g * vrow, g * sa, g * sp, sp * dsa])
                for out_ref, row in zip((dr_ref, dk_ref, db_ref, dw_ref, da_ref), rows):
                    out_ref[pl.ds(t, 1), lanes] = row
                new.append(g * wc + ac * dsa)
            return tuple(new)

        grads = _unrolled(ch, back, tuple(ds_ref[hp] for hp in range(N_PAIRS)))
        for hp in range(N_PAIRS):
            ds_ref[hp] = grads[hp]

        @pl.when(c == n_ch - 1)
        def _():
            for q, kk in transfers:
                to_chip(q, kk).wait()

    row_spec = pl.BlockSpec((ch, RW_WIDTH), lambda c: (n_ch - 1 - c, 0))
    any_spec = pl.BlockSpec(memory_space=pl.ANY)
    out_sds = jax.ShapeDtypeStruct((t_len, RW_WIDTH), F32)
    n_sem = len(transfers)
    return pl.pallas_call(
        body, name="rwkv_scan_bwd", grid=(n_ch,),
        in_specs=[row_spec] * 6 + [pl.BlockSpec((1, N_PAIRS, HEAD_DIM, LANES), lambda c: (n_ch - 1 - c, 0, 0, 0)),
                                   row_spec, any_spec],
        out_specs=[row_spec] * 6 + [any_spec],
        out_shape=[out_sds] * 6 + [jax.ShapeDtypeStruct(gsend.shape, gsend.dtype)],
        scratch_shapes=[pltpu.VMEM((N_PAIRS, HEAD_DIM, LANES), F32),
                        pltpu.VMEM((5, N_PAIRS, ch, HEAD_DIM, LANES), F32),
                        pltpu.VMEM((ch, N_PAIRS, HEAD_DIM, LANES), F32),
                        pltpu.VMEM((ch, RW_WIDTH), F32),
                        pltpu.SemaphoreType.DMA((n_sem,)), pltpu.SemaphoreType.DMA((n_sem,))],
        compiler_params=_cparams(("arbitrary",), VMEM_LIMIT),
    )(r, w, k, v, a, b, ck, dy, gsend)


def _scan_token(wsrc):
    return jnp.zeros((N_CHIPS,) + wsrc.shape, BF16)


@jax.custom_vjp
def _rwkv_scan(r, w, k, v, a, b, wsrc, gslot):
    y, _, got = _scan_fwd_call(r, w, k, v, a, b, wsrc)
    return y, got, _scan_token(wsrc)


def _rwkv_scan_fwd(r, w, k, v, a, b, wsrc, gslot):
    y, ck, got = _scan_fwd_call(r, w, k, v, a, b, wsrc)
    return (y, got, _scan_token(wsrc)), (r, w, k, v, a, b, ck, wsrc)


def _rwkv_scan_bwd(res, cts):
    *saved, wsrc = res
    dy, _, gsend = cts
    *dins, got = _scan_bwd_call(*saved, dy, gsend)
    summed = _sum_slots_own(got, gsend, _chip_of(_me()), "sum_chips_late")
    return (*dins, jnp.zeros_like(wsrc), summed)


_rwkv_scan.defvjp(_rwkv_scan_fwd, _rwkv_scan_bwd)


ATT_SCALE = HEAD_DIM ** -0.5


def _att_masks():
    qi = lax.broadcasted_iota(jnp.int32, (BLOCK, BLOCK), 0)
    ki = lax.broadcasted_iota(jnp.int32, (BLOCK, BLOCK), 1)
    lane = lax.broadcasted_iota(jnp.int32, (1, LANES), 1)
    return ki <= qi, ki >= qi, lane


def _att_fwd_call(p, bias, g, dil):
    t_len = p.shape[0]
    l_len = t_len // dil
    nb = l_len // BLOCK

    def body(q_ref, kc_ref, kp_ref, vc_ref, vp_ref, bias_ref, o_ref, lse_ref):
        n = pl.program_id(1)
        cur_ok, prev_band, lane = _att_masks()
        prev_ok = jnp.logical_and(prev_band, n > 0)
        for hp in range(N_PAIRS):
            lanes = slice(hp * LANES, (hp + 1) * LANES)
            q2 = q_ref[:, lanes].astype(BF16)
            kc = kc_ref[:, lanes].astype(BF16)
            kp = kp_ref[:, lanes].astype(BF16)
            vc = vc_ref[:, lanes].astype(BF16)
            vp = vp_ref[:, lanes].astype(BF16)
            o2 = jnp.zeros((BLOCK, LANES), F32)
            for hh in range(2):
                h = 2 * hp + hh
                mine = (lane // HEAD_DIM) == hh
                qm = jnp.where(mine, q2, jnp.zeros_like(q2))
                s_c = lax.dot_general(qm, kc, _DOT_DIMS["nt"], preferred_element_type=F32) * ATT_SCALE
                s_p = lax.dot_general(qm, kp, _DOT_DIMS["nt"], preferred_element_type=F32) * ATT_SCALE
                s_c = jnp.where(cur_ok, s_c + bias_ref[h, :, BLOCK:], NEG_BIG)
                s_p = jnp.where(prev_ok, s_p + bias_ref[h, :, :BLOCK], NEG_BIG)
                m = jnp.maximum(jnp.max(s_c, axis=-1, keepdims=True), jnp.max(s_p, axis=-1, keepdims=True))
                e_c = jnp.exp(s_c - m)
                e_p = jnp.exp(s_p - m)
                den = jnp.sum(e_c, axis=-1, keepdims=True) + jnp.sum(e_p, axis=-1, keepdims=True)
                o_h = (jnp.dot((e_c / den).astype(BF16), vc, preferred_element_type=F32)
                       + jnp.dot((e_p / den).astype(BF16), vp, preferred_element_type=F32))
                o2 = o2 + jnp.where(mine, o_h, 0.0)
                lse_ref[h] = jnp.broadcast_to(m + jnp.log(den), (BLOCK, LANES))
            o_ref[:, lanes] = o2

    def col(j):
        return lambda r, n: (r * nb + n, j)

    def col_prev(j):
        return lambda r, n: (r * nb + jnp.maximum(n - 1, 0), j)

    blk = (BLOCK, DIL_WIDTH)
    return pl.pallas_call(
        body, name=f"dil_att_fwd_g{g}", grid=(dil, nb),
        in_specs=[pl.BlockSpec(blk, col(0)), pl.BlockSpec(blk, col(1)), pl.BlockSpec(blk, col_prev(1)),
                  pl.BlockSpec(blk, col(2)), pl.BlockSpec(blk, col_prev(2)),
                  pl.BlockSpec(bias.shape, _const_map(3))],
        out_specs=[pl.BlockSpec(blk, lambda r, n: (r * nb + n, 0)),
                   pl.BlockSpec((N_HEADS, BLOCK, LANES), lambda r, n: (0, r * nb + n, 0))],
        out_shape=[jax.ShapeDtypeStruct((t_len, DIL_WIDTH), F32),
                   jax.ShapeDtypeStruct((N_HEADS, t_len, LANES), F32)],
        compiler_params=_cparams(("arbitrary", "arbitrary"), VMEM_LIMIT),
    )(p, p, p, p, p, bias)


def _att_bwd_call(p, bias, o, lse, do, dlse, g, dil):
    t_len = p.shape[0]
    l_len = t_len // dil
    nb = l_len // BLOCK

    def body(q_ref, qn_ref, k_ref, v_ref, do_ref, don_ref, o_ref, on_ref, lse_ref, lsen_ref, dl_ref, dln_ref,
             bias_ref, dq_ref, dk_ref, dv_ref, dbias_ref, carry_ref):
        r = pl.program_id(0)
        n = pl.program_id(1)
        cur_ok, prev_band, lane = _att_masks()
        has_next = n + 1 < nb

        @pl.when(jnp.logical_and(r == 0, n == 0))
        def _():
            dbias_ref[...] = jnp.zeros_like(dbias_ref)

        @pl.when(n == 0)
        def _():
            carry_ref[...] = jnp.zeros_like(carry_ref)

        for hp in range(N_PAIRS):
            lanes = slice(hp * LANES, (hp + 1) * LANES)
            k2 = k_ref[:, lanes].astype(BF16)
            v2 = v_ref[:, lanes].astype(BF16)
            dk2 = jnp.zeros((BLOCK, LANES), F32)
            dv2 = jnp.zeros((BLOCK, LANES), F32)
            dq_cur = carry_ref[:, lanes]
            dq_next = jnp.zeros((BLOCK, LANES), F32)
            for hh in range(2):
                h = 2 * hp + hh
                mine = (lane // HEAD_DIM) == hh
                tiles = (
                    (q_ref, do_ref, o_ref, lse_ref, dl_ref, cur_ok, slice(BLOCK, 2 * BLOCK), None),
                    (qn_ref, don_ref, on_ref, lsen_ref, dln_ref, prev_band, slice(0, BLOCK), has_next),
                )
                if nb == 1:
                    tiles = tiles[:1]
                for ti, (qr, dor, orf, lr, dlr, ok, bcols, gate) in enumerate(tiles):
                    q2 = qr[:, lanes].astype(BF16)
                    qm = jnp.where(mine, q2, jnp.zeros_like(q2))
                    do_f = jnp.where(mine, dor[:, lanes], 0.0)
                    dom = do_f.astype(BF16)
                    s = lax.dot_general(qm, k2, _DOT_DIMS["nt"], preferred_element_type=F32) * ATT_SCALE
                    s = s + bias_ref[h, :, bcols]
                    if gate is not None:
                        ok = jnp.logical_and(ok, gate)
                    pr = jnp.where(ok, jnp.exp(jnp.minimum(s - lr[h], 0.0)), 0.0)
                    dp = lax.dot_general(dom, v2, _DOT_DIMS["nt"], preferred_element_type=F32)
                    delta = jnp.sum(do_f * orf[:, lanes], axis=-1, keepdims=True)
                    dl = jnp.sum(dlr[h], axis=-1, keepdims=True)
                    ds = pr * (dp - delta + dl)
                    dsb = ds.astype(BF16)
                    dq_h = jnp.where(mine, jnp.dot(dsb, k2, preferred_element_type=F32), 0.0) * ATT_SCALE
                    if ti == 0:
                        dq_cur = dq_cur + dq_h
                    else:
                        dq_next = dq_next + dq_h
                    dk2 = dk2 + lax.dot_general(dsb, qm, _DOT_DIMS["tn"], preferred_element_type=F32) * ATT_SCALE
                    dv2 = dv2 + lax.dot_general(pr.astype(BF16), dom, _DOT_DIMS["tn"], preferred_element_type=F32)
                    dbias_ref[h, :, bcols] += ds
            dq_ref[:, lanes] = dq_cur
            carry_ref[:, lanes] = dq_next
            dk_ref[:, lanes] = dk2
            dv_ref[:, lanes] = dv2

    def nxt(n):
        return jnp.minimum(n + 1, nb - 1)

    blk = (BLOCK, DIL_WIDTH)
    hblk = (N_HEADS, BLOCK, LANES)
    qcol = lambda j: (lambda r, n: (r * nb + n, j))
    q_next = lambda r, n: (r * nb + nxt(n), 0)
    rown = lambda r, n: (r * nb + n, 0)
    rown_next = lambda r, n: (r * nb + nxt(n), 0)
    hrow = lambda r, n: (0, r * nb + n, 0)
    hrow_next = lambda r, n: (0, r * nb + nxt(n), 0)
    sds = jax.ShapeDtypeStruct((t_len, DIL_WIDTH), F32)
    return pl.pallas_call(
        body, name=f"dil_att_bwd_g{g}", grid=(dil, nb),
        in_specs=[pl.BlockSpec(blk, qcol(0)), pl.BlockSpec(blk, q_next),
                  pl.BlockSpec(blk, qcol(1)), pl.BlockSpec(blk, qcol(2)),
                  pl.BlockSpec(blk, rown), pl.BlockSpec(blk, rown_next),
                  pl.BlockSpec(blk, rown), pl.BlockSpec(blk, rown_next),
                  pl.BlockSpec(hblk, hrow), pl.BlockSpec(hblk, hrow_next),
                  pl.BlockSpec(hblk, hrow), pl.BlockSpec(hblk, hrow_next),
                  pl.BlockSpec(bias.shape, _const_map(3))],
        out_specs=[pl.BlockSpec(blk, rown)] * 3 + [pl.BlockSpec(bias.shape, _const_map(3))],
        out_shape=[sds, sds, sds, jax.ShapeDtypeStruct(bias.shape, F32)],
        scratch_shapes=[pltpu.VMEM((BLOCK, DIL_WIDTH), F32)],
        compiler_params=_cparams(("arbitrary", "arbitrary"), VMEM_LIMIT),
    )(p, p, p, p, do, do, o, o, lse, lse, dlse, dlse, bias)


def _att_all_groups(ps, biases):
    outs = [_att_fwd_call(ps[g], biases[g], g, dil) for g, (_, dil) in enumerate(DIL_PATTERNS)]
    return tuple(o for o, _ in outs), tuple(l for _, l in outs)


@jax.custom_vjp
def _dilated_attention(ps, biases):
    return _att_all_groups(ps, biases)


def _dilated_attention_fwd(ps, biases):
    os_, lses = _att_all_groups(ps, biases)
    return (os_, lses), (ps, biases, os_, lses)


def _dilated_attention_bwd(res, cts):
    ps, biases, os_, lses = res
    dos, dlses = cts
    dps, dbiases = [], []
    for g, (_, dil) in enumerate(DIL_PATTERNS):
        dq, dk, dv, dbias = _att_bwd_call(ps[g], biases[g], os_[g], lses[g], dos[g], dlses[g], g, dil)
        dps.append(jnp.concatenate([dq, dk, dv], axis=1))
        dbiases.append(dbias)
    return tuple(dps), tuple(dbiases)


_dilated_attention.defvjp(_dilated_attention_fwd, _dilated_attention_bwd)


def _me():
    return lax.axis_index("x"), lax.axis_index("y"), lax.axis_index("c")


def _flip(me, f):
    return tuple((1 - m) if b else m for m, b in zip(me, f))


def _chip_of(d):
    return 2 * d[0] + d[1]


def _dev_of(d):
    return 4 * d[0] + 2 * d[1] + d[2]


EXCHANGE_CHUNKS = 8
CHIP_FLIPS = ((1, 0, 0), (0, 1, 0), (1, 1, 0))
ALL_FLIPS = tuple((a, b, c) for a in (0, 1) for b in (0, 1) for c in (0, 1) if a or b or c)
CORE_FLIP = (0, 0, 1)


def _n_chunks(rows):
    return EXCHANGE_CHUNKS if rows % (EXCHANGE_CHUNKS * PACK_ROWS) == 0 else 1


def _exchange(src, n_slots, transfers, name):
    _, rows, cols = src.shape
    chunks = _n_chunks(rows)
    rc = rows // chunks
    n = len(transfers) * chunks

    def body(src_ref, dst_ref, send_sems, recv_sems):
        me = _me()
        copies = []
        for q in range(chunks):
            for kk, (f, src_slot, dst_slot) in enumerate(transfers):
                peer = _flip(me, f)
                cp = pltpu.make_async_remote_copy(
                    src_ref=src_ref.at[src_slot(me, peer), pl.ds(q * rc, rc)],
                    dst_ref=dst_ref.at[dst_slot(me, peer), pl.ds(q * rc, rc)],
                    send_sem=send_sems.at[kk * chunks + q], recv_sem=recv_sems.at[kk * chunks + q],
                    device_id=peer, device_id_type=MESH)
                cp.start()
                copies.append(cp)
        for cp in copies:
            cp.wait()

    return pl.pallas_call(
        body, name=name,
        out_shape=jax.ShapeDtypeStruct((n_slots, rows, cols), src.dtype),
        in_specs=[pl.BlockSpec(memory_space=pl.ANY)],
        out_specs=pl.BlockSpec(memory_space=pl.ANY),
        scratch_shapes=[pltpu.SemaphoreType.DMA((n,)), pltpu.SemaphoreType.DMA((n,))],
    )(src)


def _set_slot(buf, block, index):
    return lax.dynamic_update_slice_in_dim(buf, block[None].astype(buf.dtype), index, axis=0)


def _chip_all_gather(src, name):
    got = _exchange(src[None], N_CHIPS, [(f, lambda me, peer: 0, lambda me, peer: _chip_of(me)) for f in CHIP_FLIPS], name)
    return _set_slot(got, src, _chip_of(_me()))


def _dev_all_gather(src, name):
    got = _exchange(src[None], N_DEV, [(f, lambda me, peer: 0, lambda me, peer: _dev_of(me)) for f in ALL_FLIPS], name)
    return _set_slot(got, src, _dev_of(_me()))


def _chip_scatter(src, name):
    got = _exchange(src, N_CHIPS, [(f, lambda me, peer: _chip_of(peer), lambda me, peer: _chip_of(me))
                                   for f in CHIP_FLIPS], name)
    chip = _chip_of(_me())
    return _set_slot(got, lax.dynamic_index_in_dim(src, chip, 0, keepdims=False), chip)


def _core_halves(src, name):
    s, _, half, cols = src.shape
    transfers = [(CORE_FLIP, (lambda me, peer, j=j: 2 * j + peer[2]), (lambda me, peer, j=j: j)) for j in range(s)]
    return _exchange(src.reshape(2 * s, half, cols), s, transfers, name)


def _core_all_gather(src, name):
    got = _exchange(src[None], 2, [(CORE_FLIP, lambda me, peer: 0, lambda me, peer: me[2])], name)
    return _set_slot(got, src, _me()[2])


def _two_level_gather(src, name):
    rows, cols = src.shape
    half = rows // 2
    chunks = _n_chunks(half)
    rc = half // chunks
    n = len(CHIP_FLIPS) * chunks

    def body(src_ref, g_ref, send1, recv1, send2, recv2):
        me = _me()
        c = me[2]
        sibling = _flip(me, CORE_FLIP)
        first, second = [], []
        for q in range(chunks):
            for kk, f in enumerate(CHIP_FLIPS):
                peer = _flip(me, f)
                cp = pltpu.make_async_remote_copy(
                    src_ref=src_ref.at[c, pl.ds(q * rc, rc)], dst_ref=g_ref.at[_chip_of(me), c, pl.ds(q * rc, rc)],
                    send_sem=send1.at[kk * chunks + q], recv_sem=recv1.at[kk * chunks + q],
                    device_id=peer, device_id_type=MESH)
                cp.start()
                first.append((cp, _chip_of(peer), kk * chunks + q, q))
        for cp, origin, idx, q in first:
            cp.wait_recv()
            fw = pltpu.make_async_remote_copy(
                src_ref=g_ref.at[origin, c, pl.ds(q * rc, rc)], dst_ref=g_ref.at[origin, c, pl.ds(q * rc, rc)],
                send_sem=send2.at[idx], recv_sem=recv2.at[idx],
                device_id=sibling, device_id_type=MESH)
            fw.start()
            second.append(fw)
        for fw in second:
            fw.wait_recv()
        for cp, _, _, _ in first:
            cp.wait_send()
        for fw in second:
            fw.wait_send()

    got = pl.pallas_call(
        body, name=name,
        out_shape=jax.ShapeDtypeStruct((N_CHIPS, 2, half, cols), src.dtype),
        in_specs=[pl.BlockSpec(memory_space=pl.ANY)],
        out_specs=pl.BlockSpec(memory_space=pl.ANY),
        scratch_shapes=[pltpu.SemaphoreType.DMA((n,)) for _ in range(4)],
    )(src.reshape(2, half, cols))
    return _set_slot(got.reshape(N_CHIPS, rows, cols), src, _chip_of(_me()))


def _sum_slots(x, name):
    s, rows, cols = x.shape
    tile = _pick(rows, (512, 256, 128, 64, 32, 16, 8))

    def body(x_ref, o_ref):
        acc = x_ref[0].astype(F32)
        for i in range(1, s):
            acc = acc + x_ref[i].astype(F32)
        o_ref[...] = acc

    return pl.pallas_call(
        body, name=name, grid=(rows // tile,),
        in_specs=[pl.BlockSpec((s, tile, cols), lambda i: (0, i, 0))],
        out_specs=pl.BlockSpec((tile, cols), lambda i: (i, 0)),
        out_shape=jax.ShapeDtypeStruct((rows, cols), F32),
        compiler_params=_cparams(("parallel",), VMEM_LIMIT),
    )(x)


def _sum_slots_own(recv, send, chip, name):
    s, rows, cols = recv.shape
    tile = _pick(rows, (512, 256, 128, 64, 32, 16, 8))

    def body(chip_ref, recv_ref, own_ref, o_ref):
        acc = None
        for j in range(s):
            term = jnp.where(chip_ref[0] == j, own_ref[0], recv_ref[j]).astype(F32)
            acc = term if acc is None else acc + term
        o_ref[...] = acc

    grid_spec = pltpu.PrefetchScalarGridSpec(
        num_scalar_prefetch=1, grid=(rows // tile,),
        in_specs=[pl.BlockSpec((s, tile, cols), lambda i, c: (0, i, 0)),
                  pl.BlockSpec((1, tile, cols), lambda i, c: (c[0], i, 0))],
        out_specs=pl.BlockSpec((tile, cols), lambda i, c: (i, 0)))
    return pl.pallas_call(
        body, name=name, grid_spec=grid_spec,
        out_shape=jax.ShapeDtypeStruct((rows, cols), F32),
        compiler_params=_cparams(("arbitrary",), VMEM_LIMIT),
    )(jnp.reshape(chip, (1,)).astype(jnp.int32), recv, send)


def _add_pairs(a, b, name):
    s, rows, cols = a.shape
    tile = _pick(rows, (512, 256, 128, 64, 32, 16, 8))

    def body(a_ref, b_ref, o_ref):
        o_ref[...] = (a_ref[...].astype(F32) + b_ref[...].astype(F32)).astype(o_ref.dtype)

    spec = pl.BlockSpec((1, tile, cols), lambda j, i: (j, i, 0))
    return pl.pallas_call(
        body, name=name, grid=(s, rows // tile),
        in_specs=[spec, spec], out_specs=spec,
        out_shape=jax.ShapeDtypeStruct(a.shape, a.dtype),
        compiler_params=_cparams(("parallel", "parallel"), VMEM_LIMIT),
    )(a, b)


def _adamw(w, g, m, v, name):
    rows, cols = w.shape
    tile = rows
    if rows * cols * 4 > 2 * 1024 * 1024:
        tile = _pick(rows, (256, 128, 64, 32, 16, 8))
    c1 = 1.0 / (1.0 - ADAM_B1 ** ADAM_STEP)
    c2 = 1.0 / (1.0 - ADAM_B2 ** ADAM_STEP)

    def body(w_ref, g_ref, m_ref, v_ref, d_ref, nm_ref, nv_ref):
        gv = g_ref[...]
        nm = ADAM_B1 * m_ref[...] + (1.0 - ADAM_B1) * gv
        nv = ADAM_B2 * v_ref[...] + (1.0 - ADAM_B2) * (gv * gv)
        m_hat = nm * c1
        v_hat = nv * c2
        d_ref[...] = -ADAM_LR * (m_hat / (jnp.sqrt(v_hat) + ADAM_EPS) + ADAM_WD * w_ref[...])
        nm_ref[...] = nm
        nv_ref[...] = nv

    spec = pl.BlockSpec((tile, cols), lambda i: (i, 0))
    sds = jax.ShapeDtypeStruct((rows, cols), F32)
    return pl.pallas_call(
        body, name=name, grid=(rows // tile,),
        in_specs=[spec] * 4, out_specs=[spec] * 3, out_shape=[sds] * 3,
        compiler_params=_cparams(("parallel",), VMEM_LIMIT),
    )(w, g, m, v)


def _ada_fwd(c_all, ada_w, ada_b_cols):
    n_col = ada_w.shape[2]

    def body(c_ref, w_ref, b_ref, o_ref):
        cv = c_ref[...]
        cond = (cv * jax.nn.sigmoid(cv)).astype(BF16)
        o_ref[0] = jnp.dot(cond, w_ref[0].astype(BF16), preferred_element_type=F32) + b_ref[0]

    return pl.pallas_call(
        body, name="ada_fwd", grid=(DEPTH,),
        in_specs=[pl.BlockSpec(c_all.shape, lambda i: (0, 0)),
                  pl.BlockSpec((1, D_MODEL, n_col), lambda i: (i, 0, 0)),
                  pl.BlockSpec((1, 1, n_col), lambda i: (i, 0, 0))],
        out_specs=pl.BlockSpec((1, N_DEV, n_col), lambda i: (i, 0, 0)),
        out_shape=jax.ShapeDtypeStruct((DEPTH, N_DEV, n_col), F32),
        compiler_params=_cparams(("parallel",), VMEM_LIMIT),
    )(c_all, ada_w, ada_b_cols)


def _ada_grad(c_all_t, dmod_cols):
    n_col = dmod_cols.shape[2]

    def body(c_ref, d_ref, o_ref):
        cv = c_ref[...]
        cond = cv * jax.nn.sigmoid(cv)
        o_ref[0] = jnp.dot(cond, d_ref[0], precision=lax.Precision.HIGHEST, preferred_element_type=F32)

    return pl.pallas_call(
        body, name="ada_grad", grid=(DEPTH,),
        in_specs=[pl.BlockSpec(c_all_t.shape, lambda i: (0, 0)),
                  pl.BlockSpec((1, LANES, n_col), lambda i: (i, 0, 0))],
        out_specs=pl.BlockSpec((1, D_MODEL, n_col), lambda i: (i, 0, 0)),
        out_shape=jax.ShapeDtypeStruct((DEPTH, D_MODEL, n_col), F32),
        compiler_params=_cparams(("parallel",), VMEM_LIMIT),
    )(c_all_t, dmod_cols)


ROW_TILE = 256


def _shift_rows(a, n=1):
    return jnp.pad(a, ((n, 0), (0, 0)))[:-n]


def _modulate(x, sc, sh, name):
    def fn(x, sc, sh):
        return (x * (1.0 + sc) + sh,)

    t = x.shape[0]
    return _fused(fn, [_tiled(x, ROW_TILE), _shared(sc), _shared(sh)],
                  [_tiled_out(t, D_MODEL, ROW_TILE)], (t // ROW_TILE,), name)[0]


def _resid_ln_mod(x, y, gate, ln_g, ln_b, sc, sh, name):
    def fn(x, y, gate, ln_g, ln_b, sc, sh):
        x1 = _layer_norm_rows(ALPHA * x + (1.0 + gate) * y, ln_g, ln_b)
        return x1, x1 * (1.0 + sc) + sh

    t = x.shape[0]
    return _fused(fn, [_tiled(x, ROW_TILE), _tiled(y, ROW_TILE)] + [_shared(a) for a in (gate, ln_g, ln_b, sc, sh)],
                  [_tiled_out(t, D_MODEL, ROW_TILE)] * 2, (t // ROW_TILE,), name)


def _resid_ln_loss(x, y, gate, ln_g, ln_b, target, name):
    def fn(x, y, gate, ln_g, ln_b, target):
        x1 = _layer_norm_rows(ALPHA * x + (1.0 + gate) * y, ln_g, ln_b)
        err = jnp.square(x1 - target)
        per_row = jnp.mean(err, axis=-1, keepdims=True)
        return (0.5 * jnp.sum(per_row, axis=0, keepdims=True),)

    t = x.shape[0]
    return _fused(fn, [_tiled(x, ROW_TILE), _tiled(y, ROW_TILE)] + [_shared(a) for a in (gate, ln_g, ln_b)]
                  + [_tiled(target, ROW_TILE)],
                  [((1, 1), (1, 1), _const_map(2), "a")], (t // ROW_TILE,), name)[0]


def _mlp(u, w1, s1, w2, s2, name):
    def run(u, w1, w2):
        h, act = _mm(u, w1, "nn", name + "_w1_fwd", square_relu=True)
        return _mm(act, w2, "nn", name + "_w2_fwd"), (u, w1, w2, h, act)

    @jax.custom_vjp
    def op(u, w1, s1, w2, s2):
        return run(u, w1, w2)[0]

    def fwd(u, w1, s1, w2, s2):
        return run(u, w1, w2)

    def bwd(res, dy):
        u, w1, w2, h, act = res
        dh = _mm(dy, w2, "nt", name + "_w2_dx", pre_act=h)
        dw2 = _mm(act, dy, "tn", name + "_w2_dw", narrow_out=True)
        du = _mm(dh, w1, "nt", name + "_w1_dx")
        dw1 = _mm(u, dh, "tn", name + "_w1_dw", narrow_out=True)
        return du, jnp.zeros_like(w1), dw1, jnp.zeros_like(w2), dw2

    op.defvjp(fwd, bwd)
    return op(u, w1, s1, w2, s2)


AB_PIECES = (("r", 0, 512, 512), ("k", 512, 512, 512), ("v", 1024, 512, 512),
             ("wd", 1536, 64, 128), ("ad", 1600, 64, 128), ("gd", 1664, 160, 256),
             ("h", 1824, 512, 512), ("bg", 2336, 512, 512), ("cg", 2848, 512, 512))
AB_PAD_COLS = sum(p[3] for p in AB_PIECES)


def _regroup_cols(w):
    parts = []
    for _, start, width, padded in AB_PIECES:
        piece = w[..., start:start + width]
        if padded != width:
            piece = jnp.pad(piece, [(0, 0)] * (w.ndim - 1) + [(0, padded - width)])
        parts.append(piece)
    return jnp.concatenate(parts, axis=-1)


def _pad_rows(w, rows):
    return jnp.pad(w, ((0, rows - w.shape[0]), (0, 0)))


def _rwkv_shortconv(u, big, wts, wsrc, gslot, late_shapes):
    t = u.shape[0]
    p = _linear(u, _regroup_cols(big["ab_w_in"][0]), _regroup_cols(wts["ab_w_in"][0]), "ab_in")
    mu = _regroup_cols(jnp.pad(wts["rw_mu"], ((0, 0), (0, AB_PROJ - RW_PROJ))))
    w_up = _pad_rows(wts["rw_w_up"][0], 128)
    a_up = _pad_rows(wts["rw_a_up"][0], 128)
    g_up = _pad_rows(wts["rw_g_up"][0], 256)

    def pre(rp, rs, kp, ks, vp, vs, wdp, wds, adp, ads, gdp, gds, h, cg,
            mu_r, mu_k, mu_v, mu_w, mu_a, mu_g, w0, w_up, a0, a_up, g_up, k_k, k_a):
        def mix(pv, sv, m):
            return pv + m * (sv - pv)

        r, k, v = mix(rp, rs, mu_r), mix(kp, ks, mu_k), mix(vp, vs, mu_v)
        wd, ad, gd = mix(wdp, wds, mu_w), mix(adp, ads, mu_a), mix(gdp, gds, mu_g)
        logw = -_softplus(-(w0 + _bdot(jnp.tanh(wd), w_up))) - 0.5
        decay = jnp.exp(-jnp.exp(logw))
        iclr = jax.nn.sigmoid(a0 + _bdot(ad, a_up))
        gate = _bdot(jax.nn.sigmoid(gd), g_up)
        kk = k * k_k
        kk = kk / jnp.maximum(jnp.sqrt(_head_sum(kk * kk)), 1e-12)
        k_h = k * (1.0 + (iclr - 1.0) * k_a)
        return r, decay, k_h, v, -kk, kk * iclr, gate, cg * h

    tile = ROW_TILE
    names = [q[0] for q in AB_PIECES]
    cuts = list(np.cumsum([q[3] for q in AB_PIECES])[:-1])
    pp = dict(zip(names, jnp.split(p, cuts, axis=1)))
    mp = dict(zip(names, jnp.split(mu, cuts, axis=1)))

    ins = []
    for name in ("r", "k", "v", "wd", "ad", "gd"):
        ins += [_tiled(pp[name], tile), _tiled(_shift_rows(pp[name]), tile)]
    ins += [_tiled(pp["h"], tile), _tiled(pp["cg"], tile)]
    ins += [_shared(mp[name]) for name in ("r", "k", "v", "wd", "ad", "gd")]
    ins += [_shared(a) for a in (wts["rw_w0"], w_up, wts["rw_a0"], a_up, g_up, wts["rw_k_k"], wts["rw_k_a"])]
    outs = [_tiled_out(t, RW_WIDTH, tile)] * 8
    r, decay, k_h, v, a, b, gate, z = _fused(pre, ins, outs, (t // tile,), "rwkv_pre")

    y, gathered, token = _rwkv_scan(r, decay, k_h, v, a, b, wsrc, gslot)
    shard_shapes = [s[:SHARDED[n]] + (s[SHARDED[n]] // N_CHIPS,) + s[SHARDED[n] + 1:] for n, s in late_shapes.items()]
    late_big = {n: _join_chips(part, SHARDED[n])
                for n, part in zip(late_shapes, _unpack_chips(gathered, shard_shapes, own=wsrc))}
    late_slots = _gradient_slots(token, late_shapes, tuple(late_shapes))

    conv_w = wts["sc_conv_w"][0]
    r_k = wts["rw_r_k"].reshape(1, RW_WIDTH)

    def post(y, r, k_h, v, gate, bg, z, z1, z2, lnx_g, lnx_b, r_k, c0, c1, c2):
        mean = _head_sum(y) * (1.0 / HEAD_DIM)
        yc = y - mean
        var = _head_sum(yc * yc) * (1.0 / HEAD_DIM)
        yn = yc * lax.rsqrt(var + RW_GN_EPS) * lnx_g + lnx_b
        bonus = _head_sum(r * k_h * r_k) * v
        return (yn + bonus) * gate, bg * (c0 * z2 + c1 * z1 + c2 * z)

    ins = [_tiled(a_, tile) for a_ in (y, r, k_h, v, gate, pp["bg"])]
    ins += [_tiled(a_, tile) for a_ in (z, _shift_rows(z, 1), _shift_rows(z, 2))]
    ins += [_shared(a_) for a_ in (wts["rw_lnx_g"], wts["rw_lnx_b"], r_k, conv_w[0:1], conv_w[1:2], conv_w[2:3])]
    y_a, y_b = _fused(post, ins, [_tiled_out(t, RW_WIDTH, tile)] * 2, (t // tile,), "rwkv_post")
    out = _linear(jnp.concatenate([y_a, y_b], axis=1), late_big["ab_w_out"][0], late_slots["ab_w_out"][0], "ab_out")
    return out, late_big, late_slots


def _t5_bucket_np(dist):
    exact = N_BUCKETS // 2
    logd = np.log(np.maximum(dist, 1).astype(np.float32) / exact) / math.log(MAX_DISTANCE / exact)
    large = np.minimum(exact + (logd * (N_BUCKETS - exact)).astype(np.int32), N_BUCKETS - 1)
    return np.where(dist < exact, dist, large)


def _merge_groups(os_, lses, name):
    t = os_[0].shape[0]
    tile = ROW_TILE

    def fn(o0, o1, o2, l0, l1, l2):
        lane = lax.broadcasted_iota(jnp.int32, (1, LANES), 1)
        lo = lane < HEAD_DIM
        ls = [jnp.where(lo, l[0], l[1]) for l in (l0, l1, l2)]
        m = jnp.maximum(jnp.maximum(ls[0], ls[1]), ls[2])
        es = [jnp.exp(l - m) for l in ls]
        den = es[0] + es[1] + es[2]
        return ((es[0] * o0 + es[1] * o1 + es[2] * o2) / den,)

    ins = [(o, (tile, LANES), lambda i, hp: (i, hp), "t") for o in os_]
    ins += [(l, (2, tile, LANES), lambda i, hp: (hp, i, 0), "t") for l in lses]
    outs = [((t, DIL_WIDTH), (tile, LANES), lambda i, hp: (i, hp), "t")]
    return _fused(fn, ins, outs, (t // tile, N_PAIRS), name)[0]


def _residue_major(a, dil, axis=0):
    if dil == 1:
        return a
    shp = a.shape
    split = a.reshape(shp[:axis] + (shp[axis] // dil, dil) + shp[axis + 1:])
    return jnp.swapaxes(split, axis, axis + 1).reshape(shp)


def _position_major(a, dil, axis=0):
    if dil == 1:
        return a
    shp = a.shape
    split = a.reshape(shp[:axis] + (dil, shp[axis] // dil) + shp[axis + 1:])
    return jnp.swapaxes(split, axis, axis + 1).reshape(shp)


def _dilated_mixer(u, big, wts):
    group_cols = 3 * DIL_WIDTH
    ps = []
    for g, (_, dil) in enumerate(DIL_PATTERNS):
        cols = slice(g * group_cols, (g + 1) * group_cols)
        ps.append(_linear(_residue_major(u, dil), big["dil_w_qkv"][0][:, cols], wts["dil_w_qkv"][0][:, cols],
                          f"dil_qkv{g}"))
    qi = np.arange(BLOCK)[:, None]
    ki = np.arange(2 * BLOCK)[None, :]
    rel = BLOCK + qi - ki
    biases = []
    for g, (window, dil) in enumerate(DIL_PATTERNS):
        span = window // dil
        bucket = _t5_bucket_np(np.clip(rel, 0, span) * dil).reshape(-1)
        onehot = jnp.asarray(np.eye(N_BUCKETS, dtype=np.float32)[bucket])
        table = wts["rel_bias"][:, g * N_HEADS:(g + 1) * N_HEADS]
        bias = jnp.dot(onehot, table, precision=lax.Precision.HIGHEST)
        biases.append(jnp.transpose(bias.reshape(BLOCK, 2 * BLOCK, N_HEADS), (2, 0, 1)))
    os_, lses = _dilated_attention(tuple(ps), tuple(biases))
    os_ = [_position_major(o, dil) for o, (_, dil) in zip(os_, DIL_PATTERNS)]
    lses = [_position_major(l, dil, axis=1) for l, (_, dil) in zip(lses, DIL_PATTERNS)]
    o = _merge_groups(os_, lses, "dil_merge")
    return _linear(o, big["dil_w_out"][0], wts["dil_w_out"][0], "dil_out")


def _forward_local(x, mods, big, wts, wsrc, gslot, late_shapes, target):
    u = _modulate(x, mods[0, 1], mods[0, 0], "mod_in")
    for i in range(DEPTH):
        sh2, sc2, g1, g2 = mods[i, 3], mods[i, 4], mods[i, 2], mods[i, 5]
        if i == 0:
            y, late_big, late_slots = _rwkv_shortconv(u, big, wts, wsrc, gslot, late_shapes)
            big = {**big, **late_big}
            wts = {**wts, **late_slots}
        else:
            y = _dilated_mixer(u, big, wts)
        x, u = _resid_ln_mod(x, y, g1, wts["ln_g"][i, 0:1], wts["ln_b"][i, 0:1], sc2, sh2, f"ln_mix{i}")
        y = _mlp(u, big["mlp_w1"][i], wts["mlp_w1"][i], big["mlp_w2"][i], wts["mlp_w2"][i], f"mlp{i}")
        if i + 1 < DEPTH:
            x, u = _resid_ln_mod(x, y, g2, wts["ln_g"][i, 1:2], wts["ln_b"][i, 1:2],
                                 mods[i + 1, 1], mods[i + 1, 0], f"ln_mlp{i}")
        else:
            return _resid_ln_loss(x, y, g2, wts["ln_g"][i, 1:2], wts["ln_b"][i, 1:2], target, "ln_loss")


SHARDED = {"ab_w_in": 2, "ab_w_out": 1, "dil_w_qkv": 2, "dil_w_out": 2, "mlp_w1": 2, "mlp_w2": 1,
           "ln_g": 2, "ln_b": 2, "rw_w_up": 2, "rw_a_up": 2, "rw_g_up": 2, "sc_conv_w": 2}
FIRST_MIXER = ("ab_w_in",)
LATER_LAYERS = ("ab_w_out", "dil_w_qkv", "dil_w_out", "mlp_w1", "mlp_w2")
SMALL_SHARDED = ("ln_g", "ln_b", "rw_w_up", "rw_a_up", "rw_g_up", "sc_conv_w")
REPLICATED = ("ada_b", "rw_mu", "rw_w0", "rw_a0", "rw_k_k", "rw_k_a", "rw_r_k", "rw_lnx_g", "rw_lnx_b", "rel_bias")
WEIGHT_ORDER = ("ada_w", "ada_b", "ln_g", "ln_b", "ab_w_in", "rw_mu", "rw_w0", "rw_w_up", "rw_a0", "rw_a_up",
                "rw_g_up", "rw_k_k", "rw_k_a", "rw_r_k", "rw_lnx_g", "rw_lnx_b", "sc_conv_w", "ab_w_out",
                "dil_w_qkv", "dil_w_out", "rel_bias", "mlp_w1", "mlp_w2")


PACK_ROWS = 16


def _rows_of(n_elems):
    return -(-n_elems // (ROW_W * PACK_ROWS)) * PACK_ROWS


def _to_rows(a):
    flat = a.reshape(-1)
    rows = _rows_of(flat.shape[0])
    if rows * ROW_W != flat.shape[0]:
        flat = jnp.pad(flat, (0, rows * ROW_W - flat.shape[0]))
    return flat.reshape(rows, ROW_W)


def _from_rows(rows, shape):
    n = int(np.prod(shape))
    return rows.reshape(-1)[:n].reshape(shape)


def _split_chips(full, axis):
    shp = full.shape
    parts = full.reshape(shp[:axis] + (N_CHIPS, shp[axis] // N_CHIPS) + shp[axis + 1:])
    return jnp.moveaxis(parts, axis, 0)


def _join_chips(parts, axis):
    moved = jnp.moveaxis(parts, 0, axis)
    shp = moved.shape
    return moved.reshape(shp[:axis] + (shp[axis] * shp[axis + 1],) + shp[axis + 2:])


def _pack_rows(arrays, row_multiple=256):
    blocks = [_to_rows(a) for a in arrays]
    total = sum(b.shape[0] for b in blocks)
    pad = (-total) % row_multiple
    if pad:
        blocks.append(jnp.zeros((pad, ROW_W), blocks[0].dtype))
    return jnp.concatenate(blocks, axis=0)


def _unpack_rows(buf, shapes):
    out, r0 = [], 0
    for shp in shapes:
        n = _rows_of(int(np.prod(shp)))
        out.append(_from_rows(buf[r0:r0 + n], shp))
        r0 += n
    return out


def _pack_chips(parts):
    blocks = []
    for p in parts:
        flat = p.reshape(N_CHIPS, -1)
        rows = _rows_of(flat.shape[1])
        if rows * ROW_W != flat.shape[1]:
            flat = jnp.pad(flat, ((0, 0), (0, rows * ROW_W - flat.shape[1])))
        blocks.append(flat.reshape(N_CHIPS, rows, ROW_W))
    total = sum(b.shape[1] for b in blocks)
    pad = (-total) % 256
    if pad:
        blocks.append(jnp.zeros((N_CHIPS, pad, ROW_W), blocks[0].dtype))
    return jnp.concatenate(blocks, axis=1)


def _unpack_chips(buf, shapes, own=None):
    mine = lax.broadcasted_iota(jnp.int32, (N_CHIPS, 1, 1), 0) == _chip_of(_me()) if own is not None else None
    out, r0 = [], 0
    for shp in shapes:
        size = int(np.prod(shp))
        n = _rows_of(size)
        rows = buf[:, r0:r0 + n]
        if own is not None:
            rows = jnp.where(mine, own[None, r0:r0 + n], rows)
        out.append(rows.reshape(N_CHIPS, -1)[:, :size].reshape((N_CHIPS,) + tuple(shp)))
        r0 += n
    return out


def _gradient_slots(token, full_shapes, names):
    @jax.custom_vjp
    def route(token):
        return {n: jnp.zeros(full_shapes[n], BF16) for n in names}

    def fwd(token):
        return {n: jnp.zeros(full_shapes[n], BF16) for n in names}, None

    def bwd(_, d):
        return (_pack_chips([_split_chips(d[n], SHARDED[n]).astype(BF16) for n in names]),)

    route.defvjp(fwd, bwd)
    return route(token)


def _as2d(a):
    return a.reshape(-1, a.shape[-1])


def kernel(x, c, ada_w, ada_b, ln_g, ln_b, ab_w_in, rw_mu, rw_w0, rw_w_up, rw_a0, rw_a_up, rw_g_up, rw_k_k, rw_k_a, rw_r_k, rw_lnx_g, rw_lnx_b, sc_conv_w, ab_w_out, dil_w_qkv, dil_w_out, rel_bias, mlp_w1, mlp_w2, loss_target, m_ada_w, m_ada_b, m_ln_g, m_ln_b, m_ab_w_in, m_rw_mu, m_rw_w0, m_rw_w_up, m_rw_a0, m_rw_a_up, m_rw_g_up, m_rw_k_k, m_rw_k_a, m_rw_r_k, m_rw_lnx_g, m_rw_lnx_b, m_sc_conv_w, m_ab_w_out, m_dil_w_qkv, m_dil_w_out, m_rel_bias, m_mlp_w1, m_mlp_w2, v_ada_w, v_ada_b, v_ln_g, v_ln_b, v_ab_w_in, v_rw_mu, v_rw_w0, v_rw_w_up, v_rw_a0, v_rw_a_up, v_rw_g_up, v_rw_k_k, v_rw_k_a, v_rw_r_k, v_rw_lnx_g, v_rw_lnx_b, v_sc_conv_w, v_ab_w_out, v_dil_w_qkv, v_dil_w_out, v_rel_bias, v_mlp_w1, v_mlp_w2):
    args = dict(locals())
    w_in = {n: args[n] for n in WEIGHT_ORDER}
    m_in = {n: args["m_" + n] for n in WEIGHT_ORDER}
    v_in = {n: args["v_" + n] for n in WEIGHT_ORDER}
    me = _me()
    chip = _chip_of(me)
    dev = _dev_of(me)

    c_all = _dev_all_gather(c, "gather_c")[:, 0, :]
    n_col = ada_w.shape[2]
    ada_b_cols = lax.dynamic_slice_in_dim(ada_b, chip * n_col, n_col, axis=1)[:, None, :]
    mod_cols = _ada_fwd(c_all, ada_w, ada_b_cols)

    first_buf = _pack_rows([w_in[n].astype(BF16) for n in FIRST_MIXER])
    first_all = _two_level_gather(first_buf, "gather_first")
    late_buf = _pack_rows([w_in[n].astype(BF16) for n in LATER_LAYERS])
    small_buf = _pack_rows([mod_cols] + [w_in[n] for n in SMALL_SHARDED], row_multiple=PACK_ROWS)
    small_all = _chip_all_gather(small_buf, "gather_small")

    def full_shape(n):
        shp = w_in[n].shape
        return shp[:SHARDED[n]] + (shp[SHARDED[n]] * N_CHIPS,) + shp[SHARDED[n] + 1:]

    wts = {n: w_in[n] for n in REPLICATED}
    big = {}
    for n, part in zip(FIRST_MIXER, _unpack_chips(first_all, [w_in[n].shape for n in FIRST_MIXER])):
        big[n] = _join_chips(part, SHARDED[n])
        wts[n] = jnp.zeros(full_shape(n), BF16)
    small_parts = _unpack_chips(small_all, [mod_cols.shape] + [w_in[n].shape for n in SMALL_SHARDED])
    for n, part in zip(SMALL_SHARDED, small_parts[1:]):
        wts[n] = _join_chips(part, SHARDED[n])
    mod_all = _join_chips(small_parts[0], 2)
    mods = lax.dynamic_slice_in_dim(mod_all, dev, 1, axis=1).reshape(DEPTH, 6, 1, D_MODEL)
    late_shapes = {n: full_shape(n) for n in LATER_LAYERS}
    late_slot = jnp.zeros(late_buf.shape, F32)

    def local_loss(xv, modv, wv, slot):
        return _forward_local(xv, modv, big, wv, late_buf, slot, late_shapes, loss_target[0])[0, 0]

    loss_local, (grad_x, dmods, dw, late_part) = jax.value_and_grad(local_loss, argnums=(0, 1, 2, 3))(
        x[0], mods, wts, late_slot)

    small_row = jnp.concatenate([dmods.reshape(-1)] + [dw[n].reshape(-1) for n in REPLICATED[1:]]
                                + [loss_local.reshape(1)])
    n_small = small_row.shape[0]
    n_small_pad = -(-n_small // LANES) * LANES
    small_row = jnp.pad(small_row, (0, n_small_pad - n_small))[None, :]
    rows_all = _dev_all_gather(small_row, "gather_small_grads")
    small_sum = _sum_slots(rows_all, "sum_small_grads")
    loss = small_sum[0, n_small - 1]

    dmod_all = rows_all[:, 0, :DEPTH * 6 * D_MODEL].reshape(N_DEV, DEPTH, 6 * D_MODEL)
    dmod_cols = lax.dynamic_slice_in_dim(dmod_all, chip * n_col, n_col, axis=2)
    dmod_cols = jnp.pad(jnp.moveaxis(dmod_cols, 0, 1), ((0, 0), (0, LANES - N_DEV), (0, 0)))
    c_all_t = jnp.pad(c_all.T, ((0, 0), (0, LANES - N_DEV)))
    grads = {"ada_w": _ada_grad(c_all_t, dmod_cols)}
    grads["ada_b"] = small_sum[0, :DEPTH * 6 * D_MODEL].reshape(ada_b.shape)
    r0 = DEPTH * 6 * D_MODEL
    for n in REPLICATED[1:]:
        size = int(np.prod(w_in[n].shape))
        grads[n] = small_sum[0, r0:r0 + size].reshape(w_in[n].shape)
        r0 += size

    sharded_names = FIRST_MIXER + SMALL_SHARDED
    send = _pack_chips([_split_chips(dw[n], SHARDED[n]).astype(BF16) for n in sharded_names])
    n_rows = send.shape[1]
    send = send.reshape(N_CHIPS, 2, n_rows // 2, ROW_W)
    theirs = _core_halves(send, "swap_halves")
    mine = lax.dynamic_index_in_dim(send, me[2], 1, keepdims=False)
    chip_part = _add_pairs(mine, theirs, "sum_cores")
    recv = _chip_scatter(chip_part, "scatter_grads")
    half_sum = _sum_slots(recv, "sum_chips")
    g_rows = _core_all_gather(half_sum, "gather_halves").reshape(n_rows, ROW_W)
    for n, g in zip(sharded_names, _unpack_rows(g_rows, [w_in[n].shape for n in sharded_names])):
        grads[n] = g

    late_rows = _sum_slots(_core_all_gather(late_part, "swap_late"), "sum_cores_late")
    for n, g in zip(LATER_LAYERS, _unpack_rows(late_rows, [w_in[n].shape for n in LATER_LAYERS])):
        grads[n] = g

    deltas, new_m, new_v = {}, {}, {}
    for n in WEIGHT_ORDER:
        shp = w_in[n].shape
        d, nm, nv = _adamw(_as2d(w_in[n]), _as2d(grads[n]), _as2d(m_in[n]), _as2d(v_in[n]), "adamw_" + n)
        deltas[n], new_m[n], new_v[n] = d.reshape(shp), nm.reshape(shp), nv.reshape(shp)

    return (loss, grad_x[None], *[grads[n] for n in WEIGHT_ORDER], *[deltas[n] for n in WEIGHT_ORDER],
            *[new_m[n] for n in WEIGHT_ORDER], *[new_v[n] for n in WEIGHT_ORDER])
```

```python
import functools
import math

import numpy as np
import jax
import jax.numpy as jnp
from jax import lax
from jax.experimental import pallas as pl
from jax.experimental.pallas import tpu as pltpu

F32 = jnp.float32
BF16 = jnp.bfloat16
MESH = pl.DeviceIdType.MESH

D_MODEL = 1024
DEPTH = 2
RW_WIDTH = 512
HEAD_DIM = 64
N_HEADS = 8
RW_DECAY_RANK = 64
RW_ICLR_RANK = 64
RW_GATE_RANK = 160
RW_GN_EPS = 64e-5
RW_PROJ = 3 * RW_WIDTH + RW_DECAY_RANK + RW_ICLR_RANK + RW_GATE_RANK
SC_WIDTH = 512
AB_PROJ = RW_PROJ + 3 * SC_WIDTH
DIL_PATTERNS = ((128, 1), (512, 4), (2048, 16))
N_GROUPS = 3
DIL_WIDTH = 512
DIL_PROJ = N_GROUPS * 3 * DIL_WIDTH
BLOCK = 128
N_BUCKETS = 32
MAX_DISTANCE = 2048
D_FF = 4 * D_MODEL
ALPHA = (2 * DEPTH) ** 0.25
LN_EPS = 1e-5
ADAM_LR = 0.001
ADAM_B1 = 0.9
ADAM_B2 = 0.999
ADAM_EPS = 1e-08
ADAM_WD = 0.01
ADAM_STEP = 10

N_CHIPS = 4
N_DEV = 8
LANES = 128
ROW_W = 1024
SCAN_CHUNK = 16
VMEM_LIMIT = 48 * 1024 * 1024
NEG_BIG = -1e30


def _pick(n, cands):
    for c in cands:
        if n % c == 0:
            return c
    return n


def _cparams(sem=None, vmem=None):
    return pltpu.CompilerParams(dimension_semantics=sem, vmem_limit_bytes=vmem)


_DOT_DIMS = {
    "nn": (((1,), (0,)), ((), ())),
    "nt": (((1,), (1,)), ((), ())),
    "tn": (((0,), (0,)), ((), ())),
}


def _mm(a, b, mode, name, square_relu=False, pre_act=None, narrow_out=False):
    if mode == "nn":
        (m, k), (_, n) = a.shape, b.shape
    elif mode == "nt":
        (m, k), (n, _) = a.shape, b.shape
    else:
        (k, m), (_, n) = a.shape, b.shape
    tm = _pick(m, (1024, 512, 256, 128))
    tn = _pick(n, (1024, 896, 768, 512, 384, 256, 128))
    wide_k = mode != "tn" and pre_act is None
    tk = _pick(k, (2048, 1792, 1024, 512, 256, 128) if wide_k else (1024, 512, 256, 128))
    nk = k // tk
    if mode == "tn":
        a_spec = pl.BlockSpec((tk, tm), lambda i, j, kk: (kk, i))
    else:
        a_spec = pl.BlockSpec((tm, tk), lambda i, j, kk: (i, kk))
    if mode == "nt":
        b_spec = pl.BlockSpec((tn, tk), lambda i, j, kk: (j, kk))
    else:
        b_spec = pl.BlockSpec((tk, tn), lambda i, j, kk: (kk, j))
    dims = _DOT_DIMS[mode]

    out_spec = pl.BlockSpec((tm, tn), lambda i, j, kk: (i, j))

    own_acc = narrow_out and nk > 1

    def body(*refs):
        a_ref, b_ref = refs[:2]
        h_ref = refs[2] if pre_act is not None else None
        o_ref = refs[3] if pre_act is not None else refs[2]
        act_ref = refs[o_pos + 1] if square_relu else None
        acc_ref = refs[-1] if own_acc else o_ref
        part = lax.dot_general(a_ref[...].astype(BF16), b_ref[...].astype(BF16), dims, preferred_element_type=F32)

        def finish(total):
            if h_ref is not None:
                total = total * (2.0 * jnp.maximum(h_ref[...], 0.0))
            o_ref[...] = total.astype(o_ref.dtype)
            if act_ref is not None:
                act_ref[...] = jnp.square(jnp.maximum(total, 0.0)).astype(BF16)

        if nk == 1:
            finish(part)
        else:
            kk = pl.program_id(2)

            @pl.when(kk == 0)
            def _():
                acc_ref[...] = part

            @pl.when(jnp.logical_and(kk > 0, kk < nk - 1))
            def _():
                acc_ref[...] += part

            @pl.when(kk == nk - 1)
            def _():
                finish(acc_ref[...] + part)

    operands = [a, b] + ([pre_act] if pre_act is not None else [])
    in_specs = [a_spec, b_spec] + ([out_spec] if pre_act is not None else [])
    o_pos = len(operands)
    out_shape = [jax.ShapeDtypeStruct((m, n), BF16 if narrow_out else F32)]
    out_shape += [jax.ShapeDtypeStruct((m, n), BF16)] if square_relu else []
    outs = pl.pallas_call(
        body, name=name, grid=(m // tm, n // tn, nk),
        in_specs=in_specs, out_specs=[out_spec] * len(out_shape), out_shape=out_shape,
        scratch_shapes=[pltpu.VMEM((tm, tn), F32)] if own_acc else [],
        compiler_params=_cparams(("parallel", "parallel", "arbitrary"), VMEM_LIMIT),
    )(*operands)
    return tuple(outs) if square_relu else outs[0]


def _linear(x, w, slot, name):
    @jax.custom_vjp
    def op(x, w, slot):
        return _mm(x, w, "nn", name + "_fwd")

    def fwd(x, w, slot):
        return _mm(x, w, "nn", name + "_fwd"), (x, w)

    def bwd(res, dy):
        x, w = res
        return _mm(dy, w, "nt", name + "_dx"), jnp.zeros_like(w), _mm(x, dy, "tn", name + "_dw", narrow_out=True)

    op.defvjp(fwd, bwd)
    return op(x, w, slot)


def _const_map(ndim):
    return lambda *g: (0,) * ndim


def _first_step(n_grid):
    return functools.reduce(jnp.logical_and, [pl.program_id(d) == 0 for d in range(n_grid)])


def _fused(fn, ins, outs, grid, name):
    arrays = [i[0] for i in ins]
    n_in, n_out, n_grid = len(ins), len(outs), len(grid)
    in_specs = [pl.BlockSpec(bs, im) for (_, bs, im, _) in ins]
    out_specs = [pl.BlockSpec(bs, im) for (_, bs, im, _) in outs]
    out_shapes = [jax.ShapeDtypeStruct(s, F32) for (s, _, _, _) in outs]
    sem = ("arbitrary",) * n_grid

    def fwd_call(*xs):
        def body(*refs):
            vals = [r[...] for r in refs[:n_in]]
            ys = fn(*vals)
            first = _first_step(n_grid)
            for o_ref, y, (_, _, _, kind) in zip(refs[n_in:], ys, outs):
                if kind == "t":
                    o_ref[...] = y
                else:
                    @pl.when(first)
                    def _(o_ref=o_ref):
                        o_ref[...] = jnp.zeros_like(o_ref)

                    o_ref[...] += y

        return pl.pallas_call(
            body, name=name + "_fwd", grid=grid, in_specs=in_specs, out_specs=out_specs,
            out_shape=out_shapes, compiler_params=_cparams(sem, VMEM_LIMIT))(*xs)

    def bwd_call(xs, dys):
        d_specs = [pl.BlockSpec(bs, im) for (_, bs, im, _) in outs]
        g_specs = [pl.BlockSpec(bs, im) for (_, bs, im, _) in ins]
        g_shapes = [jax.ShapeDtypeStruct(a.shape, F32) for a in arrays]

        def body(*refs):
            vals = [r[...] for r in refs[:n_in]]
            dvals = tuple(r[...] for r in refs[n_in:n_in + n_out])
            _, vjp = jax.vjp(lambda *v: tuple(fn(*v)), *vals)
            gs = vjp(dvals)
            first = _first_step(n_grid)
            for g_ref, g, (_, _, _, kind) in zip(refs[n_in + n_out:], gs, ins):
                if kind == "t":
                    g_ref[...] = g
                else:
                    @pl.when(first)
                    def _(g_ref=g_ref):
                        g_ref[...] = jnp.zeros_like(g_ref)

                    g_ref[...] += g

        return pl.pallas_call(
            body, name=name + "_bwd", grid=grid, in_specs=in_specs + d_specs, out_specs=g_specs,
            out_shape=g_shapes, compiler_params=_cparams(sem, VMEM_LIMIT))(*xs, *dys)

    @jax.custom_vjp
    def op(*xs):
        return tuple(fwd_call(*xs))

    def op_fwd(*xs):
        return tuple(fwd_call(*xs)), xs

    def op_bwd(xs, dys):
        return tuple(bwd_call(xs, dys))

    op.defvjp(op_fwd, op_bwd)
    return op(*arrays)


def _tiled(a, tile, cols=None, col_block=0):
    cols = a.shape[1] if cols is None else cols
    return (a, (tile, cols), lambda i, cb=col_block: (i, cb), "t")


def _shared(a):
    return (a, a.shape, _const_map(a.ndim), "b")


def _tiled_out(rows, cols, tile):
    return ((rows, cols), (tile, cols), lambda i: (i, 0), "t")


@jax.custom_vjp
def _bdot(x, w):
    return jnp.dot(x.astype(BF16), w.astype(BF16), preferred_element_type=F32)


def _bdot_fwd(x, w):
    return _bdot(x, w), (x, w)


def _bdot_bwd(res, dy):
    x, w = res
    dyb = dy.astype(BF16)
    dx = lax.dot_general(dyb, w.astype(BF16), _DOT_DIMS["nt"], preferred_element_type=F32)
    dw = lax.dot_general(x.astype(BF16), dyb, _DOT_DIMS["tn"], preferred_element_type=F32)
    return dx, dw


_bdot.defvjp(_bdot_fwd, _bdot_bwd)


def _head_sum(x):
    n = x.shape[-1]
    hi = lax.broadcasted_iota(jnp.int32, (n, n), 0) // HEAD_DIM
    hj = lax.broadcasted_iota(jnp.int32, (n, n), 1) // HEAD_DIM
    e = (hi == hj).astype(F32)
    return jnp.dot(x, e, precision=lax.Precision.HIGHEST, preferred_element_type=F32)


def _softplus(x):
    return jnp.maximum(x, 0.0) + jnp.log1p(jnp.exp(-jnp.abs(x)))


def _layer_norm_rows(z, g, b):
    mu = jnp.mean(z, axis=-1, keepdims=True)
    zc = z - mu
    var = jnp.mean(zc * zc, axis=-1, keepdims=True)
    return zc * lax.rsqrt(var + LN_EPS) * g + b


N_PAIRS = N_HEADS // 2


def _scan_consts():
    k = lax.broadcasted_iota(jnp.int32, (HEAD_DIM, LANES), 0)
    j = lax.broadcasted_iota(jnp.int32, (HEAD_DIM, LANES), 1)
    diag = ((j % HEAD_DIM) == k).astype(F32)
    jj = lax.broadcasted_iota(jnp.int32, (LANES, LANES), 0) // HEAD_DIM
    ll = lax.broadcasted_iota(jnp.int32, (LANES, LANES), 1) // HEAD_DIM
    same_head = (jj == ll).astype(BF16)
    return diag, same_head


def _unrolled(n, body, carry):
    for i in range(n):
        carry = body(i, carry)
    return carry


def _fill_cols(srcs, col_ref, n_steps):
    assert n_steps == 16
    blocks = []
    for src in srcs:
        x = src[...]
        hi = x.astype(BF16).astype(F32)
        r1 = x - hi
        mid = r1.astype(BF16).astype(F32)
        x48 = jnp.concatenate([hi, mid, r1 - mid], axis=0)
        for hp in range(N_PAIRS):
            xp = x48[:, hp * LANES:(hp + 1) * LANES]
            y = jnp.concatenate([xp, pltpu.roll(xp, HEAD_DIM, 1), jnp.zeros((32, LANES), F32)], axis=0)
            blocks.append(y.T[:HEAD_DIM].astype(BF16))
    lhs = jnp.concatenate(blocks, axis=0)
    j = lax.broadcasted_iota(jnp.int32, (LANES, LANES), 0)
    lane_head = lax.broadcasted_iota(jnp.int32, (LANES, LANES), 1) // HEAD_DIM
    for t in range(n_steps):
        pick = jnp.logical_and(j < 96, jnp.logical_and(j % 16 == t, j // 48 == lane_head))
        out = jnp.dot(lhs, pick.astype(BF16), preferred_element_type=F32)
        for vi in range(len(srcs)):
            for hp in range(N_PAIRS):
                r0 = (vi * N_PAIRS + hp) * HEAD_DIM
                col_ref[vi, hp, t] = out[r0:r0 + HEAD_DIM]


def _scan_fwd_call(r, w, k, v, a, b, wsrc):
    t_len = r.shape[0]
    ch = SCAN_CHUNK
    n_ch = t_len // ch
    w_rows, w_cols = wsrc.shape
    half = w_rows // 2
    chunks = _n_chunks(half)
    rc = half // chunks
    transfers = [(q, kk) for q in range(chunks) for kk in range(len(CHIP_FLIPS))]
    hand_on_step = [max(1, min(n_ch - 1, (i + 1) * (n_ch - 8) // len(transfers) + 3)) for i in range(len(transfers))]

    def body(r_ref, w_ref, k_ref, v_ref, a_ref, b_ref, src_ref, y_ref, ck_ref, g_ref,
             st_ref, col_ref, send1, recv1, send2, recv2):
        c = pl.program_id(0)
        me = _me()
        core = me[2]
        sibling = _flip(me, CORE_FLIP)

        def over_ici(q, kk):
            return pltpu.make_async_remote_copy(
                src_ref=src_ref.at[core, pl.ds(q * rc, rc)], dst_ref=g_ref.at[_chip_of(me), core, pl.ds(q * rc, rc)],
                send_sem=send1.at[kk * chunks + q], recv_sem=recv1.at[kk * chunks + q],
                device_id=_flip(me, CHIP_FLIPS[kk]), device_id_type=MESH)

        def hand_on(q, kk):
            landed = g_ref.at[_chip_of(_flip(me, CHIP_FLIPS[kk])), core, pl.ds(q * rc, rc)]
            return pltpu.make_async_remote_copy(
                src_ref=landed, dst_ref=landed, send_sem=send2.at[kk * chunks + q], recv_sem=recv2.at[kk * chunks + q],
                device_id=sibling, device_id_type=MESH)

        @pl.when(c == 0)
        def _():
            st_ref[...] = jnp.zeros_like(st_ref)
            for q, kk in transfers:
                over_ici(q, kk).start()

        for step in sorted(set(hand_on_step)):
            @pl.when(c == step)
            def _(step=step):
                for (q, kk), s in zip(transfers, hand_on_step):
                    if s == step:
                        over_ici(q, kk).wait_recv()
                        hand_on(q, kk).start()

        ck_ref[0] = st_ref[...]
        _fill_cols((w_ref, a_ref, b_ref, k_ref, r_ref), col_ref, ch)

        def step(t, states):
            new = []
            for hp in range(N_PAIRS):
                lanes = slice(hp * LANES, (hp + 1) * LANES)
                s = states[hp]
                sa = jnp.sum(s * col_ref[1, hp, t], axis=0, keepdims=True)
                s = s * col_ref[0, hp, t] + col_ref[2, hp, t] * sa + col_ref[3, hp, t] * v_ref[pl.ds(t, 1), lanes]
                y_ref[pl.ds(t, 1), lanes] = jnp.sum(s * col_ref[4, hp, t], axis=0, keepdims=True)
                new.append(s)
            return tuple(new)

        states = _unrolled(ch, step, tuple(st_ref[hp] for hp in range(N_PAIRS)))
        for hp in range(N_PAIRS):
            st_ref[hp] = states[hp]

        @pl.when(c == n_ch - 1)
        def _():
            for q, kk in transfers:
                hand_on(q, kk).wait_recv()
            for q, kk in transfers:
                over_ici(q, kk).wait_send()
                hand_on(q, kk).wait_send()

    row_spec = pl.BlockSpec((ch, RW_WIDTH), lambda c: (c, 0))
    any_spec = pl.BlockSpec(memory_space=pl.ANY)
    n_sem = len(transfers)
    y, ck, got = pl.pallas_call(
        body, name="rwkv_scan_fwd", grid=(n_ch,),
        in_specs=[row_spec] * 6 + [any_spec],
        out_specs=[row_spec, pl.BlockSpec((1, N_PAIRS, HEAD_DIM, LANES), lambda c: (c, 0, 0, 0)), any_spec],
        out_shape=[jax.ShapeDtypeStruct((t_len, RW_WIDTH), F32),
                   jax.ShapeDtypeStruct((n_ch, N_PAIRS, HEAD_DIM, LANES), F32),
                   jax.ShapeDtypeStruct((N_CHIPS, 2, half, w_cols), wsrc.dtype)],
        scratch_shapes=[pltpu.VMEM((N_PAIRS, HEAD_DIM, LANES), F32),
                        pltpu.VMEM((5, N_PAIRS, ch, HEAD_DIM, LANES), F32)]
        + [pltpu.SemaphoreType.DMA((n_sem,)) for _ in range(4)],
        compiler_params=_cparams(("arbitrary",), VMEM_LIMIT),
    )(r, w, k, v, a, b, wsrc.reshape(2, half, w_cols))
    return y, ck, got.reshape(N_CHIPS, w_rows, w_cols)


def _scan_bwd_call(r, w, k, v, a, b, ck, dy, gsend):
    t_len = r.shape[0]
    ch = SCAN_CHUNK
    n_ch = t_len // ch
    _, g_rows, g_cols = gsend.shape
    chunks = _n_chunks(g_rows)
    rc = g_rows // chunks
    transfers = [(q, kk) for q in range(chunks) for kk in range(len(CHIP_FLIPS))]

    def body(r_ref, w_ref, k_ref, v_ref, a_ref, b_ref, ck_ref, dy_ref, gs_ref,
             dr_ref, dw_ref, dk_ref, dv_ref, da_ref, db_ref, gr_ref,
             ds_ref, col_ref, sp_ref, sa_ref, send_sems, recv_sems):
        c = pl.program_id(0)
        me = _me()

        def to_chip(q, kk):
            peer = _flip(me, CHIP_FLIPS[kk])
            return pltpu.make_async_remote_copy(
                src_ref=gs_ref.at[_chip_of(peer), pl.ds(q * rc, rc)], dst_ref=gr_ref.at[_chip_of(me), pl.ds(q * rc, rc)],
                send_sem=send_sems.at[kk * chunks + q], recv_sem=recv_sems.at[kk * chunks + q],
                device_id=peer, device_id_type=MESH)

        @pl.when(c == 0)
        def _():
            ds_ref[...] = jnp.zeros_like(ds_ref)
            for q, kk in transfers:
                to_chip(q, kk).start()

        diag_f, same_head = _scan_consts()
        _fill_cols((w_ref, a_ref, b_ref, k_ref, r_ref), col_ref, ch)

        def replay(t, states):
            new = []
            for hp in range(N_PAIRS):
                lanes = slice(hp * LANES, (hp + 1) * LANES)
                s = states[hp]
                sp_ref[t, hp] = s
                sa = jnp.sum(s * col_ref[1, hp, t], axis=0, keepdims=True)
                sa_ref[pl.ds(t, 1), lanes] = sa
                new.append(s * col_ref[0, hp, t] + col_ref[2, hp, t] * sa
                           + col_ref[3, hp, t] * v_ref[pl.ds(t, 1), lanes])
            return tuple(new)

        _unrolled(ch, replay, tuple(ck_ref[0, hp] for hp in range(N_PAIRS)))

        def key_rows(ps):
            stacked = jnp.concatenate([p.astype(BF16) for p in ps], axis=0)
            q = jnp.dot(stacked, same_head, preferred_element_type=F32)
            return [jnp.sum(q[i * HEAD_DIM:(i + 1) * HEAD_DIM] * diag_f, axis=0, keepdims=True)
                    for i in range(len(ps))]

        def back(i, grads):
            t = ch - 1 - i
            new = []
            for hp in range(N_PAIRS):
                lanes = slice(hp * LANES, (hp + 1) * LANES)
                wc, ac, bc, kc, rc = (col_ref[vi, hp, t] for vi in range(5))
                sp = sp_ref[t, hp]
                sa = sa_ref[pl.ds(t, 1), lanes]
                vrow = v_ref[pl.ds(t, 1), lanes]
                dyrow = dy_ref[pl.ds(t, 1), lanes]
                st = sp * wc + bc * sa + kc * vrow
                g = grads[hp] + rc * dyrow
                dsa = jnp.sum(g * bc, axis=0, keepdims=True)
                dv_ref[pl.ds(t, 1), lanes] = jnp.sum(g * kc, axis=0, keepdims=True)
                rows = key_rows([st * dyrow, g * vrow, g * sa, g * sp, sp * dsa])
                for out_ref, row in zip((dr_ref, dk_ref, db_ref, dw_ref, da_ref), rows):
                    out_ref[pl.ds(t, 1), lanes] = row
                new.append(g * wc + ac * dsa)
            return tuple(new)

        grads = _unrolled(ch, back, tuple(ds_ref[hp] for hp in range(N_PAIRS)))
        for hp in range(N_PAIRS):
            ds_ref[hp] = grads[hp]

        @pl.when(c == n_ch - 1)
        def _():
            for q, kk in transfers:
                to_chip(q, kk).wait()

    row_spec = pl.BlockSpec((ch, RW_WIDTH), lambda c: (n_ch - 1 - c, 0))
    any_spec = pl.BlockSpec(memory_space=pl.ANY)
    out_sds = jax.ShapeDtypeStruct((t_len, RW_WIDTH), F32)
    n_sem = len(transfers)
    return pl.pallas_call(
        body, name="rwkv_scan_bwd", grid=(n_ch,),
        in_specs=[row_spec] * 6 + [pl.BlockSpec((1, N_PAIRS, HEAD_DIM, LANES), lambda c: (n_ch - 1 - c, 0, 0, 0)),
                                   row_spec, any_spec],
        out_specs=[row_spec] * 6 + [any_spec],
        out_shape=[out_sds] * 6 + [jax.ShapeDtypeStruct(gsend.shape, gsend.dtype)],
        scratch_shapes=[pltpu.VMEM((N_PAIRS, HEAD_DIM, LANES), F32),
                        pltpu.VMEM((5, N_PAIRS, ch, HEAD_DIM, LANES), F32),
                        pltpu.VMEM((ch, N_PAIRS, HEAD_DIM, LANES), F32),
                        pltpu.VMEM((ch, RW_WIDTH), F32),
                        pltpu.SemaphoreType.DMA((n_sem,)), pltpu.SemaphoreType.DMA((n_sem,))],
        compiler_params=_cparams(("arbitrary",), VMEM_LIMIT),
    )(r, w, k, v, a, b, ck, dy, gsend)


def _scan_token(wsrc):
    return jnp.zeros((N_CHIPS,) + wsrc.shape, BF16)


@jax.custom_vjp
def _rwkv_scan(r, w, k, v, a, b, wsrc, gslot):
    y, _, got = _scan_fwd_call(r, w, k, v, a, b, wsrc)
    return y, got, _scan_token(wsrc)


def _rwkv_scan_fwd(r, w, k, v, a, b, wsrc, gslot):
    y, ck, got = _scan_fwd_call(r, w, k, v, a, b, wsrc)
    return (y, got, _scan_token(wsrc)), (r, w, k, v, a, b, ck, wsrc)


def _rwkv_scan_bwd(res, cts):
    *saved, wsrc = res
    dy, _, gsend = cts
    *dins, got = _scan_bwd_call(*saved, dy, gsend)
    summed = _sum_slots_own(got, gsend, _chip_of(_me()), "sum_chips_late")
    return (*dins, jnp.zeros_like(wsrc), summed)


_rwkv_scan.defvjp(_rwkv_scan_fwd, _rwkv_scan_bwd)


ATT_SCALE = HEAD_DIM ** -0.5


def _att_masks():
    qi = lax.broadcasted_iota(jnp.int32, (BLOCK, BLOCK), 0)
    ki = lax.broadcasted_iota(jnp.int32, (BLOCK, BLOCK), 1)
    lane = lax.broadcasted_iota(jnp.int32, (1, LANES), 1)
    return ki <= qi, ki >= qi, lane


def _att_fwd_call(p, bias, g, dil):
    t_len = p.shape[0]
    l_len = t_len // dil
    nb = l_len // BLOCK

    def body(q_ref, kc_ref, kp_ref, vc_ref, vp_ref, bias_ref, o_ref, lse_ref):
        n = pl.program_id(1)
        cur_ok, prev_band, lane = _att_masks()
        prev_ok = jnp.logical_and(prev_band, n > 0)
        for hp in range(N_PAIRS):
            lanes = slice(hp * LANES, (hp + 1) * LANES)
            q2 = q_ref[:, lanes].astype(BF16)
            kc = kc_ref[:, lanes].astype(BF16)
            kp = kp_ref[:, lanes].astype(BF16)
            vc = vc_ref[:, lanes].astype(BF16)
            vp = vp_ref[:, lanes].astype(BF16)
            o2 = jnp.zeros((BLOCK, LANES), F32)
            for hh in range(2):
                h = 2 * hp + hh
                mine = (lane // HEAD_DIM) == hh
                qm = jnp.where(mine, q2, jnp.zeros_like(q2))
                s_c = lax.dot_general(qm, kc, _DOT_DIMS["nt"], preferred_element_type=F32) * ATT_SCALE
                s_p = lax.dot_general(qm, kp, _DOT_DIMS["nt"], preferred_element_type=F32) * ATT_SCALE
                s_c = jnp.where(cur_ok, s_c + bias_ref[h, :, BLOCK:], NEG_BIG)
                s_p = jnp.where(prev_ok, s_p + bias_ref[h, :, :BLOCK], NEG_BIG)
                m = jnp.maximum(jnp.max(s_c, axis=-1, keepdims=True), jnp.max(s_p, axis=-1, keepdims=True))
                e_c = jnp.exp(s_c - m)
                e_p = jnp.exp(s_p - m)
                den = jnp.sum(e_c, axis=-1, keepdims=True) + jnp.sum(e_p, axis=-1, keepdims=True)
                o_h = (jnp.dot((e_c / den).astype(BF16), vc, preferred_element_type=F32)
                       + jnp.dot((e_p / den).astype(BF16), vp, preferred_element_type=F32))
                o2 = o2 + jnp.where(mine, o_h, 0.0)
                lse_ref[h] = jnp.broadcast_to(m + jnp.log(den), (BLOCK, LANES))
            o_ref[:, lanes] = o2

    def col(j):
        return lambda r, n: (r * nb + n, j)

    def col_prev(j):
        return lambda r, n: (r * nb + jnp.maximum(n - 1, 0), j)

    blk = (BLOCK, DIL_WIDTH)
    return pl.pallas_call(
        body, name=f"dil_att_fwd_g{g}", grid=(dil, nb),
        in_specs=[pl.BlockSpec(blk, col(0)), pl.BlockSpec(blk, col(1)), pl.BlockSpec(blk, col_prev(1)),
                  pl.BlockSpec(blk, col(2)), pl.BlockSpec(blk, col_prev(2)),
                  pl.BlockSpec(bias.shape, _const_map(3))],
        out_specs=[pl.BlockSpec(blk, lambda r, n: (r * nb + n, 0)),
                   pl.BlockSpec((N_HEADS, BLOCK, LANES), lambda r, n: (0, r * nb + n, 0))],
        out_shape=[jax.ShapeDtypeStruct((t_len, DIL_WIDTH), F32),
                   jax.ShapeDtypeStruct((N_HEADS, t_len, LANES), F32)],
        compiler_params=_cparams(("arbitrary", "arbitrary"), VMEM_LIMIT),
    )(p, p, p, p, p, bias)


def _att_bwd_call(p, bias, o, lse, do, dlse, g, dil):
    t_len = p.shape[0]
    l_len = t_len // dil
    nb = l_len // BLOCK

    def body(q_ref, qn_ref, k_ref, v_ref, do_ref, don_ref, o_ref, on_ref, lse_ref, lsen_ref, dl_ref, dln_ref,
             bias_ref, dq_ref, dk_ref, dv_ref, dbias_ref, carry_ref):
        r = pl.program_id(0)
        n = pl.program_id(1)
        cur_ok, prev_band, lane = _att_masks()
        has_next = n + 1 < nb

        @pl.when(jnp.logical_and(r == 0, n == 0))
        def _():
            dbias_ref[...] = jnp.zeros_like(dbias_ref)

        @pl.when(n == 0)
        def _():
            carry_ref[...] = jnp.zeros_like(carry_ref)

        for hp in range(N_PAIRS):
            lanes = slice(hp * LANES, (hp + 1) * LANES)
            k2 = k_ref[:, lanes].astype(BF16)
            v2 = v_ref[:, lanes].astype(BF16)
            dk2 = jnp.zeros((BLOCK, LANES), F32)
            dv2 = jnp.zeros((BLOCK, LANES), F32)
            dq_cur = carry_ref[:, lanes]
            dq_next = jnp.zeros((BLOCK, LANES), F32)
            for hh in range(2):
                h = 2 * hp + hh
                mine = (lane // HEAD_DIM) == hh
                tiles = (
                    (q_ref, do_ref, o_ref, lse_ref, dl_ref, cur_ok, slice(BLOCK, 2 * BLOCK), None),
                    (qn_ref, don_ref, on_ref, lsen_ref, dln_ref, prev_band, slice(0, BLOCK), has_next),
                )
                if nb == 1:
                    tiles = tiles[:1]
                for ti, (qr, dor, orf, lr, dlr, ok, bcols, gate) in enumerate(tiles):
                    q2 = qr[:, lanes].astype(BF16)
                    qm = jnp.where(mine, q2, jnp.zeros_like(q2))
                    do_f = jnp.where(mine, dor[:, lanes], 0.0)
                    dom = do_f.astype(BF16)
                    s = lax.dot_general(qm, k2, _DOT_DIMS["nt"], preferred_element_type=F32) * ATT_SCALE
                    s = s + bias_ref[h, :, bcols]
                    if gate is not None:
                        ok = jnp.logical_and(ok, gate)
                    pr = jnp.where(ok, jnp.exp(jnp.minimum(s - lr[h], 0.0)), 0.0)
                    dp = lax.dot_general(dom, v2, _DOT_DIMS["nt"], preferred_element_type=F32)
                    delta = jnp.sum(do_f * orf[:, lanes], axis=-1, keepdims=True)
                    dl = jnp.sum(dlr[h], axis=-1, keepdims=True)
                    ds = pr * (dp - delta + dl)
                    dsb = ds.astype(BF16)
                    dq_h = jnp.where(mine, jnp.dot(dsb, k2, preferred_element_type=F32), 0.0) * ATT_SCALE
                    if ti == 0:
                        dq_cur = dq_cur + dq_h
                    else:
                        dq_next = dq_next + dq_h
                    dk2 = dk2 + lax.dot_general(dsb, qm, _DOT_DIMS["tn"], preferred_element_type=F32) * ATT_SCALE
                    dv2 = dv2 + lax.dot_general(pr.astype(BF16), dom, _DOT_DIMS["tn"], preferred_element_type=F32)
                    dbias_ref[h, :, bcols] += ds
            dq_ref[:, lanes] = dq_cur
            carry_ref[:, lanes] = dq_next
            dk_ref[:, lanes] = dk2
            dv_ref[:, lanes] = dv2

    def nxt(n):
        return jnp.minimum(n + 1, nb - 1)

    blk = (BLOCK, DIL_WIDTH)
    hblk = (N_HEADS, BLOCK, LANES)
    qcol = lambda j: (lambda r, n: (r * nb + n, j))
    q_next = lambda r, n: (r * nb + nxt(n), 0)
    rown = lambda r, n: (r * nb + n, 0)
    rown_next = lambda r, n: (r * nb + nxt(n), 0)
    hrow = lambda r, n: (0, r * nb + n, 0)
    hrow_next = lambda r, n: (0, r * nb + nxt(n), 0)
    sds = jax.ShapeDtypeStruct((t_len, DIL_WIDTH), F32)
    return pl.pallas_call(
        body, name=f"dil_att_bwd_g{g}", grid=(dil, nb),
        in_specs=[pl.BlockSpec(blk, qcol(0)), pl.BlockSpec(blk, q_next),
                  pl.BlockSpec(blk, qcol(1)), pl.BlockSpec(blk, qcol(2)),
                  pl.BlockSpec(blk, rown), pl.BlockSpec(blk, rown_next),
                  pl.BlockSpec(blk, rown), pl.BlockSpec(blk, rown_next),
                  pl.BlockSpec(hblk, hrow), pl.BlockSpec(hblk, hrow_next),
                  pl.BlockSpec(hblk, hrow), pl.BlockSpec(hblk, hrow_next),
                  pl.BlockSpec(bias.shape, _const_map(3))],
        out_specs=[pl.BlockSpec(blk, rown)] * 3 + [pl.BlockSpec(bias.shape, _const_map(3))],
        out_shape=[sds, sds, sds, jax.ShapeDtypeStruct(bias.shape, F32)],
        scratch_shapes=[pltpu.VMEM((BLOCK, DIL_WIDTH), F32)],
        compiler_params=_cparams(("arbitrary", "arbitrary"), VMEM_LIMIT),
    )(p, p, p, p, do, do, o, o, lse, lse, dlse, dlse, bias)


def _att_all_groups(ps, biases):
    outs = [_att_fwd_call(ps[g], biases[g], g, dil) for g, (_, dil) in enumerate(DIL_PATTERNS)]
    return tuple(o for o, _ in outs), tuple(l for _, l in outs)


@jax.custom_vjp
def _dilated_attention(ps, biases):
    return _att_all_groups(ps, biases)


def _dilated_attention_fwd(ps, biases):
    os_, lses = _att_all_groups(ps, biases)
    return (os_, lses), (ps, biases, os_, lses)


def _dilated_attention_bwd(res, cts):
    ps, biases, os_, lses = res
    dos, dlses = cts
    dps, dbiases = [], []
    for g, (_, dil) in enumerate(DIL_PATTERNS):
        dq, dk, dv, dbias = _att_bwd_call(ps[g], biases[g], os_[g], lses[g], dos[g], dlses[g], g, dil)
        dps.append(jnp.concatenate([dq, dk, dv], axis=1))
        dbiases.append(dbias)
    return tuple(dps), tuple(dbiases)


_dilated_attention.defvjp(_dilated_attention_fwd, _dilated_attention_bwd)


def _me():
    return lax.axis_index("x"), lax.axis_index("y"), lax.axis_index("c")


def _flip(me, f):
    return tuple((1 - m) if b else m for m, b in zip(me, f))


def _chip_of(d):
    return 2 * d[0] + d[1]


def _dev_of(d):
    return 4 * d[0] + 2 * d[1] + d[2]


EXCHANGE_CHUNKS = 8
CHIP_FLIPS = ((1, 0, 0), (0, 1, 0), (1, 1, 0))
ALL_FLIPS = tuple((a, b, c) for a in (0, 1) for b in (0, 1) for c in (0, 1) if a or b or c)
CORE_FLIP = (0, 0, 1)


def _n_chunks(rows):
    return EXCHANGE_CHUNKS if rows % (EXCHANGE_CHUNKS * PACK_ROWS) == 0 else 1


def _exchange(src, n_slots, transfers, name):
    _, rows, cols = src.shape
    chunks = _n_chunks(rows)
    rc = rows // chunks
    n = len(transfers) * chunks

    def body(src_ref, dst_ref, send_sems, recv_sems):
        me = _me()
        copies = []
        for q in range(chunks):
            for kk, (f, src_slot, dst_slot) in enumerate(transfers):
                peer = _flip(me, f)
                cp = pltpu.make_async_remote_copy(
                    src_ref=src_ref.at[src_slot(me, peer), pl.ds(q * rc, rc)],
                    dst_ref=dst_ref.at[dst_slot(me, peer), pl.ds(q * rc, rc)],
                    send_sem=send_sems.at[kk * chunks + q], recv_sem=recv_sems.at[kk * chunks + q],
                    device_id=peer, device_id_type=MESH)
                cp.start()
                copies.append(cp)
        for cp in copies:
            cp.wait()

    return pl.pallas_call(
        body, name=name,
        out_shape=jax.ShapeDtypeStruct((n_slots, rows, cols), src.dtype),
        in_specs=[pl.BlockSpec(memory_space=pl.ANY)],
        out_specs=pl.BlockSpec(memory_space=pl.ANY),
        scratch_shapes=[pltpu.SemaphoreType.DMA((n,)), pltpu.SemaphoreType.DMA((n,))],
    )(src)


def _set_slot(buf, block, index):
    return lax.dynamic_update_slice_in_dim(buf, block[None].astype(buf.dtype), index, axis=0)


def _chip_all_gather(src, name):
    got = _exchange(src[None], N_CHIPS, [(f, lambda me, peer: 0, lambda me, peer: _chip_of(me)) for f in CHIP_FLIPS], name)
    return _set_slot(got, src, _chip_of(_me()))


def _dev_all_gather(src, name):
    got = _exchange(src[None], N_DEV, [(f, lambda me, peer: 0, lambda me, peer: _dev_of(me)) for f in ALL_FLIPS], name)
    return _set_slot(got, src, _dev_of(_me()))


def _chip_scatter(src, name):
    got = _exchange(src, N_CHIPS, [(f, lambda me, peer: _chip_of(peer), lambda me, peer: _chip_of(me))
                                   for f in CHIP_FLIPS], name)
    chip = _chip_of(_me())
    return _set_slot(got, lax.dynamic_index_in_dim(src, chip, 0, keepdims=False), chip)


def _core_halves(src, name):
    s, _, half, cols = src.shape
    transfers = [(CORE_FLIP, (lambda me, peer, j=j: 2 * j + peer[2]), (lambda me, peer, j=j: j)) for j in range(s)]
    return _exchange(src.reshape(2 * s, half, cols), s, transfers, name)


def _core_all_gather(src, name):
    got = _exchange(src[None], 2, [(CORE_FLIP, lambda me, peer: 0, lambda me, peer: me[2])], name)
    return _set_slot(got, src, _me()[2])


def _two_level_gather(src, name):
    rows, cols = src.shape
    half = rows // 2
    chunks = _n_chunks(half)
    rc = half // chunks
    n = len(CHIP_FLIPS) * chunks

    def body(src_ref, g_ref, send1, recv1, send2, recv2):
        me = _me()
        c = me[2]
        sibling = _flip(me, CORE_FLIP)
        first, second = [], []
        for q in range(chunks):
            for kk, f in enumerate(CHIP_FLIPS):
                peer = _flip(me, f)
                cp = pltpu.make_async_remote_copy(
                    src_ref=src_ref.at[c, pl.ds(q * rc, rc)], dst_ref=g_ref.at[_chip_of(me), c, pl.ds(q * rc, rc)],
                    send_sem=send1.at[kk * chunks + q], recv_sem=recv1.at[kk * chunks + q],
                    device_id=peer, device_id_type=MESH)
                cp.start()
                first.append((cp, _chip_of(peer), kk * chunks + q, q))
        for cp, origin, idx, q in first:
            cp.wait_recv()
            fw = pltpu.make_async_remote_copy(
                src_ref=g_ref.at[origin, c, pl.ds(q * rc, rc)], dst_ref=g_ref.at[origin, c, pl.ds(q * rc, rc)],
                send_sem=send2.at[idx], recv_sem=recv2.at[idx],
                device_id=sibling, device_id_type=MESH)
            fw.start()
            second.append(fw)
        for fw in second:
            fw.wait_recv()
        for cp, _, _, _ in first:
            cp.wait_send()
        for fw in second:
            fw.wait_send()

    got = pl.pallas_call(
        body, name=name,
        out_shape=jax.ShapeDtypeStruct((N_CHIPS, 2, half, cols), src.dtype),
        in_specs=[pl.BlockSpec(memory_space=pl.ANY)],
        out_specs=pl.BlockSpec(memory_space=pl.ANY),
        scratch_shapes=[pltpu.SemaphoreType.DMA((n,)) for _ in range(4)],
    )(src.reshape(2, half, cols))
    return _set_slot(got.reshape(N_CHIPS, rows, cols), src, _chip_of(_me()))


def _sum_slots(x, name):
    s, rows, cols = x.shape
    tile = _pick(rows, (512, 256, 128, 64, 32, 16, 8))

    def body(x_ref, o_ref):
        acc = x_ref[0].astype(F32)
        for i in range(1, s):
            acc = acc + x_ref[i].astype(F32)
        o_ref[...] = acc

    return pl.pallas_call(
        body, name=name, grid=(rows // tile,),
        in_specs=[pl.BlockSpec((s, tile, cols), lambda i: (0, i, 0))],
        out_specs=pl.BlockSpec((tile, cols), lambda i: (i, 0)),
        out_shape=jax.ShapeDtypeStruct((rows, cols), F32),
        compiler_params=_cparams(("parallel",), VMEM_LIMIT),
    )(x)


def _sum_slots_own(recv, send, chip, name):
    s, rows, cols = recv.shape
    tile = _pick(rows, (512, 256, 128, 64, 32, 16, 8))

    def body(chip_ref, recv_ref, own_ref, o_ref):
        acc = None
        for j in range(s):
            term = jnp.where(chip_ref[0] == j, own_ref[0], recv_ref[j]).astype(F32)
            acc = term if acc is None else acc + term
        o_ref[...] = acc

    grid_spec = pltpu.PrefetchScalarGridSpec(
        num_scalar_prefetch=1, grid=(rows // tile,),
        in_specs=[pl.BlockSpec((s, tile, cols), lambda i, c: (0, i, 0)),
                  pl.BlockSpec((1, tile, cols), lambda i, c: (c[0], i, 0))],
        out_specs=pl.BlockSpec((tile, cols), lambda i, c: (i, 0)))
    return pl.pallas_call(
        body, name=name, grid_spec=grid_spec,
        out_shape=jax.ShapeDtypeStruct((rows, cols), F32),
        compiler_params=_cparams(("arbitrary",), VMEM_LIMIT),
    )(jnp.reshape(chip, (1,)).astype(jnp.int32), recv, send)


def _add_pairs(a, b, name):
    s, rows, cols = a.shape
    tile = _pick(rows, (512, 256, 128, 64, 32, 16, 8))

    def body(a_ref, b_ref, o_ref):
        o_ref[...] = (a_ref[...].astype(F32) + b_ref[...].astype(F32)).astype(o_ref.dtype)

    spec = pl.BlockSpec((1, tile, cols), lambda j, i: (j, i, 0))
    return pl.pallas_call(
        body, name=name, grid=(s, rows // tile),
        in_specs=[spec, spec], out_specs=spec,
        out_shape=jax.ShapeDtypeStruct(a.shape, a.dtype),
        compiler_params=_cparams(("parallel", "parallel"), VMEM_LIMIT),
    )(a, b)


def _adamw(w, g, m, v, name):
    rows, cols = w.shape
    tile = rows
    if rows * cols * 4 > 2 * 1024 * 1024:
        tile = _pick(rows, (256, 128, 64, 32, 16, 8))
    c1 = 1.0 / (1.0 - ADAM_B1 ** ADAM_STEP)
    c2 = 1.0 / (1.0 - ADAM_B2 ** ADAM_STEP)

    def body(w_ref, g_ref, m_ref, v_ref, d_ref, nm_ref, nv_ref):
        gv = g_ref[...]
        nm = ADAM_B1 * m_ref[...] + (1.0 - ADAM_B1) * gv
        nv = ADAM_B2 * v_ref[...] + (1.0 - ADAM_B2) * (gv * gv)
        m_hat = nm * c1
        v_hat = nv * c2
        d_ref[...] = -ADAM_LR * (m_hat / (jnp.sqrt(v_hat) + ADAM_EPS) + ADAM_WD * w_ref[...])
        nm_ref[...] = nm
        nv_ref[...] = nv

    spec = pl.BlockSpec((tile, cols), lambda i: (i, 0))
    sds = jax.ShapeDtypeStruct((rows, cols), F32)
    return pl.pallas_call(
        body, name=name, grid=(rows // tile,),
        in_specs=[spec] * 4, out_specs=[spec] * 3, out_shape=[sds] * 3,
        compiler_params=_cparams(("parallel",), VMEM_LIMIT),
    )(w, g, m, v)


def _ada_fwd(c_all, ada_w, ada_b_cols):
    n_col = ada_w.shape[2]

    def body(c_ref, w_ref, b_ref, o_ref):
        cv = c_ref[...]
        cond = (cv * jax.nn.sigmoid(cv)).astype(BF16)
        o_ref[0] = jnp.dot(cond, w_ref[0].astype(BF16), preferred_element_type=F32) + b_ref[0]

    return pl.pallas_call(
        body, name="ada_fwd", grid=(DEPTH,),
        in_specs=[pl.BlockSpec(c_all.shape, lambda i: (0, 0)),
                  pl.BlockSpec((1, D_MODEL, n_col), lambda i: (i, 0, 0)),
                  pl.BlockSpec((1, 1, n_col), lambda i: (i, 0, 0))],
        out_specs=pl.BlockSpec((1, N_DEV, n_col), lambda i: (i, 0, 0)),
        out_shape=jax.ShapeDtypeStruct((DEPTH, N_DEV, n_col), F32),
        compiler_params=_cparams(("parallel",), VMEM_LIMIT),
    )(c_all, ada_w, ada_b_cols)


def _ada_grad(c_all_t, dmod_cols):
    n_col = dmod_cols.shape[2]

    def body(c_ref, d_ref, o_ref):
        cv = c_ref[...]
        cond = cv * jax.nn.sigmoid(cv)
        o_ref[0] = jnp.dot(cond, d_ref[0], precision=lax.Precision.HIGHEST, preferred_element_type=F32)

    return pl.pallas_call(
        body, name="ada_grad", grid=(DEPTH,),
        in_specs=[pl.BlockSpec(c_all_t.shape, lambda i: (0, 0)),
                  pl.BlockSpec((1, LANES, n_col), lambda i: (i, 0, 0))],
        out_specs=pl.BlockSpec((1, D_MODEL, n_col), lambda i: (i, 0, 0)),
        out_shape=jax.ShapeDtypeStruct((DEPTH, D_MODEL, n_col), F32),
        compiler_params=_cparams(("parallel",), VMEM_LIMIT),
    )(c_all_t, dmod_cols)


ROW_TILE = 256


def _shift_rows(a, n=1):
    return jnp.pad(a, ((n, 0), (0, 0)))[:-n]


def _modulate(x, sc, sh, name):
    def fn(x, sc, sh):
        return (x * (1.0 + sc) + sh,)

    t = x.shape[0]
    return _fused(fn, [_tiled(x, ROW_TILE), _shared(sc), _shared(sh)],
                  [_tiled_out(t, D_MODEL, ROW_TILE)], (t // ROW_TILE,), name)[0]


def _resid_ln_mod(x, y, gate, ln_g, ln_b, sc, sh, name):
    def fn(x, y, gate, ln_g, ln_b, sc, sh):
        x1 = _layer_norm_rows(ALPHA * x + (1.0 + gate) * y, ln_g, ln_b)
        return x1, x1 * (1.0 + sc) + sh

    t = x.shape[0]
    return _fused(fn, [_tiled(x, ROW_TILE), _tiled(y, ROW_TILE)] + [_shared(a) for a in (gate, ln_g, ln_b, sc, sh)],
                  [_tiled_out(t, D_MODEL, ROW_TILE)] * 2, (t // ROW_TILE,), name)


def _resid_ln_loss(x, y, gate, ln_g, ln_b, target, name):
    def fn(x, y, gate, ln_g, ln_b, target):
        x1 = _layer_norm_rows(ALPHA * x + (1.0 + gate) * y, ln_g, ln_b)
        err = jnp.square(x1 - target)
        per_row = jnp.mean(err, axis=-1, keepdims=True)
        return (0.5 * jnp.sum(per_row, axis=0, keepdims=True),)

    t = x.shape[0]
    return _fused(fn, [_tiled(x, ROW_TILE), _tiled(y, ROW_TILE)] + [_shared(a) for a in (gate, ln_g, ln_b)]
                  + [_tiled(target, ROW_TILE)],
                  [((1, 1), (1, 1), _const_map(2), "a")], (t // ROW_TILE,), name)[0]


def _mlp(u, w1, s1, w2, s2, name):
    def run(u, w1, w2):
        h, act = _mm(u, w1, "nn", name + "_w1_fwd", square_relu=True)
        return _mm(act, w2, "nn", name + "_w2_fwd"), (u, w1, w2, h, act)

    @jax.custom_vjp
    def op(u, w1, s1, w2, s2):
        return run(u, w1, w2)[0]

    def fwd(u, w1, s1, w2, s2):
        return run(u, w1, w2)

    def bwd(res, dy):
        u, w1, w2, h, act = res
        dh = _mm(dy, w2, "nt", name + "_w2_dx", pre_act=h)
        dw2 = _mm(act, dy, "tn", name + "_w2_dw", narrow_out=True)
        du = _mm(dh, w1, "nt", name + "_w1_dx")
        dw1 = _mm(u, dh, "tn", name + "_w1_dw", narrow_out=True)
        return du, jnp.zeros_like(w1), dw1, jnp.zeros_like(w2), dw2

    op.defvjp(fwd, bwd)
    return op(u, w1, s1, w2, s2)


AB_PIECES = (("r", 0, 512, 512), ("k", 512, 512, 512), ("v", 1024, 512, 512),
             ("wd", 1536, 64, 128), ("ad", 1600, 64, 128), ("gd", 1664, 160, 256),
             ("h", 1824, 512, 512), ("bg", 2336, 512, 512), ("cg", 2848, 512, 512))
AB_PAD_COLS = sum(p[3] for p in AB_PIECES)


def _regroup_cols(w):
    parts = []
    for _, start, width, padded in AB_PIECES:
        piece = w[..., start:start + width]
        if padded != width:
            piece = jnp.pad(piece, [(0, 0)] * (w.ndim - 1) + [(0, padded - width)])
        parts.append(piece)
    return jnp.concatenate(parts, axis=-1)


def _pad_rows(w, rows):
    return jnp.pad(w, ((0, rows - w.shape[0]), (0, 0)))


def _rwkv_shortconv(u, big, wts, wsrc, gslot, late_shapes):
    t = u.shape[0]
    p = _linear(u, _regroup_cols(big["ab_w_in"][0]), _regroup_cols(wts["ab_w_in"][0]), "ab_in")
    mu = _regroup_cols(jnp.pad(wts["rw_mu"], ((0, 0), (0, AB_PROJ - RW_PROJ))))
    w_up = _pad_rows(wts["rw_w_up"][0], 128)
    a_up = _pad_rows(wts["rw_a_up"][0], 128)
    g_up = _pad_rows(wts["rw_g_up"][0], 256)

    def pre(rp, rs, kp, ks, vp, vs, wdp, wds, adp, ads, gdp, gds, h, cg,
            mu_r, mu_k, mu_v, mu_w, mu_a, mu_g, w0, w_up, a0, a_up, g_up, k_k, k_a):
        def mix(pv, sv, m):
            return pv + m * (sv - pv)

        r, k, v = mix(rp, rs, mu_r), mix(kp, ks, mu_k), mix(vp, vs, mu_v)
        wd, ad, gd = mix(wdp, wds, mu_w), mix(adp, ads, mu_a), mix(gdp, gds, mu_g)
        logw = -_softplus(-(w0 + _bdot(jnp.tanh(wd), w_up))) - 0.5
        decay = jnp.exp(-jnp.exp(logw))
        iclr = jax.nn.sigmoid(a0 + _bdot(ad, a_up))
        gate = _bdot(jax.nn.sigmoid(gd), g_up)
        kk = k * k_k
        kk = kk / jnp.maximum(jnp.sqrt(_head_sum(kk * kk)), 1e-12)
        k_h = k * (1.0 + (iclr - 1.0) * k_a)
        return r, decay, k_h, v, -kk, kk * iclr, gate, cg * h

    tile = ROW_TILE
    names = [q[0] for q in AB_PIECES]
    cuts = list(np.cumsum([q[3] for q in AB_PIECES])[:-1])
    pp = dict(zip(names, jnp.split(p, cuts, axis=1)))
    mp = dict(zip(names, jnp.split(mu, cuts, axis=1)))

    ins = []
    for name in ("r", "k", "v", "wd", "ad", "gd"):
        ins += [_tiled(pp[name], tile), _tiled(_shift_rows(pp[name]), tile)]
    ins += [_tiled(pp["h"], tile), _tiled(pp["cg"], tile)]
    ins += [_shared(mp[name]) for name in ("r", "k", "v", "wd", "ad", "gd")]
    ins += [_shared(a) for a in (wts["rw_w0"], w_up, wts["rw_a0"], a_up, g_up, wts["rw_k_k"], wts["rw_k_a"])]
    outs = [_tiled_out(t, RW_WIDTH, tile)] * 8
    r, decay, k_h, v, a, b, gate, z = _fused(pre, ins, outs, (t // tile,), "rwkv_pre")

    y, gathered, token = _rwkv_scan(r, decay, k_h, v, a, b, wsrc, gslot)
    shard_shapes = [s[:SHARDED[n]] + (s[SHARDED[n]] // N_CHIPS,) + s[SHARDED[n] + 1:] for n, s in late_shapes.items()]
    late_big = {n: _join_chips(part, SHARDED[n])
                for n, part in zip(late_shapes, _unpack_chips(gathered, shard_shapes, own=wsrc))}
    late_slots = _gradient_slots(token, late_shapes, tuple(late_shapes))

    conv_w = wts["sc_conv_w"][0]
    r_k = wts["rw_r_k"].reshape(1, RW_WIDTH)

    def post(y, r, k_h, v, gate, bg, z, z1, z2, lnx_g, lnx_b, r_k, c0, c1, c2):
        mean = _head_sum(y) * (1.0 / HEAD_DIM)
        yc = y - mean
        var = _head_sum(yc * yc) * (1.0 / HEAD_DIM)
        yn = yc * lax.rsqrt(var + RW_GN_EPS) * lnx_g + lnx_b
        bonus = _head_sum(r * k_h * r_k) * v
        return (yn + bonus) * gate, bg * (c0 * z2 + c1 * z1 + c2 * z)

    ins = [_tiled(a_, tile) for a_ in (y, r, k_h, v, gate, pp["bg"])]
    ins += [_tiled(a_, tile) for a_ in (z, _shift_rows(z, 1), _shift_rows(z, 2))]
    ins += [_shared(a_) for a_ in (wts["rw_lnx_g"], wts["rw_lnx_b"], r_k, conv_w[0:1], conv_w[1:2], conv_w[2:3])]
    y_a, y_b = _fused(post, ins, [_tiled_out(t, RW_WIDTH, tile)] * 2, (t // tile,), "rwkv_post")
    out = _linear(jnp.concatenate([y_a, y_b], axis=1), late_big["ab_w_out"][0], late_slots["ab_w_out"][0], "ab_out")
    return out, late_big, late_slots


def _t5_bucket_np(dist):
    exact = N_BUCKETS // 2
    logd = np.log(np.maximum(dist, 1).astype(np.float32) / exact) / math.log(MAX_DISTANCE / exact)
    large = np.minimum(exact + (logd * (N_BUCKETS - exact)).astype(np.int32), N_BUCKETS - 1)
    return np.where(dist < exact, dist, large)


def _merge_groups(os_, lses, name):
    t = os_[0].shape[0]
    tile = ROW_TILE

    def fn(o0, o1, o2, l0, l1, l2):
        lane = lax.broadcasted_iota(jnp.int32, (1, LANES), 1)
        lo = lane < HEAD_DIM
        ls = [jnp.where(lo, l[0], l[1]) for l in (l0, l1, l2)]
        m = jnp.maximum(jnp.maximum(ls[0], ls[1]), ls[2])
        es = [jnp.exp(l - m) for l in ls]
        den = es[0] + es[1] + es[2]
        return ((es[0] * o0 + es[1] * o1 + es[2] * o2) / den,)

    ins = [(o, (tile, LANES), lambda i, hp: (i, hp), "t") for o in os_]
    ins += [(l, (2, tile, LANES), lambda i, hp: (hp, i, 0), "t") for l in lses]
    outs = [((t, DIL_WIDTH), (tile, LANES), lambda i, hp: (i, hp), "t")]
    return _fused(fn, ins, outs, (t // tile, N_PAIRS), name)[0]


def _residue_major(a, dil, axis=0):
    if dil == 1:
        return a
    shp = a.shape
    split = a.reshape(shp[:axis] + (shp[axis] // dil, dil) + shp[axis + 1:])
    return jnp.swapaxes(split, axis, axis + 1).reshape(shp)


def _position_major(a, dil, axis=0):
    if dil == 1:
        return a
    shp = a.shape
    split = a.reshape(shp[:axis] + (dil, shp[axis] // dil) + shp[axis + 1:])
    return jnp.swapaxes(split, axis, axis + 1).reshape(shp)


def _dilated_mixer(u, big, wts):
    group_cols = 3 * DIL_WIDTH
    ps = []
    for g, (_, dil) in enumerate(DIL_PATTERNS):
        cols = slice(g * group_cols, (g + 1) * group_cols)
        ps.append(_linear(_residue_major(u, dil), big["dil_w_qkv"][0][:, cols], wts["dil_w_qkv"][g][0],
                          f"dil_qkv{g}"))
    qi = np.arange(BLOCK)[:, None]
    ki = np.arange(2 * BLOCK)[None, :]
    rel = BLOCK + qi - ki
    biases = []
    for g, (window, dil) in enumerate(DIL_PATTERNS):
        span = window // dil
        bucket = _t5_bucket_np(np.clip(rel, 0, span) * dil).reshape(-1)
        onehot = jnp.asarray(np.eye(N_BUCKETS, dtype=np.float32)[bucket])
        table = wts["rel_bias"][:, g * N_HEADS:(g + 1) * N_HEADS]
        bias = jnp.dot(onehot, table, precision=lax.Precision.HIGHEST)
        biases.append(jnp.transpose(bias.reshape(BLOCK, 2 * BLOCK, N_HEADS), (2, 0, 1)))
    os_, lses = _dilated_attention(tuple(ps), tuple(biases))
    os_ = [_position_major(o, dil) for o, (_, dil) in zip(os_, DIL_PATTERNS)]
    lses = [_position_major(l, dil, axis=1) for l, (_, dil) in zip(lses, DIL_PATTERNS)]
    o = _merge_groups(os_, lses, "dil_merge")
    return _linear(o, big["dil_w_out"][0], wts["dil_w_out"][0], "dil_out")


def _forward_local(x, mods, big, wts, wsrc, gslot, late_shapes, target):
    u = _modulate(x, mods[0, 1], mods[0, 0], "mod_in")
    for i in range(DEPTH):
        sh2, sc2, g1, g2 = mods[i, 3], mods[i, 4], mods[i, 2], mods[i, 5]
        if i == 0:
            y, late_big, late_slots = _rwkv_shortconv(u, big, wts, wsrc, gslot, late_shapes)
            big = {**big, **late_big}
            wts = {**wts, **late_slots}
        else:
            y = _dilated_mixer(u, big, wts)
        x, u = _resid_ln_mod(x, y, g1, wts["ln_g"][i, 0:1], wts["ln_b"][i, 0:1], sc2, sh2, f"ln_mix{i}")
        y = _mlp(u, big["mlp_w1"][i], wts["mlp_w1"][i], big["mlp_w2"][i], wts["mlp_w2"][i], f"mlp{i}")
        if i + 1 < DEPTH:
            x, u = _resid_ln_mod(x, y, g2, wts["ln_g"][i, 1:2], wts["ln_b"][i, 1:2],
                                 mods[i + 1, 1], mods[i + 1, 0], f"ln_mlp{i}")
        else:
            return _resid_ln_loss(x, y, g2, wts["ln_g"][i, 1:2], wts["ln_b"][i, 1:2], target, "ln_loss")


SHARDED = {"ab_w_in": 2, "ab_w_out": 1, "dil_w_qkv": 2, "dil_w_out": 2, "mlp_w1": 2, "mlp_w2": 1,
           "ln_g": 2, "ln_b": 2, "rw_w_up": 2, "rw_a_up": 2, "rw_g_up": 2, "sc_conv_w": 2}
FIRST_MIXER = ("ab_w_in",)
LATER_LAYERS = ("ab_w_out", "dil_w_qkv", "dil_w_out", "mlp_w1", "mlp_w2")
SMALL_SHARDED = ("ln_g", "ln_b", "rw_w_up", "rw_a_up", "rw_g_up", "sc_conv_w")
REPLICATED = ("ada_b", "rw_mu", "rw_w0", "rw_a0", "rw_k_k", "rw_k_a", "rw_r_k", "rw_lnx_g", "rw_lnx_b", "rel_bias")
WEIGHT_ORDER = ("ada_w", "ada_b", "ln_g", "ln_b", "ab_w_in", "rw_mu", "rw_w0", "rw_w_up", "rw_a0", "rw_a_up",
                "rw_g_up", "rw_k_k", "rw_k_a", "rw_r_k", "rw_lnx_g", "rw_lnx_b", "sc_conv_w", "ab_w_out",
                "dil_w_qkv", "dil_w_out", "rel_bias", "mlp_w1", "mlp_w2")


PACK_ROWS = 16


def _rows_of(n_elems):
    return -(-n_elems // (ROW_W * PACK_ROWS)) * PACK_ROWS


def _to_rows(a):
    flat = a.reshape(-1)
    rows = _rows_of(flat.shape[0])
    if rows * ROW_W != flat.shape[0]:
        flat = jnp.pad(flat, (0, rows * ROW_W - flat.shape[0]))
    return flat.reshape(rows, ROW_W)


def _from_rows(rows, shape):
    n = int(np.prod(shape))
    return rows.reshape(-1)[:n].reshape(shape)


def _split_chips(full, axis):
    shp = full.shape
    parts = full.reshape(shp[:axis] + (N_CHIPS, shp[axis] // N_CHIPS) + shp[axis + 1:])
    return jnp.moveaxis(parts, axis, 0)


def _join_chips(parts, axis):
    moved = jnp.moveaxis(parts, 0, axis)
    shp = moved.shape
    return moved.reshape(shp[:axis] + (shp[axis] * shp[axis + 1],) + shp[axis + 2:])


def _pack_rows(arrays, row_multiple=256):
    blocks = [_to_rows(a) for a in arrays]
    total = sum(b.shape[0] for b in blocks)
    pad = (-total) % row_multiple
    if pad:
        blocks.append(jnp.zeros((pad, ROW_W), blocks[0].dtype))
    return jnp.concatenate(blocks, axis=0)


def _unpack_rows(buf, shapes):
    out, r0 = [], 0
    for shp in shapes:
        n = _rows_of(int(np.prod(shp)))
        out.append(_from_rows(buf[r0:r0 + n], shp))
        r0 += n
    return out


def _pack_chips(parts):
    blocks = []
    for p in parts:
        flat = p.reshape(N_CHIPS, -1)
        rows = _rows_of(flat.shape[1])
        if rows * ROW_W != flat.shape[1]:
            flat = jnp.pad(flat, ((0, 0), (0, rows * ROW_W - flat.shape[1])))
        blocks.append(flat.reshape(N_CHIPS, rows, ROW_W))
    total = sum(b.shape[1] for b in blocks)
    pad = (-total) % 256
    if pad:
        blocks.append(jnp.zeros((N_CHIPS, pad, ROW_W), blocks[0].dtype))
    return jnp.concatenate(blocks, axis=1)


def _unpack_chips(buf, shapes, own=None):
    mine = lax.broadcasted_iota(jnp.int32, (N_CHIPS, 1, 1), 0) == _chip_of(_me()) if own is not None else None
    out, r0 = [], 0
    for shp in shapes:
        size = int(np.prod(shp))
        n = _rows_of(size)
        rows = buf[:, r0:r0 + n]
        if own is not None:
            rows = jnp.where(mine, own[None, r0:r0 + n], rows)
        out.append(rows.reshape(N_CHIPS, -1)[:, :size].reshape((N_CHIPS,) + tuple(shp)))
        r0 += n
    return out


def _gradient_slots(token, full_shapes, names):
    def slots():
        out = {n: jnp.zeros(full_shapes[n], BF16) for n in names}
        shp = full_shapes["dil_w_qkv"]
        out["dil_w_qkv"] = tuple(jnp.zeros(shp[:-1] + (shp[-1] // N_GROUPS,), BF16) for _ in range(N_GROUPS))
        return out

    @jax.custom_vjp
    def route(token):
        return slots()

    def fwd(token):
        return slots(), None

    def bwd(_, d):
        d = {**d, "dil_w_qkv": jnp.concatenate(d["dil_w_qkv"], axis=-1)}
        return (_pack_chips([_split_chips(d[n], SHARDED[n]).astype(BF16) for n in names]),)

    route.defvjp(fwd, bwd)
    return route(token)


def _as2d(a):
    return a.reshape(-1, a.shape[-1])


def kernel(x, c, ada_w, ada_b, ln_g, ln_b, ab_w_in, rw_mu, rw_w0, rw_w_up, rw_a0, rw_a_up, rw_g_up, rw_k_k, rw_k_a, rw_r_k, rw_lnx_g, rw_lnx_b, sc_conv_w, ab_w_out, dil_w_qkv, dil_w_out, rel_bias, mlp_w1, mlp_w2, loss_target, m_ada_w, m_ada_b, m_ln_g, m_ln_b, m_ab_w_in, m_rw_mu, m_rw_w0, m_rw_w_up, m_rw_a0, m_rw_a_up, m_rw_g_up, m_rw_k_k, m_rw_k_a, m_rw_r_k, m_rw_lnx_g, m_rw_lnx_b, m_sc_conv_w, m_ab_w_out, m_dil_w_qkv, m_dil_w_out, m_rel_bias, m_mlp_w1, m_mlp_w2, v_ada_w, v_ada_b, v_ln_g, v_ln_b, v_ab_w_in, v_rw_mu, v_rw_w0, v_rw_w_up, v_rw_a0, v_rw_a_up, v_rw_g_up, v_rw_k_k, v_rw_k_a, v_rw_r_k, v_rw_lnx_g, v_rw_lnx_b, v_sc_conv_w, v_ab_w_out, v_dil_w_qkv, v_dil_w_out, v_rel_bias, v_mlp_w1, v_mlp_w2):
    args = dict(locals())
    w_in = {n: args[n] for n in WEIGHT_ORDER}
    m_in = {n: args["m_" + n] for n in WEIGHT_ORDER}
    v_in = {n: args["v_" + n] for n in WEIGHT_ORDER}
    me = _me()
    chip = _chip_of(me)
    dev = _dev_of(me)

    c_all = _dev_all_gather(c, "gather_c")[:, 0, :]
    n_col = ada_w.shape[2]
    ada_b_cols = lax.dynamic_slice_in_dim(ada_b, chip * n_col, n_col, axis=1)[:, None, :]
    mod_cols = _ada_fwd(c_all, ada_w, ada_b_cols)

    first_buf = _pack_rows([w_in[n].astype(BF16) for n in FIRST_MIXER])
    first_all = _two_level_gather(first_buf, "gather_first")
    late_buf = _pack_rows([w_in[n].astype(BF16) for n in LATER_LAYERS])
    small_buf = _pack_rows([mod_cols] + [w_in[n] for n in SMALL_SHARDED], row_multiple=PACK_ROWS)
    small_all = _chip_all_gather(small_buf, "gather_small")

    def full_shape(n):
        shp = w_in[n].shape
        return shp[:SHARDED[n]] + (shp[SHARDED[n]] * N_CHIPS,) + shp[SHARDED[n] + 1:]

    wts = {n: w_in[n] for n in REPLICATED}
    big = {}
    for n, part in zip(FIRST_MIXER, _unpack_chips(first_all, [w_in[n].shape for n in FIRST_MIXER])):
        big[n] = _join_chips(part, SHARDED[n])
        wts[n] = jnp.zeros(full_shape(n), BF16)
    small_parts = _unpack_chips(small_all, [mod_cols.shape] + [w_in[n].shape for n in SMALL_SHARDED])
    for n, part in zip(SMALL_SHARDED, small_parts[1:]):
        wts[n] = _join_chips(part, SHARDED[n])
    mod_all = _join_chips(small_parts[0], 2)
    mods = lax.dynamic_slice_in_dim(mod_all, dev, 1, axis=1).reshape(DEPTH, 6, 1, D_MODEL)
    late_shapes = {n: full_shape(n) for n in LATER_LAYERS}
    late_slot = jnp.zeros(late_buf.shape, F32)

    def local_loss(xv, modv, wv, slot):
        return _forward_local(xv, modv, big, wv, late_buf, slot, late_shapes, loss_target[0])[0, 0]

    loss_local, (grad_x, dmods, dw, late_part) = jax.value_and_grad(local_loss, argnums=(0, 1, 2, 3))(
        x[0], mods, wts, late_slot)

    small_row = jnp.concatenate([dmods.reshape(-1)] + [dw[n].reshape(-1) for n in REPLICATED[1:]]
                                + [loss_local.reshape(1)])
    n_small = small_row.shape[0]
    n_small_pad = -(-n_small // LANES) * LANES
    small_row = jnp.pad(small_row, (0, n_small_pad - n_small))[None, :]
    rows_all = _dev_all_gather(small_row, "gather_small_grads")
    small_sum = _sum_slots(rows_all, "sum_small_grads")
    loss = small_sum[0, n_small - 1]

    dmod_all = rows_all[:, 0, :DEPTH * 6 * D_MODEL].reshape(N_DEV, DEPTH, 6 * D_MODEL)
    dmod_cols = lax.dynamic_slice_in_dim(dmod_all, chip * n_col, n_col, axis=2)
    dmod_cols = jnp.pad(jnp.moveaxis(dmod_cols, 0, 1), ((0, 0), (0, LANES - N_DEV), (0, 0)))
    c_all_t = jnp.pad(c_all.T, ((0, 0), (0, LANES - N_DEV)))
    grads = {"ada_w": _ada_grad(c_all_t, dmod_cols)}
    grads["ada_b"] = small_sum[0, :DEPTH * 6 * D_MODEL].reshape(ada_b.shape)
    r0 = DEPTH * 6 * D_MODEL
    for n in REPLICATED[1:]:
        size = int(np.prod(w_in[n].shape))
        grads[n] = small_sum[0, r0:r0 + size].reshape(w_in[n].shape)
        r0 += size

    sharded_names = FIRST_MIXER + SMALL_SHARDED
    send = _pack_chips([_split_chips(dw[n], SHARDED[n]).astype(BF16) for n in sharded_names])
    n_rows = send.shape[1]
    send = send.reshape(N_CHIPS, 2, n_rows // 2, ROW_W)
    theirs = _core_halves(send, "swap_halves")
    mine = lax.dynamic_index_in_dim(send, me[2], 1, keepdims=False)
    chip_part = _add_pairs(mine, theirs, "sum_cores")
    recv = _chip_scatter(chip_part, "scatter_grads")
    half_sum = _sum_slots(recv, "sum_chips")
    g_rows = _core_all_gather(half_sum, "gather_halves").reshape(n_rows, ROW_W)
    for n, g in zip(sharded_names, _unpack_rows(g_rows, [w_in[n].shape for n in sharded_names])):
        grads[n] = g

    late_rows = _sum_slots(_core_all_gather(late_part, "swap_late"), "sum_cores_late")
    for n, g in zip(LATER_LAYERS, _unpack_rows(late_rows, [w_in[n].shape for n in LATER_LAYERS])):
        grads[n] = g

    deltas, new_m, new_v = {}, {}, {}
    for n in WEIGHT_ORDER:
        shp = w_in[n].shape
        d, nm, nv = _adamw(_as2d(w_in[n]), _as2d(grads[n]), _as2d(m_in[n]), _as2d(v_in[n]), "adamw_" + n)
        deltas[n], new_m[n], new_v[n] = d.reshape(shp), nm.reshape(shp), nv.reshape(shp)

    return (loss, grad_x[None], *[grads[n] for n in WEIGHT_ORDER], *[deltas[n] for n in WEIGHT_ORDER],
            *[new_m[n] for n in WEIGHT_ORDER], *[new_v[n] for n in WEIGHT_ORDER])
```

```python
import functools
import math

import numpy as np
import jax
import jax.numpy as jnp
from jax import lax
from jax.experimental import pallas as pl
from jax.experimental.pallas import tpu as pltpu

F32 = jnp.float32
BF16 = jnp.bfloat16
MESH = pl.DeviceIdType.MESH

D_MODEL = 1024
DEPTH = 2
RW_WIDTH = 512
HEAD_DIM = 64
N_HEADS = 8
RW_DECAY_RANK = 64
RW_ICLR_RANK = 64
RW_GATE_RANK = 160
RW_GN_EPS = 64e-5
RW_PROJ = 3 * RW_WIDTH + RW_DECAY_RANK + RW_ICLR_RANK + RW_GATE_RANK
SC_WIDTH = 512
AB_PROJ = RW_PROJ + 3 * SC_WIDTH
DIL_PATTERNS = ((128, 1), (512, 4), (2048, 16))
N_GROUPS = 3
DIL_WIDTH = 512
DIL_PROJ = N_GROUPS * 3 * DIL_WIDTH
BLOCK = 128
N_BUCKETS = 32
MAX_DISTANCE = 2048
D_FF = 4 * D_MODEL
ALPHA = (2 * DEPTH) ** 0.25
LN_EPS = 1e-5
ADAM_LR = 0.001
ADAM_B1 = 0.9
ADAM_B2 = 0.999
ADAM_EPS = 1e-08
ADAM_WD = 0.01
ADAM_STEP = 10

N_CHIPS = 4
N_DEV = 8
LANES = 128
ROW_W = 1024
SCAN_CHUNK = 32
VMEM_LIMIT = 48 * 1024 * 1024
NEG_BIG = -1e30


def _pick(n, cands):
    for c in cands:
        if n % c == 0:
            return c
    return n


def _cparams(sem=None, vmem=None):
    return pltpu.CompilerParams(dimension_semantics=sem, vmem_limit_bytes=vmem)


_DOT_DIMS = {
    "nn": (((1,), (0,)), ((), ())),
    "nt": (((1,), (1,)), ((), ())),
    "tn": (((0,), (0,)), ((), ())),
}


def _mm(a, b, mode, name, square_relu=False, pre_act=None, narrow_out=False):
    if mode == "nn":
        (m, k), (_, n) = a.shape, b.shape
    elif mode == "nt":
        (m, k), (n, _) = a.shape, b.shape
    else:
        (k, m), (_, n) = a.shape, b.shape
    tm = _pick(m, (1024, 512, 256, 128))
    tn = _pick(n, (1024, 896, 768, 512, 384, 256, 128))
    wide_k = mode != "tn" and pre_act is None
    tk = _pick(k, (2048, 1792, 1024, 512, 256, 128) if wide_k else (1024, 512, 256, 128))
    nk = k // tk
    if mode == "tn":
        a_spec = pl.BlockSpec((tk, tm), lambda i, j, kk: (kk, i))
    else:
        a_spec = pl.BlockSpec((tm, tk), lambda i, j, kk: (i, kk))
    if mode == "nt":
        b_spec = pl.BlockSpec((tn, tk), lambda i, j, kk: (j, kk))
    else:
        b_spec = pl.BlockSpec((tk, tn), lambda i, j, kk: (kk, j))
    dims = _DOT_DIMS[mode]

    out_spec = pl.BlockSpec((tm, tn), lambda i, j, kk: (i, j))

    own_acc = narrow_out and nk > 1

    def body(*refs):
        a_ref, b_ref = refs[:2]
        h_ref = refs[2] if pre_act is not None else None
        o_ref = refs[3] if pre_act is not None else refs[2]
        act_ref = refs[o_pos + 1] if square_relu else None
        acc_ref = refs[-1] if own_acc else o_ref
        part = lax.dot_general(a_ref[...].astype(BF16), b_ref[...].astype(BF16), dims, preferred_element_type=F32)

        def finish(total):
            if h_ref is not None:
                total = total * (2.0 * jnp.maximum(h_ref[...], 0.0))
            o_ref[...] = total.astype(o_ref.dtype)
            if act_ref is not None:
                act_ref[...] = jnp.square(jnp.maximum(total, 0.0)).astype(BF16)

        if nk == 1:
            finish(part)
        else:
            kk = pl.program_id(2)

            @pl.when(kk == 0)
            def _():
                acc_ref[...] = part

            @pl.when(jnp.logical_and(kk > 0, kk < nk - 1))
            def _():
                acc_ref[...] += part

            @pl.when(kk == nk - 1)
            def _():
                finish(acc_ref[...] + part)

    operands = [a, b] + ([pre_act] if pre_act is not None else [])
    in_specs = [a_spec, b_spec] + ([out_spec] if pre_act is not None else [])
    o_pos = len(operands)
    out_shape = [jax.ShapeDtypeStruct((m, n), BF16 if narrow_out else F32)]
    out_shape += [jax.ShapeDtypeStruct((m, n), BF16)] if square_relu else []
    outs = pl.pallas_call(
        body, name=name, grid=(m // tm, n // tn, nk),
        in_specs=in_specs, out_specs=[out_spec] * len(out_shape), out_shape=out_shape,
        scratch_shapes=[pltpu.VMEM((tm, tn), F32)] if own_acc else [],
        compiler_params=_cparams(("parallel", "parallel", "arbitrary"), VMEM_LIMIT),
    )(*operands)
    return tuple(outs) if square_relu else outs[0]


def _linear(x, w, slot, name):
    @jax.custom_vjp
    def op(x, w, slot):
        return _mm(x, w, "nn", name + "_fwd")

    def fwd(x, w, slot):
        return _mm(x, w, "nn", name + "_fwd"), (x, w)

    def bwd(res, dy):
        x, w = res
        return _mm(dy, w, "nt", name + "_dx"), jnp.zeros_like(w), _mm(x, dy, "tn", name + "_dw", narrow_out=True)

    op.defvjp(fwd, bwd)
    return op(x, w, slot)


def _const_map(ndim):
    return lambda *g: (0,) * ndim


def _first_step(n_grid):
    return functools.reduce(jnp.logical_and, [pl.program_id(d) == 0 for d in range(n_grid)])


def _fused(fn, ins, outs, grid, name):
    arrays = [i[0] for i in ins]
    n_in, n_out, n_grid = len(ins), len(outs), len(grid)
    in_specs = [pl.BlockSpec(bs, im) for (_, bs, im, _) in ins]
    out_specs = [pl.BlockSpec(bs, im) for (_, bs, im, _) in outs]
    out_shapes = [jax.ShapeDtypeStruct(s, F32) for (s, _, _, _) in outs]
    sem = ("arbitrary",) * n_grid

    def fwd_call(*xs):
        def body(*refs):
            vals = [r[...] for r in refs[:n_in]]
            ys = fn(*vals)
            first = _first_step(n_grid)
            for o_ref, y, (_, _, _, kind) in zip(refs[n_in:], ys, outs):
                if kind == "t":
                    o_ref[...] = y
                else:
                    @pl.when(first)
                    def _(o_ref=o_ref):
                        o_ref[...] = jnp.zeros_like(o_ref)

                    o_ref[...] += y

        return pl.pallas_call(
            body, name=name + "_fwd", grid=grid, in_specs=in_specs, out_specs=out_specs,
            out_shape=out_shapes, compiler_params=_cparams(sem, VMEM_LIMIT))(*xs)

    def bwd_call(xs, dys):
        d_specs = [pl.BlockSpec(bs, im) for (_, bs, im, _) in outs]
        g_specs = [pl.BlockSpec(bs, im) for (_, bs, im, _) in ins]
        g_shapes = [jax.ShapeDtypeStruct(a.shape, F32) for a in arrays]

        def body(*refs):
            vals = [r[...] for r in refs[:n_in]]
            dvals = tuple(r[...] for r in refs[n_in:n_in + n_out])
            _, vjp = jax.vjp(lambda *v: tuple(fn(*v)), *vals)
            gs = vjp(dvals)
            first = _first_step(n_grid)
            for g_ref, g, (_, _, _, kind) in zip(refs[n_in + n_out:], gs, ins):
                if kind == "t":
                    g_ref[...] = g
                else:
                    @pl.when(first)
                    def _(g_ref=g_ref):
                        g_ref[...] = jnp.zeros_like(g_ref)

                    g_ref[...] += g

        return pl.pallas_call(
            body, name=name + "_bwd", grid=grid, in_specs=in_specs + d_specs, out_specs=g_specs,
            out_shape=g_shapes, compiler_params=_cparams(sem, VMEM_LIMIT))(*xs, *dys)

    @jax.custom_vjp
    def op(*xs):
        return tuple(fwd_call(*xs))

    def op_fwd(*xs):
        return tuple(fwd_call(*xs)), xs

    def op_bwd(xs, dys):
        return tuple(bwd_call(xs, dys))

    op.defvjp(op_fwd, op_bwd)
    return op(*arrays)


def _tiled(a, tile, cols=None, col_block=0):
    cols = a.shape[1] if cols is None else cols
    return (a, (tile, cols), lambda i, cb=col_block: (i, cb), "t")


def _shared(a):
    return (a, a.shape, _const_map(a.ndim), "b")


def _tiled_out(rows, cols, tile):
    return ((rows, cols), (tile, cols), lambda i: (i, 0), "t")


@jax.custom_vjp
def _bdot(x, w):
    return jnp.dot(x.astype(BF16), w.astype(BF16), preferred_element_type=F32)


def _bdot_fwd(x, w):
    return _bdot(x, w), (x, w)


def _bdot_bwd(res, dy):
    x, w = res
    dyb = dy.astype(BF16)
    dx = lax.dot_general(dyb, w.astype(BF16), _DOT_DIMS["nt"], preferred_element_type=F32)
    dw = lax.dot_general(x.astype(BF16), dyb, _DOT_DIMS["tn"], preferred_element_type=F32)
    return dx, dw


_bdot.defvjp(_bdot_fwd, _bdot_bwd)


def _head_sum(x):
    n = x.shape[-1]
    hi = lax.broadcasted_iota(jnp.int32, (n, n), 0) // HEAD_DIM
    hj = lax.broadcasted_iota(jnp.int32, (n, n), 1) // HEAD_DIM
    e = (hi == hj).astype(F32)
    return jnp.dot(x, e, precision=lax.Precision.HIGHEST, preferred_element_type=F32)


def _softplus(x):
    return jnp.maximum(x, 0.0) + jnp.log1p(jnp.exp(-jnp.abs(x)))


def _layer_norm_rows(z, g, b):
    mu = jnp.mean(z, axis=-1, keepdims=True)
    zc = z - mu
    var = jnp.mean(zc * zc, axis=-1, keepdims=True)
    return zc * lax.rsqrt(var + LN_EPS) * g + b


N_PAIRS = N_HEADS // 2


def _scan_consts():
    k = lax.broadcasted_iota(jnp.int32, (HEAD_DIM, LANES), 0)
    j = lax.broadcasted_iota(jnp.int32, (HEAD_DIM, LANES), 1)
    diag = ((j % HEAD_DIM) == k).astype(F32)
    jj = lax.broadcasted_iota(jnp.int32, (LANES, LANES), 0) // HEAD_DIM
    ll = lax.broadcasted_iota(jnp.int32, (LANES, LANES), 1) // HEAD_DIM
    same_head = (jj == ll).astype(BF16)
    return diag, same_head


def _unrolled(n, body, carry):
    for i in range(n):
        carry = body(i, carry)
    return carry


def _fill_cols(srcs, col_ref, n_steps):
    group = 16
    assert n_steps % group == 0
    j = lax.broadcasted_iota(jnp.int32, (LANES, LANES), 0)
    lane_head = lax.broadcasted_iota(jnp.int32, (LANES, LANES), 1) // HEAD_DIM
    for t0 in range(0, n_steps, group):
        blocks = []
        for src in srcs:
            x = src[t0:t0 + group, :]
            hi = x.astype(BF16).astype(F32)
            r1 = x - hi
            mid = r1.astype(BF16).astype(F32)
            x48 = jnp.concatenate([hi, mid, r1 - mid], axis=0)
            for hp in range(N_PAIRS):
                xp = x48[:, hp * LANES:(hp + 1) * LANES]
                y = jnp.concatenate([xp, pltpu.roll(xp, HEAD_DIM, 1), jnp.zeros((32, LANES), F32)], axis=0)
                blocks.append(y.T[:HEAD_DIM].astype(BF16))
        lhs = jnp.concatenate(blocks, axis=0)
        for t in range(group):
            pick = jnp.logical_and(j < 96, jnp.logical_and(j % 16 == t, j // 48 == lane_head))
            out = jnp.dot(lhs, pick.astype(BF16), preferred_element_type=F32)
            for vi in range(len(srcs)):
                for hp in range(N_PAIRS):
                    r0 = (vi * N_PAIRS + hp) * HEAD_DIM
                    col_ref[vi, hp, t0 + t] = out[r0:r0 + HEAD_DIM]


def _scan_fwd_call(r, w, k, v, a, b, wsrc):
    t_len = r.shape[0]
    ch = SCAN_CHUNK
    n_ch = t_len // ch
    w_rows, w_cols = wsrc.shape
    half = w_rows // 2
    chunks = _n_chunks(half)
    rc = half // chunks
    transfers = [(q, kk) for q in range(chunks) for kk in range(len(CHIP_FLIPS))]
    hand_on_step = [max(1, min(n_ch - 1, (i + 1) * (n_ch - 8) // len(transfers) + 3)) for i in range(len(transfers))]

    def body(r_ref, w_ref, k_ref, v_ref, a_ref, b_ref, src_ref, y_ref, ck_ref, g_ref,
             st_ref, col_ref, send1, recv1, send2, recv2):
        c = pl.program_id(0)
        me = _me()
        core = me[2]
        sibling = _flip(me, CORE_FLIP)

        def over_ici(q, kk):
            return pltpu.make_async_remote_copy(
                src_ref=src_ref.at[core, pl.ds(q * rc, rc)], dst_ref=g_ref.at[_chip_of(me), core, pl.ds(q * rc, rc)],
                send_sem=send1.at[kk * chunks + q], recv_sem=recv1.at[kk * chunks + q],
                device_id=_flip(me, CHIP_FLIPS[kk]), device_id_type=MESH)

        def hand_on(q, kk):
            landed = g_ref.at[_chip_of(_flip(me, CHIP_FLIPS[kk])), core, pl.ds(q * rc, rc)]
            return pltpu.make_async_remote_copy(
                src_ref=landed, dst_ref=landed, send_sem=send2.at[kk * chunks + q], recv_sem=recv2.at[kk * chunks + q],
                device_id=sibling, device_id_type=MESH)

        @pl.when(c == 0)
        def _():
            st_ref[...] = jnp.zeros_like(st_ref)
            for q, kk in transfers:
                over_ici(q, kk).start()

        for step in sorted(set(hand_on_step)):
            @pl.when(c == step)
            def _(step=step):
                for (q, kk), s in zip(transfers, hand_on_step):
                    if s == step:
                        over_ici(q, kk).wait_recv()
                        hand_on(q, kk).start()

        ck_ref[0] = st_ref[...]
        _fill_cols((w_ref, a_ref, b_ref, k_ref, r_ref), col_ref, ch)

        def step(t, states):
            new = []
            for hp in range(N_PAIRS):
                lanes = slice(hp * LANES, (hp + 1) * LANES)
                s = states[hp]
                sa = jnp.sum(s * col_ref[1, hp, t], axis=0, keepdims=True)
                s = s * col_ref[0, hp, t] + col_ref[2, hp, t] * sa + col_ref[3, hp, t] * v_ref[pl.ds(t, 1), lanes]
                y_ref[pl.ds(t, 1), lanes] = jnp.sum(s * col_ref[4, hp, t], axis=0, keepdims=True)
                new.append(s)
            return tuple(new)

        states = _unrolled(ch, step, tuple(st_ref[hp] for hp in range(N_PAIRS)))
        for hp in range(N_PAIRS):
            st_ref[hp] = states[hp]

        @pl.when(c == n_ch - 1)
        def _():
            for q, kk in transfers:
                hand_on(q, kk).wait_recv()
            for q, kk in transfers:
                over_ici(q, kk).wait_send()
                hand_on(q, kk).wait_send()

    row_spec = pl.BlockSpec((ch, RW_WIDTH), lambda c: (c, 0))
    any_spec = pl.BlockSpec(memory_space=pl.ANY)
    n_sem = len(transfers)
    y, ck, got = pl.pallas_call(
        body, name="rwkv_scan_fwd", grid=(n_ch,),
        in_specs=[row_spec] * 6 + [any_spec],
        out_specs=[row_spec, pl.BlockSpec((1, N_PAIRS, HEAD_DIM, LANES), lambda c: (c, 0, 0, 0)), any_spec],
        out_shape=[jax.ShapeDtypeStruct((t_len, RW_WIDTH), F32),
                   jax.ShapeDtypeStruct((n_ch, N_PAIRS, HEAD_DIM, LANES), F32),
                   jax.ShapeDtypeStruct((N_CHIPS, 2, half, w_cols), wsrc.dtype)],
        scratch_shapes=[pltpu.VMEM((N_PAIRS, HEAD_DIM, LANES), F32),
                        pltpu.VMEM((5, N_PAIRS, ch, HEAD_DIM, LANES), F32)]
        + [pltpu.SemaphoreType.DMA((n_sem,)) for _ in range(4)],
        compiler_params=_cparams(("arbitrary",), VMEM_LIMIT),
    )(r, w, k, v, a, b, wsrc.reshape(2, half, w_cols))
    return y, ck, got.reshape(N_CHIPS, w_rows, w_cols)


def _scan_bwd_call(r, w, k, v, a, b, ck, dy, gsend):
    t_len = r.shape[0]
    ch = SCAN_CHUNK
    n_ch = t_len // ch
    _, g_rows, g_cols = gsend.shape
    chunks = _n_chunks(g_rows)
    rc = g_rows // chunks
    transfers = [(q, kk) for q in range(chunks) for kk in range(len(CHIP_FLIPS))]

    def body(r_ref, w_ref, k_ref, v_ref, a_ref, b_ref, ck_ref, dy_ref, gs_ref,
             dr_ref, dw_ref, dk_ref, dv_ref, da_ref, db_ref, gr_ref,
             ds_ref, col_ref, sp_ref, sa_ref, send_sems, recv_sems):
        c = pl.program_id(0)
        me = _me()

        def to_chip(q, kk):
            peer = _flip(me, CHIP_FLIPS[kk])
            return pltpu.make_async_remote_copy(
                src_ref=gs_ref.at[_chip_of(peer), pl.ds(q * rc, rc)], dst_ref=gr_ref.at[_chip_of(me), pl.ds(q * rc, rc)],
                send_sem=send_sems.at[kk * chunks + q], recv_sem=recv_sems.at[kk * chunks + q],
                device_id=peer, device_id_type=MESH)

        @pl.when(c == 0)
        def _():
            ds_ref[...] = jnp.zeros_like(ds_ref)
            for q, kk in transfers:
                to_chip(q, kk).start()

        diag_f, same_head = _scan_consts()
        _fill_cols((w_ref, a_ref, b_ref, k_ref, r_ref), col_ref, ch)

        def replay(t, states):
            new = []
            for hp in range(N_PAIRS):
                lanes = slice(hp * LANES, (hp + 1) * LANES)
                s = states[hp]
                sp_ref[t, hp] = s
                sa = jnp.sum(s * col_ref[1, hp, t], axis=0, keepdims=True)
                sa_ref[pl.ds(t, 1), lanes] = sa
                new.append(s * col_ref[0, hp, t] + col_ref[2, hp, t] * sa
                           + col_ref[3, hp, t] * v_ref[pl.ds(t, 1), lanes])
            return tuple(new)

        _unrolled(ch, replay, tuple(ck_ref[0, hp] for hp in range(N_PAIRS)))

        def key_rows(ps):
            stacked = jnp.concatenate([p.astype(BF16) for p in ps], axis=0)
            q = jnp.dot(stacked, same_head, preferred_element_type=F32)
            return [jnp.sum(q[i * HEAD_DIM:(i + 1) * HEAD_DIM] * diag_f, axis=0, keepdims=True)
                    for i in range(len(ps))]

        def back(i, grads):
            t = ch - 1 - i
            new = []
            for hp in range(N_PAIRS):
                lanes = slice(hp * LANES, (hp + 1) * LANES)
                wc, ac, bc, kc, rc = (col_ref[vi, hp, t] for vi in range(5))
                sp = sp_ref[t, hp]
                sa = sa_ref[pl.ds(t, 1), lanes]
                vrow = v_ref[pl.ds(t, 1), lanes]
                dyrow = dy_ref[pl.ds(t, 1), lanes]
                st = sp * wc + bc * sa + kc * vrow
                g = grads[hp] + rc * dyrow
                dsa = jnp.sum(g * bc, axis=0, keepdims=True)
                dv_ref[pl.ds(t, 1), lanes] = jnp.sum(g * kc, axis=0, keepdims=True)
                rows = key_rows([st * dyrow, g * vrow, g * sa, g * sp, sp * dsa])
                for out_ref, row in zip((dr_ref, dk_ref, db_ref, dw_ref, da_ref), rows):
                    out_ref[pl.ds(t, 1), lanes] = row
                new.append(g * wc + ac * dsa)
            return tuple(new)

        grads = _unrolled(ch, back, tuple(ds_ref[hp] for hp in range(N_PAIRS)))
        for hp in range(N_PAIRS):
            ds_ref[hp] = grads[hp]

        @pl.when(c == n_ch - 1)
        def _():
            for q, kk in transfers:
                to_chip(q, kk).wait()

    row_spec = pl.BlockSpec((ch, RW_WIDTH), lambda c: (n_ch - 1 - c, 0))
    any_spec = pl.BlockSpec(memory_space=pl.ANY)
    out_sds = jax.ShapeDtypeStruct((t_len, RW_WIDTH), F32)
    n_sem = len(transfers)
    return pl.pallas_call(
        body, name="rwkv_scan_bwd", grid=(n_ch,),
        in_specs=[row_spec] * 6 + [pl.BlockSpec((1, N_PAIRS, HEAD_DIM, LANES), lambda c: (n_ch - 1 - c, 0, 0, 0)),
                                   row_spec, any_spec],
        out_specs=[row_spec] * 6 + [any_spec],
        out_shape=[out_sds] * 6 + [jax.ShapeDtypeStruct(gsend.shape, gsend.dtype)],
        scratch_shapes=[pltpu.VMEM((N_PAIRS, HEAD_DIM, LANES), F32),
                        pltpu.VMEM((5, N_PAIRS, ch, HEAD_DIM, LANES), F32),
                        pltpu.VMEM((ch, N_PAIRS, HEAD_DIM, LANES), F32),
                        pltpu.VMEM((ch, RW_WIDTH), F32),
                        pltpu.SemaphoreType.DMA((n_sem,)), pltpu.SemaphoreType.DMA((n_sem,))],
        compiler_params=_cparams(("arbitrary",), VMEM_LIMIT),
    )(r, w, k, v, a, b, ck, dy, gsend)


def _scan_token(wsrc):
    return jnp.zeros((N_CHIPS,) + wsrc.shape, BF16)


@jax.custom_vjp
def _rwkv_scan(r, w, k, v, a, b, wsrc, gslot):
    y, _, got = _scan_fwd_call(r, w, k, v, a, b, wsrc)
    return y, got, _scan_token(wsrc)


def _rwkv_scan_fwd(r, w, k, v, a, b, wsrc, gslot):
    y, ck, got = _scan_fwd_call(r, w, k, v, a, b, wsrc)
    return (y, got, _scan_token(wsrc)), (r, w, k, v, a, b, ck, wsrc)


def _rwkv_scan_bwd(res, cts):
    *saved, wsrc = res
    dy, _, gsend = cts
    *dins, got = _scan_bwd_call(*saved, dy, gsend)
    summed = _sum_slots_own(got, gsend, _chip_of(_me()), "sum_chips_late")
    return (*dins, jnp.zeros_like(wsrc), summed)


_rwkv_scan.defvjp(_rwkv_scan_fwd, _rwkv_scan_bwd)


ATT_SCALE = HEAD_DIM ** -0.5


def _att_masks():
    qi = lax.broadcasted_iota(jnp.int32, (BLOCK, BLOCK), 0)
    ki = lax.broadcasted_iota(jnp.int32, (BLOCK, BLOCK), 1)
    lane = lax.broadcasted_iota(jnp.int32, (1, LANES), 1)
    return ki <= qi, ki >= qi, lane


def _att_fwd_call(p, bias, g, dil):
    t_len = p.shape[0]
    l_len = t_len // dil
    nb = l_len // BLOCK

    def body(q_ref, kc_ref, kp_ref, vc_ref, vp_ref, bias_ref, o_ref, lse_ref):
        n = pl.program_id(1)
        cur_ok, prev_band, lane = _att_masks()
        prev_ok = jnp.logical_and(prev_band, n > 0)
        for hp in range(N_PAIRS):
            lanes = slice(hp * LANES, (hp + 1) * LANES)
            q2 = q_ref[:, lanes].astype(BF16)
            kc = kc_ref[:, lanes].astype(BF16)
            kp = kp_ref[:, lanes].astype(BF16)
            vc = vc_ref[:, lanes].astype(BF16)
            vp = vp_ref[:, lanes].astype(BF16)
            o2 = jnp.zeros((BLOCK, LANES), F32)
            for hh in range(2):
                h = 2 * hp + hh
                mine = (lane // HEAD_DIM) == hh
                qm = jnp.where(mine, q2, jnp.zeros_like(q2))
                s_c = lax.dot_general(qm, kc, _DOT_DIMS["nt"], preferred_element_type=F32) * ATT_SCALE
                s_p = lax.dot_general(qm, kp, _DOT_DIMS["nt"], preferred_element_type=F32) * ATT_SCALE
                s_c = jnp.where(cur_ok, s_c + bias_ref[h, :, BLOCK:], NEG_BIG)
                s_p = jnp.where(prev_ok, s_p + bias_ref[h, :, :BLOCK], NEG_BIG)
                m = jnp.maximum(jnp.max(s_c, axis=-1, keepdims=True), jnp.max(s_p, axis=-1, keepdims=True))
                e_c = jnp.exp(s_c - m)
                e_p = jnp.exp(s_p - m)
                den = jnp.sum(e_c, axis=-1, keepdims=True) + jnp.sum(e_p, axis=-1, keepdims=True)
                o_h = (jnp.dot((e_c / den).astype(BF16), vc, preferred_element_type=F32)
                       + jnp.dot((e_p / den).astype(BF16), vp, preferred_element_type=F32))
                o2 = o2 + jnp.where(mine, o_h, 0.0)
                lse_ref[h] = jnp.broadcast_to(m + jnp.log(den), (BLOCK, LANES))
            o_ref[:, lanes] = o2

    def col(j):
        return lambda r, n: (r * nb + n, j)

    def col_prev(j):
        return lambda r, n: (r * nb + jnp.maximum(n - 1, 0), j)

    blk = (BLOCK, DIL_WIDTH)
    return pl.pallas_call(
        body, name=f"dil_att_fwd_g{g}", grid=(dil, nb),
        in_specs=[pl.BlockSpec(blk, col(0)), pl.BlockSpec(blk, col(1)), pl.BlockSpec(blk, col_prev(1)),
                  pl.BlockSpec(blk, col(2)), pl.BlockSpec(blk, col_prev(2)),
                  pl.BlockSpec(bias.shape, _const_map(3))],
        out_specs=[pl.BlockSpec(blk, lambda r, n: (r * nb + n, 0)),
                   pl.BlockSpec((N_HEADS, BLOCK, LANES), lambda r, n: (0, r * nb + n, 0))],
        out_shape=[jax.ShapeDtypeStruct((t_len, DIL_WIDTH), F32),
                   jax.ShapeDtypeStruct((N_HEADS, t_len, LANES), F32)],
        compiler_params=_cparams(("arbitrary", "arbitrary"), VMEM_LIMIT),
    )(p, p, p, p, p, bias)


def _att_bwd_call(p, bias, o, lse, do, dlse, g, dil):
    t_len = p.shape[0]
    l_len = t_len // dil
    nb = l_len // BLOCK

    def body(q_ref, qn_ref, k_ref, v_ref, do_ref, don_ref, o_ref, on_ref, lse_ref, lsen_ref, dl_ref, dln_ref,
             bias_ref, dq_ref, dk_ref, dv_ref, dbias_ref, carry_ref):
        r = pl.program_id(0)
        n = pl.program_id(1)
        cur_ok, prev_band, lane = _att_masks()
        has_next = n + 1 < nb

        @pl.when(jnp.logical_and(r == 0, n == 0))
        def _():
            dbias_ref[...] = jnp.zeros_like(dbias_ref)

        @pl.when(n == 0)
        def _():
            carry_ref[...] = jnp.zeros_like(carry_ref)

        for hp in range(N_PAIRS):
            lanes = slice(hp * LANES, (hp + 1) * LANES)
            k2 = k_ref[:, lanes].astype(BF16)
            v2 = v_ref[:, lanes].astype(BF16)
            dk2 = jnp.zeros((BLOCK, LANES), F32)
            dv2 = jnp.zeros((BLOCK, LANES), F32)
            dq_cur = carry_ref[:, lanes]
            dq_next = jnp.zeros((BLOCK, LANES), F32)
            for hh in range(2):
                h = 2 * hp + hh
                mine = (lane // HEAD_DIM) == hh
                tiles = (
                    (q_ref, do_ref, o_ref, lse_ref, dl_ref, cur_ok, slice(BLOCK, 2 * BLOCK), None),
                    (qn_ref, don_ref, on_ref, lsen_ref, dln_ref, prev_band, slice(0, BLOCK), has_next),
                )
                if nb == 1:
                    tiles = tiles[:1]
                for ti, (qr, dor, orf, lr, dlr, ok, bcols, gate) in enumerate(tiles):
                    q2 = qr[:, lanes].astype(BF16)
                    qm = jnp.where(mine, q2, jnp.zeros_like(q2))
                    do_f = jnp.where(mine, dor[:, lanes], 0.0)
                    dom = do_f.astype(BF16)
                    s = lax.dot_general(qm, k2, _DOT_DIMS["nt"], preferred_element_type=F32) * ATT_SCALE
                    s = s + bias_ref[h, :, bcols]
                    if gate is not None:
                        ok = jnp.logical_and(ok, gate)
                    pr = jnp.where(ok, jnp.exp(jnp.minimum(s - lr[h], 0.0)), 0.0)
                    dp = lax.dot_general(dom, v2, _DOT_DIMS["nt"], preferred_element_type=F32)
                    delta = jnp.sum(do_f * orf[:, lanes], axis=-1, keepdims=True)
                    dl = jnp.sum(dlr[h], axis=-1, keepdims=True)
                    ds = pr * (dp - delta + dl)
                    dsb = ds.astype(BF16)
                    dq_h = jnp.where(mine, jnp.dot(dsb, k2, preferred_element_type=F32), 0.0) * ATT_SCALE
                    if ti == 0:
                        dq_cur = dq_cur + dq_h
                    else:
                        dq_next = dq_next + dq_h
                    dk2 = dk2 + lax.dot_general(dsb, qm, _DOT_DIMS["tn"], preferred_element_type=F32) * ATT_SCALE
                    dv2 = dv2 + lax.dot_general(pr.astype(BF16), dom, _DOT_DIMS["tn"], preferred_element_type=F32)
                    dbias_ref[h, :, bcols] += ds
            dq_ref[:, lanes] = dq_cur
            carry_ref[:, lanes] = dq_next
            dk_ref[:, lanes] = dk2
            dv_ref[:, lanes] = dv2

    def nxt(n):
        return jnp.minimum(n + 1, nb - 1)

    blk = (BLOCK, DIL_WIDTH)
    hblk = (N_HEADS, BLOCK, LANES)
    qcol = lambda j: (lambda r, n: (r * nb + n, j))
    q_next = lambda r, n: (r * nb + nxt(n), 0)
    rown = lambda r, n: (r * nb + n, 0)
    rown_next = lambda r, n: (r * nb + nxt(n), 0)
    hrow = lambda r, n: (0, r * nb + n, 0)
    hrow_next = lambda r, n: (0, r * nb + nxt(n), 0)
    sds = jax.ShapeDtypeStruct((t_len, DIL_WIDTH), F32)
    return pl.pallas_call(
        body, name=f"dil_att_bwd_g{g}", grid=(dil, nb),
        in_specs=[pl.BlockSpec(blk, qcol(0)), pl.BlockSpec(blk, q_next),
                  pl.BlockSpec(blk, qcol(1)), pl.BlockSpec(blk, qcol(2)),
                  pl.BlockSpec(blk, rown), pl.BlockSpec(blk, rown_next),
                  pl.BlockSpec(blk, rown), pl.BlockSpec(blk, rown_next),
                  pl.BlockSpec(hblk, hrow), pl.BlockSpec(hblk, hrow_next),
                  pl.BlockSpec(hblk, hrow), pl.BlockSpec(hblk, hrow_next),
                  pl.BlockSpec(bias.shape, _const_map(3))],
        out_specs=[pl.BlockSpec(blk, rown)] * 3 + [pl.BlockSpec(bias.shape, _const_map(3))],
        out_shape=[sds, sds, sds, jax.ShapeDtypeStruct(bias.shape, F32)],
        scratch_shapes=[pltpu.VMEM((BLOCK, DIL_WIDTH), F32)],
        compiler_params=_cparams(("arbitrary", "arbitrary"), VMEM_LIMIT),
    )(p, p, p, p, do, do, o, o, lse, lse, dlse, dlse, bias)


def _att_all_groups(ps, biases):
    outs = [_att_fwd_call(ps[g], biases[g], g, dil) for g, (_, dil) in enumerate(DIL_PATTERNS)]
    return tuple(o for o, _ in outs), tuple(l for _, l in outs)


@jax.custom_vjp
def _dilated_attention(ps, biases):
    return _att_all_groups(ps, biases)


def _dilated_attention_fwd(ps, biases):
    os_, lses = _att_all_groups(ps, biases)
    return (os_, lses), (ps, biases, os_, lses)


def _dilated_attention_bwd(res, cts):
    ps, biases, os_, lses = res
    dos, dlses = cts
    dps, dbiases = [], []
    for g, (_, dil) in enumerate(DIL_PATTERNS):
        dq, dk, dv, dbias = _att_bwd_call(ps[g], biases[g], os_[g], lses[g], dos[g], dlses[g], g, dil)
        dps.append(jnp.concatenate([dq, dk, dv], axis=1))
        dbiases.append(dbias)
    return tuple(dps), tuple(dbiases)


_dilated_attention.defvjp(_dilated_attention_fwd, _dilated_attention_bwd)


def _me():
    return lax.axis_index("x"), lax.axis_index("y"), lax.axis_index("c")


def _flip(me, f):
    return tuple((1 - m) if b else m for m, b in zip(me, f))


def _chip_of(d):
    return 2 * d[0] + d[1]


def _dev_of(d):
    return 4 * d[0] + 2 * d[1] + d[2]


EXCHANGE_CHUNKS = 8
CHIP_FLIPS = ((1, 0, 0), (0, 1, 0), (1, 1, 0))
ALL_FLIPS = tuple((a, b, c) for a in (0, 1) for b in (0, 1) for c in (0, 1) if a or b or c)
CORE_FLIP = (0, 0, 1)


def _n_chunks(rows):
    return EXCHANGE_CHUNKS if rows % (EXCHANGE_CHUNKS * PACK_ROWS) == 0 else 1


def _exchange(src, n_slots, transfers, name):
    _, rows, cols = src.shape
    chunks = _n_chunks(rows)
    rc = rows // chunks
    n = len(transfers) * chunks

    def body(src_ref, dst_ref, send_sems, recv_sems):
        me = _me()
        copies = []
        for q in range(chunks):
            for kk, (f, src_slot, dst_slot) in enumerate(transfers):
                peer = _flip(me, f)
                cp = pltpu.make_async_remote_copy(
                    src_ref=src_ref.at[src_slot(me, peer), pl.ds(q * rc, rc)],
                    dst_ref=dst_ref.at[dst_slot(me, peer), pl.ds(q * rc, rc)],
                    send_sem=send_sems.at[kk * chunks + q], recv_sem=recv_sems.at[kk * chunks + q],
                    device_id=peer, device_id_type=MESH)
                cp.start()
                copies.append(cp)
        for cp in copies:
            cp.wait()

    return pl.pallas_call(
        body, name=name,
        out_shape=jax.ShapeDtypeStruct((n_slots, rows, cols), src.dtype),
        in_specs=[pl.BlockSpec(memory_space=pl.ANY)],
        out_specs=pl.BlockSpec(memory_space=pl.ANY),
        scratch_shapes=[pltpu.SemaphoreType.DMA((n,)), pltpu.SemaphoreType.DMA((n,))],
    )(src)


def _set_slot(buf, block, index):
    return lax.dynamic_update_slice_in_dim(buf, block[None].astype(buf.dtype), index, axis=0)


def _chip_all_gather(src, name):
    got = _exchange(src[None], N_CHIPS, [(f, lambda me, peer: 0, lambda me, peer: _chip_of(me)) for f in CHIP_FLIPS], name)
    return _set_slot(got, src, _chip_of(_me()))


def _dev_all_gather(src, name):
    got = _exchange(src[None], N_DEV, [(f, lambda me, peer: 0, lambda me, peer: _dev_of(me)) for f in ALL_FLIPS], name)
    return _set_slot(got, src, _dev_of(_me()))


def _chip_scatter(src, name):
    got = _exchange(src, N_CHIPS, [(f, lambda me, peer: _chip_of(peer), lambda me, peer: _chip_of(me))
                                   for f in CHIP_FLIPS], name)
    chip = _chip_of(_me())
    return _set_slot(got, lax.dynamic_index_in_dim(src, chip, 0, keepdims=False), chip)


def _core_halves(src, name):
    s, _, half, cols = src.shape
    transfers = [(CORE_FLIP, (lambda me, peer, j=j: 2 * j + peer[2]), (lambda me, peer, j=j: j)) for j in range(s)]
    return _exchange(src.reshape(2 * s, half, cols), s, transfers, name)


def _core_all_gather(src, name):
    got = _exchange(src[None], 2, [(CORE_FLIP, lambda me, peer: 0, lambda me, peer: me[2])], name)
    return _set_slot(got, src, _me()[2])


def _two_level_gather(src, name):
    rows, cols = src.shape
    half = rows // 2
    chunks = _n_chunks(half)
    rc = half // chunks
    n = len(CHIP_FLIPS) * chunks

    def body(src_ref, g_ref, send1, recv1, send2, recv2):
        me = _me()
        c = me[2]
        sibling = _flip(me, CORE_FLIP)
        first, second = [], []
        for q in range(chunks):
            for kk, f in enumerate(CHIP_FLIPS):
                peer = _flip(me, f)
                cp = pltpu.make_async_remote_copy(
                    src_ref=src_ref.at[c, pl.ds(q * rc, rc)], dst_ref=g_ref.at[_chip_of(me), c, pl.ds(q * rc, rc)],
                    send_sem=send1.at[kk * chunks + q], recv_sem=recv1.at[kk * chunks + q],
                    device_id=peer, device_id_type=MESH)
                cp.start()
                first.append((cp, _chip_of(peer), kk * chunks + q, q))
        for cp, origin, idx, q in first:
            cp.wait_recv()
            fw = pltpu.make_async_remote_copy(
                src_ref=g_ref.at[origin, c, pl.ds(q * rc, rc)], dst_ref=g_ref.at[origin, c, pl.ds(q * rc, rc)],
                send_sem=send2.at[idx], recv_sem=recv2.at[idx],
                device_id=sibling, device_id_type=MESH)
            fw.start()
            second.append(fw)
        for fw in second:
            fw.wait_recv()
        for cp, _, _, _ in first:
            cp.wait_send()
        for fw in second:
            fw.wait_send()

    got = pl.pallas_call(
        body, name=name,
        out_shape=jax.ShapeDtypeStruct((N_CHIPS, 2, half, cols), src.dtype),
        in_specs=[pl.BlockSpec(memory_space=pl.ANY)],
        out_specs=pl.BlockSpec(memory_space=pl.ANY),
        scratch_shapes=[pltpu.SemaphoreType.DMA((n,)) for _ in range(4)],
    )(src.reshape(2, half, cols))
    return _set_slot(got.reshape(N_CHIPS, rows, cols), src, _chip_of(_me()))


def _sum_slots(x, name):
    s, rows, cols = x.shape
    tile = _pick(rows, (512, 256, 128, 64, 32, 16, 8))

    def body(x_ref, o_ref):
        acc = x_ref[0].astype(F32)
        for i in range(1, s):
            acc = acc + x_ref[i].astype(F32)
        o_ref[...] = acc

    return pl.pallas_call(
        body, name=name, grid=(rows // tile,),
        in_specs=[pl.BlockSpec((s, tile, cols), lambda i: (0, i, 0))],
        out_specs=pl.BlockSpec((tile, cols), lambda i: (i, 0)),
        out_shape=jax.ShapeDtypeStruct((rows, cols), F32),
        compiler_params=_cparams(("parallel",), VMEM_LIMIT),
    )(x)


def _sum_slots_own(recv, send, chip, name):
    s, rows, cols = recv.shape
    tile = _pick(rows, (512, 256, 128, 64, 32, 16, 8))

    def body(chip_ref, recv_ref, own_ref, o_ref):
        acc = None
        for j in range(s):
            term = jnp.where(chip_ref[0] == j, own_ref[0], recv_ref[j]).astype(F32)
            acc = term if acc is None else acc + term
        o_ref[...] = acc

    grid_spec = pltpu.PrefetchScalarGridSpec(
        num_scalar_prefetch=1, grid=(rows // tile,),
        in_specs=[pl.BlockSpec((s, tile, cols), lambda i, c: (0, i, 0)),
                  pl.BlockSpec((1, tile, cols), lambda i, c: (c[0], i, 0))],
        out_specs=pl.BlockSpec((tile, cols), lambda i, c: (i, 0)))
    return pl.pallas_call(
        body, name=name, grid_spec=grid_spec,
        out_shape=jax.ShapeDtypeStruct((rows, cols), F32),
        compiler_params=_cparams(("arbitrary",), VMEM_LIMIT),
    )(jnp.reshape(chip, (1,)).astype(jnp.int32), recv, send)


def _add_pairs(a, b, name):
    s, rows, cols = a.shape
    tile = _pick(rows, (512, 256, 128, 64, 32, 16, 8))

    def body(a_ref, b_ref, o_ref):
        o_ref[...] = (a_ref[...].astype(F32) + b_ref[...].astype(F32)).astype(o_ref.dtype)

    spec = pl.BlockSpec((1, tile, cols), lambda j, i: (j, i, 0))
    return pl.pallas_call(
        body, name=name, grid=(s, rows // tile),
        in_specs=[spec, spec], out_specs=spec,
        out_shape=jax.ShapeDtypeStruct(a.shape, a.dtype),
        compiler_params=_cparams(("parallel", "parallel"), VMEM_LIMIT),
    )(a, b)


def _adamw(w, g, m, v, name):
    rows, cols = w.shape
    tile = rows
    if rows * cols * 4 > 2 * 1024 * 1024:
        tile = _pick(rows, (256, 128, 64, 32, 16, 8))
    c1 = 1.0 / (1.0 - ADAM_B1 ** ADAM_STEP)
    c2 = 1.0 / (1.0 - ADAM_B2 ** ADAM_STEP)

    def body(w_ref, g_ref, m_ref, v_ref, d_ref, nm_ref, nv_ref):
        gv = g_ref[...]
        nm = ADAM_B1 * m_ref[...] + (1.0 - ADAM_B1) * gv
        nv = ADAM_B2 * v_ref[...] + (1.0 - ADAM_B2) * (gv * gv)
        m_hat = nm * c1
        v_hat = nv * c2
        d_ref[...] = -ADAM_LR * (m_hat / (jnp.sqrt(v_hat) + ADAM_EPS) + ADAM_WD * w_ref[...])
        nm_ref[...] = nm
        nv_ref[...] = nv

    spec = pl.BlockSpec((tile, cols), lambda i: (i, 0))
    sds = jax.ShapeDtypeStruct((rows, cols), F32)
    return pl.pallas_call(
        body, name=name, grid=(rows // tile,),
        in_specs=[spec] * 4, out_specs=[spec] * 3, out_shape=[sds] * 3,
        compiler_params=_cparams(("parallel",), VMEM_LIMIT),
    )(w, g, m, v)


def _ada_fwd(c_all, ada_w, ada_b_cols):
    n_col = ada_w.shape[2]

    def body(c_ref, w_ref, b_ref, o_ref):
        cv = c_ref[...]
        cond = (cv * jax.nn.sigmoid(cv)).astype(BF16)
        o_ref[0] = jnp.dot(cond, w_ref[0].astype(BF16), preferred_element_type=F32) + b_ref[0]

    return pl.pallas_call(
        body, name="ada_fwd", grid=(DEPTH,),
        in_specs=[pl.BlockSpec(c_all.shape, lambda i: (0, 0)),
                  pl.BlockSpec((1, D_MODEL, n_col), lambda i: (i, 0, 0)),
                  pl.BlockSpec((1, 1, n_col), lambda i: (i, 0, 0))],
        out_specs=pl.BlockSpec((1, N_DEV, n_col), lambda i: (i, 0, 0)),
        out_shape=jax.ShapeDtypeStruct((DEPTH, N_DEV, n_col), F32),
        compiler_params=_cparams(("parallel",), VMEM_LIMIT),
    )(c_all, ada_w, ada_b_cols)


def _ada_grad(c_all_t, dmod_cols):
    n_col = dmod_cols.shape[2]

    def body(c_ref, d_ref, o_ref):
        cv = c_ref[...]
        cond = cv * jax.nn.sigmoid(cv)
        o_ref[0] = jnp.dot(cond, d_ref[0], precision=lax.Precision.HIGHEST, preferred_element_type=F32)

    return pl.pallas_call(
        body, name="ada_grad", grid=(DEPTH,),
        in_specs=[pl.BlockSpec(c_all_t.shape, lambda i: (0, 0)),
                  pl.BlockSpec((1, LANES, n_col), lambda i: (i, 0, 0))],
        out_specs=pl.BlockSpec((1, D_MODEL, n_col), lambda i: (i, 0, 0)),
        out_shape=jax.ShapeDtypeStruct((DEPTH, D_MODEL, n_col), F32),
        compiler_params=_cparams(("parallel",), VMEM_LIMIT),
    )(c_all_t, dmod_cols)


ROW_TILE = 256


def _shift_rows(a, n=1):
    return jnp.pad(a, ((n, 0), (0, 0)))[:-n]


def _modulate(x, sc, sh, name):
    def fn(x, sc, sh):
        return (x * (1.0 + sc) + sh,)

    t = x.shape[0]
    return _fused(fn, [_tiled(x, ROW_TILE), _shared(sc), _shared(sh)],
                  [_tiled_out(t, D_MODEL, ROW_TILE)], (t // ROW_TILE,), name)[0]


def _resid_ln_mod(x, y, gate, ln_g, ln_b, sc, sh, name):
    def fn(x, y, gate, ln_g, ln_b, sc, sh):
        x1 = _layer_norm_rows(ALPHA * x + (1.0 + gate) * y, ln_g, ln_b)
        return x1, x1 * (1.0 + sc) + sh

    t = x.shape[0]
    return _fused(fn, [_tiled(x, ROW_TILE), _tiled(y, ROW_TILE)] + [_shared(a) for a in (gate, ln_g, ln_b, sc, sh)],
                  [_tiled_out(t, D_MODEL, ROW_TILE)] * 2, (t // ROW_TILE,), name)


def _resid_ln_loss(x, y, gate, ln_g, ln_b, target, name):
    def fn(x, y, gate, ln_g, ln_b, target):
        x1 = _layer_norm_rows(ALPHA * x + (1.0 + gate) * y, ln_g, ln_b)
        err = jnp.square(x1 - target)
        per_row = jnp.mean(err, axis=-1, keepdims=True)
        return (0.5 * jnp.sum(per_row, axis=0, keepdims=True),)

    t = x.shape[0]
    return _fused(fn, [_tiled(x, ROW_TILE), _tiled(y, ROW_TILE)] + [_shared(a) for a in (gate, ln_g, ln_b)]
                  + [_tiled(target, ROW_TILE)],
                  [((1, 1), (1, 1), _const_map(2), "a")], (t // ROW_TILE,), name)[0]


def _mlp(u, w1, s1, w2, s2, name):
    def run(u, w1, w2):
        h, act = _mm(u, w1, "nn", name + "_w1_fwd", square_relu=True)
        return _mm(act, w2, "nn", name + "_w2_fwd"), (u, w1, w2, h, act)

    @jax.custom_vjp
    def op(u, w1, s1, w2, s2):
        return run(u, w1, w2)[0]

    def fwd(u, w1, s1, w2, s2):
        return run(u, w1, w2)

    def bwd(res, dy):
        u, w1, w2, h, act = res
        dh = _mm(dy, w2, "nt", name + "_w2_dx", pre_act=h)
        dw2 = _mm(act, dy, "tn", name + "_w2_dw", narrow_out=True)
        du = _mm(dh, w1, "nt", name + "_w1_dx")
        dw1 = _mm(u, dh, "tn", name + "_w1_dw", narrow_out=True)
        return du, jnp.zeros_like(w1), dw1, jnp.zeros_like(w2), dw2

    op.defvjp(fwd, bwd)
    return op(u, w1, s1, w2, s2)


AB_PIECES = (("r", 0, 512, 512), ("k", 512, 512, 512), ("v", 1024, 512, 512),
             ("wd", 1536, 64, 128), ("ad", 1600, 64, 128), ("gd", 1664, 160, 256),
             ("h", 1824, 512, 512), ("bg", 2336, 512, 512), ("cg", 2848, 512, 512))
AB_PAD_COLS = sum(p[3] for p in AB_PIECES)


def _regroup_cols(w):
    parts = []
    for _, start, width, padded in AB_PIECES:
        piece = w[..., start:start + width]
        if padded != width:
            piece = jnp.pad(piece, [(0, 0)] * (w.ndim - 1) + [(0, padded - width)])
        parts.append(piece)
    return jnp.concatenate(parts, axis=-1)


def _pad_rows(w, rows):
    return jnp.pad(w, ((0, rows - w.shape[0]), (0, 0)))


def _rwkv_shortconv(u, big, wts, wsrc, gslot, late_shapes):
    t = u.shape[0]
    p = _linear(u, _regroup_cols(big["ab_w_in"][0]), _regroup_cols(wts["ab_w_in"][0]), "ab_in")
    mu = _regroup_cols(jnp.pad(wts["rw_mu"], ((0, 0), (0, AB_PROJ - RW_PROJ))))
    w_up = _pad_rows(wts["rw_w_up"][0], 128)
    a_up = _pad_rows(wts["rw_a_up"][0], 128)
    g_up = _pad_rows(wts["rw_g_up"][0], 256)

    def pre(rp, rs, kp, ks, vp, vs, wdp, wds, adp, ads, gdp, gds, h, cg,
            mu_r, mu_k, mu_v, mu_w, mu_a, mu_g, w0, w_up, a0, a_up, g_up, k_k, k_a):
        def mix(pv, sv, m):
            return pv + m * (sv - pv)

        r, k, v = mix(rp, rs, mu_r), mix(kp, ks, mu_k), mix(vp, vs, mu_v)
        wd, ad, gd = mix(wdp, wds, mu_w), mix(adp, ads, mu_a), mix(gdp, gds, mu_g)
        logw = -_softplus(-(w0 + _bdot(jnp.tanh(wd), w_up))) - 0.5
        decay = jnp.exp(-jnp.exp(logw))
        iclr = jax.nn.sigmoid(a0 + _bdot(ad, a_up))
        gate = _bdot(jax.nn.sigmoid(gd), g_up)
        kk = k * k_k
        kk = kk / jnp.maximum(jnp.sqrt(_head_sum(kk * kk)), 1e-12)
        k_h = k * (1.0 + (iclr - 1.0) * k_a)
        return r, decay, k_h, v, -kk, kk * iclr, gate, cg * h

    tile = ROW_TILE
    names = [q[0] for q in AB_PIECES]
    cuts = list(np.cumsum([q[3] for q in AB_PIECES])[:-1])
    pp = dict(zip(names, jnp.split(p, cuts, axis=1)))
    mp = dict(zip(names, jnp.split(mu, cuts, axis=1)))

    ins = []
    for name in ("r", "k", "v", "wd", "ad", "gd"):
        ins += [_tiled(pp[name], tile), _tiled(_shift_rows(pp[name]), tile)]
    ins += [_tiled(pp["h"], tile), _tiled(pp["cg"], tile)]
    ins += [_shared(mp[name]) for name in ("r", "k", "v", "wd", "ad", "gd")]
    ins += [_shared(a) for a in (wts["rw_w0"], w_up, wts["rw_a0"], a_up, g_up, wts["rw_k_k"], wts["rw_k_a"])]
    outs = [_tiled_out(t, RW_WIDTH, tile)] * 8
    r, decay, k_h, v, a, b, gate, z = _fused(pre, ins, outs, (t // tile,), "rwkv_pre")

    y, gathered, token = _rwkv_scan(r, decay, k_h, v, a, b, wsrc, gslot)
    shard_shapes = [s[:SHARDED[n]] + (s[SHARDED[n]] // N_CHIPS,) + s[SHARDED[n] + 1:] for n, s in late_shapes.items()]
    late_big = {n: _join_chips(part, SHARDED[n])
                for n, part in zip(late_shapes, _unpack_chips(gathered, shard_shapes, own=wsrc))}
    late_slots = _gradient_slots(token, late_shapes, tuple(late_shapes))

    conv_w = wts["sc_conv_w"][0]
    r_k = wts["rw_r_k"].reshape(1, RW_WIDTH)

    def post(y, r, k_h, v, gate, bg, z, z1, z2, lnx_g, lnx_b, r_k, c0, c1, c2):
        mean = _head_sum(y) * (1.0 / HEAD_DIM)
        yc = y - mean
        var = _head_sum(yc * yc) * (1.0 / HEAD_DIM)
        yn = yc * lax.rsqrt(var + RW_GN_EPS) * lnx_g + lnx_b
        bonus = _head_sum(r * k_h * r_k) * v
        return (yn + bonus) * gate, bg * (c0 * z2 + c1 * z1 + c2 * z)

    ins = [_tiled(a_, tile) for a_ in (y, r, k_h, v, gate, pp["bg"])]
    ins += [_tiled(a_, tile) for a_ in (z, _shift_rows(z, 1), _shift_rows(z, 2))]
    ins += [_shared(a_) for a_ in (wts["rw_lnx_g"], wts["rw_lnx_b"], r_k, conv_w[0:1], conv_w[1:2], conv_w[2:3])]
    y_a, y_b = _fused(post, ins, [_tiled_out(t, RW_WIDTH, tile)] * 2, (t // tile,), "rwkv_post")
    out = _linear(jnp.concatenate([y_a, y_b], axis=1), late_big["ab_w_out"][0], late_slots["ab_w_out"][0], "ab_out")
    return out, late_big, late_slots


def _t5_bucket_np(dist):
    exact = N_BUCKETS // 2
    logd = np.log(np.maximum(dist, 1).astype(np.float32) / exact) / math.log(MAX_DISTANCE / exact)
    large = np.minimum(exact + (logd * (N_BUCKETS - exact)).astype(np.int32), N_BUCKETS - 1)
    return np.where(dist < exact, dist, large)


def _merge_groups(os_, lses, name):
    t = os_[0].shape[0]
    tile = ROW_TILE

    def fn(o0, o1, o2, l0, l1, l2):
        lane = lax.broadcasted_iota(jnp.int32, (1, LANES), 1)
        lo = lane < HEAD_DIM
        ls = [jnp.where(lo, l[0], l[1]) for l in (l0, l1, l2)]
        m = jnp.maximum(jnp.maximum(ls[0], ls[1]), ls[2])
        es = [jnp.exp(l - m) for l in ls]
        den = es[0] + es[1] + es[2]
        return ((es[0] * o0 + es[1] * o1 + es[2] * o2) / den,)

    ins = [(o, (tile, LANES), lambda i, hp: (i, hp), "t") for o in os_]
    ins += [(l, (2, tile, LANES), lambda i, hp: (hp, i, 0), "t") for l in lses]
    outs = [((t, DIL_WIDTH), (tile, LANES), lambda i, hp: (i, hp), "t")]
    return _fused(fn, ins, outs, (t // tile, N_PAIRS), name)[0]


def _residue_major(a, dil, axis=0):
    if dil == 1:
        return a
    shp = a.shape
    split = a.reshape(shp[:axis] + (shp[axis] // dil, dil) + shp[axis + 1:])
    return jnp.swapaxes(split, axis, axis + 1).reshape(shp)


def _position_major(a, dil, axis=0):
    if dil == 1:
        return a
    shp = a.shape
    split = a.reshape(shp[:axis] + (dil, shp[axis] // dil) + shp[axis + 1:])
    return jnp.swapaxes(split, axis, axis + 1).reshape(shp)


def _dilated_mixer(u, big, wts):
    group_cols = 3 * DIL_WIDTH
    ps = []
    for g, (_, dil) in enumerate(DIL_PATTERNS):
        cols = slice(g * group_cols, (g + 1) * group_cols)
        ps.append(_linear(_residue_major(u, dil), big["dil_w_qkv"][0][:, cols], wts["dil_w_qkv"][g][0],
                          f"dil_qkv{g}"))
    qi = np.arange(BLOCK)[:, None]
    ki = np.arange(2 * BLOCK)[None, :]
    rel = BLOCK + qi - ki
    biases = []
    for g, (window, dil) in enumerate(DIL_PATTERNS):
        span = window // dil
        bucket = _t5_bucket_np(np.clip(rel, 0, span) * dil).reshape(-1)
        onehot = jnp.asarray(np.eye(N_BUCKETS, dtype=np.float32)[bucket])
        table = wts["rel_bias"][:, g * N_HEADS:(g + 1) * N_HEADS]
        bias = jnp.dot(onehot, table, precision=lax.Precision.HIGHEST)
        biases.append(jnp.transpose(bias.reshape(BLOCK, 2 * BLOCK, N_HEADS), (2, 0, 1)))
    os_, lses = _dilated_attention(tuple(ps), tuple(biases))
    os_ = [_position_major(o, dil) for o, (_, dil) in zip(os_, DIL_PATTERNS)]
    lses = [_position_major(l, dil, axis=1) for l, (_, dil) in zip(lses, DIL_PATTERNS)]
    o = _merge_groups(os_, lses, "dil_merge")
    return _linear(o, big["dil_w_out"][0], wts["dil_w_out"][0], "dil_out")


def _forward_local(x, mods, big, wts, wsrc, gslot, late_shapes, target):
    u = _modulate(x, mods[0, 1], mods[0, 0], "mod_in")
    for i in range(DEPTH):
        sh2, sc2, g1, g2 = mods[i, 3], mods[i, 4], mods[i, 2], mods[i, 5]
        if i == 0:
            y, late_big, late_slots = _rwkv_shortconv(u, big, wts, wsrc, gslot, late_shapes)
            big = {**big, **late_big}
            wts = {**wts, **late_slots}
        else:
            y = _dilated_mixer(u, big, wts)
        x, u = _resid_ln_mod(x, y, g1, wts["ln_g"][i, 0:1], wts["ln_b"][i, 0:1], sc2, sh2, f"ln_mix{i}")
        y = _mlp(u, big["mlp_w1"][i], wts["mlp_w1"][i], big["mlp_w2"][i], wts["mlp_w2"][i], f"mlp{i}")
        if i + 1 < DEPTH:
            x, u = _resid_ln_mod(x, y, g2, wts["ln_g"][i, 1:2], wts["ln_b"][i, 1:2],
                                 mods[i + 1, 1], mods[i + 1, 0], f"ln_mlp{i}")
        else:
            return _resid_ln_loss(x, y, g2, wts["ln_g"][i, 1:2], wts["ln_b"][i, 1:2], target, "ln_loss")


SHARDED = {"ab_w_in": 2, "ab_w_out": 1, "dil_w_qkv": 2, "dil_w_out": 2, "mlp_w1": 2, "mlp_w2": 1,
           "ln_g": 2, "ln_b": 2, "rw_w_up": 2, "rw_a_up": 2, "rw_g_up": 2, "sc_conv_w": 2}
FIRST_MIXER = ("ab_w_in",)
LATER_LAYERS = ("ab_w_out", "dil_w_qkv", "dil_w_out", "mlp_w1", "mlp_w2")
SMALL_SHARDED = ("ln_g", "ln_b", "rw_w_up", "rw_a_up", "rw_g_up", "sc_conv_w")
REPLICATED = ("ada_b", "rw_mu", "rw_w0", "rw_a0", "rw_k_k", "rw_k_a", "rw_r_k", "rw_lnx_g", "rw_lnx_b", "rel_bias")
WEIGHT_ORDER = ("ada_w", "ada_b", "ln_g", "ln_b", "ab_w_in", "rw_mu", "rw_w0", "rw_w_up", "rw_a0", "rw_a_up",
                "rw_g_up", "rw_k_k", "rw_k_a", "rw_r_k", "rw_lnx_g", "rw_lnx_b", "sc_conv_w", "ab_w_out",
                "dil_w_qkv", "dil_w_out", "rel_bias", "mlp_w1", "mlp_w2")


PACK_ROWS = 16


def _rows_of(n_elems):
    return -(-n_elems // (ROW_W * PACK_ROWS)) * PACK_ROWS


def _to_rows(a):
    flat = a.reshape(-1)
    rows = _rows_of(flat.shape[0])
    if rows * ROW_W != flat.shape[0]:
        flat = jnp.pad(flat, (0, rows * ROW_W - flat.shape[0]))
    return flat.reshape(rows, ROW_W)


def _from_rows(rows, shape):
    n = int(np.prod(shape))
    return rows.reshape(-1)[:n].reshape(shape)


def _split_chips(full, axis):
    shp = full.shape
    parts = full.reshape(shp[:axis] + (N_CHIPS, shp[axis] // N_CHIPS) + shp[axis + 1:])
    return jnp.moveaxis(parts, axis, 0)


def _join_chips(parts, axis):
    moved = jnp.moveaxis(parts, 0, axis)
    shp = moved.shape
    return moved.reshape(shp[:axis] + (shp[axis] * shp[axis + 1],) + shp[axis + 2:])


def _pack_rows(arrays, row_multiple=256):
    blocks = [_to_rows(a) for a in arrays]
    total = sum(b.shape[0] for b in blocks)
    pad = (-total) % row_multiple
    if pad:
        blocks.append(jnp.zeros((pad, ROW_W), blocks[0].dtype))
    return jnp.concatenate(blocks, axis=0)


def _unpack_rows(buf, shapes):
    out, r0 = [], 0
    for shp in shapes:
        n = _rows_of(int(np.prod(shp)))
        out.append(_from_rows(buf[r0:r0 + n], shp))
        r0 += n
    return out


def _pack_chips(parts):
    blocks = []
    for p in parts:
        flat = p.reshape(N_CHIPS, -1)
        rows = _rows_of(flat.shape[1])
        if rows * ROW_W != flat.shape[1]:
            flat = jnp.pad(flat, ((0, 0), (0, rows * ROW_W - flat.shape[1])))
        blocks.append(flat.reshape(N_CHIPS, rows, ROW_W))
    total = sum(b.shape[1] for b in blocks)
    pad = (-total) % 256
    if pad:
        blocks.append(jnp.zeros((N_CHIPS, pad, ROW_W), blocks[0].dtype))
    return jnp.concatenate(blocks, axis=1)


def _unpack_chips(buf, shapes, own=None):
    mine = lax.broadcasted_iota(jnp.int32, (N_CHIPS, 1, 1), 0) == _chip_of(_me()) if own is not None else None
    out, r0 = [], 0
    for shp in shapes:
        size = int(np.prod(shp))
        n = _rows_of(size)
        rows = buf[:, r0:r0 + n]
        if own is not None:
            rows = jnp.where(mine, own[None, r0:r0 + n], rows)
        out.append(rows.reshape(N_CHIPS, -1)[:, :size].reshape((N_CHIPS,) + tuple(shp)))
        r0 += n
    return out


def _gradient_slots(token, full_shapes, names):
    def slots():
        out = {n: jnp.zeros(full_shapes[n], BF16) for n in names}
        shp = full_shapes["dil_w_qkv"]
        out["dil_w_qkv"] = tuple(jnp.zeros(shp[:-1] + (shp[-1] // N_GROUPS,), BF16) for _ in range(N_GROUPS))
        return out

    @jax.custom_vjp
    def route(token):
        return slots()

    def fwd(token):
        return slots(), None

    def bwd(_, d):
        d = {**d, "dil_w_qkv": jnp.concatenate(d["dil_w_qkv"], axis=-1)}
        return (_pack_chips([_split_chips(d[n], SHARDED[n]).astype(BF16) for n in names]),)

    route.defvjp(fwd, bwd)
    return route(token)


def _as2d(a):
    return a.reshape(-1, a.shape[-1])


def kernel(x, c, ada_w, ada_b, ln_g, ln_b, ab_w_in, rw_mu, rw_w0, rw_w_up, rw_a0, rw_a_up, rw_g_up, rw_k_k, rw_k_a, rw_r_k, rw_lnx_g, rw_lnx_b, sc_conv_w, ab_w_out, dil_w_qkv, dil_w_out, rel_bias, mlp_w1, mlp_w2, loss_target, m_ada_w, m_ada_b, m_ln_g, m_ln_b, m_ab_w_in, m_rw_mu, m_rw_w0, m_rw_w_up, m_rw_a0, m_rw_a_up, m_rw_g_up, m_rw_k_k, m_rw_k_a, m_rw_r_k, m_rw_lnx_g, m_rw_lnx_b, m_sc_conv_w, m_ab_w_out, m_dil_w_qkv, m_dil_w_out, m_rel_bias, m_mlp_w1, m_mlp_w2, v_ada_w, v_ada_b, v_ln_g, v_ln_b, v_ab_w_in, v_rw_mu, v_rw_w0, v_rw_w_up, v_rw_a0, v_rw_a_up, v_rw_g_up, v_rw_k_k, v_rw_k_a, v_rw_r_k, v_rw_lnx_g, v_rw_lnx_b, v_sc_conv_w, v_ab_w_out, v_dil_w_qkv, v_dil_w_out, v_rel_bias, v_mlp_w1, v_mlp_w2):
    args = dict(locals())
    w_in = {n: args[n] for n in WEIGHT_ORDER}
    m_in = {n: args["m_" + n] for n in WEIGHT_ORDER}
    v_in = {n: args["v_" + n] for n in WEIGHT_ORDER}
    me = _me()
    chip = _chip_of(me)
    dev = _dev_of(me)

    c_all = _dev_all_gather(c, "gather_c")[:, 0, :]
    n_col = ada_w.shape[2]
    ada_b_cols = lax.dynamic_slice_in_dim(ada_b, chip * n_col, n_col, axis=1)[:, None, :]
    mod_cols = _ada_fwd(c_all, ada_w, ada_b_cols)

    first_buf = _pack_rows([w_in[n].astype(BF16) for n in FIRST_MIXER])
    first_all = _two_level_gather(first_buf, "gather_first")
    late_buf = _pack_rows([w_in[n].astype(BF16) for n in LATER_LAYERS])
    small_buf = _pack_rows([mod_cols] + [w_in[n] for n in SMALL_SHARDED], row_multiple=PACK_ROWS)
    small_all = _chip_all_gather(small_buf, "gather_small")

    def full_shape(n):
        shp = w_in[n].shape
        return shp[:SHARDED[n]] + (shp[SHARDED[n]] * N_CHIPS,) + shp[SHARDED[n] + 1:]

    wts = {n: w_in[n] for n in REPLICATED}
    big = {}
    for n, part in zip(FIRST_MIXER, _unpack_chips(first_all, [w_in[n].shape for n in FIRST_MIXER])):
        big[n] = _join_chips(part, SHARDED[n])
        wts[n] = jnp.zeros(full_shape(n), BF16)
    small_parts = _unpack_chips(small_all, [mod_cols.shape] + [w_in[n].shape for n in SMALL_SHARDED])
    for n, part in zip(SMALL_SHARDED, small_parts[1:]):
        wts[n] = _join_chips(part, SHARDED[n])
    mod_all = _join_chips(small_parts[0], 2)
    mods = lax.dynamic_slice_in_dim(mod_all, dev, 1, axis=1).reshape(DEPTH, 6, 1, D_MODEL)
    late_shapes = {n: full_shape(n) for n in LATER_LAYERS}
    late_slot = jnp.zeros(late_buf.shape, F32)

    def local_loss(xv, modv, wv, slot):
        return _forward_local(xv, modv, big, wv, late_buf, slot, late_shapes, loss_target[0])[0, 0]

    loss_local, (grad_x, dmods, dw, late_part) = jax.value_and_grad(local_loss, argnums=(0, 1, 2, 3))(
        x[0], mods, wts, late_slot)

    small_row = jnp.concatenate([dmods.reshape(-1)] + [dw[n].reshape(-1) for n in REPLICATED[1:]]
                                + [loss_local.reshape(1)])
    n_small = small_row.shape[0]
    n_small_pad = -(-n_small // LANES) * LANES
    small_row = jnp.pad(small_row, (0, n_small_pad - n_small))[None, :]
    rows_all = _dev_all_gather(small_row, "gather_small_grads")
    small_sum = _sum_slots(rows_all, "sum_small_grads")
    loss = small_sum[0, n_small - 1]

    dmod_all = rows_all[:, 0, :DEPTH * 6 * D_MODEL].reshape(N_DEV, DEPTH, 6 * D_MODEL)
    dmod_cols = lax.dynamic_slice_in_dim(dmod_all, chip * n_col, n_col, axis=2)
    dmod_cols = jnp.pad(jnp.moveaxis(dmod_cols, 0, 1), ((0, 0), (0, LANES - N_DEV), (0, 0)))
    c_all_t = jnp.pad(c_all.T, ((0, 0), (0, LANES - N_DEV)))
    grads = {"ada_w": _ada_grad(c_all_t, dmod_cols)}
    grads["ada_b"] = small_sum[0, :DEPTH * 6 * D_MODEL].reshape(ada_b.shape)
    r0 = DEPTH * 6 * D_MODEL
    for n in REPLICATED[1:]:
        size = int(np.prod(w_in[n].shape))
        grads[n] = small_sum[0, r0:r0 + size].reshape(w_in[n].shape)
        r0 += size

    sharded_names = FIRST_MIXER + SMALL_SHARDED
    send = _pack_chips([_split_chips(dw[n], SHARDED[n]).astype(BF16) for n in sharded_names])
    n_rows = send.shape[1]
    send = send.reshape(N_CHIPS, 2, n_rows // 2, ROW_W)
    theirs = _core_halves(send, "swap_halves")
    mine = lax.dynamic_index_in_dim(send, me[2], 1, keepdims=False)
    chip_part = _add_pairs(mine, theirs, "sum_cores")
    recv = _chip_scatter(chip_part, "scatter_grads")
    half_sum = _sum_slots(recv, "sum_chips")
    g_rows = _core_all_gather(half_sum, "gather_halves").reshape(n_rows, ROW_W)
    for n, g in zip(sharded_names, _unpack_rows(g_rows, [w_in[n].shape for n in sharded_names])):
        grads[n] = g

    late_rows = _sum_slots(_core_all_gather(late_part, "swap_late"), "sum_cores_late")
    for n, g in zip(LATER_LAYERS, _unpack_rows(late_rows, [w_in[n].shape for n in LATER_LAYERS])):
        grads[n] = g

    deltas, new_m, new_v = {}, {}, {}
    for n in WEIGHT_ORDER:
        shp = w_in[n].shape
        d, nm, nv = _adamw(_as2d(w_in[n]), _as2d(grads[n]), _as2d(m_in[n]), _as2d(v_in[n]), "adamw_" + n)
        deltas[n], new_m[n], new_v[n] = d.reshape(shp), nm.reshape(shp), nv.reshape(shp)

    return (loss, grad_x[None], *[grads[n] for n in WEIGHT_ORDER], *[deltas[n] for n in WEIGHT_ORDER],
            *[new_m[n] for n in WEIGHT_ORDER], *[new_v[n] for n in WEIGHT_ORDER])
```

```python
import functools
import math

import numpy as np
import jax
import jax.numpy as jnp
from jax import lax
from jax.experimental import pallas as pl
from jax.experimental.pallas import tpu as pltpu

F32 = jnp.float32
BF16 = jnp.bfloat16
MESH = pl.DeviceIdType.MESH

D_MODEL = 1024
DEPTH = 2
RW_WIDTH = 512
HEAD_DIM = 64
N_HEADS = 8
RW_DECAY_RANK = 64
RW_ICLR_RANK = 64
RW_GATE_RANK = 160
RW_GN_EPS = 64e-5
RW_PROJ = 3 * RW_WIDTH + RW_DECAY_RANK + RW_ICLR_RANK + RW_GATE_RANK
SC_WIDTH = 512
AB_PROJ = RW_PROJ + 3 * SC_WIDTH
DIL_PATTERNS = ((128, 1), (512, 4), (2048, 16))
N_GROUPS = 3
DIL_WIDTH = 512
DIL_PROJ = N_GROUPS * 3 * DIL_WIDTH
BLOCK = 128
N_BUCKETS = 32
MAX_DISTANCE = 2048
D_FF = 4 * D_MODEL
ALPHA = (2 * DEPTH) ** 0.25
LN_EPS = 1e-5
ADAM_LR = 0.001
ADAM_B1 = 0.9
ADAM_B2 = 0.999
ADAM_EPS = 1e-08
ADAM_WD = 0.01
ADAM_STEP = 10

N_CHIPS = 4
N_DEV = 8
LANES = 128
ROW_W = 1024
SCAN_CHUNK = 32
VMEM_LIMIT = 48 * 1024 * 1024
NEG_BIG = -1e30


def _pick(n, cands):
    for c in cands:
        if n % c == 0:
            return c
    return n


def _cparams(sem=None, vmem=None):
    return pltpu.CompilerParams(dimension_semantics=sem, vmem_limit_bytes=vmem)


_DOT_DIMS = {
    "nn": (((1,), (0,)), ((), ())),
    "nt": (((1,), (1,)), ((), ())),
    "tn": (((0,), (0,)), ((), ())),
}


def _mm(a, b, mode, name, square_relu=False, pre_act=None, narrow_out=False):
    if mode == "nn":
        (m, k), (_, n) = a.shape, b.shape
    elif mode == "nt":
        (m, k), (n, _) = a.shape, b.shape
    else:
        (k, m), (_, n) = a.shape, b.shape
    tm = _pick(m, (1024, 512, 256, 128))
    tn = _pick(n, (1024, 896, 768, 512, 384, 256, 128))
    wide_k = mode != "tn" and pre_act is None
    tk = _pick(k, (2048, 1792, 1024, 512, 256, 128) if wide_k else (1024, 512, 256, 128))
    nk = k // tk
    if mode == "tn":
        a_spec = pl.BlockSpec((tk, tm), lambda i, j, kk: (kk, i))
    else:
        a_spec = pl.BlockSpec((tm, tk), lambda i, j, kk: (i, kk))
    if mode == "nt":
        b_spec = pl.BlockSpec((tn, tk), lambda i, j, kk: (j, kk))
    else:
        b_spec = pl.BlockSpec((tk, tn), lambda i, j, kk: (kk, j))
    dims = _DOT_DIMS[mode]

    out_spec = pl.BlockSpec((tm, tn), lambda i, j, kk: (i, j))

    own_acc = narrow_out and nk > 1

    def body(*refs):
        a_ref, b_ref = refs[:2]
        h_ref = refs[2] if pre_act is not None else None
        o_ref = refs[3] if pre_act is not None else refs[2]
        act_ref = refs[o_pos + 1] if square_relu else None
        acc_ref = refs[-1] if own_acc else o_ref
        part = lax.dot_general(a_ref[...].astype(BF16), b_ref[...].astype(BF16), dims, preferred_element_type=F32)

        def finish(total):
            if h_ref is not None:
                total = total * (2.0 * jnp.maximum(h_ref[...], 0.0))
            o_ref[...] = total.astype(o_ref.dtype)
            if act_ref is not None:
                act_ref[...] = jnp.square(jnp.maximum(total, 0.0)).astype(BF16)

        if nk == 1:
            finish(part)
        else:
            kk = pl.program_id(2)

            @pl.when(kk == 0)
            def _():
                acc_ref[...] = part

            @pl.when(jnp.logical_and(kk > 0, kk < nk - 1))
            def _():
                acc_ref[...] += part

            @pl.when(kk == nk - 1)
            def _():
                finish(acc_ref[...] + part)

    operands = [a, b] + ([pre_act] if pre_act is not None else [])
    in_specs = [a_spec, b_spec] + ([out_spec] if pre_act is not None else [])
    o_pos = len(operands)
    out_shape = [jax.ShapeDtypeStruct((m, n), BF16 if narrow_out else F32)]
    out_shape += [jax.ShapeDtypeStruct((m, n), BF16)] if square_relu else []
    outs = pl.pallas_call(
        body, name=name, grid=(m // tm, n // tn, nk),
        in_specs=in_specs, out_specs=[out_spec] * len(out_shape), out_shape=out_shape,
        scratch_shapes=[pltpu.VMEM((tm, tn), F32)] if own_acc else [],
        compiler_params=_cparams(("parallel", "parallel", "arbitrary"), VMEM_LIMIT),
    )(*operands)
    return tuple(outs) if square_relu else outs[0]


def _linear(x, w, slot, name):
    @jax.custom_vjp
    def op(x, w, slot):
        return _mm(x, w, "nn", name + "_fwd")

    def fwd(x, w, slot):
        return _mm(x, w, "nn", name + "_fwd"), (x, w)

    def bwd(res, dy):
        x, w = res
        return _mm(dy, w, "nt", name + "_dx"), jnp.zeros_like(w), _mm(x, dy, "tn", name + "_dw", narrow_out=True)

    op.defvjp(fwd, bwd)
    return op(x, w, slot)


def _const_map(ndim):
    return lambda *g: (0,) * ndim


def _first_step(n_grid):
    return functools.reduce(jnp.logical_and, [pl.program_id(d) == 0 for d in range(n_grid)])


def _fused(fn, ins, outs, grid, name):
    arrays = [i[0] for i in ins]
    n_in, n_out, n_grid = len(ins), len(outs), len(grid)
    in_specs = [pl.BlockSpec(bs, im) for (_, bs, im, _) in ins]
    out_specs = [pl.BlockSpec(bs, im) for (_, bs, im, _) in outs]
    out_shapes = [jax.ShapeDtypeStruct(s, F32) for (s, _, _, _) in outs]
    sem = ("arbitrary",) * n_grid

    def fwd_call(*xs):
        def body(*refs):
            vals = [r[...] for r in refs[:n_in]]
            ys = fn(*vals)
            first = _first_step(n_grid)
            for o_ref, y, (_, _, _, kind) in zip(refs[n_in:], ys, outs):
                if kind == "t":
                    o_ref[...] = y
                else:
                    @pl.when(first)
                    def _(o_ref=o_ref):
                        o_ref[...] = jnp.zeros_like(o_ref)

                    o_ref[...] += y

        return pl.pallas_call(
            body, name=name + "_fwd", grid=grid, in_specs=in_specs, out_specs=out_specs,
            out_shape=out_shapes, compiler_params=_cparams(sem, VMEM_LIMIT))(*xs)

    def bwd_call(xs, dys):
        d_specs = [pl.BlockSpec(bs, im) for (_, bs, im, _) in outs]
        g_specs = [pl.BlockSpec(bs, im) for (_, bs, im, _) in ins]
        g_shapes = [jax.ShapeDtypeStruct(a.shape, F32) for a in arrays]

        def body(*refs):
            vals = [r[...] for r in refs[:n_in]]
            dvals = tuple(r[...] for r in refs[n_in:n_in + n_out])
            _, vjp = jax.vjp(lambda *v: tuple(fn(*v)), *vals)
            gs = vjp(dvals)
            first = _first_step(n_grid)
            for g_ref, g, (_, _, _, kind) in zip(refs[n_in + n_out:], gs, ins):
                if kind == "t":
                    g_ref[...] = g
                else:
                    @pl.when(first)
                    def _(g_ref=g_ref):
                        g_ref[...] = jnp.zeros_like(g_ref)

                    g_ref[...] += g

        return pl.pallas_call(
            body, name=name + "_bwd", grid=grid, in_specs=in_specs + d_specs, out_specs=g_specs,
            out_shape=g_shapes, compiler_params=_cparams(sem, VMEM_LIMIT))(*xs, *dys)

    @jax.custom_vjp
    def op(*xs):
        return tuple(fwd_call(*xs))

    def op_fwd(*xs):
        return tuple(fwd_call(*xs)), xs

    def op_bwd(xs, dys):
        return tuple(bwd_call(xs, dys))

    op.defvjp(op_fwd, op_bwd)
    return op(*arrays)


def _tiled(a, tile, cols=None, col_block=0):
    cols = a.shape[1] if cols is None else cols
    return (a, (tile, cols), lambda i, cb=col_block: (i, cb), "t")


def _shared(a):
    return (a, a.shape, _const_map(a.ndim), "b")


def _tiled_out(rows, cols, tile):
    return ((rows, cols), (tile, cols), lambda i: (i, 0), "t")


@jax.custom_vjp
def _bdot(x, w):
    return jnp.dot(x.astype(BF16), w.astype(BF16), preferred_element_type=F32)


def _bdot_fwd(x, w):
    return _bdot(x, w), (x, w)


def _bdot_bwd(res, dy):
    x, w = res
    dyb = dy.astype(BF16)
    dx = lax.dot_general(dyb, w.astype(BF16), _DOT_DIMS["nt"], preferred_element_type=F32)
    dw = lax.dot_general(x.astype(BF16), dyb, _DOT_DIMS["tn"], preferred_element_type=F32)
    return dx, dw


_bdot.defvjp(_bdot_fwd, _bdot_bwd)


def _head_sum(x):
    n = x.shape[-1]
    hi = lax.broadcasted_iota(jnp.int32, (n, n), 0) // HEAD_DIM
    hj = lax.broadcasted_iota(jnp.int32, (n, n), 1) // HEAD_DIM
    e = (hi == hj).astype(F32)
    return jnp.dot(x, e, precision=lax.Precision.HIGHEST, preferred_element_type=F32)


def _softplus(x):
    return jnp.maximum(x, 0.0) + jnp.log1p(jnp.exp(-jnp.abs(x)))


def _layer_norm_rows(z, g, b):
    mu = jnp.mean(z, axis=-1, keepdims=True)
    zc = z - mu
    var = jnp.mean(zc * zc, axis=-1, keepdims=True)
    return zc * lax.rsqrt(var + LN_EPS) * g + b


N_PAIRS = N_HEADS // 2


def _scan_consts():
    k = lax.broadcasted_iota(jnp.int32, (HEAD_DIM, LANES), 0)
    j = lax.broadcasted_iota(jnp.int32, (HEAD_DIM, LANES), 1)
    diag = ((j % HEAD_DIM) == k).astype(F32)
    jj = lax.broadcasted_iota(jnp.int32, (LANES, LANES), 0) // HEAD_DIM
    ll = lax.broadcasted_iota(jnp.int32, (LANES, LANES), 1) // HEAD_DIM
    same_head = (jj == ll).astype(BF16)
    return diag, same_head


def _unrolled(n, body, carry):
    for i in range(n):
        carry = body(i, carry)
    return carry


def _fill_cols(srcs, col_ref, n_steps):
    group = 16
    assert n_steps % group == 0
    j = lax.broadcasted_iota(jnp.int32, (LANES, LANES), 0)
    lane_head = lax.broadcasted_iota(jnp.int32, (LANES, LANES), 1) // HEAD_DIM
    for t0 in range(0, n_steps, group):
        blocks = []
        for src in srcs:
            x = src[t0:t0 + group, :]
            hi = x.astype(BF16).astype(F32)
            r1 = x - hi
            mid = r1.astype(BF16).astype(F32)
            x48 = jnp.concatenate([hi, mid, r1 - mid], axis=0)
            for hp in range(N_PAIRS):
                xp = x48[:, hp * LANES:(hp + 1) * LANES]
                y = jnp.concatenate([xp, pltpu.roll(xp, HEAD_DIM, 1), jnp.zeros((32, LANES), F32)], axis=0)
                blocks.append(y.T[:HEAD_DIM].astype(BF16))
        lhs = jnp.concatenate(blocks, axis=0)
        for t in range(group):
            pick = jnp.logical_and(j < 96, jnp.logical_and(j % 16 == t, j // 48 == lane_head))
            out = jnp.dot(lhs, pick.astype(BF16), preferred_element_type=F32)
            for vi in range(len(srcs)):
                for hp in range(N_PAIRS):
                    r0 = (vi * N_PAIRS + hp) * HEAD_DIM
                    col_ref[vi, hp, t0 + t] = out[r0:r0 + HEAD_DIM]


def _scan_fwd_call(r, w, k, v, a, b, wsrc):
    t_len = r.shape[0]
    ch = SCAN_CHUNK
    n_ch = t_len // ch
    w_rows, w_cols = wsrc.shape
    half = w_rows // 2
    chunks = _n_chunks(half)
    rc = half // chunks
    transfers = [(q, kk) for q in range(chunks) for kk in range(len(CHIP_FLIPS))]
    hand_on_step = [max(1, min(n_ch - 1, (i + 1) * (n_ch - 8) // len(transfers) + 3)) for i in range(len(transfers))]

    def body(r_ref, w_ref, k_ref, v_ref, a_ref, b_ref, src_ref, y_ref, ck_ref, g_ref,
             st_ref, col_ref, send1, recv1, send2, recv2):
        c = pl.program_id(0)
        me = _me()
        core = me[2]
        sibling = _flip(me, CORE_FLIP)

        def over_ici(q, kk):
            return pltpu.make_async_remote_copy(
                src_ref=src_ref.at[core, pl.ds(q * rc, rc)], dst_ref=g_ref.at[_chip_of(me), core, pl.ds(q * rc, rc)],
                send_sem=send1.at[kk * chunks + q], recv_sem=recv1.at[kk * chunks + q],
                device_id=_flip(me, CHIP_FLIPS[kk]), device_id_type=MESH)

        def hand_on(q, kk):
            landed = g_ref.at[_chip_of(_flip(me, CHIP_FLIPS[kk])), core, pl.ds(q * rc, rc)]
            return pltpu.make_async_remote_copy(
                src_ref=landed, dst_ref=landed, send_sem=send2.at[kk * chunks + q], recv_sem=recv2.at[kk * chunks + q],
                device_id=sibling, device_id_type=MESH)

        @pl.when(c == 0)
        def _():
            st_ref[...] = jnp.zeros_like(st_ref)
            for q, kk in transfers:
                over_ici(q, kk).start()

        for step in sorted(set(hand_on_step)):
            @pl.when(c == step)
            def _(step=step):
                for (q, kk), s in zip(transfers, hand_on_step):
                    if s == step:
                        over_ici(q, kk).wait_recv()
                        hand_on(q, kk).start()

        ck_ref[0] = st_ref[...]
        _fill_cols((w_ref, a_ref, b_ref, k_ref, r_ref), col_ref, ch)

        def step(t, states):
            new = []
            for hp in range(N_PAIRS):
                lanes = slice(hp * LANES, (hp + 1) * LANES)
                s = states[hp]
                sa = jnp.sum(s * col_ref[1, hp, t], axis=0, keepdims=True)
                s = s * col_ref[0, hp, t] + col_ref[2, hp, t] * sa + col_ref[3, hp, t] * v_ref[pl.ds(t, 1), lanes]
                y_ref[pl.ds(t, 1), lanes] = jnp.sum(s * col_ref[4, hp, t], axis=0, keepdims=True)
                new.append(s)
            return tuple(new)

        states = _unrolled(ch, step, tuple(st_ref[hp] for hp in range(N_PAIRS)))
        for hp in range(N_PAIRS):
            st_ref[hp] = states[hp]

        @pl.when(c == n_ch - 1)
        def _():
            for q, kk in transfers:
                hand_on(q, kk).wait_recv()
            for q, kk in transfers:
                over_ici(q, kk).wait_send()
                hand_on(q, kk).wait_send()

    row_spec = pl.BlockSpec((ch, RW_WIDTH), lambda c: (c, 0))
    any_spec = pl.BlockSpec(memory_space=pl.ANY)
    n_sem = len(transfers)
    y, ck, got = pl.pallas_call(
        body, name="rwkv_scan_fwd", grid=(n_ch,),
        in_specs=[row_spec] * 6 + [any_spec],
        out_specs=[row_spec, pl.BlockSpec((1, N_PAIRS, HEAD_DIM, LANES), lambda c: (c, 0, 0, 0)), any_spec],
        out_shape=[jax.ShapeDtypeStruct((t_len, RW_WIDTH), F32),
                   jax.ShapeDtypeStruct((n_ch, N_PAIRS, HEAD_DIM, LANES), F32),
                   jax.ShapeDtypeStruct((N_CHIPS, 2, half, w_cols), wsrc.dtype)],
        scratch_shapes=[pltpu.VMEM((N_PAIRS, HEAD_DIM, LANES), F32),
                        pltpu.VMEM((5, N_PAIRS, ch, HEAD_DIM, LANES), F32)]
        + [pltpu.SemaphoreType.DMA((n_sem,)) for _ in range(4)],
        compiler_params=_cparams(("arbitrary",), VMEM_LIMIT),
    )(r, w, k, v, a, b, wsrc.reshape(2, half, w_cols))
    return y, ck, got.reshape(N_CHIPS, w_rows, w_cols)


def _scan_bwd_call(r, w, k, v, a, b, ck, dy, gsend):
    t_len = r.shape[0]
    ch = SCAN_CHUNK
    n_ch = t_len // ch
    _, g_rows, g_cols = gsend.shape
    chunks = _n_chunks(g_rows)
    rc = g_rows // chunks
    transfers = [(q, kk) for q in range(chunks) for kk in range(len(CHIP_FLIPS))]

    def body(r_ref, w_ref, k_ref, v_ref, a_ref, b_ref, ck_ref, dy_ref, gs_ref,
             dr_ref, dw_ref, dk_ref, dv_ref, da_ref, db_ref, gr_ref,
             ds_ref, col_ref, sp_ref, sa_ref, send_sems, recv_sems):
        c = pl.program_id(0)
        me = _me()

        def to_chip(q, kk):
            peer = _flip(me, CHIP_FLIPS[kk])
            return pltpu.make_async_remote_copy(
                src_ref=gs_ref.at[_chip_of(peer), pl.ds(q * rc, rc)], dst_ref=gr_ref.at[_chip_of(me), pl.ds(q * rc, rc)],
                send_sem=send_sems.at[kk * chunks + q], recv_sem=recv_sems.at[kk * chunks + q],
                device_id=peer, device_id_type=MESH)

        @pl.when(c == 0)
        def _():
            ds_ref[...] = jnp.zeros_like(ds_ref)
            for q, kk in transfers:
                to_chip(q, kk).start()

        diag_f, same_head = _scan_consts()
        _fill_cols((w_ref, a_ref, b_ref, k_ref, r_ref), col_ref, ch)

        def replay(t, states):
            new = []
            for hp in range(N_PAIRS):
                lanes = slice(hp * LANES, (hp + 1) * LANES)
                s = states[hp]
                sp_ref[t, hp] = s
                sa = jnp.sum(s * col_ref[1, hp, t], axis=0, keepdims=True)
                sa_ref[pl.ds(t, 1), lanes] = sa
                new.append(s * col_ref[0, hp, t] + col_ref[2, hp, t] * sa
                           + col_ref[3, hp, t] * v_ref[pl.ds(t, 1), lanes])
            return tuple(new)

        last = _unrolled(ch, replay, tuple(ck_ref[0, hp] for hp in range(N_PAIRS)))
        for hp in range(N_PAIRS):
            sp_ref[ch, hp] = last[hp]

        def key_rows(ps):
            stacked = jnp.concatenate([p.astype(BF16) for p in ps], axis=0)
            q = jnp.dot(stacked, same_head, preferred_element_type=F32)
            return [jnp.sum(q[i * HEAD_DIM:(i + 1) * HEAD_DIM] * diag_f, axis=0, keepdims=True)
                    for i in range(len(ps))]

        def back(i, grads):
            t = ch - 1 - i
            new = []
            for hp in range(N_PAIRS):
                lanes = slice(hp * LANES, (hp + 1) * LANES)
                wc, ac, bc, kc, rc = (col_ref[vi, hp, t] for vi in range(5))
                sp = sp_ref[t, hp]
                sa = sa_ref[pl.ds(t, 1), lanes]
                vrow = v_ref[pl.ds(t, 1), lanes]
                dyrow = dy_ref[pl.ds(t, 1), lanes]
                st = sp_ref[t + 1, hp]
                g = grads[hp] + rc * dyrow
                dsa = jnp.sum(g * bc, axis=0, keepdims=True)
                dv_ref[pl.ds(t, 1), lanes] = jnp.sum(g * kc, axis=0, keepdims=True)
                rows = key_rows([st * dyrow, g * vrow, g * sa, g * sp, sp * dsa])
                for out_ref, row in zip((dr_ref, dk_ref, db_ref, dw_ref, da_ref), rows):
                    out_ref[pl.ds(t, 1), lanes] = row
                new.append(g * wc + ac * dsa)
            return tuple(new)

        grads = _unrolled(ch, back, tuple(ds_ref[hp] for hp in range(N_PAIRS)))
        for hp in range(N_PAIRS):
            ds_ref[hp] = grads[hp]

        @pl.when(c == n_ch - 1)
        def _():
            for q, kk in transfers:
                to_chip(q, kk).wait()

    row_spec = pl.BlockSpec((ch, RW_WIDTH), lambda c: (n_ch - 1 - c, 0))
    any_spec = pl.BlockSpec(memory_space=pl.ANY)
    out_sds = jax.ShapeDtypeStruct((t_len, RW_WIDTH), F32)
    n_sem = len(transfers)
    return pl.pallas_call(
        body, name="rwkv_scan_bwd", grid=(n_ch,),
        in_specs=[row_spec] * 6 + [pl.BlockSpec((1, N_PAIRS, HEAD_DIM, LANES), lambda c: (n_ch - 1 - c, 0, 0, 0)),
                                   row_spec, any_spec],
        out_specs=[row_spec] * 6 + [any_spec],
        out_shape=[out_sds] * 6 + [jax.ShapeDtypeStruct(gsend.shape, gsend.dtype)],
        scratch_shapes=[pltpu.VMEM((N_PAIRS, HEAD_DIM, LANES), F32),
                        pltpu.VMEM((5, N_PAIRS, ch, HEAD_DIM, LANES), F32),
                        pltpu.VMEM((ch + 1, N_PAIRS, HEAD_DIM, LANES), F32),
                        pltpu.VMEM((ch, RW_WIDTH), F32),
                        pltpu.SemaphoreType.DMA((n_sem,)), pltpu.SemaphoreType.DMA((n_sem,))],
        compiler_params=_cparams(("arbitrary",), VMEM_LIMIT),
    )(r, w, k, v, a, b, ck, dy, gsend)


def _scan_token(wsrc):
    return jnp.zeros((N_CHIPS,) + wsrc.shape, BF16)


@jax.custom_vjp
def _rwkv_scan(r, w, k, v, a, b, wsrc, gslot):
    y, _, got = _scan_fwd_call(r, w, k, v, a, b, wsrc)
    return y, got, _scan_token(wsrc)


def _rwkv_scan_fwd(r, w, k, v, a, b, wsrc, gslot):
    y, ck, got = _scan_fwd_call(r, w, k, v, a, b, wsrc)
    return (y, got, _scan_token(wsrc)), (r, w, k, v, a, b, ck, wsrc)


def _rwkv_scan_bwd(res, cts):
    *saved, wsrc = res
    dy, _, gsend = cts
    *dins, got = _scan_bwd_call(*saved, dy, gsend)
    summed = _sum_slots_own(got, gsend, _chip_of(_me()), "sum_chips_late")
    return (*dins, jnp.zeros_like(wsrc), summed)


_rwkv_scan.defvjp(_rwkv_scan_fwd, _rwkv_scan_bwd)


ATT_SCALE = HEAD_DIM ** -0.5


def _att_masks():
    qi = lax.broadcasted_iota(jnp.int32, (BLOCK, BLOCK), 0)
    ki = lax.broadcasted_iota(jnp.int32, (BLOCK, BLOCK), 1)
    lane = lax.broadcasted_iota(jnp.int32, (1, LANES), 1)
    return ki <= qi, ki >= qi, lane


def _att_fwd_call(p, bias, g, dil):
    t_len = p.shape[0]
    l_len = t_len // dil
    nb = l_len // BLOCK

    def body(q_ref, kc_ref, kp_ref, vc_ref, vp_ref, bias_ref, o_ref, lse_ref):
        n = pl.program_id(1)
        cur_ok, prev_band, lane = _att_masks()
        prev_ok = jnp.logical_and(prev_band, n > 0)
        for hp in range(N_PAIRS):
            lanes = slice(hp * LANES, (hp + 1) * LANES)
            q2 = q_ref[:, lanes].astype(BF16)
            kc = kc_ref[:, lanes].astype(BF16)
            kp = kp_ref[:, lanes].astype(BF16)
            vc = vc_ref[:, lanes].astype(BF16)
            vp = vp_ref[:, lanes].astype(BF16)
            o2 = jnp.zeros((BLOCK, LANES), F32)
            for hh in range(2):
                h = 2 * hp + hh
                mine = (lane // HEAD_DIM) == hh
                qm = jnp.where(mine, q2, jnp.zeros_like(q2))
                s_c = lax.dot_general(qm, kc, _DOT_DIMS["nt"], preferred_element_type=F32) * ATT_SCALE
                s_p = lax.dot_general(qm, kp, _DOT_DIMS["nt"], preferred_element_type=F32) * ATT_SCALE
                s_c = jnp.where(cur_ok, s_c + bias_ref[h, :, BLOCK:], NEG_BIG)
                s_p = jnp.where(prev_ok, s_p + bias_ref[h, :, :BLOCK], NEG_BIG)
                m = jnp.maximum(jnp.max(s_c, axis=-1, keepdims=True), jnp.max(s_p, axis=-1, keepdims=True))
                e_c = jnp.exp(s_c - m)
                e_p = jnp.exp(s_p - m)
                den = jnp.sum(e_c, axis=-1, keepdims=True) + jnp.sum(e_p, axis=-1, keepdims=True)
                o_h = (jnp.dot((e_c / den).astype(BF16), vc, preferred_element_type=F32)
                       + jnp.dot((e_p / den).astype(BF16), vp, preferred_element_type=F32))
                o2 = o2 + jnp.where(mine, o_h, 0.0)
                lse_ref[h] = jnp.broadcast_to(m + jnp.log(den), (BLOCK, LANES))
            o_ref[:, lanes] = o2

    def col(j):
        return lambda r, n: (r * nb + n, j)

    def col_prev(j):
        return lambda r, n: (r * nb + jnp.maximum(n - 1, 0), j)

    blk = (BLOCK, DIL_WIDTH)
    return pl.pallas_call(
        body, name=f"dil_att_fwd_g{g}", grid=(dil, nb),
        in_specs=[pl.BlockSpec(blk, col(0)), pl.BlockSpec(blk, col(1)), pl.BlockSpec(blk, col_prev(1)),
                  pl.BlockSpec(blk, col(2)), pl.BlockSpec(blk, col_prev(2)),
                  pl.BlockSpec(bias.shape, _const_map(3))],
        out_specs=[pl.BlockSpec(blk, lambda r, n: (r * nb + n, 0)),
                   pl.BlockSpec((N_HEADS, BLOCK, LANES), lambda r, n: (0, r * nb + n, 0))],
        out_shape=[jax.ShapeDtypeStruct((t_len, DIL_WIDTH), F32),
                   jax.ShapeDtypeStruct((N_HEADS, t_len, LANES), F32)],
        compiler_params=_cparams(("arbitrary", "arbitrary"), VMEM_LIMIT),
    )(p, p, p, p, p, bias)


def _att_bwd_call(p, bias, o, lse, do, dlse, g, dil):
    t_len = p.shape[0]
    l_len = t_len // dil
    nb = l_len // BLOCK

    def body(q_ref, qn_ref, k_ref, v_ref, do_ref, don_ref, o_ref, on_ref, lse_ref, lsen_ref, dl_ref, dln_ref,
             bias_ref, dq_ref, dk_ref, dv_ref, dbias_ref, carry_ref):
        r = pl.program_id(0)
        n = pl.program_id(1)
        cur_ok, prev_band, lane = _att_masks()
        has_next = n + 1 < nb

        @pl.when(jnp.logical_and(r == 0, n == 0))
        def _():
            dbias_ref[...] = jnp.zeros_like(dbias_ref)

        @pl.when(n == 0)
        def _():
            carry_ref[...] = jnp.zeros_like(carry_ref)

        for hp in range(N_PAIRS):
            lanes = slice(hp * LANES, (hp + 1) * LANES)
            k2 = k_ref[:, lanes].astype(BF16)
            v2 = v_ref[:, lanes].astype(BF16)
            dk2 = jnp.zeros((BLOCK, LANES), F32)
            dv2 = jnp.zeros((BLOCK, LANES), F32)
            dq_cur = carry_ref[:, lanes]
            dq_next = jnp.zeros((BLOCK, LANES), F32)
            for hh in range(2):
                h = 2 * hp + hh
                mine = (lane // HEAD_DIM) == hh
                tiles = (
                    (q_ref, do_ref, o_ref, lse_ref, dl_ref, cur_ok, slice(BLOCK, 2 * BLOCK), None),
                    (qn_ref, don_ref, on_ref, lsen_ref, dln_ref, prev_band, slice(0, BLOCK), has_next),
                )
                if nb == 1:
                    tiles = tiles[:1]
                for ti, (qr, dor, orf, lr, dlr, ok, bcols, gate) in enumerate(tiles):
                    q2 = qr[:, lanes].astype(BF16)
                    qm = jnp.where(mine, q2, jnp.zeros_like(q2))
                    do_f = jnp.where(mine, dor[:, lanes], 0.0)
                    dom = do_f.astype(BF16)
                    s = lax.dot_general(qm, k2, _DOT_DIMS["nt"], preferred_element_type=F32) * ATT_SCALE
                    s = s + bias_ref[h, :, bcols]
                    if gate is not None:
                        ok = jnp.logical_and(ok, gate)
                    pr = jnp.where(ok, jnp.exp(jnp.minimum(s - lr[h], 0.0)), 0.0)
                    dp = lax.dot_general(dom, v2, _DOT_DIMS["nt"], preferred_element_type=F32)
                    delta = jnp.sum(do_f * orf[:, lanes], axis=-1, keepdims=True)
                    dl = jnp.sum(dlr[h], axis=-1, keepdims=True)
                    ds = pr * (dp - delta + dl)
                    dsb = ds.astype(BF16)
                    dq_h = jnp.where(mine, jnp.dot(dsb, k2, preferred_element_type=F32), 0.0) * ATT_SCALE
                    if ti == 0:
                        dq_cur = dq_cur + dq_h
                    else:
                        dq_next = dq_next + dq_h
                    dk2 = dk2 + lax.dot_general(dsb, qm, _DOT_DIMS["tn"], preferred_element_type=F32) * ATT_SCALE
                    dv2 = dv2 + lax.dot_general(pr.astype(BF16), dom, _DOT_DIMS["tn"], preferred_element_type=F32)
                    dbias_ref[h, :, bcols] += ds
            dq_ref[:, lanes] = dq_cur
            carry_ref[:, lanes] = dq_next
            dk_ref[:, lanes] = dk2
            dv_ref[:, lanes] = dv2

    def nxt(n):
        return jnp.minimum(n + 1, nb - 1)

    blk = (BLOCK, DIL_WIDTH)
    hblk = (N_HEADS, BLOCK, LANES)
    qcol = lambda j: (lambda r, n: (r * nb + n, j))
    q_next = lambda r, n: (r * nb + nxt(n), 0)
    rown = lambda r, n: (r * nb + n, 0)
    rown_next = lambda r, n: (r * nb + nxt(n), 0)
    hrow = lambda r, n: (0, r * nb + n, 0)
    hrow_next = lambda r, n: (0, r * nb + nxt(n), 0)
    sds = jax.ShapeDtypeStruct((t_len, DIL_WIDTH), F32)
    return pl.pallas_call(
        body, name=f"dil_att_bwd_g{g}", grid=(dil, nb),
        in_specs=[pl.BlockSpec(blk, qcol(0)), pl.BlockSpec(blk, q_next),
                  pl.BlockSpec(blk, qcol(1)), pl.BlockSpec(blk, qcol(2)),
                  pl.BlockSpec(blk, rown), pl.BlockSpec(blk, rown_next),
                  pl.BlockSpec(blk, rown), pl.BlockSpec(blk, rown_next),
                  pl.BlockSpec(hblk, hrow), pl.BlockSpec(hblk, hrow_next),
                  pl.BlockSpec(hblk, hrow), pl.BlockSpec(hblk, hrow_next),
                  pl.BlockSpec(bias.shape, _const_map(3))],
        out_specs=[pl.BlockSpec(blk, rown)] * 3 + [pl.BlockSpec(bias.shape, _const_map(3))],
        out_shape=[sds, sds, sds, jax.ShapeDtypeStruct(bias.shape, F32)],
        scratch_shapes=[pltpu.VMEM((BLOCK, DIL_WIDTH), F32)],
        compiler_params=_cparams(("arbitrary", "arbitrary"), VMEM_LIMIT),
    )(p, p, p, p, do, do, o, o, lse, lse, dlse, dlse, bias)


def _att_all_groups(ps, biases):
    outs = [_att_fwd_call(ps[g], biases[g], g, dil) for g, (_, dil) in enumerate(DIL_PATTERNS)]
    return tuple(o for o, _ in outs), tuple(l for _, l in outs)


@jax.custom_vjp
def _dilated_attention(ps, biases):
    return _att_all_groups(ps, biases)


def _dilated_attention_fwd(ps, biases):
    os_, lses = _att_all_groups(ps, biases)
    return (os_, lses), (ps, biases, os_, lses)


def _dilated_attention_bwd(res, cts):
    ps, biases, os_, lses = res
    dos, dlses = cts
    dps, dbiases = [], []
    for g, (_, dil) in enumerate(DIL_PATTERNS):
        dq, dk, dv, dbias = _att_bwd_call(ps[g], biases[g], os_[g], lses[g], dos[g], dlses[g], g, dil)
        dps.append(jnp.concatenate([dq, dk, dv], axis=1))
        dbiases.append(dbias)
    return tuple(dps), tuple(dbiases)


_dilated_attention.defvjp(_dilated_attention_fwd, _dilated_attention_bwd)


def _me():
    return lax.axis_index("x"), lax.axis_index("y"), lax.axis_index("c")


def _flip(me, f):
    return tuple((1 - m) if b else m for m, b in zip(me, f))


def _chip_of(d):
    return 2 * d[0] + d[1]


def _dev_of(d):
    return 4 * d[0] + 2 * d[1] + d[2]


EXCHANGE_CHUNKS = 8
CHIP_FLIPS = ((1, 0, 0), (0, 1, 0), (1, 1, 0))
ALL_FLIPS = tuple((a, b, c) for a in (0, 1) for b in (0, 1) for c in (0, 1) if a or b or c)
CORE_FLIP = (0, 0, 1)


def _n_chunks(rows):
    return EXCHANGE_CHUNKS if rows % (EXCHANGE_CHUNKS * PACK_ROWS) == 0 else 1


def _exchange(src, n_slots, transfers, name):
    _, rows, cols = src.shape
    chunks = _n_chunks(rows)
    rc = rows // chunks
    n = len(transfers) * chunks

    def body(src_ref, dst_ref, send_sems, recv_sems):
        me = _me()
        copies = []
        for q in range(chunks):
            for kk, (f, src_slot, dst_slot) in enumerate(transfers):
                peer = _flip(me, f)
                cp = pltpu.make_async_remote_copy(
                    src_ref=src_ref.at[src_slot(me, peer), pl.ds(q * rc, rc)],
                    dst_ref=dst_ref.at[dst_slot(me, peer), pl.ds(q * rc, rc)],
                    send_sem=send_sems.at[kk * chunks + q], recv_sem=recv_sems.at[kk * chunks + q],
                    device_id=peer, device_id_type=MESH)
                cp.start()
                copies.append(cp)
        for cp in copies:
            cp.wait()

    return pl.pallas_call(
        body, name=name,
        out_shape=jax.ShapeDtypeStruct((n_slots, rows, cols), src.dtype),
        in_specs=[pl.BlockSpec(memory_space=pl.ANY)],
        out_specs=pl.BlockSpec(memory_space=pl.ANY),
        scratch_shapes=[pltpu.SemaphoreType.DMA((n,)), pltpu.SemaphoreType.DMA((n,))],
    )(src)


def _set_slot(buf, block, index):
    return lax.dynamic_update_slice_in_dim(buf, block[None].astype(buf.dtype), index, axis=0)


def _chip_all_gather(src, name):
    got = _exchange(src[None], N_CHIPS, [(f, lambda me, peer: 0, lambda me, peer: _chip_of(me)) for f in CHIP_FLIPS], name)
    return _set_slot(got, src, _chip_of(_me()))


def _dev_all_gather(src, name):
    got = _exchange(src[None], N_DEV, [(f, lambda me, peer: 0, lambda me, peer: _dev_of(me)) for f in ALL_FLIPS], name)
    return _set_slot(got, src, _dev_of(_me()))


def _chip_scatter(src, name):
    got = _exchange(src, N_CHIPS, [(f, lambda me, peer: _chip_of(peer), lambda me, peer: _chip_of(me))
                                   for f in CHIP_FLIPS], name)
    chip = _chip_of(_me())
    return _set_slot(got, lax.dynamic_index_in_dim(src, chip, 0, keepdims=False), chip)


def _core_halves(src, name):
    s, _, half, cols = src.shape
    transfers = [(CORE_FLIP, (lambda me, peer, j=j: 2 * j + peer[2]), (lambda me, peer, j=j: j)) for j in range(s)]
    return _exchange(src.reshape(2 * s, half, cols), s, transfers, name)


def _core_all_gather(src, name):
    got = _exchange(src[None], 2, [(CORE_FLIP, lambda me, peer: 0, lambda me, peer: me[2])], name)
    return _set_slot(got, src, _me()[2])


def _two_level_gather(src, name):
    rows, cols = src.shape
    half = rows // 2
    chunks = _n_chunks(half)
    rc = half // chunks
    n = len(CHIP_FLIPS) * chunks

    def body(src_ref, g_ref, send1, recv1, send2, recv2):
        me = _me()
        c = me[2]
        sibling = _flip(me, CORE_FLIP)
        first, second = [], []
        for q in range(chunks):
            for kk, f in enumerate(CHIP_FLIPS):
                peer = _flip(me, f)
                cp = pltpu.make_async_remote_copy(
                    src_ref=src_ref.at[c, pl.ds(q * rc, rc)], dst_ref=g_ref.at[_chip_of(me), c, pl.ds(q * rc, rc)],
                    send_sem=send1.at[kk * chunks + q], recv_sem=recv1.at[kk * chunks + q],
                    device_id=peer, device_id_type=MESH)
                cp.start()
                first.append((cp, _chip_of(peer), kk * chunks + q, q))
        for cp, origin, idx, q in first:
            cp.wait_recv()
            fw = pltpu.make_async_remote_copy(
                src_ref=g_ref.at[origin, c, pl.ds(q * rc, rc)], dst_ref=g_ref.at[origin, c, pl.ds(q * rc, rc)],
                send_sem=send2.at[idx], recv_sem=recv2.at[idx],
                device_id=sibling, device_id_type=MESH)
            fw.start()
            second.append(fw)
        for fw in second:
            fw.wait_recv()
        for cp, _, _, _ in first:
            cp.wait_send()
        for fw in second:
            fw.wait_send()

    got = pl.pallas_call(
        body, name=name,
        out_shape=jax.ShapeDtypeStruct((N_CHIPS, 2, half, cols), src.dtype),
        in_specs=[pl.BlockSpec(memory_space=pl.ANY)],
        out_specs=pl.BlockSpec(memory_space=pl.ANY),
        scratch_shapes=[pltpu.SemaphoreType.DMA((n,)) for _ in range(4)],
    )(src.reshape(2, half, cols))
    return _set_slot(got.reshape(N_CHIPS, rows, cols), src, _chip_of(_me()))


def _sum_slots(x, name):
    s, rows, cols = x.shape
    tile = _pick(rows, (512, 256, 128, 64, 32, 16, 8))

    def body(x_ref, o_ref):
        acc = x_ref[0].astype(F32)
        for i in range(1, s):
            acc = acc + x_ref[i].astype(F32)
        o_ref[...] = acc

    return pl.pallas_call(
        body, name=name, grid=(rows // tile,),
        in_specs=[pl.BlockSpec((s, tile, cols), lambda i: (0, i, 0))],
        out_specs=pl.BlockSpec((tile, cols), lambda i: (i, 0)),
        out_shape=jax.ShapeDtypeStruct((rows, cols), F32),
        compiler_params=_cparams(("parallel",), VMEM_LIMIT),
    )(x)


def _sum_slots_own(recv, send, chip, name):
    s, rows, cols = recv.shape
    tile = _pick(rows, (512, 256, 128, 64, 32, 16, 8))

    def body(chip_ref, recv_ref, own_ref, o_ref):
        acc = None
        for j in range(s):
            term = jnp.where(chip_ref[0] == j, own_ref[0], recv_ref[j]).astype(F32)
            acc = term if acc is None else acc + term
        o_ref[...] = acc

    grid_spec = pltpu.PrefetchScalarGridSpec(
        num_scalar_prefetch=1, grid=(rows // tile,),
        in_specs=[pl.BlockSpec((s, tile, cols), lambda i, c: (0, i, 0)),
                  pl.BlockSpec((1, tile, cols), lambda i, c: (c[0], i, 0))],
        out_specs=pl.BlockSpec((tile, cols), lambda i, c: (i, 0)))
    return pl.pallas_call(
        body, name=name, grid_spec=grid_spec,
        out_shape=jax.ShapeDtypeStruct((rows, cols), F32),
        compiler_params=_cparams(("arbitrary",), VMEM_LIMIT),
    )(jnp.reshape(chip, (1,)).astype(jnp.int32), recv, send)


def _add_pairs(a, b, name):
    s, rows, cols = a.shape
    tile = _pick(rows, (512, 256, 128, 64, 32, 16, 8))

    def body(a_ref, b_ref, o_ref):
        o_ref[...] = (a_ref[...].astype(F32) + b_ref[...].astype(F32)).astype(o_ref.dtype)

    spec = pl.BlockSpec((1, tile, cols), lambda j, i: (j, i, 0))
    return pl.pallas_call(
        body, name=name, grid=(s, rows // tile),
        in_specs=[spec, spec], out_specs=spec,
        out_shape=jax.ShapeDtypeStruct(a.shape, a.dtype),
        compiler_params=_cparams(("parallel", "parallel"), VMEM_LIMIT),
    )(a, b)


def _adamw(w, g, m, v, name):
    rows, cols = w.shape
    tile = rows
    if rows * cols * 4 > 2 * 1024 * 1024:
        tile = _pick(rows, (256, 128, 64, 32, 16, 8))
    c1 = 1.0 / (1.0 - ADAM_B1 ** ADAM_STEP)
    c2 = 1.0 / (1.0 - ADAM_B2 ** ADAM_STEP)

    def body(w_ref, g_ref, m_ref, v_ref, d_ref, nm_ref, nv_ref):
        gv = g_ref[...]
        nm = ADAM_B1 * m_ref[...] + (1.0 - ADAM_B1) * gv
        nv = ADAM_B2 * v_ref[...] + (1.0 - ADAM_B2) * (gv * gv)
        m_hat = nm * c1
        v_hat = nv * c2
        d_ref[...] = -ADAM_LR * (m_hat / (jnp.sqrt(v_hat) + ADAM_EPS) + ADAM_WD * w_ref[...])
        nm_ref[...] = nm
        nv_ref[...] = nv

    spec = pl.BlockSpec((tile, cols), lambda i: (i, 0))
    sds = jax.ShapeDtypeStruct((rows, cols), F32)
    return pl.pallas_call(
        body, name=name, grid=(rows // tile,),
        in_specs=[spec] * 4, out_specs=[spec] * 3, out_shape=[sds] * 3,
        compiler_params=_cparams(("parallel",), VMEM_LIMIT),
    )(w, g, m, v)


def _ada_fwd(c_all, ada_w, ada_b_cols):
    n_col = ada_w.shape[2]

    def body(c_ref, w_ref, b_ref, o_ref):
        cv = c_ref[...]
        cond = (cv * jax.nn.sigmoid(cv)).astype(BF16)
        o_ref[0] = jnp.dot(cond, w_ref[0].astype(BF16), preferred_element_type=F32) + b_ref[0]

    return pl.pallas_call(
        body, name="ada_fwd", grid=(DEPTH,),
        in_specs=[pl.BlockSpec(c_all.shape, lambda i: (0, 0)),
                  pl.BlockSpec((1, D_MODEL, n_col), lambda i: (i, 0, 0)),
                  pl.BlockSpec((1, 1, n_col), lambda i: (i, 0, 0))],
        out_specs=pl.BlockSpec((1, N_DEV, n_col), lambda i: (i, 0, 0)),
        out_shape=jax.ShapeDtypeStruct((DEPTH, N_DEV, n_col), F32),
        compiler_params=_cparams(("parallel",), VMEM_LIMIT),
    )(c_all, ada_w, ada_b_cols)


def _ada_grad(c_all_t, dmod_cols):
    n_col = dmod_cols.shape[2]

    def body(c_ref, d_ref, o_ref):
        cv = c_ref[...]
        cond = cv * jax.nn.sigmoid(cv)
        o_ref[0] = jnp.dot(cond, d_ref[0], precision=lax.Precision.HIGHEST, preferred_element_type=F32)

    return pl.pallas_call(
        body, name="ada_grad", grid=(DEPTH,),
        in_specs=[pl.BlockSpec(c_all_t.shape, lambda i: (0, 0)),
                  pl.BlockSpec((1, LANES, n_col), lambda i: (i, 0, 0))],
        out_specs=pl.BlockSpec((1, D_MODEL, n_col), lambda i: (i, 0, 0)),
        out_shape=jax.ShapeDtypeStruct((DEPTH, D_MODEL, n_col), F32),
        compiler_params=_cparams(("parallel",), VMEM_LIMIT),
    )(c_all_t, dmod_cols)


ROW_TILE = 256


def _shift_rows(a, n=1):
    return jnp.pad(a, ((n, 0), (0, 0)))[:-n]


def _modulate(x, sc, sh, name):
    def fn(x, sc, sh):
        return (x * (1.0 + sc) + sh,)

    t = x.shape[0]
    return _fused(fn, [_tiled(x, ROW_TILE), _shared(sc), _shared(sh)],
                  [_tiled_out(t, D_MODEL, ROW_TILE)], (t // ROW_TILE,), name)[0]


def _resid_ln_mod(x, y, gate, ln_g, ln_b, sc, sh, name):
    def fn(x, y, gate, ln_g, ln_b, sc, sh):
        x1 = _layer_norm_rows(ALPHA * x + (1.0 + gate) * y, ln_g, ln_b)
        return x1, x1 * (1.0 + sc) + sh

    t = x.shape[0]
    return _fused(fn, [_tiled(x, ROW_TILE), _tiled(y, ROW_TILE)] + [_shared(a) for a in (gate, ln_g, ln_b, sc, sh)],
                  [_tiled_out(t, D_MODEL, ROW_TILE)] * 2, (t // ROW_TILE,), name)


def _resid_ln_loss(x, y, gate, ln_g, ln_b, target, name):
    def fn(x, y, gate, ln_g, ln_b, target):
        x1 = _layer_norm_rows(ALPHA * x + (1.0 + gate) * y, ln_g, ln_b)
        err = jnp.square(x1 - target)
        per_row = jnp.mean(err, axis=-1, keepdims=True)
        return (0.5 * jnp.sum(per_row, axis=0, keepdims=True),)

    t = x.shape[0]
    return _fused(fn, [_tiled(x, ROW_TILE), _tiled(y, ROW_TILE)] + [_shared(a) for a in (gate, ln_g, ln_b)]
                  + [_tiled(target, ROW_TILE)],
                  [((1, 1), (1, 1), _const_map(2), "a")], (t // ROW_TILE,), name)[0]


def _mlp(u, w1, s1, w2, s2, name):
    def run(u, w1, w2):
        h, act = _mm(u, w1, "nn", name + "_w1_fwd", square_relu=True)
        return _mm(act, w2, "nn", name + "_w2_fwd"), (u, w1, w2, h, act)

    @jax.custom_vjp
    def op(u, w1, s1, w2, s2):
        return run(u, w1, w2)[0]

    def fwd(u, w1, s1, w2, s2):
        return run(u, w1, w2)

    def bwd(res, dy):
        u, w1, w2, h, act = res
        dh = _mm(dy, w2, "nt", name + "_w2_dx", pre_act=h)
        dw2 = _mm(act, dy, "tn", name + "_w2_dw", narrow_out=True)
        du = _mm(dh, w1, "nt", name + "_w1_dx")
        dw1 = _mm(u, dh, "tn", name + "_w1_dw", narrow_out=True)
        return du, jnp.zeros_like(w1), dw1, jnp.zeros_like(w2), dw2

    op.defvjp(fwd, bwd)
    return op(u, w1, s1, w2, s2)


AB_PIECES = (("r", 0, 512, 512), ("k", 512, 512, 512), ("v", 1024, 512, 512),
             ("wd", 1536, 64, 128), ("ad", 1600, 64, 128), ("gd", 1664, 160, 256),
             ("h", 1824, 512, 512), ("bg", 2336, 512, 512), ("cg", 2848, 512, 512))
AB_PAD_COLS = sum(p[3] for p in AB_PIECES)


def _regroup_cols(w):
    parts = []
    for _, start, width, padded in AB_PIECES:
        piece = w[..., start:start + width]
        if padded != width:
            piece = jnp.pad(piece, [(0, 0)] * (w.ndim - 1) + [(0, padded - width)])
        parts.append(piece)
    return jnp.concatenate(parts, axis=-1)


def _pad_rows(w, rows):
    return jnp.pad(w, ((0, rows - w.shape[0]), (0, 0)))


def _rwkv_shortconv(u, big, wts, wsrc, gslot, late_shapes):
    t = u.shape[0]
    p = _linear(u, _regroup_cols(big["ab_w_in"][0]), _regroup_cols(wts["ab_w_in"][0]), "ab_in")
    mu = _regroup_cols(jnp.pad(wts["rw_mu"], ((0, 0), (0, AB_PROJ - RW_PROJ))))
    w_up = _pad_rows(wts["rw_w_up"][0], 128)
    a_up = _pad_rows(wts["rw_a_up"][0], 128)
    g_up = _pad_rows(wts["rw_g_up"][0], 256)

    def pre(rp, rs, kp, ks, vp, vs, wdp, wds, adp, ads, gdp, gds, h, cg,
            mu_r, mu_k, mu_v, mu_w, mu_a, mu_g, w0, w_up, a0, a_up, g_up, k_k, k_a):
        def mix(pv, sv, m):
            return pv + m * (sv - pv)

        r, k, v = mix(rp, rs, mu_r), mix(kp, ks, mu_k), mix(vp, vs, mu_v)
        wd, ad, gd = mix(wdp, wds, mu_w), mix(adp, ads, mu_a), mix(gdp, gds, mu_g)
        logw = -_softplus(-(w0 + _bdot(jnp.tanh(wd), w_up))) - 0.5
        decay = jnp.exp(-jnp.exp(logw))
        iclr = jax.nn.sigmoid(a0 + _bdot(ad, a_up))
        gate = _bdot(jax.nn.sigmoid(gd), g_up)
        kk = k * k_k
        kk = kk / jnp.maximum(jnp.sqrt(_head_sum(kk * kk)), 1e-12)
        k_h = k * (1.0 + (iclr - 1.0) * k_a)
        return r, decay, k_h, v, -kk, kk * iclr, gate, cg * h

    tile = ROW_TILE
    names = [q[0] for q in AB_PIECES]
    cuts = list(np.cumsum([q[3] for q in AB_PIECES])[:-1])
    pp = dict(zip(names, jnp.split(p, cuts, axis=1)))
    mp = dict(zip(names, jnp.split(mu, cuts, axis=1)))

    ins = []
    for name in ("r", "k", "v", "wd", "ad", "gd"):
        ins += [_tiled(pp[name], tile), _tiled(_shift_rows(pp[name]), tile)]
    ins += [_tiled(pp["h"], tile), _tiled(pp["cg"], tile)]
    ins += [_shared(mp[name]) for name in ("r", "k", "v", "wd", "ad", "gd")]
    ins += [_shared(a) for a in (wts["rw_w0"], w_up, wts["rw_a0"], a_up, g_up, wts["rw_k_k"], wts["rw_k_a"])]
    outs = [_tiled_out(t, RW_WIDTH, tile)] * 8
    r, decay, k_h, v, a, b, gate, z = _fused(pre, ins, outs, (t // tile,), "rwkv_pre")

    y, gathered, token = _rwkv_scan(r, decay, k_h, v, a, b, wsrc, gslot)
    shard_shapes = [s[:SHARDED[n]] + (s[SHARDED[n]] // N_CHIPS,) + s[SHARDED[n] + 1:] for n, s in late_shapes.items()]
    late_big = {n: _join_chips(part, SHARDED[n])
                for n, part in zip(late_shapes, _unpack_chips(gathered, shard_shapes, own=wsrc))}
    late_slots = _gradient_slots(token, late_shapes, tuple(late_shapes))

    conv_w = wts["sc_conv_w"][0]
    r_k = wts["rw_r_k"].reshape(1, RW_WIDTH)

    def post(y, r, k_h, v, gate, bg, z, z1, z2, lnx_g, lnx_b, r_k, c0, c1, c2):
        mean = _head_sum(y) * (1.0 / HEAD_DIM)
        yc = y - mean
        var = _head_sum(yc * yc) * (1.0 / HEAD_DIM)
        yn = yc * lax.rsqrt(var + RW_GN_EPS) * lnx_g + lnx_b
        bonus = _head_sum(r * k_h * r_k) * v
        return (yn + bonus) * gate, bg * (c0 * z2 + c1 * z1 + c2 * z)

    ins = [_tiled(a_, tile) for a_ in (y, r, k_h, v, gate, pp["bg"])]
    ins += [_tiled(a_, tile) for a_ in (z, _shift_rows(z, 1), _shift_rows(z, 2))]
    ins += [_shared(a_) for a_ in (wts["rw_lnx_g"], wts["rw_lnx_b"], r_k, conv_w[0:1], conv_w[1:2], conv_w[2:3])]
    y_a, y_b = _fused(post, ins, [_tiled_out(t, RW_WIDTH, tile)] * 2, (t // tile,), "rwkv_post")
    out = _linear(jnp.concatenate([y_a, y_b], axis=1), late_big["ab_w_out"][0], late_slots["ab_w_out"][0], "ab_out")
    return out, late_big, late_slots


def _t5_bucket_np(dist):
    exact = N_BUCKETS // 2
    logd = np.log(np.maximum(dist, 1).astype(np.float32) / exact) / math.log(MAX_DISTANCE / exact)
    large = np.minimum(exact + (logd * (N_BUCKETS - exact)).astype(np.int32), N_BUCKETS - 1)
    return np.where(dist < exact, dist, large)


def _merge_groups(os_, lses, name):
    t = os_[0].shape[0]
    tile = ROW_TILE

    def fn(o0, o1, o2, l0, l1, l2):
        lane = lax.broadcasted_iota(jnp.int32, (1, LANES), 1)
        lo = lane < HEAD_DIM
        ls = [jnp.where(lo, l[0], l[1]) for l in (l0, l1, l2)]
        m = jnp.maximum(jnp.maximum(ls[0], ls[1]), ls[2])
        es = [jnp.exp(l - m) for l in ls]
        den = es[0] + es[1] + es[2]
        return ((es[0] * o0 + es[1] * o1 + es[2] * o2) / den,)

    ins = [(o, (tile, LANES), lambda i, hp: (i, hp), "t") for o in os_]
    ins += [(l, (2, tile, LANES), lambda i, hp: (hp, i, 0), "t") for l in lses]
    outs = [((t, DIL_WIDTH), (tile, LANES), lambda i, hp: (i, hp), "t")]
    return _fused(fn, ins, outs, (t // tile, N_PAIRS), name)[0]


def _residue_major(a, dil, axis=0):
    if dil == 1:
        return a
    shp = a.shape
    split = a.reshape(shp[:axis] + (shp[axis] // dil, dil) + shp[axis + 1:])
    return jnp.swapaxes(split, axis, axis + 1).reshape(shp)


def _position_major(a, dil, axis=0):
    if dil == 1:
        return a
    shp = a.shape
    split = a.reshape(shp[:axis] + (dil, shp[axis] // dil) + shp[axis + 1:])
    return jnp.swapaxes(split, axis, axis + 1).reshape(shp)


def _dilated_mixer(u, big, wts):
    group_cols = 3 * DIL_WIDTH
    ps = []
    for g, (_, dil) in enumerate(DIL_PATTERNS):
        cols = slice(g * group_cols, (g + 1) * group_cols)
        ps.append(_linear(_residue_major(u, dil), big["dil_w_qkv"][0][:, cols], wts["dil_w_qkv"][g][0],
                          f"dil_qkv{g}"))
    qi = np.arange(BLOCK)[:, None]
    ki = np.arange(2 * BLOCK)[None, :]
    rel = BLOCK + qi - ki
    biases = []
    for g, (window, dil) in enumerate(DIL_PATTERNS):
        span = window // dil
        bucket = _t5_bucket_np(np.clip(rel, 0, span) * dil).reshape(-1)
        onehot = jnp.asarray(np.eye(N_BUCKETS, dtype=np.float32)[bucket])
        table = wts["rel_bias"][:, g * N_HEADS:(g + 1) * N_HEADS]
        bias = jnp.dot(onehot, table, precision=lax.Precision.HIGHEST)
        biases.append(jnp.transpose(bias.reshape(BLOCK, 2 * BLOCK, N_HEADS), (2, 0, 1)))
    os_, lses = _dilated_attention(tuple(ps), tuple(biases))
    os_ = [_position_major(o, dil) for o, (_, dil) in zip(os_, DIL_PATTERNS)]
    lses = [_position_major(l, dil, axis=1) for l, (_, dil) in zip(lses, DIL_PATTERNS)]
    o = _merge_groups(os_, lses, "dil_merge")
    return _linear(o, big["dil_w_out"][0], wts["dil_w_out"][0], "dil_out")


def _forward_local(x, mods, big, wts, wsrc, gslot, late_shapes, target):
    u = _modulate(x, mods[0, 1], mods[0, 0], "mod_in")
    for i in range(DEPTH):
        sh2, sc2, g1, g2 = mods[i, 3], mods[i, 4], mods[i, 2], mods[i, 5]
        if i == 0:
            y, late_big, late_slots = _rwkv_shortconv(u, big, wts, wsrc, gslot, late_shapes)
            big = {**big, **late_big}
            wts = {**wts, **late_slots}
        else:
            y = _dilated_mixer(u, big, wts)
        x, u = _resid_ln_mod(x, y, g1, wts["ln_g"][i, 0:1], wts["ln_b"][i, 0:1], sc2, sh2, f"ln_mix{i}")
        y = _mlp(u, big["mlp_w1"][i], wts["mlp_w1"][i], big["mlp_w2"][i], wts["mlp_w2"][i], f"mlp{i}")
        if i + 1 < DEPTH:
            x, u = _resid_ln_mod(x, y, g2, wts["ln_g"][i, 1:2], wts["ln_b"][i, 1:2],
                                 mods[i + 1, 1], mods[i + 1, 0], f"ln_mlp{i}")
        else:
            return _resid_ln_loss(x, y, g2, wts["ln_g"][i, 1:2], wts["ln_b"][i, 1:2], target, "ln_loss")


SHARDED = {"ab_w_in": 2, "ab_w_out": 1, "dil_w_qkv": 2, "dil_w_out": 2, "mlp_w1": 2, "mlp_w2": 1,
           "ln_g": 2, "ln_b": 2, "rw_w_up": 2, "rw_a_up": 2, "rw_g_up": 2, "sc_conv_w": 2}
FIRST_MIXER = ("ab_w_in",)
LATER_LAYERS = ("ab_w_out", "dil_w_qkv", "dil_w_out", "mlp_w1", "mlp_w2")
SMALL_SHARDED = ("ln_g", "ln_b", "rw_w_up", "rw_a_up", "rw_g_up", "sc_conv_w")
REPLICATED = ("ada_b", "rw_mu", "rw_w0", "rw_a0", "rw_k_k", "rw_k_a", "rw_r_k", "rw_lnx_g", "rw_lnx_b", "rel_bias")
WEIGHT_ORDER = ("ada_w", "ada_b", "ln_g", "ln_b", "ab_w_in", "rw_mu", "rw_w0", "rw_w_up", "rw_a0", "rw_a_up",
                "rw_g_up", "rw_k_k", "rw_k_a", "rw_r_k", "rw_lnx_g", "rw_lnx_b", "sc_conv_w", "ab_w_out",
                "dil_w_qkv", "dil_w_out", "rel_bias", "mlp_w1", "mlp_w2")


PACK_ROWS = 16
EXCHANGE_ROWS = 2 * EXCHANGE_CHUNKS * PACK_ROWS


def _rows_of(n_elems):
    return -(-n_elems // (ROW_W * PACK_ROWS)) * PACK_ROWS


def _to_rows(a):
    flat = a.reshape(-1)
    rows = _rows_of(flat.shape[0])
    if rows * ROW_W != flat.shape[0]:
        flat = jnp.pad(flat, (0, rows * ROW_W - flat.shape[0]))
    return flat.reshape(rows, ROW_W)


def _from_rows(rows, shape):
    n = int(np.prod(shape))
    return rows.reshape(-1)[:n].reshape(shape)


def _split_chips(full, axis):
    shp = full.shape
    parts = full.reshape(shp[:axis] + (N_CHIPS, shp[axis] // N_CHIPS) + shp[axis + 1:])
    return jnp.moveaxis(parts, axis, 0)


def _join_chips(parts, axis):
    moved = jnp.moveaxis(parts, 0, axis)
    shp = moved.shape
    return moved.reshape(shp[:axis] + (shp[axis] * shp[axis + 1],) + shp[axis + 2:])


def _pack_rows(arrays, row_multiple=None):
    row_multiple = EXCHANGE_ROWS if row_multiple is None else row_multiple
    blocks = [_to_rows(a) for a in arrays]
    total = sum(b.shape[0] for b in blocks)
    pad = (-total) % row_multiple
    if pad:
        blocks.append(jnp.zeros((pad, ROW_W), blocks[0].dtype))
    return jnp.concatenate(blocks, axis=0)


def _unpack_rows(buf, shapes):
    out, r0 = [], 0
    for shp in shapes:
        n = _rows_of(int(np.prod(shp)))
        out.append(_from_rows(buf[r0:r0 + n], shp))
        r0 += n
    return out


def _pack_chips(parts):
    blocks = []
    for p in parts:
        flat = p.reshape(N_CHIPS, -1)
        rows = _rows_of(flat.shape[1])
        if rows * ROW_W != flat.shape[1]:
            flat = jnp.pad(flat, ((0, 0), (0, rows * ROW_W - flat.shape[1])))
        blocks.append(flat.reshape(N_CHIPS, rows, ROW_W))
    total = sum(b.shape[1] for b in blocks)
    pad = (-total) % EXCHANGE_ROWS
    if pad:
        blocks.append(jnp.zeros((N_CHIPS, pad, ROW_W), blocks[0].dtype))
    return jnp.concatenate(blocks, axis=1)


def _unpack_chips(buf, shapes, own=None):
    mine = lax.broadcasted_iota(jnp.int32, (N_CHIPS, 1, 1), 0) == _chip_of(_me()) if own is not None else None
    out, r0 = [], 0
    for shp in shapes:
        size = int(np.prod(shp))
        n = _rows_of(size)
        rows = buf[:, r0:r0 + n]
        if own is not None:
            rows = jnp.where(mine, own[None, r0:r0 + n], rows)
        out.append(rows.reshape(N_CHIPS, -1)[:, :size].reshape((N_CHIPS,) + tuple(shp)))
        r0 += n
    return out


def _gradient_slots(token, full_shapes, names):
    def slots():
        out = {n: jnp.zeros(full_shapes[n], BF16) for n in names}
        shp = full_shapes["dil_w_qkv"]
        out["dil_w_qkv"] = tuple(jnp.zeros(shp[:-1] + (shp[-1] // N_GROUPS,), BF16) for _ in range(N_GROUPS))
        return out

    @jax.custom_vjp
    def route(token):
        return slots()

    def fwd(token):
        return slots(), None

    def bwd(_, d):
        d = {**d, "dil_w_qkv": jnp.concatenate(d["dil_w_qkv"], axis=-1)}
        return (_pack_chips([_split_chips(d[n], SHARDED[n]).astype(BF16) for n in names]),)

    route.defvjp(fwd, bwd)
    return route(token)


def _as2d(a):
    return a.reshape(-1, a.shape[-1])


def kernel(x, c, ada_w, ada_b, ln_g, ln_b, ab_w_in, rw_mu, rw_w0, rw_w_up, rw_a0, rw_a_up, rw_g_up, rw_k_k, rw_k_a, rw_r_k, rw_lnx_g, rw_lnx_b, sc_conv_w, ab_w_out, dil_w_qkv, dil_w_out, rel_bias, mlp_w1, mlp_w2, loss_target, m_ada_w, m_ada_b, m_ln_g, m_ln_b, m_ab_w_in, m_rw_mu, m_rw_w0, m_rw_w_up, m_rw_a0, m_rw_a_up, m_rw_g_up, m_rw_k_k, m_rw_k_a, m_rw_r_k, m_rw_lnx_g, m_rw_lnx_b, m_sc_conv_w, m_ab_w_out, m_dil_w_qkv, m_dil_w_out, m_rel_bias, m_mlp_w1, m_mlp_w2, v_ada_w, v_ada_b, v_ln_g, v_ln_b, v_ab_w_in, v_rw_mu, v_rw_w0, v_rw_w_up, v_rw_a0, v_rw_a_up, v_rw_g_up, v_rw_k_k, v_rw_k_a, v_rw_r_k, v_rw_lnx_g, v_rw_lnx_b, v_sc_conv_w, v_ab_w_out, v_dil_w_qkv, v_dil_w_out, v_rel_bias, v_mlp_w1, v_mlp_w2):
    args = dict(locals())
    w_in = {n: args[n] for n in WEIGHT_ORDER}
    m_in = {n: args["m_" + n] for n in WEIGHT_ORDER}
    v_in = {n: args["v_" + n] for n in WEIGHT_ORDER}
    me = _me()
    chip = _chip_of(me)
    dev = _dev_of(me)

    c_all = _dev_all_gather(c, "gather_c")[:, 0, :]
    n_col = ada_w.shape[2]
    ada_b_cols = lax.dynamic_slice_in_dim(ada_b, chip * n_col, n_col, axis=1)[:, None, :]
    mod_cols = _ada_fwd(c_all, ada_w, ada_b_cols)

    first_buf = _pack_rows([w_in[n].astype(BF16) for n in FIRST_MIXER])
    first_all = _two_level_gather(first_buf, "gather_first")
    late_buf = _pack_rows([w_in[n].astype(BF16) for n in LATER_LAYERS])
    small_buf = _pack_rows([mod_cols] + [w_in[n] for n in SMALL_SHARDED], row_multiple=PACK_ROWS)
    small_all = _chip_all_gather(small_buf, "gather_small")

    def full_shape(n):
        shp = w_in[n].shape
        return shp[:SHARDED[n]] + (shp[SHARDED[n]] * N_CHIPS,) + shp[SHARDED[n] + 1:]

    wts = {n: w_in[n] for n in REPLICATED}
    big = {}
    for n, part in zip(FIRST_MIXER, _unpack_chips(first_all, [w_in[n].shape for n in FIRST_MIXER])):
        big[n] = _join_chips(part, SHARDED[n])
        wts[n] = jnp.zeros(full_shape(n), BF16)
    small_parts = _unpack_chips(small_all, [mod_cols.shape] + [w_in[n].shape for n in SMALL_SHARDED])
    for n, part in zip(SMALL_SHARDED, small_parts[1:]):
        wts[n] = _join_chips(part, SHARDED[n])
    mod_all = _join_chips(small_parts[0], 2)
    mods = lax.dynamic_slice_in_dim(mod_all, dev, 1, axis=1).reshape(DEPTH, 6, 1, D_MODEL)
    late_shapes = {n: full_shape(n) for n in LATER_LAYERS}
    late_slot = jnp.zeros(late_buf.shape, F32)

    def local_loss(xv, modv, wv, slot):
        return _forward_local(xv, modv, big, wv, late_buf, slot, late_shapes, loss_target[0])[0, 0]

    loss_local, (grad_x, dmods, dw, late_part) = jax.value_and_grad(local_loss, argnums=(0, 1, 2, 3))(
        x[0], mods, wts, late_slot)

    small_row = jnp.concatenate([dmods.reshape(-1)] + [dw[n].reshape(-1) for n in REPLICATED[1:]]
                                + [loss_local.reshape(1)])
    n_small = small_row.shape[0]
    n_small_pad = -(-n_small // LANES) * LANES
    small_row = jnp.pad(small_row, (0, n_small_pad - n_small))[None, :]
    rows_all = _dev_all_gather(small_row, "gather_small_grads")
    small_sum = _sum_slots(rows_all, "sum_small_grads")
    loss = small_sum[0, n_small - 1]

    dmod_all = rows_all[:, 0, :DEPTH * 6 * D_MODEL].reshape(N_DEV, DEPTH, 6 * D_MODEL)
    dmod_cols = lax.dynamic_slice_in_dim(dmod_all, chip * n_col, n_col, axis=2)
    dmod_cols = jnp.pad(jnp.moveaxis(dmod_cols, 0, 1), ((0, 0), (0, LANES - N_DEV), (0, 0)))
    c_all_t = jnp.pad(c_all.T, ((0, 0), (0, LANES - N_DEV)))
    grads = {"ada_w": _ada_grad(c_all_t, dmod_cols)}
    grads["ada_b"] = small_sum[0, :DEPTH * 6 * D_MODEL].reshape(ada_b.shape)
    r0 = DEPTH * 6 * D_MODEL
    for n in REPLICATED[1:]:
        size = int(np.prod(w_in[n].shape))
        grads[n] = small_sum[0, r0:r0 + size].reshape(w_in[n].shape)
        r0 += size

    sharded_names = FIRST_MIXER + SMALL_SHARDED
    send = _pack_chips([_split_chips(dw[n], SHARDED[n]).astype(BF16) for n in sharded_names])
    n_rows = send.shape[1]
    send = send.reshape(N_CHIPS, 2, n_rows // 2, ROW_W)
    theirs = _core_halves(send, "swap_halves")
    mine = lax.dynamic_index_in_dim(send, me[2], 1, keepdims=False)
    chip_part = _add_pairs(mine, theirs, "sum_cores")
    recv = _chip_scatter(chip_part, "scatter_grads")
    half_sum = _sum_slots(recv, "sum_chips")
    g_rows = _core_all_gather(half_sum, "gather_halves").reshape(n_rows, ROW_W)
    for n, g in zip(sharded_names, _unpack_rows(g_rows, [w_in[n].shape for n in sharded_names])):
        grads[n] = g

    late_rows = _sum_slots(_core_all_gather(late_part, "swap_late"), "sum_cores_late")
    for n, g in zip(LATER_LAYERS, _unpack_rows(late_rows, [w_in[n].shape for n in LATER_LAYERS])):
        grads[n] = g

    deltas, new_m, new_v = {}, {}, {}
    for n in WEIGHT_ORDER:
        shp = w_in[n].shape
        d, nm, nv = _adamw(_as2d(w_in[n]), _as2d(grads[n]), _as2d(m_in[n]), _as2d(v_in[n]), "adamw_" + n)
        deltas[n], new_m[n], new_v[n] = d.reshape(shp), nm.reshape(shp), nv.reshape(shp)

    return (loss, grad_x[None], *[grads[n] for n in WEIGHT_ORDER], *[deltas[n] for n in WEIGHT_ORDER],
            *[new_m[n] for n in WEIGHT_ORDER], *[new_v[n] for n in WEIGHT_ORDER])
```

```python
import functools
import math

import numpy as np
import jax
import jax.numpy as jnp
from jax import lax
from jax.experimental import pallas as pl
from jax.experimental.pallas import tpu as pltpu

F32 = jnp.float32
BF16 = jnp.bfloat16
MESH = pl.DeviceIdType.MESH

D_MODEL = 1024
DEPTH = 2
RW_WIDTH = 512
HEAD_DIM = 64
N_HEADS = 8
RW_DECAY_RANK = 64
RW_ICLR_RANK = 64
RW_GATE_RANK = 160
RW_GN_EPS = 64e-5
RW_PROJ = 3 * RW_WIDTH + RW_DECAY_RANK + RW_ICLR_RANK + RW_GATE_RANK
SC_WIDTH = 512
AB_PROJ = RW_PROJ + 3 * SC_WIDTH
DIL_PATTERNS = ((128, 1), (512, 4), (2048, 16))
N_GROUPS = 3
DIL_WIDTH = 512
DIL_PROJ = N_GROUPS * 3 * DIL_WIDTH
BLOCK = 128
N_BUCKETS = 32
MAX_DISTANCE = 2048
D_FF = 4 * D_MODEL
ALPHA = (2 * DEPTH) ** 0.25
LN_EPS = 1e-5
ADAM_LR = 0.001
ADAM_B1 = 0.9
ADAM_B2 = 0.999
ADAM_EPS = 1e-08
ADAM_WD = 0.01
ADAM_STEP = 10

N_CHIPS = 4
N_DEV = 8
LANES = 128
ROW_W = 1024
SCAN_CHUNK = 32
VMEM_LIMIT = 48 * 1024 * 1024
NEG_BIG = -1e30


def _pick(n, cands):
    for c in cands:
        if n % c == 0:
            return c
    return n


def _cparams(sem=None, vmem=None):
    return pltpu.CompilerParams(dimension_semantics=sem, vmem_limit_bytes=vmem)


_DOT_DIMS = {
    "nn": (((1,), (0,)), ((), ())),
    "nt": (((1,), (1,)), ((), ())),
    "tn": (((0,), (0,)), ((), ())),
}


def _mm(a, b, mode, name, square_relu=False, pre_act=None, narrow_out=False):
    if mode == "nn":
        (m, k), (_, n) = a.shape, b.shape
    elif mode == "nt":
        (m, k), (n, _) = a.shape, b.shape
    else:
        (k, m), (_, n) = a.shape, b.shape
    tm = _pick(m, (1024, 512, 256, 128))
    tn = _pick(n, (1024, 896, 768, 512, 384, 256, 128))
    wide_k = mode != "tn" and pre_act is None
    tk = _pick(k, (2048, 1792, 1024, 512, 256, 128) if wide_k else (1024, 512, 256, 128))
    nk = k // tk
    if mode == "tn":
        a_spec = pl.BlockSpec((tk, tm), lambda i, j, kk: (kk, i))
    else:
        a_spec = pl.BlockSpec((tm, tk), lambda i, j, kk: (i, kk))
    if mode == "nt":
        b_spec = pl.BlockSpec((tn, tk), lambda i, j, kk: (j, kk))
    else:
        b_spec = pl.BlockSpec((tk, tn), lambda i, j, kk: (kk, j))
    dims = _DOT_DIMS[mode]

    out_spec = pl.BlockSpec((tm, tn), lambda i, j, kk: (i, j))

    own_acc = narrow_out and nk > 1

    def body(*refs):
        a_ref, b_ref = refs[:2]
        h_ref = refs[2] if pre_act is not None else None
        o_ref = refs[3] if pre_act is not None else refs[2]
        act_ref = refs[o_pos + 1] if square_relu else None
        acc_ref = refs[-1] if own_acc else o_ref
        part = lax.dot_general(a_ref[...].astype(BF16), b_ref[...].astype(BF16), dims, preferred_element_type=F32)

        def finish(total):
            if h_ref is not None:
                total = total * (2.0 * jnp.maximum(h_ref[...], 0.0))
            o_ref[...] = total.astype(o_ref.dtype)
            if act_ref is not None:
                act_ref[...] = jnp.square(jnp.maximum(total, 0.0)).astype(BF16)

        if nk == 1:
            finish(part)
        else:
            kk = pl.program_id(2)

            @pl.when(kk == 0)
            def _():
                acc_ref[...] = part

            @pl.when(jnp.logical_and(kk > 0, kk < nk - 1))
            def _():
                acc_ref[...] += part

            @pl.when(kk == nk - 1)
            def _():
                finish(acc_ref[...] + part)

    operands = [a, b] + ([pre_act] if pre_act is not None else [])
    in_specs = [a_spec, b_spec] + ([out_spec] if pre_act is not None else [])
    o_pos = len(operands)
    out_shape = [jax.ShapeDtypeStruct((m, n), BF16 if narrow_out else F32)]
    out_shape += [jax.ShapeDtypeStruct((m, n), BF16)] if square_relu else []
    outs = pl.pallas_call(
        body, name=name, grid=(m // tm, n // tn, nk),
        in_specs=in_specs, out_specs=[out_spec] * len(out_shape), out_shape=out_shape,
        scratch_shapes=[pltpu.VMEM((tm, tn), F32)] if own_acc else [],
        compiler_params=_cparams(("parallel", "parallel", "arbitrary"), VMEM_LIMIT),
    )(*operands)
    return tuple(outs) if square_relu else outs[0]


def _linear(x, w, slot, name):
    @jax.custom_vjp
    def op(x, w, slot):
        return _mm(x, w, "nn", name + "_fwd")

    def fwd(x, w, slot):
        return _mm(x, w, "nn", name + "_fwd"), (x, w)

    def bwd(res, dy):
        x, w = res
        return _mm(dy, w, "nt", name + "_dx"), jnp.zeros_like(w), _mm(x, dy, "tn", name + "_dw", narrow_out=True)

    op.defvjp(fwd, bwd)
    return op(x, w, slot)


def _const_map(ndim):
    return lambda *g: (0,) * ndim


def _first_step(n_grid):
    return functools.reduce(jnp.logical_and, [pl.program_id(d) == 0 for d in range(n_grid)])


def _fused(fn, ins, outs, grid, name):
    arrays = [i[0] for i in ins]
    n_in, n_out, n_grid = len(ins), len(outs), len(grid)
    in_specs = [pl.BlockSpec(bs, im) for (_, bs, im, _) in ins]
    out_specs = [pl.BlockSpec(bs, im) for (_, bs, im, _) in outs]
    out_shapes = [jax.ShapeDtypeStruct(s, F32) for (s, _, _, _) in outs]
    sem = ("arbitrary",) * n_grid

    def fwd_call(*xs):
        def body(*refs):
            vals = [r[...] for r in refs[:n_in]]
            ys = fn(*vals)
            first = _first_step(n_grid)
            for o_ref, y, (_, _, _, kind) in zip(refs[n_in:], ys, outs):
                if kind == "t":
                    o_ref[...] = y
                else:
                    @pl.when(first)
                    def _(o_ref=o_ref):
                        o_ref[...] = jnp.zeros_like(o_ref)

                    o_ref[...] += y

        return pl.pallas_call(
            body, name=name + "_fwd", grid=grid, in_specs=in_specs, out_specs=out_specs,
            out_shape=out_shapes, compiler_params=_cparams(sem, VMEM_LIMIT))(*xs)

    def bwd_call(xs, dys):
        d_specs = [pl.BlockSpec(bs, im) for (_, bs, im, _) in outs]
        g_specs = [pl.BlockSpec(bs, im) for (_, bs, im, _) in ins]
        g_shapes = [jax.ShapeDtypeStruct(a.shape, F32) for a in arrays]

        def body(*refs):
            vals = [r[...] for r in refs[:n_in]]
            dvals = tuple(r[...] for r in refs[n_in:n_in + n_out])
            _, vjp = jax.vjp(lambda *v: tuple(fn(*v)), *vals)
            gs = vjp(dvals)
            first = _first_step(n_grid)
            for g_ref, g, (_, _, _, kind) in zip(refs[n_in + n_out:], gs, ins):
                if kind == "t":
                    g_ref[...] = g
                else:
                    @pl.when(first)
                    def _(g_ref=g_ref):
                        g_ref[...] = jnp.zeros_like(g_ref)

                    g_ref[...] += g

        return pl.pallas_call(
            body, name=name + "_bwd", grid=grid, in_specs=in_specs + d_specs, out_specs=g_specs,
            out_shape=g_shapes, compiler_params=_cparams(sem, VMEM_LIMIT))(*xs, *dys)

    @jax.custom_vjp
    def op(*xs):
        return tuple(fwd_call(*xs))

    def op_fwd(*xs):
        return tuple(fwd_call(*xs)), xs

    def op_bwd(xs, dys):
        return tuple(bwd_call(xs, dys))

    op.defvjp(op_fwd, op_bwd)
    return op(*arrays)


def _tiled(a, tile, cols=None, col_block=0):
    cols = a.shape[1] if cols is None else cols
    return (a, (tile, cols), lambda i, cb=col_block: (i, cb), "t")


def _shared(a):
    return (a, a.shape, _const_map(a.ndim), "b")


def _tiled_out(rows, cols, tile):
    return ((rows, cols), (tile, cols), lambda i: (i, 0), "t")


@jax.custom_vjp
def _bdot(x, w):
    return jnp.dot(x.astype(BF16), w.astype(BF16), preferred_element_type=F32)


def _bdot_fwd(x, w):
    return _bdot(x, w), (x, w)


def _bdot_bwd(res, dy):
    x, w = res
    dyb = dy.astype(BF16)
    dx = lax.dot_general(dyb, w.astype(BF16), _DOT_DIMS["nt"], preferred_element_type=F32)
    dw = lax.dot_general(x.astype(BF16), dyb, _DOT_DIMS["tn"], preferred_element_type=F32)
    return dx, dw


_bdot.defvjp(_bdot_fwd, _bdot_bwd)


def _head_sum(x):
    n = x.shape[-1]
    hi = lax.broadcasted_iota(jnp.int32, (n, n), 0) // HEAD_DIM
    hj = lax.broadcasted_iota(jnp.int32, (n, n), 1) // HEAD_DIM
    e = (hi == hj).astype(F32)
    return jnp.dot(x, e, precision=lax.Precision.HIGHEST, preferred_element_type=F32)


def _softplus(x):
    return jnp.maximum(x, 0.0) + jnp.log1p(jnp.exp(-jnp.abs(x)))


def _layer_norm_rows(z, g, b):
    mu = jnp.mean(z, axis=-1, keepdims=True)
    zc = z - mu
    var = jnp.mean(zc * zc, axis=-1, keepdims=True)
    return zc * lax.rsqrt(var + LN_EPS) * g + b


N_PAIRS = N_HEADS // 2


def _scan_consts():
    k = lax.broadcasted_iota(jnp.int32, (HEAD_DIM, LANES), 0)
    j = lax.broadcasted_iota(jnp.int32, (HEAD_DIM, LANES), 1)
    diag = ((j % HEAD_DIM) == k).astype(F32)
    jj = lax.broadcasted_iota(jnp.int32, (LANES, LANES), 0) // HEAD_DIM
    ll = lax.broadcasted_iota(jnp.int32, (LANES, LANES), 1) // HEAD_DIM
    same_head = (jj == ll).astype(BF16)
    return diag, same_head


def _unrolled(n, body, carry):
    for i in range(n):
        carry = body(i, carry)
    return carry


def _fill_cols(srcs, col_ref, n_steps):
    group = 16
    assert n_steps % group == 0
    j = lax.broadcasted_iota(jnp.int32, (LANES, LANES), 0)
    lane_head = lax.broadcasted_iota(jnp.int32, (LANES, LANES), 1) // HEAD_DIM
    for t0 in range(0, n_steps, group):
        blocks = []
        for src in srcs:
            x = src[t0:t0 + group, :]
            hi = x.astype(BF16).astype(F32)
            r1 = x - hi
            mid = r1.astype(BF16).astype(F32)
            x48 = jnp.concatenate([hi, mid, r1 - mid], axis=0)
            for hp in range(N_PAIRS):
                xp = x48[:, hp * LANES:(hp + 1) * LANES]
                y = jnp.concatenate([xp, pltpu.roll(xp, HEAD_DIM, 1), jnp.zeros((32, LANES), F32)], axis=0)
                blocks.append(y.T[:HEAD_DIM].astype(BF16))
        lhs = jnp.concatenate(blocks, axis=0)
        for t in range(group):
            pick = jnp.logical_and(j < 96, jnp.logical_and(j % 16 == t, j // 48 == lane_head))
            out = jnp.dot(lhs, pick.astype(BF16), preferred_element_type=F32)
            for vi in range(len(srcs)):
                for hp in range(N_PAIRS):
                    r0 = (vi * N_PAIRS + hp) * HEAD_DIM
                    col_ref[vi, hp, t0 + t] = out[r0:r0 + HEAD_DIM]


def _scan_fwd_call(r, w, k, v, a, b, wsrc):
    t_len = r.shape[0]
    ch = SCAN_CHUNK
    n_ch = t_len // ch
    w_rows, w_cols = wsrc.shape
    half = w_rows // 2
    chunks = _n_chunks(half)
    rc = half // chunks
    transfers = [(q, kk) for q in range(chunks) for kk in range(len(CHIP_FLIPS))]
    hand_on_step = [max(1, min(n_ch - 1, (i + 1) * (n_ch - 8) // len(transfers) + 3)) for i in range(len(transfers))]

    def body(r_ref, w_ref, k_ref, v_ref, a_ref, b_ref, src_ref, y_ref, ck_ref, g_ref,
             st_ref, col_ref, send1, recv1, send2, recv2):
        c = pl.program_id(0)
        me = _me()
        core = me[2]
        sibling = _flip(me, CORE_FLIP)

        def over_ici(q, kk):
            return pltpu.make_async_remote_copy(
                src_ref=src_ref.at[core, pl.ds(q * rc, rc)], dst_ref=g_ref.at[_chip_of(me), core, pl.ds(q * rc, rc)],
                send_sem=send1.at[kk * chunks + q], recv_sem=recv1.at[kk * chunks + q],
                device_id=_flip(me, CHIP_FLIPS[kk]), device_id_type=MESH)

        def hand_on(q, kk):
            landed = g_ref.at[_chip_of(_flip(me, CHIP_FLIPS[kk])), core, pl.ds(q * rc, rc)]
            return pltpu.make_async_remote_copy(
                src_ref=landed, dst_ref=landed, send_sem=send2.at[kk * chunks + q], recv_sem=recv2.at[kk * chunks + q],
                device_id=sibling, device_id_type=MESH)

        @pl.when(c == 0)
        def _():
            st_ref[...] = jnp.zeros_like(st_ref)
            for q, kk in transfers:
                over_ici(q, kk).start()

        for step in sorted(set(hand_on_step)):
            @pl.when(c == step)
            def _(step=step):
                for (q, kk), s in zip(transfers, hand_on_step):
                    if s == step:
                        over_ici(q, kk).wait_recv()
                        hand_on(q, kk).start()

        ck_ref[0] = st_ref[...]
        _fill_cols((w_ref, a_ref, b_ref, k_ref, r_ref), col_ref, ch)

        def step(t, states):
            new = []
            for hp in range(N_PAIRS):
                lanes = slice(hp * LANES, (hp + 1) * LANES)
                s = states[hp]
                sa = jnp.sum(s * col_ref[1, hp, t], axis=0, keepdims=True)
                s = s * col_ref[0, hp, t] + col_ref[2, hp, t] * sa + col_ref[3, hp, t] * v_ref[pl.ds(t, 1), lanes]
                y_ref[pl.ds(t, 1), lanes] = jnp.sum(s * col_ref[4, hp, t], axis=0, keepdims=True)
                new.append(s)
            return tuple(new)

        states = _unrolled(ch, step, tuple(st_ref[hp] for hp in range(N_PAIRS)))
        for hp in range(N_PAIRS):
            st_ref[hp] = states[hp]

        @pl.when(c == n_ch - 1)
        def _():
            for q, kk in transfers:
                hand_on(q, kk).wait_recv()
            for q, kk in transfers:
                over_ici(q, kk).wait_send()
                hand_on(q, kk).wait_send()

    row_spec = pl.BlockSpec((ch, RW_WIDTH), lambda c: (c, 0))
    any_spec = pl.BlockSpec(memory_space=pl.ANY)
    n_sem = len(transfers)
    y, ck, got = pl.pallas_call(
        body, name="rwkv_scan_fwd", grid=(n_ch,),
        in_specs=[row_spec] * 6 + [any_spec],
        out_specs=[row_spec, pl.BlockSpec((1, N_PAIRS, HEAD_DIM, LANES), lambda c: (c, 0, 0, 0)), any_spec],
        out_shape=[jax.ShapeDtypeStruct((t_len, RW_WIDTH), F32),
                   jax.ShapeDtypeStruct((n_ch, N_PAIRS, HEAD_DIM, LANES), F32),
                   jax.ShapeDtypeStruct((N_CHIPS, 2, half, w_cols), wsrc.dtype)],
        scratch_shapes=[pltpu.VMEM((N_PAIRS, HEAD_DIM, LANES), F32),
                        pltpu.VMEM((5, N_PAIRS, ch, HEAD_DIM, LANES), F32)]
        + [pltpu.SemaphoreType.DMA((n_sem,)) for _ in range(4)],
        compiler_params=_cparams(("arbitrary",), VMEM_LIMIT),
    )(r, w, k, v, a, b, wsrc.reshape(2, half, w_cols))
    return y, ck, got.reshape(N_CHIPS, w_rows, w_cols)


def _scan_bwd_call(r, w, k, v, a, b, ck, dy, gsend):
    t_len = r.shape[0]
    ch = SCAN_CHUNK
    n_ch = t_len // ch
    _, g_rows, g_cols = gsend.shape
    chunks = _n_chunks(g_rows)
    rc = g_rows // chunks
    transfers = [(q, kk) for q in range(chunks) for kk in range(len(CHIP_FLIPS))]

    def body(r_ref, w_ref, k_ref, v_ref, a_ref, b_ref, ck_ref, dy_ref, gs_ref,
             dr_ref, dw_ref, dk_ref, dv_ref, da_ref, db_ref, gr_ref,
             ds_ref, col_ref, sp_ref, sa_ref, send_sems, recv_sems):
        c = pl.program_id(0)
        me = _me()

        def to_chip(q, kk):
            peer = _flip(me, CHIP_FLIPS[kk])
            return pltpu.make_async_remote_copy(
                src_ref=gs_ref.at[_chip_of(peer), pl.ds(q * rc, rc)], dst_ref=gr_ref.at[_chip_of(me), pl.ds(q * rc, rc)],
                send_sem=send_sems.at[kk * chunks + q], recv_sem=recv_sems.at[kk * chunks + q],
                device_id=peer, device_id_type=MESH)

        @pl.when(c == 0)
        def _():
            ds_ref[...] = jnp.zeros_like(ds_ref)
            for q, kk in transfers:
                to_chip(q, kk).start()

        diag_f, same_head = _scan_consts()
        _fill_cols((w_ref, a_ref, b_ref, k_ref, r_ref), col_ref, ch)

        def replay(t, states):
            new = []
            for hp in range(N_PAIRS):
                lanes = slice(hp * LANES, (hp + 1) * LANES)
                s = states[hp]
                sp_ref[t, hp] = s
                sa = jnp.sum(s * col_ref[1, hp, t], axis=0, keepdims=True)
                sa_ref[pl.ds(t, 1), lanes] = sa
                new.append(s * col_ref[0, hp, t] + col_ref[2, hp, t] * sa
                           + col_ref[3, hp, t] * v_ref[pl.ds(t, 1), lanes])
            return tuple(new)

        last = _unrolled(ch, replay, tuple(ck_ref[0, hp] for hp in range(N_PAIRS)))
        for hp in range(N_PAIRS):
            sp_ref[ch, hp] = last[hp]

        def key_rows(ps):
            stacked = jnp.concatenate([p.astype(BF16) for p in ps], axis=0)
            q = jnp.dot(stacked, same_head, preferred_element_type=F32)
            return [jnp.sum(q[i * HEAD_DIM:(i + 1) * HEAD_DIM] * diag_f, axis=0, keepdims=True)
                    for i in range(len(ps))]

        def back(i, grads):
            t = ch - 1 - i
            new = []
            for hp in range(N_PAIRS):
                lanes = slice(hp * LANES, (hp + 1) * LANES)
                wc, ac, bc, kc, rc = (col_ref[vi, hp, t] for vi in range(5))
                sp = sp_ref[t, hp]
                sa = sa_ref[pl.ds(t, 1), lanes]
                vrow = v_ref[pl.ds(t, 1), lanes]
                dyrow = dy_ref[pl.ds(t, 1), lanes]
                st = sp_ref[t + 1, hp]
                g = grads[hp] + rc * dyrow
                dsa = jnp.sum(g * bc, axis=0, keepdims=True)
                dv_ref[pl.ds(t, 1), lanes] = jnp.sum(g * kc, axis=0, keepdims=True)
                rows = key_rows([st * dyrow, g * vrow, g * sa, g * sp, sp * dsa])
                for out_ref, row in zip((dr_ref, dk_ref, db_ref, dw_ref, da_ref), rows):
                    out_ref[pl.ds(t, 1), lanes] = row
                new.append(g * wc + ac * dsa)
            return tuple(new)

        grads = _unrolled(ch, back, tuple(ds_ref[hp] for hp in range(N_PAIRS)))
        for hp in range(N_PAIRS):
            ds_ref[hp] = grads[hp]

        @pl.when(c == n_ch - 1)
        def _():
            for q, kk in transfers:
                to_chip(q, kk).wait()

    row_spec = pl.BlockSpec((ch, RW_WIDTH), lambda c: (n_ch - 1 - c, 0))
    any_spec = pl.BlockSpec(memory_space=pl.ANY)
    out_sds = jax.ShapeDtypeStruct((t_len, RW_WIDTH), F32)
    n_sem = len(transfers)
    return pl.pallas_call(
        body, name="rwkv_scan_bwd", grid=(n_ch,),
        in_specs=[row_spec] * 6 + [pl.BlockSpec((1, N_PAIRS, HEAD_DIM, LANES), lambda c: (n_ch - 1 - c, 0, 0, 0)),
                                   row_spec, any_spec],
        out_specs=[row_spec] * 6 + [any_spec],
        out_shape=[out_sds] * 6 + [jax.ShapeDtypeStruct(gsend.shape, gsend.dtype)],
        scratch_shapes=[pltpu.VMEM((N_PAIRS, HEAD_DIM, LANES), F32),
                        pltpu.VMEM((5, N_PAIRS, ch, HEAD_DIM, LANES), F32),
                        pltpu.VMEM((ch + 1, N_PAIRS, HEAD_DIM, LANES), F32),
                        pltpu.VMEM((ch, RW_WIDTH), F32),
                        pltpu.SemaphoreType.DMA((n_sem,)), pltpu.SemaphoreType.DMA((n_sem,))],
        compiler_params=_cparams(("arbitrary",), VMEM_LIMIT),
    )(r, w, k, v, a, b, ck, dy, gsend)


def _scan_token(wsrc):
    return jnp.zeros((N_CHIPS,) + wsrc.shape, BF16)


@jax.custom_vjp
def _rwkv_scan(r, w, k, v, a, b, wsrc, gslot):
    y, _, got = _scan_fwd_call(r, w, k, v, a, b, wsrc)
    return y, got, _scan_token(wsrc)


def _rwkv_scan_fwd(r, w, k, v, a, b, wsrc, gslot):
    y, ck, got = _scan_fwd_call(r, w, k, v, a, b, wsrc)
    return (y, got, _scan_token(wsrc)), (r, w, k, v, a, b, ck, wsrc)


def _rwkv_scan_bwd(res, cts):
    *saved, wsrc = res
    dy, _, gsend = cts
    *dins, got = _scan_bwd_call(*saved, dy, gsend)
    summed = _sum_slots_own(got, gsend, _chip_of(_me()), "sum_chips_late")
    return (*dins, jnp.zeros_like(wsrc), summed)


_rwkv_scan.defvjp(_rwkv_scan_fwd, _rwkv_scan_bwd)


ATT_SCALE = HEAD_DIM ** -0.5


def _att_masks():
    qi = lax.broadcasted_iota(jnp.int32, (BLOCK, BLOCK), 0)
    ki = lax.broadcasted_iota(jnp.int32, (BLOCK, BLOCK), 1)
    lane = lax.broadcasted_iota(jnp.int32, (1, LANES), 1)
    return ki <= qi, ki >= qi, lane


def _att_fwd_call(p, bias, g, dil):
    t_len = p.shape[0]
    l_len = t_len // dil
    nb = l_len // BLOCK

    def body(q_ref, kc_ref, kp_ref, vc_ref, vp_ref, bias_ref, o_ref, lse_ref):
        n = pl.program_id(1)
        cur_ok, prev_band, lane = _att_masks()
        prev_ok = jnp.logical_and(prev_band, n > 0)
        for hp in range(N_PAIRS):
            lanes = slice(hp * LANES, (hp + 1) * LANES)
            q2 = q_ref[:, lanes].astype(BF16)
            kc = kc_ref[:, lanes].astype(BF16)
            kp = kp_ref[:, lanes].astype(BF16)
            vc = vc_ref[:, lanes].astype(BF16)
            vp = vp_ref[:, lanes].astype(BF16)
            o2 = jnp.zeros((BLOCK, LANES), F32)
            for hh in range(2):
                h = 2 * hp + hh
                mine = (lane // HEAD_DIM) == hh
                qm = jnp.where(mine, q2, jnp.zeros_like(q2))
                s_c = lax.dot_general(qm, kc, _DOT_DIMS["nt"], preferred_element_type=F32) * ATT_SCALE
                s_p = lax.dot_general(qm, kp, _DOT_DIMS["nt"], preferred_element_type=F32) * ATT_SCALE
                s_c = jnp.where(cur_ok, s_c + bias_ref[h, :, BLOCK:], NEG_BIG)
                s_p = jnp.where(prev_ok, s_p + bias_ref[h, :, :BLOCK], NEG_BIG)
                m = jnp.maximum(jnp.max(s_c, axis=-1, keepdims=True), jnp.max(s_p, axis=-1, keepdims=True))
                e_c = jnp.exp(s_c - m)
                e_p = jnp.exp(s_p - m)
                den = jnp.sum(e_c, axis=-1, keepdims=True) + jnp.sum(e_p, axis=-1, keepdims=True)
                o_h = (jnp.dot((e_c / den).astype(BF16), vc, preferred_element_type=F32)
                       + jnp.dot((e_p / den).astype(BF16), vp, preferred_element_type=F32))
                o2 = o2 + jnp.where(mine, o_h, 0.0)
                lse_ref[h] = jnp.broadcast_to(m + jnp.log(den), (BLOCK, LANES))
            o_ref[:, lanes] = o2

    def col(j):
        return lambda r, n: (r * nb + n, j)

    def col_prev(j):
        return lambda r, n: (r * nb + jnp.maximum(n - 1, 0), j)

    blk = (BLOCK, DIL_WIDTH)
    return pl.pallas_call(
        body, name=f"dil_att_fwd_g{g}", grid=(dil, nb),
        in_specs=[pl.BlockSpec(blk, col(0)), pl.BlockSpec(blk, col(1)), pl.BlockSpec(blk, col_prev(1)),
                  pl.BlockSpec(blk, col(2)), pl.BlockSpec(blk, col_prev(2)),
                  pl.BlockSpec(bias.shape, _const_map(3))],
        out_specs=[pl.BlockSpec(blk, lambda r, n: (r * nb + n, 0)),
                   pl.BlockSpec((N_HEADS, BLOCK, LANES), lambda r, n: (0, r * nb + n, 0))],
        out_shape=[jax.ShapeDtypeStruct((t_len, DIL_WIDTH), F32),
                   jax.ShapeDtypeStruct((N_HEADS, t_len, LANES), F32)],
        compiler_params=_cparams(("arbitrary", "arbitrary"), VMEM_LIMIT),
    )(p, p, p, p, p, bias)


def _att_bwd_call(p, bias, o, lse, do, dlse, g, dil):
    t_len = p.shape[0]
    l_len = t_len // dil
    nb = l_len // BLOCK

    def body(q_ref, qn_ref, k_ref, v_ref, do_ref, don_ref, o_ref, on_ref, lse_ref, lsen_ref, dl_ref, dln_ref,
             bias_ref, dq_ref, dk_ref, dv_ref, dbias_ref, carry_ref):
        r = pl.program_id(0)
        n = pl.program_id(1)
        cur_ok, prev_band, lane = _att_masks()
        has_next = n + 1 < nb

        @pl.when(jnp.logical_and(r == 0, n == 0))
        def _():
            dbias_ref[...] = jnp.zeros_like(dbias_ref)

        @pl.when(n == 0)
        def _():
            carry_ref[...] = jnp.zeros_like(carry_ref)

        for hp in range(N_PAIRS):
            lanes = slice(hp * LANES, (hp + 1) * LANES)
            k2 = k_ref[:, lanes].astype(BF16)
            v2 = v_ref[:, lanes].astype(BF16)
            dk2 = jnp.zeros((BLOCK, LANES), F32)
            dv2 = jnp.zeros((BLOCK, LANES), F32)
            dq_cur = carry_ref[:, lanes]
            dq_next = jnp.zeros((BLOCK, LANES), F32)
            for hh in range(2):
                h = 2 * hp + hh
                mine = (lane // HEAD_DIM) == hh
                tiles = (
                    (q_ref, do_ref, o_ref, lse_ref, dl_ref, cur_ok, slice(BLOCK, 2 * BLOCK), None),
                    (qn_ref, don_ref, on_ref, lsen_ref, dln_ref, prev_band, slice(0, BLOCK), has_next),
                )
                if nb == 1:
                    tiles = tiles[:1]
                for ti, (qr, dor, orf, lr, dlr, ok, bcols, gate) in enumerate(tiles):
                    q2 = qr[:, lanes].astype(BF16)
                    qm = jnp.where(mine, q2, jnp.zeros_like(q2))
                    do_f = jnp.where(mine, dor[:, lanes], 0.0)
                    dom = do_f.astype(BF16)
                    s = lax.dot_general(qm, k2, _DOT_DIMS["nt"], preferred_element_type=F32) * ATT_SCALE
                    s = s + bias_ref[h, :, bcols]
                    if gate is not None:
                        ok = jnp.logical_and(ok, gate)
                    pr = jnp.where(ok, jnp.exp(jnp.minimum(s - lr[h], 0.0)), 0.0)
                    dp = lax.dot_general(dom, v2, _DOT_DIMS["nt"], preferred_element_type=F32)
                    delta = jnp.sum(do_f * orf[:, lanes], axis=-1, keepdims=True)
                    dl = jnp.sum(dlr[h], axis=-1, keepdims=True)
                    ds = pr * (dp - delta + dl)
                    dsb = ds.astype(BF16)
                    dq_h = jnp.where(mine, jnp.dot(dsb, k2, preferred_element_type=F32), 0.0) * ATT_SCALE
                    if ti == 0:
                        dq_cur = dq_cur + dq_h
                    else:
                        dq_next = dq_next + dq_h
                    dk2 = dk2 + lax.dot_general(dsb, qm, _DOT_DIMS["tn"], preferred_element_type=F32) * ATT_SCALE
                    dv2 = dv2 + lax.dot_general(pr.astype(BF16), dom, _DOT_DIMS["tn"], preferred_element_type=F32)
                    dbias_ref[h, :, bcols] += ds
            dq_ref[:, lanes] = dq_cur
            carry_ref[:, lanes] = dq_next
            dk_ref[:, lanes] = dk2
            dv_ref[:, lanes] = dv2

    def nxt(n):
        return jnp.minimum(n + 1, nb - 1)

    blk = (BLOCK, DIL_WIDTH)
    hblk = (N_HEADS, BLOCK, LANES)
    qcol = lambda j: (lambda r, n: (r * nb + n, j))
    q_next = lambda r, n: (r * nb + nxt(n), 0)
    rown = lambda r, n: (r * nb + n, 0)
    rown_next = lambda r, n: (r * nb + nxt(n), 0)
    hrow = lambda r, n: (0, r * nb + n, 0)
    hrow_next = lambda r, n: (0, r * nb + nxt(n), 0)
    sds = jax.ShapeDtypeStruct((t_len, DIL_WIDTH), F32)
    return pl.pallas_call(
        body, name=f"dil_att_bwd_g{g}", grid=(dil, nb),
        in_specs=[pl.BlockSpec(blk, qcol(0)), pl.BlockSpec(blk, q_next),
                  pl.BlockSpec(blk, qcol(1)), pl.BlockSpec(blk, qcol(2)),
                  pl.BlockSpec(blk, rown), pl.BlockSpec(blk, rown_next),
                  pl.BlockSpec(blk, rown), pl.BlockSpec(blk, rown_next),
                  pl.BlockSpec(hblk, hrow), pl.BlockSpec(hblk, hrow_next),
                  pl.BlockSpec(hblk, hrow), pl.BlockSpec(hblk, hrow_next),
                  pl.BlockSpec(bias.shape, _const_map(3))],
        out_specs=[pl.BlockSpec(blk, rown)] * 3 + [pl.BlockSpec(bias.shape, _const_map(3))],
        out_shape=[sds, sds, sds, jax.ShapeDtypeStruct(bias.shape, F32)],
        scratch_shapes=[pltpu.VMEM((BLOCK, DIL_WIDTH), F32)],
        compiler_params=_cparams(("arbitrary", "arbitrary"), VMEM_LIMIT),
    )(p, p, p, p, do, do, o, o, lse, lse, dlse, dlse, bias)


def _att_all_groups(ps, biases):
    outs = [_att_fwd_call(ps[g], biases[g], g, dil) for g, (_, dil) in enumerate(DIL_PATTERNS)]
    return tuple(o for o, _ in outs), tuple(l for _, l in outs)


@jax.custom_vjp
def _dilated_attention(ps, biases):
    return _att_all_groups(ps, biases)


def _dilated_attention_fwd(ps, biases):
    os_, lses = _att_all_groups(ps, biases)
    return (os_, lses), (ps, biases, os_, lses)


def _dilated_attention_bwd(res, cts):
    ps, biases, os_, lses = res
    dos, dlses = cts
    dps, dbiases = [], []
    for g, (_, dil) in enumerate(DIL_PATTERNS):
        dq, dk, dv, dbias = _att_bwd_call(ps[g], biases[g], os_[g], lses[g], dos[g], dlses[g], g, dil)
        dps.append(jnp.concatenate([dq, dk, dv], axis=1))
        dbiases.append(dbias)
    return tuple(dps), tuple(dbiases)


_dilated_attention.defvjp(_dilated_attention_fwd, _dilated_attention_bwd)


def _me():
    return lax.axis_index("x"), lax.axis_index("y"), lax.axis_index("c")


def _flip(me, f):
    return tuple((1 - m) if b else m for m, b in zip(me, f))


def _chip_of(d):
    return 2 * d[0] + d[1]


def _dev_of(d):
    return 4 * d[0] + 2 * d[1] + d[2]


EXCHANGE_CHUNKS = 4
CHIP_FLIPS = ((1, 0, 0), (0, 1, 0), (1, 1, 0))
ALL_FLIPS = tuple((a, b, c) for a in (0, 1) for b in (0, 1) for c in (0, 1) if a or b or c)
CORE_FLIP = (0, 0, 1)


def _n_chunks(rows):
    return EXCHANGE_CHUNKS if rows % (EXCHANGE_CHUNKS * PACK_ROWS) == 0 else 1


def _exchange(src, n_slots, transfers, name):
    _, rows, cols = src.shape
    chunks = _n_chunks(rows)
    rc = rows // chunks
    n = len(transfers) * chunks

    def body(src_ref, dst_ref, send_sems, recv_sems):
        me = _me()
        copies = []
        for q in range(chunks):
            for kk, (f, src_slot, dst_slot) in enumerate(transfers):
                peer = _flip(me, f)
                cp = pltpu.make_async_remote_copy(
                    src_ref=src_ref.at[src_slot(me, peer), pl.ds(q * rc, rc)],
                    dst_ref=dst_ref.at[dst_slot(me, peer), pl.ds(q * rc, rc)],
                    send_sem=send_sems.at[kk * chunks + q], recv_sem=recv_sems.at[kk * chunks + q],
                    device_id=peer, device_id_type=MESH)
                cp.start()
                copies.append(cp)
        for cp in copies:
            cp.wait()

    return pl.pallas_call(
        body, name=name,
        out_shape=jax.ShapeDtypeStruct((n_slots, rows, cols), src.dtype),
        in_specs=[pl.BlockSpec(memory_space=pl.ANY)],
        out_specs=pl.BlockSpec(memory_space=pl.ANY),
        scratch_shapes=[pltpu.SemaphoreType.DMA((n,)), pltpu.SemaphoreType.DMA((n,))],
    )(src)


def _set_slot(buf, block, index):
    return lax.dynamic_update_slice_in_dim(buf, block[None].astype(buf.dtype), index, axis=0)


def _chip_all_gather(src, name):
    got = _exchange(src[None], N_CHIPS, [(f, lambda me, peer: 0, lambda me, peer: _chip_of(me)) for f in CHIP_FLIPS], name)
    return _set_slot(got, src, _chip_of(_me()))


def _dev_all_gather(src, name):
    got = _exchange(src[None], N_DEV, [(f, lambda me, peer: 0, lambda me, peer: _dev_of(me)) for f in ALL_FLIPS], name)
    return _set_slot(got, src, _dev_of(_me()))


def _chip_scatter(src, name):
    got = _exchange(src, N_CHIPS, [(f, lambda me, peer: _chip_of(peer), lambda me, peer: _chip_of(me))
                                   for f in CHIP_FLIPS], name)
    chip = _chip_of(_me())
    return _set_slot(got, lax.dynamic_index_in_dim(src, chip, 0, keepdims=False), chip)


def _core_halves(src, name):
    s, _, half, cols = src.shape
    transfers = [(CORE_FLIP, (lambda me, peer, j=j: 2 * j + peer[2]), (lambda me, peer, j=j: j)) for j in range(s)]
    return _exchange(src.reshape(2 * s, half, cols), s, transfers, name)


def _core_all_gather(src, name):
    got = _exchange(src[None], 2, [(CORE_FLIP, lambda me, peer: 0, lambda me, peer: me[2])], name)
    return _set_slot(got, src, _me()[2])


def _two_level_gather(src, name):
    rows, cols = src.shape
    half = rows // 2
    chunks = _n_chunks(half)
    rc = half // chunks
    n = len(CHIP_FLIPS) * chunks

    def body(src_ref, g_ref, send1, recv1, send2, recv2):
        me = _me()
        c = me[2]
        sibling = _flip(me, CORE_FLIP)
        first, second = [], []
        for q in range(chunks):
            for kk, f in enumerate(CHIP_FLIPS):
                peer = _flip(me, f)
                cp = pltpu.make_async_remote_copy(
                    src_ref=src_ref.at[c, pl.ds(q * rc, rc)], dst_ref=g_ref.at[_chip_of(me), c, pl.ds(q * rc, rc)],
                    send_sem=send1.at[kk * chunks + q], recv_sem=recv1.at[kk * chunks + q],
                    device_id=peer, device_id_type=MESH)
                cp.start()
                first.append((cp, _chip_of(peer), kk * chunks + q, q))
        for cp, origin, idx, q in first:
            cp.wait_recv()
            fw = pltpu.make_async_remote_copy(
                src_ref=g_ref.at[origin, c, pl.ds(q * rc, rc)], dst_ref=g_ref.at[origin, c, pl.ds(q * rc, rc)],
                send_sem=send2.at[idx], recv_sem=recv2.at[idx],
                device_id=sibling, device_id_type=MESH)
            fw.start()
            second.append(fw)
        for fw in second:
            fw.wait_recv()
        for cp, _, _, _ in first:
            cp.wait_send()
        for fw in second:
            fw.wait_send()

    got = pl.pallas_call(
        body, name=name,
        out_shape=jax.ShapeDtypeStruct((N_CHIPS, 2, half, cols), src.dtype),
        in_specs=[pl.BlockSpec(memory_space=pl.ANY)],
        out_specs=pl.BlockSpec(memory_space=pl.ANY),
        scratch_shapes=[pltpu.SemaphoreType.DMA((n,)) for _ in range(4)],
    )(src.reshape(2, half, cols))
    return _set_slot(got.reshape(N_CHIPS, rows, cols), src, _chip_of(_me()))


def _sum_slots(x, name):
    s, rows, cols = x.shape
    tile = _pick(rows, (512, 256, 128, 64, 32, 16, 8))

    def body(x_ref, o_ref):
        acc = x_ref[0].astype(F32)
        for i in range(1, s):
            acc = acc + x_ref[i].astype(F32)
        o_ref[...] = acc

    return pl.pallas_call(
        body, name=name, grid=(rows // tile,),
        in_specs=[pl.BlockSpec((s, tile, cols), lambda i: (0, i, 0))],
        out_specs=pl.BlockSpec((tile, cols), lambda i: (i, 0)),
        out_shape=jax.ShapeDtypeStruct((rows, cols), F32),
        compiler_params=_cparams(("parallel",), VMEM_LIMIT),
    )(x)


def _sum_slots_own(recv, send, chip, name):
    s, rows, cols = recv.shape
    tile = _pick(rows, (512, 256, 128, 64, 32, 16, 8))

    def body(chip_ref, recv_ref, own_ref, o_ref):
        acc = None
        for j in range(s):
            term = jnp.where(chip_ref[0] == j, own_ref[0], recv_ref[j]).astype(F32)
            acc = term if acc is None else acc + term
        o_ref[...] = acc

    grid_spec = pltpu.PrefetchScalarGridSpec(
        num_scalar_prefetch=1, grid=(rows // tile,),
        in_specs=[pl.BlockSpec((s, tile, cols), lambda i, c: (0, i, 0)),
                  pl.BlockSpec((1, tile, cols), lambda i, c: (c[0], i, 0))],
        out_specs=pl.BlockSpec((tile, cols), lambda i, c: (i, 0)))
    return pl.pallas_call(
        body, name=name, grid_spec=grid_spec,
        out_shape=jax.ShapeDtypeStruct((rows, cols), F32),
        compiler_params=_cparams(("arbitrary",), VMEM_LIMIT),
    )(jnp.reshape(chip, (1,)).astype(jnp.int32), recv, send)


def _add_pairs(a, b, name):
    s, rows, cols = a.shape
    tile = _pick(rows, (512, 256, 128, 64, 32, 16, 8))

    def body(a_ref, b_ref, o_ref):
        o_ref[...] = (a_ref[...].astype(F32) + b_ref[...].astype(F32)).astype(o_ref.dtype)

    spec = pl.BlockSpec((1, tile, cols), lambda j, i: (j, i, 0))
    return pl.pallas_call(
        body, name=name, grid=(s, rows // tile),
        in_specs=[spec, spec], out_specs=spec,
        out_shape=jax.ShapeDtypeStruct(a.shape, a.dtype),
        compiler_params=_cparams(("parallel", "parallel"), VMEM_LIMIT),
    )(a, b)


def _adamw(w, g, m, v, name):
    rows, cols = w.shape
    tile = rows
    if rows * cols * 4 > 2 * 1024 * 1024:
        tile = _pick(rows, (256, 128, 64, 32, 16, 8))
    c1 = 1.0 / (1.0 - ADAM_B1 ** ADAM_STEP)
    c2 = 1.0 / (1.0 - ADAM_B2 ** ADAM_STEP)

    def body(w_ref, g_ref, m_ref, v_ref, d_ref, nm_ref, nv_ref):
        gv = g_ref[...]
        nm = ADAM_B1 * m_ref[...] + (1.0 - ADAM_B1) * gv
        nv = ADAM_B2 * v_ref[...] + (1.0 - ADAM_B2) * (gv * gv)
        m_hat = nm * c1
        v_hat = nv * c2
        d_ref[...] = -ADAM_LR * (m_hat / (jnp.sqrt(v_hat) + ADAM_EPS) + ADAM_WD * w_ref[...])
        nm_ref[...] = nm
        nv_ref[...] = nv

    spec = pl.BlockSpec((tile, cols), lambda i: (i, 0))
    sds = jax.ShapeDtypeStruct((rows, cols), F32)
    return pl.pallas_call(
        body, name=name, grid=(rows // tile,),
        in_specs=[spec] * 4, out_specs=[spec] * 3, out_shape=[sds] * 3,
        compiler_params=_cparams(("parallel",), VMEM_LIMIT),
    )(w, g, m, v)


def _ada_fwd(c_all, ada_w, ada_b_cols):
    n_col = ada_w.shape[2]

    def body(c_ref, w_ref, b_ref, o_ref):
        cv = c_ref[...]
        cond = (cv * jax.nn.sigmoid(cv)).astype(BF16)
        o_ref[0] = jnp.dot(cond, w_ref[0].astype(BF16), preferred_element_type=F32) + b_ref[0]

    return pl.pallas_call(
        body, name="ada_fwd", grid=(DEPTH,),
        in_specs=[pl.BlockSpec(c_all.shape, lambda i: (0, 0)),
                  pl.BlockSpec((1, D_MODEL, n_col), lambda i: (i, 0, 0)),
                  pl.BlockSpec((1, 1, n_col), lambda i: (i, 0, 0))],
        out_specs=pl.BlockSpec((1, N_DEV, n_col), lambda i: (i, 0, 0)),
        out_shape=jax.ShapeDtypeStruct((DEPTH, N_DEV, n_col), F32),
        compiler_params=_cparams(("parallel",), VMEM_LIMIT),
    )(c_all, ada_w, ada_b_cols)


def _ada_grad(c_all_t, dmod_cols):
    n_col = dmod_cols.shape[2]

    def body(c_ref, d_ref, o_ref):
        cv = c_ref[...]
        cond = cv * jax.nn.sigmoid(cv)
        o_ref[0] = jnp.dot(cond, d_ref[0], precision=lax.Precision.HIGHEST, preferred_element_type=F32)

    return pl.pallas_call(
        body, name="ada_grad", grid=(DEPTH,),
        in_specs=[pl.BlockSpec(c_all_t.shape, lambda i: (0, 0)),
                  pl.BlockSpec((1, LANES, n_col), lambda i: (i, 0, 0))],
        out_specs=pl.BlockSpec((1, D_MODEL, n_col), lambda i: (i, 0, 0)),
        out_shape=jax.ShapeDtypeStruct((DEPTH, D_MODEL, n_col), F32),
        compiler_params=_cparams(("parallel",), VMEM_LIMIT),
    )(c_all_t, dmod_cols)


ROW_TILE = 256


def _shift_rows(a, n=1):
    return jnp.pad(a, ((n, 0), (0, 0)))[:-n]


def _modulate(x, sc, sh, name):
    def fn(x, sc, sh):
        return (x * (1.0 + sc) + sh,)

    t = x.shape[0]
    return _fused(fn, [_tiled(x, ROW_TILE), _shared(sc), _shared(sh)],
                  [_tiled_out(t, D_MODEL, ROW_TILE)], (t // ROW_TILE,), name)[0]


def _resid_ln_mod(x, y, gate, ln_g, ln_b, sc, sh, name):
    def fn(x, y, gate, ln_g, ln_b, sc, sh):
        x1 = _layer_norm_rows(ALPHA * x + (1.0 + gate) * y, ln_g, ln_b)
        return x1, x1 * (1.0 + sc) + sh

    t = x.shape[0]
    return _fused(fn, [_tiled(x, ROW_TILE), _tiled(y, ROW_TILE)] + [_shared(a) for a in (gate, ln_g, ln_b, sc, sh)],
                  [_tiled_out(t, D_MODEL, ROW_TILE)] * 2, (t // ROW_TILE,), name)


def _resid_ln_loss(x, y, gate, ln_g, ln_b, target, name):
    def fn(x, y, gate, ln_g, ln_b, target):
        x1 = _layer_norm_rows(ALPHA * x + (1.0 + gate) * y, ln_g, ln_b)
        err = jnp.square(x1 - target)
        per_row = jnp.mean(err, axis=-1, keepdims=True)
        return (0.5 * jnp.sum(per_row, axis=0, keepdims=True),)

    t = x.shape[0]
    return _fused(fn, [_tiled(x, ROW_TILE), _tiled(y, ROW_TILE)] + [_shared(a) for a in (gate, ln_g, ln_b)]
                  + [_tiled(target, ROW_TILE)],
                  [((1, 1), (1, 1), _const_map(2), "a")], (t // ROW_TILE,), name)[0]


def _mlp(u, w1, s1, w2, s2, name):
    def run(u, w1, w2):
        h, act = _mm(u, w1, "nn", name + "_w1_fwd", square_relu=True)
        return _mm(act, w2, "nn", name + "_w2_fwd"), (u, w1, w2, h, act)

    @jax.custom_vjp
    def op(u, w1, s1, w2, s2):
        return run(u, w1, w2)[0]

    def fwd(u, w1, s1, w2, s2):
        return run(u, w1, w2)

    def bwd(res, dy):
        u, w1, w2, h, act = res
        dh = _mm(dy, w2, "nt", name + "_w2_dx", pre_act=h)
        dw2 = _mm(act, dy, "tn", name + "_w2_dw", narrow_out=True)
        du = _mm(dh, w1, "nt", name + "_w1_dx")
        dw1 = _mm(u, dh, "tn", name + "_w1_dw", narrow_out=True)
        return du, jnp.zeros_like(w1), dw1, jnp.zeros_like(w2), dw2

    op.defvjp(fwd, bwd)
    return op(u, w1, s1, w2, s2)


AB_PIECES = (("r", 0, 512, 512), ("k", 512, 512, 512), ("v", 1024, 512, 512),
             ("wd", 1536, 64, 128), ("ad", 1600, 64, 128), ("gd", 1664, 160, 256),
             ("h", 1824, 512, 512), ("bg", 2336, 512, 512), ("cg", 2848, 512, 512))
AB_PAD_COLS = sum(p[3] for p in AB_PIECES)


def _regroup_cols(w):
    parts = []
    for _, start, width, padded in AB_PIECES:
        piece = w[..., start:start + width]
        if padded != width:
            piece = jnp.pad(piece, [(0, 0)] * (w.ndim - 1) + [(0, padded - width)])
        parts.append(piece)
    return jnp.concatenate(parts, axis=-1)


def _pad_rows(w, rows):
    return jnp.pad(w, ((0, rows - w.shape[0]), (0, 0)))


def _rwkv_shortconv(u, big, wts, wsrc, gslot, late_shapes):
    t = u.shape[0]
    p = _linear(u, _regroup_cols(big["ab_w_in"][0]), _regroup_cols(wts["ab_w_in"][0]), "ab_in")
    mu = _regroup_cols(jnp.pad(wts["rw_mu"], ((0, 0), (0, AB_PROJ - RW_PROJ))))
    w_up = _pad_rows(wts["rw_w_up"][0], 128)
    a_up = _pad_rows(wts["rw_a_up"][0], 128)
    g_up = _pad_rows(wts["rw_g_up"][0], 256)

    def pre(rp, rs, kp, ks, vp, vs, wdp, wds, adp, ads, gdp, gds, h, cg,
            mu_r, mu_k, mu_v, mu_w, mu_a, mu_g, w0, w_up, a0, a_up, g_up, k_k, k_a):
        def mix(pv, sv, m):
            return pv + m * (sv - pv)

        r, k, v = mix(rp, rs, mu_r), mix(kp, ks, mu_k), mix(vp, vs, mu_v)
        wd, ad, gd = mix(wdp, wds, mu_w), mix(adp, ads, mu_a), mix(gdp, gds, mu_g)
        logw = -_softplus(-(w0 + _bdot(jnp.tanh(wd), w_up))) - 0.5
        decay = jnp.exp(-jnp.exp(logw))
        iclr = jax.nn.sigmoid(a0 + _bdot(ad, a_up))
        gate = _bdot(jax.nn.sigmoid(gd), g_up)
        kk = k * k_k
        kk = kk / jnp.maximum(jnp.sqrt(_head_sum(kk * kk)), 1e-12)
        k_h = k * (1.0 + (iclr - 1.0) * k_a)
        return r, decay, k_h, v, -kk, kk * iclr, gate, cg * h

    tile = ROW_TILE
    names = [q[0] for q in AB_PIECES]
    cuts = list(np.cumsum([q[3] for q in AB_PIECES])[:-1])
    pp = dict(zip(names, jnp.split(p, cuts, axis=1)))
    mp = dict(zip(names, jnp.split(mu, cuts, axis=1)))

    ins = []
    for name in ("r", "k", "v", "wd", "ad", "gd"):
        ins += [_tiled(pp[name], tile), _tiled(_shift_rows(pp[name]), tile)]
    ins += [_tiled(pp["h"], tile), _tiled(pp["cg"], tile)]
    ins += [_shared(mp[name]) for name in ("r", "k", "v", "wd", "ad", "gd")]
    ins += [_shared(a) for a in (wts["rw_w0"], w_up, wts["rw_a0"], a_up, g_up, wts["rw_k_k"], wts["rw_k_a"])]
    outs = [_tiled_out(t, RW_WIDTH, tile)] * 8
    r, decay, k_h, v, a, b, gate, z = _fused(pre, ins, outs, (t // tile,), "rwkv_pre")

    y, gathered, token = _rwkv_scan(r, decay, k_h, v, a, b, wsrc, gslot)
    shard_shapes = [s[:SHARDED[n]] + (s[SHARDED[n]] // N_CHIPS,) + s[SHARDED[n] + 1:] for n, s in late_shapes.items()]
    late_big = {n: _join_chips(part, SHARDED[n])
                for n, part in zip(late_shapes, _unpack_chips(gathered, shard_shapes, own=wsrc))}
    late_slots = _gradient_slots(token, late_shapes, tuple(late_shapes))

    conv_w = wts["sc_conv_w"][0]
    r_k = wts["rw_r_k"].reshape(1, RW_WIDTH)

    def post(y, r, k_h, v, gate, bg, z, z1, z2, lnx_g, lnx_b, r_k, c0, c1, c2):
        mean = _head_sum(y) * (1.0 / HEAD_DIM)
        yc = y - mean
        var = _head_sum(yc * yc) * (1.0 / HEAD_DIM)
        yn = yc * lax.rsqrt(var + RW_GN_EPS) * lnx_g + lnx_b
        bonus = _head_sum(r * k_h * r_k) * v
        return (yn + bonus) * gate, bg * (c0 * z2 + c1 * z1 + c2 * z)

    ins = [_tiled(a_, tile) for a_ in (y, r, k_h, v, gate, pp["bg"])]
    ins += [_tiled(a_, tile) for a_ in (z, _shift_rows(z, 1), _shift_rows(z, 2))]
    ins += [_shared(a_) for a_ in (wts["rw_lnx_g"], wts["rw_lnx_b"], r_k, conv_w[0:1], conv_w[1:2], conv_w[2:3])]
    y_a, y_b = _fused(post, ins, [_tiled_out(t, RW_WIDTH, tile)] * 2, (t // tile,), "rwkv_post")
    out = _linear(jnp.concatenate([y_a, y_b], axis=1), late_big["ab_w_out"][0], late_slots["ab_w_out"][0], "ab_out")
    return out, late_big, late_slots


def _t5_bucket_np(dist):
    exact = N_BUCKETS // 2
    logd = np.log(np.maximum(dist, 1).astype(np.float32) / exact) / math.log(MAX_DISTANCE / exact)
    large = np.minimum(exact + (logd * (N_BUCKETS - exact)).astype(np.int32), N_BUCKETS - 1)
    return np.where(dist < exact, dist, large)


def _merge_groups(os_, lses, name):
    t = os_[0].shape[0]
    tile = ROW_TILE

    def fn(o0, o1, o2, l0, l1, l2):
        lane = lax.broadcasted_iota(jnp.int32, (1, LANES), 1)
        lo = lane < HEAD_DIM
        ls = [jnp.where(lo, l[0], l[1]) for l in (l0, l1, l2)]
        m = jnp.maximum(jnp.maximum(ls[0], ls[1]), ls[2])
        es = [jnp.exp(l - m) for l in ls]
        den = es[0] + es[1] + es[2]
        return ((es[0] * o0 + es[1] * o1 + es[2] * o2) / den,)

    ins = [(o, (tile, LANES), lambda i, hp: (i, hp), "t") for o in os_]
    ins += [(l, (2, tile, LANES), lambda i, hp: (hp, i, 0), "t") for l in lses]
    outs = [((t, DIL_WIDTH), (tile, LANES), lambda i, hp: (i, hp), "t")]
    return _fused(fn, ins, outs, (t // tile, N_PAIRS), name)[0]


def _residue_major(a, dil, axis=0):
    if dil == 1:
        return a
    shp = a.shape
    split = a.reshape(shp[:axis] + (shp[axis] // dil, dil) + shp[axis + 1:])
    return jnp.swapaxes(split, axis, axis + 1).reshape(shp)


def _position_major(a, dil, axis=0):
    if dil == 1:
        return a
    shp = a.shape
    split = a.reshape(shp[:axis] + (dil, shp[axis] // dil) + shp[axis + 1:])
    return jnp.swapaxes(split, axis, axis + 1).reshape(shp)


def _dilated_mixer(u, big, wts):
    group_cols = 3 * DIL_WIDTH
    ps = []
    for g, (_, dil) in enumerate(DIL_PATTERNS):
        cols = slice(g * group_cols, (g + 1) * group_cols)
        ps.append(_linear(_residue_major(u, dil), big["dil_w_qkv"][0][:, cols], wts["dil_w_qkv"][g][0],
                          f"dil_qkv{g}"))
    qi = np.arange(BLOCK)[:, None]
    ki = np.arange(2 * BLOCK)[None, :]
    rel = BLOCK + qi - ki
    biases = []
    for g, (window, dil) in enumerate(DIL_PATTERNS):
        span = window // dil
        bucket = _t5_bucket_np(np.clip(rel, 0, span) * dil).reshape(-1)
        onehot = jnp.asarray(np.eye(N_BUCKETS, dtype=np.float32)[bucket])
        table = wts["rel_bias"][:, g * N_HEADS:(g + 1) * N_HEADS]
        bias = jnp.dot(onehot, table, precision=lax.Precision.HIGHEST)
        biases.append(jnp.transpose(bias.reshape(BLOCK, 2 * BLOCK, N_HEADS), (2, 0, 1)))
    os_, lses = _dilated_attention(tuple(ps), tuple(biases))
    os_ = [_position_major(o, dil) for o, (_, dil) in zip(os_, DIL_PATTERNS)]
    lses = [_position_major(l, dil, axis=1) for l, (_, dil) in zip(lses, DIL_PATTERNS)]
    o = _merge_groups(os_, lses, "dil_merge")
    return _linear(o, big["dil_w_out"][0], wts["dil_w_out"][0], "dil_out")


def _forward_local(x, mods, big, wts, wsrc, gslot, late_shapes, target):
    u = _modulate(x, mods[0, 1], mods[0, 0], "mod_in")
    for i in range(DEPTH):
        sh2, sc2, g1, g2 = mods[i, 3], mods[i, 4], mods[i, 2], mods[i, 5]
        if i == 0:
            y, late_big, late_slots = _rwkv_shortconv(u, big, wts, wsrc, gslot, late_shapes)
            big = {**big, **late_big}
            wts = {**wts, **late_slots}
        else:
            y = _dilated_mixer(u, big, wts)
        x, u = _resid_ln_mod(x, y, g1, wts["ln_g"][i, 0:1], wts["ln_b"][i, 0:1], sc2, sh2, f"ln_mix{i}")
        y = _mlp(u, big["mlp_w1"][i], wts["mlp_w1"][i], big["mlp_w2"][i], wts["mlp_w2"][i], f"mlp{i}")
        if i + 1 < DEPTH:
            x, u = _resid_ln_mod(x, y, g2, wts["ln_g"][i, 1:2], wts["ln_b"][i, 1:2],
                                 mods[i + 1, 1], mods[i + 1, 0], f"ln_mlp{i}")
        else:
            return _resid_ln_loss(x, y, g2, wts["ln_g"][i, 1:2], wts["ln_b"][i, 1:2], target, "ln_loss")


SHARDED = {"ab_w_in": 2, "ab_w_out": 1, "dil_w_qkv": 2, "dil_w_out": 2, "mlp_w1": 2, "mlp_w2": 1,
           "ln_g": 2, "ln_b": 2, "rw_w_up": 2, "rw_a_up": 2, "rw_g_up": 2, "sc_conv_w": 2}
FIRST_MIXER = ("ab_w_in",)
LATER_LAYERS = ("ab_w_out", "dil_w_qkv", "dil_w_out", "mlp_w1", "mlp_w2")
SMALL_SHARDED = ("ln_g", "ln_b", "rw_w_up", "rw_a_up", "rw_g_up", "sc_conv_w")
REPLICATED = ("ada_b", "rw_mu", "rw_w0", "rw_a0", "rw_k_k", "rw_k_a", "rw_r_k", "rw_lnx_g", "rw_lnx_b", "rel_bias")
WEIGHT_ORDER = ("ada_w", "ada_b", "ln_g", "ln_b", "ab_w_in", "rw_mu", "rw_w0", "rw_w_up", "rw_a0", "rw_a_up",
                "rw_g_up", "rw_k_k", "rw_k_a", "rw_r_k", "rw_lnx_g", "rw_lnx_b", "sc_conv_w", "ab_w_out",
                "dil_w_qkv", "dil_w_out", "rel_bias", "mlp_w1", "mlp_w2")


PACK_ROWS = 16
EXCHANGE_ROWS = 2 * EXCHANGE_CHUNKS * PACK_ROWS


def _rows_of(n_elems):
    return -(-n_elems // (ROW_W * PACK_ROWS)) * PACK_ROWS


def _to_rows(a):
    flat = a.reshape(-1)
    rows = _rows_of(flat.shape[0])
    if rows * ROW_W != flat.shape[0]:
        flat = jnp.pad(flat, (0, rows * ROW_W - flat.shape[0]))
    return flat.reshape(rows, ROW_W)


def _from_rows(rows, shape):
    n = int(np.prod(shape))
    return rows.reshape(-1)[:n].reshape(shape)


def _split_chips(full, axis):
    shp = full.shape
    parts = full.reshape(shp[:axis] + (N_CHIPS, shp[axis] // N_CHIPS) + shp[axis + 1:])
    return jnp.moveaxis(parts, axis, 0)


def _join_chips(parts, axis):
    moved = jnp.moveaxis(parts, 0, axis)
    shp = moved.shape
    return moved.reshape(shp[:axis] + (shp[axis] * shp[axis + 1],) + shp[axis + 2:])


def _pack_rows(arrays, row_multiple=None):
    row_multiple = EXCHANGE_ROWS if row_multiple is None else row_multiple
    blocks = [_to_rows(a) for a in arrays]
    total = sum(b.shape[0] for b in blocks)
    pad = (-total) % row_multiple
    if pad:
        blocks.append(jnp.zeros((pad, ROW_W), blocks[0].dtype))
    return jnp.concatenate(blocks, axis=0)


def _unpack_rows(buf, shapes):
    out, r0 = [], 0
    for shp in shapes:
        n = _rows_of(int(np.prod(shp)))
        out.append(_from_rows(buf[r0:r0 + n], shp))
        r0 += n
    return out


def _pack_chips(parts):
    blocks = []
    for p in parts:
        flat = p.reshape(N_CHIPS, -1)
        rows = _rows_of(flat.shape[1])
        if rows * ROW_W != flat.shape[1]:
            flat = jnp.pad(flat, ((0, 0), (0, rows * ROW_W - flat.shape[1])))
        blocks.append(flat.reshape(N_CHIPS, rows, ROW_W))
    total = sum(b.shape[1] for b in blocks)
    pad = (-total) % EXCHANGE_ROWS
    if pad:
        blocks.append(jnp.zeros((N_CHIPS, pad, ROW_W), blocks[0].dtype))
    return jnp.concatenate(blocks, axis=1)


def _unpack_chips(buf, shapes, own=None):
    mine = lax.broadcasted_iota(jnp.int32, (N_CHIPS, 1, 1), 0) == _chip_of(_me()) if own is not None else None
    out, r0 = [], 0
    for shp in shapes:
        size = int(np.prod(shp))
        n = _rows_of(size)
        rows = buf[:, r0:r0 + n]
        if own is not None:
            rows = jnp.where(mine, own[None, r0:r0 + n], rows)
        out.append(rows.reshape(N_CHIPS, -1)[:, :size].reshape((N_CHIPS,) + tuple(shp)))
        r0 += n
    return out


def _gradient_slots(token, full_shapes, names):
    def slots():
        out = {n: jnp.zeros(full_shapes[n], BF16) for n in names}
        shp = full_shapes["dil_w_qkv"]
        out["dil_w_qkv"] = tuple(jnp.zeros(shp[:-1] + (shp[-1] // N_GROUPS,), BF16) for _ in range(N_GROUPS))
        return out

    @jax.custom_vjp
    def route(token):
        return slots()

    def fwd(token):
        return slots(), None

    def bwd(_, d):
        d = {**d, "dil_w_qkv": jnp.concatenate(d["dil_w_qkv"], axis=-1)}
        return (_pack_chips([_split_chips(d[n], SHARDED[n]).astype(BF16) for n in names]),)

    route.defvjp(fwd, bwd)
    return route(token)


def _as2d(a):
    return a.reshape(-1, a.shape[-1])


def kernel(x, c, ada_w, ada_b, ln_g, ln_b, ab_w_in, rw_mu, rw_w0, rw_w_up, rw_a0, rw_a_up, rw_g_up, rw_k_k, rw_k_a, rw_r_k, rw_lnx_g, rw_lnx_b, sc_conv_w, ab_w_out, dil_w_qkv, dil_w_out, rel_bias, mlp_w1, mlp_w2, loss_target, m_ada_w, m_ada_b, m_ln_g, m_ln_b, m_ab_w_in, m_rw_mu, m_rw_w0, m_rw_w_up, m_rw_a0, m_rw_a_up, m_rw_g_up, m_rw_k_k, m_rw_k_a, m_rw_r_k, m_rw_lnx_g, m_rw_lnx_b, m_sc_conv_w, m_ab_w_out, m_dil_w_qkv, m_dil_w_out, m_rel_bias, m_mlp_w1, m_mlp_w2, v_ada_w, v_ada_b, v_ln_g, v_ln_b, v_ab_w_in, v_rw_mu, v_rw_w0, v_rw_w_up, v_rw_a0, v_rw_a_up, v_rw_g_up, v_rw_k_k, v_rw_k_a, v_rw_r_k, v_rw_lnx_g, v_rw_lnx_b, v_sc_conv_w, v_ab_w_out, v_dil_w_qkv, v_dil_w_out, v_rel_bias, v_mlp_w1, v_mlp_w2):
    args = dict(locals())
    w_in = {n: args[n] for n in WEIGHT_ORDER}
    m_in = {n: args["m_" + n] for n in WEIGHT_ORDER}
    v_in = {n: args["v_" + n] for n in WEIGHT_ORDER}
    me = _me()
    chip = _chip_of(me)
    dev = _dev_of(me)

    c_all = _dev_all_gather(c, "gather_c")[:, 0, :]
    n_col = ada_w.shape[2]
    ada_b_cols = lax.dynamic_slice_in_dim(ada_b, chip * n_col, n_col, axis=1)[:, None, :]
    mod_cols = _ada_fwd(c_all, ada_w, ada_b_cols)

    first_buf = _pack_rows([w_in[n].astype(BF16) for n in FIRST_MIXER])
    first_all = _two_level_gather(first_buf, "gather_first")
    late_buf = _pack_rows([w_in[n].astype(BF16) for n in LATER_LAYERS])
    small_buf = _pack_rows([mod_cols] + [w_in[n] for n in SMALL_SHARDED], row_multiple=PACK_ROWS)
    small_all = _chip_all_gather(small_buf, "gather_small")

    def full_shape(n):
        shp = w_in[n].shape
        return shp[:SHARDED[n]] + (shp[SHARDED[n]] * N_CHIPS,) + shp[SHARDED[n] + 1:]

    wts = {n: w_in[n] for n in REPLICATED}
    big = {}
    for n, part in zip(FIRST_MIXER, _unpack_chips(first_all, [w_in[n].shape for n in FIRST_MIXER])):
        big[n] = _join_chips(part, SHARDED[n])
        wts[n] = jnp.zeros(full_shape(n), BF16)
    small_parts = _unpack_chips(small_all, [mod_cols.shape] + [w_in[n].shape for n in SMALL_SHARDED])
    for n, part in zip(SMALL_SHARDED, small_parts[1:]):
        wts[n] = _join_chips(part, SHARDED[n])
    mod_all = _join_chips(small_parts[0], 2)
    mods = lax.dynamic_slice_in_dim(mod_all, dev, 1, axis=1).reshape(DEPTH, 6, 1, D_MODEL)
    late_shapes = {n: full_shape(n) for n in LATER_LAYERS}
    late_slot = jnp.zeros(late_buf.shape, F32)

    def local_loss(xv, modv, wv, slot):
        return _forward_local(xv, modv, big, wv, late_buf, slot, late_shapes, loss_target[0])[0, 0]

    loss_local, (grad_x, dmods, dw, late_part) = jax.value_and_grad(local_loss, argnums=(0, 1, 2, 3))(
        x[0], mods, wts, late_slot)

    small_row = jnp.concatenate([dmods.reshape(-1)] + [dw[n].reshape(-1) for n in REPLICATED[1:]]
                                + [loss_local.reshape(1)])
    n_small = small_row.shape[0]
    n_small_pad = -(-n_small // LANES) * LANES
    small_row = jnp.pad(small_row, (0, n_small_pad - n_small))[None, :]
    rows_all = _dev_all_gather(small_row, "gather_small_grads")
    small_sum = _sum_slots(rows_all, "sum_small_grads")
    loss = small_sum[0, n_small - 1]

    dmod_all = rows_all[:, 0, :DEPTH * 6 * D_MODEL].reshape(N_DEV, DEPTH, 6 * D_MODEL)
    dmod_cols = lax.dynamic_slice_in_dim(dmod_all, chip * n_col, n_col, axis=2)
    dmod_cols = jnp.pad(jnp.moveaxis(dmod_cols, 0, 1), ((0, 0), (0, LANES - N_DEV), (0, 0)))
    c_all_t = jnp.pad(c_all.T, ((0, 0), (0, LANES - N_DEV)))
    grads = {"ada_w": _ada_grad(c_all_t, dmod_cols)}
    grads["ada_b"] = small_sum[0, :DEPTH * 6 * D_MODEL].reshape(ada_b.shape)
    r0 = DEPTH * 6 * D_MODEL
    for n in REPLICATED[1:]:
        size = int(np.prod(w_in[n].shape))
        grads[n] = small_sum[0, r0:r0 + size].reshape(w_in[n].shape)
        r0 += size

    sharded_names = FIRST_MIXER + SMALL_SHARDED
    send = _pack_chips([_split_chips(dw[n], SHARDED[n]).astype(BF16) for n in sharded_names])
    n_rows = send.shape[1]
    send = send.reshape(N_CHIPS, 2, n_rows // 2, ROW_W)
    theirs = _core_halves(send, "swap_halves")
    mine = lax.dynamic_index_in_dim(send, me[2], 1, keepdims=False)
    chip_part = _add_pairs(mine, theirs, "sum_cores")
    recv = _chip_scatter(chip_part, "scatter_grads")
    half_sum = _sum_slots(recv, "sum_chips")
    g_rows = _core_all_gather(half_sum, "gather_halves").reshape(n_rows, ROW_W)
    for n, g in zip(sharded_names, _unpack_rows(g_rows, [w_in[n].shape for n in sharded_names])):
        grads[n] = g

    late_rows = _sum_slots(_core_all_gather(late_part, "swap_late"), "sum_cores_late")
    for n, g in zip(LATER_LAYERS, _unpack_rows(late_rows, [w_in[n].shape for n in LATER_LAYERS])):
        grads[n] = g

    deltas, new_m, new_v = {}, {}, {}
    for n in WEIGHT_ORDER:
        shp = w_in[n].shape
        d, nm, nv = _adamw(_as2d(w_in[n]), _as2d(grads[n]), _as2d(m_in[n]), _as2d(v_in[n]), "adamw_" + n)
        deltas[n], new_m[n], new_v[n] = d.reshape(shp), nm.reshape(shp), nv.reshape(shp)

    return (loss, grad_x[None], *[grads[n] for n in WEIGHT_ORDER], *[deltas[n] for n in WEIGHT_ORDER],
            *[new_m[n] for n in WEIGHT_ORDER], *[new_v[n] for n in WEIGHT_ORDER])
```
